```python
import math
import jax, jax.numpy as jnp
from jax import lax
import numpy as np

D_MODEL = 1024
BATCH = 32
SEQ = 256
DEPTH = 2
DEC_BATCH = 8
DEC_SEQ = 1024
PAST_LEN = 512

GRID_W = 64
HEAD_DIM = 64
N_BRANCH = 4
BRANCH_W = D_MODEL // N_BRANCH
ATT_HEADS = BRANCH_W // HEAD_DIM
ATT_KV_HEADS = ATT_HEADS // 2
GDN_HEADS = BRANCH_W // HEAD_DIM
RET_HEADS = BRANCH_W // HEAD_DIM
NA_HEADS = BRANCH_W // HEAD_DIM
SHORT_CONV = 5
CHUNK = 64
Q_BLOCK = 128
NA_ROWS = 8
NA_COLS = 16
NA_KCOLS = 2 * NA_COLS
ROPE_THETA = 10000.0
RET_DECAY_BASE_FWD = 5.0
RET_DECAY_BASE_BWD = 5.5
EPS = 1e-6
IN_SPLITS = (
    ATT_HEADS * HEAD_DIM, ATT_KV_HEADS * HEAD_DIM, ATT_KV_HEADS * HEAD_DIM, BRANCH_W,
    3 * GDN_HEADS * HEAD_DIM, 4 * GDN_HEADS, BRANCH_W,
    RET_HEADS * HEAD_DIM, RET_HEADS * HEAD_DIM, RET_HEADS * HEAD_DIM, BRANCH_W,
    NA_HEADS * HEAD_DIM, NA_HEADS * HEAD_DIM, NA_HEADS * HEAD_DIM, BRANCH_W,
    N_BRANCH * D_MODEL,
)
IN_W = sum(IN_SPLITS)

kernel_name = "hybrid_diffusion_parallel_mixer_step"


def _rmsnorm(x, g):
    xf = x.astype(jnp.float32)
    y = xf * lax.rsqrt(jnp.mean(xf * xf, axis=-1, keepdims=True) + EPS)
    return (y * g.astype(jnp.float32)).astype(x.dtype)


def _l2norm(x):
    return x * lax.rsqrt(jnp.sum(x * x, axis=-1, keepdims=True) + EPS)


def _heads(x, h):
    return x.reshape(x.shape[0], x.shape[1], h, HEAD_DIM)


def _split_proj(p):
    offs = np.cumsum(IN_SPLITS)[:-1].tolist()
    return jnp.split(p, offs, axis=-1)


def _axial_rope(x):
    s = x.shape[1]
    t = jnp.arange(s)
    row = (t // GRID_W).astype(jnp.float32)
    col = (t % GRID_W).astype(jnp.float32)
    half = HEAD_DIM // 2
    quarter = half // 2
    inv = ROPE_THETA ** (-jnp.arange(quarter, dtype=jnp.float32) / quarter)

    def rot(xp, pos):
        ang = pos[:, None] * inv
        cos = jnp.cos(ang)[None, :, None, :]
        sin = jnp.sin(ang)[None, :, None, :]
        x1, x2 = xp[..., :quarter], xp[..., quarter:]
        return jnp.concatenate([x1 * cos - x2 * sin, x2 * cos + x1 * sin], axis=-1)

    xf = x.astype(jnp.float32)
    out = jnp.concatenate([rot(xf[..., :half], row), rot(xf[..., half:], col)], axis=-1)
    return out.astype(x.dtype)


def _block_attention(q, k, v):
    b, s, hq, d = q.shape
    hkv = k.shape[2]
    rep = hq // hkv
    nblk = s // Q_BLOCK
    qb = q.reshape(b, nblk, Q_BLOCK, hkv, rep, d).transpose(1, 0, 2, 3, 4, 5)
    scale = d ** -0.5

    def one(qi):
        sc = jnp.einsum('bqgrd,bkgd->bgrqk', qi, k, preferred_element_type=jnp.float32) * scale
        p = jax.nn.softmax(sc, axis=-1).astype(v.dtype)
        return jnp.einsum('bgrqk,bkgd->bqgrd', p, v)

    o = lax.map(one, qb)
    return o.transpose(1, 0, 2, 3, 4, 5).reshape(b, s, hq * d)


def _neighbourhood_attention(q, k, v, ctx_k, ctx_v, bias_table):
    b, n, h, d = q.shape
    rows = n // GRID_W
    wr = min(NA_ROWS, rows)
    ncb = GRID_W // NA_COLS
    r = jnp.arange(rows)
    rs = jnp.clip(r - wr // 2, 0, rows - wr)
    row_idx = rs[:, None] + jnp.arange(wr)[None, :]
    j = jnp.arange(ncb)
    bs = jnp.clip(j * NA_COLS - NA_COLS // 2, 0, GRID_W - NA_KCOLS)
    col_idx = bs[:, None] + jnp.arange(NA_KCOLS)[None, :]
    kg = k.reshape(b, rows, GRID_W, h, d)[:, row_idx][:, :, :, col_idx]
    vg = v.reshape(b, rows, GRID_W, h, d)[:, row_idx][:, :, :, col_idx]
    qg = q.reshape(b, rows, ncb, NA_COLS, h, d)
    scale = d ** -0.5
    s_loc = jnp.einsum('brjqhd,brwjkhd->brjhqwk', qg, kg, preferred_element_type=jnp.float32) * scale
    qc = j[:, None] * NA_COLS + jnp.arange(NA_COLS)[None, :]
    cs = jnp.clip(qc - NA_COLS // 2, 0, GRID_W - NA_COLS)
    kc = col_idx[:, None, :]
    col_ok = (kc >= cs[:, :, None]) & (kc < cs[:, :, None] + NA_COLS)
    dc_i = jnp.clip(kc - qc[:, :, None] + NA_COLS - 1, 0, 2 * NA_COLS - 2)
    dr_i = row_idx - r[:, None] + NA_ROWS - 1
    bias = bias_table[:, dr_i[:, None, None, :, None], dc_i[None, :, :, None, :]].astype(jnp.float32)
    bias = bias.transpose(1, 2, 0, 3, 4, 5)
    s_loc = jnp.where(col_ok[None, None, :, None, :, None, :], s_loc + bias[None], -jnp.inf)
    s_loc = s_loc.reshape(b, rows, ncb, h, NA_COLS, wr * NA_KCOLS)
    s_ctx = jnp.einsum('brjqhd,blhd->brjhql', qg, ctx_k.astype(q.dtype), preferred_element_type=jnp.float32) * scale
    p = jax.nn.softmax(jnp.concatenate([s_loc, s_ctx], axis=-1), axis=-1).astype(v.dtype)
    p_loc = p[..., :wr * NA_KCOLS].reshape(b, rows, ncb, h, NA_COLS, wr, NA_KCOLS)
    p_ctx = p[..., wr * NA_KCOLS:]
    o = (jnp.einsum('brjhqwk,brwjkhd->brjqhd', p_loc, vg)
         + jnp.einsum('brjhql,blhd->brjqhd', p_ctx, ctx_v.astype(v.dtype)))
    return o.reshape(b, n, h * d)


def _short_conv(x, w):
    ch = x.shape[-1]
    pad = SHORT_CONV // 2
    y = lax.conv_general_dilated(x, w[:, None, :].astype(x.dtype), window_strides=(1,),
                                 padding=((pad, pad),), dimension_numbers=('NWC', 'WIO', 'NWC'),
                                 feature_group_count=ch)
    return jax.nn.silu(y)


def _to_chunks(x):
    b, s = x.shape[:2]
    x = x.reshape((b, s // CHUNK, CHUNK) + x.shape[2:])
    return x.transpose((1, 0, 3, 2) + tuple(range(4, x.ndim)))


def _from_chunks(o, b, s):
    return o.transpose(1, 0, 3, 2, 4).reshape(b, s, o.shape[2], o.shape[4])


def _gated_delta_chunked(q, k, v, beta, log_a, s0):
    b, s = q.shape[:2]
    qc, kc, vc = _to_chunks(q), _to_chunks(k), _to_chunks(v)
    bc = _to_chunks(beta)
    gc = jnp.cumsum(_to_chunks(log_a), axis=-1)
    idx = jnp.arange(CHUNK)
    causal = idx[:, None] >= idx[None, :]
    strict = idx[:, None] > idx[None, :]
    gdiff = gc[..., :, None] - gc[..., None, :]
    decay_incl = jnp.exp(jnp.where(causal, gdiff, -jnp.inf))
    decay_strict = jnp.where(strict, decay_incl, 0.0)
    kb = kc * bc[..., None]
    a_mat = jnp.einsum('nbhid,nbhjd->nbhij', kb, kc) * decay_strict
    u = lax.linalg.triangular_solve(a_mat, vc * bc[..., None], left_side=True, lower=True, unit_diagonal=True)
    w = lax.linalg.triangular_solve(a_mat, kb * jnp.exp(gc)[..., None], left_side=True, lower=True, unit_diagonal=True)
    qk = jnp.einsum('nbhid,nbhjd->nbhij', qc, kc) * decay_incl
    q_dec = qc * jnp.exp(gc)[..., None]
    g_last = gc[..., -1]
    k_dec = kc * jnp.exp(g_last[..., None] - gc)[..., None]

    def step(state, xs):
        u_i, w_i, qk_i, qd_i, kd_i, gl_i = xs
        v_new = u_i - jnp.einsum('bhck,bhkv->bhcv', w_i, state)
        o = jnp.einsum('bhck,bhkv->bhcv', qd_i, state) + jnp.einsum('bhij,bhjv->bhiv', qk_i, v_new)
        state = state * jnp.exp(gl_i)[..., None, None] + jnp.einsum('bhck,bhcv->bhkv', kd_i, v_new)
        return state, o

    s_fin, o = lax.scan(step, s0, (u, w, qk, q_dec, k_dec, g_last))
    return _from_chunks(o, b, s), s_fin


def _retention_chunked(q, k, v, log_gamma, s0):
    b, s = q.shape[:2]
    qc, kc, vc = _to_chunks(q), _to_chunks(k), _to_chunks(v)
    idx = jnp.arange(CHUNK, dtype=jnp.float32)
    diff = idx[:, None] - idx[None, :]
    decay = jnp.where(diff >= 0, jnp.exp(jnp.maximum(diff, 0.0)[None] * log_gamma[:, None, None]), 0.0)
    q_dec = jnp.exp((idx + 1.0)[None, :] * log_gamma[:, None])
    k_dec = jnp.exp((CHUNK - 1.0 - idx)[None, :] * log_gamma[:, None])
    c_dec = jnp.exp(CHUNK * log_gamma)
    intra = jnp.einsum('nbhij,nbhjv->nbhiv', jnp.einsum('nbhid,nbhjd->nbhij', qc, kc) * decay, vc)
    qd = qc * q_dec[:, :, None]
    kd = kc * k_dec[:, :, None]

    def step(state, xs):
        intra_i, qd_i, kd_i, v_i = xs
        o = intra_i + jnp.einsum('bhck,bhkv->bhcv', qd_i, state)
        state = state * c_dec[:, None, None] + jnp.einsum('bhck,bhcv->bhkv', kd_i, v_i)
        return state, o

    s_fin, o = lax.scan(step, s0, (intra, qd, kd, vc))
    return _from_chunks(o, b, s), s_fin


def _flip(x):
    return jnp.flip(x, axis=1)


def _gdn_branch(qkv_raw, ab_raw, z, conv_w, a_log, dt_bias, norm_g, s0):
    f32 = jnp.float32
    b, s, _ = qkv_raw.shape
    qkv = _short_conv(qkv_raw, conv_w).astype(f32)
    q, k, v = jnp.split(qkv, 3, axis=-1)
    q = _l2norm(_heads(q, GDN_HEADS)) * HEAD_DIM ** -0.5
    k = _l2norm(_heads(k, GDN_HEADS))
    v = _heads(v, GDN_HEADS)
    ab = ab_raw.astype(f32).reshape(b, s, 4, GDN_HEADS)
    beta = jax.nn.sigmoid(ab[:, :, 0:2])
    log_a = -jnp.exp(a_log.astype(f32)) * jax.nn.softplus(ab[:, :, 2:4] + dt_bias.astype(f32))
    s0 = s0.astype(f32)
    o_f, st_f = _gated_delta_chunked(q, k, v, beta[:, :, 0], log_a[:, :, 0], s0[:, 0])
    o_b, st_b = _gated_delta_chunked(_flip(q), _flip(k), _flip(v), _flip(beta[:, :, 1]),
                                     _flip(log_a[:, :, 1]), s0[:, 1])
    o = _rmsnorm(o_f + _flip(o_b), norm_g).reshape(b, s, BRANCH_W).astype(z.dtype)
    return o * jax.nn.silu(z), jnp.stack([st_f, st_b], axis=1)


def _ret_log_gamma(base):
    return jnp.log1p(-jnp.exp2(-(base + jnp.arange(RET_HEADS, dtype=jnp.float32))))


def _ret_branch(q_raw, k_raw, v_raw, z, norm_g, s0):
    f32 = jnp.float32
    b, s, _ = q_raw.shape
    q = _heads(q_raw.astype(f32), RET_HEADS) * HEAD_DIM ** -0.5
    k = _heads(k_raw.astype(f32), RET_HEADS)
    v = _heads(v_raw.astype(f32), RET_HEADS)
    s0 = s0.astype(f32)
    o_f, st_f = _retention_chunked(q, k, v, _ret_log_gamma(RET_DECAY_BASE_FWD), s0[:, 0])
    o_b, st_b = _retention_chunked(_flip(q), _flip(k), _flip(v), _ret_log_gamma(RET_DECAY_BASE_BWD), s0[:, 1])
    o = _rmsnorm(o_f + _flip(o_b), norm_g).reshape(b, s, BRANCH_W).astype(z.dtype)
    return o * jax.nn.silu(z), jnp.stack([st_f, st_b], axis=1)


def _layer(h, cond, ctx, w_ada, b_ada, norm_g, w_in, conv_w, gdn_a_log, gdn_dt_bias, gdn_norm,
           attn_q_norm, attn_k_norm, ret_norm, na_bias, w_branch, w_out):
    b, s, _ = h.shape
    mod = jax.nn.silu(cond) @ w_ada + b_ada
    shift, scale, gate = jnp.split(mod, 3, axis=-1)
    hn = _rmsnorm(h, norm_g) * (1 + scale[:, None]) + shift[:, None]
    (aq, ak, av, az, gqkv, gab, gz, rq, rk, rv, rz, nq, nk, nv, nz, mg) = _split_proj(hn @ w_in)
    qa = _rmsnorm(_heads(aq, ATT_HEADS), attn_q_norm)
    ka = _rmsnorm(_heads(ak, ATT_KV_HEADS), attn_k_norm)
    va = _heads(av, ATT_KV_HEADS)
    qn, kn, vn = _heads(nq, NA_HEADS), _heads(nk, NA_HEADS), _heads(nv, NA_HEADS)
    if ctx is None:
        oa = _block_attention(qa, ka, va)
        od = _block_attention(qn, kn, vn)
        sg0 = jnp.zeros((b, 2, GDN_HEADS, HEAD_DIM, HEAD_DIM), jnp.float32)
        sr0 = jnp.zeros((b, 2, RET_HEADS, HEAD_DIM, HEAD_DIM), jnp.float32)
    else:
        ctx_ka, ctx_va, ctx_kn, ctx_vn, sg0, sr0 = ctx
        keys = jnp.concatenate([_axial_rope(ka), ctx_ka.astype(ka.dtype)], axis=1)
        vals = jnp.concatenate([va, ctx_va.astype(va.dtype)], axis=1)
        oa = _block_attention(_axial_rope(qa), keys, vals)
        od = _neighbourhood_attention(qn, kn, vn, ctx_kn, ctx_vn, na_bias)
    oa = oa * jax.nn.silu(az)
    od = od * jax.nn.silu(nz)
    ob, sg = _gdn_branch(gqkv, gab, gz, conv_w, gdn_a_log, gdn_dt_bias, gdn_norm, sg0)
    oc, sr = _ret_branch(rq, rk, rv, rz, ret_norm, sr0)
    gates = jax.nn.sigmoid(mg.astype(jnp.float32)).astype(h.dtype).reshape(b, s, N_BRANCH, D_MODEL)
    branches = jnp.stack([oa, ob, oc, od], axis=2)
    up = jnp.einsum('bsnw,nwd->bsnd', branches, w_branch)
    out = jnp.einsum('bsnd,bsnd->bsd', gates, up) @ w_out
    return h + gate[:, None] * out, (ka, va, kn, vn, sg, sr)


def setup_inputs(seed: int = 0) -> dict:
    key = jax.random.key(seed)
    ks = jax.random.split(key, 26)
    f32 = jnp.float32

    def nrm(k, shape, sd):
        return jax.random.normal(k, shape, f32) * sd

    dt = jnp.exp(jax.random.uniform(ks[16], (DEPTH, 2, GDN_HEADS), f32, math.log(1e-3), math.log(1e-1)))
    return {
        "x_prompt": nrm(ks[0], (BATCH, SEQ, D_MODEL), 1.0),
        "x_sample": nrm(ks[1], (DEC_BATCH, DEC_SEQ, D_MODEL), 1.0),
        "cache_attn_k": nrm(ks[2], (DEC_BATCH, DEPTH, PAST_LEN, ATT_KV_HEADS, HEAD_DIM), 1.0),
        "cache_attn_v": nrm(ks[3], (DEC_BATCH, DEPTH, PAST_LEN, ATT_KV_HEADS, HEAD_DIM), 1.0),
        "cache_na_k": nrm(ks[4], (DEC_BATCH, DEPTH, PAST_LEN, NA_HEADS, HEAD_DIM), 1.0),
        "cache_na_v": nrm(ks[5], (DEC_BATCH, DEPTH, PAST_LEN, NA_HEADS, HEAD_DIM), 1.0),
        "state_gdn": nrm(ks[6], (DEC_BATCH, DEPTH, 2, GDN_HEADS, HEAD_DIM, HEAD_DIM), 0.5),
        "state_ret": nrm(ks[7], (DEC_BATCH, DEPTH, 2, RET_HEADS, HEAD_DIM, HEAD_DIM), 1.0),
        "c": nrm(ks[8], (DEC_BATCH, D_MODEL), 1.0),
        "c_ctx": nrm(ks[9], (D_MODEL,), 1.0),
        "w_ada": nrm(ks[10], (DEPTH, D_MODEL, 3 * D_MODEL), 0.5 * D_MODEL ** -0.5),
        "b_ada": nrm(ks[11], (DEPTH, 3 * D_MODEL), 0.01),
        "norm_g": 1.0 + nrm(ks[12], (DEPTH, D_MODEL), 0.02),
        "w_in": nrm(ks[13], (DEPTH, D_MODEL, IN_W), D_MODEL ** -0.5),
        "conv_w": nrm(ks[14], (DEPTH, SHORT_CONV, 3 * GDN_HEADS * HEAD_DIM), SHORT_CONV ** -0.5),
        "gdn_a_log": jnp.log(jax.random.uniform(ks[15], (DEPTH, 2, GDN_HEADS), f32, 1.0, 16.0)),
        "gdn_dt_bias": dt + jnp.log(-jnp.expm1(-dt)),
        "gdn_norm": 1.0 + nrm(ks[17], (DEPTH, HEAD_DIM), 0.02),
        "attn_q_norm": 1.0 + nrm(ks[18], (DEPTH, HEAD_DIM), 0.02),
        "attn_k_norm": 1.0 + nrm(ks[19], (DEPTH, HEAD_DIM), 0.02),
        "ret_norm": 1.0 + nrm(ks[20], (DEPTH, HEAD_DIM), 0.02),
        "na_bias": nrm(ks[21], (DEPTH, NA_HEADS, 2 * NA_ROWS - 1, 2 * NA_COLS - 1), 0.1),
        "w_branch": nrm(ks[22], (DEPTH, N_BRANCH, BRANCH_W, D_MODEL), BRANCH_W ** -0.5),
        "w_out": nrm(ks[23], (DEPTH, D_MODEL, D_MODEL), D_MODEL ** -0.5),
        "final_norm": 1.0 + nrm(ks[24], (D_MODEL,), 0.02),
    }


def reference(x_prompt, x_sample, cache_attn_k, cache_attn_v, cache_na_k, cache_na_v, state_gdn, state_ret,
              c, c_ctx, w_ada, b_ada, norm_g, w_in, conv_w, gdn_a_log, gdn_dt_bias, gdn_norm,
              attn_q_norm, attn_k_norm, ret_norm, na_bias, w_branch, w_out, final_norm):
    def params(l):
        return dict(w_ada=w_ada[l], b_ada=b_ada[l], norm_g=norm_g[l], w_in=w_in[l], conv_w=conv_w[l],
                    gdn_a_log=gdn_a_log[l], gdn_dt_bias=gdn_dt_bias[l], gdn_norm=gdn_norm[l],
                    attn_q_norm=attn_q_norm[l], attn_k_norm=attn_k_norm[l], ret_norm=ret_norm[l],
                    na_bias=na_bias[l], w_branch=w_branch[l], w_out=w_out[l])

    h = x_prompt
    cond_ctx = jnp.broadcast_to(c_ctx, (x_prompt.shape[0], D_MODEL))
    ka_l, va_l, kn_l, vn_l, sg_l, sr_l = [], [], [], [], [], []
    for l in range(DEPTH):
        h, (ka, va, kn, vn, sg, sr) = _layer(h, cond_ctx, None, **params(l))
        ka_l.append(ka); va_l.append(va); kn_l.append(kn); vn_l.append(vn); sg_l.append(sg); sr_l.append(sr)
    y_prompt = _rmsnorm(h, final_norm)
    dt = x_prompt.dtype
    new_attn_k = jnp.stack(ka_l, axis=1).astype(dt)
    new_attn_v = jnp.stack(va_l, axis=1).astype(dt)
    new_na_k = jnp.stack(kn_l, axis=1).astype(dt)
    new_na_v = jnp.stack(vn_l, axis=1).astype(dt)
    new_state_gdn = jnp.stack(sg_l, axis=1).astype(dt)
    new_state_ret = jnp.stack(sr_l, axis=1).astype(dt)

    h = x_sample
    for l in range(DEPTH):
        ctx = (cache_attn_k[:, l], cache_attn_v[:, l], cache_na_k[:, l], cache_na_v[:, l],
               state_gdn[:, l], state_ret[:, l])
        h, _ = _layer(h, c, ctx, **params(l))
    y_sample = _rmsnorm(h, final_norm)
    return (y_prompt, y_sample, new_attn_k, new_attn_v, new_na_k, new_na_v, new_state_gdn, new_state_ret)
```

```python
import functools

import numpy as np
import jax
import jax.numpy as jnp
from jax import lax
from jax.experimental import pallas as pl
from jax.experimental.pallas import tpu as pltpu

F32 = jnp.float32
BF16 = jnp.bfloat16

D_MODEL = 1024
HEAD_DIM = 64
N_HEADS = 4
BRANCH_W = N_HEADS * HEAD_DIM
N_BRANCH = 4
DEPTH = 2
GRID_W = 64
CHUNK = 64
SHORT_CONV = 5
NA_ROWS = 8
NA_COLS = 16
N_DR = 2 * NA_ROWS - 1
N_DC = 2 * NA_COLS - 1
ROPE_THETA = 10000.0
RET_DECAY_BASE = (5.0, 5.5)
EPS = 1e-6
SCALE = HEAD_DIM ** -0.5
NEG_INF = float("-inf")

IN_SPLITS = (256, 128, 128, 256, 768, 16, 256, 256, 256, 256, 256, 256, 256, 256, 256, 4096)
PROJ_W = 4096
OFF_C = 0
OFF_D = 1024
OFF_GZ = 2048
OFF_GQKV = 2304
OFF_AQKV = 3072
OFF_AZ = 3584
OFF_GAB = 3840

V7X_VMEM_BYTES = 64 * 1024 * 1024
MIB = 1024 * 1024


def _params(vmem_mib, n_axes):
    assert vmem_mib * MIB < V7X_VMEM_BYTES
    return pltpu.CompilerParams(dimension_semantics=("arbitrary",) * n_axes,
                                vmem_limit_bytes=vmem_mib * MIB)


def _mm(a, b):
    return jnp.dot(a.astype(BF16), b.astype(BF16), preferred_element_type=F32)


def _mm_nt(a, b):
    return lax.dot_general(a.astype(BF16), b.astype(BF16), (((1,), (1,)), ((), ())),
                           preferred_element_type=F32)


def _mm_tn(a, b):
    return lax.dot_general(a.astype(BF16), b.astype(BF16), (((0,), (0,)), ((), ())),
                           preferred_element_type=F32)


def _split3(x):
    hi = x.astype(BF16)
    r = x - hi.astype(F32)
    mid = r.astype(BF16)
    lo = (r - mid.astype(F32)).astype(BF16)
    return hi, mid, lo


def _mm_exact(sel, x):
    hi, mid, lo = _split3(x)
    return (jnp.dot(sel, hi, preferred_element_type=F32) + jnp.dot(sel, mid, preferred_element_type=F32)
            + jnp.dot(sel, lo, preferred_element_type=F32))


def _mm_nt_exact(sel, x):
    dn = (((1,), (1,)), ((), ()))
    hi, mid, lo = _split3(x)
    return (lax.dot_general(sel, hi, dn, preferred_element_type=F32)
            + lax.dot_general(sel, mid, dn, preferred_element_type=F32)
            + lax.dot_general(sel, lo, dn, preferred_element_type=F32))


def _silu(x):
    return x * jax.nn.sigmoid(x)


def _head_mean_matrix(width):
    ri = lax.broadcasted_iota(jnp.int32, (width, width), 0) >> 6
    ci = lax.broadcasted_iota(jnp.int32, (width, width), 1) >> 6
    return jnp.where(ri == ci, 1.0 / HEAD_DIM, 0.0).astype(BF16)


def _head_rms(x):
    g = _head_mean_matrix(x.shape[1])
    x2 = x * x
    hi = x2.astype(BF16)
    lo = (x2 - hi.astype(F32)).astype(BF16)
    ms = jnp.dot(hi, g, preferred_element_type=F32) + jnp.dot(lo, g, preferred_element_type=F32)
    return x * lax.rsqrt(ms + EPS)


def _rope(x, tab_ref):
    w = x.shape[1]
    return (x * tab_ref[0] + pltpu.roll(x, w - 16, 1) * tab_ref[1] + pltpu.roll(x, 16, 1) * tab_ref[2])


def _attend(qh, parts):
    scores = []
    for k, _, bias in parts:
        s = _mm_nt(qh, k) * SCALE
        if bias is not None:
            s = s + bias
        scores.append(s)
    m = scores[0].max(axis=-1, keepdims=True)
    for s in scores[1:]:
        m = jnp.maximum(m, s.max(axis=-1, keepdims=True))
    den = None
    out = None
    for s, (_, v, _) in zip(scores, parts):
        p = jnp.exp(s - m)
        ps = p.sum(axis=-1, keepdims=True)
        po = _mm(p, v)
        den = ps if den is None else den + ps
        out = po if out is None else out + po
    return out / den


def _hs(h):
    return slice(h * HEAD_DIM, (h + 1) * HEAD_DIM)


def _mod_kernel(c_ref, w_ref, b_ref, o_ref):
    o_ref[...] = _mm(_silu(c_ref[...]), w_ref[...]) + b_ref[...]


def _mod_call(cond8, w_ada, b_ada):
    tn = 512
    return pl.pallas_call(
        _mod_kernel,
        grid=(3 * D_MODEL // tn,),
        in_specs=[pl.BlockSpec((8, D_MODEL), lambda j: (0, 0)),
                  pl.BlockSpec((D_MODEL, tn), lambda j: (0, j)),
                  pl.BlockSpec((1, tn), lambda j: (0, j))],
        out_specs=pl.BlockSpec((8, tn), lambda j: (0, j)),
        out_shape=jax.ShapeDtypeStruct((8, 3 * D_MODEL), F32),
        compiler_params=_params(24, 1),
        name="adaln_mod",
    )(cond8, w_ada, b_ada.reshape(1, 3 * D_MODEL))


def _modulated_norm(x, mod, g):
    ms = jnp.mean(x * x, axis=-1, keepdims=True)
    y = x * lax.rsqrt(ms + EPS) * g
    return y * (1.0 + mod[:, D_MODEL:2 * D_MODEL]) + mod[:, :D_MODEL]


def _inproj_kernel(x_ref, mod_ref, g_ref, w_ref, o_ref, hn_ref):
    @pl.when(pl.program_id(1) == 0)
    def _():
        hn_ref[...] = _modulated_norm(x_ref[...], mod_ref[0], g_ref[...]).astype(BF16)

    o_ref[...] = jnp.dot(hn_ref[...], w_ref[...], preferred_element_type=F32)


def _inproj_call(x2d, mod3, norm_g, wcat, rows_per_mod):
    t = x2d.shape[0]
    tm, tn = 1024, 512
    if mod3.shape[0] == 1:
        mod_idx = lambda i, j: (0, 0, 0)
    else:
        mod_idx = lambda i, j: ((i * tm) // rows_per_mod, 0, 0)
    return pl.pallas_call(
        _inproj_kernel,
        grid=(t // tm, PROJ_W // tn),
        in_specs=[pl.BlockSpec((tm, D_MODEL), lambda i, j: (i, 0)),
                  pl.BlockSpec((1, 1, 3 * D_MODEL), mod_idx),
                  pl.BlockSpec((1, D_MODEL), lambda i, j: (0, 0)),
                  pl.BlockSpec((D_MODEL, tn), lambda i, j: (0, j))],
        out_specs=pl.BlockSpec((tm, tn), lambda i, j: (i, j)),
        out_shape=jax.ShapeDtypeStruct((t, PROJ_W), F32),
        scratch_shapes=[pltpu.VMEM((tm, D_MODEL), BF16)],
        compiler_params=_params(40, 2),
        name="inproj",
    )(x2d, mod3, norm_g.reshape(1, D_MODEL), wcat)


def _attn_ctx_kernel(*refs, n_kv, norm):
    if norm:
        q_ref, k_ref, v_ref, z_ref, qn_ref, kn_ref, o_ref, kout_ref = refs
    else:
        q_ref, k_ref, v_ref, z_ref, o_ref = refs
    q, k, v, z = q_ref[...], k_ref[...], v_ref[...], z_ref[...]
    if norm:
        q = _head_rms(q) * qn_ref[...]
        k = _head_rms(k) * kn_ref[...]
        kout_ref[...] = k
    rep = N_HEADS // n_kv
    outs = [_attend(q[:, _hs(h)], [(k[:, _hs(h // rep)], v[:, _hs(h // rep)], None)]) for h in range(N_HEADS)]
    o_ref[...] = jnp.concatenate(outs, axis=-1) * _silu(z)


def _attn_ctx_call(proj, batch, seq, off_q, off_k, off_v, off_z, n_kv, qn=None, kn=None):
    t = batch * seq
    kvw = n_kv * HEAD_DIM
    norm = qn is not None
    in_specs = [pl.BlockSpec((seq, BRANCH_W), lambda b: (b, off_q // BRANCH_W)),
                pl.BlockSpec((seq, kvw), lambda b: (b, off_k // kvw)),
                pl.BlockSpec((seq, kvw), lambda b: (b, off_v // kvw)),
                pl.BlockSpec((seq, BRANCH_W), lambda b: (b, off_z // BRANCH_W))]
    args = [proj, proj, proj, proj]
    out_specs = [pl.BlockSpec((seq, BRANCH_W), lambda b: (b, 0))]
    out_shape = [jax.ShapeDtypeStruct((t, BRANCH_W), F32)]
    if norm:
        in_specs += [pl.BlockSpec((1, BRANCH_W), lambda b: (0, 0)), pl.BlockSpec((1, kvw), lambda b: (0, 0))]
        args += [qn, kn]
        out_specs.append(pl.BlockSpec((seq, kvw), lambda b: (b, 0)))
        out_shape.append(jax.ShapeDtypeStruct((t, kvw), F32))
    return pl.pallas_call(
        functools.partial(_attn_ctx_kernel, n_kv=n_kv, norm=norm),
        grid=(batch,), in_specs=in_specs, out_specs=out_specs, out_shape=out_shape,
        compiler_params=_params(32, 1),
        name="attn_ctx_norm" if norm else "attn_ctx",
    )(*args)


def _attn_lat_kernel(q_ref, kv_ref, z_ref, ck_ref, cv_ref, qtab_ref, ktab_ref, qn_ref, kn_ref, o_ref,
                     kall_ref, vall_ref, *, seq):
    kw = N_HEADS // 2 * HEAD_DIM

    @pl.when(pl.program_id(1) == 0)
    def _():
        kv = kv_ref[...]
        k = _rope(_head_rms(kv[:, :kw]) * kn_ref[...], ktab_ref)
        kall_ref[0:seq, :] = k.astype(BF16)
        kall_ref[seq:, :] = ck_ref[...].astype(BF16)
        vall_ref[0:seq, :] = kv[:, kw:].astype(BF16)
        vall_ref[seq:, :] = cv_ref[...].astype(BF16)

    q = _rope(_head_rms(q_ref[...]) * qn_ref[...], qtab_ref)
    kall, vall = kall_ref[...], vall_ref[...]
    outs = [_attend(q[:, _hs(h)], [(kall[:, _hs(h // 2)], vall[:, _hs(h // 2)], None)]) for h in range(N_HEADS)]
    o_ref[...] = jnp.concatenate(outs, axis=-1) * _silu(z_ref[...])


def _attn_lat_call(proj, cache_k, cache_v, layer, batch, seq, qtab, ktab, qn, kn):
    tq = 256
    nq = seq // tq
    past = cache_k.shape[2]
    kw = N_HEADS // 2 * HEAD_DIM
    ctx_spec = pl.BlockSpec((None, None, past, kw), lambda b, i: (b, layer, 0, 0))
    return pl.pallas_call(
        functools.partial(_attn_lat_kernel, seq=seq),
        grid=(batch, nq),
        in_specs=[pl.BlockSpec((tq, BRANCH_W), lambda b, i: (b * nq + i, OFF_AQKV // BRANCH_W)),
                  pl.BlockSpec((seq, 2 * kw), lambda b, i: (b, (OFF_AQKV + BRANCH_W) // (2 * kw))),
                  pl.BlockSpec((tq, BRANCH_W), lambda b, i: (b * nq + i, OFF_AZ // BRANCH_W)),
                  ctx_spec, ctx_spec,
                  pl.BlockSpec((3, tq, BRANCH_W), lambda b, i: (0, i, 0)),
                  pl.BlockSpec((3, seq, kw), lambda b, i: (0, 0, 0)),
                  pl.BlockSpec((1, BRANCH_W), lambda b, i: (0, 0)),
                  pl.BlockSpec((1, kw), lambda b, i: (0, 0))],
        out_specs=pl.BlockSpec((tq, BRANCH_W), lambda b, i: (b * nq + i, 0)),
        out_shape=jax.ShapeDtypeStruct((batch * seq, BRANCH_W), F32),
        scratch_shapes=[pltpu.VMEM((seq + past, kw), BF16), pltpu.VMEM((seq + past, kw), BF16)],
        compiler_params=_params(40, 2),
        name="attn_lat",
    )(proj, proj, proj, cache_k, cache_v, qtab, ktab, qn, kn)


def _na_bias_kernel(t_ref, o_ref):
    nblk = o_ref.shape[0]
    c = lax.broadcasted_iota(jnp.int32, (GRID_W, 2 * GRID_W), 0)
    j = lax.broadcasted_iota(jnp.int32, (GRID_W, 2 * GRID_W), 1)
    kc = j & (GRID_W - 1)
    dc = kc - c + (NA_COLS - 1)
    cs = jnp.clip(c - NA_COLS // 2, 0, GRID_W - NA_COLS)
    valid = jnp.logical_and(kc >= cs, kc < cs + NA_COLS)
    left = j < GRID_W

    def body(b, carry):
        b2 = jnp.minimum(b + 1, nblk - 1)
        acc = jnp.full((GRID_W, 2 * GRID_W), NEG_INF, F32)
        for i in range(N_DC):
            acc = jnp.where(dc == i, jnp.where(left, t_ref[b * N_DC + i], t_ref[b2 * N_DC + i]), acc)
        o_ref[b] = jnp.where(valid, acc, NEG_INF)
        return carry

    lax.fori_loop(0, nblk, body, 0)


def _na_bias_call(na_bias):
    nblk = DEPTH * N_HEADS * N_DR
    return pl.pallas_call(
        _na_bias_kernel,
        in_specs=[pl.BlockSpec(memory_space=pltpu.SMEM)],
        out_specs=pl.BlockSpec((nblk, GRID_W, 2 * GRID_W), lambda: (0, 0, 0)),
        out_shape=jax.ShapeDtypeStruct((nblk, GRID_W, 2 * GRID_W), F32),
        name="na_bias",
    )(na_bias.reshape(-1))


def _na_kernel(q_ref, k_ref, v_ref, z_ref, ck_ref, cv_ref, tb_ref, o_ref, *, rows):
    win = NA_ROWS * GRID_W
    r = pl.program_id(1)
    rs = jnp.clip(r - NA_ROWS // 2, 0, rows - NA_ROWS)
    r0 = pl.multiple_of(rs * GRID_W, GRID_W)
    kwin = k_ref[pl.ds(r0, win), :]
    vwin = v_ref[pl.ds(r0, win), :]
    ck, cv, q = ck_ref[...], cv_ref[...], q_ref[...]
    dr0 = rs - r + NA_ROWS - 1
    outs = []
    for h in range(N_HEADS):
        sl = _hs(h)
        bias = jnp.concatenate([tb_ref[h * N_DR + dr0 + 2 * p] for p in range(NA_ROWS // 2)], axis=1)
        outs.append(_attend(q[:, sl], [(kwin[:, sl], vwin[:, sl], bias), (ck[:, sl], cv[:, sl], None)]))
    o_ref[...] = jnp.concatenate(outs, axis=-1) * _silu(z_ref[...])


def _na_call(proj, cache_k, cache_v, tb, layer, batch, seq):
    rows = seq // GRID_W
    assert rows >= NA_ROWS
    past = cache_k.shape[2]
    nblk = N_HEADS * N_DR
    cq = OFF_D // BRANCH_W
    ctx_spec = pl.BlockSpec((None, None, past, BRANCH_W), lambda b, r: (b, layer, 0, 0))
    return pl.pallas_call(
        functools.partial(_na_kernel, rows=rows),
        grid=(batch, rows),
        in_specs=[pl.BlockSpec((GRID_W, BRANCH_W), lambda b, r: (b * rows + r, cq)),
                  pl.BlockSpec((seq, BRANCH_W), lambda b, r: (b, cq + 1)),
                  pl.BlockSpec((seq, BRANCH_W), lambda b, r: (b, cq + 2)),
                  pl.BlockSpec((GRID_W, BRANCH_W), lambda b, r: (b * rows + r, cq + 3)),
                  ctx_spec, ctx_spec,
                  pl.BlockSpec((nblk, GRID_W, 2 * GRID_W), lambda b, r: (layer, 0, 0))],
        out_specs=pl.BlockSpec((GRID_W, BRANCH_W), lambda b, r: (b * rows + r, 0)),
        out_shape=jax.ShapeDtypeStruct((batch * seq, BRANCH_W), F32),
        compiler_params=_params(32, 2),
        name="na_lat",
    )(proj, proj, proj, proj, cache_k, cache_v, tb)


def _gdn_kernel(*refs, seq, has_s0):
    if has_s0:
        (qkv_ref, z_ref, ab_ref, cw_ref, par_ref, g_ref, s0_ref, o_ref, st_ref,
         xp_ref, qh_ref, kh_ref, vh_ref, gate_ref, oacc_ref) = refs
    else:
        (qkv_ref, z_ref, ab_ref, cw_ref, par_ref, g_ref, o_ref, st_ref,
         xp_ref, qh_ref, kh_ref, vh_ref, gate_ref, oacc_ref) = refs
    n_chunks = seq // CHUNK
    qkv_w = 3 * BRANCH_W
    pad = 8
    half = SHORT_CONV // 2

    xp_ref[0:pad, :] = jnp.zeros((pad, qkv_w), F32)
    xp_ref[seq + pad:seq + 2 * pad, :] = jnp.zeros((pad, qkv_w), F32)
    xp_ref[pad:seq + pad, :] = qkv_ref[...]
    tr = 256
    for t in range(seq // tr):
        base = pad + t * tr - half
        y = xp_ref[base:base + tr, :] * cw_ref[0:1, :]
        for j in range(1, SHORT_CONV):
            y = y + xp_ref[base + j:base + j + tr, :] * cw_ref[j:j + 1, :]
        y = _silu(y)
        rows = slice(t * tr, (t + 1) * tr)
        for h in range(N_HEADS):
            qq = y[:, _hs(h)]
            kk = y[:, BRANCH_W + h * HEAD_DIM:BRANCH_W + (h + 1) * HEAD_DIM]
            vv = y[:, 2 * BRANCH_W + h * HEAD_DIM:2 * BRANCH_W + (h + 1) * HEAD_DIM]
            qh_ref[h, rows, :] = qq * lax.rsqrt(jnp.sum(qq * qq, axis=-1, keepdims=True) + EPS) * SCALE
            kh_ref[h, rows, :] = kk * lax.rsqrt(jnp.sum(kk * kk, axis=-1, keepdims=True) + EPS)
            vh_ref[h, rows, :] = vv

    x = ab_ref[...]
    gate_ref[0] = jax.nn.sigmoid(x)
    xs = x + par_ref[0:1, :]
    softplus = jnp.maximum(xs, 0.0) + jnp.log1p(jnp.exp(-jnp.abs(xs)))
    gate_ref[1] = -jnp.exp(par_ref[1:2, :]) * softplus

    if has_s0:
        st_ref[...] = s0_ref[...]
    else:
        st_ref[...] = jnp.zeros(st_ref.shape, F32)

    li = lax.broadcasted_iota(jnp.int32, (CHUNK, CHUNK), 0)
    lj = lax.broadcasted_iota(jnp.int32, (CHUNK, CHUNK), 1)
    incl = (li >= lj, li <= lj)
    strict = (li > lj, li < lj)
    level = [((li ^ lj) >> l) == 1 for l in range(6)]
    ea =lax.broadcasted_iota(jnp.int32, (8, 128), 0)
    el = lax.broadcasted_iota(jnp.int32, (8, 128), 1)

    def body(i, carry):
        for d in range(2):
            c = i if d == 0 else n_chunks - 1 - i
            r0 = pl.multiple_of(c * CHUNK, CHUNK)
            rows = pl.ds(r0, CHUNK)
            beta = gate_ref[0, rows, :]
            la = gate_ref[1, rows, :]
            tri = jnp.where(incl[d], 1.0, 0.0).astype(BF16)
            gc = _mm_exact(tri, la)
            gl = gc[CHUNK - 1:CHUNK, :] if d == 0 else gc[0:1, :]
            sel = jnp.where(jnp.logical_and(el == ea + (8 + 4 * d), ea < N_HEADS), 1.0, 0.0).astype(BF16)
            gct = _mm_nt_exact(sel, gc)
            eg = jnp.exp(gc)
            egl = jnp.exp(gl - gc)
            eglast = jnp.exp(gl)
            for h in range(N_HEADS):
                ln = 8 + 4 * d + h
                k = kh_ref[h, rows, :]
                q = qh_ref[h, rows, :]
                v = vh_ref[h, rows, :]
                gcol = gc[:, ln:ln + 1]
                grow = gct[h:h + 1, :]
                bcol = beta[:, 4 * d + h:4 * d + h + 1]
                kq = _mm_nt(jnp.concatenate([k, q], axis=0), k)
                kk, qk = kq[:CHUNK], kq[CHUNK:]
                dm = jnp.exp(jnp.where(incl[d], gcol - grow, NEG_INF))
                a = jnp.where(strict[d], bcol * kk * dm, 0.0)
                tm = -jnp.where(level[0], a, 0.0)
                for l in range(1, 6):
                    b = jnp.where(level[l], a, 0.0)
                    y = b + _mm(tm, b)
                    tm = tm - (y + _mm(y, tm))
                bv = bcol * v
                bk = bcol * k * eg[:, ln:ln + 1]
                u = bv + _mm(tm, bv)
                w = bk + _mm(tm, bk)
                s = st_ref[d, h]
                vnew = u - _mm(w, s)
                o = _mm(q * eg[:, ln:ln + 1], s) + _mm(qk * dm, vnew)
                st_ref[d, h] = s * eglast[:, ln:ln + 1] + _mm_tn(k * egl[:, ln:ln + 1], vnew)
                oacc_ref[d, h, rows, :] = o
        return carry

    lax.fori_loop(0, n_chunks, body, 0)

    outs = []
    for h in range(N_HEADS):
        o = oacc_ref[0, h] + oacc_ref[1, h]
        ms = jnp.mean(o * o, axis=-1, keepdims=True)
        outs.append(o * lax.rsqrt(ms + EPS) * g_ref[...])
    o_ref[...] = jnp.concatenate(outs, axis=-1) * _silu(z_ref[...])


def _gdn_call(proj, cw8, par, norm_g, batch, seq, s0=None, layer=0):
    has_s0 = s0 is not None
    qkv_w = 3 * BRANCH_W
    in_specs = [pl.BlockSpec((seq, qkv_w), lambda b: (b, OFF_GQKV // qkv_w)),
                pl.BlockSpec((seq, BRANCH_W), lambda b: (b, OFF_GZ // BRANCH_W)),
                pl.BlockSpec((seq, 128), lambda b: (b, OFF_GAB // 128)),
                pl.BlockSpec((8, qkv_w), lambda b: (0, 0)),
                pl.BlockSpec((2, 128), lambda b: (0, 0)),
                pl.BlockSpec((1, HEAD_DIM), lambda b: (0, 0))]
    args = [proj, proj, proj, cw8, par, norm_g.reshape(1, HEAD_DIM)]
    st_block = (None, 2, N_HEADS, HEAD_DIM, HEAD_DIM)
    if has_s0:
        in_specs.append(pl.BlockSpec((None, None, 2, N_HEADS, HEAD_DIM, HEAD_DIM),
                                     lambda b: (b, layer, 0, 0, 0, 0)))
        args.append(s0)
    return pl.pallas_call(
        functools.partial(_gdn_kernel, seq=seq, has_s0=has_s0),
        grid=(batch,), in_specs=in_specs,
        out_specs=[pl.BlockSpec((seq, BRANCH_W), lambda b: (b, 0)),
                   pl.BlockSpec(st_block, lambda b: (b, 0, 0, 0, 0))],
        out_shape=[jax.ShapeDtypeStruct((batch * seq, BRANCH_W), F32),
                   jax.ShapeDtypeStruct((batch, 2, N_HEADS, HEAD_DIM, HEAD_DIM), F32)],
        scratch_shapes=[pltpu.VMEM((seq + 16, qkv_w), F32),
                        pltpu.VMEM((N_HEADS, seq, HEAD_DIM), F32),
                        pltpu.VMEM((N_HEADS, seq, HEAD_DIM), F32),
                        pltpu.VMEM((N_HEADS, seq, HEAD_DIM), F32),
                        pltpu.VMEM((2, seq, 128), F32),
                        pltpu.VMEM((2, N_HEADS, seq, HEAD_DIM), F32)],
        compiler_params=_params(48, 1),
        name="gdn",
    )(*args)


_RET_LOG_GAMMA = [[float(np.log1p(-np.exp2(-(base + h)))) for h in range(N_HEADS)] for base in RET_DECAY_BASE]


def _ret_kernel(*refs, seq, has_s0):
    if has_s0:
        qkv_ref, z_ref, g_ref, s0_ref, o_ref, st_ref = refs
    else:
        qkv_ref, z_ref, g_ref, o_ref, st_ref = refs
    tq = 256
    tcol = lax.broadcasted_iota(jnp.int32, (seq, 1), 0).astype(F32)
    heads_out = []
    for h in range(N_HEADS):
        lgf, lgb = _RET_LOG_GAMMA[0][h], _RET_LOG_GAMMA[1][h]
        k = qkv_ref[:, BRANCH_W + h * HEAD_DIM:BRANCH_W + (h + 1) * HEAD_DIM]
        v = qkv_ref[:, 2 * BRANCH_W + h * HEAD_DIM:2 * BRANCH_W + (h + 1) * HEAD_DIM]
        stf = _mm_tn(k * jnp.exp((seq - 1.0 - tcol) * lgf), v)
        stb = _mm_tn(k * jnp.exp(tcol * lgb), v)
        if has_s0:
            s0f, s0b = s0_ref[0, h], s0_ref[1, h]
            stf = stf + float(np.exp(seq * lgf)) * s0f
            stb = stb + float(np.exp(seq * lgb)) * s0b
        st_ref[0, h] = stf
        st_ref[1, h] = stb
        tiles = []
        for t in range(seq // tq):
            q = qkv_ref[t * tq:(t + 1) * tq, _hs(h)] * SCALE
            qk = _mm_nt(q, k)
            di = (lax.broadcasted_iota(jnp.int32, (tq, seq), 0) + t * tq
                  - lax.broadcasted_iota(jnp.int32, (tq, seq), 1)).astype(F32)
            dm = (jnp.where(di >= 0, jnp.exp(di * lgf), 0.0) + jnp.where(di <= 0, jnp.exp(-di * lgb), 0.0))
            o = _mm(qk * dm, v)
            if has_s0:
                tt = lax.broadcasted_iota(jnp.int32, (tq, 1), 0).astype(F32) + float(t * tq)
                o = o + _mm(q * jnp.exp((tt + 1.0) * lgf), s0f) + _mm(q * jnp.exp((seq - tt) * lgb), s0b)
            ms = jnp.mean(o * o, axis=-1, keepdims=True)
            tiles.append(o * lax.rsqrt(ms + EPS) * g_ref[...])
        heads_out.append(jnp.concatenate(tiles, axis=0))
    o_ref[...] = jnp.concatenate(heads_out, axis=-1) * _silu(z_ref[...])


def _ret_call(proj, norm_g, batch, seq, s0=None, layer=0):
    has_s0 = s0 is not None
    qkv_w = 3 * BRANCH_W
    in_specs = [pl.BlockSpec((seq, qkv_w), lambda b: (b, OFF_C // qkv_w)),
                pl.BlockSpec((seq, BRANCH_W), lambda b: (b, (OFF_C + qkv_w) // BRANCH_W)),
                pl.BlockSpec((1, HEAD_DIM), lambda b: (0, 0))]
    args = [proj, proj, norm_g.reshape(1, HEAD_DIM)]
    if has_s0:
        in_specs.append(pl.BlockSpec((None, None, 2, N_HEADS, HEAD_DIM, HEAD_DIM),
                                     lambda b: (b, layer, 0, 0, 0, 0)))
        args.append(s0)
    return pl.pallas_call(
        functools.partial(_ret_kernel, seq=seq, has_s0=has_s0),
        grid=(batch,), in_specs=in_specs,
        out_specs=[pl.BlockSpec((seq, BRANCH_W), lambda b: (b, 0)),
                   pl.BlockSpec((None, 2, N_HEADS, HEAD_DIM, HEAD_DIM), lambda b: (b, 0, 0, 0, 0))],
        out_shape=[jax.ShapeDtypeStruct((batch * seq, BRANCH_W), F32),
                   jax.ShapeDtypeStruct((batch, 2, N_HEADS, HEAD_DIM, HEAD_DIM), F32)],
        compiler_params=_params(48, 1),
        name="retention",
    )(*args)


def _out_kernel(*refs, final):
    if final:
        (h_ref, mod_ref, g_ref, oa_ref, ob_ref, oc_ref, od_ref, wg_ref, wb_ref, wo_ref, fn_ref,
         o_ref, y_ref) = refs
    else:
        h_ref, mod_ref, g_ref, oa_ref, ob_ref, oc_ref, od_ref, wg_ref, wb_ref, wo_ref, o_ref = refs
    x = h_ref[...]
    mod = mod_ref[0]
    hn = _modulated_norm(x, mod, g_ref[...]).astype(BF16)
    merged = None
    for n, br_ref in enumerate((oa_ref, ob_ref, oc_ref, od_ref)):
        gate = jax.nn.sigmoid(jnp.dot(hn, wg_ref[:, n * D_MODEL:(n + 1) * D_MODEL], preferred_element_type=F32))
        up = jnp.dot(br_ref[...].astype(BF16), wb_ref[n], preferred_element_type=F32)
        merged = gate * up if merged is None else merged + gate * up
    out = jnp.dot(merged.astype(BF16), wo_ref[...], preferred_element_type=F32)
    hnew = x + mod[:, 2 * D_MODEL:] * out
    o_ref[...] = hnew
    if final:
        ms = jnp.mean(hnew * hnew, axis=-1, keepdims=True)
        y_ref[...] = hnew * lax.rsqrt(ms + EPS) * fn_ref[...]


def _out_call(h2d, mod3, norm_g, branches, wg, wb, wo, rows_per_mod, final_norm=None):
    t = h2d.shape[0]
    tm = 512
    final = final_norm is not None
    if mod3.shape[0] == 1:
        mod_idx = lambda i: (0, 0, 0)
    else:
        mod_idx = lambda i: ((i * tm) // rows_per_mod, 0, 0)
    once = pl.Buffered(1)
    in_specs = [pl.BlockSpec((tm, D_MODEL), lambda i: (i, 0)),
                pl.BlockSpec((1, 1, 3 * D_MODEL), mod_idx),
                pl.BlockSpec((1, D_MODEL), lambda i: (0, 0))]
    in_specs += [pl.BlockSpec((tm, BRANCH_W), lambda i: (i, 0))] * N_BRANCH
    in_specs += [pl.BlockSpec((D_MODEL, N_BRANCH * D_MODEL), lambda i: (0, 0), pipeline_mode=once),
                 pl.BlockSpec((N_BRANCH, BRANCH_W, D_MODEL), lambda i: (0, 0, 0), pipeline_mode=once),
                 pl.BlockSpec((D_MODEL, D_MODEL), lambda i: (0, 0), pipeline_mode=once)]
    args = [h2d, mod3, norm_g.reshape(1, D_MODEL), *branches, wg, wb, wo]
    out_specs = [pl.BlockSpec((tm, D_MODEL), lambda i: (i, 0))]
    out_shape = [jax.ShapeDtypeStruct((t, D_MODEL), F32)]
    if final:
        in_specs.append(pl.BlockSpec((1, D_MODEL), lambda i: (0, 0)))
        args.append(final_norm.reshape(1, D_MODEL))
        out_specs.append(pl.BlockSpec((tm, D_MODEL), lambda i: (i, 0)))
        out_shape.append(jax.ShapeDtypeStruct((t, D_MODEL), F32))
    return pl.pallas_call(
        functools.partial(_out_kernel, final=final),
        grid=(t // tm,), in_specs=in_specs, out_specs=out_specs, out_shape=out_shape,
        compiler_params=_params(48, 1),
        name="merge_out_final" if final else "merge_out",
    )(*args)


def _prep_layer_weights(w_in):
    offs = np.concatenate([[0], np.cumsum(IN_SPLITS)])
    seg = lambda i, j: w_in[:, offs[i]:offs[j]]
    pad = jnp.zeros((D_MODEL, PROJ_W - OFF_GAB - IN_SPLITS[5]), w_in.dtype)
    wcat = jnp.concatenate([seg(7, 11), seg(11, 15), seg(6, 7), seg(4, 5), seg(0, 3), seg(3, 4), seg(5, 6), pad],
                           axis=1).astype(BF16)
    return wcat, seg(15, 16).astype(BF16)


def _rope_tables(seq):
    t = jnp.arange(seq)
    quarter = HEAD_DIM // 4
    inv = ROPE_THETA ** (-jnp.arange(quarter, dtype=F32) / quarter)

    def half(pos):
        ang = pos.astype(F32)[:, None] * inv
        c, s, zero = jnp.cos(ang), jnp.sin(ang), jnp.zeros_like(ang)
        return jnp.concatenate([c, c], -1), jnp.concatenate([-s, zero], -1), jnp.concatenate([zero, s], -1)

    parts = [jnp.concatenate([a, b], -1) for a, b in zip(half(t // GRID_W), half(t % GRID_W))]
    tab = jnp.stack(parts)
    return jnp.tile(tab, (1, 1, N_HEADS)), jnp.tile(tab, (1, 1, N_HEADS // 2))


def _layer(h2d, batch, seq, cond8, n_mod, lw, layer, ctx, final_norm):
    mod = _mod_call(cond8, lw["w_ada"], lw["b_ada"])[:n_mod].reshape(n_mod, 1, 3 * D_MODEL)
    proj = _inproj_call(h2d, mod, lw["norm_g"], lw["wcat"], seq)
    kw = N_HEADS // 2 * HEAD_DIM
    if ctx is None:
        oa, ka = _attn_ctx_call(proj, batch, seq, OFF_AQKV, OFF_AQKV + BRANCH_W, OFF_AQKV + BRANCH_W + kw, OFF_AZ,
                                N_HEADS // 2, lw["qn"], lw["kn"])
        od, = _attn_ctx_call(proj, batch, seq, OFF_D, OFF_D + BRANCH_W, OFF_D + 2 * BRANCH_W, OFF_D + 3 * BRANCH_W,
                             N_HEADS)
        ob, sg = _gdn_call(proj, lw["cw8"], lw["gdn_par"], lw["gdn_norm"], batch, seq)
        oc, sr = _ret_call(proj, lw["ret_norm"], batch, seq)
        va = proj[:, OFF_AQKV + BRANCH_W + kw:OFF_AQKV + BRANCH_W + 2 * kw]
        kn = proj[:, OFF_D + BRANCH_W:OFF_D + 2 * BRANCH_W]
        vn = proj[:, OFF_D + 2 * BRANCH_W:OFF_D + 3 * BRANCH_W]
        extras = (ka, va, kn, vn, sg, sr)
    else:
        oa = _attn_lat_call(proj, ctx["ak"], ctx["av"], layer, batch, seq, ctx["qtab"], ctx["ktab"],
                            lw["qn"], lw["kn"])
        od = _na_call(proj, ctx["nk"], ctx["nv"], ctx["tb"], layer, batch, seq)
        ob, _ = _gdn_call(proj, lw["cw8"], lw["gdn_par"], lw["gdn_norm"], batch, seq, ctx["sg"], layer)
        oc, _ = _ret_call(proj, lw["ret_norm"], batch, seq, ctx["sr"], layer)
        extras = None
    outs = _out_call(h2d, mod, lw["norm_g"], (oa, ob, oc, od), lw["wg"], lw["wb"], lw["wo"], seq, final_norm)
    return outs, extras


def kernel(x_prompt, x_sample, cache_attn_k, cache_attn_v, cache_na_k, cache_na_v, state_gdn, state_ret, c, c_ctx, w_ada, b_ada, norm_g, w_in, conv_w, gdn_a_log, gdn_dt_bias, gdn_norm, attn_q_norm, attn_k_norm, ret_norm, na_bias, w_branch, w_out, final_norm):
    batch, seq, _ = x_prompt.shape
    dbatch, dseq, _ = x_sample.shape
    past = cache_attn_k.shape[2]
    assert dbatch == 8, "the modulation kernel handles exactly one sublane tile of conditioning rows"

    layers = []
    for l in range(DEPTH):
        wcat, wg = _prep_layer_weights(w_in[l])
        par = jnp.zeros((2, 128), F32)
        par = par.at[0, 8:16].set(gdn_dt_bias[l].reshape(-1)).at[1, 8:16].set(gdn_a_log[l].reshape(-1))
        layers.append(dict(
            w_ada=w_ada[l], b_ada=b_ada[l], norm_g=norm_g[l], wcat=wcat, wg=wg,
            wb=w_branch[l].astype(BF16), wo=w_out[l].astype(BF16),
            cw8=jnp.concatenate([conv_w[l], jnp.zeros((8 - SHORT_CONV, 3 * BRANCH_W), F32)], axis=0),
            gdn_par=par, gdn_norm=gdn_norm[l], ret_norm=ret_norm[l],
            qn=jnp.tile(attn_q_norm[l], N_HEADS).reshape(1, BRANCH_W),
            kn=jnp.tile(attn_k_norm[l], N_HEADS // 2).reshape(1, BRANCH_W // 2)))

    h = x_prompt.reshape(batch * seq, D_MODEL)
    cond_ctx = jnp.broadcast_to(c_ctx, (8, D_MODEL))
    per_layer = []
    for l in range(DEPTH):
        outs, extras = _layer(h, batch, seq, cond_ctx, 1, layers[l], l, None,
                              final_norm if l == DEPTH - 1 else None)
        h = outs[0]
        per_layer.append(extras)
    y_prompt = outs[1].reshape(batch, seq, D_MODEL)
    kvh = N_HEADS // 2
    stack = lambda i, shape: jnp.stack([e[i].reshape(shape) for e in per_layer], axis=1)
    new_attn_k = stack(0, (batch, seq, kvh, HEAD_DIM))
    new_attn_v = stack(1, (batch, seq, kvh, HEAD_DIM))
    new_na_k = stack(2, (batch, seq, N_HEADS, HEAD_DIM))
    new_na_v = stack(3, (batch, seq, N_HEADS, HEAD_DIM))
    new_state_gdn = jnp.stack([e[4] for e in per_layer], axis=1)
    new_state_ret = jnp.stack([e[5] for e in per_layer], axis=1)

    qtab, ktab = _rope_tables(dseq)
    ctx = dict(ak=cache_attn_k.reshape(dbatch, DEPTH, past, kvh * HEAD_DIM),
               av=cache_attn_v.reshape(dbatch, DEPTH, past, kvh * HEAD_DIM),
               nk=cache_na_k.reshape(dbatch, DEPTH, past, BRANCH_W),
               nv=cache_na_v.reshape(dbatch, DEPTH, past, BRANCH_W),
               sg=state_gdn, sr=state_ret, tb=_na_bias_call(na_bias), qtab=qtab, ktab=ktab)
    h = x_sample.reshape(dbatch * dseq, D_MODEL)
    for l in range(DEPTH):
        outs, _ = _layer(h, dbatch, dseq, c, dbatch, layers[l], l, ctx, final_norm if l == DEPTH - 1 else None)
        h = outs[0]
    y_sample = outs[1].reshape(dbatch, dseq, D_MODEL)
    return (y_prompt, y_sample, new_attn_k, new_attn_v, new_na_k, new_na_v, new_state_gdn, new_state_ret)
```

```python
import functools

import numpy as np
import jax
import jax.numpy as jnp
from jax import lax
from jax.experimental import pallas as pl
from jax.experimental.pallas import tpu as pltpu

F32 = jnp.float32
BF16 = jnp.bfloat16

D_MODEL = 1024
HEAD_DIM = 64
N_HEADS = 4
BRANCH_W = N_HEADS * HEAD_DIM
N_BRANCH = 4
DEPTH = 2
GRID_W = 64
CHUNK = 64
SHORT_CONV = 5
NA_ROWS = 8
NA_COLS = 16
N_DR = 2 * NA_ROWS - 1
N_DC = 2 * NA_COLS - 1
ROPE_THETA = 10000.0
RET_DECAY_BASE = (5.0, 5.5)
EPS = 1e-6
SCALE = HEAD_DIM ** -0.5
NEG_INF = float("-inf")

IN_SPLITS = (256, 128, 128, 256, 768, 16, 256, 256, 256, 256, 256, 256, 256, 256, 256, 4096)
PROJ_W = 4096
OFF_C = 0
OFF_D = 1024
OFF_GZ = 2048
OFF_GQKV = 2304
OFF_AQKV = 3072
OFF_AZ = 3584
OFF_GAB = 3840

V7X_VMEM_BYTES = 64 * 1024 * 1024
MIB = 1024 * 1024


def _params(vmem_mib, n_axes):
    assert vmem_mib * MIB < V7X_VMEM_BYTES
    return pltpu.CompilerParams(dimension_semantics=("arbitrary",) * n_axes,
                                vmem_limit_bytes=vmem_mib * MIB)


def _mm(a, b):
    return jnp.dot(a.astype(BF16), b.astype(BF16), preferred_element_type=F32)


def _mm_nt(a, b):
    return lax.dot_general(a.astype(BF16), b.astype(BF16), (((1,), (1,)), ((), ())),
                           preferred_element_type=F32)


def _mm_tn(a, b):
    return lax.dot_general(a.astype(BF16), b.astype(BF16), (((0,), (0,)), ((), ())),
                           preferred_element_type=F32)


def _split3(x):
    hi = x.astype(BF16)
    r = x - hi.astype(F32)
    mid = r.astype(BF16)
    lo = (r - mid.astype(F32)).astype(BF16)
    return hi, mid, lo


def _mm_exact(sel, x):
    hi, mid, lo = _split3(x)
    return (jnp.dot(sel, hi, preferred_element_type=F32) + jnp.dot(sel, mid, preferred_element_type=F32)
            + jnp.dot(sel, lo, preferred_element_type=F32))


def _mm_nt_exact(sel, x):
    dn = (((1,), (1,)), ((), ()))
    hi, mid, lo = _split3(x)
    return (lax.dot_general(sel, hi, dn, preferred_element_type=F32)
            + lax.dot_general(sel, mid, dn, preferred_element_type=F32)
            + lax.dot_general(sel, lo, dn, preferred_element_type=F32))


def _silu(x):
    return x * jax.nn.sigmoid(x)


def _mm_exact_lhs(x, sel):
    hi, mid, lo = _split3(x)
    return (jnp.dot(hi, sel, preferred_element_type=F32) + jnp.dot(mid, sel, preferred_element_type=F32)
            + jnp.dot(lo, sel, preferred_element_type=F32))


def _head_block_matrix(width, value):
    ri = lax.broadcasted_iota(jnp.int32, (width, width), 0) >> 6
    ci = lax.broadcasted_iota(jnp.int32, (width, width), 1) >> 6
    return jnp.where(ri == ci, value, 0.0).astype(BF16)


def _head_reduce(x, g):
    hi = x.astype(BF16)
    lo = (x - hi.astype(F32)).astype(BF16)
    return jnp.dot(hi, g, preferred_element_type=F32) + jnp.dot(lo, g, preferred_element_type=F32)


def _head_rms(x):
    ms = _head_reduce(x * x, _head_block_matrix(x.shape[1], 1.0 / HEAD_DIM))
    return x * lax.rsqrt(ms + EPS)


def _rope(x, tab_ref):
    w = x.shape[1]
    return (x * tab_ref[0] + pltpu.roll(x, w - 16, 1) * tab_ref[1] + pltpu.roll(x, 16, 1) * tab_ref[2])


def _attend(qh, parts):
    scores = []
    for k, _, bias in parts:
        s = _mm_nt(qh, k) * SCALE
        if bias is not None:
            s = s + bias
        scores.append(s)
    m = scores[0].max(axis=-1, keepdims=True)
    for s in scores[1:]:
        m = jnp.maximum(m, s.max(axis=-1, keepdims=True))
    den = None
    out = None
    for s, (_, v, _) in zip(scores, parts):
        p = jnp.exp(s - m)
        ps = p.sum(axis=-1, keepdims=True)
        po = _mm(p, v)
        den = ps if den is None else den + ps
        out = po if out is None else out + po
    return out / den


def _hs(h):
    return slice(h * HEAD_DIM, (h + 1) * HEAD_DIM)


def _mod_kernel(c_ref, w_ref, b_ref, o_ref):
    o_ref[...] = _mm(_silu(c_ref[...]), w_ref[...]) + b_ref[...]


def _mod_call(cond8, w_ada, b_ada):
    tn = 512
    return pl.pallas_call(
        _mod_kernel,
        grid=(3 * D_MODEL // tn,),
        in_specs=[pl.BlockSpec((8, D_MODEL), lambda j: (0, 0)),
                  pl.BlockSpec((D_MODEL, tn), lambda j: (0, j)),
                  pl.BlockSpec((1, tn), lambda j: (0, j))],
        out_specs=pl.BlockSpec((8, tn), lambda j: (0, j)),
        out_shape=jax.ShapeDtypeStruct((8, 3 * D_MODEL), F32),
        compiler_params=_params(24, 1),
        name="adaln_mod",
    )(cond8, w_ada, b_ada.reshape(1, 3 * D_MODEL))


def _modulated_norm(x, mod, g):
    ms = jnp.mean(x * x, axis=-1, keepdims=True)
    y = x * lax.rsqrt(ms + EPS) * g
    return y * (1.0 + mod[:, D_MODEL:2 * D_MODEL]) + mod[:, :D_MODEL]


def _inproj_kernel(x_ref, mod_ref, g_ref, w_ref, o_ref, hn_ref):
    @pl.when(pl.program_id(1) == 0)
    def _():
        hn_ref[...] = _modulated_norm(x_ref[...], mod_ref[0], g_ref[...]).astype(BF16)

    o_ref[...] = jnp.dot(hn_ref[...], w_ref[...], preferred_element_type=F32)


def _inproj_call(x2d, mod3, norm_g, wcat, rows_per_mod):
    t = x2d.shape[0]
    tm, tn = 1024, 512
    if mod3.shape[0] == 1:
        mod_idx = lambda i, j: (0, 0, 0)
    else:
        mod_idx = lambda i, j: ((i * tm) // rows_per_mod, 0, 0)
    return pl.pallas_call(
        _inproj_kernel,
        grid=(t // tm, PROJ_W // tn),
        in_specs=[pl.BlockSpec((tm, D_MODEL), lambda i, j: (i, 0)),
                  pl.BlockSpec((1, 1, 3 * D_MODEL), mod_idx),
                  pl.BlockSpec((1, D_MODEL), lambda i, j: (0, 0)),
                  pl.BlockSpec((D_MODEL, tn), lambda i, j: (0, j))],
        out_specs=pl.BlockSpec((tm, tn), lambda i, j: (i, j)),
        out_shape=jax.ShapeDtypeStruct((t, PROJ_W), F32),
        scratch_shapes=[pltpu.VMEM((tm, D_MODEL), BF16)],
        compiler_params=_params(40, 2),
        name="inproj",
    )(x2d, mod3, norm_g.reshape(1, D_MODEL), wcat)


def _attn_ctx_kernel(*refs, n_kv, norm):
    if norm:
        q_ref, k_ref, v_ref, z_ref, qn_ref, kn_ref, o_ref, kout_ref = refs
    else:
        q_ref, k_ref, v_ref, z_ref, o_ref = refs
    q, k, v, z = q_ref[...], k_ref[...], v_ref[...], z_ref[...]
    if norm:
        q = _head_rms(q) * qn_ref[...]
        k = _head_rms(k) * kn_ref[...]
        kout_ref[...] = k
    rep = N_HEADS // n_kv
    outs = [_attend(q[:, _hs(h)], [(k[:, _hs(h // rep)], v[:, _hs(h // rep)], None)]) for h in range(N_HEADS)]
    o_ref[...] = jnp.concatenate(outs, axis=-1) * _silu(z)


def _attn_ctx_call(proj, batch, seq, off_q, off_k, off_v, off_z, n_kv, qn=None, kn=None):
    t = batch * seq
    kvw = n_kv * HEAD_DIM
    norm = qn is not None
    in_specs = [pl.BlockSpec((seq, BRANCH_W), lambda b: (b, off_q // BRANCH_W)),
                pl.BlockSpec((seq, kvw), lambda b: (b, off_k // kvw)),
                pl.BlockSpec((seq, kvw), lambda b: (b, off_v // kvw)),
                pl.BlockSpec((seq, BRANCH_W), lambda b: (b, off_z // BRANCH_W))]
    args = [proj, proj, proj, proj]
    out_specs = [pl.BlockSpec((seq, BRANCH_W), lambda b: (b, 0))]
    out_shape = [jax.ShapeDtypeStruct((t, BRANCH_W), F32)]
    if norm:
        in_specs += [pl.BlockSpec((1, BRANCH_W), lambda b: (0, 0)), pl.BlockSpec((1, kvw), lambda b: (0, 0))]
        args += [qn, kn]
        out_specs.append(pl.BlockSpec((seq, kvw), lambda b: (b, 0)))
        out_shape.append(jax.ShapeDtypeStruct((t, kvw), F32))
    return pl.pallas_call(
        functools.partial(_attn_ctx_kernel, n_kv=n_kv, norm=norm),
        grid=(batch,), in_specs=in_specs, out_specs=out_specs, out_shape=out_shape,
        compiler_params=_params(32, 1),
        name="attn_ctx_norm" if norm else "attn_ctx",
    )(*args)


def _attn_lat_kernel(q_ref, kv_ref, z_ref, ck_ref, cv_ref, qtab_ref, ktab_ref, qn_ref, kn_ref, o_ref,
                     kall_ref, vall_ref, *, seq):
    kw = N_HEADS // 2 * HEAD_DIM

    @pl.when(pl.program_id(1) == 0)
    def _():
        kv = kv_ref[...]
        k = _rope(_head_rms(kv[:, :kw]) * kn_ref[...], ktab_ref)
        kall_ref[0:seq, :] = k.astype(BF16)
        kall_ref[seq:, :] = ck_ref[...].astype(BF16)
        vall_ref[0:seq, :] = kv[:, kw:].astype(BF16)
        vall_ref[seq:, :] = cv_ref[...].astype(BF16)

    q = _rope(_head_rms(q_ref[...]) * qn_ref[...], qtab_ref)
    kall, vall = kall_ref[...], vall_ref[...]
    outs = [_attend(q[:, _hs(h)], [(kall[:, _hs(h // 2)], vall[:, _hs(h // 2)], None)]) for h in range(N_HEADS)]
    o_ref[...] = jnp.concatenate(outs, axis=-1) * _silu(z_ref[...])


def _attn_lat_call(proj, cache_k, cache_v, layer, batch, seq, qtab, ktab, qn, kn):
    tq = 256
    nq = seq // tq
    past = cache_k.shape[2]
    kw = N_HEADS // 2 * HEAD_DIM
    ctx_spec = pl.BlockSpec((None, None, past, kw), lambda b, i: (b, layer, 0, 0))
    return pl.pallas_call(
        functools.partial(_attn_lat_kernel, seq=seq),
        grid=(batch, nq),
        in_specs=[pl.BlockSpec((tq, BRANCH_W), lambda b, i: (b * nq + i, OFF_AQKV // BRANCH_W)),
                  pl.BlockSpec((seq, 2 * kw), lambda b, i: (b, (OFF_AQKV + BRANCH_W) // (2 * kw))),
                  pl.BlockSpec((tq, BRANCH_W), lambda b, i: (b * nq + i, OFF_AZ // BRANCH_W)),
                  ctx_spec, ctx_spec,
                  pl.BlockSpec((3, tq, BRANCH_W), lambda b, i: (0, i, 0)),
                  pl.BlockSpec((3, seq, kw), lambda b, i: (0, 0, 0)),
                  pl.BlockSpec((1, BRANCH_W), lambda b, i: (0, 0)),
                  pl.BlockSpec((1, kw), lambda b, i: (0, 0))],
        out_specs=pl.BlockSpec((tq, BRANCH_W), lambda b, i: (b * nq + i, 0)),
        out_shape=jax.ShapeDtypeStruct((batch * seq, BRANCH_W), F32),
        scratch_shapes=[pltpu.VMEM((seq + past, kw), BF16), pltpu.VMEM((seq + past, kw), BF16)],
        compiler_params=_params(40, 2),
        name="attn_lat",
    )(proj, proj, proj, cache_k, cache_v, qtab, ktab, qn, kn)


def _na_bias_kernel(t_ref, o_ref):
    nblk = o_ref.shape[0]
    c = lax.broadcasted_iota(jnp.int32, (GRID_W, 2 * GRID_W), 0)
    j = lax.broadcasted_iota(jnp.int32, (GRID_W, 2 * GRID_W), 1)
    kc = j & (GRID_W - 1)
    dc = kc - c + (NA_COLS - 1)
    cs = jnp.clip(c - NA_COLS // 2, 0, GRID_W - NA_COLS)
    valid = jnp.logical_and(kc >= cs, kc < cs + NA_COLS)
    left = j < GRID_W

    def body(b, carry):
        b2 = jnp.minimum(b + 1, nblk - 1)
        acc = jnp.full((GRID_W, 2 * GRID_W), NEG_INF, F32)
        for i in range(N_DC):
            acc = jnp.where(dc == i, jnp.where(left, t_ref[b * N_DC + i], t_ref[b2 * N_DC + i]), acc)
        o_ref[b] = jnp.where(valid, acc, NEG_INF)
        return carry

    lax.fori_loop(0, nblk, body, 0)


def _na_bias_call(na_bias):
    nblk = DEPTH * N_HEADS * N_DR
    return pl.pallas_call(
        _na_bias_kernel,
        in_specs=[pl.BlockSpec(memory_space=pltpu.SMEM)],
        out_specs=pl.BlockSpec((nblk, GRID_W, 2 * GRID_W), lambda: (0, 0, 0)),
        out_shape=jax.ShapeDtypeStruct((nblk, GRID_W, 2 * GRID_W), F32),
        name="na_bias",
    )(na_bias.reshape(-1))


def _na_kernel(q_ref, k_ref, v_ref, z_ref, ck_ref, cv_ref, tb_ref, o_ref, *, rows):
    win = NA_ROWS * GRID_W
    r = pl.program_id(1)
    rs = jnp.clip(r - NA_ROWS // 2, 0, rows - NA_ROWS)
    r0 = pl.multiple_of(rs * GRID_W, GRID_W)
    kwin = k_ref[pl.ds(r0, win), :]
    vwin = v_ref[pl.ds(r0, win), :]
    ck, cv, q = ck_ref[...], cv_ref[...], q_ref[...]
    dr0 = rs - r + NA_ROWS - 1
    outs = []
    for h in range(N_HEADS):
        sl = _hs(h)
        bias = jnp.concatenate([tb_ref[h * N_DR + dr0 + 2 * p] for p in range(NA_ROWS // 2)], axis=1)
        outs.append(_attend(q[:, sl], [(kwin[:, sl], vwin[:, sl], bias), (ck[:, sl], cv[:, sl], None)]))
    o_ref[...] = jnp.concatenate(outs, axis=-1) * _silu(z_ref[...])


def _na_call(proj, cache_k, cache_v, tb, layer, batch, seq):
    rows = seq // GRID_W
    assert rows >= NA_ROWS
    past = cache_k.shape[2]
    nblk = N_HEADS * N_DR
    cq = OFF_D // BRANCH_W
    ctx_spec = pl.BlockSpec((None, None, past, BRANCH_W), lambda b, r: (b, layer, 0, 0))
    return pl.pallas_call(
        functools.partial(_na_kernel, rows=rows),
        grid=(batch, rows),
        in_specs=[pl.BlockSpec((GRID_W, BRANCH_W), lambda b, r: (b * rows + r, cq)),
                  pl.BlockSpec((seq, BRANCH_W), lambda b, r: (b, cq + 1)),
                  pl.BlockSpec((seq, BRANCH_W), lambda b, r: (b, cq + 2)),
                  pl.BlockSpec((GRID_W, BRANCH_W), lambda b, r: (b * rows + r, cq + 3)),
                  ctx_spec, ctx_spec,
                  pl.BlockSpec((nblk, GRID_W, 2 * GRID_W), lambda b, r: (layer, 0, 0))],
        out_specs=pl.BlockSpec((GRID_W, BRANCH_W), lambda b, r: (b * rows + r, 0)),
        out_shape=jax.ShapeDtypeStruct((batch * seq, BRANCH_W), F32),
        compiler_params=_params(32, 2),
        name="na_lat",
    )(proj, proj, proj, proj, cache_k, cache_v, tb)


def _gdn_kernel(*refs, seq, has_s0):
    if has_s0:
        (qkv_ref, z_ref, ab_ref, cw_ref, par_ref, g_ref, s0_ref, o_ref, st_ref,
         xp_ref, q_s, k_s, v_s, gate_ref, oacc_ref) = refs
    else:
        (qkv_ref, z_ref, ab_ref, cw_ref, par_ref, g_ref, o_ref, st_ref,
         xp_ref, q_s, k_s, v_s, gate_ref, oacc_ref) = refs
    n_chunks = seq // CHUNK
    n_levels = CHUNK.bit_length() - 1
    qkv_w = 3 * BRANCH_W
    pad = 8
    half = SHORT_CONV // 2
    head_sum = _head_block_matrix(BRANCH_W, 1.0)

    xp_ref[0:pad, :] = jnp.zeros((pad, qkv_w), F32)
    xp_ref[seq + pad:seq + 2 * pad, :] = jnp.zeros((pad, qkv_w), F32)
    xp_ref[pad:seq + pad, :] = qkv_ref[...]
    tr = 256
    for t in range(seq // tr):
        base = pad + t * tr - half
        y = xp_ref[base:base + tr, :] * cw_ref[0:1, :]
        for j in range(1, SHORT_CONV):
            y = y + xp_ref[base + j:base + j + tr, :] * cw_ref[j:j + 1, :]
        y = _silu(y)
        rows = slice(t * tr, (t + 1) * tr)
        qq, kk = y[:, :BRANCH_W], y[:, BRANCH_W:2 * BRANCH_W]
        q_s[rows, :] = qq * lax.rsqrt(_head_reduce(qq * qq, head_sum) + EPS) * SCALE
        k_s[rows, :] = kk * lax.rsqrt(_head_reduce(kk * kk, head_sum) + EPS)
        v_s[rows, :] = y[:, 2 * BRANCH_W:]

    x = ab_ref[...]
    gate_ref[0] = jax.nn.sigmoid(x)
    xs = x + par_ref[0:1, :]
    softplus = jnp.maximum(xs, 0.0) + jnp.log1p(jnp.exp(-jnp.abs(xs)))
    gate_ref[1] = -jnp.exp(par_ref[1:2, :]) * softplus

    if has_s0:
        st_ref[...] = s0_ref[...]
    else:
        st_ref[...] = jnp.zeros(st_ref.shape, F32)

    li = lax.broadcasted_iota(jnp.int32, (CHUNK, BRANCH_W), 0)
    lj = lax.broadcasted_iota(jnp.int32, (CHUNK, BRANCH_W), 1) & (HEAD_DIM - 1)
    incl = (li >= lj, li <= lj)
    strict = (li > lj, li < lj)
    level = [((li ^ lj) >> l) == 1 for l in range(n_levels)]
    ti = lax.broadcasted_iota(jnp.int32, (CHUNK, CHUNK), 0)
    tj = lax.broadcasted_iota(jnp.int32, (CHUNK, CHUNK), 1)
    tri = (jnp.where(ti >= tj, 1.0, 0.0).astype(BF16), jnp.where(ti <= tj, 1.0, 0.0).astype(BF16))
    bi = lax.broadcasted_iota(jnp.int32, (BRANCH_W, BRANCH_W), 0) >> 6
    bj = lax.broadcasted_iota(jnp.int32, (BRANCH_W, BRANCH_W), 1) >> 6
    same_head = bi == bj
    gc_i = lax.broadcasted_iota(jnp.int32, (128, BRANCH_W), 0)
    gh_j = lax.broadcasted_iota(jnp.int32, (128, BRANCH_W), 1) >> 6
    sel_beta = [jnp.where(gc_i == gh_j + 4 * d, 1.0, 0.0).astype(BF16) for d in range(2)]
    sel_gate = [jnp.where(gc_i == gh_j + 8 + 4 * d, 1.0, 0.0).astype(BF16) for d in range(2)]
    rh_i = lax.broadcasted_iota(jnp.int32, (BRANCH_W, 128), 0) >> 6
    rc_j = lax.broadcasted_iota(jnp.int32, (BRANCH_W, 128), 1)
    row_gate = [rc_j == rh_i + 8 + 4 * d for d in range(2)]
    ones_l = jnp.ones((CHUNK, 128), BF16)
    lane_head = lax.broadcasted_iota(jnp.int32, (CHUNK, BRANCH_W), 1) >> 6

    def block_diag(y):
        yb = y.astype(BF16)
        return jnp.where(same_head, jnp.concatenate([yb] * N_HEADS, axis=0), jnp.zeros((), BF16))

    def bdmm(x, ybd):
        return jnp.dot(x.astype(BF16), ybd, preferred_element_type=F32)

    def body(i, carry):
        dirs = range(2)
        rows = [pl.ds(pl.multiple_of((i if d == 0 else n_chunks - 1 - i) * CHUNK, CHUNK), CHUNK) for d in dirs]
        beta = [gate_ref[0, rows[d], :] for d in dirs]
        la = [gate_ref[1, rows[d], :] for d in dirs]
        gc = [_mm_exact(tri[d], la[d]) for d in dirs]
        gcb = [_mm_exact_lhs(gc[d], sel_gate[d]) for d in dirs]
        bcb = [_mm_exact_lhs(beta[d], sel_beta[d]) for d in dirs]
        grow = [_mm_nt_exact(ones_l, jnp.where(row_gate[d], jnp.concatenate([gc[d]] * N_HEADS, axis=0), 0.0))
                for d in dirs]
        dm = [jnp.exp(jnp.where(incl[d], gcb[d] - grow[d], NEG_INF)) for d in dirs]
        k = [k_s[rows[d], :] for d in dirs]
        q = [q_s[rows[d], :] for d in dirs]
        v = [v_s[rows[d], :] for d in dirs]
        kq = [_mm_nt(jnp.concatenate([k[d], q[d]], axis=0), block_diag(k[d])) for d in dirs]
        a = [jnp.where(strict[d], bcb[d] * kq[d][:CHUNK] * dm[d], 0.0) for d in dirs]
        tm = [-jnp.where(level[0], a[d], 0.0) for d in dirs]
        for l in range(1, n_levels):
            b = [jnp.where(level[l], a[d], 0.0) for d in dirs]
            y = [b[d] + bdmm(tm[d], block_diag(b[d])) for d in dirs]
            tm = [tm[d] - (y[d] + bdmm(y[d], block_diag(tm[d]))) for d in dirs]
        eg = [jnp.exp(gcb[d]) for d in dirs]
        bv = [bcb[d] * v[d] for d in dirs]
        bk = [bcb[d] * k[d] * eg[d] for d in dirs]
        u = [bv[d] + bdmm(tm[d], block_diag(bv[d])) for d in dirs]
        w = [bk[d] + bdmm(tm[d], block_diag(bk[d])) for d in dirs]
        s = [st_ref[d] for d in dirs]
        sbd = [block_diag(s[d]) for d in dirs]
        vnew = [u[d] - bdmm(w[d], sbd[d]) for d in dirs]
        o = [bdmm(q[d] * eg[d], sbd[d]) + bdmm(kq[d][CHUNK:] * dm[d], block_diag(vnew[d])) for d in dirs]
        gl = [gcb[0][CHUNK - 1:CHUNK, :], gcb[1][0:1, :]]
        full = [_mm_tn(k[d] * jnp.exp(gl[d] - gcb[d]), vnew[d]) for d in dirs]
        for d in dirs:
            upd = jnp.zeros((CHUNK, BRANCH_W), F32)
            for h in range(N_HEADS):
                upd = jnp.where(lane_head == h, full[d][h * HEAD_DIM:(h + 1) * HEAD_DIM, :], upd)
            st_ref[d] = s[d] * jnp.exp(gl[d]) + upd
            oacc_ref[d, rows[d], :] = o[d]
        return carry

    lax.fori_loop(0, n_chunks, body, 0)

    o = oacc_ref[0] + oacc_ref[1]
    ms = _head_reduce(o * o, _head_block_matrix(BRANCH_W, 1.0 / HEAD_DIM))
    o_ref[...] = o * lax.rsqrt(ms + EPS) * g_ref[...] * _silu(z_ref[...])


def _gdn_call(proj, cw8, par, norm_g, batch, seq, s0=None, layer=0):
    has_s0 = s0 is not None
    qkv_w = 3 * BRANCH_W
    in_specs = [pl.BlockSpec((seq, qkv_w), lambda b: (b, OFF_GQKV // qkv_w)),
                pl.BlockSpec((seq, BRANCH_W), lambda b: (b, OFF_GZ // BRANCH_W)),
                pl.BlockSpec((seq, 128), lambda b: (b, OFF_GAB // 128)),
                pl.BlockSpec((8, qkv_w), lambda b: (0, 0)),
                pl.BlockSpec((2, 128), lambda b: (0, 0)),
                pl.BlockSpec((1, BRANCH_W), lambda b: (0, 0))]
    args = [proj, proj, proj, cw8, par, jnp.tile(norm_g, N_HEADS).reshape(1, BRANCH_W)]
    if has_s0:
        in_specs.append(pl.BlockSpec((None, None, 2, HEAD_DIM, BRANCH_W), lambda b: (b, layer, 0, 0, 0)))
        args.append(s0)
    o, st = pl.pallas_call(
        functools.partial(_gdn_kernel, seq=seq, has_s0=has_s0),
        grid=(batch,), in_specs=in_specs,
        out_specs=[pl.BlockSpec((seq, BRANCH_W), lambda b: (b, 0)),
                   pl.BlockSpec((None, 2, HEAD_DIM, BRANCH_W), lambda b: (b, 0, 0, 0))],
        out_shape=[jax.ShapeDtypeStruct((batch * seq, BRANCH_W), F32),
                   jax.ShapeDtypeStruct((batch, 2, HEAD_DIM, BRANCH_W), F32)],
        scratch_shapes=[pltpu.VMEM((seq + 16, qkv_w), F32),
                        pltpu.VMEM((seq, BRANCH_W), F32),
                        pltpu.VMEM((seq, BRANCH_W), F32),
                        pltpu.VMEM((seq, BRANCH_W), F32),
                        pltpu.VMEM((2, seq, 128), F32),
                        pltpu.VMEM((2, seq, BRANCH_W), F32)],
        compiler_params=_params(48, 1),
        name="gdn",
    )(*args)
    st = st.reshape(batch, 2, HEAD_DIM, N_HEADS, HEAD_DIM).transpose(0, 1, 3, 2, 4)
    return o, st


_RET_LOG_GAMMA = [[float(np.log1p(-np.exp2(-(base + h)))) for h in range(N_HEADS)] for base in RET_DECAY_BASE]


def _ret_kernel(*refs, seq, has_s0):
    if has_s0:
        qkv_ref, z_ref, g_ref, s0_ref, o_ref, st_ref = refs
    else:
        qkv_ref, z_ref, g_ref, o_ref, st_ref = refs
    tq = 256
    tcol = lax.broadcasted_iota(jnp.int32, (seq, 1), 0).astype(F32)
    heads_out = []
    for h in range(N_HEADS):
        lgf, lgb = _RET_LOG_GAMMA[0][h], _RET_LOG_GAMMA[1][h]
        k = qkv_ref[:, BRANCH_W + h * HEAD_DIM:BRANCH_W + (h + 1) * HEAD_DIM]
        v = qkv_ref[:, 2 * BRANCH_W + h * HEAD_DIM:2 * BRANCH_W + (h + 1) * HEAD_DIM]
        stf = _mm_tn(k * jnp.exp((seq - 1.0 - tcol) * lgf), v)
        stb = _mm_tn(k * jnp.exp(tcol * lgb), v)
        if has_s0:
            s0f, s0b = s0_ref[0, h], s0_ref[1, h]
            stf = stf + float(np.exp(seq * lgf)) * s0f
            stb = stb + float(np.exp(seq * lgb)) * s0b
        st_ref[0, h] = stf
        st_ref[1, h] = stb
        tiles = []
        for t in range(seq // tq):
            q = qkv_ref[t * tq:(t + 1) * tq, _hs(h)] * SCALE
            qk = _mm_nt(q, k)
            di = (lax.broadcasted_iota(jnp.int32, (tq, seq), 0) + t * tq
                  - lax.broadcasted_iota(jnp.int32, (tq, seq), 1)).astype(F32)
            dm = (jnp.where(di >= 0, jnp.exp(di * lgf), 0.0) + jnp.where(di <= 0, jnp.exp(-di * lgb), 0.0))
            o = _mm(qk * dm, v)
            if has_s0:
                tt = lax.broadcasted_iota(jnp.int32, (tq, 1), 0).astype(F32) + float(t * tq)
                o = o + _mm(q * jnp.exp((tt + 1.0) * lgf), s0f) + _mm(q * jnp.exp((seq - tt) * lgb), s0b)
            ms = jnp.mean(o * o, axis=-1, keepdims=True)
            tiles.append(o * lax.rsqrt(ms + EPS) * g_ref[...])
        heads_out.append(jnp.concatenate(tiles, axis=0))
    o_ref[...] = jnp.concatenate(heads_out, axis=-1) * _silu(z_ref[...])


def _ret_call(proj, norm_g, batch, seq, s0=None, layer=0):
    has_s0 = s0 is not None
    qkv_w = 3 * BRANCH_W
    in_specs = [pl.BlockSpec((seq, qkv_w), lambda b: (b, OFF_C // qkv_w)),
                pl.BlockSpec((seq, BRANCH_W), lambda b: (b, (OFF_C + qkv_w) // BRANCH_W)),
                pl.BlockSpec((1, HEAD_DIM), lambda b: (0, 0))]
    args = [proj, proj, norm_g.reshape(1, HEAD_DIM)]
    if has_s0:
        in_specs.append(pl.BlockSpec((None, None, 2, N_HEADS, HEAD_DIM, HEAD_DIM),
                                     lambda b: (b, layer, 0, 0, 0, 0)))
        args.append(s0)
    return pl.pallas_call(
        functools.partial(_ret_kernel, seq=seq, has_s0=has_s0),
        grid=(batch,), in_specs=in_specs,
        out_specs=[pl.BlockSpec((seq, BRANCH_W), lambda b: (b, 0)),
                   pl.BlockSpec((None, 2, N_HEADS, HEAD_DIM, HEAD_DIM), lambda b: (b, 0, 0, 0, 0))],
        out_shape=[jax.ShapeDtypeStruct((batch * seq, BRANCH_W), F32),
                   jax.ShapeDtypeStruct((batch, 2, N_HEADS, HEAD_DIM, HEAD_DIM), F32)],
        compiler_params=_params(48, 1),
        name="retention",
    )(*args)


def _out_kernel(*refs, final):
    if final:
        (h_ref, mod_ref, g_ref, oa_ref, ob_ref, oc_ref, od_ref, wg_ref, wb_ref, wo_ref, fn_ref,
         o_ref, y_ref) = refs
    else:
        h_ref, mod_ref, g_ref, oa_ref, ob_ref, oc_ref, od_ref, wg_ref, wb_ref, wo_ref, o_ref = refs
    x = h_ref[...]
    mod = mod_ref[0]
    hn = _modulated_norm(x, mod, g_ref[...]).astype(BF16)
    merged = None
    for n, br_ref in enumerate((oa_ref, ob_ref, oc_ref, od_ref)):
        gate = jax.nn.sigmoid(jnp.dot(hn, wg_ref[:, n * D_MODEL:(n + 1) * D_MODEL], preferred_element_type=F32))
        up = jnp.dot(br_ref[...].astype(BF16), wb_ref[n], preferred_element_type=F32)
        merged = gate * up if merged is None else merged + gate * up
    out = jnp.dot(merged.astype(BF16), wo_ref[...], preferred_element_type=F32)
    hnew = x + mod[:, 2 * D_MODEL:] * out
    o_ref[...] = hnew
    if final:
        ms = jnp.mean(hnew * hnew, axis=-1, keepdims=True)
        y_ref[...] = hnew * lax.rsqrt(ms + EPS) * fn_ref[...]


def _out_call(h2d, mod3, norm_g, branches, wg, wb, wo, rows_per_mod, final_norm=None):
    t = h2d.shape[0]
    tm = 512
    final = final_norm is not None
    if mod3.shape[0] == 1:
        mod_idx = lambda i: (0, 0, 0)
    else:
        mod_idx = lambda i: ((i * tm) // rows_per_mod, 0, 0)
    once = pl.Buffered(1)
    in_specs = [pl.BlockSpec((tm, D_MODEL), lambda i: (i, 0)),
                pl.BlockSpec((1, 1, 3 * D_MODEL), mod_idx),
                pl.BlockSpec((1, D_MODEL), lambda i: (0, 0))]
    in_specs += [pl.BlockSpec((tm, BRANCH_W), lambda i: (i, 0))] * N_BRANCH
    in_specs += [pl.BlockSpec((D_MODEL, N_BRANCH * D_MODEL), lambda i: (0, 0), pipeline_mode=once),
                 pl.BlockSpec((N_BRANCH, BRANCH_W, D_MODEL), lambda i: (0, 0, 0), pipeline_mode=once),
                 pl.BlockSpec((D_MODEL, D_MODEL), lambda i: (0, 0), pipeline_mode=once)]
    args = [h2d, mod3, norm_g.reshape(1, D_MODEL), *branches, wg, wb, wo]
    out_specs = [pl.BlockSpec((tm, D_MODEL), lambda i: (i, 0))]
    out_shape = [jax.ShapeDtypeStruct((t, D_MODEL), F32)]
    if final:
        in_specs.append(pl.BlockSpec((1, D_MODEL), lambda i: (0, 0)))
        args.append(final_norm.reshape(1, D_MODEL))
        out_specs.append(pl.BlockSpec((tm, D_MODEL), lambda i: (i, 0)))
        out_shape.append(jax.ShapeDtypeStruct((t, D_MODEL), F32))
    return pl.pallas_call(
        functools.partial(_out_kernel, final=final),
        grid=(t // tm,), in_specs=in_specs, out_specs=out_specs, out_shape=out_shape,
        compiler_params=_params(48, 1),
        name="merge_out_final" if final else "merge_out",
    )(*args)


def _prep_layer_weights(w_in):
    offs = np.concatenate([[0], np.cumsum(IN_SPLITS)])
    seg = lambda i, j: w_in[:, offs[i]:offs[j]]
    pad = jnp.zeros((D_MODEL, PROJ_W - OFF_GAB - IN_SPLITS[5]), w_in.dtype)
    wcat = jnp.concatenate([seg(7, 11), seg(11, 15), seg(6, 7), seg(4, 5), seg(0, 3), seg(3, 4), seg(5, 6), pad],
                           axis=1).astype(BF16)
    return wcat, seg(15, 16).astype(BF16)


def _rope_tables(seq):
    t = jnp.arange(seq)
    quarter = HEAD_DIM // 4
    inv = ROPE_THETA ** (-jnp.arange(quarter, dtype=F32) / quarter)

    def half(pos):
        ang = pos.astype(F32)[:, None] * inv
        c, s, zero = jnp.cos(ang), jnp.sin(ang), jnp.zeros_like(ang)
        return jnp.concatenate([c, c], -1), jnp.concatenate([-s, zero], -1), jnp.concatenate([zero, s], -1)

    parts = [jnp.concatenate([a, b], -1) for a, b in zip(half(t // GRID_W), half(t % GRID_W))]
    tab = jnp.stack(parts)
    return jnp.tile(tab, (1, 1, N_HEADS)), jnp.tile(tab, (1, 1, N_HEADS // 2))


def _layer(h2d, batch, seq, cond8, n_mod, lw, layer, ctx, final_norm):
    mod = _mod_call(cond8, lw["w_ada"], lw["b_ada"])[:n_mod].reshape(n_mod, 1, 3 * D_MODEL)
    proj = _inproj_call(h2d, mod, lw["norm_g"], lw["wcat"], seq)
    kw = N_HEADS // 2 * HEAD_DIM
    if ctx is None:
        oa, ka = _attn_ctx_call(proj, batch, seq, OFF_AQKV, OFF_AQKV + BRANCH_W, OFF_AQKV + BRANCH_W + kw, OFF_AZ,
                                N_HEADS // 2, lw["qn"], lw["kn"])
        od, = _attn_ctx_call(proj, batch, seq, OFF_D, OFF_D + BRANCH_W, OFF_D + 2 * BRANCH_W, OFF_D + 3 * BRANCH_W,
                             N_HEADS)
        ob, sg = _gdn_call(proj, lw["cw8"], lw["gdn_par"], lw["gdn_norm"], batch, seq)
        oc, sr = _ret_call(proj, lw["ret_norm"], batch, seq)
        va = proj[:, OFF_AQKV + BRANCH_W + kw:OFF_AQKV + BRANCH_W + 2 * kw]
        kn = proj[:, OFF_D + BRANCH_W:OFF_D + 2 * BRANCH_W]
        vn = proj[:, OFF_D + 2 * BRANCH_W:OFF_D + 3 * BRANCH_W]
        extras = (ka, va, kn, vn, sg, sr)
    else:
        oa = _attn_lat_call(proj, ctx["ak"], ctx["av"], layer, batch, seq, ctx["qtab"], ctx["ktab"],
                            lw["qn"], lw["kn"])
        od = _na_call(proj, ctx["nk"], ctx["nv"], ctx["tb"], layer, batch, seq)
        ob, _ = _gdn_call(proj, lw["cw8"], lw["gdn_par"], lw["gdn_norm"], batch, seq, ctx["sg"], layer)
        oc, _ = _ret_call(proj, lw["ret_norm"], batch, seq, ctx["sr"], layer)
        extras = None
    outs = _out_call(h2d, mod, lw["norm_g"], (oa, ob, oc, od), lw["wg"], lw["wb"], lw["wo"], seq, final_norm)
    return outs, extras


def kernel(x_prompt, x_sample, cache_attn_k, cache_attn_v, cache_na_k, cache_na_v, state_gdn, state_ret, c, c_ctx, w_ada, b_ada, norm_g, w_in, conv_w, gdn_a_log, gdn_dt_bias, gdn_norm, attn_q_norm, attn_k_norm, ret_norm, na_bias, w_branch, w_out, final_norm):
    batch, seq, _ = x_prompt.shape
    dbatch, dseq, _ = x_sample.shape
    past = cache_attn_k.shape[2]
    assert dbatch == 8, "the modulation kernel handles exactly one sublane tile of conditioning rows"

    layers = []
    for l in range(DEPTH):
        wcat, wg = _prep_layer_weights(w_in[l])
        par = jnp.zeros((2, 128), F32)
        par = par.at[0, 8:16].set(gdn_dt_bias[l].reshape(-1)).at[1, 8:16].set(gdn_a_log[l].reshape(-1))
        layers.append(dict(
            w_ada=w_ada[l], b_ada=b_ada[l], norm_g=norm_g[l], wcat=wcat, wg=wg,
            wb=w_branch[l].astype(BF16), wo=w_out[l].astype(BF16),
            cw8=jnp.concatenate([conv_w[l], jnp.zeros((8 - SHORT_CONV, 3 * BRANCH_W), F32)], axis=0),
            gdn_par=par, gdn_norm=gdn_norm[l], ret_norm=ret_norm[l],
            qn=jnp.tile(attn_q_norm[l], N_HEADS).reshape(1, BRANCH_W),
            kn=jnp.tile(attn_k_norm[l], N_HEADS // 2).reshape(1, BRANCH_W // 2)))

    h = x_prompt.reshape(batch * seq, D_MODEL)
    cond_ctx = jnp.broadcast_to(c_ctx, (8, D_MODEL))
    per_layer = []
    for l in range(DEPTH):
        outs, extras = _layer(h, batch, seq, cond_ctx, 1, layers[l], l, None,
                              final_norm if l == DEPTH - 1 else None)
        h = outs[0]
        per_layer.append(extras)
    y_prompt = outs[1].reshape(batch, seq, D_MODEL)
    kvh = N_HEADS // 2
    stack = lambda i, shape: jnp.stack([e[i].reshape(shape) for e in per_layer], axis=1)
    new_attn_k = stack(0, (batch, seq, kvh, HEAD_DIM))
    new_attn_v = stack(1, (batch, seq, kvh, HEAD_DIM))
    new_na_k = stack(2, (batch, seq, N_HEADS, HEAD_DIM))
    new_na_v = stack(3, (batch, seq, N_HEADS, HEAD_DIM))
    new_state_gdn = jnp.stack([e[4] for e in per_layer], axis=1)
    new_state_ret = jnp.stack([e[5] for e in per_layer], axis=1)

    qtab, ktab = _rope_tables(dseq)
    ctx = dict(ak=cache_attn_k.reshape(dbatch, DEPTH, past, kvh * HEAD_DIM),
               av=cache_attn_v.reshape(dbatch, DEPTH, past, kvh * HEAD_DIM),
               nk=cache_na_k.reshape(dbatch, DEPTH, past, BRANCH_W),
               nv=cache_na_v.reshape(dbatch, DEPTH, past, BRANCH_W),
               sg=state_gdn.transpose(0, 1, 2, 4, 3, 5).reshape(dbatch, DEPTH, 2, HEAD_DIM, BRANCH_W),
               sr=state_ret, tb=_na_bias_call(na_bias), qtab=qtab, ktab=ktab)
    h = x_sample.reshape(dbatch * dseq, D_MODEL)
    for l in range(DEPTH):
        outs, _ = _layer(h, dbatch, dseq, c, dbatch, layers[l], l, ctx, final_norm if l == DEPTH - 1 else None)
        h = outs[0]
    y_sample = outs[1].reshape(dbatch, dseq, D_MODEL)
    return (y_prompt, y_sample, new_attn_k, new_attn_v, new_na_k, new_na_v, new_state_gdn, new_state_ret)
```

```python
import functools

import numpy as np
import jax
import jax.numpy as jnp
from jax import lax
from jax.experimental import pallas as pl
from jax.experimental.pallas import tpu as pltpu

F32 = jnp.float32
BF16 = jnp.bfloat16

D_MODEL = 1024
HEAD_DIM = 64
N_HEADS = 4
KV_HEADS = N_HEADS // 2
BRANCH_W = N_HEADS * HEAD_DIM
N_BRANCH = 4
DEPTH = 2
GRID_W = 64
CHUNK = 64
PREP_CHUNKS = 4
SHORT_CONV = 5
NA_ROWS = 8
NA_COLS = 16
N_DR = 2 * NA_ROWS - 1
N_DC = 2 * NA_COLS - 1
ROPE_THETA = 10000.0
RET_DECAY_BASE = (5.0, 5.5)
EPS = 1e-6
SCALE = HEAD_DIM ** -0.5
NEG_INF = float("-inf")

IN_SPLITS = (256, 128, 128, 256, 768, 16, 256, 256, 256, 256, 256, 256, 256, 256, 256, 4096)
PROJ_W = 4096
OFF_C = 0
OFF_D = 1024
OFF_GZ = 2048
OFF_GQKV = 2304
OFF_AQKV = 3072
OFF_AZ = 3584
OFF_GAB = 3840

V7X_VMEM_BYTES = 64 * 1024 * 1024
MIB = 1024 * 1024


def _params(vmem_mib, n_axes):
    assert vmem_mib * MIB < V7X_VMEM_BYTES
    return pltpu.CompilerParams(dimension_semantics=("arbitrary",) * n_axes,
                                vmem_limit_bytes=vmem_mib * MIB)


def _layer_spec(block, layer, n_grid):
    zeros = (0,) * len(block)
    if n_grid == 1:
        return pl.BlockSpec((None,) + block, lambda i: (layer,) + zeros)
    return pl.BlockSpec((None,) + block, lambda i, j: (layer,) + zeros)


def _mm(a, b):
    return jnp.dot(a.astype(BF16), b.astype(BF16), preferred_element_type=F32)


def _mm_nt(a, b):
    return lax.dot_general(a.astype(BF16), b.astype(BF16), (((1,), (1,)), ((), ())),
                           preferred_element_type=F32)


def _mm_tn(a, b):
    return lax.dot_general(a.astype(BF16), b.astype(BF16), (((0,), (0,)), ((), ())),
                           preferred_element_type=F32)


def _split3(x):
    hi = x.astype(BF16)
    r = x - hi.astype(F32)
    mid = r.astype(BF16)
    lo = (r - mid.astype(F32)).astype(BF16)
    return hi, mid, lo


def _mm_exact(sel, x):
    hi, mid, lo = _split3(x)
    return (jnp.dot(sel, hi, preferred_element_type=F32) + jnp.dot(sel, mid, preferred_element_type=F32)
            + jnp.dot(sel, lo, preferred_element_type=F32))


def _mm_exact_lhs(x, sel):
    hi, mid, lo = _split3(x)
    return (jnp.dot(hi, sel, preferred_element_type=F32) + jnp.dot(mid, sel, preferred_element_type=F32)
            + jnp.dot(lo, sel, preferred_element_type=F32))


def _mm_nt_exact(sel, x):
    dn = (((1,), (1,)), ((), ()))
    hi, mid, lo = _split3(x)
    return (lax.dot_general(sel, hi, dn, preferred_element_type=F32)
            + lax.dot_general(sel, mid, dn, preferred_element_type=F32)
            + lax.dot_general(sel, lo, dn, preferred_element_type=F32))


def _silu(x):
    return x * jax.nn.sigmoid(x)


def _head_block_matrix(width, value):
    ri = lax.broadcasted_iota(jnp.int32, (width, width), 0) >> 6
    ci = lax.broadcasted_iota(jnp.int32, (width, width), 1) >> 6
    return jnp.where(ri == ci, value, 0.0).astype(BF16)


def _head_reduce(x, g):
    hi = x.astype(BF16)
    lo = (x - hi.astype(F32)).astype(BF16)
    return jnp.dot(hi, g, preferred_element_type=F32) + jnp.dot(lo, g, preferred_element_type=F32)


def _head_rms(x):
    ms = _head_reduce(x * x, _head_block_matrix(x.shape[1], 1.0 / HEAD_DIM))
    return x * lax.rsqrt(ms + EPS)


def _rope(x, tab_ref):
    w = x.shape[1]
    return (x * tab_ref[0] + pltpu.roll(x, w - 16, 1) * tab_ref[1] + pltpu.roll(x, 16, 1) * tab_ref[2])


def _attend(qh, parts):
    scores = []
    for k, _, bias, feature_major in parts:
        s = (_mm(qh, k) if feature_major else _mm_nt(qh, k)) * SCALE
        if bias is not None:
            s = s + bias
        scores.append(s)
    m = scores[0].max(axis=-1, keepdims=True)
    for s in scores[1:]:
        m = jnp.maximum(m, s.max(axis=-1, keepdims=True))
    den = None
    out = None
    for s, (_, v, _, feature_major) in zip(scores, parts):
        p = jnp.exp(s - m)
        ps = p.sum(axis=-1, keepdims=True)
        po = _mm_nt(p, v) if feature_major else _mm(p, v)
        den = ps if den is None else den + ps
        out = po if out is None else out + po
    return out / den


def _hs(h):
    return slice(h * HEAD_DIM, (h + 1) * HEAD_DIM)


def _aligned(x, m):
    return x if isinstance(x, int) else pl.multiple_of(x, m)


def _mod_kernel(c_ref, w_ref, b_ref, o_ref):
    o_ref[...] = _mm(_silu(c_ref[...]), w_ref[...]) + b_ref[...]


def _mod_call(cond8, w_ada, b_ada3, layer):
    tn = 512
    return pl.pallas_call(
        _mod_kernel,
        grid=(3 * D_MODEL // tn,),
        in_specs=[pl.BlockSpec((8, D_MODEL), lambda j: (0, 0)),
                  pl.BlockSpec((None, D_MODEL, tn), lambda j: (layer, 0, j)),
                  pl.BlockSpec((None, 1, tn), lambda j: (layer, 0, j))],
        out_specs=pl.BlockSpec((8, tn), lambda j: (0, j)),
        out_shape=jax.ShapeDtypeStruct((8, 3 * D_MODEL), F32),
        compiler_params=_params(24, 1),
        name="adaln_mod",
    )(cond8, w_ada, b_ada3)


def _modulated_norm(x, mod, g):
    ms = jnp.mean(x * x, axis=-1, keepdims=True)
    y = x * lax.rsqrt(ms + EPS) * g
    return y * (1.0 + mod[:, D_MODEL:2 * D_MODEL]) + mod[:, :D_MODEL]


def _inproj_kernel(x_ref, mod_ref, g_ref, w_ref, o_ref, hn_ref):
    @pl.when(pl.program_id(1) == 0)
    def _():
        hn_ref[...] = _modulated_norm(x_ref[...], mod_ref[0], g_ref[...]).astype(BF16)

    o_ref[...] = jnp.dot(hn_ref[...], w_ref[...], preferred_element_type=F32)


def _inproj_call(x2d, mod3, norm_g3, wcat, layer, rows_per_mod):
    t = x2d.shape[0]
    tm, tn = 1024, 512
    if mod3.shape[0] == 1:
        mod_idx = lambda i, j: (0, 0, 0)
    else:
        mod_idx = lambda i, j: ((i * tm) // rows_per_mod, 0, 0)
    return pl.pallas_call(
        _inproj_kernel,
        grid=(t // tm, PROJ_W // tn),
        in_specs=[pl.BlockSpec((tm, D_MODEL), lambda i, j: (i, 0)),
                  pl.BlockSpec((1, 1, 3 * D_MODEL), mod_idx),
                  _layer_spec((1, D_MODEL), layer, 2),
                  pl.BlockSpec((None, D_MODEL, tn), lambda i, j: (layer, 0, j))],
        out_specs=pl.BlockSpec((tm, tn), lambda i, j: (i, j)),
        out_shape=jax.ShapeDtypeStruct((t, PROJ_W), F32),
        scratch_shapes=[pltpu.VMEM((tm, D_MODEL), BF16)],
        compiler_params=_params(40, 2),
        name="inproj",
    )(x2d, mod3, norm_g3, wcat)


def _attn_ctx_kernel(*refs, n_kv, norm):
    if norm:
        q_ref, k_ref, v_ref, z_ref, qn_ref, kn_ref, _, _, o_ref, kt_ref, vt_ref = refs
    else:
        q_ref, k_ref, v_ref, z_ref, _, _, o_ref, kt_ref, vt_ref = refs
    q, k, v, z = q_ref[...], k_ref[...], v_ref[...], z_ref[...]
    if norm:
        q = _head_rms(q) * qn_ref[...]
        k = _head_rms(k) * kn_ref[...]
    seq = k.shape[0]
    kt_ref[...] = k.T.reshape(n_kv, HEAD_DIM, seq)
    vt_ref[...] = v.T.reshape(n_kv, HEAD_DIM, seq)
    rep = N_HEADS // n_kv
    outs = [_attend(q[:, _hs(h)], [(k[:, _hs(h // rep)], v[:, _hs(h // rep)], None, False)])
            for h in range(N_HEADS)]
    o_ref[...] = jnp.concatenate(outs, axis=-1) * _silu(z)


def _attn_ctx_call(proj, kt_all, vt_all, layer, batch, seq, off_q, off_k, off_v, off_z, n_kv, qn=None, kn=None):
    t = batch * seq
    kvw = n_kv * HEAD_DIM
    norm = qn is not None
    in_specs = [pl.BlockSpec((seq, BRANCH_W), lambda b: (b, off_q // BRANCH_W)),
                pl.BlockSpec((seq, kvw), lambda b: (b, off_k // kvw)),
                pl.BlockSpec((seq, kvw), lambda b: (b, off_v // kvw)),
                pl.BlockSpec((seq, BRANCH_W), lambda b: (b, off_z // BRANCH_W))]
    args = [proj, proj, proj, proj]
    if norm:
        in_specs += [_layer_spec((1, BRANCH_W), layer, 1), _layer_spec((1, kvw), layer, 1)]
        args += [qn, kn]
    n_in = len(args)
    in_specs += [pl.BlockSpec(memory_space=pl.ANY)] * 2
    args += [kt_all, vt_all]
    cache_spec = pl.BlockSpec((None, None, n_kv, HEAD_DIM, seq), lambda b: (b, layer, 0, 0, 0))
    return pl.pallas_call(
        functools.partial(_attn_ctx_kernel, n_kv=n_kv, norm=norm),
        grid=(batch,), in_specs=in_specs,
        out_specs=[pl.BlockSpec((seq, BRANCH_W), lambda b: (b, 0)), cache_spec, cache_spec],
        out_shape=[jax.ShapeDtypeStruct((t, BRANCH_W), F32),
                   jax.ShapeDtypeStruct(kt_all.shape, F32), jax.ShapeDtypeStruct(vt_all.shape, F32)],
        input_output_aliases={n_in: 1, n_in + 1: 2},
        compiler_params=_params(32, 1),
        name="attn_ctx_norm" if norm else "attn_ctx",
    )(*args)


def _attn_lat_kernel(q_ref, kv_ref, z_ref, ckt_ref, cvt_ref, qtab_ref, ktab_ref, qn_ref, kn_ref, o_ref,
                     k_s, v_s):
    kw = KV_HEADS * HEAD_DIM

    @pl.when(pl.program_id(1) == 0)
    def _():
        kv = kv_ref[...]
        k_s[...] = _rope(_head_rms(kv[:, :kw]) * kn_ref[...], ktab_ref).astype(BF16)
        v_s[...] = kv[:, kw:].astype(BF16)

    q = _rope(_head_rms(q_ref[...]) * qn_ref[...], qtab_ref)
    k, v = k_s[...], v_s[...]
    outs = []
    for h in range(N_HEADS):
        g = h // (N_HEADS // KV_HEADS)
        outs.append(_attend(q[:, _hs(h)], [(k[:, _hs(g)], v[:, _hs(g)], None, False),
                                           (ckt_ref[g], cvt_ref[g], None, True)]))
    o_ref[...] = jnp.concatenate(outs, axis=-1) * _silu(z_ref[...])


def _attn_lat_call(proj, cache_kt, cache_vt, layer, batch, seq, qtab, ktab, qn, kn):
    tq = 256
    nq = seq // tq
    past = cache_kt.shape[-1]
    kw = KV_HEADS * HEAD_DIM
    ctx_spec = pl.BlockSpec((None, None, KV_HEADS, HEAD_DIM, past), lambda b, i: (b, layer, 0, 0, 0))
    return pl.pallas_call(
        _attn_lat_kernel,
        grid=(batch, nq),
        in_specs=[pl.BlockSpec((tq, BRANCH_W), lambda b, i: (b * nq + i, OFF_AQKV // BRANCH_W)),
                  pl.BlockSpec((seq, 2 * kw), lambda b, i: (b, (OFF_AQKV + BRANCH_W) // (2 * kw))),
                  pl.BlockSpec((tq, BRANCH_W), lambda b, i: (b * nq + i, OFF_AZ // BRANCH_W)),
                  ctx_spec, ctx_spec,
                  pl.BlockSpec((3, tq, BRANCH_W), lambda b, i: (0, i, 0)),
                  pl.BlockSpec((3, seq, kw), lambda b, i: (0, 0, 0)),
                  _layer_spec((1, BRANCH_W), layer, 2),
                  _layer_spec((1, kw), layer, 2)],
        out_specs=pl.BlockSpec((tq, BRANCH_W), lambda b, i: (b * nq + i, 0)),
        out_shape=jax.ShapeDtypeStruct((batch * seq, BRANCH_W), F32),
        scratch_shapes=[pltpu.VMEM((seq, kw), BF16), pltpu.VMEM((seq, kw), BF16)],
        compiler_params=_params(40, 2),
        name="attn_lat",
    )(proj, proj, proj, cache_kt, cache_vt, qtab, ktab, qn, kn)


def _na_bias_kernel(t_ref, o_ref):
    nblk = o_ref.shape[0]
    c = lax.broadcasted_iota(jnp.int32, (GRID_W, 2 * GRID_W), 0)
    j = lax.broadcasted_iota(jnp.int32, (GRID_W, 2 * GRID_W), 1)
    kc = j & (GRID_W - 1)
    dc = kc - c + (NA_COLS - 1)
    cs = jnp.clip(c - NA_COLS // 2, 0, GRID_W - NA_COLS)
    valid = jnp.logical_and(kc >= cs, kc < cs + NA_COLS)
    left = j < GRID_W

    def body(b, carry):
        b2 = jnp.minimum(b + 1, nblk - 1)
        acc = jnp.full((GRID_W, 2 * GRID_W), NEG_INF, F32)
        for i in range(N_DC):
            acc = jnp.where(dc == i, jnp.where(left, t_ref[b * N_DC + i], t_ref[b2 * N_DC + i]), acc)
        o_ref[b] = jnp.where(valid, acc, NEG_INF)
        return carry

    lax.fori_loop(0, nblk, body, 0)


def _na_bias_call(na_bias):
    nblk = DEPTH * N_HEADS * N_DR
    return pl.pallas_call(
        _na_bias_kernel,
        in_specs=[pl.BlockSpec(memory_space=pltpu.SMEM)],
        out_specs=pl.BlockSpec((nblk, GRID_W, 2 * GRID_W), lambda: (0, 0, 0)),
        out_shape=jax.ShapeDtypeStruct((nblk, GRID_W, 2 * GRID_W), F32),
        name="na_bias",
    )(na_bias.reshape(-1))


def _na_kernel(q_ref, k_ref, v_ref, z_ref, ckt_ref, cvt_ref, tb_ref, o_ref, *, rows):
    win = NA_ROWS * GRID_W
    r = pl.program_id(1)
    rs = jnp.clip(r - NA_ROWS // 2, 0, rows - NA_ROWS)
    r0 = pl.multiple_of(rs * GRID_W, GRID_W)
    kwin = k_ref[pl.ds(r0, win), :]
    vwin = v_ref[pl.ds(r0, win), :]
    q = q_ref[...]
    dr0 = rs - r + NA_ROWS - 1
    outs = []
    for h in range(N_HEADS):
        sl = _hs(h)
        bias = jnp.concatenate([tb_ref[h * N_DR + dr0 + 2 * p] for p in range(NA_ROWS // 2)], axis=1)
        outs.append(_attend(q[:, sl], [(kwin[:, sl], vwin[:, sl], bias, False),
                                       (ckt_ref[h], cvt_ref[h], None, True)]))
    o_ref[...] = jnp.concatenate(outs, axis=-1) * _silu(z_ref[...])


def _na_call(proj, cache_kt, cache_vt, tb, layer, batch, seq):
    rows = seq // GRID_W
    assert rows >= NA_ROWS
    past = cache_kt.shape[-1]
    nblk = N_HEADS * N_DR
    cq = OFF_D // BRANCH_W
    ctx_spec = pl.BlockSpec((None, None, N_HEADS, HEAD_DIM, past), lambda b, r: (b, layer, 0, 0, 0))
    return pl.pallas_call(
        functools.partial(_na_kernel, rows=rows),
        grid=(batch, rows),
        in_specs=[pl.BlockSpec((GRID_W, BRANCH_W), lambda b, r: (b * rows + r, cq)),
                  pl.BlockSpec((seq, BRANCH_W), lambda b, r: (b, cq + 1)),
                  pl.BlockSpec((seq, BRANCH_W), lambda b, r: (b, cq + 2)),
                  pl.BlockSpec((GRID_W, BRANCH_W), lambda b, r: (b * rows + r, cq + 3)),
                  ctx_spec, ctx_spec,
                  pl.BlockSpec((nblk, GRID_W, 2 * GRID_W), lambda b, r: (layer, 0, 0))],
        out_specs=pl.BlockSpec((GRID_W, BRANCH_W), lambda b, r: (b * rows + r, 0)),
        out_shape=jax.ShapeDtypeStruct((batch * seq, BRANCH_W), F32),
        compiler_params=_params(32, 2),
        name="na_lat",
    )(proj, proj, proj, proj, cache_kt, cache_vt, tb)


def _gdn_kernel(*refs, seq, has_s0):
    if has_s0:
        qkv_ref, z_ref, ab_ref, cw_ref, par_ref, g_ref, s0_ref, _, o_ref, st_ref = refs[:10]
    else:
        qkv_ref, z_ref, ab_ref, cw_ref, par_ref, g_ref, _, o_ref, st_ref = refs[:9]
    xp_ref, q_s, k_s, v_s, gate_ref, u_s, w_s, qd_s, qkm_s, kd_s, egl_s, s_s, oacc_ref = refs[-13:]
    n_chunks = seq // CHUNK
    n_levels = CHUNK.bit_length() - 1
    qkv_w = 3 * BRANCH_W
    pad = 8
    half = SHORT_CONV // 2
    head_sum = _head_block_matrix(BRANCH_W, 1.0)

    xp_ref[0:pad, :] = jnp.zeros((pad, qkv_w), F32)
    xp_ref[seq + pad:seq + 2 * pad, :] = jnp.zeros((pad, qkv_w), F32)
    xp_ref[pad:seq + pad, :] = qkv_ref[...]
    tr = 256
    for t in range(seq // tr):
        base = pad + t * tr - half
        y = xp_ref[base:base + tr, :] * cw_ref[0:1, :]
        for j in range(1, SHORT_CONV):
            y = y + xp_ref[base + j:base + j + tr, :] * cw_ref[j:j + 1, :]
        y = _silu(y)
        rows = slice(t * tr, (t + 1) * tr)
        qq, kk = y[:, :BRANCH_W], y[:, BRANCH_W:2 * BRANCH_W]
        q_s[rows, :] = qq * lax.rsqrt(_head_reduce(qq * qq, head_sum) + EPS) * SCALE
        k_s[rows, :] = kk * lax.rsqrt(_head_reduce(kk * kk, head_sum) + EPS)
        v_s[rows, :] = y[:, 2 * BRANCH_W:]

    x = ab_ref[...]
    gate_ref[0] = jax.nn.sigmoid(x)
    xs = x + par_ref[0:1, :]
    softplus = jnp.maximum(xs, 0.0) + jnp.log1p(jnp.exp(-jnp.abs(xs)))
    gate_ref[1] = -jnp.exp(par_ref[1:2, :]) * softplus

    for d in range(2):
        if has_s0:
            s_s[d] = jnp.concatenate([s0_ref[d, h] for h in range(N_HEADS)], axis=-1)
        else:
            s_s[d] = jnp.zeros((HEAD_DIM, BRANCH_W), F32)

    li = lax.broadcasted_iota(jnp.int32, (CHUNK, BRANCH_W), 0)
    lj = lax.broadcasted_iota(jnp.int32, (CHUNK, BRANCH_W), 1) & (HEAD_DIM - 1)
    incl = (li >= lj, li <= lj)
    strict = (li > lj, li < lj)
    level = [((li ^ lj) >> l) == 1 for l in range(n_levels)]
    ti = lax.broadcasted_iota(jnp.int32, (CHUNK, CHUNK), 0)
    tj = lax.broadcasted_iota(jnp.int32, (CHUNK, CHUNK), 1)
    tri = (jnp.where(ti >= tj, 1.0, 0.0).astype(BF16), jnp.where(ti <= tj, 1.0, 0.0).astype(BF16))
    bi = lax.broadcasted_iota(jnp.int32, (BRANCH_W, BRANCH_W), 0) >> 6
    bj = lax.broadcasted_iota(jnp.int32, (BRANCH_W, BRANCH_W), 1) >> 6
    same_head = bi == bj
    gc_i = lax.broadcasted_iota(jnp.int32, (128, BRANCH_W), 0)
    gh_j = lax.broadcasted_iota(jnp.int32, (128, BRANCH_W), 1) >> 6
    sel_beta = [jnp.where(gc_i == gh_j + 4 * d, 1.0, 0.0).astype(BF16) for d in range(2)]
    sel_gate = [jnp.where(gc_i == gh_j + 8 + 4 * d, 1.0, 0.0).astype(BF16) for d in range(2)]
    rh_i = lax.broadcasted_iota(jnp.int32, (BRANCH_W, 128), 0) >> 6
    rc_j = lax.broadcasted_iota(jnp.int32, (BRANCH_W, 128), 1)
    row_gate = [rc_j == rh_i + 8 + 4 * d for d in range(2)]
    ones_l = jnp.ones((CHUNK, 128), BF16)
    lane_head = lax.broadcasted_iota(jnp.int32, (CHUNK, BRANCH_W), 1) >> 6

    def block_diag(y):
        yb = y.astype(BF16)
        return jnp.where(same_head, jnp.concatenate([yb] * N_HEADS, axis=0), jnp.zeros((), BF16))

    def bdmm(x, ybd):
        return jnp.dot(x.astype(BF16), ybd, preferred_element_type=F32)

    def prepare(chains):
        n = range(len(chains))
        dd = [d for d, _ in chains]
        rows = [pl.ds(_aligned(c * CHUNK, CHUNK), CHUNK) for _, c in chains]
        beta = [gate_ref[0, rows[i], :] for i in n]
        la = [gate_ref[1, rows[i], :] for i in n]
        gc = [_mm_exact(tri[dd[i]], la[i]) for i in n]
        gcb = [_mm_exact_lhs(gc[i], sel_gate[dd[i]]) for i in n]
        bcb = [_mm_exact_lhs(beta[i], sel_beta[dd[i]]) for i in n]
        grow = [_mm_nt_exact(ones_l, jnp.where(row_gate[dd[i]], jnp.concatenate([gc[i]] * N_HEADS, axis=0), 0.0))
                for i in n]
        dm = [jnp.exp(jnp.where(incl[dd[i]], gcb[i] - grow[i], NEG_INF)) for i in n]
        k = [k_s[rows[i], :] for i in n]
        q = [q_s[rows[i], :] for i in n]
        v = [v_s[rows[i], :] for i in n]
        kq = [_mm_nt(jnp.concatenate([k[i], q[i]], axis=0), block_diag(k[i])) for i in n]
        a = [jnp.where(strict[dd[i]], bcb[i] * kq[i][:CHUNK] * dm[i], 0.0) for i in n]
        tm = [-jnp.where(level[0], a[i], 0.0) for i in n]
        for l in range(1, n_levels):
            b = [jnp.where(level[l], a[i], 0.0) for i in n]
            y = [b[i] + bdmm(tm[i], block_diag(b[i])) for i in n]
            tm = [tm[i] - (y[i] + bdmm(y[i], block_diag(tm[i]))) for i in n]
        eg = [jnp.exp(gcb[i]) for i in n]
        bv = [bcb[i] * v[i] for i in n]
        bk = [bcb[i] * k[i] * eg[i] for i in n]
        u = [bv[i] + bdmm(tm[i], block_diag(bv[i])) for i in n]
        w = [bk[i] + bdmm(tm[i], block_diag(bk[i])) for i in n]
        for i in n:
            d, c = chains[i]
            gl = gcb[i][CHUNK - 1:CHUNK, :] if d == 0 else gcb[i][0:1, :]
            u_s[d, rows[i], :] = u[i]
            w_s[d, rows[i], :] = w[i].astype(BF16)
            qd_s[d, rows[i], :] = (q[i] * eg[i]).astype(BF16)
            qkm_s[d, rows[i], :] = (kq[i][CHUNK:] * dm[i]).astype(BF16)
            kd_s[d, rows[i], :] = (k[i] * jnp.exp(gl - gcb[i])).astype(BF16)
            egl_s[d, pl.ds(_aligned(c * 8, 8), 8), :] = jnp.broadcast_to(jnp.exp(gl), (8, BRANCH_W))

    group = min(PREP_CHUNKS, n_chunks)
    n_groups = n_chunks // group
    if n_groups == 1:
        prepare([(d, c) for c in range(group) for d in range(2)])
    else:
        def prep_body(j, carry):
            prepare([(d, j * group + c) for c in range(group) for d in range(2)])
            return carry
        lax.fori_loop(0, n_groups, prep_body, 0)

    def scan_body(i, carry):
        dirs = range(2)
        cidx = [i, n_chunks - 1 - i]
        rows = [pl.ds(pl.multiple_of(cidx[d] * CHUNK, CHUNK), CHUNK) for d in dirs]
        s = [s_s[d] for d in dirs]
        sbd = [block_diag(s[d]) for d in dirs]
        vnew = [u_s[d, rows[d], :] - bdmm(w_s[d, rows[d], :], sbd[d]) for d in dirs]
        o = [bdmm(qd_s[d, rows[d], :], sbd[d]) + bdmm(qkm_s[d, rows[d], :], block_diag(vnew[d])) for d in dirs]
        full = [_mm_tn(kd_s[d, rows[d], :], vnew[d]) for d in dirs]
        for d in dirs:
            upd = jnp.zeros((CHUNK, BRANCH_W), F32)
            for h in range(N_HEADS):
                upd = jnp.where(lane_head == h, full[d][h * HEAD_DIM:(h + 1) * HEAD_DIM, :], upd)
            egl = egl_s[d, pl.ds(pl.multiple_of(cidx[d] * 8, 8), 8), :][0:1, :]
            s_s[d] = s[d] * egl + upd
            oacc_ref[d, rows[d], :] = o[d]
        return carry

    lax.fori_loop(0, n_chunks, scan_body, 0)

    for d in range(2):
        s = s_s[d]
        for h in range(N_HEADS):
            st_ref[d, h] = s[:, _hs(h)]
    o = oacc_ref[0] + oacc_ref[1]
    ms = _head_reduce(o * o, _head_block_matrix(BRANCH_W, 1.0 / HEAD_DIM))
    o_ref[...] = o * lax.rsqrt(ms + EPS) * g_ref[...] * _silu(z_ref[...])


def _state_spec(layer):
    return pl.BlockSpec((None, None, 2, N_HEADS, HEAD_DIM, HEAD_DIM), lambda b: (b, layer, 0, 0, 0, 0))


def _gdn_call(proj, cw8, par, norm_g, st_all, layer, batch, seq, s0=None):
    has_s0 = s0 is not None
    qkv_w = 3 * BRANCH_W
    in_specs = [pl.BlockSpec((seq, qkv_w), lambda b: (b, OFF_GQKV // qkv_w)),
                pl.BlockSpec((seq, BRANCH_W), lambda b: (b, OFF_GZ // BRANCH_W)),
                pl.BlockSpec((seq, 128), lambda b: (b, OFF_GAB // 128)),
                _layer_spec((8, qkv_w), layer, 1),
                _layer_spec((2, 128), layer, 1),
                _layer_spec((1, BRANCH_W), layer, 1)]
    args = [proj, proj, proj, cw8, par, norm_g]
    if has_s0:
        in_specs.append(_state_spec(layer))
        args.append(s0)
    n_in = len(args)
    in_specs.append(pl.BlockSpec(memory_space=pl.ANY))
    args.append(st_all)
    return pl.pallas_call(
        functools.partial(_gdn_kernel, seq=seq, has_s0=has_s0),
        grid=(batch,), in_specs=in_specs,
        out_specs=[pl.BlockSpec((seq, BRANCH_W), lambda b: (b, 0)), _state_spec(layer)],
        out_shape=[jax.ShapeDtypeStruct((batch * seq, BRANCH_W), F32),
                   jax.ShapeDtypeStruct(st_all.shape, F32)],
        input_output_aliases={n_in: 1},
        scratch_shapes=[pltpu.VMEM((seq + 16, qkv_w), F32),
                        pltpu.VMEM((seq, BRANCH_W), F32),
                        pltpu.VMEM((seq, BRANCH_W), F32),
                        pltpu.VMEM((seq, BRANCH_W), F32),
                        pltpu.VMEM((2, seq, 128), F32),
                        pltpu.VMEM((2, seq, BRANCH_W), F32),
                        pltpu.VMEM((2, seq, BRANCH_W), BF16),
                        pltpu.VMEM((2, seq, BRANCH_W), BF16),
                        pltpu.VMEM((2, seq, BRANCH_W), BF16),
                        pltpu.VMEM((2, seq, BRANCH_W), BF16),
                        pltpu.VMEM((2, seq // CHUNK * 8, BRANCH_W), F32),
                        pltpu.VMEM((2, HEAD_DIM, BRANCH_W), F32),
                        pltpu.VMEM((2, seq, BRANCH_W), F32)],
        compiler_params=_params(48, 1),
        name="gdn",
    )(*args)


_RET_LOG_GAMMA = [[float(np.log1p(-np.exp2(-(base + h)))) for h in range(N_HEADS)] for base in RET_DECAY_BASE]


def _ret_kernel(*refs, seq, has_s0):
    if has_s0:
        qkv_ref, z_ref, g_ref, s0_ref, _, o_ref, st_ref = refs
    else:
        qkv_ref, z_ref, g_ref, _, o_ref, st_ref = refs
    tq = 256
    tcol = lax.broadcasted_iota(jnp.int32, (seq, 1), 0).astype(F32)
    heads_out = []
    for h in range(N_HEADS):
        lgf, lgb = _RET_LOG_GAMMA[0][h], _RET_LOG_GAMMA[1][h]
        k = qkv_ref[:, BRANCH_W + h * HEAD_DIM:BRANCH_W + (h + 1) * HEAD_DIM]
        v = qkv_ref[:, 2 * BRANCH_W + h * HEAD_DIM:2 * BRANCH_W + (h + 1) * HEAD_DIM]
        stf = _mm_tn(k * jnp.exp((seq - 1.0 - tcol) * lgf), v)
        stb = _mm_tn(k * jnp.exp(tcol * lgb), v)
        if has_s0:
            s0f, s0b = s0_ref[0, h], s0_ref[1, h]
            stf = stf + float(np.exp(seq * lgf)) * s0f
            stb = stb + float(np.exp(seq * lgb)) * s0b
        st_ref[0, h] = stf
        st_ref[1, h] = stb
        tiles = []
        for t in range(seq // tq):
            q = qkv_ref[t * tq:(t + 1) * tq, _hs(h)] * SCALE
            qk = _mm_nt(q, k)
            di = (lax.broadcasted_iota(jnp.int32, (tq, seq), 0) + t * tq
                  - lax.broadcasted_iota(jnp.int32, (tq, seq), 1)).astype(F32)
            dm = (jnp.where(di >= 0, jnp.exp(di * lgf), 0.0) + jnp.where(di <= 0, jnp.exp(-di * lgb), 0.0))
            o = _mm(qk * dm, v)
            if has_s0:
                tt = lax.broadcasted_iota(jnp.int32, (tq, 1), 0).astype(F32) + float(t * tq)
                o = o + _mm(q * jnp.exp((tt + 1.0) * lgf), s0f) + _mm(q * jnp.exp((seq - tt) * lgb), s0b)
            ms = jnp.mean(o * o, axis=-1, keepdims=True)
            tiles.append(o * lax.rsqrt(ms + EPS) * g_ref[...])
        heads_out.append(jnp.concatenate(tiles, axis=0))
    o_ref[...] = jnp.concatenate(heads_out, axis=-1) * _silu(z_ref[...])


def _ret_call(proj, norm_g, st_all, layer, batch, seq, s0=None):
    has_s0 = s0 is not None
    qkv_w = 3 * BRANCH_W
    in_specs = [pl.BlockSpec((seq, qkv_w), lambda b: (b, OFF_C // qkv_w)),
                pl.BlockSpec((seq, BRANCH_W), lambda b: (b, (OFF_C + qkv_w) // BRANCH_W)),
                _layer_spec((1, HEAD_DIM), layer, 1)]
    args = [proj, proj, norm_g]
    if has_s0:
        in_specs.append(_state_spec(layer))
        args.append(s0)
    n_in = len(args)
    in_specs.append(pl.BlockSpec(memory_space=pl.ANY))
    args.append(st_all)
    return pl.pallas_call(
        functools.partial(_ret_kernel, seq=seq, has_s0=has_s0),
        grid=(batch,), in_specs=in_specs,
        out_specs=[pl.BlockSpec((seq, BRANCH_W), lambda b: (b, 0)), _state_spec(layer)],
        out_shape=[jax.ShapeDtypeStruct((batch * seq, BRANCH_W), F32),
                   jax.ShapeDtypeStruct(st_all.shape, F32)],
        input_output_aliases={n_in: 1},
        compiler_params=_params(48, 1),
        name="retention",
    )(*args)


def _out_kernel(*refs, final):
    if final:
        (h_ref, mod_ref, g_ref, oa_ref, ob_ref, oc_ref, od_ref, wg_ref, wb_ref, wo_ref, fn_ref,
         o_ref, y_ref) = refs
    else:
        h_ref, mod_ref, g_ref, oa_ref, ob_ref, oc_ref, od_ref, wg_ref, wb_ref, wo_ref, o_ref = refs
    x = h_ref[...]
    mod = mod_ref[0]
    hn = _modulated_norm(x, mod, g_ref[...]).astype(BF16)
    merged = None
    for n, br_ref in enumerate((oa_ref, ob_ref, oc_ref, od_ref)):
        gate = jax.nn.sigmoid(jnp.dot(hn, wg_ref[:, n * D_MODEL:(n + 1) * D_MODEL], preferred_element_type=F32))
        up = jnp.dot(br_ref[...].astype(BF16), wb_ref[n], preferred_element_type=F32)
        merged = gate * up if merged is None else merged + gate * up
    out = jnp.dot(merged.astype(BF16), wo_ref[...], preferred_element_type=F32)
    hnew = x + mod[:, 2 * D_MODEL:] * out
    o_ref[...] = hnew
    if final:
        ms = jnp.mean(hnew * hnew, axis=-1, keepdims=True)
        y_ref[...] = hnew * lax.rsqrt(ms + EPS) * fn_ref[...]


def _out_call(h2d, mod3, norm_g3, branches, wg, wb, wo, layer, rows_per_mod, final_norm=None):
    t = h2d.shape[0]
    tm = 512
    final = final_norm is not None
    if mod3.shape[0] == 1:
        mod_idx = lambda i: (0, 0, 0)
    else:
        mod_idx = lambda i: ((i * tm) // rows_per_mod, 0, 0)
    once = pl.Buffered(1)
    in_specs = [pl.BlockSpec((tm, D_MODEL), lambda i: (i, 0)),
                pl.BlockSpec((1, 1, 3 * D_MODEL), mod_idx),
                _layer_spec((1, D_MODEL), layer, 1)]
    in_specs += [pl.BlockSpec((tm, BRANCH_W), lambda i: (i, 0))] * N_BRANCH
    in_specs += [pl.BlockSpec((None, D_MODEL, N_BRANCH * D_MODEL), lambda i: (layer, 0, 0), pipeline_mode=once),
                 pl.BlockSpec((None, N_BRANCH, BRANCH_W, D_MODEL), lambda i: (layer, 0, 0, 0), pipeline_mode=once),
                 pl.BlockSpec((None, D_MODEL, D_MODEL), lambda i: (layer, 0, 0), pipeline_mode=once)]
    args = [h2d, mod3, norm_g3, *branches, wg, wb, wo]
    out_specs = [pl.BlockSpec((tm, D_MODEL), lambda i: (i, 0))]
    out_shape = [jax.ShapeDtypeStruct((t, D_MODEL), F32)]
    if final:
        in_specs.append(pl.BlockSpec((1, D_MODEL), lambda i: (0, 0)))
        args.append(final_norm.reshape(1, D_MODEL))
        out_specs.append(pl.BlockSpec((tm, D_MODEL), lambda i: (i, 0)))
        out_shape.append(jax.ShapeDtypeStruct((t, D_MODEL), F32))
    return pl.pallas_call(
        functools.partial(_out_kernel, final=final),
        grid=(t // tm,), in_specs=in_specs, out_specs=out_specs, out_shape=out_shape,
        compiler_params=_params(48, 1),
        name="merge_out_final" if final else "merge_out",
    )(*args)


def _prep_weights(w_in):
    offs = np.concatenate([[0], np.cumsum(IN_SPLITS)])
    seg = lambda i, j: w_in[:, :, offs[i]:offs[j]]
    pad = jnp.zeros((DEPTH, D_MODEL, PROJ_W - OFF_GAB - IN_SPLITS[5]), w_in.dtype)
    wcat = jnp.concatenate([seg(7, 11), seg(11, 15), seg(6, 7), seg(4, 5), seg(0, 3), seg(3, 4), seg(5, 6), pad],
                           axis=2).astype(BF16)
    return wcat, seg(15, 16).astype(BF16)


def _rope_tables(seq):
    t = jnp.arange(seq)
    quarter = HEAD_DIM // 4
    inv = ROPE_THETA ** (-jnp.arange(quarter, dtype=F32) / quarter)

    def half(pos):
        ang = pos.astype(F32)[:, None] * inv
        c, s, zero = jnp.cos(ang), jnp.sin(ang), jnp.zeros_like(ang)
        return jnp.concatenate([c, c], -1), jnp.concatenate([-s, zero], -1), jnp.concatenate([zero, s], -1)

    parts = [jnp.concatenate([a, b], -1) for a, b in zip(half(t // GRID_W), half(t % GRID_W))]
    tab = jnp.stack(parts)
    return jnp.tile(tab, (1, 1, N_HEADS)), jnp.tile(tab, (1, 1, KV_HEADS))


def _layer(h2d, batch, seq, cond8, n_mod, pw, layer, ctx, caches, final_norm):
    mod = _mod_call(cond8, pw["w_ada"], pw["b_ada"], layer)[:n_mod].reshape(n_mod, 1, 3 * D_MODEL)
    proj = _inproj_call(h2d, mod, pw["norm_g"], pw["wcat"], layer, seq)
    kw = KV_HEADS * HEAD_DIM
    if ctx is None:
        akt, avt, nkt, nvt, sg_all, sr_all = caches
        oa, akt, avt = _attn_ctx_call(proj, akt, avt, layer, batch, seq, OFF_AQKV, OFF_AQKV + BRANCH_W,
                                      OFF_AQKV + BRANCH_W + kw, OFF_AZ, KV_HEADS, pw["qn"], pw["kn"])
        od, nkt, nvt = _attn_ctx_call(proj, nkt, nvt, layer, batch, seq, OFF_D, OFF_D + BRANCH_W,
                                      OFF_D + 2 * BRANCH_W, OFF_D + 3 * BRANCH_W, N_HEADS)
        ob, sg_all = _gdn_call(proj, pw["cw8"], pw["gdn_par"], pw["gdn_norm"], sg_all, layer, batch, seq)
        oc, sr_all = _ret_call(proj, pw["ret_norm"], sr_all, layer, batch, seq)
        caches = (akt, avt, nkt, nvt, sg_all, sr_all)
    else:
        sg_all, sr_all = caches
        oa = _attn_lat_call(proj, ctx["akt"], ctx["avt"], layer, batch, seq, ctx["qtab"], ctx["ktab"],
                            pw["qn"], pw["kn"])
        od = _na_call(proj, ctx["nkt"], ctx["nvt"], ctx["tb"], layer, batch, seq)
        ob, sg_all = _gdn_call(proj, pw["cw8"], pw["gdn_par"], pw["gdn_norm"], sg_all, layer, batch, seq, ctx["sg"])
        oc, sr_all = _ret_call(proj, pw["ret_norm"], sr_all, layer, batch, seq, ctx["sr"])
        caches = (sg_all, sr_all)
    outs = _out_call(h2d, mod, pw["norm_g"], (oa, ob, oc, od), pw["wg"], pw["wb"], pw["wo"], layer, seq, final_norm)
    return outs, caches


def kernel(x_prompt, x_sample, cache_attn_k, cache_attn_v, cache_na_k, cache_na_v, state_gdn, state_ret, c, c_ctx, w_ada, b_ada, norm_g, w_in, conv_w, gdn_a_log, gdn_dt_bias, gdn_norm, attn_q_norm, attn_k_norm, ret_norm, na_bias, w_branch, w_out, final_norm):
    batch, seq, _ = x_prompt.shape
    dbatch, dseq, _ = x_sample.shape
    assert dbatch == 8, "the modulation kernel handles exactly one sublane tile of conditioning rows"

    wcat, wg = _prep_weights(w_in)
    par = jnp.zeros((DEPTH, 2, 128), F32)
    par = par.at[:, 0, 8:16].set(gdn_dt_bias.reshape(DEPTH, 8)).at[:, 1, 8:16].set(gdn_a_log.reshape(DEPTH, 8))
    pw = dict(
        w_ada=w_ada, b_ada=b_ada.reshape(DEPTH, 1, 3 * D_MODEL), norm_g=norm_g.reshape(DEPTH, 1, D_MODEL),
        wcat=wcat, wg=wg, wb=w_branch.astype(BF16), wo=w_out.astype(BF16),
        cw8=jnp.concatenate([conv_w, jnp.zeros((DEPTH, 8 - SHORT_CONV, 3 * BRANCH_W), F32)], axis=1),
        gdn_par=par,
        gdn_norm=jnp.tile(gdn_norm, (1, N_HEADS)).reshape(DEPTH, 1, BRANCH_W),
        ret_norm=ret_norm.reshape(DEPTH, 1, HEAD_DIM),
        qn=jnp.tile(attn_q_norm, (1, N_HEADS)).reshape(DEPTH, 1, BRANCH_W),
        kn=jnp.tile(attn_k_norm, (1, KV_HEADS)).reshape(DEPTH, 1, KV_HEADS * HEAD_DIM))

    h = x_prompt.reshape(batch * seq, D_MODEL)
    cond_ctx = jnp.broadcast_to(c_ctx, (8, D_MODEL))
    caches = (jnp.zeros((batch, DEPTH, KV_HEADS, HEAD_DIM, seq), F32),
              jnp.zeros((batch, DEPTH, KV_HEADS, HEAD_DIM, seq), F32),
              jnp.zeros((batch, DEPTH, N_HEADS, HEAD_DIM, seq), F32),
              jnp.zeros((batch, DEPTH, N_HEADS, HEAD_DIM, seq), F32),
              jnp.zeros((batch, DEPTH, 2, N_HEADS, HEAD_DIM, HEAD_DIM), F32),
              jnp.zeros((batch, DEPTH, 2, N_HEADS, HEAD_DIM, HEAD_DIM), F32))
    for l in range(DEPTH):
        outs, caches = _layer(h, batch, seq, cond_ctx, 1, pw, l, None, caches,
                              final_norm if l == DEPTH - 1 else None)
        h = outs[0]
    y_prompt = outs[1].reshape(batch, seq, D_MODEL)
    token_major = lambda a: a.transpose(0, 1, 4, 2, 3)
    new_attn_k, new_attn_v, new_na_k, new_na_v = (token_major(a) for a in caches[:4])
    new_state_gdn, new_state_ret = caches[4], caches[5]

    qtab, ktab = _rope_tables(dseq)
    feature_major = lambda a: a.transpose(0, 1, 3, 4, 2)
    ctx = dict(akt=feature_major(cache_attn_k), avt=feature_major(cache_attn_v),
               nkt=feature_major(cache_na_k), nvt=feature_major(cache_na_v),
               sg=state_gdn, sr=state_ret, tb=_na_bias_call(na_bias), qtab=qtab, ktab=ktab)
    h = x_sample.reshape(dbatch * dseq, D_MODEL)
    lat_states = (jnp.zeros(state_gdn.shape, F32), jnp.zeros(state_ret.shape, F32))
    for l in range(DEPTH):
        outs, lat_states = _layer(h, dbatch, dseq, c, dbatch, pw, l, ctx, lat_states,
                                  final_norm if l == DEPTH - 1 else None)
        h = outs[0]
    y_sample = outs[1].reshape(dbatch, dseq, D_MODEL)
    return (y_prompt, y_sample, new_attn_k, new_attn_v, new_na_k, new_na_v, new_state_gdn, new_state_ret)
```

```python
import functools

import numpy as np
import jax
import jax.numpy as jnp
from jax import lax
from jax.experimental import pallas as pl
from jax.experimental.pallas import tpu as pltpu

F32 = jnp.float32
BF16 = jnp.bfloat16

D_MODEL = 1024
HEAD_DIM = 64
N_HEADS = 4
KV_HEADS = N_HEADS // 2
BRANCH_W = N_HEADS * HEAD_DIM
N_BRANCH = 4
DEPTH = 2
GRID_W = 64
CHUNK = 64
PREP_CHUNKS = 4
SHORT_CONV = 5
NA_ROWS = 8
NA_COLS = 16
N_DR = 2 * NA_ROWS - 1
N_DC = 2 * NA_COLS - 1
ROPE_THETA = 10000.0
RET_DECAY_BASE = (5.0, 5.5)
EPS = 1e-6
SCALE = HEAD_DIM ** -0.5
NEG_INF = float("-inf")

IN_SPLITS = (256, 128, 128, 256, 768, 16, 256, 256, 256, 256, 256, 256, 256, 256, 256, 4096)
PROJ_W = 4096
OFF_C = 0
OFF_D = 1024
OFF_GZ = 2048
OFF_GQKV = 2304
OFF_AQKV = 3072
OFF_AZ = 3584
OFF_GAB = 3840

V7X_VMEM_BYTES = 64 * 1024 * 1024
MIB = 1024 * 1024


def _params(vmem_mib, n_axes):
    assert vmem_mib * MIB < V7X_VMEM_BYTES
    return pltpu.CompilerParams(dimension_semantics=("arbitrary",) * n_axes,
                                vmem_limit_bytes=vmem_mib * MIB)


def _layer_spec(block, layer, n_grid):
    zeros = (0,) * len(block)
    if n_grid == 1:
        return pl.BlockSpec((None,) + block, lambda i: (layer,) + zeros)
    return pl.BlockSpec((None,) + block, lambda i, j: (layer,) + zeros)


def _mm(a, b):
    return jnp.dot(a.astype(BF16), b.astype(BF16), preferred_element_type=F32)


def _mm_nt(a, b):
    return lax.dot_general(a.astype(BF16), b.astype(BF16), (((1,), (1,)), ((), ())),
                           preferred_element_type=F32)


def _mm_tn(a, b):
    return lax.dot_general(a.astype(BF16), b.astype(BF16), (((0,), (0,)), ((), ())),
                           preferred_element_type=F32)


def _split3(x):
    hi = x.astype(BF16)
    r = x - hi.astype(F32)
    mid = r.astype(BF16)
    lo = (r - mid.astype(F32)).astype(BF16)
    return hi, mid, lo


def _mm_exact(sel, x):
    hi, mid, lo = _split3(x)
    return (jnp.dot(sel, hi, preferred_element_type=F32) + jnp.dot(sel, mid, preferred_element_type=F32)
            + jnp.dot(sel, lo, preferred_element_type=F32))


def _mm_exact_lhs(x, sel):
    hi, mid, lo = _split3(x)
    return (jnp.dot(hi, sel, preferred_element_type=F32) + jnp.dot(mid, sel, preferred_element_type=F32)
            + jnp.dot(lo, sel, preferred_element_type=F32))


def _mm_nt_exact(sel, x):
    dn = (((1,), (1,)), ((), ()))
    hi, mid, lo = _split3(x)
    return (lax.dot_general(sel, hi, dn, preferred_element_type=F32)
            + lax.dot_general(sel, mid, dn, preferred_element_type=F32)
            + lax.dot_general(sel, lo, dn, preferred_element_type=F32))


def _silu(x):
    return x * jax.nn.sigmoid(x)


def _head_block_matrix(width, value):
    ri = lax.broadcasted_iota(jnp.int32, (width, width), 0) >> 6
    ci = lax.broadcasted_iota(jnp.int32, (width, width), 1) >> 6
    return jnp.where(ri == ci, value, 0.0).astype(BF16)


def _head_reduce(x, g):
    hi = x.astype(BF16)
    lo = (x - hi.astype(F32)).astype(BF16)
    return jnp.dot(hi, g, preferred_element_type=F32) + jnp.dot(lo, g, preferred_element_type=F32)


def _head_rms(x):
    ms = _head_reduce(x * x, _head_block_matrix(x.shape[1], 1.0 / HEAD_DIM))
    return x * lax.rsqrt(ms + EPS)


def _rope(x, tab_ref):
    w = x.shape[1]
    return (x * tab_ref[0] + pltpu.roll(x, w - 16, 1) * tab_ref[1] + pltpu.roll(x, 16, 1) * tab_ref[2])


def _attend(qs, parts):
    groups = range(len(qs))

    def score(q, part):
        k, _, bias, feature_major = part
        s = (_mm(q, k) if feature_major else _mm_nt(q, k)) * SCALE
        return s if bias is None else s + bias

    scores = [[score(qs[g], part) for part in parts[g]] for g in groups]
    m = [functools.reduce(jnp.maximum, [s.max(axis=-1, keepdims=True) for s in scores[g]]) for g in groups]
    p = [[jnp.exp(s - m[g]) for s in scores[g]] for g in groups]
    den = [sum(x.sum(axis=-1, keepdims=True) for x in p[g]) for g in groups]
    out = [sum(_mm_nt(x, part[1]) if part[3] else _mm(x, part[1]) for x, part in zip(p[g], parts[g]))
           for g in groups]
    return [out[g] / den[g] for g in groups]


def _hs(h):
    return slice(h * HEAD_DIM, (h + 1) * HEAD_DIM)


def _aligned(x, m):
    return x if isinstance(x, int) else pl.multiple_of(x, m)


def _mod_kernel(c_ref, w_ref, b_ref, o_ref):
    o_ref[...] = _mm(_silu(c_ref[...]), w_ref[...]) + b_ref[...]


def _mod_call(cond, w_ada, b_ada3):
    tn = 512
    rows = cond.shape[0]
    return pl.pallas_call(
        _mod_kernel,
        grid=(DEPTH, 3 * D_MODEL // tn),
        in_specs=[pl.BlockSpec((rows, D_MODEL), lambda l, j: (0, 0)),
                  pl.BlockSpec((None, D_MODEL, tn), lambda l, j: (l, 0, j)),
                  pl.BlockSpec((None, 1, tn), lambda l, j: (l, 0, j))],
        out_specs=pl.BlockSpec((None, rows, tn), lambda l, j: (l, 0, j)),
        out_shape=jax.ShapeDtypeStruct((DEPTH, rows, 3 * D_MODEL), F32),
        compiler_params=_params(24, 2),
        name="adaln_mod",
    )(cond, w_ada, b_ada3)


def _modulated_norm(x, mod, g):
    ms = jnp.mean(x * x, axis=-1, keepdims=True)
    y = x * lax.rsqrt(ms + EPS) * g
    return y * (1.0 + mod[:, D_MODEL:2 * D_MODEL]) + mod[:, :D_MODEL]


def _inproj_kernel(x_ref, mod_ref, g_ref, w_ref, o_ref, hn_ref):
    @pl.when(pl.program_id(1) == 0)
    def _():
        hn_ref[...] = _modulated_norm(x_ref[...], mod_ref[0], g_ref[...]).astype(BF16)

    o_ref[...] = jnp.dot(hn_ref[...], w_ref[...], preferred_element_type=F32)


def _inproj_call(x2d, mod3, norm_g3, wcat, layer, rows_per_mod):
    t = x2d.shape[0]
    tm, tn = 1024, 512
    if mod3.shape[0] == 1:
        mod_idx = lambda i, j: (0, 0, 0)
    else:
        mod_idx = lambda i, j: ((i * tm) // rows_per_mod, 0, 0)
    return pl.pallas_call(
        _inproj_kernel,
        grid=(t // tm, PROJ_W // tn),
        in_specs=[pl.BlockSpec((tm, D_MODEL), lambda i, j: (i, 0)),
                  pl.BlockSpec((1, 1, 3 * D_MODEL), mod_idx),
                  _layer_spec((1, D_MODEL), layer, 2),
                  pl.BlockSpec((None, D_MODEL, tn), lambda i, j: (layer, 0, j))],
        out_specs=pl.BlockSpec((tm, tn), lambda i, j: (i, j)),
        out_shape=jax.ShapeDtypeStruct((t, PROJ_W), F32),
        scratch_shapes=[pltpu.VMEM((tm, D_MODEL), BF16)],
        compiler_params=_params(40, 2),
        name="inproj",
    )(x2d, mod3, norm_g3, wcat)


def _stacked_heads(q, n_kv):
    rep = N_HEADS // n_kv
    return [jnp.concatenate([q[:, _hs(g * rep + r)] for r in range(rep)], axis=0) for g in range(n_kv)]


def _unstack_heads(outs, n_kv):
    rep = N_HEADS // n_kv
    m = outs[0].shape[0] // rep
    return jnp.concatenate([outs[g][r * m:(r + 1) * m] for g in range(n_kv) for r in range(rep)], axis=-1)


def _write_layer(ref, layer, value, stacked):
    if not stacked:
        ref[...] = value
        return
    for l in range(ref.shape[0]):
        ref[l] = value if l == layer else jnp.zeros(value.shape, value.dtype)


def _write_state(st_ref, layer, stacked, piece):
    for d in range(2):
        for h in range(N_HEADS):
            value = piece(d, h)
            if stacked:
                for l in range(st_ref.shape[0]):
                    st_ref[l, d, h] = value if l == layer else jnp.zeros(value.shape, value.dtype)
            else:
                st_ref[d, h] = value


def _attn_ctx_kernel(*refs, n_kv, norm, layer, first):
    if norm:
        q_ref, k_ref, v_ref, z_ref, qn_ref, kn_ref = refs[:6]
    else:
        q_ref, k_ref, v_ref, z_ref = refs[:4]
    o_ref, kt_ref, vt_ref = refs[-3:]
    q, k, v, z = q_ref[...], k_ref[...], v_ref[...], z_ref[...]
    if norm:
        q = _head_rms(q) * qn_ref[...]
        k = _head_rms(k) * kn_ref[...]
    seq = k.shape[0]
    _write_layer(kt_ref, layer, k.T.reshape(n_kv, HEAD_DIM, seq), first)
    _write_layer(vt_ref, layer, v.T.reshape(n_kv, HEAD_DIM, seq), first)
    outs = _attend(_stacked_heads(q, n_kv), [[(k[:, _hs(g)], v[:, _hs(g)], None, False)] for g in range(n_kv)])
    o_ref[...] = _unstack_heads(outs, n_kv) * _silu(z)


def _attn_ctx_call(proj, prev, layer, batch, seq, off_q, off_k, off_v, off_z, n_kv, qn=None, kn=None):
    t = batch * seq
    kvw = n_kv * HEAD_DIM
    norm = qn is not None
    first = prev is None
    in_specs = [pl.BlockSpec((seq, BRANCH_W), lambda b: (b, off_q // BRANCH_W)),
                pl.BlockSpec((seq, kvw), lambda b: (b, off_k // kvw)),
                pl.BlockSpec((seq, kvw), lambda b: (b, off_v // kvw)),
                pl.BlockSpec((seq, BRANCH_W), lambda b: (b, off_z // BRANCH_W))]
    args = [proj, proj, proj, proj]
    if norm:
        in_specs += [_layer_spec((1, BRANCH_W), layer, 1), _layer_spec((1, kvw), layer, 1)]
        args += [qn, kn]
    aliases = {}
    if first:
        cache_spec = pl.BlockSpec((None, DEPTH, n_kv, HEAD_DIM, seq), lambda b: (b, 0, 0, 0, 0))
    else:
        aliases = {len(args): 1, len(args) + 1: 2}
        in_specs += [pl.BlockSpec(memory_space=pl.ANY)] * 2
        args += list(prev)
        cache_spec = pl.BlockSpec((None, None, n_kv, HEAD_DIM, seq), lambda b: (b, layer, 0, 0, 0))
    cache_shape = jax.ShapeDtypeStruct((batch, DEPTH, n_kv, HEAD_DIM, seq), F32)
    return pl.pallas_call(
        functools.partial(_attn_ctx_kernel, n_kv=n_kv, norm=norm, layer=layer, first=first),
        grid=(batch,), in_specs=in_specs,
        out_specs=[pl.BlockSpec((seq, BRANCH_W), lambda b: (b, 0)), cache_spec, cache_spec],
        out_shape=[jax.ShapeDtypeStruct((t, BRANCH_W), F32), cache_shape, cache_shape],
        input_output_aliases=aliases,
        compiler_params=_params(32, 1),
        name="attn_ctx_norm" if norm else "attn_ctx",
    )(*args)


def _attn_lat_kernel(q_ref, kv_ref, z_ref, ckt_ref, cvt_ref, qtab_ref, ktab_ref, qn_ref, kn_ref, o_ref,
                     k_s, v_s):
    kw = KV_HEADS * HEAD_DIM

    @pl.when(pl.program_id(1) == 0)
    def _():
        kv = kv_ref[...]
        k_s[...] = _rope(_head_rms(kv[:, :kw]) * kn_ref[...], ktab_ref).astype(BF16)
        v_s[...] = kv[:, kw:].astype(BF16)

    q = _rope(_head_rms(q_ref[...]) * qn_ref[...], qtab_ref)
    k, v = k_s[...], v_s[...]
    outs = _attend(_stacked_heads(q, KV_HEADS),
                   [[(k[:, _hs(g)], v[:, _hs(g)], None, False), (ckt_ref[g], cvt_ref[g], None, True)]
                    for g in range(KV_HEADS)])
    o_ref[...] = _unstack_heads(outs, KV_HEADS) * _silu(z_ref[...])


def _attn_lat_call(proj, cache_kt, cache_vt, layer, batch, seq, qtab, ktab, qn, kn):
    tq = 256
    nq = seq // tq
    past = cache_kt.shape[-1]
    kw = KV_HEADS * HEAD_DIM
    ctx_spec = pl.BlockSpec((None, None, KV_HEADS, HEAD_DIM, past), lambda b, i: (b, layer, 0, 0, 0))
    return pl.pallas_call(
        _attn_lat_kernel,
        grid=(batch, nq),
        in_specs=[pl.BlockSpec((tq, BRANCH_W), lambda b, i: (b * nq + i, OFF_AQKV // BRANCH_W)),
                  pl.BlockSpec((seq, 2 * kw), lambda b, i: (b, (OFF_AQKV + BRANCH_W) // (2 * kw))),
                  pl.BlockSpec((tq, BRANCH_W), lambda b, i: (b * nq + i, OFF_AZ // BRANCH_W)),
                  ctx_spec, ctx_spec,
                  pl.BlockSpec((3, tq, BRANCH_W), lambda b, i: (0, i, 0)),
                  pl.BlockSpec((3, seq, kw), lambda b, i: (0, 0, 0)),
                  _layer_spec((1, BRANCH_W), layer, 2),
                  _layer_spec((1, kw), layer, 2)],
        out_specs=pl.BlockSpec((tq, BRANCH_W), lambda b, i: (b * nq + i, 0)),
        out_shape=jax.ShapeDtypeStruct((batch * seq, BRANCH_W), F32),
        scratch_shapes=[pltpu.VMEM((seq, kw), BF16), pltpu.VMEM((seq, kw), BF16)],
        compiler_params=_params(40, 2),
        name="attn_lat",
    )(proj, proj, proj, cache_kt, cache_vt, qtab, ktab, qn, kn)


def _na_bias_kernel(t_ref, o_ref):
    nblk = o_ref.shape[0]
    c = lax.broadcasted_iota(jnp.int32, (GRID_W, 2 * GRID_W), 0)
    j = lax.broadcasted_iota(jnp.int32, (GRID_W, 2 * GRID_W), 1)
    kc = j & (GRID_W - 1)
    dc = kc - c + (NA_COLS - 1)
    cs = jnp.clip(c - NA_COLS // 2, 0, GRID_W - NA_COLS)
    valid = jnp.logical_and(kc >= cs, kc < cs + NA_COLS)
    left = j < GRID_W

    def body(b, carry):
        b2 = jnp.minimum(b + 1, nblk - 1)
        acc = jnp.full((GRID_W, 2 * GRID_W), NEG_INF, F32)
        for i in range(N_DC):
            acc = jnp.where(dc == i, jnp.where(left, t_ref[b * N_DC + i], t_ref[b2 * N_DC + i]), acc)
        o_ref[b] = jnp.where(valid, acc, NEG_INF)
        return carry

    lax.fori_loop(0, nblk, body, 0)


def _na_bias_call(na_bias):
    nblk = DEPTH * N_HEADS * N_DR
    return pl.pallas_call(
        _na_bias_kernel,
        in_specs=[pl.BlockSpec(memory_space=pltpu.SMEM)],
        out_specs=pl.BlockSpec((nblk, GRID_W, 2 * GRID_W), lambda: (0, 0, 0)),
        out_shape=jax.ShapeDtypeStruct((nblk, GRID_W, 2 * GRID_W), F32),
        name="na_bias",
    )(na_bias.reshape(-1))


def _na_kernel(q_ref, k_ref, v_ref, z_ref, ckt_ref, cvt_ref, tb_ref, o_ref, *, rows):
    win = NA_ROWS * GRID_W
    r = pl.program_id(1)
    rs = jnp.clip(r - NA_ROWS // 2, 0, rows - NA_ROWS)
    r0 = pl.multiple_of(rs * GRID_W, GRID_W)
    kwin = k_ref[pl.ds(r0, win), :]
    vwin = v_ref[pl.ds(r0, win), :]
    q = q_ref[...]
    dr0 = rs - r + NA_ROWS - 1
    bias = [jnp.concatenate([tb_ref[h * N_DR + dr0 + 2 * p] for p in range(NA_ROWS // 2)], axis=1)
            for h in range(N_HEADS)]
    outs = _attend([q[:, _hs(h)] for h in range(N_HEADS)],
                   [[(kwin[:, _hs(h)], vwin[:, _hs(h)], bias[h], False), (ckt_ref[h], cvt_ref[h], None, True)]
                    for h in range(N_HEADS)])
    o_ref[...] = jnp.concatenate(outs, axis=-1) * _silu(z_ref[...])


def _na_call(proj, cache_kt, cache_vt, tb, layer, batch, seq):
    rows = seq // GRID_W
    assert rows >= NA_ROWS
    past = cache_kt.shape[-1]
    nblk = N_HEADS * N_DR
    cq = OFF_D // BRANCH_W
    ctx_spec = pl.BlockSpec((None, None, N_HEADS, HEAD_DIM, past), lambda b, r: (b, layer, 0, 0, 0))
    return pl.pallas_call(
        functools.partial(_na_kernel, rows=rows),
        grid=(batch, rows),
        in_specs=[pl.BlockSpec((GRID_W, BRANCH_W), lambda b, r: (b * rows + r, cq)),
                  pl.BlockSpec((seq, BRANCH_W), lambda b, r: (b, cq + 1)),
                  pl.BlockSpec((seq, BRANCH_W), lambda b, r: (b, cq + 2)),
                  pl.BlockSpec((GRID_W, BRANCH_W), lambda b, r: (b * rows + r, cq + 3)),
                  ctx_spec, ctx_spec,
                  pl.BlockSpec((nblk, GRID_W, 2 * GRID_W), lambda b, r: (layer, 0, 0))],
        out_specs=pl.BlockSpec((GRID_W, BRANCH_W), lambda b, r: (b * rows + r, 0)),
        out_shape=jax.ShapeDtypeStruct((batch * seq, BRANCH_W), F32),
        compiler_params=_params(32, 2),
        name="na_lat",
    )(proj, proj, proj, proj, cache_kt, cache_vt, tb)


def _gdn_kernel(*refs, seq, has_s0, layer, emit):
    qkv_ref, z_ref, ab_ref, cw_ref, par_ref, g_ref = refs[:6]
    s0_ref = refs[6] if has_s0 else None
    xp_ref, q_s, k_s, v_s, gate_ref, u_s, w_s, qd_s, qkm_s, kd_s, egl_s, s_s, oacc_ref = refs[-13:]
    if emit == "none":
        o_ref, st_ref = refs[-14], None
    else:
        o_ref, st_ref = refs[-15], refs[-14]
    n_chunks = seq // CHUNK
    n_levels = CHUNK.bit_length() - 1
    qkv_w = 3 * BRANCH_W
    pad = 8
    half = SHORT_CONV // 2
    head_sum = _head_block_matrix(BRANCH_W, 1.0)

    xp_ref[0:pad, :] = jnp.zeros((pad, qkv_w), F32)
    xp_ref[seq + pad:seq + 2 * pad, :] = jnp.zeros((pad, qkv_w), F32)
    xp_ref[pad:seq + pad, :] = qkv_ref[...]
    tr = 256
    for t in range(seq // tr):
        base = pad + t * tr - half
        y = xp_ref[base:base + tr, :] * cw_ref[0:1, :]
        for j in range(1, SHORT_CONV):
            y = y + xp_ref[base + j:base + j + tr, :] * cw_ref[j:j + 1, :]
        y = _silu(y)
        rows = slice(t * tr, (t + 1) * tr)
        qq, kk = y[:, :BRANCH_W], y[:, BRANCH_W:2 * BRANCH_W]
        q_s[rows, :] = qq * lax.rsqrt(_head_reduce(qq * qq, head_sum) + EPS) * SCALE
        k_s[rows, :] = kk * lax.rsqrt(_head_reduce(kk * kk, head_sum) + EPS)
        v_s[rows, :] = y[:, 2 * BRANCH_W:]

    x = ab_ref[...]
    gate_ref[0] = jax.nn.sigmoid(x)
    xs = x + par_ref[0:1, :]
    softplus = jnp.maximum(xs, 0.0) + jnp.log1p(jnp.exp(-jnp.abs(xs)))
    gate_ref[1] = -jnp.exp(par_ref[1:2, :]) * softplus

    for d in range(2):
        if has_s0:
            s_s[d] = jnp.concatenate([s0_ref[d, h] for h in range(N_HEADS)], axis=-1)
        else:
            s_s[d] = jnp.zeros((HEAD_DIM, BRANCH_W), F32)

    li = lax.broadcasted_iota(jnp.int32, (CHUNK, BRANCH_W), 0)
    lj = lax.broadcasted_iota(jnp.int32, (CHUNK, BRANCH_W), 1) & (HEAD_DIM - 1)
    incl = (li >= lj, li <= lj)
    strict = (li > lj, li < lj)
    level = [((li ^ lj) >> l) == 1 for l in range(n_levels)]
    ti = lax.broadcasted_iota(jnp.int32, (CHUNK, CHUNK), 0)
    tj = lax.broadcasted_iota(jnp.int32, (CHUNK, CHUNK), 1)
    tri = (jnp.where(ti >= tj, 1.0, 0.0).astype(BF16), jnp.where(ti <= tj, 1.0, 0.0).astype(BF16))
    bi = lax.broadcasted_iota(jnp.int32, (BRANCH_W, BRANCH_W), 0) >> 6
    bj = lax.broadcasted_iota(jnp.int32, (BRANCH_W, BRANCH_W), 1) >> 6
    same_head = bi == bj
    gc_i = lax.broadcasted_iota(jnp.int32, (128, BRANCH_W), 0)
    gh_j = lax.broadcasted_iota(jnp.int32, (128, BRANCH_W), 1) >> 6
    sel_beta = [jnp.where(gc_i == gh_j + 4 * d, 1.0, 0.0).astype(BF16) for d in range(2)]
    sel_gate = [jnp.where(gc_i == gh_j + 8 + 4 * d, 1.0, 0.0).astype(BF16) for d in range(2)]
    rh_i = lax.broadcasted_iota(jnp.int32, (BRANCH_W, 128), 0) >> 6
    rc_j = lax.broadcasted_iota(jnp.int32, (BRANCH_W, 128), 1)
    row_gate = [rc_j == rh_i + 8 + 4 * d for d in range(2)]
    ones_l = jnp.ones((CHUNK, 128), BF16)
    lane_head = lax.broadcasted_iota(jnp.int32, (CHUNK, BRANCH_W), 1) >> 6

    def block_diag(y):
        yb = y.astype(BF16)
        return jnp.where(same_head, jnp.concatenate([yb] * N_HEADS, axis=0), jnp.zeros((), BF16))

    def bdmm(x, ybd):
        return jnp.dot(x.astype(BF16), ybd, preferred_element_type=F32)

    def prepare(chains):
        n = range(len(chains))
        dd = [d for d, _ in chains]
        rows = [pl.ds(_aligned(c * CHUNK, CHUNK), CHUNK) for _, c in chains]
        beta = [gate_ref[0, rows[i], :] for i in n]
        la = [gate_ref[1, rows[i], :] for i in n]
        gc = [_mm_exact(tri[dd[i]], la[i]) for i in n]
        gcb = [_mm_exact_lhs(gc[i], sel_gate[dd[i]]) for i in n]
        bcb = [_mm_exact_lhs(beta[i], sel_beta[dd[i]]) for i in n]
        grow = [_mm_nt_exact(ones_l, jnp.where(row_gate[dd[i]], jnp.concatenate([gc[i]] * N_HEADS, axis=0), 0.0))
                for i in n]
        dm = [jnp.exp(jnp.where(incl[dd[i]], gcb[i] - grow[i], NEG_INF)) for i in n]
        k = [k_s[rows[i], :] for i in n]
        q = [q_s[rows[i], :] for i in n]
        v = [v_s[rows[i], :] for i in n]
        kq = [_mm_nt(jnp.concatenate([k[i], q[i]], axis=0), block_diag(k[i])) for i in n]
        a = [jnp.where(strict[dd[i]], bcb[i] * kq[i][:CHUNK] * dm[i], 0.0) for i in n]
        tm = [-jnp.where(level[0], a[i], 0.0) for i in n]
        for l in range(1, n_levels):
            b = [jnp.where(level[l], a[i], 0.0) for i in n]
            y = [b[i] + bdmm(tm[i], block_diag(b[i])) for i in n]
            tm = [tm[i] - (y[i] + bdmm(y[i], block_diag(tm[i]))) for i in n]
        eg = [jnp.exp(gcb[i]) for i in n]
        bv = [bcb[i] * v[i] for i in n]
        bk = [bcb[i] * k[i] * eg[i] for i in n]
        u = [bv[i] + bdmm(tm[i], block_diag(bv[i])) for i in n]
        w = [bk[i] + bdmm(tm[i], block_diag(bk[i])) for i in n]
        for i in n:
            d, c = chains[i]
            gl = gcb[i][CHUNK - 1:CHUNK, :] if d == 0 else gcb[i][0:1, :]
            u_s[d, rows[i], :] = u[i]
            w_s[d, rows[i], :] = w[i].astype(BF16)
            qd_s[d, rows[i], :] = (q[i] * eg[i]).astype(BF16)
            qkm_s[d, rows[i], :] = (kq[i][CHUNK:] * dm[i]).astype(BF16)
            kd_s[d, rows[i], :] = (k[i] * jnp.exp(gl - gcb[i])).astype(BF16)
            egl_s[d, pl.ds(_aligned(c * 8, 8), 8), :] = jnp.broadcast_to(jnp.exp(gl), (8, BRANCH_W))

    group = min(PREP_CHUNKS, n_chunks)
    n_groups = n_chunks // group
    if n_groups == 1:
        prepare([(d, c) for c in range(group) for d in range(2)])
    else:
        def prep_body(j, carry):
            prepare([(d, j * group + c) for c in range(group) for d in range(2)])
            return carry
        lax.fori_loop(0, n_groups, prep_body, 0)

    def scan_body(i, carry):
        dirs = range(2)
        cidx = [i, n_chunks - 1 - i]
        rows = [pl.ds(pl.multiple_of(cidx[d] * CHUNK, CHUNK), CHUNK) for d in dirs]
        s = [s_s[d] for d in dirs]
        sbd = [block_diag(s[d]) for d in dirs]
        vnew = [u_s[d, rows[d], :] - bdmm(w_s[d, rows[d], :], sbd[d]) for d in dirs]
        o = [bdmm(qd_s[d, rows[d], :], sbd[d]) + bdmm(qkm_s[d, rows[d], :], block_diag(vnew[d])) for d in dirs]
        full = [_mm_tn(kd_s[d, rows[d], :], vnew[d]) for d in dirs]
        for d in dirs:
            upd = jnp.zeros((CHUNK, BRANCH_W), F32)
            for h in range(N_HEADS):
                upd = jnp.where(lane_head == h, full[d][h * HEAD_DIM:(h + 1) * HEAD_DIM, :], upd)
            egl = egl_s[d, pl.ds(pl.multiple_of(cidx[d] * 8, 8), 8), :][0:1, :]
            s_s[d] = s[d] * egl + upd
            oacc_ref[d, rows[d], :] = o[d]
        return carry

    lax.fori_loop(0, n_chunks, scan_body, 0)

    if st_ref is not None:
        _write_state(st_ref, layer, emit == "first", lambda d, h: s_s[d][:, _hs(h)])
    o = oacc_ref[0] + oacc_ref[1]
    ms = _head_reduce(o * o, _head_block_matrix(BRANCH_W, 1.0 / HEAD_DIM))
    o_ref[...] = o * lax.rsqrt(ms + EPS) * g_ref[...] * _silu(z_ref[...])


def _state_spec(layer):
    return pl.BlockSpec((None, None, 2, N_HEADS, HEAD_DIM, HEAD_DIM), lambda b: (b, layer, 0, 0, 0, 0))


def _state_output(emit, prev, layer, batch, n_args):
    if emit == "none":
        return [], [], [], [], {}
    shape = jax.ShapeDtypeStruct((batch, DEPTH, 2, N_HEADS, HEAD_DIM, HEAD_DIM), F32)
    if emit == "first":
        spec = pl.BlockSpec((None, DEPTH, 2, N_HEADS, HEAD_DIM, HEAD_DIM), lambda b: (b, 0, 0, 0, 0, 0))
        return [], [], [spec], [shape], {}
    return [pl.BlockSpec(memory_space=pl.ANY)], [prev], [_state_spec(layer)], [shape], {n_args: 1}


def _gdn_call(proj, cw8, par, norm_g, layer, batch, seq, s0=None, emit="none", prev=None):
    has_s0 = s0 is not None
    qkv_w = 3 * BRANCH_W
    in_specs = [pl.BlockSpec((seq, qkv_w), lambda b: (b, OFF_GQKV // qkv_w)),
                pl.BlockSpec((seq, BRANCH_W), lambda b: (b, OFF_GZ // BRANCH_W)),
                pl.BlockSpec((seq, 128), lambda b: (b, OFF_GAB // 128)),
                _layer_spec((8, qkv_w), layer, 1),
                _layer_spec((2, 128), layer, 1),
                _layer_spec((1, BRANCH_W), layer, 1)]
    args = [proj, proj, proj, cw8, par, norm_g]
    if has_s0:
        in_specs.append(_state_spec(layer))
        args.append(s0)
    st_in_specs, st_args, st_out_specs, st_shapes, aliases = _state_output(emit, prev, layer, batch, len(args))
    return pl.pallas_call(
        functools.partial(_gdn_kernel, seq=seq, has_s0=has_s0, layer=layer, emit=emit),
        grid=(batch,), in_specs=in_specs + st_in_specs,
        out_specs=[pl.BlockSpec((seq, BRANCH_W), lambda b: (b, 0))] + st_out_specs,
        out_shape=[jax.ShapeDtypeStruct((batch * seq, BRANCH_W), F32)] + st_shapes,
        input_output_aliases=aliases,
        scratch_shapes=[pltpu.VMEM((seq + 16, qkv_w), F32),
                        pltpu.VMEM((seq, BRANCH_W), F32),
                        pltpu.VMEM((seq, BRANCH_W), F32),
                        pltpu.VMEM((seq, BRANCH_W), F32),
                        pltpu.VMEM((2, seq, 128), F32),
                        pltpu.VMEM((2, seq, BRANCH_W), F32),
                        pltpu.VMEM((2, seq, BRANCH_W), BF16),
                        pltpu.VMEM((2, seq, BRANCH_W), BF16),
                        pltpu.VMEM((2, seq, BRANCH_W), BF16),
                        pltpu.VMEM((2, seq, BRANCH_W), BF16),
                        pltpu.VMEM((2, seq // CHUNK * 8, BRANCH_W), F32),
                        pltpu.VMEM((2, HEAD_DIM, BRANCH_W), F32),
                        pltpu.VMEM((2, seq, BRANCH_W), F32)],
        compiler_params=_params(48, 1),
        name="gdn",
    )(*args, *st_args)


_RET_LOG_GAMMA = [[float(np.log1p(-np.exp2(-(base + h)))) for h in range(N_HEADS)] for base in RET_DECAY_BASE]


def _ret_kernel(*refs, seq, has_s0, layer, emit):
    qkv_ref, z_ref, g_ref = refs[:3]
    s0_ref = refs[3] if has_s0 else None
    o_ref, st_ref = (refs[-1], None) if emit == "none" else (refs[-2], refs[-1])
    tq = 256
    heads = range(N_HEADS)
    lgf, lgb = _RET_LOG_GAMMA
    ks = [qkv_ref[:, BRANCH_W + h * HEAD_DIM:BRANCH_W + (h + 1) * HEAD_DIM] for h in heads]
    vs = [qkv_ref[:, 2 * BRANCH_W + h * HEAD_DIM:2 * BRANCH_W + (h + 1) * HEAD_DIM] for h in heads]
    if st_ref is not None:
        tcol = lax.broadcasted_iota(jnp.int32, (seq, 1), 0).astype(F32)
        stf = [_mm_tn(ks[h] * jnp.exp((seq - 1.0 - tcol) * lgf[h]), vs[h]) for h in heads]
        stb = [_mm_tn(ks[h] * jnp.exp(tcol * lgb[h]), vs[h]) for h in heads]
        if has_s0:
            stf = [stf[h] + float(np.exp(seq * lgf[h])) * s0_ref[0, h] for h in heads]
            stb = [stb[h] + float(np.exp(seq * lgb[h])) * s0_ref[1, h] for h in heads]
        _write_state(st_ref, layer, emit == "first", lambda d, h: (stf, stb)[d][h])
    tiles = []
    for t in range(seq // tq):
        qs = [qkv_ref[t * tq:(t + 1) * tq, _hs(h)] * SCALE for h in heads]
        qk = [_mm_nt(qs[h], ks[h]) for h in heads]
        di = (lax.broadcasted_iota(jnp.int32, (tq, seq), 0) + t * tq
              - lax.broadcasted_iota(jnp.int32, (tq, seq), 1)).astype(F32)
        dm = [jnp.where(di == 0, 2.0, jnp.exp(di * jnp.where(di >= 0, lgf[h], -lgb[h]))) for h in heads]
        o = [_mm(qk[h] * dm[h], vs[h]) for h in heads]
        if has_s0:
            tt = lax.broadcasted_iota(jnp.int32, (tq, 1), 0).astype(F32) + float(t * tq)
            o = [o[h] + _mm(qs[h] * jnp.exp((tt + 1.0) * lgf[h]), s0_ref[0, h])
                 + _mm(qs[h] * jnp.exp((seq - tt) * lgb[h]), s0_ref[1, h]) for h in heads]
        ms = [jnp.mean(o[h] * o[h], axis=-1, keepdims=True) for h in heads]
        tiles.append(jnp.concatenate([o[h] * lax.rsqrt(ms[h] + EPS) * g_ref[...] for h in heads], axis=-1))
    o_ref[...] = jnp.concatenate(tiles, axis=0) * _silu(z_ref[...])


def _ret_call(proj, norm_g, layer, batch, seq, s0=None, emit="none", prev=None):
    has_s0 = s0 is not None
    qkv_w = 3 * BRANCH_W
    in_specs = [pl.BlockSpec((seq, qkv_w), lambda b: (b, OFF_C // qkv_w)),
                pl.BlockSpec((seq, BRANCH_W), lambda b: (b, (OFF_C + qkv_w) // BRANCH_W)),
                _layer_spec((1, HEAD_DIM), layer, 1)]
    args = [proj, proj, norm_g]
    if has_s0:
        in_specs.append(_state_spec(layer))
        args.append(s0)
    st_in_specs, st_args, st_out_specs, st_shapes, aliases = _state_output(emit, prev, layer, batch, len(args))
    return pl.pallas_call(
        functools.partial(_ret_kernel, seq=seq, has_s0=has_s0, layer=layer, emit=emit),
        grid=(batch,), in_specs=in_specs + st_in_specs,
        out_specs=[pl.BlockSpec((seq, BRANCH_W), lambda b: (b, 0))] + st_out_specs,
        out_shape=[jax.ShapeDtypeStruct((batch * seq, BRANCH_W), F32)] + st_shapes,
        input_output_aliases=aliases,
        compiler_params=_params(48, 1),
        name="retention",
    )(*args, *st_args)


def _out_kernel(*refs, final):
    if final:
        (h_ref, mod_ref, g_ref, oa_ref, ob_ref, oc_ref, od_ref, wg_ref, wb_ref, wo_ref, fn_ref,
         o_ref, y_ref) = refs
    else:
        h_ref, mod_ref, g_ref, oa_ref, ob_ref, oc_ref, od_ref, wg_ref, wb_ref, wo_ref, o_ref = refs
    x = h_ref[...]
    mod = mod_ref[0]
    hn = _modulated_norm(x, mod, g_ref[...]).astype(BF16)
    merged = None
    for n, br_ref in enumerate((oa_ref, ob_ref, oc_ref, od_ref)):
        gate = jax.nn.sigmoid(jnp.dot(hn, wg_ref[:, n * D_MODEL:(n + 1) * D_MODEL], preferred_element_type=F32))
        up = jnp.dot(br_ref[...].astype(BF16), wb_ref[n], preferred_element_type=F32)
        merged = gate * up if merged is None else merged + gate * up
    out = jnp.dot(merged.astype(BF16), wo_ref[...], preferred_element_type=F32)
    hnew = x + mod[:, 2 * D_MODEL:] * out
    o_ref[...] = hnew
    if final:
        ms = jnp.mean(hnew * hnew, axis=-1, keepdims=True)
        y_ref[...] = hnew * lax.rsqrt(ms + EPS) * fn_ref[...]


def _out_call(h2d, mod3, norm_g3, branches, wg, wb, wo, layer, rows_per_mod, final_norm=None):
    t = h2d.shape[0]
    tm = 512
    final = final_norm is not None
    if mod3.shape[0] == 1:
        mod_idx = lambda i: (0, 0, 0)
    else:
        mod_idx = lambda i: ((i * tm) // rows_per_mod, 0, 0)
    once = pl.Buffered(1)
    in_specs = [pl.BlockSpec((tm, D_MODEL), lambda i: (i, 0)),
                pl.BlockSpec((1, 1, 3 * D_MODEL), mod_idx),
                _layer_spec((1, D_MODEL), layer, 1)]
    in_specs += [pl.BlockSpec((tm, BRANCH_W), lambda i: (i, 0))] * N_BRANCH
    in_specs += [pl.BlockSpec((None, D_MODEL, N_BRANCH * D_MODEL), lambda i: (layer, 0, 0), pipeline_mode=once),
                 pl.BlockSpec((None, N_BRANCH, BRANCH_W, D_MODEL), lambda i: (layer, 0, 0, 0), pipeline_mode=once),
                 pl.BlockSpec((None, D_MODEL, D_MODEL), lambda i: (layer, 0, 0), pipeline_mode=once)]
    args = [h2d, mod3, norm_g3, *branches, wg, wb, wo]
    out_specs = [pl.BlockSpec((tm, D_MODEL), lambda i: (i, 0))]
    out_shape = [jax.ShapeDtypeStruct((t, D_MODEL), F32)]
    if final:
        in_specs.append(pl.BlockSpec((1, D_MODEL), lambda i: (0, 0)))
        args.append(final_norm.reshape(1, D_MODEL))
        out_specs.append(pl.BlockSpec((tm, D_MODEL), lambda i: (i, 0)))
        out_shape.append(jax.ShapeDtypeStruct((t, D_MODEL), F32))
    return pl.pallas_call(
        functools.partial(_out_kernel, final=final),
        grid=(t // tm,), in_specs=in_specs, out_specs=out_specs, out_shape=out_shape,
        compiler_params=_params(48, 1),
        name="merge_out_final" if final else "merge_out",
    )(*args)


def _prep_weights(w_in):
    offs = np.concatenate([[0], np.cumsum(IN_SPLITS)])
    seg = lambda i, j: w_in[:, :, offs[i]:offs[j]]
    pad = jnp.zeros((DEPTH, D_MODEL, PROJ_W - OFF_GAB - IN_SPLITS[5]), w_in.dtype)
    wcat = jnp.concatenate([seg(7, 11), seg(11, 15), seg(6, 7), seg(4, 5), seg(0, 3), seg(3, 4), seg(5, 6), pad],
                           axis=2).astype(BF16)
    return wcat, seg(15, 16).astype(BF16)


def _rope_tables(seq):
    t = jnp.arange(seq)
    quarter = HEAD_DIM // 4
    inv = ROPE_THETA ** (-jnp.arange(quarter, dtype=F32) / quarter)

    def half(pos):
        ang = pos.astype(F32)[:, None] * inv
        c, s, zero = jnp.cos(ang), jnp.sin(ang), jnp.zeros_like(ang)
        return jnp.concatenate([c, c], -1), jnp.concatenate([-s, zero], -1), jnp.concatenate([zero, s], -1)

    parts = [jnp.concatenate([a, b], -1) for a, b in zip(half(t // GRID_W), half(t % GRID_W))]
    tab = jnp.stack(parts)
    return jnp.tile(tab, (1, 1, N_HEADS)), jnp.tile(tab, (1, 1, KV_HEADS))


def _layer(h2d, batch, seq, mod, pw, layer, ctx, caches, final_norm):
    proj = _inproj_call(h2d, mod, pw["norm_g"], pw["wcat"], layer, seq)
    kw = KV_HEADS * HEAD_DIM
    if ctx is None:
        emit = "first" if caches is None else "update"
        akv, nkv, sg_all, sr_all = caches or (None, None, None, None)
        oa, *akv = _attn_ctx_call(proj, akv, layer, batch, seq, OFF_AQKV, OFF_AQKV + BRANCH_W,
                                  OFF_AQKV + BRANCH_W + kw, OFF_AZ, KV_HEADS, pw["qn"], pw["kn"])
        od, *nkv = _attn_ctx_call(proj, nkv, layer, batch, seq, OFF_D, OFF_D + BRANCH_W,
                                  OFF_D + 2 * BRANCH_W, OFF_D + 3 * BRANCH_W, N_HEADS)
        ob, sg_all = _gdn_call(proj, pw["cw8"], pw["gdn_par"], pw["gdn_norm"], layer, batch, seq,
                               emit=emit, prev=sg_all)
        oc, sr_all = _ret_call(proj, pw["ret_norm"], layer, batch, seq, emit=emit, prev=sr_all)
        caches = (akv, nkv, sg_all, sr_all)
    else:
        oa = _attn_lat_call(proj, ctx["akt"], ctx["avt"], layer, batch, seq, ctx["qtab"], ctx["ktab"],
                            pw["qn"], pw["kn"])
        od = _na_call(proj, ctx["nkt"], ctx["nvt"], ctx["tb"], layer, batch, seq)
        ob, = _gdn_call(proj, pw["cw8"], pw["gdn_par"], pw["gdn_norm"], layer, batch, seq, s0=ctx["sg"])
        oc, = _ret_call(proj, pw["ret_norm"], layer, batch, seq, s0=ctx["sr"])
    outs = _out_call(h2d, mod, pw["norm_g"], (oa, ob, oc, od), pw["wg"], pw["wb"], pw["wo"], layer, seq, final_norm)
    return outs, caches


def kernel(x_prompt, x_sample, cache_attn_k, cache_attn_v, cache_na_k, cache_na_v, state_gdn, state_ret, c, c_ctx, w_ada, b_ada, norm_g, w_in, conv_w, gdn_a_log, gdn_dt_bias, gdn_norm, attn_q_norm, attn_k_norm, ret_norm, na_bias, w_branch, w_out, final_norm):
    batch, seq, _ = x_prompt.shape
    dbatch, dseq, _ = x_sample.shape
    assert dbatch == 8, "the modulation kernel handles exactly one sublane tile of conditioning rows"

    wcat, wg = _prep_weights(w_in)
    par = jnp.zeros((DEPTH, 2, 128), F32)
    par = par.at[:, 0, 8:16].set(gdn_dt_bias.reshape(DEPTH, 8)).at[:, 1, 8:16].set(gdn_a_log.reshape(DEPTH, 8))
    pw = dict(
        w_ada=w_ada, b_ada=b_ada.reshape(DEPTH, 1, 3 * D_MODEL), norm_g=norm_g.reshape(DEPTH, 1, D_MODEL),
        wcat=wcat, wg=wg, wb=w_branch.astype(BF16), wo=w_out.astype(BF16),
        cw8=jnp.concatenate([conv_w, jnp.zeros((DEPTH, 8 - SHORT_CONV, 3 * BRANCH_W), F32)], axis=1),
        gdn_par=par,
        gdn_norm=jnp.tile(gdn_norm, (1, N_HEADS)).reshape(DEPTH, 1, BRANCH_W),
        ret_norm=ret_norm.reshape(DEPTH, 1, HEAD_DIM),
        qn=jnp.tile(attn_q_norm, (1, N_HEADS)).reshape(DEPTH, 1, BRANCH_W),
        kn=jnp.tile(attn_k_norm, (1, KV_HEADS)).reshape(DEPTH, 1, KV_HEADS * HEAD_DIM))

    cond = jnp.concatenate([jnp.broadcast_to(c_ctx, (8, D_MODEL)), c], axis=0)
    mods = _mod_call(cond, w_ada, pw["b_ada"])

    h = x_prompt.reshape(batch * seq, D_MODEL)
    caches = None
    for l in range(DEPTH):
        outs, caches = _layer(h, batch, seq, mods[l, 0:1].reshape(1, 1, 3 * D_MODEL), pw, l, None, caches,
                              final_norm if l == DEPTH - 1 else None)
        h = outs[0]
    y_prompt = outs[1].reshape(batch, seq, D_MODEL)
    token_major = lambda a: a.transpose(0, 1, 4, 2, 3)
    (akt, avt), (nkt, nvt), new_state_gdn, new_state_ret = caches
    new_attn_k, new_attn_v, new_na_k, new_na_v = (token_major(a) for a in (akt, avt, nkt, nvt))

    qtab, ktab = _rope_tables(dseq)
    feature_major = lambda a: a.transpose(0, 1, 3, 4, 2)
    ctx = dict(akt=feature_major(cache_attn_k), avt=feature_major(cache_attn_v),
               nkt=feature_major(cache_na_k), nvt=feature_major(cache_na_v),
               sg=state_gdn, sr=state_ret, tb=_na_bias_call(na_bias), qtab=qtab, ktab=ktab)
    h = x_sample.reshape(dbatch * dseq, D_MODEL)
    for l in range(DEPTH):
        outs, _ = _layer(h, dbatch, dseq, mods[l, 8:16].reshape(dbatch, 1, 3 * D_MODEL), pw, l, ctx, None,
                         final_norm if l == DEPTH - 1 else None)
        h = outs[0]
    y_sample = outs[1].reshape(dbatch, dseq, D_MODEL)
    return (y_prompt, y_sample, new_attn_k, new_attn_v, new_na_k, new_na_v, new_state_gdn, new_state_ret)
```

```python
import functools

import numpy as np
import jax
import jax.numpy as jnp
from jax import lax
from jax.experimental import pallas as pl
from jax.experimental.pallas import tpu as pltpu

F32 = jnp.float32
BF16 = jnp.bfloat16

D_MODEL = 1024
HEAD_DIM = 64
N_HEADS = 4
KV_HEADS = N_HEADS // 2
BRANCH_W = N_HEADS * HEAD_DIM
N_BRANCH = 4
DEPTH = 2
GRID_W = 64
CHUNK = 64
PREP_CHUNKS = 4
SHORT_CONV = 5
NA_ROWS = 8
NA_COLS = 16
N_DR = 2 * NA_ROWS - 1
N_DC = 2 * NA_COLS - 1
ROPE_THETA = 10000.0
RET_DECAY_BASE = (5.0, 5.5)
EPS = 1e-6
SCALE = HEAD_DIM ** -0.5
NEG_INF = float("-inf")

IN_SPLITS = (256, 128, 128, 256, 768, 16, 256, 256, 256, 256, 256, 256, 256, 256, 256, 4096)
PROJ_W = 4096
OFF_C = 0
OFF_D = 1024
OFF_GZ = 2048
OFF_GQKV = 2304
OFF_AQKV = 3072
OFF_AZ = 3584
OFF_GAB = 3840

V7X_VMEM_BYTES = 64 * 1024 * 1024
MIB = 1024 * 1024


def _params(vmem_mib, n_axes):
    assert vmem_mib * MIB < V7X_VMEM_BYTES
    return pltpu.CompilerParams(dimension_semantics=("arbitrary",) * n_axes,
                                vmem_limit_bytes=vmem_mib * MIB)


def _layer_spec(block, layer, n_grid):
    zeros = (0,) * len(block)
    if n_grid == 1:
        return pl.BlockSpec((None,) + block, lambda i: (layer,) + zeros)
    return pl.BlockSpec((None,) + block, lambda i, j: (layer,) + zeros)


def _mm(a, b):
    return jnp.dot(a.astype(BF16), b.astype(BF16), preferred_element_type=F32)


def _mm_nt(a, b):
    return lax.dot_general(a.astype(BF16), b.astype(BF16), (((1,), (1,)), ((), ())),
                           preferred_element_type=F32)


def _mm_tn(a, b):
    return lax.dot_general(a.astype(BF16), b.astype(BF16), (((0,), (0,)), ((), ())),
                           preferred_element_type=F32)


def _split3(x):
    hi = x.astype(BF16)
    r = x - hi.astype(F32)
    mid = r.astype(BF16)
    lo = (r - mid.astype(F32)).astype(BF16)
    return hi, mid, lo


def _mm_exact(sel, x):
    hi, mid, lo = _split3(x)
    return (jnp.dot(sel, hi, preferred_element_type=F32) + jnp.dot(sel, mid, preferred_element_type=F32)
            + jnp.dot(sel, lo, preferred_element_type=F32))


def _mm_exact_lhs(x, sel):
    hi, mid, lo = _split3(x)
    return (jnp.dot(hi, sel, preferred_element_type=F32) + jnp.dot(mid, sel, preferred_element_type=F32)
            + jnp.dot(lo, sel, preferred_element_type=F32))


def _mm_nt_exact(sel, x):
    dn = (((1,), (1,)), ((), ()))
    hi, mid, lo = _split3(x)
    return (lax.dot_general(sel, hi, dn, preferred_element_type=F32)
            + lax.dot_general(sel, mid, dn, preferred_element_type=F32)
            + lax.dot_general(sel, lo, dn, preferred_element_type=F32))


def _silu(x):
    return x * jax.nn.sigmoid(x)


def _head_block_matrix(width, value):
    ri = lax.broadcasted_iota(jnp.int32, (width, width), 0) >> 6
    ci = lax.broadcasted_iota(jnp.int32, (width, width), 1) >> 6
    return jnp.where(ri == ci, value, 0.0).astype(BF16)


def _head_reduce(x, g):
    hi = x.astype(BF16)
    lo = (x - hi.astype(F32)).astype(BF16)
    return jnp.dot(hi, g, preferred_element_type=F32) + jnp.dot(lo, g, preferred_element_type=F32)


def _head_rms(x):
    ms = _head_reduce(x * x, _head_block_matrix(x.shape[1], 1.0 / HEAD_DIM))
    return x * lax.rsqrt(ms + EPS)


def _rope(x, tab_ref):
    w = x.shape[1]
    return (x * tab_ref[0] + pltpu.roll(x, w - 16, 1) * tab_ref[1] + pltpu.roll(x, 16, 1) * tab_ref[2])


def _attend(qs, parts):
    groups = range(len(qs))

    def score(q, part):
        k, _, bias, feature_major = part
        s = (_mm(q, k) if feature_major else _mm_nt(q, k)) * SCALE
        return s if bias is None else s + bias

    scores = [[score(qs[g], part) for part in parts[g]] for g in groups]
    m = [functools.reduce(jnp.maximum, [s.max(axis=-1, keepdims=True) for s in scores[g]]) for g in groups]
    p = [[jnp.exp(s - m[g]) for s in scores[g]] for g in groups]
    den = [sum(x.sum(axis=-1, keepdims=True) for x in p[g]) for g in groups]
    out = [sum(_mm_nt(x, part[1]) if part[3] else _mm(x, part[1]) for x, part in zip(p[g], parts[g]))
           for g in groups]
    return [out[g] / den[g] for g in groups]


def _hs(h):
    return slice(h * HEAD_DIM, (h + 1) * HEAD_DIM)


def _aligned(x, m):
    return x if isinstance(x, int) else pl.multiple_of(x, m)


def _mod_kernel(c_ref, w_ref, b_ref, o_ref):
    o_ref[...] = _mm(_silu(c_ref[...]), w_ref[...]) + b_ref[...]


def _mod_call(cond, w_ada, b_ada3):
    tn = 512
    rows = cond.shape[0]
    return pl.pallas_call(
        _mod_kernel,
        grid=(DEPTH, 3 * D_MODEL // tn),
        in_specs=[pl.BlockSpec((rows, D_MODEL), lambda l, j: (0, 0)),
                  pl.BlockSpec((None, D_MODEL, tn), lambda l, j: (l, 0, j)),
                  pl.BlockSpec((None, 1, tn), lambda l, j: (l, 0, j))],
        out_specs=pl.BlockSpec((None, rows, tn), lambda l, j: (l, 0, j)),
        out_shape=jax.ShapeDtypeStruct((DEPTH, rows, 3 * D_MODEL), F32),
        compiler_params=_params(24, 2),
        name="adaln_mod",
    )(cond, w_ada, b_ada3)


def _modulated_norm(x, mod, g):
    ms = jnp.mean(x * x, axis=-1, keepdims=True)
    y = x * lax.rsqrt(ms + EPS) * g
    return y * (1.0 + mod[:, D_MODEL:2 * D_MODEL]) + mod[:, :D_MODEL]


def _inproj_kernel(x_ref, mod_ref, g_ref, w_ref, o_ref, ab_ref):
    hn = _modulated_norm(x_ref[...], mod_ref[0], g_ref[...]).astype(BF16)
    tn = 512
    for j in range(PROJ_W // tn):
        y = jnp.dot(hn, w_ref[:, j * tn:(j + 1) * tn], preferred_element_type=F32)
        o_ref[:, j * tn:(j + 1) * tn] = y.astype(BF16)
        if j == OFF_GAB // tn:
            ab_ref[...] = y[:, OFF_GAB % tn:OFF_GAB % tn + 128]


def _inproj_call(x2d, mod3, norm_g3, wcat, layer, rows_per_mod):
    t = x2d.shape[0]
    tm = 512
    if mod3.shape[0] == 1:
        mod_idx = lambda i: (0, 0, 0)
    else:
        mod_idx = lambda i: ((i * tm) // rows_per_mod, 0, 0)
    return pl.pallas_call(
        _inproj_kernel,
        grid=(t // tm,),
        in_specs=[pl.BlockSpec((tm, D_MODEL), lambda i: (i, 0)),
                  pl.BlockSpec((1, 1, 3 * D_MODEL), mod_idx),
                  _layer_spec((1, D_MODEL), layer, 1),
                  pl.BlockSpec((None, D_MODEL, PROJ_W), lambda i: (layer, 0, 0), pipeline_mode=pl.Buffered(1))],
        out_specs=[pl.BlockSpec((tm, PROJ_W), lambda i: (i, 0)), pl.BlockSpec((tm, 128), lambda i: (i, 0))],
        out_shape=[jax.ShapeDtypeStruct((t, PROJ_W), BF16), jax.ShapeDtypeStruct((t, 128), F32)],
        compiler_params=_params(40, 1),
        name="inproj",
    )(x2d, mod3, norm_g3, wcat)


def _stacked_heads(q, n_kv):
    rep = N_HEADS // n_kv
    return [jnp.concatenate([q[:, _hs(g * rep + r)] for r in range(rep)], axis=0) for g in range(n_kv)]


def _unstack_heads(outs, n_kv):
    rep = N_HEADS // n_kv
    m = outs[0].shape[0] // rep
    return jnp.concatenate([outs[g][r * m:(r + 1) * m] for g in range(n_kv) for r in range(rep)], axis=-1)


def _write_layer(ref, layer, value, stacked):
    if not stacked:
        ref[...] = value
        return
    for l in range(ref.shape[0]):
        ref[l] = value if l == layer else jnp.zeros(value.shape, value.dtype)


def _write_state(st_ref, layer, stacked, piece):
    for d in range(2):
        for h in range(N_HEADS):
            value = piece(d, h)
            if stacked:
                for l in range(st_ref.shape[0]):
                    st_ref[l, d, h] = value if l == layer else jnp.zeros(value.shape, value.dtype)
            else:
                st_ref[d, h] = value


def _attn_ctx_kernel(*refs, n_kv, norm, layer, first):
    if norm:
        q_ref, k_ref, v_ref, z_ref, qn_ref, kn_ref = refs[:6]
    else:
        q_ref, k_ref, v_ref, z_ref = refs[:4]
    o_ref, kt_ref, vt_ref = refs[-3:]
    q, k, v, z = q_ref[...], k_ref[...].astype(F32), v_ref[...].astype(F32), z_ref[...].astype(F32)
    if norm:
        q = _head_rms(q.astype(F32)) * qn_ref[...]
        k = _head_rms(k) * kn_ref[...]
    seq = k.shape[0]
    _write_layer(kt_ref, layer, k.T.reshape(n_kv, HEAD_DIM, seq), first)
    _write_layer(vt_ref, layer, v.T.reshape(n_kv, HEAD_DIM, seq), first)
    outs = _attend(_stacked_heads(q, n_kv), [[(k[:, _hs(g)], v[:, _hs(g)], None, False)] for g in range(n_kv)])
    o_ref[...] = (_unstack_heads(outs, n_kv) * _silu(z)).astype(BF16)


def _attn_ctx_call(proj, prev, layer, batch, seq, off_q, off_k, off_v, off_z, n_kv, qn=None, kn=None):
    t = batch * seq
    kvw = n_kv * HEAD_DIM
    norm = qn is not None
    first = prev is None
    in_specs = [pl.BlockSpec((seq, BRANCH_W), lambda b: (b, off_q // BRANCH_W)),
                pl.BlockSpec((seq, kvw), lambda b: (b, off_k // kvw)),
                pl.BlockSpec((seq, kvw), lambda b: (b, off_v // kvw)),
                pl.BlockSpec((seq, BRANCH_W), lambda b: (b, off_z // BRANCH_W))]
    args = [proj, proj, proj, proj]
    if norm:
        in_specs += [_layer_spec((1, BRANCH_W), layer, 1), _layer_spec((1, kvw), layer, 1)]
        args += [qn, kn]
    aliases = {}
    if first:
        cache_spec = pl.BlockSpec((None, DEPTH, n_kv, HEAD_DIM, seq), lambda b: (b, 0, 0, 0, 0))
    else:
        aliases = {len(args): 1, len(args) + 1: 2}
        in_specs += [pl.BlockSpec(memory_space=pl.ANY)] * 2
        args += list(prev)
        cache_spec = pl.BlockSpec((None, None, n_kv, HEAD_DIM, seq), lambda b: (b, layer, 0, 0, 0))
    cache_shape = jax.ShapeDtypeStruct((batch, DEPTH, n_kv, HEAD_DIM, seq), F32)
    return pl.pallas_call(
        functools.partial(_attn_ctx_kernel, n_kv=n_kv, norm=norm, layer=layer, first=first),
        grid=(batch,), in_specs=in_specs,
        out_specs=[pl.BlockSpec((seq, BRANCH_W), lambda b: (b, 0)), cache_spec, cache_spec],
        out_shape=[jax.ShapeDtypeStruct((t, BRANCH_W), BF16), cache_shape, cache_shape],
        input_output_aliases=aliases,
        compiler_params=_params(32, 1),
        name="attn_ctx_norm" if norm else "attn_ctx",
    )(*args)


def _attn_lat_kernel(q_ref, kv_ref, z_ref, ckt_ref, cvt_ref, qtab_ref, ktab_ref, qn_ref, kn_ref, o_ref,
                     k_s, v_s):
    kw = KV_HEADS * HEAD_DIM

    @pl.when(pl.program_id(1) == 0)
    def _():
        kv = kv_ref[...]
        k_s[...] = _rope(_head_rms(kv[:, :kw].astype(F32)) * kn_ref[...], ktab_ref).astype(BF16)
        v_s[...] = kv[:, kw:]

    q = _rope(_head_rms(q_ref[...].astype(F32)) * qn_ref[...], qtab_ref)
    k, v = k_s[...], v_s[...]
    outs = _attend(_stacked_heads(q, KV_HEADS),
                   [[(k[:, _hs(g)], v[:, _hs(g)], None, False), (ckt_ref[g], cvt_ref[g], None, True)]
                    for g in range(KV_HEADS)])
    o_ref[...] = (_unstack_heads(outs, KV_HEADS) * _silu(z_ref[...].astype(F32))).astype(BF16)


def _attn_lat_call(proj, cache_kt, cache_vt, layer, batch, seq, qtab, ktab, qn, kn):
    tq = 256
    nq = seq // tq
    past = cache_kt.shape[-1]
    kw = KV_HEADS * HEAD_DIM
    ctx_spec = pl.BlockSpec((None, None, KV_HEADS, HEAD_DIM, past), lambda b, i: (b, layer, 0, 0, 0))
    return pl.pallas_call(
        _attn_lat_kernel,
        grid=(batch, nq),
        in_specs=[pl.BlockSpec((tq, BRANCH_W), lambda b, i: (b * nq + i, OFF_AQKV // BRANCH_W)),
                  pl.BlockSpec((seq, 2 * kw), lambda b, i: (b, (OFF_AQKV + BRANCH_W) // (2 * kw))),
                  pl.BlockSpec((tq, BRANCH_W), lambda b, i: (b * nq + i, OFF_AZ // BRANCH_W)),
                  ctx_spec, ctx_spec,
                  pl.BlockSpec((3, tq, BRANCH_W), lambda b, i: (0, i, 0)),
                  pl.BlockSpec((3, seq, kw), lambda b, i: (0, 0, 0)),
                  _layer_spec((1, BRANCH_W), layer, 2),
                  _layer_spec((1, kw), layer, 2)],
        out_specs=pl.BlockSpec((tq, BRANCH_W), lambda b, i: (b * nq + i, 0)),
        out_shape=jax.ShapeDtypeStruct((batch * seq, BRANCH_W), BF16),
        scratch_shapes=[pltpu.VMEM((seq, kw), BF16), pltpu.VMEM((seq, kw), BF16)],
        compiler_params=_params(40, 2),
        name="attn_lat",
    )(proj, proj, proj, cache_kt, cache_vt, qtab, ktab, qn, kn)


def _na_bias_kernel(t_ref, o_ref):
    nblk = o_ref.shape[0]
    c = lax.broadcasted_iota(jnp.int32, (GRID_W, 2 * GRID_W), 0)
    j = lax.broadcasted_iota(jnp.int32, (GRID_W, 2 * GRID_W), 1)
    kc = j & (GRID_W - 1)
    dc = kc - c + (NA_COLS - 1)
    cs = jnp.clip(c - NA_COLS // 2, 0, GRID_W - NA_COLS)
    valid = jnp.logical_and(kc >= cs, kc < cs + NA_COLS)
    left = j < GRID_W

    def body(b, carry):
        b2 = jnp.minimum(b + 1, nblk - 1)
        acc = jnp.full((GRID_W, 2 * GRID_W), NEG_INF, F32)
        for i in range(N_DC):
            acc = jnp.where(dc == i, jnp.where(left, t_ref[b * N_DC + i], t_ref[b2 * N_DC + i]), acc)
        o_ref[b] = jnp.where(valid, acc, NEG_INF)
        return carry

    lax.fori_loop(0, nblk, body, 0)


def _na_bias_call(na_bias):
    nblk = DEPTH * N_HEADS * N_DR
    return pl.pallas_call(
        _na_bias_kernel,
        in_specs=[pl.BlockSpec(memory_space=pltpu.SMEM)],
        out_specs=pl.BlockSpec((nblk, GRID_W, 2 * GRID_W), lambda: (0, 0, 0)),
        out_shape=jax.ShapeDtypeStruct((nblk, GRID_W, 2 * GRID_W), F32),
        name="na_bias",
    )(na_bias.reshape(-1))


def _na_kernel(q_ref, k_ref, v_ref, z_ref, ckt_ref, cvt_ref, tb_ref, o_ref, *, rows):
    win = NA_ROWS * GRID_W
    r = pl.program_id(1)
    rs = jnp.clip(r - NA_ROWS // 2, 0, rows - NA_ROWS)
    r0 = pl.multiple_of(rs * GRID_W, GRID_W)
    kwin = k_ref[pl.ds(r0, win), :]
    vwin = v_ref[pl.ds(r0, win), :]
    q = q_ref[...]
    dr0 = rs - r + NA_ROWS - 1
    bias = [jnp.concatenate([tb_ref[h * N_DR + dr0 + 2 * p] for p in range(NA_ROWS // 2)], axis=1)
            for h in range(N_HEADS)]
    outs = _attend([q[:, _hs(h)] for h in range(N_HEADS)],
                   [[(kwin[:, _hs(h)], vwin[:, _hs(h)], bias[h], False), (ckt_ref[h], cvt_ref[h], None, True)]
                    for h in range(N_HEADS)])
    o_ref[...] = (jnp.concatenate(outs, axis=-1) * _silu(z_ref[...].astype(F32))).astype(BF16)


def _na_call(proj, cache_kt, cache_vt, tb, layer, batch, seq):
    rows = seq // GRID_W
    assert rows >= NA_ROWS
    past = cache_kt.shape[-1]
    nblk = N_HEADS * N_DR
    cq = OFF_D // BRANCH_W
    ctx_spec = pl.BlockSpec((None, None, N_HEADS, HEAD_DIM, past), lambda b, r: (b, layer, 0, 0, 0))
    return pl.pallas_call(
        functools.partial(_na_kernel, rows=rows),
        grid=(batch, rows),
        in_specs=[pl.BlockSpec((GRID_W, BRANCH_W), lambda b, r: (b * rows + r, cq)),
                  pl.BlockSpec((seq, BRANCH_W), lambda b, r: (b, cq + 1)),
                  pl.BlockSpec((seq, BRANCH_W), lambda b, r: (b, cq + 2)),
                  pl.BlockSpec((GRID_W, BRANCH_W), lambda b, r: (b * rows + r, cq + 3)),
                  ctx_spec, ctx_spec,
                  pl.BlockSpec((nblk, GRID_W, 2 * GRID_W), lambda b, r: (layer, 0, 0))],
        out_specs=pl.BlockSpec((GRID_W, BRANCH_W), lambda b, r: (b * rows + r, 0)),
        out_shape=jax.ShapeDtypeStruct((batch * seq, BRANCH_W), BF16),
        compiler_params=_params(32, 2),
        name="na_lat",
    )(proj, proj, proj, proj, cache_kt, cache_vt, tb)


def _gdn_kernel(*refs, seq, has_s0, layer, emit):
    qkv_ref, z_ref, ab_ref, cw_ref, par_ref, g_ref = refs[:6]
    s0_ref = refs[6] if has_s0 else None
    xp_ref, q_s, k_s, v_s, gate_ref, u_s, w_s, qd_s, qkm_s, kd_s, egl_s, s_s, oacc_ref = refs[-13:]
    if emit == "none":
        o_ref, st_ref = refs[-14], None
    else:
        o_ref, st_ref = refs[-15], refs[-14]
    n_chunks = seq // CHUNK
    n_levels = CHUNK.bit_length() - 1
    qkv_w = 3 * BRANCH_W
    pad = 8
    half = SHORT_CONV // 2
    head_sum = _head_block_matrix(BRANCH_W, 1.0)

    xp_ref[0:pad, :] = jnp.zeros((pad, qkv_w), F32)
    xp_ref[seq + pad:seq + 2 * pad, :] = jnp.zeros((pad, qkv_w), F32)
    xp_ref[pad:seq + pad, :] = qkv_ref[...].astype(F32)
    tr = 256
    for t in range(seq // tr):
        base = pad + t * tr - half
        y = xp_ref[base:base + tr, :] * cw_ref[0:1, :]
        for j in range(1, SHORT_CONV):
            y = y + xp_ref[base + j:base + j + tr, :] * cw_ref[j:j + 1, :]
        y = _silu(y)
        rows = slice(t * tr, (t + 1) * tr)
        qq, kk = y[:, :BRANCH_W], y[:, BRANCH_W:2 * BRANCH_W]
        q_s[rows, :] = qq * lax.rsqrt(_head_reduce(qq * qq, head_sum) + EPS) * SCALE
        k_s[rows, :] = kk * lax.rsqrt(_head_reduce(kk * kk, head_sum) + EPS)
        v_s[rows, :] = y[:, 2 * BRANCH_W:]

    x = ab_ref[...]
    gate_ref[0] = jax.nn.sigmoid(x)
    xs = x + par_ref[0:1, :]
    softplus = jnp.maximum(xs, 0.0) + jnp.log1p(jnp.exp(-jnp.abs(xs)))
    gate_ref[1] = -jnp.exp(par_ref[1:2, :]) * softplus

    for d in range(2):
        if has_s0:
            s_s[d] = jnp.concatenate([s0_ref[d, h] for h in range(N_HEADS)], axis=-1)
        else:
            s_s[d] = jnp.zeros((HEAD_DIM, BRANCH_W), F32)

    li = lax.broadcasted_iota(jnp.int32, (CHUNK, BRANCH_W), 0)
    lj = lax.broadcasted_iota(jnp.int32, (CHUNK, BRANCH_W), 1) & (HEAD_DIM - 1)
    incl = (li >= lj, li <= lj)
    strict = (li > lj, li < lj)
    level = [((li ^ lj) >> l) == 1 for l in range(n_levels)]
    ti = lax.broadcasted_iota(jnp.int32, (CHUNK, CHUNK), 0)
    tj = lax.broadcasted_iota(jnp.int32, (CHUNK, CHUNK), 1)
    tri = (jnp.where(ti >= tj, 1.0, 0.0).astype(BF16), jnp.where(ti <= tj, 1.0, 0.0).astype(BF16))
    bi = lax.broadcasted_iota(jnp.int32, (BRANCH_W, BRANCH_W), 0) >> 6
    bj = lax.broadcasted_iota(jnp.int32, (BRANCH_W, BRANCH_W), 1) >> 6
    same_head = bi == bj
    gc_i = lax.broadcasted_iota(jnp.int32, (128, BRANCH_W), 0)
    gh_j = lax.broadcasted_iota(jnp.int32, (128, BRANCH_W), 1) >> 6
    sel_beta = [jnp.where(gc_i == gh_j + 4 * d, 1.0, 0.0).astype(BF16) for d in range(2)]
    sel_gate = [jnp.where(gc_i == gh_j + 8 + 4 * d, 1.0, 0.0).astype(BF16) for d in range(2)]
    rh_i = lax.broadcasted_iota(jnp.int32, (BRANCH_W, 128), 0) >> 6
    rc_j = lax.broadcasted_iota(jnp.int32, (BRANCH_W, 128), 1)
    row_gate = [rc_j == rh_i + 8 + 4 * d for d in range(2)]
    ones_l = jnp.ones((CHUNK, 128), BF16)
    lane_head = lax.broadcasted_iota(jnp.int32, (CHUNK, BRANCH_W), 1) >> 6

    def block_diag(y):
        yb = y.astype(BF16)
        return jnp.where(same_head, jnp.concatenate([yb] * N_HEADS, axis=0), jnp.zeros((), BF16))

    def bdmm(x, ybd):
        return jnp.dot(x.astype(BF16), ybd, preferred_element_type=F32)

    def prepare(chains):
        n = range(len(chains))
        dd = [d for d, _ in chains]
        rows = [pl.ds(_aligned(c * CHUNK, CHUNK), CHUNK) for _, c in chains]
        beta = [gate_ref[0, rows[i], :] for i in n]
        la = [gate_ref[1, rows[i], :] for i in n]
        gc = [_mm_exact(tri[dd[i]], la[i]) for i in n]
        gcb = [_mm_exact_lhs(gc[i], sel_gate[dd[i]]) for i in n]
        bcb = [_mm_exact_lhs(beta[i], sel_beta[dd[i]]) for i in n]
        grow = [_mm_nt_exact(ones_l, jnp.where(row_gate[dd[i]], jnp.concatenate([gc[i]] * N_HEADS, axis=0), 0.0))
                for i in n]
        dm = [jnp.exp(jnp.where(incl[dd[i]], gcb[i] - grow[i], NEG_INF)) for i in n]
        k = [k_s[rows[i], :] for i in n]
        q = [q_s[rows[i], :] for i in n]
        v = [v_s[rows[i], :] for i in n]
        kq = [_mm_nt(jnp.concatenate([k[i], q[i]], axis=0), block_diag(k[i])) for i in n]
        a = [jnp.where(strict[dd[i]], bcb[i] * kq[i][:CHUNK] * dm[i], 0.0) for i in n]
        tm = [-jnp.where(level[0], a[i], 0.0) for i in n]
        for l in range(1, n_levels):
            b = [jnp.where(level[l], a[i], 0.0) for i in n]
            y = [b[i] + bdmm(tm[i], block_diag(b[i])) for i in n]
            tm = [tm[i] - (y[i] + bdmm(y[i], block_diag(tm[i]))) for i in n]
        eg = [jnp.exp(gcb[i]) for i in n]
        bv = [bcb[i] * v[i] for i in n]
        bk = [bcb[i] * k[i] * eg[i] for i in n]
        u = [bv[i] + bdmm(tm[i], block_diag(bv[i])) for i in n]
        w = [bk[i] + bdmm(tm[i], block_diag(bk[i])) for i in n]
        for i in n:
            d, c = chains[i]
            gl = gcb[i][CHUNK - 1:CHUNK, :] if d == 0 else gcb[i][0:1, :]
            u_s[d, rows[i], :] = u[i]
            w_s[d, rows[i], :] = w[i].astype(BF16)
            qd_s[d, rows[i], :] = (q[i] * eg[i]).astype(BF16)
            qkm_s[d, rows[i], :] = (kq[i][CHUNK:] * dm[i]).astype(BF16)
            kd_s[d, rows[i], :] = (k[i] * jnp.exp(gl - gcb[i])).astype(BF16)
            egl_s[d, pl.ds(_aligned(c * 8, 8), 8), :] = jnp.broadcast_to(jnp.exp(gl), (8, BRANCH_W))

    group = min(PREP_CHUNKS, n_chunks)
    n_groups = n_chunks // group
    if n_groups == 1:
        prepare([(d, c) for c in range(group) for d in range(2)])
    else:
        def prep_body(j, carry):
            prepare([(d, j * group + c) for c in range(group) for d in range(2)])
            return carry
        lax.fori_loop(0, n_groups, prep_body, 0)

    def scan_body(i, carry):
        dirs = range(2)
        cidx = [i, n_chunks - 1 - i]
        rows = [pl.ds(pl.multiple_of(cidx[d] * CHUNK, CHUNK), CHUNK) for d in dirs]
        s = [s_s[d] for d in dirs]
        sbd = [block_diag(s[d]) for d in dirs]
        vnew = [u_s[d, rows[d], :] - bdmm(w_s[d, rows[d], :], sbd[d]) for d in dirs]
        o = [bdmm(qd_s[d, rows[d], :], sbd[d]) + bdmm(qkm_s[d, rows[d], :], block_diag(vnew[d])) for d in dirs]
        full = [_mm_tn(kd_s[d, rows[d], :], vnew[d]) for d in dirs]
        for d in dirs:
            upd = jnp.zeros((CHUNK, BRANCH_W), F32)
            for h in range(N_HEADS):
                upd = jnp.where(lane_head == h, full[d][h * HEAD_DIM:(h + 1) * HEAD_DIM, :], upd)
            egl = egl_s[d, pl.ds(pl.multiple_of(cidx[d] * 8, 8), 8), :][0:1, :]
            s_s[d] = s[d] * egl + upd
            oacc_ref[d, rows[d], :] = o[d]
        return carry

    lax.fori_loop(0, n_chunks, scan_body, 0)

    if st_ref is not None:
        _write_state(st_ref, layer, emit == "first", lambda d, h: s_s[d][:, _hs(h)])
    o = oacc_ref[0] + oacc_ref[1]
    ms = _head_reduce(o * o, _head_block_matrix(BRANCH_W, 1.0 / HEAD_DIM))
    o_ref[...] = (o * lax.rsqrt(ms + EPS) * g_ref[...] * _silu(z_ref[...].astype(F32))).astype(BF16)


def _state_spec(layer):
    return pl.BlockSpec((None, None, 2, N_HEADS, HEAD_DIM, HEAD_DIM), lambda b: (b, layer, 0, 0, 0, 0))


def _state_output(emit, prev, layer, batch, n_args):
    if emit == "none":
        return [], [], [], [], {}
    shape = jax.ShapeDtypeStruct((batch, DEPTH, 2, N_HEADS, HEAD_DIM, HEAD_DIM), F32)
    if emit == "first":
        spec = pl.BlockSpec((None, DEPTH, 2, N_HEADS, HEAD_DIM, HEAD_DIM), lambda b: (b, 0, 0, 0, 0, 0))
        return [], [], [spec], [shape], {}
    return [pl.BlockSpec(memory_space=pl.ANY)], [prev], [_state_spec(layer)], [shape], {n_args: 1}


def _gdn_call(proj, gab, cw8, par, norm_g, layer, batch, seq, s0=None, emit="none", prev=None):
    has_s0 = s0 is not None
    qkv_w = 3 * BRANCH_W
    in_specs = [pl.BlockSpec((seq, qkv_w), lambda b: (b, OFF_GQKV // qkv_w)),
                pl.BlockSpec((seq, BRANCH_W), lambda b: (b, OFF_GZ // BRANCH_W)),
                pl.BlockSpec((seq, 128), lambda b: (b, 0)),
                _layer_spec((8, qkv_w), layer, 1),
                _layer_spec((2, 128), layer, 1),
                _layer_spec((1, BRANCH_W), layer, 1)]
    args = [proj, proj, gab, cw8, par, norm_g]
    if has_s0:
        in_specs.append(_state_spec(layer))
        args.append(s0)
    st_in_specs, st_args, st_out_specs, st_shapes, aliases = _state_output(emit, prev, layer, batch, len(args))
    return pl.pallas_call(
        functools.partial(_gdn_kernel, seq=seq, has_s0=has_s0, layer=layer, emit=emit),
        grid=(batch,), in_specs=in_specs + st_in_specs,
        out_specs=[pl.BlockSpec((seq, BRANCH_W), lambda b: (b, 0))] + st_out_specs,
        out_shape=[jax.ShapeDtypeStruct((batch * seq, BRANCH_W), BF16)] + st_shapes,
        input_output_aliases=aliases,
        scratch_shapes=[pltpu.VMEM((seq + 16, qkv_w), F32),
                        pltpu.VMEM((seq, BRANCH_W), F32),
                        pltpu.VMEM((seq, BRANCH_W), F32),
                        pltpu.VMEM((seq, BRANCH_W), F32),
                        pltpu.VMEM((2, seq, 128), F32),
                        pltpu.VMEM((2, seq, BRANCH_W), F32),
                        pltpu.VMEM((2, seq, BRANCH_W), BF16),
                        pltpu.VMEM((2, seq, BRANCH_W), BF16),
                        pltpu.VMEM((2, seq, BRANCH_W), BF16),
                        pltpu.VMEM((2, seq, BRANCH_W), BF16),
                        pltpu.VMEM((2, seq // CHUNK * 8, BRANCH_W), F32),
                        pltpu.VMEM((2, HEAD_DIM, BRANCH_W), F32),
                        pltpu.VMEM((2, seq, BRANCH_W), F32)],
        compiler_params=_params(48, 1),
        name="gdn",
    )(*args, *st_args)


_RET_LOG_GAMMA = [[float(np.log1p(-np.exp2(-(base + h)))) for h in range(N_HEADS)] for base in RET_DECAY_BASE]


def _ret_kernel(*refs, seq, has_s0, layer, emit):
    qkv_ref, z_ref, g_ref = refs[:3]
    s0_ref = refs[3] if has_s0 else None
    o_ref, st_ref = (refs[-1], None) if emit == "none" else (refs[-2], refs[-1])
    tq = 256
    heads = range(N_HEADS)
    lgf, lgb = _RET_LOG_GAMMA
    ks = [qkv_ref[:, BRANCH_W + h * HEAD_DIM:BRANCH_W + (h + 1) * HEAD_DIM] for h in heads]
    vs = [qkv_ref[:, 2 * BRANCH_W + h * HEAD_DIM:2 * BRANCH_W + (h + 1) * HEAD_DIM] for h in heads]
    if st_ref is not None:
        tcol = lax.broadcasted_iota(jnp.int32, (seq, 1), 0).astype(F32)
        stf = [_mm_tn(ks[h] * jnp.exp((seq - 1.0 - tcol) * lgf[h]), vs[h]) for h in heads]
        stb = [_mm_tn(ks[h] * jnp.exp(tcol * lgb[h]), vs[h]) for h in heads]
        if has_s0:
            stf = [stf[h] + float(np.exp(seq * lgf[h])) * s0_ref[0, h] for h in heads]
            stb = [stb[h] + float(np.exp(seq * lgb[h])) * s0_ref[1, h] for h in heads]
        _write_state(st_ref, layer, emit == "first", lambda d, h: (stf, stb)[d][h])
    tiles = []
    for t in range(seq // tq):
        qs = [qkv_ref[t * tq:(t + 1) * tq, _hs(h)] * SCALE for h in heads]
        qk = [_mm_nt(qs[h], ks[h]) for h in heads]
        di = (lax.broadcasted_iota(jnp.int32, (tq, seq), 0) + t * tq
              - lax.broadcasted_iota(jnp.int32, (tq, seq), 1)).astype(F32)
        dm = [jnp.where(di == 0, 2.0, jnp.exp(di * jnp.where(di >= 0, lgf[h], -lgb[h]))) for h in heads]
        o = [_mm(qk[h] * dm[h], vs[h]) for h in heads]
        if has_s0:
            tt = lax.broadcasted_iota(jnp.int32, (tq, 1), 0).astype(F32) + float(t * tq)
            o = [o[h] + _mm(qs[h] * jnp.exp((tt + 1.0) * lgf[h]), s0_ref[0, h])
                 + _mm(qs[h] * jnp.exp((seq - tt) * lgb[h]), s0_ref[1, h]) for h in heads]
        ms = [jnp.mean(o[h] * o[h], axis=-1, keepdims=True) for h in heads]
        tiles.append(jnp.concatenate([o[h] * lax.rsqrt(ms[h] + EPS) * g_ref[...] for h in heads], axis=-1))
    o_ref[...] = (jnp.concatenate(tiles, axis=0) * _silu(z_ref[...].astype(F32))).astype(BF16)


def _ret_call(proj, norm_g, layer, batch, seq, s0=None, emit="none", prev=None):
    has_s0 = s0 is not None
    qkv_w = 3 * BRANCH_W
    in_specs = [pl.BlockSpec((seq, qkv_w), lambda b: (b, OFF_C // qkv_w)),
                pl.BlockSpec((seq, BRANCH_W), lambda b: (b, (OFF_C + qkv_w) // BRANCH_W)),
                _layer_spec((1, HEAD_DIM), layer, 1)]
    args = [proj, proj, norm_g]
    if has_s0:
        in_specs.append(_state_spec(layer))
        args.append(s0)
    st_in_specs, st_args, st_out_specs, st_shapes, aliases = _state_output(emit, prev, layer, batch, len(args))
    return pl.pallas_call(
        functools.partial(_ret_kernel, seq=seq, has_s0=has_s0, layer=layer, emit=emit),
        grid=(batch,), in_specs=in_specs + st_in_specs,
        out_specs=[pl.BlockSpec((seq, BRANCH_W), lambda b: (b, 0))] + st_out_specs,
        out_shape=[jax.ShapeDtypeStruct((batch * seq, BRANCH_W), BF16)] + st_shapes,
        input_output_aliases=aliases,
        compiler_params=_params(48, 1),
        name="retention",
    )(*args, *st_args)


def _out_kernel(*refs, final):
    if final:
        (h_ref, mod_ref, g_ref, oa_ref, ob_ref, oc_ref, od_ref, wg_ref, wb_ref, wo_ref, fn_ref,
         o_ref, y_ref) = refs
    else:
        h_ref, mod_ref, g_ref, oa_ref, ob_ref, oc_ref, od_ref, wg_ref, wb_ref, wo_ref, o_ref = refs
    x = h_ref[...]
    mod = mod_ref[0]
    hn = _modulated_norm(x, mod, g_ref[...]).astype(BF16)
    merged = None
    for n, br_ref in enumerate((oa_ref, ob_ref, oc_ref, od_ref)):
        gate = jax.nn.sigmoid(jnp.dot(hn, wg_ref[:, n * D_MODEL:(n + 1) * D_MODEL], preferred_element_type=F32))
        up = jnp.dot(br_ref[...], wb_ref[n], preferred_element_type=F32)
        merged = gate * up if merged is None else merged + gate * up
    out = jnp.dot(merged.astype(BF16), wo_ref[...], preferred_element_type=F32)
    hnew = x + mod[:, 2 * D_MODEL:] * out
    o_ref[...] = hnew
    if final:
        ms = jnp.mean(hnew * hnew, axis=-1, keepdims=True)
        y_ref[...] = hnew * lax.rsqrt(ms + EPS) * fn_ref[...]


def _out_call(h2d, mod3, norm_g3, branches, wg, wb, wo, layer, rows_per_mod, final_norm=None):
    t = h2d.shape[0]
    tm = 512
    final = final_norm is not None
    if mod3.shape[0] == 1:
        mod_idx = lambda i: (0, 0, 0)
    else:
        mod_idx = lambda i: ((i * tm) // rows_per_mod, 0, 0)
    once = pl.Buffered(1)
    in_specs = [pl.BlockSpec((tm, D_MODEL), lambda i: (i, 0)),
                pl.BlockSpec((1, 1, 3 * D_MODEL), mod_idx),
                _layer_spec((1, D_MODEL), layer, 1)]
    in_specs += [pl.BlockSpec((tm, BRANCH_W), lambda i: (i, 0))] * N_BRANCH
    in_specs += [pl.BlockSpec((None, D_MODEL, N_BRANCH * D_MODEL), lambda i: (layer, 0, 0), pipeline_mode=once),
                 pl.BlockSpec((None, N_BRANCH, BRANCH_W, D_MODEL), lambda i: (layer, 0, 0, 0), pipeline_mode=once),
                 pl.BlockSpec((None, D_MODEL, D_MODEL), lambda i: (layer, 0, 0), pipeline_mode=once)]
    args = [h2d, mod3, norm_g3, *branches, wg, wb, wo]
    out_specs = [pl.BlockSpec((tm, D_MODEL), lambda i: (i, 0))]
    out_shape = [jax.ShapeDtypeStruct((t, D_MODEL), F32)]
    if final:
        in_specs.append(pl.BlockSpec((1, D_MODEL), lambda i: (0, 0)))
        args.append(final_norm.reshape(1, D_MODEL))
        out_specs.append(pl.BlockSpec((tm, D_MODEL), lambda i: (i, 0)))
        out_shape.append(jax.ShapeDtypeStruct((t, D_MODEL), F32))
    return pl.pallas_call(
        functools.partial(_out_kernel, final=final),
        grid=(t // tm,), in_specs=in_specs, out_specs=out_specs, out_shape=out_shape,
        compiler_params=_params(48, 1),
        name="merge_out_final" if final else "merge_out",
    )(*args)


def _prep_weights(w_in):
    offs = np.concatenate([[0], np.cumsum(IN_SPLITS)])
    seg = lambda i, j: w_in[:, :, offs[i]:offs[j]]
    pad = jnp.zeros((DEPTH, D_MODEL, PROJ_W - OFF_GAB - IN_SPLITS[5]), w_in.dtype)
    wcat = jnp.concatenate([seg(7, 11), seg(11, 15), seg(6, 7), seg(4, 5), seg(0, 3), seg(3, 4), seg(5, 6), pad],
                           axis=2).astype(BF16)
    return wcat, seg(15, 16).astype(BF16)


def _rope_tables(seq):
    t = jnp.arange(seq)
    quarter = HEAD_DIM // 4
    inv = ROPE_THETA ** (-jnp.arange(quarter, dtype=F32) / quarter)

    def half(pos):
        ang = pos.astype(F32)[:, None] * inv
        c, s, zero = jnp.cos(ang), jnp.sin(ang), jnp.zeros_like(ang)
        return jnp.concatenate([c, c], -1), jnp.concatenate([-s, zero], -1), jnp.concatenate([zero, s], -1)

    parts = [jnp.concatenate([a, b], -1) for a, b in zip(half(t // GRID_W), half(t % GRID_W))]
    tab = jnp.stack(parts)
    return jnp.tile(tab, (1, 1, N_HEADS)), jnp.tile(tab, (1, 1, KV_HEADS))


def _layer(h2d, batch, seq, mod, pw, layer, ctx, caches, final_norm):
    proj, gab = _inproj_call(h2d, mod, pw["norm_g"], pw["wcat"], layer, seq)
    kw = KV_HEADS * HEAD_DIM
    if ctx is None:
        emit = "first" if caches is None else "update"
        akv, nkv, sg_all, sr_all = caches or (None, None, None, None)
        oa, *akv = _attn_ctx_call(proj, akv, layer, batch, seq, OFF_AQKV, OFF_AQKV + BRANCH_W,
                                  OFF_AQKV + BRANCH_W + kw, OFF_AZ, KV_HEADS, pw["qn"], pw["kn"])
        od, *nkv = _attn_ctx_call(proj, nkv, layer, batch, seq, OFF_D, OFF_D + BRANCH_W,
                                  OFF_D + 2 * BRANCH_W, OFF_D + 3 * BRANCH_W, N_HEADS)
        ob, sg_all = _gdn_call(proj, gab, pw["cw8"], pw["gdn_par"], pw["gdn_norm"], layer, batch, seq,
                               emit=emit, prev=sg_all)
        oc, sr_all = _ret_call(proj, pw["ret_norm"], layer, batch, seq, emit=emit, prev=sr_all)
        caches = (akv, nkv, sg_all, sr_all)
    else:
        oa = _attn_lat_call(proj, ctx["akt"], ctx["avt"], layer, batch, seq, ctx["qtab"], ctx["ktab"],
                            pw["qn"], pw["kn"])
        od = _na_call(proj, ctx["nkt"], ctx["nvt"], ctx["tb"], layer, batch, seq)
        ob, = _gdn_call(proj, gab, pw["cw8"], pw["gdn_par"], pw["gdn_norm"], layer, batch, seq, s0=ctx["sg"])
        oc, = _ret_call(proj, pw["ret_norm"], layer, batch, seq, s0=ctx["sr"])
    outs = _out_call(h2d, mod, pw["norm_g"], (oa, ob, oc, od), pw["wg"], pw["wb"], pw["wo"], layer, seq, final_norm)
    return outs, caches


def kernel(x_prompt, x_sample, cache_attn_k, cache_attn_v, cache_na_k, cache_na_v, state_gdn, state_ret, c, c_ctx, w_ada, b_ada, norm_g, w_in, conv_w, gdn_a_log, gdn_dt_bias, gdn_norm, attn_q_norm, attn_k_norm, ret_norm, na_bias, w_branch, w_out, final_norm):
    batch, seq, _ = x_prompt.shape
    dbatch, dseq, _ = x_sample.shape
    assert dbatch == 8, "the modulation kernel handles exactly one sublane tile of conditioning rows"

    wcat, wg = _prep_weights(w_in)
    par = jnp.zeros((DEPTH, 2, 128), F32)
    par = par.at[:, 0, 8:16].set(gdn_dt_bias.reshape(DEPTH, 8)).at[:, 1, 8:16].set(gdn_a_log.reshape(DEPTH, 8))
    pw = dict(
        w_ada=w_ada, b_ada=b_ada.reshape(DEPTH, 1, 3 * D_MODEL), norm_g=norm_g.reshape(DEPTH, 1, D_MODEL),
        wcat=wcat, wg=wg, wb=w_branch.astype(BF16), wo=w_out.astype(BF16),
        cw8=jnp.concatenate([conv_w, jnp.zeros((DEPTH, 8 - SHORT_CONV, 3 * BRANCH_W), F32)], axis=1),
        gdn_par=par,
        gdn_norm=jnp.tile(gdn_norm, (1, N_HEADS)).reshape(DEPTH, 1, BRANCH_W),
        ret_norm=ret_norm.reshape(DEPTH, 1, HEAD_DIM),
        qn=jnp.tile(attn_q_norm, (1, N_HEADS)).reshape(DEPTH, 1, BRANCH_W),
        kn=jnp.tile(attn_k_norm, (1, KV_HEADS)).reshape(DEPTH, 1, KV_HEADS * HEAD_DIM))

    cond = jnp.concatenate([jnp.broadcast_to(c_ctx, (8, D_MODEL)), c], axis=0)
    mods = _mod_call(cond, w_ada, pw["b_ada"])

    h = x_prompt.reshape(batch * seq, D_MODEL)
    caches = None
    for l in range(DEPTH):
        outs, caches = _layer(h, batch, seq, mods[l, 0:1].reshape(1, 1, 3 * D_MODEL), pw, l, None, caches,
                              final_norm if l == DEPTH - 1 else None)
        h = outs[0]
    y_prompt = outs[1].reshape(batch, seq, D_MODEL)
    token_major = lambda a: a.transpose(0, 1, 4, 2, 3)
    (akt, avt), (nkt, nvt), new_state_gdn, new_state_ret = caches
    new_attn_k, new_attn_v, new_na_k, new_na_v = (token_major(a) for a in (akt, avt, nkt, nvt))

    qtab, ktab = _rope_tables(dseq)
    feature_major = lambda a: a.transpose(0, 1, 3, 4, 2)
    ctx = dict(akt=feature_major(cache_attn_k), avt=feature_major(cache_attn_v),
               nkt=feature_major(cache_na_k), nvt=feature_major(cache_na_v),
               sg=state_gdn, sr=state_ret, tb=_na_bias_call(na_bias), qtab=qtab, ktab=ktab)
    h = x_sample.reshape(dbatch * dseq, D_MODEL)
    for l in range(DEPTH):
        outs, _ = _layer(h, dbatch, dseq, mods[l, 8:16].reshape(dbatch, 1, 3 * D_MODEL), pw, l, ctx, None,
                         final_norm if l == DEPTH - 1 else None)
        h = outs[0]
    y_sample = outs[1].reshape(dbatch, dseq, D_MODEL)
    return (y_prompt, y_sample, new_attn_k, new_attn_v, new_na_k, new_na_v, new_state_gdn, new_state_ret)
```

```python
import functools

import numpy as np
import jax
import jax.numpy as jnp
from jax import lax
from jax.experimental import pallas as pl
from jax.experimental.pallas import tpu as pltpu

F32 = jnp.float32
BF16 = jnp.bfloat16

D_MODEL = 1024
HEAD_DIM = 64
N_HEADS = 4
KV_HEADS = N_HEADS // 2
BRANCH_W = N_HEADS * HEAD_DIM
N_BRANCH = 4
DEPTH = 2
GRID_W = 64
CHUNK = 64
PREP_CHUNKS = 4
SCAN_UNROLL = 4
assert CHUNK == HEAD_DIM
SHORT_CONV = 5
NA_ROWS = 8
NA_COLS = 16
N_DR = 2 * NA_ROWS - 1
N_DC = 2 * NA_COLS - 1
ROPE_THETA = 10000.0
RET_DECAY_BASE = (5.0, 5.5)
EPS = 1e-6
SCALE = HEAD_DIM ** -0.5
NEG_INF = float("-inf")

IN_SPLITS = (256, 128, 128, 256, 768, 16, 256, 256, 256, 256, 256, 256, 256, 256, 256, 4096)
PROJ_W = 4096
OFF_C = 0
OFF_D = 1024
OFF_GZ = 2048
OFF_GQKV = 2304
OFF_AQKV = 3072
OFF_AZ = 3584
OFF_GAB = 3840

V7X_VMEM_BYTES = 64 * 1024 * 1024
MIB = 1024 * 1024


def _params(vmem_mib, n_axes):
    assert vmem_mib * MIB < V7X_VMEM_BYTES
    return pltpu.CompilerParams(dimension_semantics=("arbitrary",) * n_axes,
                                vmem_limit_bytes=vmem_mib * MIB)


def _layer_spec(block, layer, n_grid):
    zeros = (0,) * len(block)
    if n_grid == 1:
        return pl.BlockSpec((None,) + block, lambda i: (layer,) + zeros)
    return pl.BlockSpec((None,) + block, lambda i, j: (layer,) + zeros)


def _mm(a, b):
    return jnp.dot(a.astype(BF16), b.astype(BF16), preferred_element_type=F32)


def _mm_nt(a, b):
    return lax.dot_general(a.astype(BF16), b.astype(BF16), (((1,), (1,)), ((), ())),
                           preferred_element_type=F32)


def _mm_tn(a, b):
    return lax.dot_general(a.astype(BF16), b.astype(BF16), (((0,), (0,)), ((), ())),
                           preferred_element_type=F32)


def _split3(x):
    hi = x.astype(BF16)
    r = x - hi.astype(F32)
    mid = r.astype(BF16)
    lo = (r - mid.astype(F32)).astype(BF16)
    return hi, mid, lo


def _mm_exact(sel, x):
    hi, mid, lo = _split3(x)
    return (jnp.dot(sel, hi, preferred_element_type=F32) + jnp.dot(sel, mid, preferred_element_type=F32)
            + jnp.dot(sel, lo, preferred_element_type=F32))


def _mm_exact_lhs(x, sel, terms=3):
    return sum(jnp.dot(part, sel, preferred_element_type=F32) for part in _split3(x)[:terms])


def _silu(x):
    return x * jax.nn.sigmoid(x)


def _head_block_matrix(width, value):
    ri = lax.broadcasted_iota(jnp.int32, (width, width), 0) >> 6
    ci = lax.broadcasted_iota(jnp.int32, (width, width), 1) >> 6
    return jnp.where(ri == ci, value, 0.0).astype(BF16)


def _head_reduce(x, g):
    hi = x.astype(BF16)
    lo = (x - hi.astype(F32)).astype(BF16)
    return jnp.dot(hi, g, preferred_element_type=F32) + jnp.dot(lo, g, preferred_element_type=F32)


def _head_rms(x):
    ms = _head_reduce(x * x, _head_block_matrix(x.shape[1], 1.0 / HEAD_DIM))
    return x * lax.rsqrt(ms + EPS)


def _rope(x, tab_ref):
    w = x.shape[1]
    return (x * tab_ref[0] + pltpu.roll(x, w - 16, 1) * tab_ref[1] + pltpu.roll(x, 16, 1) * tab_ref[2])


def _attend(qs, parts):
    groups = range(len(qs))

    def score(q, part):
        k, _, bias, feature_major = part
        s = (_mm(q, k) if feature_major else _mm_nt(q, k)) * SCALE
        return s if bias is None else s + bias

    scores = [[score(qs[g], part) for part in parts[g]] for g in groups]
    m = [functools.reduce(jnp.maximum, [s.max(axis=-1, keepdims=True) for s in scores[g]]) for g in groups]
    p = [[jnp.exp(s - m[g]) for s in scores[g]] for g in groups]
    den = [sum(x.sum(axis=-1, keepdims=True) for x in p[g]) for g in groups]
    out = [sum(_mm_nt(x, part[1]) if part[3] else _mm(x, part[1]) for x, part in zip(p[g], parts[g]))
           for g in groups]
    return [out[g] / den[g] for g in groups]


def _hs(h):
    return slice(h * HEAD_DIM, (h + 1) * HEAD_DIM)


def _aligned(x, m):
    return x if isinstance(x, int) else pl.multiple_of(x, m)


def _mod_kernel(c_ref, w_ref, b_ref, o_ref):
    o_ref[...] = _mm(_silu(c_ref[...]), w_ref[...]) + b_ref[...]


def _mod_call(cond, w_ada, b_ada3):
    tn = 512
    rows = cond.shape[0]
    return pl.pallas_call(
        _mod_kernel,
        grid=(DEPTH, 3 * D_MODEL // tn),
        in_specs=[pl.BlockSpec((rows, D_MODEL), lambda l, j: (0, 0)),
                  pl.BlockSpec((None, D_MODEL, tn), lambda l, j: (l, 0, j)),
                  pl.BlockSpec((None, 1, tn), lambda l, j: (l, 0, j))],
        out_specs=pl.BlockSpec((None, rows, tn), lambda l, j: (l, 0, j)),
        out_shape=jax.ShapeDtypeStruct((DEPTH, rows, 3 * D_MODEL), F32),
        compiler_params=_params(24, 2),
        name="adaln_mod",
    )(cond, w_ada, b_ada3)


def _modulated_norm(x, mod, g):
    ms = jnp.mean(x * x, axis=-1, keepdims=True)
    y = x * lax.rsqrt(ms + EPS) * g
    return y * (1.0 + mod[:, D_MODEL:2 * D_MODEL]) + mod[:, :D_MODEL]


def _inproj_kernel(x_ref, mod_ref, g_ref, w_ref, o_ref, ab_ref):
    hn = _modulated_norm(x_ref[...], mod_ref[0], g_ref[...]).astype(BF16)
    tn = 512
    for j in range(PROJ_W // tn):
        y = jnp.dot(hn, w_ref[:, j * tn:(j + 1) * tn], preferred_element_type=F32)
        o_ref[:, j * tn:(j + 1) * tn] = y.astype(BF16)
        if j == OFF_GAB // tn:
            ab_ref[...] = y[:, OFF_GAB % tn:OFF_GAB % tn + 128]


def _inproj_call(x2d, mod3, norm_g3, wcat, layer, rows_per_mod):
    t = x2d.shape[0]
    tm = 512
    if mod3.shape[0] == 1:
        mod_idx = lambda i: (0, 0, 0)
    else:
        mod_idx = lambda i: ((i * tm) // rows_per_mod, 0, 0)
    return pl.pallas_call(
        _inproj_kernel,
        grid=(t // tm,),
        in_specs=[pl.BlockSpec((tm, D_MODEL), lambda i: (i, 0)),
                  pl.BlockSpec((1, 1, 3 * D_MODEL), mod_idx),
                  _layer_spec((1, D_MODEL), layer, 1),
                  pl.BlockSpec((None, D_MODEL, PROJ_W), lambda i: (layer, 0, 0), pipeline_mode=pl.Buffered(1))],
        out_specs=[pl.BlockSpec((tm, PROJ_W), lambda i: (i, 0)), pl.BlockSpec((tm, 128), lambda i: (i, 0))],
        out_shape=[jax.ShapeDtypeStruct((t, PROJ_W), BF16), jax.ShapeDtypeStruct((t, 128), F32)],
        compiler_params=_params(40, 1),
        name="inproj",
    )(x2d, mod3, norm_g3, wcat)


def _stacked_heads(q, n_kv):
    rep = N_HEADS // n_kv
    return [jnp.concatenate([q[:, _hs(g * rep + r)] for r in range(rep)], axis=0) for g in range(n_kv)]


def _unstack_heads(outs, n_kv):
    rep = N_HEADS // n_kv
    m = outs[0].shape[0] // rep
    return jnp.concatenate([outs[g][r * m:(r + 1) * m] for g in range(n_kv) for r in range(rep)], axis=-1)


def _write_layer(ref, layer, value, stacked):
    if not stacked:
        ref[...] = value
        return
    for l in range(ref.shape[0]):
        ref[l] = value if l == layer else jnp.zeros(value.shape, value.dtype)


def _write_state(st_ref, layer, stacked, piece):
    for d in range(2):
        for h in range(N_HEADS):
            value = piece(d, h)
            if stacked:
                for l in range(st_ref.shape[0]):
                    st_ref[l, d, h] = value if l == layer else jnp.zeros(value.shape, value.dtype)
            else:
                st_ref[d, h] = value


def _attn_ctx_kernel(*refs, n_kv, norm, layer, first):
    if norm:
        q_ref, k_ref, v_ref, z_ref, qn_ref, kn_ref = refs[:6]
    else:
        q_ref, k_ref, v_ref, z_ref = refs[:4]
    o_ref, kt_ref, vt_ref = refs[-3:]
    q, k, v, z = q_ref[...], k_ref[...].astype(F32), v_ref[...].astype(F32), z_ref[...].astype(F32)
    if norm:
        q = _head_rms(q.astype(F32)) * qn_ref[...]
        k = _head_rms(k) * kn_ref[...]
    seq = k.shape[0]
    _write_layer(kt_ref, layer, k.T.reshape(n_kv, HEAD_DIM, seq), first)
    _write_layer(vt_ref, layer, v.T.reshape(n_kv, HEAD_DIM, seq), first)
    outs = _attend(_stacked_heads(q, n_kv), [[(k[:, _hs(g)], v[:, _hs(g)], None, False)] for g in range(n_kv)])
    o_ref[...] = (_unstack_heads(outs, n_kv) * _silu(z)).astype(BF16)


def _attn_ctx_call(proj, prev, layer, batch, seq, off_q, off_k, off_v, off_z, n_kv, qn=None, kn=None):
    t = batch * seq
    kvw = n_kv * HEAD_DIM
    norm = qn is not None
    first = prev is None
    in_specs = [pl.BlockSpec((seq, BRANCH_W), lambda b: (b, off_q // BRANCH_W)),
                pl.BlockSpec((seq, kvw), lambda b: (b, off_k // kvw)),
                pl.BlockSpec((seq, kvw), lambda b: (b, off_v // kvw)),
                pl.BlockSpec((seq, BRANCH_W), lambda b: (b, off_z // BRANCH_W))]
    args = [proj, proj, proj, proj]
    if norm:
        in_specs += [_layer_spec((1, BRANCH_W), layer, 1), _layer_spec((1, kvw), layer, 1)]
        args += [qn, kn]
    aliases = {}
    if first:
        cache_spec = pl.BlockSpec((None, DEPTH, n_kv, HEAD_DIM, seq), lambda b: (b, 0, 0, 0, 0))
    else:
        aliases = {len(args): 1, len(args) + 1: 2}
        in_specs += [pl.BlockSpec(memory_space=pl.ANY)] * 2
        args += list(prev)
        cache_spec = pl.BlockSpec((None, None, n_kv, HEAD_DIM, seq), lambda b: (b, layer, 0, 0, 0))
    cache_shape = jax.ShapeDtypeStruct((batch, DEPTH, n_kv, HEAD_DIM, seq), F32)
    return pl.pallas_call(
        functools.partial(_attn_ctx_kernel, n_kv=n_kv, norm=norm, layer=layer, first=first),
        grid=(batch,), in_specs=in_specs,
        out_specs=[pl.BlockSpec((seq, BRANCH_W), lambda b: (b, 0)), cache_spec, cache_spec],
        out_shape=[jax.ShapeDtypeStruct((t, BRANCH_W), BF16), cache_shape, cache_shape],
        input_output_aliases=aliases,
        compiler_params=_params(32, 1),
        name="attn_ctx_norm" if norm else "attn_ctx",
    )(*args)


def _attn_lat_kernel(q_ref, kv_ref, z_ref, ckt_ref, cvt_ref, qtab_ref, ktab_ref, qn_ref, kn_ref, o_ref,
                     k_s, v_s):
    kw = KV_HEADS * HEAD_DIM

    @pl.when(pl.program_id(1) == 0)
    def _():
        kv = kv_ref[...]
        k_s[...] = _rope(_head_rms(kv[:, :kw].astype(F32)) * kn_ref[...], ktab_ref).astype(BF16)
        v_s[...] = kv[:, kw:]

    q = _rope(_head_rms(q_ref[...].astype(F32)) * qn_ref[...], qtab_ref)
    k, v = k_s[...], v_s[...]
    outs = _attend(_stacked_heads(q, KV_HEADS),
                   [[(k[:, _hs(g)], v[:, _hs(g)], None, False), (ckt_ref[g], cvt_ref[g], None, True)]
                    for g in range(KV_HEADS)])
    o_ref[...] = (_unstack_heads(outs, KV_HEADS) * _silu(z_ref[...].astype(F32))).astype(BF16)


def _attn_lat_call(proj, cache_kt, cache_vt, layer, batch, seq, qtab, ktab, qn, kn):
    tq = 256
    nq = seq // tq
    past = cache_kt.shape[-1]
    kw = KV_HEADS * HEAD_DIM
    ctx_spec = pl.BlockSpec((None, None, KV_HEADS, HEAD_DIM, past), lambda b, i: (b, layer, 0, 0, 0))
    return pl.pallas_call(
        _attn_lat_kernel,
        grid=(batch, nq),
        in_specs=[pl.BlockSpec((tq, BRANCH_W), lambda b, i: (b * nq + i, OFF_AQKV // BRANCH_W)),
                  pl.BlockSpec((seq, 2 * kw), lambda b, i: (b, (OFF_AQKV + BRANCH_W) // (2 * kw))),
                  pl.BlockSpec((tq, BRANCH_W), lambda b, i: (b * nq + i, OFF_AZ // BRANCH_W)),
                  ctx_spec, ctx_spec,
                  pl.BlockSpec((3, tq, BRANCH_W), lambda b, i: (0, i, 0)),
                  pl.BlockSpec((3, seq, kw), lambda b, i: (0, 0, 0)),
                  _layer_spec((1, BRANCH_W), layer, 2),
                  _layer_spec((1, kw), layer, 2)],
        out_specs=pl.BlockSpec((tq, BRANCH_W), lambda b, i: (b * nq + i, 0)),
        out_shape=jax.ShapeDtypeStruct((batch * seq, BRANCH_W), BF16),
        scratch_shapes=[pltpu.VMEM((seq, kw), BF16), pltpu.VMEM((seq, kw), BF16)],
        compiler_params=_params(40, 2),
        name="attn_lat",
    )(proj, proj, proj, cache_kt, cache_vt, qtab, ktab, qn, kn)


def _na_bias_kernel(t_ref, o_ref):
    nblk = o_ref.shape[0]
    c = lax.broadcasted_iota(jnp.int32, (GRID_W, 2 * GRID_W), 0)
    j = lax.broadcasted_iota(jnp.int32, (GRID_W, 2 * GRID_W), 1)
    kc = j & (GRID_W - 1)
    dc = kc - c + (NA_COLS - 1)
    cs = jnp.clip(c - NA_COLS // 2, 0, GRID_W - NA_COLS)
    valid = jnp.logical_and(kc >= cs, kc < cs + NA_COLS)
    left = j < GRID_W

    def body(b, carry):
        b2 = jnp.minimum(b + 1, nblk - 1)
        acc = jnp.full((GRID_W, 2 * GRID_W), NEG_INF, F32)
        for i in range(N_DC):
            acc = jnp.where(dc == i, jnp.where(left, t_ref[b * N_DC + i], t_ref[b2 * N_DC + i]), acc)
        o_ref[b] = jnp.where(valid, acc, NEG_INF)
        return carry

    lax.fori_loop(0, nblk, body, 0)


def _na_bias_call(na_bias):
    nblk = DEPTH * N_HEADS * N_DR
    return pl.pallas_call(
        _na_bias_kernel,
        in_specs=[pl.BlockSpec(memory_space=pltpu.SMEM)],
        out_specs=pl.BlockSpec((nblk, GRID_W, 2 * GRID_W), lambda: (0, 0, 0)),
        out_shape=jax.ShapeDtypeStruct((nblk, GRID_W, 2 * GRID_W), F32),
        name="na_bias",
    )(na_bias.reshape(-1))


def _na_kernel(q_ref, k_ref, v_ref, z_ref, ckt_ref, cvt_ref, tb_ref, o_ref, *, rows):
    win = NA_ROWS * GRID_W
    r = pl.program_id(1)
    rs = jnp.clip(r - NA_ROWS // 2, 0, rows - NA_ROWS)
    r0 = pl.multiple_of(rs * GRID_W, GRID_W)
    kwin = k_ref[pl.ds(r0, win), :]
    vwin = v_ref[pl.ds(r0, win), :]
    q = q_ref[...]
    dr0 = rs - r + NA_ROWS - 1
    bias = [jnp.concatenate([tb_ref[h * N_DR + dr0 + 2 * p] for p in range(NA_ROWS // 2)], axis=1)
            for h in range(N_HEADS)]
    outs = _attend([q[:, _hs(h)] for h in range(N_HEADS)],
                   [[(kwin[:, _hs(h)], vwin[:, _hs(h)], bias[h], False), (ckt_ref[h], cvt_ref[h], None, True)]
                    for h in range(N_HEADS)])
    o_ref[...] = (jnp.concatenate(outs, axis=-1) * _silu(z_ref[...].astype(F32))).astype(BF16)


def _na_call(proj, cache_kt, cache_vt, tb, layer, batch, seq):
    rows = seq // GRID_W
    assert rows >= NA_ROWS
    past = cache_kt.shape[-1]
    nblk = N_HEADS * N_DR
    cq = OFF_D // BRANCH_W
    ctx_spec = pl.BlockSpec((None, None, N_HEADS, HEAD_DIM, past), lambda b, r: (b, layer, 0, 0, 0))
    return pl.pallas_call(
        functools.partial(_na_kernel, rows=rows),
        grid=(batch, rows),
        in_specs=[pl.BlockSpec((GRID_W, BRANCH_W), lambda b, r: (b * rows + r, cq)),
                  pl.BlockSpec((seq, BRANCH_W), lambda b, r: (b, cq + 1)),
                  pl.BlockSpec((seq, BRANCH_W), lambda b, r: (b, cq + 2)),
                  pl.BlockSpec((GRID_W, BRANCH_W), lambda b, r: (b * rows + r, cq + 3)),
                  ctx_spec, ctx_spec,
                  pl.BlockSpec((nblk, GRID_W, 2 * GRID_W), lambda b, r: (layer, 0, 0))],
        out_specs=pl.BlockSpec((GRID_W, BRANCH_W), lambda b, r: (b * rows + r, 0)),
        out_shape=jax.ShapeDtypeStruct((batch * seq, BRANCH_W), BF16),
        compiler_params=_params(32, 2),
        name="na_lat",
    )(proj, proj, proj, proj, cache_kt, cache_vt, tb)


def _gdn_kernel(*refs, seq, has_s0, layer, emit):
    qkv_ref, z_ref, ab_ref, cw_ref, par_ref, g_ref = refs[:6]
    s0_ref = refs[6] if has_s0 else None
    (q_s, k_s, v_s, gcb_s, bcb_s, r_s, mc_s, nc_s, qp_s, op_s, egl_s, s_s, oacc_ref) = refs[-13:]
    if emit == "none":
        o_ref, st_ref = refs[-14], None
    else:
        o_ref, st_ref = refs[-15], refs[-14]
    n_chunks = seq // CHUNK
    n_levels = CHUNK.bit_length() - 1
    qkv_w = 3 * BRANCH_W
    half = SHORT_CONV // 2
    pair_w = 2 * HEAD_DIM
    pairs = [slice(p * pair_w, (p + 1) * pair_w) for p in range(BRANCH_W // pair_w)]
    tr = 256
    cpt = tr // CHUNK
    head_sum = _head_block_matrix(BRANCH_W, 1.0)

    gc_i = lax.broadcasted_iota(jnp.int32, (128, BRANCH_W), 0)
    gh_j = lax.broadcasted_iota(jnp.int32, (128, BRANCH_W), 1) >> 6
    sel_beta = [jnp.where(gc_i == gh_j + 4 * d, 1.0, 0.0).astype(BF16) for d in range(2)]
    sel_gate = [jnp.where(gc_i == gh_j + 8 + 4 * d, 1.0, 0.0).astype(BF16) for d in range(2)]
    ti = lax.broadcasted_iota(jnp.int32, (tr, tr), 0)
    tj = lax.broadcasted_iota(jnp.int32, (tr, tr), 1)
    same_chunk = (ti >> 6) == (tj >> 6)
    tri = [jnp.where(jnp.logical_and(same_chunk, ti >= tj), 1.0, 0.0).astype(BF16),
           jnp.where(jnp.logical_and(same_chunk, ti <= tj), 1.0, 0.0).astype(BF16)]
    assert tr == BRANCH_W
    lane_head = lax.broadcasted_iota(jnp.int32, (1, BRANCH_W), 1) >> 6

    halo = 16
    edge = 8
    assert half <= edge
    si = lax.broadcasted_iota(jnp.int32, (tr, tr), 0)
    sj = lax.broadcasted_iota(jnp.int32, (tr, tr), 1)
    ei = lax.broadcasted_iota(jnp.int32, (edge, halo), 0)
    ej = lax.broadcasted_iota(jnp.int32, (edge, halo), 1)
    taps = [j for j in range(SHORT_CONV) if j != half]
    shift = {j: jnp.where(sj == si + (j - half), 1.0, 0.0).astype(BF16) for j in taps}
    shift_before = {j: jnp.where(ej == ei + (halo + j - half), 1.0, 0.0).astype(BF16) for j in taps if j < half}
    shift_after = {j: jnp.where(ej == ei + (j - half - edge), 1.0, 0.0).astype(BF16) for j in taps if j > half}
    for t in range(seq // tr):
        rows = slice(t * tr, (t + 1) * tr)
        x = qkv_ref[rows, :]
        y = x.astype(F32) * cw_ref[half:half + 1, :]
        for j in taps:
            y = y + jnp.dot(shift[j], x, preferred_element_type=F32) * cw_ref[j:j + 1, :]
        if t > 0:
            before = qkv_ref[t * tr - halo:t * tr, :]
            top = sum(jnp.dot(shift_before[j], before, preferred_element_type=F32) * cw_ref[j:j + 1, :]
                      for j in shift_before)
            y = jnp.concatenate([y[:edge] + top, y[edge:]], axis=0)
        if (t + 1) * tr < seq:
            after = qkv_ref[(t + 1) * tr:(t + 1) * tr + halo, :]
            bottom = sum(jnp.dot(shift_after[j], after, preferred_element_type=F32) * cw_ref[j:j + 1, :]
                         for j in shift_after)
            y = jnp.concatenate([y[:tr - edge], y[tr - edge:] + bottom], axis=0)
        y = _silu(y)
        qq, kk = y[:, :BRANCH_W], y[:, BRANCH_W:2 * BRANCH_W]
        q_s[rows, :] = qq * lax.rsqrt(_head_reduce(qq * qq, head_sum) + EPS) * SCALE
        k_s[rows, :] = kk * lax.rsqrt(_head_reduce(kk * kk, head_sum) + EPS)
        v_s[rows, :] = y[:, 2 * BRANCH_W:]
        x = ab_ref[rows, :]
        beta = jax.nn.sigmoid(x)
        xs = x + par_ref[0:1, :]
        softplus = jnp.maximum(xs, 0.0) + jnp.log1p(jnp.exp(-jnp.abs(xs)))
        la = -jnp.exp(par_ref[1:2, :]) * softplus
        for d in range(2):
            gc = _mm_exact(tri[d], la)
            gcb_s[d, rows, :] = _mm_exact_lhs(gc, sel_gate[d])
            bcb_s[d, rows, :] = _mm_exact_lhs(beta, sel_beta[d], terms=2)
            gt = gc.T[8:16, :]
            shifted = {s: (gt if s == 0 else pltpu.roll(gt, (s * HEAD_DIM) % tr, 1))
                       for s in range(1 - cpt, N_HEADS)}
            for c in range(cpt):
                r = jnp.zeros((1, BRANCH_W), F32)
                for h in range(N_HEADS):
                    r = jnp.where(lane_head == h, shifted[h - c][4 * d + h:4 * d + h + 1, :], r)
                r_s[d, (t * cpt + c) * 8:(t * cpt + c + 1) * 8, :] = jnp.broadcast_to(r, (8, BRANCH_W))

    for d in range(2):
        if has_s0:
            s_s[d] = jnp.concatenate([s0_ref[d, h] for h in range(N_HEADS)], axis=-1)
        else:
            s_s[d] = jnp.zeros((HEAD_DIM, BRANCH_W), F32)

    li = lax.broadcasted_iota(jnp.int32, (CHUNK, BRANCH_W), 0)
    lj = lax.broadcasted_iota(jnp.int32, (CHUNK, BRANCH_W), 1) & (HEAD_DIM - 1)
    incl = (li >= lj, li <= lj)
    strict = (li > lj, li < lj)
    level = [((li ^ lj) >> l) == 1 for l in range(n_levels)]
    first_head = lax.broadcasted_iota(jnp.int32, (CHUNK, pair_w), 1) < HEAD_DIM

    def expand(y):
        yb = y.astype(BF16)
        zero = jnp.zeros((CHUNK, pair_w), BF16)
        return [jnp.concatenate([jnp.where(first_head, yb[:, p], zero), jnp.where(first_head, zero, yb[:, p])],
                                axis=0) for p in pairs]

    def bdmm(x, ybd):
        xb = x.astype(BF16)
        return jnp.concatenate([jnp.dot(xb[:, p], ybd[i], preferred_element_type=F32)
                                for i, p in enumerate(pairs)], axis=1)

    def bdmm_nt(x, ybd):
        xb = x.astype(BF16)
        return jnp.concatenate([lax.dot_general(xb[:, p], ybd[i], (((1,), (1,)), ((), ())),
                                                preferred_element_type=F32)
                                for i, p in enumerate(pairs)], axis=1)

    def tn_diag(a, b):
        ab, bb = a.astype(BF16), b.astype(BF16)
        outs = []
        for p in pairs:
            full = lax.dot_general(ab[:, p], bb[:, p], (((0,), (0,)), ((), ())), preferred_element_type=F32)
            outs.append(jnp.where(first_head, full[:HEAD_DIM], full[HEAD_DIM:]))
        return jnp.concatenate(outs, axis=1)

    def prepare(chains):
        n = range(len(chains))
        dd = [d for d, _ in chains]
        rows = [pl.ds(_aligned(c * CHUNK, CHUNK), CHUNK) for _, c in chains]
        gcb = [gcb_s[dd[i], rows[i], :] for i in n]
        bcb = [bcb_s[dd[i], rows[i], :] for i in n]
        grow = [r_s[dd[i], pl.ds(_aligned(chains[i][1] * 8, 8), 8), :][0:1, :] for i in n]
        dm = [jnp.exp(jnp.where(incl[dd[i]], gcb[i] - grow[i], NEG_INF)) for i in n]
        k = [k_s[rows[i], :] for i in n]
        q = [q_s[rows[i], :] for i in n]
        v = [v_s[rows[i], :] for i in n]
        kq = [bdmm_nt(jnp.concatenate([k[i], q[i]], axis=0), expand(k[i])) for i in n]
        a = [jnp.where(strict[dd[i]], bcb[i] * kq[i][:CHUNK] * dm[i], 0.0) for i in n]
        tm = [-jnp.where(level[0], a[i], 0.0) for i in n]
        for l in range(1, n_levels):
            b = [jnp.where(level[l], a[i], 0.0) for i in n]
            y = [b[i] + bdmm(tm[i], expand(b[i])) for i in n]
            tm = [tm[i] - (y[i] + bdmm(y[i], expand(tm[i]))) for i in n]
        eg = [jnp.exp(gcb[i]) for i in n]
        bv = [bcb[i] * v[i] for i in n]
        bk = [bcb[i] * k[i] * eg[i] for i in n]
        u = [bv[i] + bdmm(tm[i], expand(bv[i])) for i in n]
        w = [bk[i] + bdmm(tm[i], expand(bk[i])) for i in n]
        gl = [gcb[i][CHUNK - 1:CHUNK, :] if dd[i] == 0 else gcb[i][0:1, :] for i in n]
        kd = [k[i] * jnp.exp(gl[i] - gcb[i]) for i in n]
        qkm = [kq[i][CHUNK:] * dm[i] for i in n]
        mc = [tn_diag(kd[i], w[i]) for i in n]
        nc = [tn_diag(kd[i], u[i]) for i in n]
        qp = [q[i] * eg[i] - bdmm(qkm[i], expand(w[i])) for i in n]
        op = [bdmm(qkm[i], expand(u[i])) for i in n]
        for i in n:
            d, c = chains[i]
            mc_s[d, rows[i], :] = mc[i].astype(BF16)
            nc_s[d, rows[i], :] = nc[i]
            qp_s[d, rows[i], :] = qp[i].astype(BF16)
            op_s[d, rows[i], :] = op[i]
            egl_s[d, pl.ds(_aligned(c * 8, 8), 8), :] = jnp.broadcast_to(jnp.exp(gl[i]), (8, BRANCH_W))

    group = min(PREP_CHUNKS, n_chunks)
    if n_chunks == group:
        prepare([(d, c) for c in range(group) for d in range(2)])
    else:
        def prep_body(j, carry):
            prepare([(d, j * group + c) for c in range(group) for d in range(2)])
            return carry
        lax.fori_loop(0, n_chunks // group, prep_body, 0)

    def scan_step(i):
        for d, c in ((0, i), (1, n_chunks - 1 - i)):
            rows = pl.ds(_aligned(c * CHUNK, CHUNK), CHUNK)
            s = s_s[d]
            sbd = expand(s)
            oacc_ref[d, rows, :] = bdmm(qp_s[d, rows, :], sbd) + op_s[d, rows, :]
            egl = egl_s[d, pl.ds(_aligned(c * 8, 8), 8), :][0:1, :]
            s_s[d] = s * egl - bdmm(mc_s[d, rows, :], sbd) + nc_s[d, rows, :]

    unroll = min(SCAN_UNROLL, n_chunks)

    def scan_body(j, carry):
        for i in range(unroll):
            scan_step(j * unroll + i)
        return carry

    lax.fori_loop(0, n_chunks // unroll, scan_body, 0)

    if st_ref is not None:
        _write_state(st_ref, layer, emit == "first", lambda d, h: s_s[d][:, _hs(h)])
    o = oacc_ref[0] + oacc_ref[1]
    ms = _head_reduce(o * o, _head_block_matrix(BRANCH_W, 1.0 / HEAD_DIM))
    o_ref[...] = (o * lax.rsqrt(ms + EPS) * g_ref[...] * _silu(z_ref[...].astype(F32))).astype(BF16)


def _state_spec(layer):
    return pl.BlockSpec((None, None, 2, N_HEADS, HEAD_DIM, HEAD_DIM), lambda b: (b, layer, 0, 0, 0, 0))


def _state_output(emit, prev, layer, batch, n_args):
    if emit == "none":
        return [], [], [], [], {}
    shape = jax.ShapeDtypeStruct((batch, DEPTH, 2, N_HEADS, HEAD_DIM, HEAD_DIM), F32)
    if emit == "first":
        spec = pl.BlockSpec((None, DEPTH, 2, N_HEADS, HEAD_DIM, HEAD_DIM), lambda b: (b, 0, 0, 0, 0, 0))
        return [], [], [spec], [shape], {}
    return [pl.BlockSpec(memory_space=pl.ANY)], [prev], [_state_spec(layer)], [shape], {n_args: 1}


def _gdn_call(proj, gab, cw8, par, norm_g, layer, batch, seq, s0=None, emit="none", prev=None):
    has_s0 = s0 is not None
    qkv_w = 3 * BRANCH_W
    in_specs = [pl.BlockSpec((seq, qkv_w), lambda b: (b, OFF_GQKV // qkv_w)),
                pl.BlockSpec((seq, BRANCH_W), lambda b: (b, OFF_GZ // BRANCH_W)),
                pl.BlockSpec((seq, 128), lambda b: (b, 0)),
                _layer_spec((8, qkv_w), layer, 1),
                _layer_spec((2, 128), layer, 1),
                _layer_spec((1, BRANCH_W), layer, 1)]
    args = [proj, proj, gab, cw8, par, norm_g]
    if has_s0:
        in_specs.append(_state_spec(layer))
        args.append(s0)
    st_in_specs, st_args, st_out_specs, st_shapes, aliases = _state_output(emit, prev, layer, batch, len(args))
    return pl.pallas_call(
        functools.partial(_gdn_kernel, seq=seq, has_s0=has_s0, layer=layer, emit=emit),
        grid=(batch,), in_specs=in_specs + st_in_specs,
        out_specs=[pl.BlockSpec((seq, BRANCH_W), lambda b: (b, 0))] + st_out_specs,
        out_shape=[jax.ShapeDtypeStruct((batch * seq, BRANCH_W), BF16)] + st_shapes,
        input_output_aliases=aliases,
        scratch_shapes=[pltpu.VMEM((seq, BRANCH_W), F32),
                        pltpu.VMEM((seq, BRANCH_W), F32),
                        pltpu.VMEM((seq, BRANCH_W), F32),
                        pltpu.VMEM((2, seq, BRANCH_W), F32),
                        pltpu.VMEM((2, seq, BRANCH_W), F32),
                        pltpu.VMEM((2, seq // CHUNK * 8, BRANCH_W), F32),
                        pltpu.VMEM((2, seq, BRANCH_W), BF16),
                        pltpu.VMEM((2, seq, BRANCH_W), F32),
                        pltpu.VMEM((2, seq, BRANCH_W), BF16),
                        pltpu.VMEM((2, seq, BRANCH_W), F32),
                        pltpu.VMEM((2, seq // CHUNK * 8, BRANCH_W), F32),
                        pltpu.VMEM((2, HEAD_DIM, BRANCH_W), F32),
                        pltpu.VMEM((2, seq, BRANCH_W), F32)],
        compiler_params=_params(48, 1),
        name="gdn",
    )(*args, *st_args)


_RET_LOG_GAMMA = [[float(np.log1p(-np.exp2(-(base + h)))) for h in range(N_HEADS)] for base in RET_DECAY_BASE]


def _ret_kernel(*refs, seq, has_s0, layer, emit):
    qkv_ref, z_ref, g_ref = refs[:3]
    s0_ref = refs[3] if has_s0 else None
    o_ref, st_ref = (refs[-1], None) if emit == "none" else (refs[-2], refs[-1])
    tq = 256
    heads = range(N_HEADS)
    lgf, lgb = _RET_LOG_GAMMA
    ks = [qkv_ref[:, BRANCH_W + h * HEAD_DIM:BRANCH_W + (h + 1) * HEAD_DIM] for h in heads]
    vs = [qkv_ref[:, 2 * BRANCH_W + h * HEAD_DIM:2 * BRANCH_W + (h + 1) * HEAD_DIM] for h in heads]
    if st_ref is not None:
        tcol = lax.broadcasted_iota(jnp.int32, (seq, 1), 0).astype(F32)
        stf = [_mm_tn(ks[h] * jnp.exp((seq - 1.0 - tcol) * lgf[h]), vs[h]) for h in heads]
        stb = [_mm_tn(ks[h] * jnp.exp(tcol * lgb[h]), vs[h]) for h in heads]
        if has_s0:
            stf = [stf[h] + float(np.exp(seq * lgf[h])) * s0_ref[0, h] for h in heads]
            stb = [stb[h] + float(np.exp(seq * lgb[h])) * s0_ref[1, h] for h in heads]
        _write_state(st_ref, layer, emit == "first", lambda d, h: (stf, stb)[d][h])
    tiles = []
    for t in range(seq // tq):
        qs = [qkv_ref[t * tq:(t + 1) * tq, _hs(h)] * SCALE for h in heads]
        qk = [_mm_nt(qs[h], ks[h]) for h in heads]
        di = (lax.broadcasted_iota(jnp.int32, (tq, seq), 0) + t * tq
              - lax.broadcasted_iota(jnp.int32, (tq, seq), 1)).astype(F32)
        dm = [jnp.where(di == 0, 2.0, jnp.exp(di * jnp.where(di >= 0, lgf[h], -lgb[h]))) for h in heads]
        o = [_mm(qk[h] * dm[h], vs[h]) for h in heads]
        if has_s0:
            tt = lax.broadcasted_iota(jnp.int32, (tq, 1), 0).astype(F32) + float(t * tq)
            o = [o[h] + _mm(qs[h] * jnp.exp((tt + 1.0) * lgf[h]), s0_ref[0, h])
                 + _mm(qs[h] * jnp.exp((seq - tt) * lgb[h]), s0_ref[1, h]) for h in heads]
        ms = [jnp.mean(o[h] * o[h], axis=-1, keepdims=True) for h in heads]
        tiles.append(jnp.concatenate([o[h] * lax.rsqrt(ms[h] + EPS) * g_ref[...] for h in heads], axis=-1))
    o_ref[...] = (jnp.concatenate(tiles, axis=0) * _silu(z_ref[...].astype(F32))).astype(BF16)


def _ret_call(proj, norm_g, layer, batch, seq, s0=None, emit="none", prev=None):
    has_s0 = s0 is not None
    qkv_w = 3 * BRANCH_W
    in_specs = [pl.BlockSpec((seq, qkv_w), lambda b: (b, OFF_C // qkv_w)),
                pl.BlockSpec((seq, BRANCH_W), lambda b: (b, (OFF_C + qkv_w) // BRANCH_W)),
                _layer_spec((1, HEAD_DIM), layer, 1)]
    args = [proj, proj, norm_g]
    if has_s0:
        in_specs.append(_state_spec(layer))
        args.append(s0)
    st_in_specs, st_args, st_out_specs, st_shapes, aliases = _state_output(emit, prev, layer, batch, len(args))
    return pl.pallas_call(
        functools.partial(_ret_kernel, seq=seq, has_s0=has_s0, layer=layer, emit=emit),
        grid=(batch,), in_specs=in_specs + st_in_specs,
        out_specs=[pl.BlockSpec((seq, BRANCH_W), lambda b: (b, 0))] + st_out_specs,
        out_shape=[jax.ShapeDtypeStruct((batch * seq, BRANCH_W), BF16)] + st_shapes,
        input_output_aliases=aliases,
        compiler_params=_params(48, 1),
        name="retention",
    )(*args, *st_args)


def _out_kernel(*refs, final):
    if final:
        (h_ref, mod_ref, g_ref, oa_ref, ob_ref, oc_ref, od_ref, wg_ref, wb_ref, wo_ref, fn_ref,
         o_ref, y_ref) = refs
    else:
        h_ref, mod_ref, g_ref, oa_ref, ob_ref, oc_ref, od_ref, wg_ref, wb_ref, wo_ref, o_ref = refs
    x = h_ref[...]
    mod = mod_ref[0]
    hn = _modulated_norm(x, mod, g_ref[...]).astype(BF16)
    merged = None
    for n, br_ref in enumerate((oa_ref, ob_ref, oc_ref, od_ref)):
        gate = jax.nn.sigmoid(jnp.dot(hn, wg_ref[:, n * D_MODEL:(n + 1) * D_MODEL], preferred_element_type=F32))
        up = jnp.dot(br_ref[...], wb_ref[n], preferred_element_type=F32)
        merged = gate * up if merged is None else merged + gate * up
    out = jnp.dot(merged.astype(BF16), wo_ref[...], preferred_element_type=F32)
    hnew = x + mod[:, 2 * D_MODEL:] * out
    o_ref[...] = hnew
    if final:
        ms = jnp.mean(hnew * hnew, axis=-1, keepdims=True)
        y_ref[...] = hnew * lax.rsqrt(ms + EPS) * fn_ref[...]


def _out_call(h2d, mod3, norm_g3, branches, wg, wb, wo, layer, rows_per_mod, final_norm=None):
    t = h2d.shape[0]
    tm = 512
    final = final_norm is not None
    if mod3.shape[0] == 1:
        mod_idx = lambda i: (0, 0, 0)
    else:
        mod_idx = lambda i: ((i * tm) // rows_per_mod, 0, 0)
    once = pl.Buffered(1)
    in_specs = [pl.BlockSpec((tm, D_MODEL), lambda i: (i, 0)),
                pl.BlockSpec((1, 1, 3 * D_MODEL), mod_idx),
                _layer_spec((1, D_MODEL), layer, 1)]
    in_specs += [pl.BlockSpec((tm, BRANCH_W), lambda i: (i, 0))] * N_BRANCH
    in_specs += [pl.BlockSpec((None, D_MODEL, N_BRANCH * D_MODEL), lambda i: (layer, 0, 0), pipeline_mode=once),
                 pl.BlockSpec((None, N_BRANCH, BRANCH_W, D_MODEL), lambda i: (layer, 0, 0, 0), pipeline_mode=once),
                 pl.BlockSpec((None, D_MODEL, D_MODEL), lambda i: (layer, 0, 0), pipeline_mode=once)]
    args = [h2d, mod3, norm_g3, *branches, wg, wb, wo]
    out_specs = [pl.BlockSpec((tm, D_MODEL), lambda i: (i, 0))]
    out_shape = [jax.ShapeDtypeStruct((t, D_MODEL), F32)]
    if final:
        in_specs.append(pl.BlockSpec((1, D_MODEL), lambda i: (0, 0)))
        args.append(final_norm.reshape(1, D_MODEL))
        out_specs.append(pl.BlockSpec((tm, D_MODEL), lambda i: (i, 0)))
        out_shape.append(jax.ShapeDtypeStruct((t, D_MODEL), F32))
    return pl.pallas_call(
        functools.partial(_out_kernel, final=final),
        grid=(t // tm,), in_specs=in_specs, out_specs=out_specs, out_shape=out_shape,
        compiler_params=_params(48, 1),
        name="merge_out_final" if final else "merge_out",
    )(*args)


def _prep_weights(w_in):
    offs = np.concatenate([[0], np.cumsum(IN_SPLITS)])
    seg = lambda i, j: w_in[:, :, offs[i]:offs[j]]
    pad = jnp.zeros((DEPTH, D_MODEL, PROJ_W - OFF_GAB - IN_SPLITS[5]), w_in.dtype)
    wcat = jnp.concatenate([seg(7, 11), seg(11, 15), seg(6, 7), seg(4, 5), seg(0, 3), seg(3, 4), seg(5, 6), pad],
                           axis=2).astype(BF16)
    return wcat, seg(15, 16).astype(BF16)


def _rope_tables(seq):
    t = jnp.arange(seq)
    quarter = HEAD_DIM // 4
    inv = ROPE_THETA ** (-jnp.arange(quarter, dtype=F32) / quarter)

    def half(pos):
        ang = pos.astype(F32)[:, None] * inv
        c, s, zero = jnp.cos(ang), jnp.sin(ang), jnp.zeros_like(ang)
        return jnp.concatenate([c, c], -1), jnp.concatenate([-s, zero], -1), jnp.concatenate([zero, s], -1)

    parts = [jnp.concatenate([a, b], -1) for a, b in zip(half(t // GRID_W), half(t % GRID_W))]
    tab = jnp.stack(parts)
    return jnp.tile(tab, (1, 1, N_HEADS)), jnp.tile(tab, (1, 1, KV_HEADS))


def _layer(h2d, batch, seq, mod, pw, layer, ctx, caches, final_norm):
    proj, gab = _inproj_call(h2d, mod, pw["norm_g"], pw["wcat"], layer, seq)
    kw = KV_HEADS * HEAD_DIM
    if ctx is None:
        emit = "first" if caches is None else "update"
        akv, nkv, sg_all, sr_all = caches or (None, None, None, None)
        oa, *akv = _attn_ctx_call(proj, akv, layer, batch, seq, OFF_AQKV, OFF_AQKV + BRANCH_W,
                                  OFF_AQKV + BRANCH_W + kw, OFF_AZ, KV_HEADS, pw["qn"], pw["kn"])
        od, *nkv = _attn_ctx_call(proj, nkv, layer, batch, seq, OFF_D, OFF_D + BRANCH_W,
                                  OFF_D + 2 * BRANCH_W, OFF_D + 3 * BRANCH_W, N_HEADS)
        ob, sg_all = _gdn_call(proj, gab, pw["cw8"], pw["gdn_par"], pw["gdn_norm"], layer, batch, seq,
                               emit=emit, prev=sg_all)
        oc, sr_all = _ret_call(proj, pw["ret_norm"], layer, batch, seq, emit=emit, prev=sr_all)
        caches = (akv, nkv, sg_all, sr_all)
    else:
        oa = _attn_lat_call(proj, ctx["akt"], ctx["avt"], layer, batch, seq, ctx["qtab"], ctx["ktab"],
                            pw["qn"], pw["kn"])
        od = _na_call(proj, ctx["nkt"], ctx["nvt"], ctx["tb"], layer, batch, seq)
        ob, = _gdn_call(proj, gab, pw["cw8"], pw["gdn_par"], pw["gdn_norm"], layer, batch, seq, s0=ctx["sg"])
        oc, = _ret_call(proj, pw["ret_norm"], layer, batch, seq, s0=ctx["sr"])
    outs = _out_call(h2d, mod, pw["norm_g"], (oa, ob, oc, od), pw["wg"], pw["wb"], pw["wo"], layer, seq, final_norm)
    return outs, caches


def kernel(x_prompt, x_sample, cache_attn_k, cache_attn_v, cache_na_k, cache_na_v, state_gdn, state_ret, c, c_ctx, w_ada, b_ada, norm_g, w_in, conv_w, gdn_a_log, gdn_dt_bias, gdn_norm, attn_q_norm, attn_k_norm, ret_norm, na_bias, w_branch, w_out, final_norm):
    batch, seq, _ = x_prompt.shape
    dbatch, dseq, _ = x_sample.shape
    assert dbatch == 8, "the modulation kernel handles exactly one sublane tile of conditioning rows"

    wcat, wg = _prep_weights(w_in)
    par = jnp.zeros((DEPTH, 2, 128), F32)
    par = par.at[:, 0, 8:16].set(gdn_dt_bias.reshape(DEPTH, 8)).at[:, 1, 8:16].set(gdn_a_log.reshape(DEPTH, 8))
    pw = dict(
        w_ada=w_ada, b_ada=b_ada.reshape(DEPTH, 1, 3 * D_MODEL), norm_g=norm_g.reshape(DEPTH, 1, D_MODEL),
        wcat=wcat, wg=wg, wb=w_branch.astype(BF16), wo=w_out.astype(BF16),
        cw8=jnp.concatenate([conv_w, jnp.zeros((DEPTH, 8 - SHORT_CONV, 3 * BRANCH_W), F32)], axis=1),
        gdn_par=par,
        gdn_norm=jnp.tile(gdn_norm, (1, N_HEADS)).reshape(DEPTH, 1, BRANCH_W),
        ret_norm=ret_norm.reshape(DEPTH, 1, HEAD_DIM),
        qn=jnp.tile(attn_q_norm, (1, N_HEADS)).reshape(DEPTH, 1, BRANCH_W),
        kn=jnp.tile(attn_k_norm, (1, KV_HEADS)).reshape(DEPTH, 1, KV_HEADS * HEAD_DIM))

    cond = jnp.concatenate([jnp.broadcast_to(c_ctx, (8, D_MODEL)), c], axis=0)
    mods = _mod_call(cond, w_ada, pw["b_ada"])

    h = x_prompt.reshape(batch * seq, D_MODEL)
    caches = None
    for l in range(DEPTH):
        outs, caches = _layer(h, batch, seq, mods[l, 0:1].reshape(1, 1, 3 * D_MODEL), pw, l, None, caches,
                              final_norm if l == DEPTH - 1 else None)
        h = outs[0]
    y_prompt = outs[1].reshape(batch, seq, D_MODEL)
    token_major = lambda a: a.transpose(0, 1, 4, 2, 3)
    (akt, avt), (nkt, nvt), new_state_gdn, new_state_ret = caches
    new_attn_k, new_attn_v, new_na_k, new_na_v = (token_major(a) for a in (akt, avt, nkt, nvt))

    qtab, ktab = _rope_tables(dseq)
    feature_major = lambda a: a.transpose(0, 1, 3, 4, 2)
    ctx = dict(akt=feature_major(cache_attn_k), avt=feature_major(cache_attn_v),
               nkt=feature_major(cache_na_k), nvt=feature_major(cache_na_v),
               sg=state_gdn, sr=state_ret, tb=_na_bias_call(na_bias), qtab=qtab, ktab=ktab)
    h = x_sample.reshape(dbatch * dseq, D_MODEL)
    for l in range(DEPTH):
        outs, _ = _layer(h, dbatch, dseq, mods[l, 8:16].reshape(dbatch, 1, 3 * D_MODEL), pw, l, ctx, None,
                         final_norm if l == DEPTH - 1 else None)
        h = outs[0]
    y_sample = outs[1].reshape(dbatch, dseq, D_MODEL)
    return (y_prompt, y_sample, new_attn_k, new_attn_v, new_na_k, new_na_v, new_state_gdn, new_state_ret)
```

```python
import functools

import numpy as np
import jax
import jax.numpy as jnp
from jax import lax
from jax.experimental import pallas as pl
from jax.experimental.pallas import tpu as pltpu

F32 = jnp.float32
BF16 = jnp.bfloat16

D_MODEL = 1024
HEAD_DIM = 64
N_HEADS = 4
KV_HEADS = N_HEADS // 2
BRANCH_W = N_HEADS * HEAD_DIM
N_BRANCH = 4
DEPTH = 2
GRID_W = 64
CHUNK = 64
PREP_CHUNKS = 4
SCAN_UNROLL = 4
assert CHUNK == HEAD_DIM
SHORT_CONV = 5
NA_ROWS = 8
NA_COLS = 16
CTX_SEQS_PER_STEP = 2
NA_ROWS_PER_STEP = 4
N_DR = 2 * NA_ROWS - 1
N_DC = 2 * NA_COLS - 1
ROPE_THETA = 10000.0
RET_DECAY_BASE = (5.0, 5.5)
EPS = 1e-6
SCALE = HEAD_DIM ** -0.5
LOG2E = 1.4426950408889634
NEG_INF = float("-inf")

IN_SPLITS = (256, 128, 128, 256, 768, 16, 256, 256, 256, 256, 256, 256, 256, 256, 256, 4096)
PROJ_W = 4096
OFF_C = 0
OFF_D = 1024
OFF_GZ = 2048
OFF_GQKV = 2304
OFF_AQKV = 3072
OFF_AZ = 3584
OFF_GAB = 3840

V7X_VMEM_BYTES = 64 * 1024 * 1024
MIB = 1024 * 1024


def _params(vmem_mib, n_axes):
    assert vmem_mib * MIB < V7X_VMEM_BYTES
    return pltpu.CompilerParams(dimension_semantics=("arbitrary",) * n_axes,
                                vmem_limit_bytes=vmem_mib * MIB)


def _layer_spec(block, layer, n_grid):
    zeros = (0,) * len(block)
    if n_grid == 1:
        return pl.BlockSpec((None,) + block, lambda i: (layer,) + zeros)
    return pl.BlockSpec((None,) + block, lambda i, j: (layer,) + zeros)


def _mm(a, b):
    return jnp.dot(a.astype(BF16), b.astype(BF16), preferred_element_type=F32)


def _mm_nt(a, b):
    return lax.dot_general(a.astype(BF16), b.astype(BF16), (((1,), (1,)), ((), ())),
                           preferred_element_type=F32)


def _mm_tn(a, b):
    return lax.dot_general(a.astype(BF16), b.astype(BF16), (((0,), (0,)), ((), ())),
                           preferred_element_type=F32)


def _split3(x):
    hi = x.astype(BF16)
    r = x - hi.astype(F32)
    mid = r.astype(BF16)
    lo = (r - mid.astype(F32)).astype(BF16)
    return hi, mid, lo


def _mm_exact(sel, x):
    hi, mid, lo = _split3(x)
    return (jnp.dot(sel, hi, preferred_element_type=F32) + jnp.dot(sel, mid, preferred_element_type=F32)
            + jnp.dot(sel, lo, preferred_element_type=F32))


def _mm_exact_lhs(x, sel, terms=3):
    return sum(jnp.dot(part, sel, preferred_element_type=F32) for part in _split3(x)[:terms])


def _silu(x):
    return x * jax.nn.sigmoid(x)


def _head_block_matrix(width, value):
    ri = lax.broadcasted_iota(jnp.int32, (width, width), 0) >> 6
    ci = lax.broadcasted_iota(jnp.int32, (width, width), 1) >> 6
    return jnp.where(ri == ci, value, 0.0).astype(BF16)


def _head_reduce(x, g):
    hi = x.astype(BF16)
    lo = (x - hi.astype(F32)).astype(BF16)
    return jnp.dot(hi, g, preferred_element_type=F32) + jnp.dot(lo, g, preferred_element_type=F32)


def _head_rms(x):
    ms = _head_reduce(x * x, _head_block_matrix(x.shape[1], 1.0 / HEAD_DIM))
    return x * lax.rsqrt(ms + EPS)


def _rope(x, tab_ref):
    w = x.shape[1]
    return (x * tab_ref[0] + pltpu.roll(x, w - 16, 1) * tab_ref[1] + pltpu.roll(x, 16, 1) * tab_ref[2])


def _attend(qs, parts):
    groups = range(len(qs))
    qs = [(q.astype(F32) * (SCALE * LOG2E)).astype(BF16) for q in qs]

    def score(q, part):
        k, _, bias, feature_major = part
        s = _mm(q, k) if feature_major else _mm_nt(q, k)
        return s if bias is None else s + bias

    scores = [[score(qs[g], part) for part in parts[g]] for g in groups]
    m = [functools.reduce(jnp.maximum, [s.max(axis=-1, keepdims=True) for s in scores[g]]) for g in groups]
    p = [[jnp.exp2(s - m[g]) for s in scores[g]] for g in groups]
    den = [sum(x.sum(axis=-1, keepdims=True) for x in p[g]) for g in groups]
    out = [sum(_mm_nt(x, part[1]) if part[3] else _mm(x, part[1]) for x, part in zip(p[g], parts[g]))
           for g in groups]
    return [out[g] / den[g] for g in groups]


def _hs(h):
    return slice(h * HEAD_DIM, (h + 1) * HEAD_DIM)


def _aligned(x, m):
    return x if isinstance(x, int) else pl.multiple_of(x, m)


def _mod_kernel(c_ref, w_ref, b_ref, o_ref):
    o_ref[...] = _mm(_silu(c_ref[...]), w_ref[...]) + b_ref[...]


def _mod_call(cond, w_ada, b_ada3):
    tn = 512
    rows = cond.shape[0]
    return pl.pallas_call(
        _mod_kernel,
        grid=(DEPTH, 3 * D_MODEL // tn),
        in_specs=[pl.BlockSpec((rows, D_MODEL), lambda l, j: (0, 0)),
                  pl.BlockSpec((None, D_MODEL, tn), lambda l, j: (l, 0, j)),
                  pl.BlockSpec((None, 1, tn), lambda l, j: (l, 0, j))],
        out_specs=pl.BlockSpec((None, rows, tn), lambda l, j: (l, 0, j)),
        out_shape=jax.ShapeDtypeStruct((DEPTH, rows, 3 * D_MODEL), F32),
        compiler_params=_params(24, 2),
        name="adaln_mod",
    )(cond, w_ada, b_ada3)


def _modulated_norm(x, mod, g):
    ms = jnp.mean(x * x, axis=-1, keepdims=True)
    y = x * lax.rsqrt(ms + EPS) * g
    return y * (1.0 + mod[:, D_MODEL:2 * D_MODEL]) + mod[:, :D_MODEL]


def _inproj_kernel(x_ref, mod_ref, g_ref, w_ref, o_ref, ab_ref):
    hn = _modulated_norm(x_ref[...], mod_ref[0], g_ref[...]).astype(BF16)
    tn = 512
    for j in range(PROJ_W // tn):
        y = jnp.dot(hn, w_ref[:, j * tn:(j + 1) * tn], preferred_element_type=F32)
        o_ref[:, j * tn:(j + 1) * tn] = y.astype(BF16)
        if j == OFF_GAB // tn:
            ab_ref[...] = y[:, OFF_GAB % tn:OFF_GAB % tn + 128]


def _inproj_call(x2d, mod3, norm_g3, wcat, layer, rows_per_mod):
    t = x2d.shape[0]
    tm = 512
    if mod3.shape[0] == 1:
        mod_idx = lambda i: (0, 0, 0)
    else:
        mod_idx = lambda i: ((i * tm) // rows_per_mod, 0, 0)
    return pl.pallas_call(
        _inproj_kernel,
        grid=(t // tm,),
        in_specs=[pl.BlockSpec((tm, D_MODEL), lambda i: (i, 0)),
                  pl.BlockSpec((1, 1, 3 * D_MODEL), mod_idx),
                  _layer_spec((1, D_MODEL), layer, 1),
                  pl.BlockSpec((None, D_MODEL, PROJ_W), lambda i: (layer, 0, 0), pipeline_mode=pl.Buffered(1))],
        out_specs=[pl.BlockSpec((tm, PROJ_W), lambda i: (i, 0)), pl.BlockSpec((tm, 128), lambda i: (i, 0))],
        out_shape=[jax.ShapeDtypeStruct((t, PROJ_W), BF16), jax.ShapeDtypeStruct((t, 128), F32)],
        compiler_params=_params(40, 1),
        name="inproj",
    )(x2d, mod3, norm_g3, wcat)


def _stacked_heads(q, n_kv):
    rep = N_HEADS // n_kv
    return [jnp.concatenate([q[:, _hs(g * rep + r)] for r in range(rep)], axis=0) for g in range(n_kv)]


def _unstack_heads(outs, n_kv):
    rep = N_HEADS // n_kv
    m = outs[0].shape[0] // rep
    return jnp.concatenate([outs[g][r * m:(r + 1) * m] for g in range(n_kv) for r in range(rep)], axis=-1)


def _write_layer(ref, layer, value, stacked):
    if not stacked:
        ref[...] = value
        return
    for l in range(ref.shape[0]):
        ref[l] = value if l == layer else jnp.zeros(value.shape, value.dtype)


def _write_state(st_ref, layer, stacked, piece):
    for d in range(2):
        for h in range(N_HEADS):
            value = piece(d, h)
            if stacked:
                for l in range(st_ref.shape[0]):
                    st_ref[l, d, h] = value if l == layer else jnp.zeros(value.shape, value.dtype)
            else:
                st_ref[d, h] = value


def _attn_ctx_kernel(*refs, n_kv, norm, layer, first):
    if norm:
        q_ref, k_ref, v_ref, z_ref, qn_ref, kn_ref = refs[:6]
    else:
        q_ref, k_ref, v_ref, z_ref = refs[:4]
    o_ref, kt_ref, vt_ref = refs[-3:]
    q, k, v, z = q_ref[...], k_ref[...].astype(F32), v_ref[...].astype(F32), z_ref[...].astype(F32)
    if norm:
        q = _head_rms(q.astype(F32)) * qn_ref[...]
        k = _head_rms(k) * kn_ref[...]
    n_seq = kt_ref.shape[0]
    seq = k.shape[0] // n_seq
    qs, parts = [], []
    for s in range(n_seq):
        rows = slice(s * seq, (s + 1) * seq)
        _write_layer(kt_ref.at[s], layer, k[rows].T.reshape(n_kv, HEAD_DIM, seq), first)
        _write_layer(vt_ref.at[s], layer, v[rows].T.reshape(n_kv, HEAD_DIM, seq), first)
        qs += _stacked_heads(q[rows], n_kv)
        parts += [[(k[rows, _hs(g)], v[rows, _hs(g)], None, False)] for g in range(n_kv)]
    outs = _attend(qs, parts)
    o = jnp.concatenate([_unstack_heads(outs[s * n_kv:(s + 1) * n_kv], n_kv) for s in range(n_seq)], axis=0)
    o_ref[...] = (o * _silu(z)).astype(BF16)


def _attn_ctx_call(proj, prev, layer, batch, seq, off_q, off_k, off_v, off_z, n_kv, qn=None, kn=None):
    t = batch * seq
    kvw = n_kv * HEAD_DIM
    norm = qn is not None
    first = prev is None
    n_seq = CTX_SEQS_PER_STEP
    assert batch % n_seq == 0
    tm = n_seq * seq
    in_specs = [pl.BlockSpec((tm, BRANCH_W), lambda b: (b, off_q // BRANCH_W)),
                pl.BlockSpec((tm, kvw), lambda b: (b, off_k // kvw)),
                pl.BlockSpec((tm, kvw), lambda b: (b, off_v // kvw)),
                pl.BlockSpec((tm, BRANCH_W), lambda b: (b, off_z // BRANCH_W))]
    args = [proj, proj, proj, proj]
    if norm:
        in_specs += [_layer_spec((1, BRANCH_W), layer, 1), _layer_spec((1, kvw), layer, 1)]
        args += [qn, kn]
    aliases = {}
    if first:
        cache_spec = pl.BlockSpec((n_seq, DEPTH, n_kv, HEAD_DIM, seq), lambda b: (b, 0, 0, 0, 0))
    else:
        aliases = {len(args): 1, len(args) + 1: 2}
        in_specs += [pl.BlockSpec(memory_space=pl.ANY)] * 2
        args += list(prev)
        cache_spec = pl.BlockSpec((n_seq, None, n_kv, HEAD_DIM, seq), lambda b: (b, layer, 0, 0, 0))
    cache_shape = jax.ShapeDtypeStruct((batch, DEPTH, n_kv, HEAD_DIM, seq), F32)
    return pl.pallas_call(
        functools.partial(_attn_ctx_kernel, n_kv=n_kv, norm=norm, layer=layer, first=first),
        grid=(batch // n_seq,), in_specs=in_specs,
        out_specs=[pl.BlockSpec((tm, BRANCH_W), lambda b: (b, 0)), cache_spec, cache_spec],
        out_shape=[jax.ShapeDtypeStruct((t, BRANCH_W), BF16), cache_shape, cache_shape],
        input_output_aliases=aliases,
        compiler_params=_params(32, 1),
        name="attn_ctx_norm" if norm else "attn_ctx",
    )(*args)


def _attn_lat_kernel(q_ref, kv_ref, z_ref, ckt_ref, cvt_ref, qtab_ref, ktab_ref, qn_ref, kn_ref, o_ref,
                     k_s, v_s):
    kw = KV_HEADS * HEAD_DIM

    @pl.when(pl.program_id(1) == 0)
    def _():
        kv = kv_ref[...]
        k_s[...] = _rope(_head_rms(kv[:, :kw].astype(F32)) * kn_ref[...], ktab_ref).astype(BF16)
        v_s[...] = kv[:, kw:]

    q = _rope(_head_rms(q_ref[...].astype(F32)) * qn_ref[...], qtab_ref)
    k, v = k_s[...], v_s[...]
    outs = _attend(_stacked_heads(q, KV_HEADS),
                   [[(k[:, _hs(g)], v[:, _hs(g)], None, False), (ckt_ref[g], cvt_ref[g], None, True)]
                    for g in range(KV_HEADS)])
    o_ref[...] = (_unstack_heads(outs, KV_HEADS) * _silu(z_ref[...].astype(F32))).astype(BF16)


def _attn_lat_call(proj, cache_kt, cache_vt, layer, batch, seq, qtab, ktab, qn, kn):
    tq = 256
    nq = seq // tq
    past = cache_kt.shape[-1]
    kw = KV_HEADS * HEAD_DIM
    ctx_spec = pl.BlockSpec((None, None, KV_HEADS, HEAD_DIM, past), lambda b, i: (b, layer, 0, 0, 0))
    return pl.pallas_call(
        _attn_lat_kernel,
        grid=(batch, nq),
        in_specs=[pl.BlockSpec((tq, BRANCH_W), lambda b, i: (b * nq + i, OFF_AQKV // BRANCH_W)),
                  pl.BlockSpec((seq, 2 * kw), lambda b, i: (b, (OFF_AQKV + BRANCH_W) // (2 * kw))),
                  pl.BlockSpec((tq, BRANCH_W), lambda b, i: (b * nq + i, OFF_AZ // BRANCH_W)),
                  ctx_spec, ctx_spec,
                  pl.BlockSpec((3, tq, BRANCH_W), lambda b, i: (0, i, 0)),
                  pl.BlockSpec((3, seq, kw), lambda b, i: (0, 0, 0)),
                  _layer_spec((1, BRANCH_W), layer, 2),
                  _layer_spec((1, kw), layer, 2)],
        out_specs=pl.BlockSpec((tq, BRANCH_W), lambda b, i: (b * nq + i, 0)),
        out_shape=jax.ShapeDtypeStruct((batch * seq, BRANCH_W), BF16),
        scratch_shapes=[pltpu.VMEM((seq, kw), BF16), pltpu.VMEM((seq, kw), BF16)],
        compiler_params=_params(40, 2),
        name="attn_lat",
    )(proj, proj, proj, cache_kt, cache_vt, qtab, ktab, qn, kn)


def _na_bias_kernel(t_ref, o_ref):
    nblk = o_ref.shape[0]
    c = lax.broadcasted_iota(jnp.int32, (GRID_W, 2 * GRID_W), 0)
    j = lax.broadcasted_iota(jnp.int32, (GRID_W, 2 * GRID_W), 1)
    kc = j & (GRID_W - 1)
    dc = kc - c + (NA_COLS - 1)
    cs = jnp.clip(c - NA_COLS // 2, 0, GRID_W - NA_COLS)
    valid = jnp.logical_and(kc >= cs, kc < cs + NA_COLS)
    left = j < GRID_W

    def body(b, carry):
        b2 = jnp.minimum(b + 1, nblk - 1)
        acc = jnp.full((GRID_W, 2 * GRID_W), NEG_INF, F32)
        for i in range(N_DC):
            acc = jnp.where(dc == i, jnp.where(left, t_ref[b * N_DC + i], t_ref[b2 * N_DC + i]), acc)
        o_ref[b] = jnp.where(valid, acc * LOG2E, NEG_INF)
        return carry

    lax.fori_loop(0, nblk, body, 0)


def _na_bias_call(na_bias):
    nblk = DEPTH * N_HEADS * N_DR
    return pl.pallas_call(
        _na_bias_kernel,
        in_specs=[pl.BlockSpec(memory_space=pltpu.SMEM)],
        out_specs=pl.BlockSpec((nblk, GRID_W, 2 * GRID_W), lambda: (0, 0, 0)),
        out_shape=jax.ShapeDtypeStruct((nblk, GRID_W, 2 * GRID_W), F32),
        name="na_bias",
    )(na_bias.reshape(-1))


def _na_kernel(q_ref, k_ref, v_ref, z_ref, ckt_ref, cvt_ref, tb_ref, o_ref, *, rows):
    win = NA_ROWS * GRID_W
    qs, parts = [], []
    for i in range(NA_ROWS_PER_STEP):
        r = pl.program_id(1) * NA_ROWS_PER_STEP + i
        rs = jnp.clip(r - NA_ROWS // 2, 0, rows - NA_ROWS)
        r0 = pl.multiple_of(rs * GRID_W, GRID_W)
        kwin = k_ref[pl.ds(r0, win), :]
        vwin = v_ref[pl.ds(r0, win), :]
        q = q_ref[i * GRID_W:(i + 1) * GRID_W, :]
        dr0 = rs - r + NA_ROWS - 1
        for h in range(N_HEADS):
            bias = jnp.concatenate([tb_ref[h * N_DR + dr0 + 2 * p] for p in range(NA_ROWS // 2)], axis=1)
            qs.append(q[:, _hs(h)])
            parts.append([(kwin[:, _hs(h)], vwin[:, _hs(h)], bias, False), (ckt_ref[h], cvt_ref[h], None, True)])
    outs = _attend(qs, parts)
    o = jnp.concatenate([jnp.concatenate(outs[i * N_HEADS:(i + 1) * N_HEADS], axis=-1)
                         for i in range(NA_ROWS_PER_STEP)], axis=0)
    o_ref[...] = (o * _silu(z_ref[...].astype(F32))).astype(BF16)


def _na_call(proj, cache_kt, cache_vt, tb, layer, batch, seq):
    rows = seq // GRID_W
    assert rows >= NA_ROWS and rows % NA_ROWS_PER_STEP == 0
    steps = rows // NA_ROWS_PER_STEP
    tq = NA_ROWS_PER_STEP * GRID_W
    past = cache_kt.shape[-1]
    nblk = N_HEADS * N_DR
    cq = OFF_D // BRANCH_W
    ctx_spec = pl.BlockSpec((None, None, N_HEADS, HEAD_DIM, past), lambda b, r: (b, layer, 0, 0, 0))
    return pl.pallas_call(
        functools.partial(_na_kernel, rows=rows),
        grid=(batch, steps),
        in_specs=[pl.BlockSpec((tq, BRANCH_W), lambda b, r: (b * steps + r, cq)),
                  pl.BlockSpec((seq, BRANCH_W), lambda b, r: (b, cq + 1)),
                  pl.BlockSpec((seq, BRANCH_W), lambda b, r: (b, cq + 2)),
                  pl.BlockSpec((tq, BRANCH_W), lambda b, r: (b * steps + r, cq + 3)),
                  ctx_spec, ctx_spec,
                  pl.BlockSpec((nblk, GRID_W, 2 * GRID_W), lambda b, r: (layer, 0, 0))],
        out_specs=pl.BlockSpec((tq, BRANCH_W), lambda b, r: (b * steps + r, 0)),
        out_shape=jax.ShapeDtypeStruct((batch * seq, BRANCH_W), BF16),
        compiler_params=_params(32, 2),
        name="na_lat",
    )(proj, proj, proj, proj, cache_kt, cache_vt, tb)


def _gdn_kernel(*refs, seq, has_s0, layer, emit):
    qkv_ref, z_ref, ab_ref, cw_ref, par_ref, g_ref = refs[:6]
    s0_ref = refs[6] if has_s0 else None
    (q_s, k_s, v_s, gcb_s, bcb_s, r_s, mc_s, nc_s, qp_s, op_s, egl_s, s_s, oacc_ref) = refs[-13:]
    if emit == "none":
        o_ref, st_ref = refs[-14], None
    else:
        o_ref, st_ref = refs[-15], refs[-14]
    n_chunks = seq // CHUNK
    n_levels = CHUNK.bit_length() - 1
    qkv_w = 3 * BRANCH_W
    half = SHORT_CONV // 2
    pair_w = 2 * HEAD_DIM
    pairs = [slice(p * pair_w, (p + 1) * pair_w) for p in range(BRANCH_W // pair_w)]
    tr = 256
    cpt = tr // CHUNK
    head_sum = _head_block_matrix(BRANCH_W, 1.0)

    gc_i = lax.broadcasted_iota(jnp.int32, (128, BRANCH_W), 0)
    gh_j = lax.broadcasted_iota(jnp.int32, (128, BRANCH_W), 1) >> 6
    sel_beta = [jnp.where(gc_i == gh_j + 4 * d, 1.0, 0.0).astype(BF16) for d in range(2)]
    sel_gate = [jnp.where(gc_i == gh_j + 8 + 4 * d, 1.0, 0.0).astype(BF16) for d in range(2)]
    ti = lax.broadcasted_iota(jnp.int32, (tr, tr), 0)
    tj = lax.broadcasted_iota(jnp.int32, (tr, tr), 1)
    same_chunk = (ti >> 6) == (tj >> 6)
    tri = [jnp.where(jnp.logical_and(same_chunk, ti >= tj), 1.0, 0.0).astype(BF16),
           jnp.where(jnp.logical_and(same_chunk, ti <= tj), 1.0, 0.0).astype(BF16)]
    assert tr == BRANCH_W
    lane_head = lax.broadcasted_iota(jnp.int32, (1, BRANCH_W), 1) >> 6

    halo = 16
    edge = 8
    assert half <= edge
    si = lax.broadcasted_iota(jnp.int32, (tr, tr), 0)
    sj = lax.broadcasted_iota(jnp.int32, (tr, tr), 1)
    ei = lax.broadcasted_iota(jnp.int32, (edge, halo), 0)
    ej = lax.broadcasted_iota(jnp.int32, (edge, halo), 1)
    taps = [j for j in range(SHORT_CONV) if j != half]
    shift = {j: jnp.where(sj == si + (j - half), 1.0, 0.0).astype(BF16) for j in taps}
    shift_before = {j: jnp.where(ej == ei + (halo + j - half), 1.0, 0.0).astype(BF16) for j in taps if j < half}
    shift_after = {j: jnp.where(ej == ei + (j - half - edge), 1.0, 0.0).astype(BF16) for j in taps if j > half}
    for t in range(seq // tr):
        rows = slice(t * tr, (t + 1) * tr)
        x = qkv_ref[rows, :]
        y = x.astype(F32) * cw_ref[half:half + 1, :]
        for j in taps:
            y = y + jnp.dot(shift[j], x, preferred_element_type=F32) * cw_ref[j:j + 1, :]
        if t > 0:
            before = qkv_ref[t * tr - halo:t * tr, :]
            top = sum(jnp.dot(shift_before[j], before, preferred_element_type=F32) * cw_ref[j:j + 1, :]
                      for j in shift_before)
            y = jnp.concatenate([y[:edge] + top, y[edge:]], axis=0)
        if (t + 1) * tr < seq:
            after = qkv_ref[(t + 1) * tr:(t + 1) * tr + halo, :]
            bottom = sum(jnp.dot(shift_after[j], after, preferred_element_type=F32) * cw_ref[j:j + 1, :]
                         for j in shift_after)
            y = jnp.concatenate([y[:tr - edge], y[tr - edge:] + bottom], axis=0)
        y = _silu(y)
        qq, kk = y[:, :BRANCH_W], y[:, BRANCH_W:2 * BRANCH_W]
        q_s[rows, :] = qq * lax.rsqrt(_head_reduce(qq * qq, head_sum) + EPS) * SCALE
        k_s[rows, :] = kk * lax.rsqrt(_head_reduce(kk * kk, head_sum) + EPS)
        v_s[rows, :] = y[:, 2 * BRANCH_W:]
        x = ab_ref[rows, :]
        beta = jax.nn.sigmoid(x)
        xs = x + par_ref[0:1, :]
        softplus = jnp.maximum(xs, 0.0) + jnp.log1p(jnp.exp(-jnp.abs(xs)))
        la = -jnp.exp(par_ref[1:2, :]) * softplus
        for d in range(2):
            gc = _mm_exact(tri[d], la)
            gcb_s[d, rows, :] = _mm_exact_lhs(gc, sel_gate[d])
            bcb_s[d, rows, :] = _mm_exact_lhs(beta, sel_beta[d], terms=2)
            gt = gc.T[8:16, :]
            shifted = {s: (gt if s == 0 else pltpu.roll(gt, (s * HEAD_DIM) % tr, 1))
                       for s in range(1 - cpt, N_HEADS)}
            for c in range(cpt):
                r = jnp.zeros((1, BRANCH_W), F32)
                for h in range(N_HEADS):
                    r = jnp.where(lane_head == h, shifted[h - c][4 * d + h:4 * d + h + 1, :], r)
                r_s[d, (t * cpt + c) * 8:(t * cpt + c + 1) * 8, :] = jnp.broadcast_to(r, (8, BRANCH_W))

    for d in range(2):
        if has_s0:
            s_s[d] = jnp.concatenate([s0_ref[d, h] for h in range(N_HEADS)], axis=-1)
        else:
            s_s[d] = jnp.zeros((HEAD_DIM, BRANCH_W), F32)

    li = lax.broadcasted_iota(jnp.int32, (CHUNK, BRANCH_W), 0)
    lj = lax.broadcasted_iota(jnp.int32, (CHUNK, BRANCH_W), 1) & (HEAD_DIM - 1)
    incl = (li >= lj, li <= lj)
    strict = (li > lj, li < lj)
    level = [((li ^ lj) >> l) == 1 for l in range(n_levels)]
    first_head = lax.broadcasted_iota(jnp.int32, (CHUNK, pair_w), 1) < HEAD_DIM

    def expand(y):
        yb = y.astype(BF16)
        zero = jnp.zeros((CHUNK, pair_w), BF16)
        return [jnp.concatenate([jnp.where(first_head, yb[:, p], zero), jnp.where(first_head, zero, yb[:, p])],
                                axis=0) for p in pairs]

    def bdmm(x, ybd):
        xb = x.astype(BF16)
        return jnp.concatenate([jnp.dot(xb[:, p], ybd[i], preferred_element_type=F32)
                                for i, p in enumerate(pairs)], axis=1)

    def bdmm_nt(x, ybd):
        xb = x.astype(BF16)
        return jnp.concatenate([lax.dot_general(xb[:, p], ybd[i], (((1,), (1,)), ((), ())),
                                                preferred_element_type=F32)
                                for i, p in enumerate(pairs)], axis=1)

    def tn_diag(a, b):
        ab, bb = a.astype(BF16), b.astype(BF16)
        outs = []
        for p in pairs:
            full = lax.dot_general(ab[:, p], bb[:, p], (((0,), (0,)), ((), ())), preferred_element_type=F32)
            outs.append(jnp.where(first_head, full[:HEAD_DIM], full[HEAD_DIM:]))
        return jnp.concatenate(outs, axis=1)

    def prepare(chains):
        n = range(len(chains))
        dd = [d for d, _ in chains]
        rows = [pl.ds(_aligned(c * CHUNK, CHUNK), CHUNK) for _, c in chains]
        gcb = [gcb_s[dd[i], rows[i], :] for i in n]
        bcb = [bcb_s[dd[i], rows[i], :] for i in n]
        grow = [r_s[dd[i], pl.ds(_aligned(chains[i][1] * 8, 8), 8), :][0:1, :] for i in n]
        dm = [jnp.exp(jnp.where(incl[dd[i]], gcb[i] - grow[i], NEG_INF)) for i in n]
        k = [k_s[rows[i], :] for i in n]
        q = [q_s[rows[i], :] for i in n]
        v = [v_s[rows[i], :] for i in n]
        kq = [bdmm_nt(jnp.concatenate([k[i], q[i]], axis=0), expand(k[i])) for i in n]
        a = [jnp.where(strict[dd[i]], bcb[i] * kq[i][:CHUNK] * dm[i], 0.0) for i in n]
        tm = [-jnp.where(level[0], a[i], 0.0) for i in n]
        for l in range(1, n_levels):
            b = [jnp.where(level[l], a[i], 0.0) for i in n]
            y = [b[i] + bdmm(tm[i], expand(b[i])) for i in n]
            tm = [tm[i] - (y[i] + bdmm(y[i], expand(tm[i]))) for i in n]
        eg = [jnp.exp(gcb[i]) for i in n]
        bv = [bcb[i] * v[i] for i in n]
        bk = [bcb[i] * k[i] * eg[i] for i in n]
        u = [bv[i] + bdmm(tm[i], expand(bv[i])) for i in n]
        w = [bk[i] + bdmm(tm[i], expand(bk[i])) for i in n]
        gl = [gcb[i][CHUNK - 1:CHUNK, :] if dd[i] == 0 else gcb[i][0:1, :] for i in n]
        kd = [k[i] * jnp.exp(gl[i] - gcb[i]) for i in n]
        qkm = [kq[i][CHUNK:] * dm[i] for i in n]
        mc = [tn_diag(kd[i], w[i]) for i in n]
        nc = [tn_diag(kd[i], u[i]) for i in n]
        qp = [q[i] * eg[i] - bdmm(qkm[i], expand(w[i])) for i in n]
        op = [bdmm(qkm[i], expand(u[i])) for i in n]
        for i in n:
            d, c = chains[i]
            mc_s[d, rows[i], :] = mc[i].astype(BF16)
            nc_s[d, rows[i], :] = nc[i]
            qp_s[d, rows[i], :] = qp[i].astype(BF16)
            op_s[d, rows[i], :] = op[i]
            egl_s[d, pl.ds(_aligned(c * 8, 8), 8), :] = jnp.broadcast_to(jnp.exp(gl[i]), (8, BRANCH_W))

    group = min(PREP_CHUNKS, n_chunks)
    if n_chunks == group:
        prepare([(d, c) for c in range(group) for d in range(2)])
    else:
        def prep_body(j, carry):
            prepare([(d, j * group + c) for c in range(group) for d in range(2)])
            return carry
        lax.fori_loop(0, n_chunks // group, prep_body, 0)

    def scan_step(i):
        for d, c in ((0, i), (1, n_chunks - 1 - i)):
            rows = pl.ds(_aligned(c * CHUNK, CHUNK), CHUNK)
            s = s_s[d]
            sbd = expand(s)
            oacc_ref[d, rows, :] = bdmm(qp_s[d, rows, :], sbd) + op_s[d, rows, :]
            egl = egl_s[d, pl.ds(_aligned(c * 8, 8), 8), :][0:1, :]
            s_s[d] = s * egl - bdmm(mc_s[d, rows, :], sbd) + nc_s[d, rows, :]

    unroll = min(SCAN_UNROLL, n_chunks)

    def scan_body(j, carry):
        for i in range(unroll):
            scan_step(j * unroll + i)
        return carry

    lax.fori_loop(0, n_chunks // unroll, scan_body, 0)

    if st_ref is not None:
        _write_state(st_ref, layer, emit == "first", lambda d, h: s_s[d][:, _hs(h)])
    o = oacc_ref[0] + oacc_ref[1]
    ms = _head_reduce(o * o, _head_block_matrix(BRANCH_W, 1.0 / HEAD_DIM))
    o_ref[...] = (o * lax.rsqrt(ms + EPS) * g_ref[...] * _silu(z_ref[...].astype(F32))).astype(BF16)


def _state_spec(layer):
    return pl.BlockSpec((None, None, 2, N_HEADS, HEAD_DIM, HEAD_DIM), lambda b: (b, layer, 0, 0, 0, 0))


def _state_output(emit, prev, layer, batch, n_args):
    if emit == "none":
        return [], [], [], [], {}
    shape = jax.ShapeDtypeStruct((batch, DEPTH, 2, N_HEADS, HEAD_DIM, HEAD_DIM), F32)
    if emit == "first":
        spec = pl.BlockSpec((None, DEPTH, 2, N_HEADS, HEAD_DIM, HEAD_DIM), lambda b: (b, 0, 0, 0, 0, 0))
        return [], [], [spec], [shape], {}
    return [pl.BlockSpec(memory_space=pl.ANY)], [prev], [_state_spec(layer)], [shape], {n_args: 1}


def _gdn_call(proj, gab, cw8, par, norm_g, layer, batch, seq, s0=None, emit="none", prev=None):
    has_s0 = s0 is not None
    qkv_w = 3 * BRANCH_W
    in_specs = [pl.BlockSpec((seq, qkv_w), lambda b: (b, OFF_GQKV // qkv_w)),
                pl.BlockSpec((seq, BRANCH_W), lambda b: (b, OFF_GZ // BRANCH_W)),
                pl.BlockSpec((seq, 128), lambda b: (b, 0)),
                _layer_spec((8, qkv_w), layer, 1),
                _layer_spec((2, 128), layer, 1),
                _layer_spec((1, BRANCH_W), layer, 1)]
    args = [proj, proj, gab, cw8, par, norm_g]
    if has_s0:
        in_specs.append(_state_spec(layer))
        args.append(s0)
    st_in_specs, st_args, st_out_specs, st_shapes, aliases = _state_output(emit, prev, layer, batch, len(args))
    return pl.pallas_call(
        functools.partial(_gdn_kernel, seq=seq, has_s0=has_s0, layer=layer, emit=emit),
        grid=(batch,), in_specs=in_specs + st_in_specs,
        out_specs=[pl.BlockSpec((seq, BRANCH_W), lambda b: (b, 0))] + st_out_specs,
        out_shape=[jax.ShapeDtypeStruct((batch * seq, BRANCH_W), BF16)] + st_shapes,
        input_output_aliases=aliases,
        scratch_shapes=[pltpu.VMEM((seq, BRANCH_W), F32),
                        pltpu.VMEM((seq, BRANCH_W), F32),
                        pltpu.VMEM((seq, BRANCH_W), F32),
                        pltpu.VMEM((2, seq, BRANCH_W), F32),
                        pltpu.VMEM((2, seq, BRANCH_W), F32),
                        pltpu.VMEM((2, seq // CHUNK * 8, BRANCH_W), F32),
                        pltpu.VMEM((2, seq, BRANCH_W), BF16),
                        pltpu.VMEM((2, seq, BRANCH_W), F32),
                        pltpu.VMEM((2, seq, BRANCH_W), BF16),
                        pltpu.VMEM((2, seq, BRANCH_W), F32),
                        pltpu.VMEM((2, seq // CHUNK * 8, BRANCH_W), F32),
                        pltpu.VMEM((2, HEAD_DIM, BRANCH_W), F32),
                        pltpu.VMEM((2, seq, BRANCH_W), F32)],
        compiler_params=_params(48, 1),
        name="gdn",
    )(*args, *st_args)


_RET_LOG_GAMMA = [[float(np.log1p(-np.exp2(-(base + h)))) for h in range(N_HEADS)] for base in RET_DECAY_BASE]


def _ret_kernel(*refs, seq, has_s0, layer, emit):
    qkv_ref, z_ref, g_ref = refs[:3]
    s0_ref = refs[3] if has_s0 else None
    o_ref, st_ref = (refs[-1], None) if emit == "none" else (refs[-2], refs[-1])
    tq = 256
    heads = range(N_HEADS)
    lgf, lgb = _RET_LOG_GAMMA
    ks = [qkv_ref[:, BRANCH_W + h * HEAD_DIM:BRANCH_W + (h + 1) * HEAD_DIM] for h in heads]
    vs = [qkv_ref[:, 2 * BRANCH_W + h * HEAD_DIM:2 * BRANCH_W + (h + 1) * HEAD_DIM] for h in heads]
    if st_ref is not None:
        tcol = lax.broadcasted_iota(jnp.int32, (seq, 1), 0).astype(F32)
        stf = [_mm_tn(ks[h] * jnp.exp((seq - 1.0 - tcol) * lgf[h]), vs[h]) for h in heads]
        stb = [_mm_tn(ks[h] * jnp.exp(tcol * lgb[h]), vs[h]) for h in heads]
        if has_s0:
            stf = [stf[h] + float(np.exp(seq * lgf[h])) * s0_ref[0, h] for h in heads]
            stb = [stb[h] + float(np.exp(seq * lgb[h])) * s0_ref[1, h] for h in heads]
        _write_state(st_ref, layer, emit == "first", lambda d, h: (stf, stb)[d][h])
    tiles = []
    for t in range(seq // tq):
        qs = [qkv_ref[t * tq:(t + 1) * tq, _hs(h)] * SCALE for h in heads]
        qk = [_mm_nt(qs[h], ks[h]) for h in heads]
        di = (lax.broadcasted_iota(jnp.int32, (tq, seq), 0) + t * tq
              - lax.broadcasted_iota(jnp.int32, (tq, seq), 1)).astype(F32)
        dm = [jnp.where(di == 0, 2.0, jnp.exp(di * jnp.where(di >= 0, lgf[h], -lgb[h]))) for h in heads]
        o = [_mm(qk[h] * dm[h], vs[h]) for h in heads]
        if has_s0:
            tt = lax.broadcasted_iota(jnp.int32, (tq, 1), 0).astype(F32) + float(t * tq)
            o = [o[h] + _mm(qs[h] * jnp.exp((tt + 1.0) * lgf[h]), s0_ref[0, h])
                 + _mm(qs[h] * jnp.exp((seq - tt) * lgb[h]), s0_ref[1, h]) for h in heads]
        ms = [jnp.mean(o[h] * o[h], axis=-1, keepdims=True) for h in heads]
        tiles.append(jnp.concatenate([o[h] * lax.rsqrt(ms[h] + EPS) * g_ref[...] for h in heads], axis=-1))
    o_ref[...] = (jnp.concatenate(tiles, axis=0) * _silu(z_ref[...].astype(F32))).astype(BF16)


def _ret_call(proj, norm_g, layer, batch, seq, s0=None, emit="none", prev=None):
    has_s0 = s0 is not None
    qkv_w = 3 * BRANCH_W
    in_specs = [pl.BlockSpec((seq, qkv_w), lambda b: (b, OFF_C // qkv_w)),
                pl.BlockSpec((seq, BRANCH_W), lambda b: (b, (OFF_C + qkv_w) // BRANCH_W)),
                _layer_spec((1, HEAD_DIM), layer, 1)]
    args = [proj, proj, norm_g]
    if has_s0:
        in_specs.append(_state_spec(layer))
        args.append(s0)
    st_in_specs, st_args, st_out_specs, st_shapes, aliases = _state_output(emit, prev, layer, batch, len(args))
    return pl.pallas_call(
        functools.partial(_ret_kernel, seq=seq, has_s0=has_s0, layer=layer, emit=emit),
        grid=(batch,), in_specs=in_specs + st_in_specs,
        out_specs=[pl.BlockSpec((seq, BRANCH_W), lambda b: (b, 0))] + st_out_specs,
        out_shape=[jax.ShapeDtypeStruct((batch * seq, BRANCH_W), BF16)] + st_shapes,
        input_output_aliases=aliases,
        compiler_params=_params(48, 1),
        name="retention",
    )(*args, *st_args)


def _out_kernel(*refs, final):
    if final:
        (h_ref, mod_ref, g_ref, oa_ref, ob_ref, oc_ref, od_ref, wg_ref, wb_ref, wo_ref, fn_ref,
         o_ref, y_ref) = refs
    else:
        h_ref, mod_ref, g_ref, oa_ref, ob_ref, oc_ref, od_ref, wg_ref, wb_ref, wo_ref, o_ref = refs
    x = h_ref[...]
    mod = mod_ref[0]
    hn = _modulated_norm(x, mod, g_ref[...]).astype(BF16)
    merged = None
    for n, br_ref in enumerate((oa_ref, ob_ref, oc_ref, od_ref)):
        gate = jax.nn.sigmoid(jnp.dot(hn, wg_ref[:, n * D_MODEL:(n + 1) * D_MODEL], preferred_element_type=F32))
        up = jnp.dot(br_ref[...], wb_ref[n], preferred_element_type=F32)
        merged = gate * up if merged is None else merged + gate * up
    out = jnp.dot(merged.astype(BF16), wo_ref[...], preferred_element_type=F32)
    hnew = x + mod[:, 2 * D_MODEL:] * out
    o_ref[...] = hnew
    if final:
        ms = jnp.mean(hnew * hnew, axis=-1, keepdims=True)
        y_ref[...] = hnew * lax.rsqrt(ms + EPS) * fn_ref[...]


def _out_call(h2d, mod3, norm_g3, branches, wg, wb, wo, layer, rows_per_mod, final_norm=None):
    t = h2d.shape[0]
    tm = 512
    final = final_norm is not None
    if mod3.shape[0] == 1:
        mod_idx = lambda i: (0, 0, 0)
    else:
        mod_idx = lambda i: ((i * tm) // rows_per_mod, 0, 0)
    once = pl.Buffered(1)
    in_specs = [pl.BlockSpec((tm, D_MODEL), lambda i: (i, 0)),
                pl.BlockSpec((1, 1, 3 * D_MODEL), mod_idx),
                _layer_spec((1, D_MODEL), layer, 1)]
    in_specs += [pl.BlockSpec((tm, BRANCH_W), lambda i: (i, 0))] * N_BRANCH
    in_specs += [pl.BlockSpec((None, D_MODEL, N_BRANCH * D_MODEL), lambda i: (layer, 0, 0), pipeline_mode=once),
                 pl.BlockSpec((None, N_BRANCH, BRANCH_W, D_MODEL), lambda i: (layer, 0, 0, 0), pipeline_mode=once),
                 pl.BlockSpec((None, D_MODEL, D_MODEL), lambda i: (layer, 0, 0), pipeline_mode=once)]
    args = [h2d, mod3, norm_g3, *branches, wg, wb, wo]
    out_specs = [pl.BlockSpec((tm, D_MODEL), lambda i: (i, 0))]
    out_shape = [jax.ShapeDtypeStruct((t, D_MODEL), F32)]
    if final:
        in_specs.append(pl.BlockSpec((1, D_MODEL), lambda i: (0, 0)))
        args.append(final_norm.reshape(1, D_MODEL))
        out_specs.append(pl.BlockSpec((tm, D_MODEL), lambda i: (i, 0)))
        out_shape.append(jax.ShapeDtypeStruct((t, D_MODEL), F32))
    return pl.pallas_call(
        functools.partial(_out_kernel, final=final),
        grid=(t // tm,), in_specs=in_specs, out_specs=out_specs, out_shape=out_shape,
        compiler_params=_params(48, 1),
        name="merge_out_final" if final else "merge_out",
    )(*args)


def _prep_weights(w_in):
    offs = np.concatenate([[0], np.cumsum(IN_SPLITS)])
    seg = lambda i, j: w_in[:, :, offs[i]:offs[j]]
    pad = jnp.zeros((DEPTH, D_MODEL, PROJ_W - OFF_GAB - IN_SPLITS[5]), w_in.dtype)
    wcat = jnp.concatenate([seg(7, 11), seg(11, 15), seg(6, 7), seg(4, 5), seg(0, 3), seg(3, 4), seg(5, 6), pad],
                           axis=2).astype(BF16)
    return wcat, seg(15, 16).astype(BF16)


def _rope_tables(seq):
    t = jnp.arange(seq)
    quarter = HEAD_DIM // 4
    inv = ROPE_THETA ** (-jnp.arange(quarter, dtype=F32) / quarter)

    def half(pos):
        ang = pos.astype(F32)[:, None] * inv
        c, s, zero = jnp.cos(ang), jnp.sin(ang), jnp.zeros_like(ang)
        return jnp.concatenate([c, c], -1), jnp.concatenate([-s, zero], -1), jnp.concatenate([zero, s], -1)

    parts = [jnp.concatenate([a, b], -1) for a, b in zip(half(t // GRID_W), half(t % GRID_W))]
    tab = jnp.stack(parts)
    return jnp.tile(tab, (1, 1, N_HEADS)), jnp.tile(tab, (1, 1, KV_HEADS))


def _layer(h2d, batch, seq, mod, pw, layer, ctx, caches, final_norm):
    proj, gab = _inproj_call(h2d, mod, pw["norm_g"], pw["wcat"], layer, seq)
    kw = KV_HEADS * HEAD_DIM
    if ctx is None:
        emit = "first" if caches is None else "update"
        akv, nkv, sg_all, sr_all = caches or (None, None, None, None)
        oa, *akv = _attn_ctx_call(proj, akv, layer, batch, seq, OFF_AQKV, OFF_AQKV + BRANCH_W,
                                  OFF_AQKV + BRANCH_W + kw, OFF_AZ, KV_HEADS, pw["qn"], pw["kn"])
        od, *nkv = _attn_ctx_call(proj, nkv, layer, batch, seq, OFF_D, OFF_D + BRANCH_W,
                                  OFF_D + 2 * BRANCH_W, OFF_D + 3 * BRANCH_W, N_HEADS)
        ob, sg_all = _gdn_call(proj, gab, pw["cw8"], pw["gdn_par"], pw["gdn_norm"], layer, batch, seq,
                               emit=emit, prev=sg_all)
        oc, sr_all = _ret_call(proj, pw["ret_norm"], layer, batch, seq, emit=emit, prev=sr_all)
        caches = (akv, nkv, sg_all, sr_all)
    else:
        oa = _attn_lat_call(proj, ctx["akt"], ctx["avt"], layer, batch, seq, ctx["qtab"], ctx["ktab"],
                            pw["qn"], pw["kn"])
        od = _na_call(proj, ctx["nkt"], ctx["nvt"], ctx["tb"], layer, batch, seq)
        ob, = _gdn_call(proj, gab, pw["cw8"], pw["gdn_par"], pw["gdn_norm"], layer, batch, seq, s0=ctx["sg"])
        oc, = _ret_call(proj, pw["ret_norm"], layer, batch, seq, s0=ctx["sr"])
    outs = _out_call(h2d, mod, pw["norm_g"], (oa, ob, oc, od), pw["wg"], pw["wb"], pw["wo"], layer, seq, final_norm)
    return outs, caches


def kernel(x_prompt, x_sample, cache_attn_k, cache_attn_v, cache_na_k, cache_na_v, state_gdn, state_ret, c, c_ctx, w_ada, b_ada, norm_g, w_in, conv_w, gdn_a_log, gdn_dt_bias, gdn_norm, attn_q_norm, attn_k_norm, ret_norm, na_bias, w_branch, w_out, final_norm):
    batch, seq, _ = x_prompt.shape
    dbatch, dseq, _ = x_sample.shape
    assert dbatch == 8, "the modulation kernel handles exactly one sublane tile of conditioning rows"

    wcat, wg = _prep_weights(w_in)
    par = jnp.zeros((DEPTH, 2, 128), F32)
    par = par.at[:, 0, 8:16].set(gdn_dt_bias.reshape(DEPTH, 8)).at[:, 1, 8:16].set(gdn_a_log.reshape(DEPTH, 8))
    pw = dict(
        w_ada=w_ada, b_ada=b_ada.reshape(DEPTH, 1, 3 * D_MODEL), norm_g=norm_g.reshape(DEPTH, 1, D_MODEL),
        wcat=wcat, wg=wg, wb=w_branch.astype(BF16), wo=w_out.astype(BF16),
        cw8=jnp.concatenate([conv_w, jnp.zeros((DEPTH, 8 - SHORT_CONV, 3 * BRANCH_W), F32)], axis=1),
        gdn_par=par,
        gdn_norm=jnp.tile(gdn_norm, (1, N_HEADS)).reshape(DEPTH, 1, BRANCH_W),
        ret_norm=ret_norm.reshape(DEPTH, 1, HEAD_DIM),
        qn=jnp.tile(attn_q_norm, (1, N_HEADS)).reshape(DEPTH, 1, BRANCH_W),
        kn=jnp.tile(attn_k_norm, (1, KV_HEADS)).reshape(DEPTH, 1, KV_HEADS * HEAD_DIM))

    cond = jnp.concatenate([jnp.broadcast_to(c_ctx, (8, D_MODEL)), c], axis=0)
    mods = _mod_call(cond, w_ada, pw["b_ada"])

    h = x_prompt.reshape(batch * seq, D_MODEL)
    caches = None
    for l in range(DEPTH):
        outs, caches = _layer(h, batch, seq, mods[l, 0:1].reshape(1, 1, 3 * D_MODEL), pw, l, None, caches,
                              final_norm if l == DEPTH - 1 else None)
        h = outs[0]
    y_prompt = outs[1].reshape(batch, seq, D_MODEL)
    token_major = lambda a: a.transpose(0, 1, 4, 2, 3)
    (akt, avt), (nkt, nvt), new_state_gdn, new_state_ret = caches
    new_attn_k, new_attn_v, new_na_k, new_na_v = (token_major(a) for a in (akt, avt, nkt, nvt))

    qtab, ktab = _rope_tables(dseq)
    feature_major = lambda a: a.transpose(0, 1, 3, 4, 2)
    ctx = dict(akt=feature_major(cache_attn_k), avt=feature_major(cache_attn_v),
               nkt=feature_major(cache_na_k), nvt=feature_major(cache_na_v),
               sg=state_gdn, sr=state_ret, tb=_na_bias_call(na_bias), qtab=qtab, ktab=ktab)
    h = x_sample.reshape(dbatch * dseq, D_MODEL)
    for l in range(DEPTH):
        outs, _ = _layer(h, dbatch, dseq, mods[l, 8:16].reshape(dbatch, 1, 3 * D_MODEL), pw, l, ctx, None,
                         final_norm if l == DEPTH - 1 else None)
        h = outs[0]
    y_sample = outs[1].reshape(dbatch, dseq, D_MODEL)
    return (y_prompt, y_sample, new_attn_k, new_attn_v, new_na_k, new_na_v, new_state_gdn, new_state_ret)
```

```python
import functools

import numpy as np
import jax
import jax.numpy as jnp
from jax import lax
from jax.experimental import pallas as pl
from jax.experimental.pallas import tpu as pltpu

F32 = jnp.float32
BF16 = jnp.bfloat16

D_MODEL = 1024
HEAD_DIM = 64
N_HEADS = 4
KV_HEADS = N_HEADS // 2
BRANCH_W = N_HEADS * HEAD_DIM
N_BRANCH = 4
DEPTH = 2
GRID_W = 64
CHUNK = 64
PREP_CHUNKS = 4
SCAN_UNROLL = 4
assert CHUNK == HEAD_DIM
SHORT_CONV = 5
NA_ROWS = 8
NA_COLS = 16
CTX_SEQS_PER_STEP = 2
NA_ROWS_PER_STEP = 4
N_DR = 2 * NA_ROWS - 1
N_DC = 2 * NA_COLS - 1
ROPE_THETA = 10000.0
RET_DECAY_BASE = (5.0, 5.5)
RET_TILE = 256
EPS = 1e-6
SCALE = HEAD_DIM ** -0.5
LOG2E = 1.4426950408889634
NEG_INF = float("-inf")

IN_SPLITS = (256, 128, 128, 256, 768, 16, 256, 256, 256, 256, 256, 256, 256, 256, 256, 4096)
PROJ_W = 4096
OFF_C = 0
OFF_D = 1024
OFF_GZ = 2048
OFF_GQKV = 2304
OFF_AQKV = 3072
OFF_AZ = 3584
OFF_GAB = 3840

V7X_VMEM_BYTES = 64 * 1024 * 1024
MIB = 1024 * 1024


def _params(vmem_mib, n_axes):
    assert vmem_mib * MIB < V7X_VMEM_BYTES
    return pltpu.CompilerParams(dimension_semantics=("arbitrary",) * n_axes,
                                vmem_limit_bytes=vmem_mib * MIB)


def _layer_spec(block, layer, n_grid):
    zeros = (0,) * len(block)
    if n_grid == 1:
        return pl.BlockSpec((None,) + block, lambda i: (layer,) + zeros)
    return pl.BlockSpec((None,) + block, lambda i, j: (layer,) + zeros)


def _mm(a, b):
    return jnp.dot(a.astype(BF16), b.astype(BF16), preferred_element_type=F32)


def _mm_nt(a, b):
    return lax.dot_general(a.astype(BF16), b.astype(BF16), (((1,), (1,)), ((), ())),
                           preferred_element_type=F32)


def _mm_tn(a, b):
    return lax.dot_general(a.astype(BF16), b.astype(BF16), (((0,), (0,)), ((), ())),
                           preferred_element_type=F32)


def _split3(x):
    hi = x.astype(BF16)
    r = x - hi.astype(F32)
    mid = r.astype(BF16)
    lo = (r - mid.astype(F32)).astype(BF16)
    return hi, mid, lo


def _mm_exact(sel, x):
    hi, mid, lo = _split3(x)
    return (jnp.dot(sel, hi, preferred_element_type=F32) + jnp.dot(sel, mid, preferred_element_type=F32)
            + jnp.dot(sel, lo, preferred_element_type=F32))


def _mm_exact_lhs(x, sel, terms=3):
    return sum(jnp.dot(part, sel, preferred_element_type=F32) for part in _split3(x)[:terms])


def _silu(x):
    return x * jax.nn.sigmoid(x)


def _head_block_matrix(width, value):
    ri = lax.broadcasted_iota(jnp.int32, (width, width), 0) >> 6
    ci = lax.broadcasted_iota(jnp.int32, (width, width), 1) >> 6
    return jnp.where(ri == ci, value, 0.0).astype(BF16)


def _head_reduce(x, g):
    hi = x.astype(BF16)
    lo = (x - hi.astype(F32)).astype(BF16)
    return jnp.dot(hi, g, preferred_element_type=F32) + jnp.dot(lo, g, preferred_element_type=F32)


def _head_rms(x):
    ms = _head_reduce(x * x, _head_block_matrix(x.shape[1], 1.0 / HEAD_DIM))
    return x * lax.rsqrt(ms + EPS)


def _rope(x, tab_ref):
    w = x.shape[1]
    return (x * tab_ref[0] + pltpu.roll(x, w - 16, 1) * tab_ref[1] + pltpu.roll(x, 16, 1) * tab_ref[2])


def _attend(qs, parts):
    groups = range(len(qs))
    qs = [(q.astype(F32) * (SCALE * LOG2E)).astype(BF16) for q in qs]

    def score(q, part):
        k, _, bias, feature_major = part
        s = _mm(q, k) if feature_major else _mm_nt(q, k)
        return s if bias is None else s + bias

    scores = [[score(qs[g], part) for part in parts[g]] for g in groups]
    m = [functools.reduce(jnp.maximum, [s.max(axis=-1, keepdims=True) for s in scores[g]]) for g in groups]
    p = [[jnp.exp2(s - m[g]) for s in scores[g]] for g in groups]
    den = [sum(x.sum(axis=-1, keepdims=True) for x in p[g]) for g in groups]
    out = [sum(_mm_nt(x, part[1]) if part[3] else _mm(x, part[1]) for x, part in zip(p[g], parts[g]))
           for g in groups]
    return [out[g] / den[g] for g in groups]


def _hs(h):
    return slice(h * HEAD_DIM, (h + 1) * HEAD_DIM)


def _aligned(x, m):
    return x if isinstance(x, int) else pl.multiple_of(x, m)


def _mod_kernel(c_ref, w_ref, b_ref, o_ref):
    o_ref[...] = _mm(_silu(c_ref[...]), w_ref[...]) + b_ref[...]


def _mod_call(cond, w_ada, b_ada3):
    tn = 512
    rows = cond.shape[0]
    return pl.pallas_call(
        _mod_kernel,
        grid=(DEPTH, 3 * D_MODEL // tn),
        in_specs=[pl.BlockSpec((rows, D_MODEL), lambda l, j: (0, 0)),
                  pl.BlockSpec((None, D_MODEL, tn), lambda l, j: (l, 0, j)),
                  pl.BlockSpec((None, 1, tn), lambda l, j: (l, 0, j))],
        out_specs=pl.BlockSpec((None, rows, tn), lambda l, j: (l, 0, j)),
        out_shape=jax.ShapeDtypeStruct((DEPTH, rows, 3 * D_MODEL), F32),
        compiler_params=_params(24, 2),
        name="adaln_mod",
    )(cond, w_ada, b_ada3)


def _modulated_norm(x, mod, g):
    ms = jnp.mean(x * x, axis=-1, keepdims=True)
    y = x * lax.rsqrt(ms + EPS) * g
    return y * (1.0 + mod[:, D_MODEL:2 * D_MODEL]) + mod[:, :D_MODEL]


def _inproj_kernel(x_ref, mod_ref, g_ref, w_ref, o_ref, ab_ref):
    hn = _modulated_norm(x_ref[...], mod_ref[0], g_ref[...]).astype(BF16)
    tn = 512
    for j in range(PROJ_W // tn):
        y = jnp.dot(hn, w_ref[:, j * tn:(j + 1) * tn], preferred_element_type=F32)
        o_ref[:, j * tn:(j + 1) * tn] = y.astype(BF16)
        if j == OFF_GAB // tn:
            ab_ref[...] = y[:, OFF_GAB % tn:OFF_GAB % tn + 128]


def _inproj_call(x2d, mod3, norm_g3, wcat, layer, rows_per_mod):
    t = x2d.shape[0]
    tm = 512
    if mod3.shape[0] == 1:
        mod_idx = lambda i: (0, 0, 0)
    else:
        mod_idx = lambda i: ((i * tm) // rows_per_mod, 0, 0)
    return pl.pallas_call(
        _inproj_kernel,
        grid=(t // tm,),
        in_specs=[pl.BlockSpec((tm, D_MODEL), lambda i: (i, 0)),
                  pl.BlockSpec((1, 1, 3 * D_MODEL), mod_idx),
                  _layer_spec((1, D_MODEL), layer, 1),
                  pl.BlockSpec((None, D_MODEL, PROJ_W), lambda i: (layer, 0, 0), pipeline_mode=pl.Buffered(1))],
        out_specs=[pl.BlockSpec((tm, PROJ_W), lambda i: (i, 0)), pl.BlockSpec((tm, 128), lambda i: (i, 0))],
        out_shape=[jax.ShapeDtypeStruct((t, PROJ_W), BF16), jax.ShapeDtypeStruct((t, 128), F32)],
        compiler_params=_params(40, 1),
        name="inproj",
    )(x2d, mod3, norm_g3, wcat)


def _stacked_heads(q, n_kv):
    rep = N_HEADS // n_kv
    return [jnp.concatenate([q[:, _hs(g * rep + r)] for r in range(rep)], axis=0) for g in range(n_kv)]


def _unstack_heads(outs, n_kv):
    rep = N_HEADS // n_kv
    m = outs[0].shape[0] // rep
    return jnp.concatenate([outs[g][r * m:(r + 1) * m] for g in range(n_kv) for r in range(rep)], axis=-1)


def _write_layer(ref, layer, value, stacked):
    if not stacked:
        ref[...] = value
        return
    for l in range(ref.shape[0]):
        ref[l] = value if l == layer else jnp.zeros(value.shape, value.dtype)


def _write_state(st_ref, layer, stacked, piece):
    for d in range(2):
        for h in range(N_HEADS):
            value = piece(d, h)
            if stacked:
                for l in range(st_ref.shape[0]):
                    st_ref[l, d, h] = value if l == layer else jnp.zeros(value.shape, value.dtype)
            else:
                st_ref[d, h] = value


def _attn_ctx_kernel(*refs, n_kv, norm, layer, first):
    if norm:
        q_ref, k_ref, v_ref, z_ref, qn_ref, kn_ref = refs[:6]
    else:
        q_ref, k_ref, v_ref, z_ref = refs[:4]
    o_ref, kt_ref, vt_ref = refs[-3:]
    q, k, v, z = q_ref[...], k_ref[...].astype(F32), v_ref[...].astype(F32), z_ref[...].astype(F32)
    if norm:
        q = _head_rms(q.astype(F32)) * qn_ref[...]
        k = _head_rms(k) * kn_ref[...]
    n_seq = kt_ref.shape[0]
    seq = k.shape[0] // n_seq
    qs, parts = [], []
    for s in range(n_seq):
        rows = slice(s * seq, (s + 1) * seq)
        _write_layer(kt_ref.at[s], layer, k[rows].T.reshape(n_kv, HEAD_DIM, seq), first)
        _write_layer(vt_ref.at[s], layer, v[rows].T.reshape(n_kv, HEAD_DIM, seq), first)
        qs += _stacked_heads(q[rows], n_kv)
        parts += [[(k[rows, _hs(g)], v[rows, _hs(g)], None, False)] for g in range(n_kv)]
    outs = _attend(qs, parts)
    o = jnp.concatenate([_unstack_heads(outs[s * n_kv:(s + 1) * n_kv], n_kv) for s in range(n_seq)], axis=0)
    o_ref[...] = (o * _silu(z)).astype(BF16)


def _attn_ctx_call(proj, prev, layer, batch, seq, off_q, off_k, off_v, off_z, n_kv, qn=None, kn=None):
    t = batch * seq
    kvw = n_kv * HEAD_DIM
    norm = qn is not None
    first = prev is None
    n_seq = CTX_SEQS_PER_STEP
    assert batch % n_seq == 0
    tm = n_seq * seq
    in_specs = [pl.BlockSpec((tm, BRANCH_W), lambda b: (b, off_q // BRANCH_W)),
                pl.BlockSpec((tm, kvw), lambda b: (b, off_k // kvw)),
                pl.BlockSpec((tm, kvw), lambda b: (b, off_v // kvw)),
                pl.BlockSpec((tm, BRANCH_W), lambda b: (b, off_z // BRANCH_W))]
    args = [proj, proj, proj, proj]
    if norm:
        in_specs += [_layer_spec((1, BRANCH_W), layer, 1), _layer_spec((1, kvw), layer, 1)]
        args += [qn, kn]
    aliases = {}
    if first:
        cache_spec = pl.BlockSpec((n_seq, DEPTH, n_kv, HEAD_DIM, seq), lambda b: (b, 0, 0, 0, 0))
    else:
        aliases = {len(args): 1, len(args) + 1: 2}
        in_specs += [pl.BlockSpec(memory_space=pl.ANY)] * 2
        args += list(prev)
        cache_spec = pl.BlockSpec((n_seq, None, n_kv, HEAD_DIM, seq), lambda b: (b, layer, 0, 0, 0))
    cache_shape = jax.ShapeDtypeStruct((batch, DEPTH, n_kv, HEAD_DIM, seq), F32)
    return pl.pallas_call(
        functools.partial(_attn_ctx_kernel, n_kv=n_kv, norm=norm, layer=layer, first=first),
        grid=(batch // n_seq,), in_specs=in_specs,
        out_specs=[pl.BlockSpec((tm, BRANCH_W), lambda b: (b, 0)), cache_spec, cache_spec],
        out_shape=[jax.ShapeDtypeStruct((t, BRANCH_W), BF16), cache_shape, cache_shape],
        input_output_aliases=aliases,
        compiler_params=_params(32, 1),
        name="attn_ctx_norm" if norm else "attn_ctx",
    )(*args)


def _attn_lat_kernel(q_ref, kv_ref, z_ref, ckt_ref, cvt_ref, qtab_ref, ktab_ref, qn_ref, kn_ref, o_ref,
                     k_s, v_s):
    kw = KV_HEADS * HEAD_DIM

    @pl.when(pl.program_id(1) == 0)
    def _():
        kv = kv_ref[...]
        k_s[...] = _rope(_head_rms(kv[:, :kw].astype(F32)) * kn_ref[...], ktab_ref).astype(BF16)
        v_s[...] = kv[:, kw:]

    q = _rope(_head_rms(q_ref[...].astype(F32)) * qn_ref[...], qtab_ref)
    k, v = k_s[...], v_s[...]
    outs = _attend(_stacked_heads(q, KV_HEADS),
                   [[(k[:, _hs(g)], v[:, _hs(g)], None, False), (ckt_ref[g], cvt_ref[g], None, True)]
                    for g in range(KV_HEADS)])
    o_ref[...] = (_unstack_heads(outs, KV_HEADS) * _silu(z_ref[...].astype(F32))).astype(BF16)


def _attn_lat_call(proj, cache_kt, cache_vt, layer, batch, seq, qtab, ktab, qn, kn):
    tq = 256
    nq = seq // tq
    past = cache_kt.shape[-1]
    kw = KV_HEADS * HEAD_DIM
    ctx_spec = pl.BlockSpec((None, None, KV_HEADS, HEAD_DIM, past), lambda b, i: (b, layer, 0, 0, 0))
    return pl.pallas_call(
        _attn_lat_kernel,
        grid=(batch, nq),
        in_specs=[pl.BlockSpec((tq, BRANCH_W), lambda b, i: (b * nq + i, OFF_AQKV // BRANCH_W)),
                  pl.BlockSpec((seq, 2 * kw), lambda b, i: (b, (OFF_AQKV + BRANCH_W) // (2 * kw))),
                  pl.BlockSpec((tq, BRANCH_W), lambda b, i: (b * nq + i, OFF_AZ // BRANCH_W)),
                  ctx_spec, ctx_spec,
                  pl.BlockSpec((3, tq, BRANCH_W), lambda b, i: (0, i, 0)),
                  pl.BlockSpec((3, seq, kw), lambda b, i: (0, 0, 0)),
                  _layer_spec((1, BRANCH_W), layer, 2),
                  _layer_spec((1, kw), layer, 2)],
        out_specs=pl.BlockSpec((tq, BRANCH_W), lambda b, i: (b * nq + i, 0)),
        out_shape=jax.ShapeDtypeStruct((batch * seq, BRANCH_W), BF16),
        scratch_shapes=[pltpu.VMEM((seq, kw), BF16), pltpu.VMEM((seq, kw), BF16)],
        compiler_params=_params(40, 2),
        name="attn_lat",
    )(proj, proj, proj, cache_kt, cache_vt, qtab, ktab, qn, kn)


def _na_bias_kernel(t_ref, o_ref):
    nblk = o_ref.shape[0]
    c = lax.broadcasted_iota(jnp.int32, (GRID_W, 2 * GRID_W), 0)
    j = lax.broadcasted_iota(jnp.int32, (GRID_W, 2 * GRID_W), 1)
    kc = j & (GRID_W - 1)
    dc = kc - c + (NA_COLS - 1)
    cs = jnp.clip(c - NA_COLS // 2, 0, GRID_W - NA_COLS)
    valid = jnp.logical_and(kc >= cs, kc < cs + NA_COLS)
    left = j < GRID_W

    def body(b, carry):
        b2 = jnp.minimum(b + 1, nblk - 1)
        acc = jnp.full((GRID_W, 2 * GRID_W), NEG_INF, F32)
        for i in range(N_DC):
            acc = jnp.where(dc == i, jnp.where(left, t_ref[b * N_DC + i], t_ref[b2 * N_DC + i]), acc)
        o_ref[b] = jnp.where(valid, acc * LOG2E, NEG_INF)
        return carry

    lax.fori_loop(0, nblk, body, 0)


def _na_bias_call(na_bias):
    nblk = DEPTH * N_HEADS * N_DR
    return pl.pallas_call(
        _na_bias_kernel,
        in_specs=[pl.BlockSpec(memory_space=pltpu.SMEM)],
        out_specs=pl.BlockSpec((nblk, GRID_W, 2 * GRID_W), lambda: (0, 0, 0)),
        out_shape=jax.ShapeDtypeStruct((nblk, GRID_W, 2 * GRID_W), F32),
        name="na_bias",
    )(na_bias.reshape(-1))


def _na_kernel(q_ref, k_ref, v_ref, z_ref, ckt_ref, cvt_ref, tb_ref, o_ref, *, rows):
    win = NA_ROWS * GRID_W
    qs, parts = [], []
    for i in range(NA_ROWS_PER_STEP):
        r = pl.program_id(1) * NA_ROWS_PER_STEP + i
        rs = jnp.clip(r - NA_ROWS // 2, 0, rows - NA_ROWS)
        r0 = pl.multiple_of(rs * GRID_W, GRID_W)
        kwin = k_ref[pl.ds(r0, win), :]
        vwin = v_ref[pl.ds(r0, win), :]
        q = q_ref[i * GRID_W:(i + 1) * GRID_W, :]
        dr0 = rs - r + NA_ROWS - 1
        for h in range(N_HEADS):
            bias = jnp.concatenate([tb_ref[h * N_DR + dr0 + 2 * p] for p in range(NA_ROWS // 2)], axis=1)
            qs.append(q[:, _hs(h)])
            parts.append([(kwin[:, _hs(h)], vwin[:, _hs(h)], bias, False), (ckt_ref[h], cvt_ref[h], None, True)])
    outs = _attend(qs, parts)
    o = jnp.concatenate([jnp.concatenate(outs[i * N_HEADS:(i + 1) * N_HEADS], axis=-1)
                         for i in range(NA_ROWS_PER_STEP)], axis=0)
    o_ref[...] = (o * _silu(z_ref[...].astype(F32))).astype(BF16)


def _na_call(proj, cache_kt, cache_vt, tb, layer, batch, seq):
    rows = seq // GRID_W
    assert rows >= NA_ROWS and rows % NA_ROWS_PER_STEP == 0
    steps = rows // NA_ROWS_PER_STEP
    tq = NA_ROWS_PER_STEP * GRID_W
    past = cache_kt.shape[-1]
    nblk = N_HEADS * N_DR
    cq = OFF_D // BRANCH_W
    ctx_spec = pl.BlockSpec((None, None, N_HEADS, HEAD_DIM, past), lambda b, r: (b, layer, 0, 0, 0))
    return pl.pallas_call(
        functools.partial(_na_kernel, rows=rows),
        grid=(batch, steps),
        in_specs=[pl.BlockSpec((tq, BRANCH_W), lambda b, r: (b * steps + r, cq)),
                  pl.BlockSpec((seq, BRANCH_W), lambda b, r: (b, cq + 1)),
                  pl.BlockSpec((seq, BRANCH_W), lambda b, r: (b, cq + 2)),
                  pl.BlockSpec((tq, BRANCH_W), lambda b, r: (b * steps + r, cq + 3)),
                  ctx_spec, ctx_spec,
                  pl.BlockSpec((nblk, GRID_W, 2 * GRID_W), lambda b, r: (layer, 0, 0))],
        out_specs=pl.BlockSpec((tq, BRANCH_W), lambda b, r: (b * steps + r, 0)),
        out_shape=jax.ShapeDtypeStruct((batch * seq, BRANCH_W), BF16),
        compiler_params=_params(32, 2),
        name="na_lat",
    )(proj, proj, proj, proj, cache_kt, cache_vt, tb)


def _gdn_kernel(*refs, seq, has_s0, layer, emit):
    qkv_ref, z_ref, ab_ref, cw_ref, par_ref, g_ref = refs[:6]
    s0_ref = refs[6] if has_s0 else None
    (q_s, k_s, v_s, gcb_s, bcb_s, r_s, mc_s, nc_s, qp_s, op_s, egl_s, s_s, oacc_ref) = refs[-13:]
    if emit == "none":
        o_ref, st_ref = refs[-14], None
    else:
        o_ref, st_ref = refs[-15], refs[-14]
    n_chunks = seq // CHUNK
    n_levels = CHUNK.bit_length() - 1
    qkv_w = 3 * BRANCH_W
    half = SHORT_CONV // 2
    pair_w = 2 * HEAD_DIM
    pairs = [slice(p * pair_w, (p + 1) * pair_w) for p in range(BRANCH_W // pair_w)]
    tr = 256
    cpt = tr // CHUNK
    head_sum = _head_block_matrix(BRANCH_W, 1.0)

    gc_i = lax.broadcasted_iota(jnp.int32, (128, BRANCH_W), 0)
    gh_j = lax.broadcasted_iota(jnp.int32, (128, BRANCH_W), 1) >> 6
    sel_beta = [jnp.where(gc_i == gh_j + 4 * d, 1.0, 0.0).astype(BF16) for d in range(2)]
    sel_gate = [jnp.where(gc_i == gh_j + 8 + 4 * d, 1.0, 0.0).astype(BF16) for d in range(2)]
    ti = lax.broadcasted_iota(jnp.int32, (tr, tr), 0)
    tj = lax.broadcasted_iota(jnp.int32, (tr, tr), 1)
    same_chunk = (ti >> 6) == (tj >> 6)
    tri = [jnp.where(jnp.logical_and(same_chunk, ti >= tj), 1.0, 0.0).astype(BF16),
           jnp.where(jnp.logical_and(same_chunk, ti <= tj), 1.0, 0.0).astype(BF16)]
    assert tr == BRANCH_W
    lane_head = lax.broadcasted_iota(jnp.int32, (1, BRANCH_W), 1) >> 6

    halo = 16
    edge = 8
    assert half <= edge
    si = lax.broadcasted_iota(jnp.int32, (tr, tr), 0)
    sj = lax.broadcasted_iota(jnp.int32, (tr, tr), 1)
    ei = lax.broadcasted_iota(jnp.int32, (edge, halo), 0)
    ej = lax.broadcasted_iota(jnp.int32, (edge, halo), 1)
    taps = [j for j in range(SHORT_CONV) if j != half]
    shift = {j: jnp.where(sj == si + (j - half), 1.0, 0.0).astype(BF16) for j in taps}
    shift_before = {j: jnp.where(ej == ei + (halo + j - half), 1.0, 0.0).astype(BF16) for j in taps if j < half}
    shift_after = {j: jnp.where(ej == ei + (j - half - edge), 1.0, 0.0).astype(BF16) for j in taps if j > half}
    for t in range(seq // tr):
        rows = slice(t * tr, (t + 1) * tr)
        x = qkv_ref[rows, :]
        y = x.astype(F32) * cw_ref[half:half + 1, :]
        for j in taps:
            y = y + jnp.dot(shift[j], x, preferred_element_type=F32) * cw_ref[j:j + 1, :]
        if t > 0:
            before = qkv_ref[t * tr - halo:t * tr, :]
            top = sum(jnp.dot(shift_before[j], before, preferred_element_type=F32) * cw_ref[j:j + 1, :]
                      for j in shift_before)
            y = jnp.concatenate([y[:edge] + top, y[edge:]], axis=0)
        if (t + 1) * tr < seq:
            after = qkv_ref[(t + 1) * tr:(t + 1) * tr + halo, :]
            bottom = sum(jnp.dot(shift_after[j], after, preferred_element_type=F32) * cw_ref[j:j + 1, :]
                         for j in shift_after)
            y = jnp.concatenate([y[:tr - edge], y[tr - edge:] + bottom], axis=0)
        y = _silu(y)
        qq, kk = y[:, :BRANCH_W], y[:, BRANCH_W:2 * BRANCH_W]
        q_s[rows, :] = qq * lax.rsqrt(_head_reduce(qq * qq, head_sum) + EPS) * SCALE
        k_s[rows, :] = kk * lax.rsqrt(_head_reduce(kk * kk, head_sum) + EPS)
        v_s[rows, :] = y[:, 2 * BRANCH_W:]
        x = ab_ref[rows, :]
        beta = jax.nn.sigmoid(x)
        xs = x + par_ref[0:1, :]
        softplus = jnp.maximum(xs, 0.0) + jnp.log1p(jnp.exp(-jnp.abs(xs)))
        la = -jnp.exp(par_ref[1:2, :]) * softplus
        for d in range(2):
            gc = _mm_exact(tri[d], la)
            gcb_s[d, rows, :] = _mm_exact_lhs(gc, sel_gate[d])
            bcb_s[d, rows, :] = _mm_exact_lhs(beta, sel_beta[d], terms=2)
            gt = gc.T[8:16, :]
            shifted = {s: (gt if s == 0 else pltpu.roll(gt, (s * HEAD_DIM) % tr, 1))
                       for s in range(1 - cpt, N_HEADS)}
            for c in range(cpt):
                r = jnp.zeros((1, BRANCH_W), F32)
                for h in range(N_HEADS):
                    r = jnp.where(lane_head == h, shifted[h - c][4 * d + h:4 * d + h + 1, :], r)
                r_s[d, (t * cpt + c) * 8:(t * cpt + c + 1) * 8, :] = jnp.broadcast_to(r, (8, BRANCH_W))

    for d in range(2):
        if has_s0:
            s_s[d] = jnp.concatenate([s0_ref[d, h] for h in range(N_HEADS)], axis=-1)
        else:
            s_s[d] = jnp.zeros((HEAD_DIM, BRANCH_W), F32)

    li = lax.broadcasted_iota(jnp.int32, (CHUNK, BRANCH_W), 0)
    lj = lax.broadcasted_iota(jnp.int32, (CHUNK, BRANCH_W), 1) & (HEAD_DIM - 1)
    incl = (li >= lj, li <= lj)
    strict = (li > lj, li < lj)
    level = [((li ^ lj) >> l) == 1 for l in range(n_levels)]
    first_head = lax.broadcasted_iota(jnp.int32, (CHUNK, pair_w), 1) < HEAD_DIM

    def expand(y):
        yb = y.astype(BF16)
        zero = jnp.zeros((CHUNK, pair_w), BF16)
        return [jnp.concatenate([jnp.where(first_head, yb[:, p], zero), jnp.where(first_head, zero, yb[:, p])],
                                axis=0) for p in pairs]

    def bdmm(x, ybd):
        xb = x.astype(BF16)
        return jnp.concatenate([jnp.dot(xb[:, p], ybd[i], preferred_element_type=F32)
                                for i, p in enumerate(pairs)], axis=1)

    def bdmm_nt(x, ybd):
        xb = x.astype(BF16)
        return jnp.concatenate([lax.dot_general(xb[:, p], ybd[i], (((1,), (1,)), ((), ())),
                                                preferred_element_type=F32)
                                for i, p in enumerate(pairs)], axis=1)

    def tn_diag(a, b):
        ab, bb = a.astype(BF16), b.astype(BF16)
        outs = []
        for p in pairs:
            full = lax.dot_general(ab[:, p], bb[:, p], (((0,), (0,)), ((), ())), preferred_element_type=F32)
            outs.append(jnp.where(first_head, full[:HEAD_DIM], full[HEAD_DIM:]))
        return jnp.concatenate(outs, axis=1)

    def prepare(chains):
        n = range(len(chains))
        dd = [d for d, _ in chains]
        rows = [pl.ds(_aligned(c * CHUNK, CHUNK), CHUNK) for _, c in chains]
        gcb = [gcb_s[dd[i], rows[i], :] for i in n]
        bcb = [bcb_s[dd[i], rows[i], :] for i in n]
        grow = [r_s[dd[i], pl.ds(_aligned(chains[i][1] * 8, 8), 8), :][0:1, :] for i in n]
        dm = [jnp.exp(jnp.where(incl[dd[i]], gcb[i] - grow[i], NEG_INF)) for i in n]
        k = [k_s[rows[i], :] for i in n]
        q = [q_s[rows[i], :] for i in n]
        v = [v_s[rows[i], :] for i in n]
        kq = [bdmm_nt(jnp.concatenate([k[i], q[i]], axis=0), expand(k[i])) for i in n]
        a = [jnp.where(strict[dd[i]], bcb[i] * kq[i][:CHUNK] * dm[i], 0.0) for i in n]
        tm = [-jnp.where(level[0], a[i], 0.0) for i in n]
        for l in range(1, n_levels):
            b = [jnp.where(level[l], a[i], 0.0) for i in n]
            y = [b[i] + bdmm(tm[i], expand(b[i])) for i in n]
            tm = [tm[i] - (y[i] + bdmm(y[i], expand(tm[i]))) for i in n]
        eg = [jnp.exp(gcb[i]) for i in n]
        bv = [bcb[i] * v[i] for i in n]
        bk = [bcb[i] * k[i] * eg[i] for i in n]
        u = [bv[i] + bdmm(tm[i], expand(bv[i])) for i in n]
        w = [bk[i] + bdmm(tm[i], expand(bk[i])) for i in n]
        gl = [gcb[i][CHUNK - 1:CHUNK, :] if dd[i] == 0 else gcb[i][0:1, :] for i in n]
        kd = [k[i] * jnp.exp(gl[i] - gcb[i]) for i in n]
        qkm = [kq[i][CHUNK:] * dm[i] for i in n]
        mc = [tn_diag(kd[i], w[i]) for i in n]
        nc = [tn_diag(kd[i], u[i]) for i in n]
        qp = [q[i] * eg[i] - bdmm(qkm[i], expand(w[i])) for i in n]
        op = [bdmm(qkm[i], expand(u[i])) for i in n]
        for i in n:
            d, c = chains[i]
            mc_s[d, rows[i], :] = mc[i].astype(BF16)
            nc_s[d, rows[i], :] = nc[i]
            qp_s[d, rows[i], :] = qp[i].astype(BF16)
            op_s[d, rows[i], :] = op[i]
            egl_s[d, pl.ds(_aligned(c * 8, 8), 8), :] = jnp.broadcast_to(jnp.exp(gl[i]), (8, BRANCH_W))

    group = min(PREP_CHUNKS, n_chunks)
    if n_chunks == group:
        prepare([(d, c) for c in range(group) for d in range(2)])
    else:
        def prep_body(j, carry):
            prepare([(d, j * group + c) for c in range(group) for d in range(2)])
            return carry
        lax.fori_loop(0, n_chunks // group, prep_body, 0)

    def scan_step(i):
        for d, c in ((0, i), (1, n_chunks - 1 - i)):
            rows = pl.ds(_aligned(c * CHUNK, CHUNK), CHUNK)
            s = s_s[d]
            sbd = expand(s)
            oacc_ref[d, rows, :] = bdmm(qp_s[d, rows, :], sbd) + op_s[d, rows, :]
            egl = egl_s[d, pl.ds(_aligned(c * 8, 8), 8), :][0:1, :]
            s_s[d] = s * egl - bdmm(mc_s[d, rows, :], sbd) + nc_s[d, rows, :]

    unroll = min(SCAN_UNROLL, n_chunks)

    def scan_body(j, carry):
        for i in range(unroll):
            scan_step(j * unroll + i)
        return carry

    lax.fori_loop(0, n_chunks // unroll, scan_body, 0)

    if st_ref is not None:
        _write_state(st_ref, layer, emit == "first", lambda d, h: s_s[d][:, _hs(h)])
    o = oacc_ref[0] + oacc_ref[1]
    ms = _head_reduce(o * o, _head_block_matrix(BRANCH_W, 1.0 / HEAD_DIM))
    o_ref[...] = (o * lax.rsqrt(ms + EPS) * g_ref[...] * _silu(z_ref[...].astype(F32))).astype(BF16)


def _state_spec(layer):
    return pl.BlockSpec((None, None, 2, N_HEADS, HEAD_DIM, HEAD_DIM), lambda b: (b, layer, 0, 0, 0, 0))


def _state_output(emit, prev, layer, batch, n_args):
    if emit == "none":
        return [], [], [], [], {}
    shape = jax.ShapeDtypeStruct((batch, DEPTH, 2, N_HEADS, HEAD_DIM, HEAD_DIM), F32)
    if emit == "first":
        spec = pl.BlockSpec((None, DEPTH, 2, N_HEADS, HEAD_DIM, HEAD_DIM), lambda b: (b, 0, 0, 0, 0, 0))
        return [], [], [spec], [shape], {}
    return [pl.BlockSpec(memory_space=pl.ANY)], [prev], [_state_spec(layer)], [shape], {n_args: 1}


def _gdn_call(proj, gab, cw8, par, norm_g, layer, batch, seq, s0=None, emit="none", prev=None):
    has_s0 = s0 is not None
    qkv_w = 3 * BRANCH_W
    in_specs = [pl.BlockSpec((seq, qkv_w), lambda b: (b, OFF_GQKV // qkv_w)),
                pl.BlockSpec((seq, BRANCH_W), lambda b: (b, OFF_GZ // BRANCH_W)),
                pl.BlockSpec((seq, 128), lambda b: (b, 0)),
                _layer_spec((8, qkv_w), layer, 1),
                _layer_spec((2, 128), layer, 1),
                _layer_spec((1, BRANCH_W), layer, 1)]
    args = [proj, proj, gab, cw8, par, norm_g]
    if has_s0:
        in_specs.append(_state_spec(layer))
        args.append(s0)
    st_in_specs, st_args, st_out_specs, st_shapes, aliases = _state_output(emit, prev, layer, batch, len(args))
    return pl.pallas_call(
        functools.partial(_gdn_kernel, seq=seq, has_s0=has_s0, layer=layer, emit=emit),
        grid=(batch,), in_specs=in_specs + st_in_specs,
        out_specs=[pl.BlockSpec((seq, BRANCH_W), lambda b: (b, 0))] + st_out_specs,
        out_shape=[jax.ShapeDtypeStruct((batch * seq, BRANCH_W), BF16)] + st_shapes,
        input_output_aliases=aliases,
        scratch_shapes=[pltpu.VMEM((seq, BRANCH_W), F32),
                        pltpu.VMEM((seq, BRANCH_W), F32),
                        pltpu.VMEM((seq, BRANCH_W), F32),
                        pltpu.VMEM((2, seq, BRANCH_W), F32),
                        pltpu.VMEM((2, seq, BRANCH_W), F32),
                        pltpu.VMEM((2, seq // CHUNK * 8, BRANCH_W), F32),
                        pltpu.VMEM((2, seq, BRANCH_W), BF16),
                        pltpu.VMEM((2, seq, BRANCH_W), F32),
                        pltpu.VMEM((2, seq, BRANCH_W), BF16),
                        pltpu.VMEM((2, seq, BRANCH_W), F32),
                        pltpu.VMEM((2, seq // CHUNK * 8, BRANCH_W), F32),
                        pltpu.VMEM((2, HEAD_DIM, BRANCH_W), F32),
                        pltpu.VMEM((2, seq, BRANCH_W), F32)],
        compiler_params=_params(48, 1),
        name="gdn",
    )(*args, *st_args)


_RET_LOG_GAMMA = [[float(np.log1p(-np.exp2(-(base + h)))) for h in range(N_HEADS)] for base in RET_DECAY_BASE]


def _ret_kernel(*refs, seq, has_s0, layer, emit):
    qkv_ref, z_ref, g_ref = refs[:3]
    s0_ref = refs[3] if has_s0 else None
    o_ref, st_ref = (refs[-1], None) if emit == "none" else (refs[-2], refs[-1])
    tile = RET_TILE
    n_tiles = seq // tile
    heads = range(N_HEADS)
    lgf, lgb = _RET_LOG_GAMMA
    a = lax.broadcasted_iota(jnp.int32, (tile, 1), 0).astype(F32)
    ef = [jnp.exp(a * lgf[h]) for h in heads]
    eif = [jnp.exp(-a * lgf[h]) for h in heads]
    eb = [jnp.exp(a * lgb[h]) for h in heads]
    eib = [jnp.exp(-a * lgb[h]) for h in heads]
    gf_tile = [float(np.exp(tile * lgf[h])) for h in heads]
    gb_tile = [float(np.exp(tile * lgb[h])) for h in heads]
    ii = lax.broadcasted_iota(jnp.int32, (tile, tile), 0)
    jj = lax.broadcasted_iota(jnp.int32, (tile, tile), 1)

    def rows(t):
        return slice(t * tile, (t + 1) * tile)

    def head_cols(t, part, h):
        return qkv_ref[rows(t), part * BRANCH_W + h * HEAD_DIM:part * BRANCH_W + (h + 1) * HEAD_DIM]

    kf = [[head_cols(t, 1, h) * eif[h] for h in heads] for t in range(n_tiles)]
    kb = [[head_cols(t, 1, h) * eb[h] for h in heads] for t in range(n_tiles)]
    vs = [[head_cols(t, 2, h) for h in heads] for t in range(n_tiles)]
    use_states = has_s0 or n_tiles > 1 or st_ref is not None
    if use_states:
        kvf = [[_mm_tn(kf[t][h], vs[t][h]) for h in heads] for t in range(n_tiles)]
        kvb = [[_mm_tn(kb[t][h], vs[t][h]) for h in heads] for t in range(n_tiles)]
        zero = jnp.zeros((HEAD_DIM, HEAD_DIM), F32)
        zf = [[(float(np.exp(lgf[h])) * s0_ref[0, h]) if has_s0 else zero for h in heads]]
        for t in range(n_tiles):
            zf.append([gf_tile[h] * (zf[t][h] + kvf[t][h]) for h in heads])
        acc = [s0_ref[1, h] if has_s0 else zero for h in heads]
        zb = [None] * n_tiles
        for t in reversed(range(n_tiles)):
            zb[t] = [gb_tile[h] * acc[h] for h in heads]
            acc = [zb[t][h] + kvb[t][h] for h in heads]
        if st_ref is not None:
            stf = [zf[n_tiles][h] * float(np.exp(-lgf[h])) for h in heads]
            _write_state(st_ref, layer, emit == "first", lambda d, h: (stf, acc)[d][h])
    tiles = []
    for t in range(n_tiles):
        q = [head_cols(t, 0, h) * SCALE for h in heads]
        qf = [q[h] * ef[h] for h in heads]
        qb = [q[h] * eib[h] for h in heads]
        sd = [jnp.where(ii >= jj, _mm_nt(qf[h], kf[t][h]), 0.0) + jnp.where(ii <= jj, _mm_nt(qb[h], kb[t][h]), 0.0)
              for h in heads]
        o = [_mm(sd[h], vs[t][h]) for h in heads]
        if has_s0 or n_tiles > 1:
            o = [o[h] + _mm(qf[h], zf[t][h]) + _mm(qb[h], zb[t][h]) for h in heads]
        ms = [jnp.mean(o[h] * o[h], axis=-1, keepdims=True) for h in heads]
        tiles.append(jnp.concatenate([o[h] * lax.rsqrt(ms[h] + EPS) * g_ref[...] for h in heads], axis=-1))
    o_ref[...] = (jnp.concatenate(tiles, axis=0) * _silu(z_ref[...].astype(F32))).astype(BF16)


def _ret_call(proj, norm_g, layer, batch, seq, s0=None, emit="none", prev=None):
    has_s0 = s0 is not None
    qkv_w = 3 * BRANCH_W
    in_specs = [pl.BlockSpec((seq, qkv_w), lambda b: (b, OFF_C // qkv_w)),
                pl.BlockSpec((seq, BRANCH_W), lambda b: (b, (OFF_C + qkv_w) // BRANCH_W)),
                _layer_spec((1, HEAD_DIM), layer, 1)]
    args = [proj, proj, norm_g]
    if has_s0:
        in_specs.append(_state_spec(layer))
        args.append(s0)
    st_in_specs, st_args, st_out_specs, st_shapes, aliases = _state_output(emit, prev, layer, batch, len(args))
    return pl.pallas_call(
        functools.partial(_ret_kernel, seq=seq, has_s0=has_s0, layer=layer, emit=emit),
        grid=(batch,), in_specs=in_specs + st_in_specs,
        out_specs=[pl.BlockSpec((seq, BRANCH_W), lambda b: (b, 0))] + st_out_specs,
        out_shape=[jax.ShapeDtypeStruct((batch * seq, BRANCH_W), BF16)] + st_shapes,
        input_output_aliases=aliases,
        compiler_params=_params(48, 1),
        name="retention",
    )(*args, *st_args)


def _out_kernel(*refs, final):
    if final:
        (h_ref, mod_ref, g_ref, oa_ref, ob_ref, oc_ref, od_ref, wg_ref, wb_ref, wo_ref, fn_ref,
         o_ref, y_ref) = refs
    else:
        h_ref, mod_ref, g_ref, oa_ref, ob_ref, oc_ref, od_ref, wg_ref, wb_ref, wo_ref, o_ref = refs
    x = h_ref[...]
    mod = mod_ref[0]
    hn = _modulated_norm(x, mod, g_ref[...]).astype(BF16)
    merged = None
    for n, br_ref in enumerate((oa_ref, ob_ref, oc_ref, od_ref)):
        gate = jax.nn.sigmoid(jnp.dot(hn, wg_ref[:, n * D_MODEL:(n + 1) * D_MODEL], preferred_element_type=F32))
        up = jnp.dot(br_ref[...], wb_ref[n], preferred_element_type=F32)
        merged = gate * up if merged is None else merged + gate * up
    out = jnp.dot(merged.astype(BF16), wo_ref[...], preferred_element_type=F32)
    hnew = x + mod[:, 2 * D_MODEL:] * out
    o_ref[...] = hnew
    if final:
        ms = jnp.mean(hnew * hnew, axis=-1, keepdims=True)
        y_ref[...] = hnew * lax.rsqrt(ms + EPS) * fn_ref[...]


def _out_call(h2d, mod3, norm_g3, branches, wg, wb, wo, layer, rows_per_mod, final_norm=None):
    t = h2d.shape[0]
    tm = 512
    final = final_norm is not None
    if mod3.shape[0] == 1:
        mod_idx = lambda i: (0, 0, 0)
    else:
        mod_idx = lambda i: ((i * tm) // rows_per_mod, 0, 0)
    once = pl.Buffered(1)
    in_specs = [pl.BlockSpec((tm, D_MODEL), lambda i: (i, 0)),
                pl.BlockSpec((1, 1, 3 * D_MODEL), mod_idx),
                _layer_spec((1, D_MODEL), layer, 1)]
    in_specs += [pl.BlockSpec((tm, BRANCH_W), lambda i: (i, 0))] * N_BRANCH
    in_specs += [pl.BlockSpec((None, D_MODEL, N_BRANCH * D_MODEL), lambda i: (layer, 0, 0), pipeline_mode=once),
                 pl.BlockSpec((None, N_BRANCH, BRANCH_W, D_MODEL), lambda i: (layer, 0, 0, 0), pipeline_mode=once),
                 pl.BlockSpec((None, D_MODEL, D_MODEL), lambda i: (layer, 0, 0), pipeline_mode=once)]
    args = [h2d, mod3, norm_g3, *branches, wg, wb, wo]
    out_specs = [pl.BlockSpec((tm, D_MODEL), lambda i: (i, 0))]
    out_shape = [jax.ShapeDtypeStruct((t, D_MODEL), F32)]
    if final:
        in_specs.append(pl.BlockSpec((1, D_MODEL), lambda i: (0, 0)))
        args.append(final_norm.reshape(1, D_MODEL))
        out_specs.append(pl.BlockSpec((tm, D_MODEL), lambda i: (i, 0)))
        out_shape.append(jax.ShapeDtypeStruct((t, D_MODEL), F32))
    return pl.pallas_call(
        functools.partial(_out_kernel, final=final),
        grid=(t // tm,), in_specs=in_specs, out_specs=out_specs, out_shape=out_shape,
        compiler_params=_params(48, 1),
        name="merge_out_final" if final else "merge_out",
    )(*args)


def _prep_weights(w_in):
    offs = np.concatenate([[0], np.cumsum(IN_SPLITS)])
    seg = lambda i, j: w_in[:, :, offs[i]:offs[j]]
    pad = jnp.zeros((DEPTH, D_MODEL, PROJ_W - OFF_GAB - IN_SPLITS[5]), w_in.dtype)
    wcat = jnp.concatenate([seg(7, 11), seg(11, 15), seg(6, 7), seg(4, 5), seg(0, 3), seg(3, 4), seg(5, 6), pad],
                           axis=2).astype(BF16)
    return wcat, seg(15, 16).astype(BF16)


def _rope_tables(seq):
    t = jnp.arange(seq)
    quarter = HEAD_DIM // 4
    inv = ROPE_THETA ** (-jnp.arange(quarter, dtype=F32) / quarter)

    def half(pos):
        ang = pos.astype(F32)[:, None] * inv
        c, s, zero = jnp.cos(ang), jnp.sin(ang), jnp.zeros_like(ang)
        return jnp.concatenate([c, c], -1), jnp.concatenate([-s, zero], -1), jnp.concatenate([zero, s], -1)

    parts = [jnp.concatenate([a, b], -1) for a, b in zip(half(t // GRID_W), half(t % GRID_W))]
    tab = jnp.stack(parts)
    return jnp.tile(tab, (1, 1, N_HEADS)), jnp.tile(tab, (1, 1, KV_HEADS))


def _layer(h2d, batch, seq, mod, pw, layer, ctx, caches, final_norm):
    proj, gab = _inproj_call(h2d, mod, pw["norm_g"], pw["wcat"], layer, seq)
    kw = KV_HEADS * HEAD_DIM
    if ctx is None:
        emit = "first" if caches is None else "update"
        akv, nkv, sg_all, sr_all = caches or (None, None, None, None)
        oa, *akv = _attn_ctx_call(proj, akv, layer, batch, seq, OFF_AQKV, OFF_AQKV + BRANCH_W,
                                  OFF_AQKV + BRANCH_W + kw, OFF_AZ, KV_HEADS, pw["qn"], pw["kn"])
        od, *nkv = _attn_ctx_call(proj, nkv, layer, batch, seq, OFF_D, OFF_D + BRANCH_W,
                                  OFF_D + 2 * BRANCH_W, OFF_D + 3 * BRANCH_W, N_HEADS)
        ob, sg_all = _gdn_call(proj, gab, pw["cw8"], pw["gdn_par"], pw["gdn_norm"], layer, batch, seq,
                               emit=emit, prev=sg_all)
        oc, sr_all = _ret_call(proj, pw["ret_norm"], layer, batch, seq, emit=emit, prev=sr_all)
        caches = (akv, nkv, sg_all, sr_all)
    else:
        oa = _attn_lat_call(proj, ctx["akt"], ctx["avt"], layer, batch, seq, ctx["qtab"], ctx["ktab"],
                            pw["qn"], pw["kn"])
        od = _na_call(proj, ctx["nkt"], ctx["nvt"], ctx["tb"], layer, batch, seq)
        ob, = _gdn_call(proj, gab, pw["cw8"], pw["gdn_par"], pw["gdn_norm"], layer, batch, seq, s0=ctx["sg"])
        oc, = _ret_call(proj, pw["ret_norm"], layer, batch, seq, s0=ctx["sr"])
    outs = _out_call(h2d, mod, pw["norm_g"], (oa, ob, oc, od), pw["wg"], pw["wb"], pw["wo"], layer, seq, final_norm)
    return outs, caches


def kernel(x_prompt, x_sample, cache_attn_k, cache_attn_v, cache_na_k, cache_na_v, state_gdn, state_ret, c, c_ctx, w_ada, b_ada, norm_g, w_in, conv_w, gdn_a_log, gdn_dt_bias, gdn_norm, attn_q_norm, attn_k_norm, ret_norm, na_bias, w_branch, w_out, final_norm):
    batch, seq, _ = x_prompt.shape
    dbatch, dseq, _ = x_sample.shape
    assert dbatch == 8, "the modulation kernel handles exactly one sublane tile of conditioning rows"

    wcat, wg = _prep_weights(w_in)
    par = jnp.zeros((DEPTH, 2, 128), F32)
    par = par.at[:, 0, 8:16].set(gdn_dt_bias.reshape(DEPTH, 8)).at[:, 1, 8:16].set(gdn_a_log.reshape(DEPTH, 8))
    pw = dict(
        w_ada=w_ada, b_ada=b_ada.reshape(DEPTH, 1, 3 * D_MODEL), norm_g=norm_g.reshape(DEPTH, 1, D_MODEL),
        wcat=wcat, wg=wg, wb=w_branch.astype(BF16), wo=w_out.astype(BF16),
        cw8=jnp.concatenate([conv_w, jnp.zeros((DEPTH, 8 - SHORT_CONV, 3 * BRANCH_W), F32)], axis=1),
        gdn_par=par,
        gdn_norm=jnp.tile(gdn_norm, (1, N_HEADS)).reshape(DEPTH, 1, BRANCH_W),
        ret_norm=ret_norm.reshape(DEPTH, 1, HEAD_DIM),
        qn=jnp.tile(attn_q_norm, (1, N_HEADS)).reshape(DEPTH, 1, BRANCH_W),
        kn=jnp.tile(attn_k_norm, (1, KV_HEADS)).reshape(DEPTH, 1, KV_HEADS * HEAD_DIM))

    cond = jnp.concatenate([jnp.broadcast_to(c_ctx, (8, D_MODEL)), c], axis=0)
    mods = _mod_call(cond, w_ada, pw["b_ada"])

    h = x_prompt.reshape(batch * seq, D_MODEL)
    caches = None
    for l in range(DEPTH):
        outs, caches = _layer(h, batch, seq, mods[l, 0:1].reshape(1, 1, 3 * D_MODEL), pw, l, None, caches,
                              final_norm if l == DEPTH - 1 else None)
        h = outs[0]
    y_prompt = outs[1].reshape(batch, seq, D_MODEL)
    token_major = lambda a: a.transpose(0, 1, 4, 2, 3)
    (akt, avt), (nkt, nvt), new_state_gdn, new_state_ret = caches
    new_attn_k, new_attn_v, new_na_k, new_na_v = (token_major(a) for a in (akt, avt, nkt, nvt))

    qtab, ktab = _rope_tables(dseq)
    feature_major = lambda a: a.transpose(0, 1, 3, 4, 2)
    ctx = dict(akt=feature_major(cache_attn_k), avt=feature_major(cache_attn_v),
               nkt=feature_major(cache_na_k), nvt=feature_major(cache_na_v),
               sg=state_gdn, sr=state_ret, tb=_na_bias_call(na_bias), qtab=qtab, ktab=ktab)
    h = x_sample.reshape(dbatch * dseq, D_MODEL)
    for l in range(DEPTH):
        outs, _ = _layer(h, dbatch, dseq, mods[l, 8:16].reshape(dbatch, 1, 3 * D_MODEL), pw, l, ctx, None,
                         final_norm if l == DEPTH - 1 else None)
        h = outs[0]
    y_sample = outs[1].reshape(dbatch, dseq, D_MODEL)
    return (y_prompt, y_sample, new_attn_k, new_attn_v, new_na_k, new_na_v, new_state_gdn, new_state_ret)
```

```python
import functools

import numpy as np
import jax
import jax.numpy as jnp
from jax import lax
from jax.experimental import pallas as pl
from jax.experimental.pallas import tpu as pltpu

F32 = jnp.float32
BF16 = jnp.bfloat16

D_MODEL = 1024
HEAD_DIM = 64
N_HEADS = 4
KV_HEADS = N_HEADS // 2
BRANCH_W = N_HEADS * HEAD_DIM
N_BRANCH = 4
DEPTH = 2
GRID_W = 64
CHUNK = 64
PREP_CHUNKS = 4
SCAN_UNROLL = 4
assert CHUNK == HEAD_DIM
SHORT_CONV = 5
NA_ROWS = 8
NA_COLS = 16
RET_ROWS_PER_STEP = 1024
CTX_SEQS_PER_STEP = 4
NA_ROWS_PER_STEP = 4
N_DR = 2 * NA_ROWS - 1
N_DC = 2 * NA_COLS - 1
ROPE_THETA = 10000.0
RET_DECAY_BASE = (5.0, 5.5)
RET_TILE = 256
EPS = 1e-6
SCALE = HEAD_DIM ** -0.5
LOG2E = 1.4426950408889634
NEG_INF = float("-inf")

IN_SPLITS = (256, 128, 128, 256, 768, 16, 256, 256, 256, 256, 256, 256, 256, 256, 256, 4096)
PROJ_W = 4096
OFF_C = 0
OFF_D = 1024
OFF_GZ = 2048
OFF_GQKV = 2304
OFF_AQKV = 3072
OFF_AZ = 3584
OFF_GAB = 3840

V7X_VMEM_BYTES = 64 * 1024 * 1024
MIB = 1024 * 1024


def _params(vmem_mib, n_axes):
    assert vmem_mib * MIB < V7X_VMEM_BYTES
    return pltpu.CompilerParams(dimension_semantics=("arbitrary",) * n_axes,
                                vmem_limit_bytes=vmem_mib * MIB)


def _layer_spec(block, layer, n_grid):
    zeros = (0,) * len(block)
    if n_grid == 1:
        return pl.BlockSpec((None,) + block, lambda i: (layer,) + zeros)
    return pl.BlockSpec((None,) + block, lambda i, j: (layer,) + zeros)


def _mm(a, b):
    return jnp.dot(a.astype(BF16), b.astype(BF16), preferred_element_type=F32)


def _mm_nt(a, b):
    return lax.dot_general(a.astype(BF16), b.astype(BF16), (((1,), (1,)), ((), ())),
                           preferred_element_type=F32)


def _mm_tn(a, b):
    return lax.dot_general(a.astype(BF16), b.astype(BF16), (((0,), (0,)), ((), ())),
                           preferred_element_type=F32)


def _split3(x):
    hi = x.astype(BF16)
    r = x - hi.astype(F32)
    mid = r.astype(BF16)
    lo = (r - mid.astype(F32)).astype(BF16)
    return hi, mid, lo


def _mm_exact(sel, x):
    hi, mid, lo = _split3(x)
    return (jnp.dot(sel, hi, preferred_element_type=F32) + jnp.dot(sel, mid, preferred_element_type=F32)
            + jnp.dot(sel, lo, preferred_element_type=F32))


def _mm_exact_lhs(x, sel, terms=3):
    return sum(jnp.dot(part, sel, preferred_element_type=F32) for part in _split3(x)[:terms])


def _silu(x):
    return x * jax.nn.sigmoid(x)


def _head_block_matrix(width, value):
    ri = lax.broadcasted_iota(jnp.int32, (width, width), 0) >> 6
    ci = lax.broadcasted_iota(jnp.int32, (width, width), 1) >> 6
    return jnp.where(ri == ci, value, 0.0).astype(BF16)


def _head_reduce(x, g):
    hi = x.astype(BF16)
    lo = (x - hi.astype(F32)).astype(BF16)
    return jnp.dot(hi, g, preferred_element_type=F32) + jnp.dot(lo, g, preferred_element_type=F32)


def _head_rms(x):
    ms = _head_reduce(x * x, _head_block_matrix(x.shape[1], 1.0 / HEAD_DIM))
    return x * lax.rsqrt(ms + EPS)


def _rope(x, tab_ref):
    w = x.shape[1]
    return (x * tab_ref[0] + pltpu.roll(x, w - 16, 1) * tab_ref[1] + pltpu.roll(x, 16, 1) * tab_ref[2])


def _attend(qs, parts):
    groups = range(len(qs))
    qs = [(q.astype(F32) * (SCALE * LOG2E)).astype(BF16) for q in qs]

    def score(q, part):
        k, _, bias, feature_major = part
        s = _mm(q, k) if feature_major else _mm_nt(q, k)
        return s if bias is None else s + bias

    scores = [[score(qs[g], part) for part in parts[g]] for g in groups]
    m = [functools.reduce(jnp.maximum, [s.max(axis=-1, keepdims=True) for s in scores[g]]) for g in groups]
    p = [[jnp.exp2(s - m[g]) for s in scores[g]] for g in groups]
    den = [sum(x.sum(axis=-1, keepdims=True) for x in p[g]) for g in groups]
    out = [sum(_mm_nt(x, part[1]) if part[3] else _mm(x, part[1]) for x, part in zip(p[g], parts[g]))
           for g in groups]
    return [out[g] / den[g] for g in groups]


def _hs(h):
    return slice(h * HEAD_DIM, (h + 1) * HEAD_DIM)


def _aligned(x, m):
    return x if isinstance(x, int) else pl.multiple_of(x, m)


def _mod_kernel(c_ref, w_ref, b_ref, o_ref):
    o_ref[...] = _mm(_silu(c_ref[...]), w_ref[...]) + b_ref[...]


def _mod_call(cond, w_ada, b_ada3):
    tn = 512
    rows = cond.shape[0]
    return pl.pallas_call(
        _mod_kernel,
        grid=(DEPTH, 3 * D_MODEL // tn),
        in_specs=[pl.BlockSpec((rows, D_MODEL), lambda l, j: (0, 0)),
                  pl.BlockSpec((None, D_MODEL, tn), lambda l, j: (l, 0, j)),
                  pl.BlockSpec((None, 1, tn), lambda l, j: (l, 0, j))],
        out_specs=pl.BlockSpec((None, rows, tn), lambda l, j: (l, 0, j)),
        out_shape=jax.ShapeDtypeStruct((DEPTH, rows, 3 * D_MODEL), F32),
        compiler_params=_params(24, 2),
        name="adaln_mod",
    )(cond, w_ada, b_ada3)


def _modulated_norm(x, mod, g):
    ms = jnp.mean(x * x, axis=-1, keepdims=True)
    y = x * lax.rsqrt(ms + EPS) * g
    return y * (1.0 + mod[:, D_MODEL:2 * D_MODEL]) + mod[:, :D_MODEL]


def _inproj_kernel(x_ref, mod_ref, g_ref, w_ref, o_ref, ab_ref):
    hn = _modulated_norm(x_ref[...], mod_ref[0], g_ref[...]).astype(BF16)
    tn = 512
    for j in range(PROJ_W // tn):
        y = jnp.dot(hn, w_ref[:, j * tn:(j + 1) * tn], preferred_element_type=F32)
        o_ref[:, j * tn:(j + 1) * tn] = y.astype(BF16)
        if j == OFF_GAB // tn:
            ab_ref[...] = y[:, OFF_GAB % tn:OFF_GAB % tn + 128]


def _inproj_call(x2d, mod3, norm_g3, wcat, layer, rows_per_mod):
    t = x2d.shape[0]
    tm = 512
    if mod3.shape[0] == 1:
        mod_idx = lambda i: (0, 0, 0)
    else:
        mod_idx = lambda i: ((i * tm) // rows_per_mod, 0, 0)
    return pl.pallas_call(
        _inproj_kernel,
        grid=(t // tm,),
        in_specs=[pl.BlockSpec((tm, D_MODEL), lambda i: (i, 0)),
                  pl.BlockSpec((1, 1, 3 * D_MODEL), mod_idx),
                  _layer_spec((1, D_MODEL), layer, 1),
                  pl.BlockSpec((None, D_MODEL, PROJ_W), lambda i: (layer, 0, 0), pipeline_mode=pl.Buffered(1))],
        out_specs=[pl.BlockSpec((tm, PROJ_W), lambda i: (i, 0)), pl.BlockSpec((tm, 128), lambda i: (i, 0))],
        out_shape=[jax.ShapeDtypeStruct((t, PROJ_W), BF16), jax.ShapeDtypeStruct((t, 128), F32)],
        compiler_params=_params(40, 1),
        name="inproj",
    )(x2d, mod3, norm_g3, wcat)


def _stacked_heads(q, n_kv):
    rep = N_HEADS // n_kv
    return [jnp.concatenate([q[:, _hs(g * rep + r)] for r in range(rep)], axis=0) for g in range(n_kv)]


def _unstack_heads(outs, n_kv):
    rep = N_HEADS // n_kv
    m = outs[0].shape[0] // rep
    return jnp.concatenate([outs[g][r * m:(r + 1) * m] for g in range(n_kv) for r in range(rep)], axis=-1)


def _write_layer(ref, layer, value, stacked):
    if not stacked:
        ref[...] = value
        return
    for l in range(ref.shape[0]):
        ref[l] = value if l == layer else jnp.zeros(value.shape, value.dtype)


def _write_state(st_ref, layer, stacked, piece):
    for d in range(2):
        for h in range(N_HEADS):
            value = piece(d, h)
            if stacked:
                for l in range(st_ref.shape[0]):
                    st_ref[l, d, h] = value if l == layer else jnp.zeros(value.shape, value.dtype)
            else:
                st_ref[d, h] = value


def _attn_ctx_kernel(*refs, n_kv, norm, layer, first):
    if norm:
        q_ref, k_ref, v_ref, z_ref, qn_ref, kn_ref = refs[:6]
    else:
        q_ref, k_ref, v_ref, z_ref = refs[:4]
    o_ref, kt_ref, vt_ref = refs[-3:]
    q, k, v, z = q_ref[...], k_ref[...].astype(F32), v_ref[...].astype(F32), z_ref[...].astype(F32)
    if norm:
        q = _head_rms(q.astype(F32)) * qn_ref[...]
        k = _head_rms(k) * kn_ref[...]
    n_seq = kt_ref.shape[0]
    seq = k.shape[0] // n_seq
    qs, parts = [], []
    for s in range(n_seq):
        rows = slice(s * seq, (s + 1) * seq)
        _write_layer(kt_ref.at[s], layer, k[rows].T.reshape(n_kv, HEAD_DIM, seq), first)
        _write_layer(vt_ref.at[s], layer, v[rows].T.reshape(n_kv, HEAD_DIM, seq), first)
        qs += _stacked_heads(q[rows], n_kv)
        parts += [[(k[rows, _hs(g)], v[rows, _hs(g)], None, False)] for g in range(n_kv)]
    outs = _attend(qs, parts)
    o = jnp.concatenate([_unstack_heads(outs[s * n_kv:(s + 1) * n_kv], n_kv) for s in range(n_seq)], axis=0)
    o_ref[...] = (o * _silu(z)).astype(BF16)


def _attn_ctx_call(proj, prev, layer, batch, seq, off_q, off_k, off_v, off_z, n_kv, qn=None, kn=None):
    t = batch * seq
    kvw = n_kv * HEAD_DIM
    norm = qn is not None
    first = prev is None
    n_seq = CTX_SEQS_PER_STEP
    assert batch % n_seq == 0
    tm = n_seq * seq
    in_specs = [pl.BlockSpec((tm, BRANCH_W), lambda b: (b, off_q // BRANCH_W)),
                pl.BlockSpec((tm, kvw), lambda b: (b, off_k // kvw)),
                pl.BlockSpec((tm, kvw), lambda b: (b, off_v // kvw)),
                pl.BlockSpec((tm, BRANCH_W), lambda b: (b, off_z // BRANCH_W))]
    args = [proj, proj, proj, proj]
    if norm:
        in_specs += [_layer_spec((1, BRANCH_W), layer, 1), _layer_spec((1, kvw), layer, 1)]
        args += [qn, kn]
    aliases = {}
    if first:
        cache_spec = pl.BlockSpec((n_seq, DEPTH, n_kv, HEAD_DIM, seq), lambda b: (b, 0, 0, 0, 0))
    else:
        aliases = {len(args): 1, len(args) + 1: 2}
        in_specs += [pl.BlockSpec(memory_space=pl.ANY)] * 2
        args += list(prev)
        cache_spec = pl.BlockSpec((n_seq, None, n_kv, HEAD_DIM, seq), lambda b: (b, layer, 0, 0, 0))
    cache_shape = jax.ShapeDtypeStruct((batch, DEPTH, n_kv, HEAD_DIM, seq), F32)
    return pl.pallas_call(
        functools.partial(_attn_ctx_kernel, n_kv=n_kv, norm=norm, layer=layer, first=first),
        grid=(batch // n_seq,), in_specs=in_specs,
        out_specs=[pl.BlockSpec((tm, BRANCH_W), lambda b: (b, 0)), cache_spec, cache_spec],
        out_shape=[jax.ShapeDtypeStruct((t, BRANCH_W), BF16), cache_shape, cache_shape],
        input_output_aliases=aliases,
        compiler_params=_params(32, 1),
        name="attn_ctx_norm" if norm else "attn_ctx",
    )(*args)


def _attn_lat_kernel(q_ref, kv_ref, z_ref, ckt_ref, cvt_ref, qtab_ref, ktab_ref, qn_ref, kn_ref, o_ref,
                     k_s, v_s):
    kw = KV_HEADS * HEAD_DIM

    @pl.when(pl.program_id(1) == 0)
    def _():
        kv = kv_ref[...]
        k_s[...] = _rope(_head_rms(kv[:, :kw].astype(F32)) * kn_ref[...], ktab_ref).astype(BF16)
        v_s[...] = kv[:, kw:]

    q = _rope(_head_rms(q_ref[...].astype(F32)) * qn_ref[...], qtab_ref)
    k, v = k_s[...], v_s[...]
    outs = _attend(_stacked_heads(q, KV_HEADS),
                   [[(k[:, _hs(g)], v[:, _hs(g)], None, False), (ckt_ref[g], cvt_ref[g], None, True)]
                    for g in range(KV_HEADS)])
    o_ref[...] = (_unstack_heads(outs, KV_HEADS) * _silu(z_ref[...].astype(F32))).astype(BF16)


def _attn_lat_call(proj, cache_kt, cache_vt, layer, batch, seq, qtab, ktab, qn, kn):
    tq = 256
    nq = seq // tq
    past = cache_kt.shape[-1]
    kw = KV_HEADS * HEAD_DIM
    ctx_spec = pl.BlockSpec((None, None, KV_HEADS, HEAD_DIM, past), lambda b, i: (b, layer, 0, 0, 0))
    return pl.pallas_call(
        _attn_lat_kernel,
        grid=(batch, nq),
        in_specs=[pl.BlockSpec((tq, BRANCH_W), lambda b, i: (b * nq + i, OFF_AQKV // BRANCH_W)),
                  pl.BlockSpec((seq, 2 * kw), lambda b, i: (b, (OFF_AQKV + BRANCH_W) // (2 * kw))),
                  pl.BlockSpec((tq, BRANCH_W), lambda b, i: (b * nq + i, OFF_AZ // BRANCH_W)),
                  ctx_spec, ctx_spec,
                  pl.BlockSpec((3, tq, BRANCH_W), lambda b, i: (0, i, 0)),
                  pl.BlockSpec((3, seq, kw), lambda b, i: (0, 0, 0)),
                  _layer_spec((1, BRANCH_W), layer, 2),
                  _layer_spec((1, kw), layer, 2)],
        out_specs=pl.BlockSpec((tq, BRANCH_W), lambda b, i: (b * nq + i, 0)),
        out_shape=jax.ShapeDtypeStruct((batch * seq, BRANCH_W), BF16),
        scratch_shapes=[pltpu.VMEM((seq, kw), BF16), pltpu.VMEM((seq, kw), BF16)],
        compiler_params=_params(40, 2),
        name="attn_lat",
    )(proj, proj, proj, cache_kt, cache_vt, qtab, ktab, qn, kn)


def _na_bias_kernel(t_ref, o_ref):
    nblk = o_ref.shape[0]
    c = lax.broadcasted_iota(jnp.int32, (GRID_W, 2 * GRID_W), 0)
    j = lax.broadcasted_iota(jnp.int32, (GRID_W, 2 * GRID_W), 1)
    kc = j & (GRID_W - 1)
    dc = kc - c + (NA_COLS - 1)
    cs = jnp.clip(c - NA_COLS // 2, 0, GRID_W - NA_COLS)
    valid = jnp.logical_and(kc >= cs, kc < cs + NA_COLS)
    left = j < GRID_W

    def body(b, carry):
        b2 = jnp.minimum(b + 1, nblk - 1)
        acc = jnp.full((GRID_W, 2 * GRID_W), NEG_INF, F32)
        for i in range(N_DC):
            acc = jnp.where(dc == i, jnp.where(left, t_ref[b * N_DC + i], t_ref[b2 * N_DC + i]), acc)
        o_ref[b] = jnp.where(valid, acc * LOG2E, NEG_INF)
        return carry

    lax.fori_loop(0, nblk, body, 0)


def _na_bias_call(na_bias):
    nblk = DEPTH * N_HEADS * N_DR
    return pl.pallas_call(
        _na_bias_kernel,
        in_specs=[pl.BlockSpec(memory_space=pltpu.SMEM)],
        out_specs=pl.BlockSpec((nblk, GRID_W, 2 * GRID_W), lambda: (0, 0, 0)),
        out_shape=jax.ShapeDtypeStruct((nblk, GRID_W, 2 * GRID_W), F32),
        name="na_bias",
    )(na_bias.reshape(-1))


def _na_kernel(q_ref, k_ref, v_ref, z_ref, ckt_ref, cvt_ref, tb_ref, o_ref, kh_s, vh_s, *, rows):
    win = NA_ROWS * GRID_W

    @pl.when(pl.program_id(1) == 0)
    def _():
        for h in range(N_HEADS):
            kh_s[h] = k_ref[:, _hs(h)]
            vh_s[h] = v_ref[:, _hs(h)]

    qs, parts = [], []
    for i in range(NA_ROWS_PER_STEP):
        r = pl.program_id(1) * NA_ROWS_PER_STEP + i
        rs = jnp.clip(r - NA_ROWS // 2, 0, rows - NA_ROWS)
        r0 = pl.multiple_of(rs * GRID_W, GRID_W)
        q = q_ref[i * GRID_W:(i + 1) * GRID_W, :]
        dr0 = rs - r + NA_ROWS - 1
        for h in range(N_HEADS):
            bias = jnp.concatenate([tb_ref[h * N_DR + dr0 + 2 * p] for p in range(NA_ROWS // 2)], axis=1)
            qs.append(q[:, _hs(h)])
            parts.append([(kh_s[h, pl.ds(r0, win), :], vh_s[h, pl.ds(r0, win), :], bias, False),
                          (ckt_ref[h], cvt_ref[h], None, True)])
    outs = _attend(qs, parts)
    o = jnp.concatenate([jnp.concatenate(outs[i * N_HEADS:(i + 1) * N_HEADS], axis=-1)
                         for i in range(NA_ROWS_PER_STEP)], axis=0)
    o_ref[...] = (o * _silu(z_ref[...].astype(F32))).astype(BF16)


def _na_call(proj, cache_kt, cache_vt, tb, layer, batch, seq):
    rows = seq // GRID_W
    assert rows >= NA_ROWS and rows % NA_ROWS_PER_STEP == 0
    steps = rows // NA_ROWS_PER_STEP
    tq = NA_ROWS_PER_STEP * GRID_W
    past = cache_kt.shape[-1]
    nblk = N_HEADS * N_DR
    cq = OFF_D // BRANCH_W
    ctx_spec = pl.BlockSpec((None, None, N_HEADS, HEAD_DIM, past), lambda b, r: (b, layer, 0, 0, 0))
    return pl.pallas_call(
        functools.partial(_na_kernel, rows=rows),
        grid=(batch, steps),
        in_specs=[pl.BlockSpec((tq, BRANCH_W), lambda b, r: (b * steps + r, cq)),
                  pl.BlockSpec((seq, BRANCH_W), lambda b, r: (b, cq + 1)),
                  pl.BlockSpec((seq, BRANCH_W), lambda b, r: (b, cq + 2)),
                  pl.BlockSpec((tq, BRANCH_W), lambda b, r: (b * steps + r, cq + 3)),
                  ctx_spec, ctx_spec,
                  pl.BlockSpec((nblk, GRID_W, 2 * GRID_W), lambda b, r: (layer, 0, 0))],
        out_specs=pl.BlockSpec((tq, BRANCH_W), lambda b, r: (b * steps + r, 0)),
        out_shape=jax.ShapeDtypeStruct((batch * seq, BRANCH_W), BF16),
        scratch_shapes=[pltpu.VMEM((N_HEADS, seq, HEAD_DIM), BF16), pltpu.VMEM((N_HEADS, seq, HEAD_DIM), BF16)],
        compiler_params=_params(32, 2),
        name="na_lat",
    )(proj, proj, proj, proj, cache_kt, cache_vt, tb)


def _gdn_kernel(*refs, seq, has_s0, layer, emit):
    qkv_ref, z_ref, ab_ref, cw_ref, par_ref, g_ref = refs[:6]
    s0_ref = refs[6] if has_s0 else None
    (q_s, k_s, v_s, gcb_s, bcb_s, r_s, mc_s, nc_s, qp_s, op_s, egl_s, s_s, oacc_ref) = refs[-13:]
    if emit == "none":
        o_ref, st_ref = refs[-14], None
    else:
        o_ref, st_ref = refs[-15], refs[-14]
    n_chunks = seq // CHUNK
    n_levels = CHUNK.bit_length() - 1
    qkv_w = 3 * BRANCH_W
    half = SHORT_CONV // 2
    pair_w = 2 * HEAD_DIM
    pairs = [slice(p * pair_w, (p + 1) * pair_w) for p in range(BRANCH_W // pair_w)]
    tr = 256
    cpt = tr // CHUNK
    head_sum = _head_block_matrix(BRANCH_W, 1.0)

    gc_i = lax.broadcasted_iota(jnp.int32, (128, BRANCH_W), 0)
    gh_j = lax.broadcasted_iota(jnp.int32, (128, BRANCH_W), 1) >> 6
    sel_beta = [jnp.where(gc_i == gh_j + 4 * d, 1.0, 0.0).astype(BF16) for d in range(2)]
    sel_gate = [jnp.where(gc_i == gh_j + 8 + 4 * d, 1.0, 0.0).astype(BF16) for d in range(2)]
    ti = lax.broadcasted_iota(jnp.int32, (tr, tr), 0)
    tj = lax.broadcasted_iota(jnp.int32, (tr, tr), 1)
    same_chunk = (ti >> 6) == (tj >> 6)
    tri = [jnp.where(jnp.logical_and(same_chunk, ti >= tj), 1.0, 0.0).astype(BF16),
           jnp.where(jnp.logical_and(same_chunk, ti <= tj), 1.0, 0.0).astype(BF16)]
    assert tr == BRANCH_W
    lane_head = lax.broadcasted_iota(jnp.int32, (1, BRANCH_W), 1) >> 6

    halo = 16
    edge = 8
    assert half <= edge
    si = lax.broadcasted_iota(jnp.int32, (tr, tr), 0)
    sj = lax.broadcasted_iota(jnp.int32, (tr, tr), 1)
    ei = lax.broadcasted_iota(jnp.int32, (edge, halo), 0)
    ej = lax.broadcasted_iota(jnp.int32, (edge, halo), 1)
    taps = [j for j in range(SHORT_CONV) if j != half]
    shift = {j: jnp.where(sj == si + (j - half), 1.0, 0.0).astype(BF16) for j in taps}
    shift_before = {j: jnp.where(ej == ei + (halo + j - half), 1.0, 0.0).astype(BF16) for j in taps if j < half}
    shift_after = {j: jnp.where(ej == ei + (j - half - edge), 1.0, 0.0).astype(BF16) for j in taps if j > half}
    for t in range(seq // tr):
        rows = slice(t * tr, (t + 1) * tr)
        x = qkv_ref[rows, :]
        y = x.astype(F32) * cw_ref[half:half + 1, :]
        for j in taps:
            y = y + jnp.dot(shift[j], x, preferred_element_type=F32) * cw_ref[j:j + 1, :]
        if t > 0:
            before = qkv_ref[t * tr - halo:t * tr, :]
            top = sum(jnp.dot(shift_before[j], before, preferred_element_type=F32) * cw_ref[j:j + 1, :]
                      for j in shift_before)
            y = jnp.concatenate([y[:edge] + top, y[edge:]], axis=0)
        if (t + 1) * tr < seq:
            after = qkv_ref[(t + 1) * tr:(t + 1) * tr + halo, :]
            bottom = sum(jnp.dot(shift_after[j], after, preferred_element_type=F32) * cw_ref[j:j + 1, :]
                         for j in shift_after)
            y = jnp.concatenate([y[:tr - edge], y[tr - edge:] + bottom], axis=0)
        y = _silu(y)
        qq, kk = y[:, :BRANCH_W], y[:, BRANCH_W:2 * BRANCH_W]
        q_s[rows, :] = qq * lax.rsqrt(_head_reduce(qq * qq, head_sum) + EPS) * SCALE
        k_s[rows, :] = kk * lax.rsqrt(_head_reduce(kk * kk, head_sum) + EPS)
        v_s[rows, :] = y[:, 2 * BRANCH_W:]
        x = ab_ref[rows, :]
        beta = jax.nn.sigmoid(x)
        xs = x + par_ref[0:1, :]
        softplus = jnp.maximum(xs, 0.0) + jnp.log1p(jnp.exp(-jnp.abs(xs)))
        la = -jnp.exp(par_ref[1:2, :]) * softplus
        for d in range(2):
            gc = _mm_exact(tri[d], la)
            gcb_s[d, rows, :] = _mm_exact_lhs(gc, sel_gate[d])
            bcb_s[d, rows, :] = _mm_exact_lhs(beta, sel_beta[d], terms=2)
            gt = gc.T[8:16, :]
            shifted = {s: (gt if s == 0 else pltpu.roll(gt, (s * HEAD_DIM) % tr, 1))
                       for s in range(1 - cpt, N_HEADS)}
            for c in range(cpt):
                r = jnp.zeros((1, BRANCH_W), F32)
                for h in range(N_HEADS):
                    r = jnp.where(lane_head == h, shifted[h - c][4 * d + h:4 * d + h + 1, :], r)
                r_s[d, (t * cpt + c) * 8:(t * cpt + c + 1) * 8, :] = jnp.broadcast_to(r, (8, BRANCH_W))

    for d in range(2):
        if has_s0:
            s_s[d] = jnp.concatenate([s0_ref[d, h] for h in range(N_HEADS)], axis=-1)
        else:
            s_s[d] = jnp.zeros((HEAD_DIM, BRANCH_W), F32)

    li = lax.broadcasted_iota(jnp.int32, (CHUNK, BRANCH_W), 0)
    lj = lax.broadcasted_iota(jnp.int32, (CHUNK, BRANCH_W), 1) & (HEAD_DIM - 1)
    incl = (li >= lj, li <= lj)
    strict = (li > lj, li < lj)
    level = [((li ^ lj) >> l) == 1 for l in range(n_levels)]
    first_head = lax.broadcasted_iota(jnp.int32, (CHUNK, pair_w), 1) < HEAD_DIM

    def expand(y):
        yb = y.astype(BF16)
        zero = jnp.zeros((CHUNK, pair_w), BF16)
        return [jnp.concatenate([jnp.where(first_head, yb[:, p], zero), jnp.where(first_head, zero, yb[:, p])],
                                axis=0) for p in pairs]

    def bdmm(x, ybd):
        xb = x.astype(BF16)
        return jnp.concatenate([jnp.dot(xb[:, p], ybd[i], preferred_element_type=F32)
                                for i, p in enumerate(pairs)], axis=1)

    def bdmm_nt(x, ybd):
        xb = x.astype(BF16)
        return jnp.concatenate([lax.dot_general(xb[:, p], ybd[i], (((1,), (1,)), ((), ())),
                                                preferred_element_type=F32)
                                for i, p in enumerate(pairs)], axis=1)

    def tn_diag(a, b):
        ab, bb = a.astype(BF16), b.astype(BF16)
        outs = []
        for p in pairs:
            full = lax.dot_general(ab[:, p], bb[:, p], (((0,), (0,)), ((), ())), preferred_element_type=F32)
            outs.append(jnp.where(first_head, full[:HEAD_DIM], full[HEAD_DIM:]))
        return jnp.concatenate(outs, axis=1)

    def prepare(chains):
        n = range(len(chains))
        dd = [d for d, _ in chains]
        rows = [pl.ds(_aligned(c * CHUNK, CHUNK), CHUNK) for _, c in chains]
        gcb = [gcb_s[dd[i], rows[i], :] for i in n]
        bcb = [bcb_s[dd[i], rows[i], :] for i in n]
        grow = [r_s[dd[i], pl.ds(_aligned(chains[i][1] * 8, 8), 8), :][0:1, :] for i in n]
        dm = [jnp.exp(jnp.where(incl[dd[i]], gcb[i] - grow[i], NEG_INF)) for i in n]
        k = [k_s[rows[i], :] for i in n]
        q = [q_s[rows[i], :] for i in n]
        v = [v_s[rows[i], :] for i in n]
        kq = [bdmm_nt(jnp.concatenate([k[i], q[i]], axis=0), expand(k[i])) for i in n]
        a = [jnp.where(strict[dd[i]], bcb[i] * kq[i][:CHUNK] * dm[i], 0.0) for i in n]
        tm = [-jnp.where(level[0], a[i], 0.0) for i in n]
        for l in range(1, n_levels):
            b = [jnp.where(level[l], a[i], 0.0) for i in n]
            y = [b[i] + bdmm(tm[i], expand(b[i])) for i in n]
            tm = [tm[i] - (y[i] + bdmm(y[i], expand(tm[i]))) for i in n]
        eg = [jnp.exp(gcb[i]) for i in n]
        bv = [bcb[i] * v[i] for i in n]
        bk = [bcb[i] * k[i] * eg[i] for i in n]
        u = [bv[i] + bdmm(tm[i], expand(bv[i])) for i in n]
        w = [bk[i] + bdmm(tm[i], expand(bk[i])) for i in n]
        gl = [gcb[i][CHUNK - 1:CHUNK, :] if dd[i] == 0 else gcb[i][0:1, :] for i in n]
        kd = [k[i] * jnp.exp(gl[i] - gcb[i]) for i in n]
        qkm = [kq[i][CHUNK:] * dm[i] for i in n]
        mc = [tn_diag(kd[i], w[i]) for i in n]
        nc = [tn_diag(kd[i], u[i]) for i in n]
        qp = [q[i] * eg[i] - bdmm(qkm[i], expand(w[i])) for i in n]
        op = [bdmm(qkm[i], expand(u[i])) for i in n]
        for i in n:
            d, c = chains[i]
            mc_s[d, rows[i], :] = mc[i].astype(BF16)
            nc_s[d, rows[i], :] = nc[i]
            qp_s[d, rows[i], :] = qp[i].astype(BF16)
            op_s[d, rows[i], :] = op[i]
            egl_s[d, pl.ds(_aligned(c * 8, 8), 8), :] = jnp.broadcast_to(jnp.exp(gl[i]), (8, BRANCH_W))

    group = min(PREP_CHUNKS, n_chunks)
    if n_chunks == group:
        prepare([(d, c) for c in range(group) for d in range(2)])
    else:
        def prep_body(j, carry):
            prepare([(d, j * group + c) for c in range(group) for d in range(2)])
            return carry
        lax.fori_loop(0, n_chunks // group, prep_body, 0)

    def scan_step(i):
        for d, c in ((0, i), (1, n_chunks - 1 - i)):
            rows = pl.ds(_aligned(c * CHUNK, CHUNK), CHUNK)
            s = s_s[d]
            sbd = expand(s)
            oacc_ref[d, rows, :] = bdmm(qp_s[d, rows, :], sbd) + op_s[d, rows, :]
            egl = egl_s[d, pl.ds(_aligned(c * 8, 8), 8), :][0:1, :]
            s_s[d] = s * egl - bdmm(mc_s[d, rows, :], sbd) + nc_s[d, rows, :]

    unroll = min(SCAN_UNROLL, n_chunks)

    def scan_body(j, carry):
        for i in range(unroll):
            scan_step(j * unroll + i)
        return carry

    lax.fori_loop(0, n_chunks // unroll, scan_body, 0)

    if st_ref is not None:
        _write_state(st_ref, layer, emit == "first", lambda d, h: s_s[d][:, _hs(h)])
    o = oacc_ref[0] + oacc_ref[1]
    ms = _head_reduce(o * o, _head_block_matrix(BRANCH_W, 1.0 / HEAD_DIM))
    o_ref[...] = (o * lax.rsqrt(ms + EPS) * g_ref[...] * _silu(z_ref[...].astype(F32))).astype(BF16)


def _state_spec(layer, n_seq=None):
    return pl.BlockSpec((n_seq, None, 2, N_HEADS, HEAD_DIM, HEAD_DIM), lambda b: (b, layer, 0, 0, 0, 0))


def _state_output(emit, prev, layer, batch, n_args, n_seq=None):
    if emit == "none":
        return [], [], [], [], {}
    shape = jax.ShapeDtypeStruct((batch, DEPTH, 2, N_HEADS, HEAD_DIM, HEAD_DIM), F32)
    if emit == "first":
        spec = pl.BlockSpec((n_seq, DEPTH, 2, N_HEADS, HEAD_DIM, HEAD_DIM), lambda b: (b, 0, 0, 0, 0, 0))
        return [], [], [spec], [shape], {}
    return [pl.BlockSpec(memory_space=pl.ANY)], [prev], [_state_spec(layer, n_seq)], [shape], {n_args: 1}


def _gdn_call(proj, gab, cw8, par, norm_g, layer, batch, seq, s0=None, emit="none", prev=None):
    has_s0 = s0 is not None
    qkv_w = 3 * BRANCH_W
    in_specs = [pl.BlockSpec((seq, qkv_w), lambda b: (b, OFF_GQKV // qkv_w)),
                pl.BlockSpec((seq, BRANCH_W), lambda b: (b, OFF_GZ // BRANCH_W)),
                pl.BlockSpec((seq, 128), lambda b: (b, 0)),
                _layer_spec((8, qkv_w), layer, 1),
                _layer_spec((2, 128), layer, 1),
                _layer_spec((1, BRANCH_W), layer, 1)]
    args = [proj, proj, gab, cw8, par, norm_g]
    if has_s0:
        in_specs.append(_state_spec(layer))
        args.append(s0)
    st_in_specs, st_args, st_out_specs, st_shapes, aliases = _state_output(emit, prev, layer, batch, len(args))
    return pl.pallas_call(
        functools.partial(_gdn_kernel, seq=seq, has_s0=has_s0, layer=layer, emit=emit),
        grid=(batch,), in_specs=in_specs + st_in_specs,
        out_specs=[pl.BlockSpec((seq, BRANCH_W), lambda b: (b, 0))] + st_out_specs,
        out_shape=[jax.ShapeDtypeStruct((batch * seq, BRANCH_W), BF16)] + st_shapes,
        input_output_aliases=aliases,
        scratch_shapes=[pltpu.VMEM((seq, BRANCH_W), F32),
                        pltpu.VMEM((seq, BRANCH_W), F32),
                        pltpu.VMEM((seq, BRANCH_W), F32),
                        pltpu.VMEM((2, seq, BRANCH_W), F32),
                        pltpu.VMEM((2, seq, BRANCH_W), F32),
                        pltpu.VMEM((2, seq // CHUNK * 8, BRANCH_W), F32),
                        pltpu.VMEM((2, seq, BRANCH_W), BF16),
                        pltpu.VMEM((2, seq, BRANCH_W), F32),
                        pltpu.VMEM((2, seq, BRANCH_W), BF16),
                        pltpu.VMEM((2, seq, BRANCH_W), F32),
                        pltpu.VMEM((2, seq // CHUNK * 8, BRANCH_W), F32),
                        pltpu.VMEM((2, HEAD_DIM, BRANCH_W), F32),
                        pltpu.VMEM((2, seq, BRANCH_W), F32)],
        compiler_params=_params(48, 1),
        name="gdn",
    )(*args, *st_args)


_RET_LOG_GAMMA = [[float(np.log1p(-np.exp2(-(base + h)))) for h in range(N_HEADS)] for base in RET_DECAY_BASE]


def _ret_kernel(*refs, seq, n_seq, has_s0, layer, emit):
    qkv_ref, z_ref, g_ref = refs[:3]
    s0_ref = refs[3] if has_s0 else None
    o_ref, st_ref = (refs[-1], None) if emit == "none" else (refs[-2], refs[-1])
    tile = RET_TILE
    n_tiles = seq // tile
    problems = [(s, h) for s in range(n_seq) for h in range(N_HEADS)]
    heads = range(len(problems))
    lgf = [_RET_LOG_GAMMA[0][h] for _, h in problems]
    lgb = [_RET_LOG_GAMMA[1][h] for _, h in problems]
    a = lax.broadcasted_iota(jnp.int32, (tile, 1), 0).astype(F32)
    ef = [jnp.exp(a * lgf[h]) for h in heads]
    eif = [jnp.exp(-a * lgf[h]) for h in heads]
    eb = [jnp.exp(a * lgb[h]) for h in heads]
    eib = [jnp.exp(-a * lgb[h]) for h in heads]
    gf_tile = [float(np.exp(tile * lgf[h])) for h in heads]
    gb_tile = [float(np.exp(tile * lgb[h])) for h in heads]
    ii = lax.broadcasted_iota(jnp.int32, (tile, tile), 0)
    jj = lax.broadcasted_iota(jnp.int32, (tile, tile), 1)

    def head_cols(t, part, i):
        s, h = problems[i]
        return qkv_ref[s * seq + t * tile:s * seq + (t + 1) * tile,
                       part * BRANCH_W + h * HEAD_DIM:part * BRANCH_W + (h + 1) * HEAD_DIM]

    def initial_state(d, i):
        s, h = problems[i]
        return s0_ref[s, d, h]

    kf = [[head_cols(t, 1, h) * eif[h] for h in heads] for t in range(n_tiles)]
    kb = [[head_cols(t, 1, h) * eb[h] for h in heads] for t in range(n_tiles)]
    vs = [[head_cols(t, 2, h) for h in heads] for t in range(n_tiles)]
    use_states = has_s0 or n_tiles > 1 or st_ref is not None
    if use_states:
        kvf = [[_mm_tn(kf[t][h], vs[t][h]) for h in heads] for t in range(n_tiles)]
        kvb = [[_mm_tn(kb[t][h], vs[t][h]) for h in heads] for t in range(n_tiles)]
        zero = jnp.zeros((HEAD_DIM, HEAD_DIM), F32)
        zf = [[(float(np.exp(lgf[h])) * initial_state(0, h)) if has_s0 else zero for h in heads]]
        for t in range(n_tiles):
            zf.append([gf_tile[h] * (zf[t][h] + kvf[t][h]) for h in heads])
        acc = [initial_state(1, h) if has_s0 else zero for h in heads]
        zb = [None] * n_tiles
        for t in reversed(range(n_tiles)):
            zb[t] = [gb_tile[h] * acc[h] for h in heads]
            acc = [zb[t][h] + kvb[t][h] for h in heads]
        if st_ref is not None:
            stf = [zf[n_tiles][h] * float(np.exp(-lgf[h])) for h in heads]
            for s in range(n_seq):
                _write_state(st_ref.at[s], layer, emit == "first",
                             lambda d, h, s=s: (stf, acc)[d][s * N_HEADS + h])
    tiles = [[] for _ in range(n_seq)]
    for t in range(n_tiles):
        q = [head_cols(t, 0, h) * SCALE for h in heads]
        qf = [q[h] * ef[h] for h in heads]
        qb = [q[h] * eib[h] for h in heads]
        sd = [jnp.where(ii >= jj, _mm_nt(qf[h], kf[t][h]), 0.0) + jnp.where(ii <= jj, _mm_nt(qb[h], kb[t][h]), 0.0)
              for h in heads]
        o = [_mm(sd[h], vs[t][h]) for h in heads]
        if has_s0 or n_tiles > 1:
            o = [o[h] + _mm(qf[h], zf[t][h]) + _mm(qb[h], zb[t][h]) for h in heads]
        ms = [jnp.mean(o[h] * o[h], axis=-1, keepdims=True) for h in heads]
        for s in range(n_seq):
            tiles[s].append(jnp.concatenate([o[h] * lax.rsqrt(ms[h] + EPS) * g_ref[...]
                                             for h in range(s * N_HEADS, (s + 1) * N_HEADS)], axis=-1))
    o = jnp.concatenate([tile_out for s in range(n_seq) for tile_out in tiles[s]], axis=0)
    o_ref[...] = (o * _silu(z_ref[...].astype(F32))).astype(BF16)


def _ret_call(proj, norm_g, layer, batch, seq, s0=None, emit="none", prev=None):
    has_s0 = s0 is not None
    qkv_w = 3 * BRANCH_W
    n_seq = max(1, RET_ROWS_PER_STEP // seq)
    assert batch % n_seq == 0
    tm = n_seq * seq
    in_specs = [pl.BlockSpec((tm, qkv_w), lambda b: (b, OFF_C // qkv_w)),
                pl.BlockSpec((tm, BRANCH_W), lambda b: (b, (OFF_C + qkv_w) // BRANCH_W)),
                _layer_spec((1, HEAD_DIM), layer, 1)]
    args = [proj, proj, norm_g]
    if has_s0:
        in_specs.append(_state_spec(layer, n_seq))
        args.append(s0)
    st_in_specs, st_args, st_out_specs, st_shapes, aliases = _state_output(emit, prev, layer, batch, len(args),
                                                                           n_seq)
    return pl.pallas_call(
        functools.partial(_ret_kernel, seq=seq, n_seq=n_seq, has_s0=has_s0, layer=layer, emit=emit),
        grid=(batch // n_seq,), in_specs=in_specs + st_in_specs,
        out_specs=[pl.BlockSpec((tm, BRANCH_W), lambda b: (b, 0))] + st_out_specs,
        out_shape=[jax.ShapeDtypeStruct((batch * seq, BRANCH_W), BF16)] + st_shapes,
        input_output_aliases=aliases,
        compiler_params=_params(48, 1),
        name="retention",
    )(*args, *st_args)


def _out_kernel(*refs, final):
    if final:
        (h_ref, mod_ref, g_ref, oa_ref, ob_ref, oc_ref, od_ref, wg_ref, wb_ref, wo_ref, fn_ref,
         o_ref, y_ref) = refs
    else:
        h_ref, mod_ref, g_ref, oa_ref, ob_ref, oc_ref, od_ref, wg_ref, wb_ref, wo_ref, o_ref = refs
    x = h_ref[...]
    mod = mod_ref[0]
    hn = _modulated_norm(x, mod, g_ref[...]).astype(BF16)
    merged = None
    for n, br_ref in enumerate((oa_ref, ob_ref, oc_ref, od_ref)):
        gate = jax.nn.sigmoid(jnp.dot(hn, wg_ref[:, n * D_MODEL:(n + 1) * D_MODEL], preferred_element_type=F32))
        up = jnp.dot(br_ref[...], wb_ref[n], preferred_element_type=F32)
        merged = gate * up if merged is None else merged + gate * up
    out = jnp.dot(merged.astype(BF16), wo_ref[...], preferred_element_type=F32)
    hnew = x + mod[:, 2 * D_MODEL:] * out
    o_ref[...] = hnew
    if final:
        ms = jnp.mean(hnew * hnew, axis=-1, keepdims=True)
        y_ref[...] = hnew * lax.rsqrt(ms + EPS) * fn_ref[...]


def _out_call(h2d, mod3, norm_g3, branches, wg, wb, wo, layer, rows_per_mod, final_norm=None):
    t = h2d.shape[0]
    tm = 512
    final = final_norm is not None
    if mod3.shape[0] == 1:
        mod_idx = lambda i: (0, 0, 0)
    else:
        mod_idx = lambda i: ((i * tm) // rows_per_mod, 0, 0)
    once = pl.Buffered(1)
    in_specs = [pl.BlockSpec((tm, D_MODEL), lambda i: (i, 0)),
                pl.BlockSpec((1, 1, 3 * D_MODEL), mod_idx),
                _layer_spec((1, D_MODEL), layer, 1)]
    in_specs += [pl.BlockSpec((tm, BRANCH_W), lambda i: (i, 0))] * N_BRANCH
    in_specs += [pl.BlockSpec((None, D_MODEL, N_BRANCH * D_MODEL), lambda i: (layer, 0, 0), pipeline_mode=once),
                 pl.BlockSpec((None, N_BRANCH, BRANCH_W, D_MODEL), lambda i: (layer, 0, 0, 0), pipeline_mode=once),
                 pl.BlockSpec((None, D_MODEL, D_MODEL), lambda i: (layer, 0, 0), pipeline_mode=once)]
    args = [h2d, mod3, norm_g3, *branches, wg, wb, wo]
    out_specs = [pl.BlockSpec((tm, D_MODEL), lambda i: (i, 0))]
    out_shape = [jax.ShapeDtypeStruct((t, D_MODEL), F32)]
    if final:
        in_specs.append(pl.BlockSpec((1, D_MODEL), lambda i: (0, 0)))
        args.append(final_norm.reshape(1, D_MODEL))
        out_specs.append(pl.BlockSpec((tm, D_MODEL), lambda i: (i, 0)))
        out_shape.append(jax.ShapeDtypeStruct((t, D_MODEL), F32))
    return pl.pallas_call(
        functools.partial(_out_kernel, final=final),
        grid=(t // tm,), in_specs=in_specs, out_specs=out_specs, out_shape=out_shape,
        compiler_params=_params(48, 1),
        name="merge_out_final" if final else "merge_out",
    )(*args)


def _prep_weights(w_in):
    offs = np.concatenate([[0], np.cumsum(IN_SPLITS)])
    seg = lambda i, j: w_in[:, :, offs[i]:offs[j]]
    pad = jnp.zeros((DEPTH, D_MODEL, PROJ_W - OFF_GAB - IN_SPLITS[5]), w_in.dtype)
    wcat = jnp.concatenate([seg(7, 11), seg(11, 15), seg(6, 7), seg(4, 5), seg(0, 3), seg(3, 4), seg(5, 6), pad],
                           axis=2).astype(BF16)
    return wcat, seg(15, 16).astype(BF16)


def _rope_tables(seq):
    t = jnp.arange(seq)
    quarter = HEAD_DIM // 4
    inv = ROPE_THETA ** (-jnp.arange(quarter, dtype=F32) / quarter)

    def half(pos):
        ang = pos.astype(F32)[:, None] * inv
        c, s, zero = jnp.cos(ang), jnp.sin(ang), jnp.zeros_like(ang)
        return jnp.concatenate([c, c], -1), jnp.concatenate([-s, zero], -1), jnp.concatenate([zero, s], -1)

    parts = [jnp.concatenate([a, b], -1) for a, b in zip(half(t // GRID_W), half(t % GRID_W))]
    tab = jnp.stack(parts)
    return jnp.tile(tab, (1, 1, N_HEADS)), jnp.tile(tab, (1, 1, KV_HEADS))


def _layer(h2d, batch, seq, mod, pw, layer, ctx, caches, final_norm):
    proj, gab = _inproj_call(h2d, mod, pw["norm_g"], pw["wcat"], layer, seq)
    kw = KV_HEADS * HEAD_DIM
    if ctx is None:
        emit = "first" if caches is None else "update"
        akv, nkv, sg_all, sr_all = caches or (None, None, None, None)
        oa, *akv = _attn_ctx_call(proj, akv, layer, batch, seq, OFF_AQKV, OFF_AQKV + BRANCH_W,
                                  OFF_AQKV + BRANCH_W + kw, OFF_AZ, KV_HEADS, pw["qn"], pw["kn"])
        od, *nkv = _attn_ctx_call(proj, nkv, layer, batch, seq, OFF_D, OFF_D + BRANCH_W,
                                  OFF_D + 2 * BRANCH_W, OFF_D + 3 * BRANCH_W, N_HEADS)
        ob, sg_all = _gdn_call(proj, gab, pw["cw8"], pw["gdn_par"], pw["gdn_norm"], layer, batch, seq,
                               emit=emit, prev=sg_all)
        oc, sr_all = _ret_call(proj, pw["ret_norm"], layer, batch, seq, emit=emit, prev=sr_all)
        caches = (akv, nkv, sg_all, sr_all)
    else:
        oa = _attn_lat_call(proj, ctx["akt"], ctx["avt"], layer, batch, seq, ctx["qtab"], ctx["ktab"],
                            pw["qn"], pw["kn"])
        od = _na_call(proj, ctx["nkt"], ctx["nvt"], ctx["tb"], layer, batch, seq)
        ob, = _gdn_call(proj, gab, pw["cw8"], pw["gdn_par"], pw["gdn_norm"], layer, batch, seq, s0=ctx["sg"])
        oc, = _ret_call(proj, pw["ret_norm"], layer, batch, seq, s0=ctx["sr"])
    outs = _out_call(h2d, mod, pw["norm_g"], (oa, ob, oc, od), pw["wg"], pw["wb"], pw["wo"], layer, seq, final_norm)
    return outs, caches


def kernel(x_prompt, x_sample, cache_attn_k, cache_attn_v, cache_na_k, cache_na_v, state_gdn, state_ret, c, c_ctx, w_ada, b_ada, norm_g, w_in, conv_w, gdn_a_log, gdn_dt_bias, gdn_norm, attn_q_norm, attn_k_norm, ret_norm, na_bias, w_branch, w_out, final_norm):
    batch, seq, _ = x_prompt.shape
    dbatch, dseq, _ = x_sample.shape
    assert dbatch == 8, "the modulation kernel handles exactly one sublane tile of conditioning rows"

    wcat, wg = _prep_weights(w_in)
    par = jnp.zeros((DEPTH, 2, 128), F32)
    par = par.at[:, 0, 8:16].set(gdn_dt_bias.reshape(DEPTH, 8)).at[:, 1, 8:16].set(gdn_a_log.reshape(DEPTH, 8))
    pw = dict(
        w_ada=w_ada, b_ada=b_ada.reshape(DEPTH, 1, 3 * D_MODEL), norm_g=norm_g.reshape(DEPTH, 1, D_MODEL),
        wcat=wcat, wg=wg, wb=w_branch.astype(BF16), wo=w_out.astype(BF16),
        cw8=jnp.concatenate([conv_w, jnp.zeros((DEPTH, 8 - SHORT_CONV, 3 * BRANCH_W), F32)], axis=1),
        gdn_par=par,
        gdn_norm=jnp.tile(gdn_norm, (1, N_HEADS)).reshape(DEPTH, 1, BRANCH_W),
        ret_norm=ret_norm.reshape(DEPTH, 1, HEAD_DIM),
        qn=jnp.tile(attn_q_norm, (1, N_HEADS)).reshape(DEPTH, 1, BRANCH_W),
        kn=jnp.tile(attn_k_norm, (1, KV_HEADS)).reshape(DEPTH, 1, KV_HEADS * HEAD_DIM))

    cond = jnp.concatenate([jnp.broadcast_to(c_ctx, (8, D_MODEL)), c], axis=0)
    mods = _mod_call(cond, w_ada, pw["b_ada"])

    h = x_prompt.reshape(batch * seq, D_MODEL)
    caches = None
    for l in range(DEPTH):
        outs, caches = _layer(h, batch, seq, mods[l, 0:1].reshape(1, 1, 3 * D_MODEL), pw, l, None, caches,
                              final_norm if l == DEPTH - 1 else None)
        h = outs[0]
    y_prompt = outs[1].reshape(batch, seq, D_MODEL)
    token_major = lambda a: a.transpose(0, 1, 4, 2, 3)
    (akt, avt), (nkt, nvt), new_state_gdn, new_state_ret = caches
    new_attn_k, new_attn_v, new_na_k, new_na_v = (token_major(a) for a in (akt, avt, nkt, nvt))

    qtab, ktab = _rope_tables(dseq)
    feature_major = lambda a: a.transpose(0, 1, 3, 4, 2)
    ctx = dict(akt=feature_major(cache_attn_k), avt=feature_major(cache_attn_v),
               nkt=feature_major(cache_na_k), nvt=feature_major(cache_na_v),
               sg=state_gdn, sr=state_ret, tb=_na_bias_call(na_bias), qtab=qtab, ktab=ktab)
    h = x_sample.reshape(dbatch * dseq, D_MODEL)
    for l in range(DEPTH):
        outs, _ = _layer(h, dbatch, dseq, mods[l, 8:16].reshape(dbatch, 1, 3 * D_MODEL), pw, l, ctx, None,
                         final_norm if l == DEPTH - 1 else None)
        h = outs[0]
    y_sample = outs[1].reshape(dbatch, dseq, D_MODEL)
    return (y_prompt, y_sample, new_attn_k, new_attn_v, new_na_k, new_na_v, new_state_gdn, new_state_ret)
```

```python
import functools

import numpy as np
import jax
import jax.numpy as jnp
from jax import lax
from jax.experimental import pallas as pl
from jax.experimental.pallas import tpu as pltpu

F32 = jnp.float32
BF16 = jnp.bfloat16

D_MODEL = 1024
HEAD_DIM = 64
N_HEADS = 4
KV_HEADS = N_HEADS // 2
BRANCH_W = N_HEADS * HEAD_DIM
N_BRANCH = 4
DEPTH = 2
GRID_W = 64
CHUNK = 64
PREP_CHUNKS = 4
SCAN_UNROLL = 4
assert CHUNK == HEAD_DIM
SHORT_CONV = 5
NA_ROWS = 8
NA_COLS = 16
RET_ROWS_PER_STEP = 1024
CTX_SEQS_PER_STEP = 4
NA_ROWS_PER_STEP = 4
N_DR = 2 * NA_ROWS - 1
N_DC = 2 * NA_COLS - 1
ROPE_THETA = 10000.0
RET_DECAY_BASE = (5.0, 5.5)
RET_TILE = 256
EPS = 1e-6
SCALE = HEAD_DIM ** -0.5
LOG2E = 1.4426950408889634
NEG_INF = float("-inf")

IN_SPLITS = (256, 128, 128, 256, 768, 16, 256, 256, 256, 256, 256, 256, 256, 256, 256, 4096)
PROJ_W = 4096
OFF_C = 0
OFF_D = 1024
OFF_GZ = 2048
OFF_GQKV = 2304
OFF_AQKV = 3072
OFF_AZ = 3584
OFF_GAB = 3840

V7X_VMEM_BYTES = 64 * 1024 * 1024
MIB = 1024 * 1024


def _params(vmem_mib, n_axes):
    assert vmem_mib * MIB < V7X_VMEM_BYTES
    return pltpu.CompilerParams(dimension_semantics=("arbitrary",) * n_axes,
                                vmem_limit_bytes=vmem_mib * MIB)


def _in_hbm(*arrays):
    return [pltpu.with_memory_space_constraint(a, pltpu.HBM) for a in arrays]


def _layer_spec(block, layer, n_grid):
    zeros = (0,) * len(block)
    if n_grid == 1:
        return pl.BlockSpec((None,) + block, lambda i: (layer,) + zeros)
    return pl.BlockSpec((None,) + block, lambda i, j: (layer,) + zeros)


def _mm(a, b):
    return jnp.dot(a.astype(BF16), b.astype(BF16), preferred_element_type=F32)


def _mm_nt(a, b):
    return lax.dot_general(a.astype(BF16), b.astype(BF16), (((1,), (1,)), ((), ())),
                           preferred_element_type=F32)


def _mm_tn(a, b):
    return lax.dot_general(a.astype(BF16), b.astype(BF16), (((0,), (0,)), ((), ())),
                           preferred_element_type=F32)


def _split3(x):
    hi = x.astype(BF16)
    r = x - hi.astype(F32)
    mid = r.astype(BF16)
    lo = (r - mid.astype(F32)).astype(BF16)
    return hi, mid, lo


def _mm_exact(sel, x):
    hi, mid, lo = _split3(x)
    return (jnp.dot(sel, hi, preferred_element_type=F32) + jnp.dot(sel, mid, preferred_element_type=F32)
            + jnp.dot(sel, lo, preferred_element_type=F32))


def _mm_exact_lhs(x, sel, terms=3):
    return sum(jnp.dot(part, sel, preferred_element_type=F32) for part in _split3(x)[:terms])


def _silu(x):
    return x * jax.nn.sigmoid(x)


def _head_block_matrix(width, value):
    ri = lax.broadcasted_iota(jnp.int32, (width, width), 0) >> 6
    ci = lax.broadcasted_iota(jnp.int32, (width, width), 1) >> 6
    return jnp.where(ri == ci, value, 0.0).astype(BF16)


def _head_reduce(x, g):
    hi = x.astype(BF16)
    lo = (x - hi.astype(F32)).astype(BF16)
    return jnp.dot(hi, g, preferred_element_type=F32) + jnp.dot(lo, g, preferred_element_type=F32)


def _head_rms(x):
    ms = _head_reduce(x * x, _head_block_matrix(x.shape[1], 1.0 / HEAD_DIM))
    return x * lax.rsqrt(ms + EPS)


def _rope(x, tab_ref):
    w = x.shape[1]
    return (x * tab_ref[0] + pltpu.roll(x, w - 16, 1) * tab_ref[1] + pltpu.roll(x, 16, 1) * tab_ref[2])


def _attend(qs, parts):
    groups = range(len(qs))
    qs = [(q.astype(F32) * (SCALE * LOG2E)).astype(BF16) for q in qs]

    def score(q, part):
        k, _, bias, feature_major = part
        s = _mm(q, k) if feature_major else _mm_nt(q, k)
        return s if bias is None else s + bias

    scores = [[score(qs[g], part) for part in parts[g]] for g in groups]
    m = [functools.reduce(jnp.maximum, [s.max(axis=-1, keepdims=True) for s in scores[g]]) for g in groups]
    p = [[jnp.exp2(s - m[g]) for s in scores[g]] for g in groups]
    den = [sum(x.sum(axis=-1, keepdims=True) for x in p[g]) for g in groups]
    out = [sum(_mm_nt(x, part[1]) if part[3] else _mm(x, part[1]) for x, part in zip(p[g], parts[g]))
           for g in groups]
    return [out[g] / den[g] for g in groups]


def _hs(h):
    return slice(h * HEAD_DIM, (h + 1) * HEAD_DIM)


def _aligned(x, m):
    return x if isinstance(x, int) else pl.multiple_of(x, m)


def _mod_kernel(c_ref, w_ref, b_ref, o_ref):
    o_ref[...] = _mm(_silu(c_ref[...]), w_ref[...]) + b_ref[...]


def _mod_call(cond, w_ada, b_ada3):
    tn = 512
    rows = cond.shape[0]
    return pl.pallas_call(
        _mod_kernel,
        grid=(DEPTH, 3 * D_MODEL // tn),
        in_specs=[pl.BlockSpec((rows, D_MODEL), lambda l, j: (0, 0)),
                  pl.BlockSpec((None, D_MODEL, tn), lambda l, j: (l, 0, j)),
                  pl.BlockSpec((None, 1, tn), lambda l, j: (l, 0, j))],
        out_specs=pl.BlockSpec((None, rows, tn), lambda l, j: (l, 0, j)),
        out_shape=jax.ShapeDtypeStruct((DEPTH, rows, 3 * D_MODEL), F32),
        compiler_params=_params(24, 2),
        name="adaln_mod",
    )(*_in_hbm(cond, w_ada, b_ada3))


def _modulated_norm(x, mod, g):
    ms = jnp.mean(x * x, axis=-1, keepdims=True)
    y = x * lax.rsqrt(ms + EPS) * g
    return y * (1.0 + mod[:, D_MODEL:2 * D_MODEL]) + mod[:, :D_MODEL]


def _inproj_kernel(x_ref, mod_ref, g_ref, w_ref, o_ref, ab_ref):
    hn = _modulated_norm(x_ref[...], mod_ref[0], g_ref[...]).astype(BF16)
    tn = 512
    for j in range(PROJ_W // tn):
        y = jnp.dot(hn, w_ref[:, j * tn:(j + 1) * tn], preferred_element_type=F32)
        o_ref[:, j * tn:(j + 1) * tn] = y.astype(BF16)
        if j == OFF_GAB // tn:
            ab_ref[...] = y[:, OFF_GAB % tn:OFF_GAB % tn + 128]


def _inproj_call(x2d, mod3, norm_g3, wcat, layer, rows_per_mod):
    t = x2d.shape[0]
    tm = 512
    if mod3.shape[0] == 1:
        mod_idx = lambda i: (0, 0, 0)
    else:
        mod_idx = lambda i: ((i * tm) // rows_per_mod, 0, 0)
    return pl.pallas_call(
        _inproj_kernel,
        grid=(t // tm,),
        in_specs=[pl.BlockSpec((tm, D_MODEL), lambda i: (i, 0)),
                  pl.BlockSpec((1, 1, 3 * D_MODEL), mod_idx),
                  _layer_spec((1, D_MODEL), layer, 1),
                  pl.BlockSpec((None, D_MODEL, PROJ_W), lambda i: (layer, 0, 0), pipeline_mode=pl.Buffered(1))],
        out_specs=[pl.BlockSpec((tm, PROJ_W), lambda i: (i, 0)), pl.BlockSpec((tm, 128), lambda i: (i, 0))],
        out_shape=[jax.ShapeDtypeStruct((t, PROJ_W), BF16), jax.ShapeDtypeStruct((t, 128), F32)],
        compiler_params=_params(40, 1),
        name="inproj",
    )(x2d, *_in_hbm(mod3, norm_g3, wcat))


def _stacked_heads(q, n_kv):
    rep = N_HEADS // n_kv
    return [jnp.concatenate([q[:, _hs(g * rep + r)] for r in range(rep)], axis=0) for g in range(n_kv)]


def _unstack_heads(outs, n_kv):
    rep = N_HEADS // n_kv
    m = outs[0].shape[0] // rep
    return jnp.concatenate([outs[g][r * m:(r + 1) * m] for g in range(n_kv) for r in range(rep)], axis=-1)


def _write_layer(ref, layer, value, stacked):
    if not stacked:
        ref[...] = value
        return
    for l in range(ref.shape[0]):
        ref[l] = value if l == layer else jnp.zeros(value.shape, value.dtype)


def _write_state(st_ref, layer, stacked, piece):
    for d in range(2):
        for h in range(N_HEADS):
            value = piece(d, h)
            if stacked:
                for l in range(st_ref.shape[0]):
                    st_ref[l, d, h] = value if l == layer else jnp.zeros(value.shape, value.dtype)
            else:
                st_ref[d, h] = value


def _attn_ctx_kernel(*refs, n_kv, norm, layer, first):
    if norm:
        q_ref, k_ref, v_ref, z_ref, qn_ref, kn_ref = refs[:6]
    else:
        q_ref, k_ref, v_ref, z_ref = refs[:4]
    o_ref, kt_ref, vt_ref = refs[-3:]
    q, k, v, z = q_ref[...], k_ref[...].astype(F32), v_ref[...].astype(F32), z_ref[...].astype(F32)
    if norm:
        q = _head_rms(q.astype(F32)) * qn_ref[...]
        k = _head_rms(k) * kn_ref[...]
    n_seq = kt_ref.shape[0]
    seq = k.shape[0] // n_seq
    qs, parts = [], []
    for s in range(n_seq):
        rows = slice(s * seq, (s + 1) * seq)
        _write_layer(kt_ref.at[s], layer, k[rows].T.reshape(n_kv, HEAD_DIM, seq), first)
        _write_layer(vt_ref.at[s], layer, v[rows].T.reshape(n_kv, HEAD_DIM, seq), first)
        qs += _stacked_heads(q[rows], n_kv)
        parts += [[(k[rows, _hs(g)], v[rows, _hs(g)], None, False)] for g in range(n_kv)]
    outs = _attend(qs, parts)
    o = jnp.concatenate([_unstack_heads(outs[s * n_kv:(s + 1) * n_kv], n_kv) for s in range(n_seq)], axis=0)
    o_ref[...] = (o * _silu(z)).astype(BF16)


def _attn_ctx_call(proj, prev, layer, batch, seq, off_q, off_k, off_v, off_z, n_kv, qn=None, kn=None):
    t = batch * seq
    kvw = n_kv * HEAD_DIM
    norm = qn is not None
    first = prev is None
    n_seq = CTX_SEQS_PER_STEP
    assert batch % n_seq == 0
    tm = n_seq * seq
    in_specs = [pl.BlockSpec((tm, BRANCH_W), lambda b: (b, off_q // BRANCH_W)),
                pl.BlockSpec((tm, kvw), lambda b: (b, off_k // kvw)),
                pl.BlockSpec((tm, kvw), lambda b: (b, off_v // kvw)),
                pl.BlockSpec((tm, BRANCH_W), lambda b: (b, off_z // BRANCH_W))]
    args = [proj, proj, proj, proj]
    if norm:
        in_specs += [_layer_spec((1, BRANCH_W), layer, 1), _layer_spec((1, kvw), layer, 1)]
        args += _in_hbm(qn, kn)
    aliases = {}
    if first:
        cache_spec = pl.BlockSpec((n_seq, DEPTH, n_kv, HEAD_DIM, seq), lambda b: (b, 0, 0, 0, 0))
    else:
        aliases = {len(args): 1, len(args) + 1: 2}
        in_specs += [pl.BlockSpec(memory_space=pl.ANY)] * 2
        args += list(prev)
        cache_spec = pl.BlockSpec((n_seq, None, n_kv, HEAD_DIM, seq), lambda b: (b, layer, 0, 0, 0))
    cache_shape = jax.ShapeDtypeStruct((batch, DEPTH, n_kv, HEAD_DIM, seq), F32)
    return pl.pallas_call(
        functools.partial(_attn_ctx_kernel, n_kv=n_kv, norm=norm, layer=layer, first=first),
        grid=(batch // n_seq,), in_specs=in_specs,
        out_specs=[pl.BlockSpec((tm, BRANCH_W), lambda b: (b, 0)), cache_spec, cache_spec],
        out_shape=[jax.ShapeDtypeStruct((t, BRANCH_W), BF16), cache_shape, cache_shape],
        input_output_aliases=aliases,
        compiler_params=_params(32, 1),
        name="attn_ctx_norm" if norm else "attn_ctx",
    )(*args)


def _attn_lat_kernel(q_ref, kv_ref, z_ref, ckt_ref, cvt_ref, qtab_ref, ktab_ref, qn_ref, kn_ref, o_ref,
                     k_s, v_s):
    kw = KV_HEADS * HEAD_DIM

    @pl.when(pl.program_id(1) == 0)
    def _():
        kv = kv_ref[...]
        k_s[...] = _rope(_head_rms(kv[:, :kw].astype(F32)) * kn_ref[...], ktab_ref).astype(BF16)
        v_s[...] = kv[:, kw:]

    q = _rope(_head_rms(q_ref[...].astype(F32)) * qn_ref[...], qtab_ref)
    k, v = k_s[...], v_s[...]
    outs = _attend(_stacked_heads(q, KV_HEADS),
                   [[(k[:, _hs(g)], v[:, _hs(g)], None, False), (ckt_ref[g], cvt_ref[g], None, True)]
                    for g in range(KV_HEADS)])
    o_ref[...] = (_unstack_heads(outs, KV_HEADS) * _silu(z_ref[...].astype(F32))).astype(BF16)


def _attn_lat_call(proj, cache_kt, cache_vt, layer, batch, seq, qtab, ktab, qn, kn):
    tq = 256
    nq = seq // tq
    past = cache_kt.shape[-1]
    kw = KV_HEADS * HEAD_DIM
    ctx_spec = pl.BlockSpec((None, None, KV_HEADS, HEAD_DIM, past), lambda b, i: (b, layer, 0, 0, 0))
    return pl.pallas_call(
        _attn_lat_kernel,
        grid=(batch, nq),
        in_specs=[pl.BlockSpec((tq, BRANCH_W), lambda b, i: (b * nq + i, OFF_AQKV // BRANCH_W)),
                  pl.BlockSpec((seq, 2 * kw), lambda b, i: (b, (OFF_AQKV + BRANCH_W) // (2 * kw))),
                  pl.BlockSpec((tq, BRANCH_W), lambda b, i: (b * nq + i, OFF_AZ // BRANCH_W)),
                  ctx_spec, ctx_spec,
                  pl.BlockSpec((3, tq, BRANCH_W), lambda b, i: (0, i, 0)),
                  pl.BlockSpec((3, seq, kw), lambda b, i: (0, 0, 0)),
                  _layer_spec((1, BRANCH_W), layer, 2),
                  _layer_spec((1, kw), layer, 2)],
        out_specs=pl.BlockSpec((tq, BRANCH_W), lambda b, i: (b * nq + i, 0)),
        out_shape=jax.ShapeDtypeStruct((batch * seq, BRANCH_W), BF16),
        scratch_shapes=[pltpu.VMEM((seq, kw), BF16), pltpu.VMEM((seq, kw), BF16)],
        compiler_params=_params(40, 2),
        name="attn_lat",
    )(proj, proj, proj, *_in_hbm(cache_kt, cache_vt, qtab, ktab, qn, kn))


def _na_bias_kernel(t_ref, o_ref):
    nblk = o_ref.shape[0]
    c = lax.broadcasted_iota(jnp.int32, (GRID_W, 2 * GRID_W), 0)
    kc = lax.broadcasted_iota(jnp.int32, (GRID_W, 2 * GRID_W), 1) & (GRID_W - 1)
    cs = jnp.clip(c - NA_COLS // 2, 0, GRID_W - NA_COLS)
    valid = jnp.logical_and(kc >= cs, kc < cs + NA_COLS)

    unroll = 8
    assert nblk % unroll == 0

    def body(i, carry):
        rows8 = t_ref[pl.ds(pl.multiple_of(i * unroll, unroll), unroll), :]
        for u in range(unroll):
            row = jnp.broadcast_to(rows8[u:u + 1, :], (GRID_W, 2 * GRID_W))
            skewed = pltpu.roll(row, 2 * GRID_W - (NA_COLS - 1), 1, stride=1, stride_axis=0)
            o_ref[i * unroll + u] = jnp.where(valid, skewed * LOG2E, NEG_INF)
        return carry

    lax.fori_loop(0, nblk // unroll, body, 0)


def _na_bias_call(na_bias):
    nblk = DEPTH * N_HEADS * N_DR
    rows = jnp.pad(na_bias.reshape(nblk, N_DC), ((0, 1), (0, GRID_W - N_DC)))
    pairs = jnp.concatenate([rows[:-1], rows[1:]], axis=1)
    return pl.pallas_call(
        _na_bias_kernel,
        in_specs=[pl.BlockSpec((nblk, 2 * GRID_W), lambda: (0, 0))],
        out_specs=pl.BlockSpec((nblk, GRID_W, 2 * GRID_W), lambda: (0, 0, 0)),
        out_shape=jax.ShapeDtypeStruct((nblk, GRID_W, 2 * GRID_W), F32),
        name="na_bias",
    )(pairs)


def _na_kernel(q_ref, k_ref, v_ref, z_ref, ckt_ref, cvt_ref, tb_ref, o_ref, kh_s, vh_s, *, rows):
    win = NA_ROWS * GRID_W

    @pl.when(pl.program_id(1) == 0)
    def _():
        for h in range(N_HEADS):
            kh_s[h] = k_ref[:, _hs(h)]
            vh_s[h] = v_ref[:, _hs(h)]

    qs, parts = [], []
    for i in range(NA_ROWS_PER_STEP):
        r = pl.program_id(1) * NA_ROWS_PER_STEP + i
        rs = jnp.clip(r - NA_ROWS // 2, 0, rows - NA_ROWS)
        r0 = pl.multiple_of(rs * GRID_W, GRID_W)
        q = q_ref[i * GRID_W:(i + 1) * GRID_W, :]
        dr0 = rs - r + NA_ROWS - 1
        for h in range(N_HEADS):
            bias = jnp.concatenate([tb_ref[h * N_DR + dr0 + 2 * p] for p in range(NA_ROWS // 2)], axis=1)
            qs.append(q[:, _hs(h)])
            parts.append([(kh_s[h, pl.ds(r0, win), :], vh_s[h, pl.ds(r0, win), :], bias, False),
                          (ckt_ref[h], cvt_ref[h], None, True)])
    outs = _attend(qs, parts)
    o = jnp.concatenate([jnp.concatenate(outs[i * N_HEADS:(i + 1) * N_HEADS], axis=-1)
                         for i in range(NA_ROWS_PER_STEP)], axis=0)
    o_ref[...] = (o * _silu(z_ref[...].astype(F32))).astype(BF16)


def _na_call(proj, cache_kt, cache_vt, tb, layer, batch, seq):
    rows = seq // GRID_W
    assert rows >= NA_ROWS and rows % NA_ROWS_PER_STEP == 0
    steps = rows // NA_ROWS_PER_STEP
    tq = NA_ROWS_PER_STEP * GRID_W
    past = cache_kt.shape[-1]
    nblk = N_HEADS * N_DR
    cq = OFF_D // BRANCH_W
    ctx_spec = pl.BlockSpec((None, None, N_HEADS, HEAD_DIM, past), lambda b, r: (b, layer, 0, 0, 0))
    return pl.pallas_call(
        functools.partial(_na_kernel, rows=rows),
        grid=(batch, steps),
        in_specs=[pl.BlockSpec((tq, BRANCH_W), lambda b, r: (b * steps + r, cq)),
                  pl.BlockSpec((seq, BRANCH_W), lambda b, r: (b, cq + 1)),
                  pl.BlockSpec((seq, BRANCH_W), lambda b, r: (b, cq + 2)),
                  pl.BlockSpec((tq, BRANCH_W), lambda b, r: (b * steps + r, cq + 3)),
                  ctx_spec, ctx_spec,
                  pl.BlockSpec((nblk, GRID_W, 2 * GRID_W), lambda b, r: (layer, 0, 0))],
        out_specs=pl.BlockSpec((tq, BRANCH_W), lambda b, r: (b * steps + r, 0)),
        out_shape=jax.ShapeDtypeStruct((batch * seq, BRANCH_W), BF16),
        scratch_shapes=[pltpu.VMEM((N_HEADS, seq, HEAD_DIM), BF16), pltpu.VMEM((N_HEADS, seq, HEAD_DIM), BF16)],
        compiler_params=_params(32, 2),
        name="na_lat",
    )(proj, proj, proj, proj, *_in_hbm(cache_kt, cache_vt, tb))


def _gdn_kernel(*refs, seq, has_s0, layer, emit):
    qkv_ref, z_ref, ab_ref, cw_ref, par_ref, g_ref = refs[:6]
    s0_ref = refs[6] if has_s0 else None
    (q_s, k_s, v_s, gcb_s, bcb_s, r_s, mc_s, nc_s, qp_s, op_s, egl_s, s_s, oacc_ref) = refs[-13:]
    if emit == "none":
        o_ref, st_ref = refs[-14], None
    else:
        o_ref, st_ref = refs[-15], refs[-14]
    n_chunks = seq // CHUNK
    n_levels = CHUNK.bit_length() - 1
    qkv_w = 3 * BRANCH_W
    half = SHORT_CONV // 2
    pair_w = 2 * HEAD_DIM
    pairs = [slice(p * pair_w, (p + 1) * pair_w) for p in range(BRANCH_W // pair_w)]
    tr = 256
    cpt = tr // CHUNK
    head_sum = _head_block_matrix(BRANCH_W, 1.0)

    gc_i = lax.broadcasted_iota(jnp.int32, (128, BRANCH_W), 0)
    gh_j = lax.broadcasted_iota(jnp.int32, (128, BRANCH_W), 1) >> 6
    sel_beta = [jnp.where(gc_i == gh_j + 4 * d, 1.0, 0.0).astype(BF16) for d in range(2)]
    sel_gate = [jnp.where(gc_i == gh_j + 8 + 4 * d, 1.0, 0.0).astype(BF16) for d in range(2)]
    ti = lax.broadcasted_iota(jnp.int32, (tr, tr), 0)
    tj = lax.broadcasted_iota(jnp.int32, (tr, tr), 1)
    same_chunk = (ti >> 6) == (tj >> 6)
    tri = [jnp.where(jnp.logical_and(same_chunk, ti >= tj), 1.0, 0.0).astype(BF16),
           jnp.where(jnp.logical_and(same_chunk, ti <= tj), 1.0, 0.0).astype(BF16)]
    assert tr == BRANCH_W
    lane_head = lax.broadcasted_iota(jnp.int32, (1, BRANCH_W), 1) >> 6

    halo = 16
    edge = 8
    assert half <= edge
    si = lax.broadcasted_iota(jnp.int32, (tr, tr), 0)
    sj = lax.broadcasted_iota(jnp.int32, (tr, tr), 1)
    ei = lax.broadcasted_iota(jnp.int32, (edge, halo), 0)
    ej = lax.broadcasted_iota(jnp.int32, (edge, halo), 1)
    taps = [j for j in range(SHORT_CONV) if j != half]
    shift = {j: jnp.where(sj == si + (j - half), 1.0, 0.0).astype(BF16) for j in taps}
    shift_before = {j: jnp.where(ej == ei + (halo + j - half), 1.0, 0.0).astype(BF16) for j in taps if j < half}
    shift_after = {j: jnp.where(ej == ei + (j - half - edge), 1.0, 0.0).astype(BF16) for j in taps if j > half}
    for t in range(seq // tr):
        rows = slice(t * tr, (t + 1) * tr)
        x = qkv_ref[rows, :]
        y = x.astype(F32) * cw_ref[half:half + 1, :]
        for j in taps:
            y = y + jnp.dot(shift[j], x, preferred_element_type=F32) * cw_ref[j:j + 1, :]
        if t > 0:
            before = qkv_ref[t * tr - halo:t * tr, :]
            top = sum(jnp.dot(shift_before[j], before, preferred_element_type=F32) * cw_ref[j:j + 1, :]
                      for j in shift_before)
            y = jnp.concatenate([y[:edge] + top, y[edge:]], axis=0)
        if (t + 1) * tr < seq:
            after = qkv_ref[(t + 1) * tr:(t + 1) * tr + halo, :]
            bottom = sum(jnp.dot(shift_after[j], after, preferred_element_type=F32) * cw_ref[j:j + 1, :]
                         for j in shift_after)
            y = jnp.concatenate([y[:tr - edge], y[tr - edge:] + bottom], axis=0)
        y = _silu(y)
        qq, kk = y[:, :BRANCH_W], y[:, BRANCH_W:2 * BRANCH_W]
        q_s[rows, :] = qq * lax.rsqrt(_head_reduce(qq * qq, head_sum) + EPS) * SCALE
        k_s[rows, :] = kk * lax.rsqrt(_head_reduce(kk * kk, head_sum) + EPS)
        v_s[rows, :] = y[:, 2 * BRANCH_W:]
        x = ab_ref[rows, :]
        beta = jax.nn.sigmoid(x)
        xs = x + par_ref[0:1, :]
        softplus = jnp.maximum(xs, 0.0) + jnp.log1p(jnp.exp(-jnp.abs(xs)))
        la = -jnp.exp(par_ref[1:2, :]) * softplus
        for d in range(2):
            gc = _mm_exact(tri[d], la)
            gcb_s[d, rows, :] = _mm_exact_lhs(gc, sel_gate[d])
            bcb_s[d, rows, :] = _mm_exact_lhs(beta, sel_beta[d], terms=2)
            gt = gc.T[8:16, :]
            shifted = {s: (gt if s == 0 else pltpu.roll(gt, (s * HEAD_DIM) % tr, 1))
                       for s in range(1 - cpt, N_HEADS)}
            for c in range(cpt):
                r = jnp.zeros((1, BRANCH_W), F32)
                for h in range(N_HEADS):
                    r = jnp.where(lane_head == h, shifted[h - c][4 * d + h:4 * d + h + 1, :], r)
                r_s[d, (t * cpt + c) * 8:(t * cpt + c + 1) * 8, :] = jnp.broadcast_to(r, (8, BRANCH_W))

    for d in range(2):
        if has_s0:
            s_s[d] = jnp.concatenate([s0_ref[d, h] for h in range(N_HEADS)], axis=-1)
        else:
            s_s[d] = jnp.zeros((HEAD_DIM, BRANCH_W), F32)

    li = lax.broadcasted_iota(jnp.int32, (CHUNK, BRANCH_W), 0)
    lj = lax.broadcasted_iota(jnp.int32, (CHUNK, BRANCH_W), 1) & (HEAD_DIM - 1)
    incl = (li >= lj, li <= lj)
    strict = (li > lj, li < lj)
    level = [((li ^ lj) >> l) == 1 for l in range(n_levels)]
    first_head = lax.broadcasted_iota(jnp.int32, (CHUNK, pair_w), 1) < HEAD_DIM

    def expand(y):
        yb = y.astype(BF16)
        zero = jnp.zeros((CHUNK, pair_w), BF16)
        return [jnp.concatenate([jnp.where(first_head, yb[:, p], zero), jnp.where(first_head, zero, yb[:, p])],
                                axis=0) for p in pairs]

    def bdmm(x, ybd):
        xb = x.astype(BF16)
        return jnp.concatenate([jnp.dot(xb[:, p], ybd[i], preferred_element_type=F32)
                                for i, p in enumerate(pairs)], axis=1)

    def bdmm_nt(x, ybd):
        xb = x.astype(BF16)
        return jnp.concatenate([lax.dot_general(xb[:, p], ybd[i], (((1,), (1,)), ((), ())),
                                                preferred_element_type=F32)
                                for i, p in enumerate(pairs)], axis=1)

    def tn_diag(a, b):
        ab, bb = a.astype(BF16), b.astype(BF16)
        outs = []
        for p in pairs:
            full = lax.dot_general(ab[:, p], bb[:, p], (((0,), (0,)), ((), ())), preferred_element_type=F32)
            outs.append(jnp.where(first_head, full[:HEAD_DIM], full[HEAD_DIM:]))
        return jnp.concatenate(outs, axis=1)

    def prepare(chains):
        n = range(len(chains))
        dd = [d for d, _ in chains]
        rows = [pl.ds(_aligned(c * CHUNK, CHUNK), CHUNK) for _, c in chains]
        gcb = [gcb_s[dd[i], rows[i], :] for i in n]
        bcb = [bcb_s[dd[i], rows[i], :] for i in n]
        grow = [r_s[dd[i], pl.ds(_aligned(chains[i][1] * 8, 8), 8), :][0:1, :] for i in n]
        dm = [jnp.exp(jnp.where(incl[dd[i]], gcb[i] - grow[i], NEG_INF)) for i in n]
        k = [k_s[rows[i], :] for i in n]
        q = [q_s[rows[i], :] for i in n]
        v = [v_s[rows[i], :] for i in n]
        kq = [bdmm_nt(jnp.concatenate([k[i], q[i]], axis=0), expand(k[i])) for i in n]
        a = [jnp.where(strict[dd[i]], bcb[i] * kq[i][:CHUNK] * dm[i], 0.0) for i in n]
        tm = [-jnp.where(level[0], a[i], 0.0) for i in n]
        for l in range(1, n_levels):
            b = [jnp.where(level[l], a[i], 0.0) for i in n]
            y = [b[i] + bdmm(tm[i], expand(b[i])) for i in n]
            tm = [tm[i] - (y[i] + bdmm(y[i], expand(tm[i]))) for i in n]
        eg = [jnp.exp(gcb[i]) for i in n]
        bv = [bcb[i] * v[i] for i in n]
        bk = [bcb[i] * k[i] * eg[i] for i in n]
        u = [bv[i] + bdmm(tm[i], expand(bv[i])) for i in n]
        w = [bk[i] + bdmm(tm[i], expand(bk[i])) for i in n]
        gl = [gcb[i][CHUNK - 1:CHUNK, :] if dd[i] == 0 else gcb[i][0:1, :] for i in n]
        kd = [k[i] * jnp.exp(gl[i] - gcb[i]) for i in n]
        qkm = [kq[i][CHUNK:] * dm[i] for i in n]
        mc = [tn_diag(kd[i], w[i]) for i in n]
        nc = [tn_diag(kd[i], u[i]) for i in n]
        qp = [q[i] * eg[i] - bdmm(qkm[i], expand(w[i])) for i in n]
        op = [bdmm(qkm[i], expand(u[i])) for i in n]
        for i in n:
            d, c = chains[i]
            mc_s[d, rows[i], :] = mc[i].astype(BF16)
            nc_s[d, rows[i], :] = nc[i]
            qp_s[d, rows[i], :] = qp[i].astype(BF16)
            op_s[d, rows[i], :] = op[i]
            egl_s[d, pl.ds(_aligned(c * 8, 8), 8), :] = jnp.broadcast_to(jnp.exp(gl[i]), (8, BRANCH_W))

    group = min(PREP_CHUNKS, n_chunks)
    if n_chunks == group:
        prepare([(d, c) for c in range(group) for d in range(2)])
    else:
        def prep_body(j, carry):
            prepare([(d, j * group + c) for c in range(group) for d in range(2)])
            return carry
        lax.fori_loop(0, n_chunks // group, prep_body, 0)

    def scan_step(i):
        for d, c in ((0, i), (1, n_chunks - 1 - i)):
            rows = pl.ds(_aligned(c * CHUNK, CHUNK), CHUNK)
            s = s_s[d]
            sbd = expand(s)
            oacc_ref[d, rows, :] = bdmm(qp_s[d, rows, :], sbd) + op_s[d, rows, :]
            egl = egl_s[d, pl.ds(_aligned(c * 8, 8), 8), :][0:1, :]
            s_s[d] = s * egl - bdmm(mc_s[d, rows, :], sbd) + nc_s[d, rows, :]

    unroll = min(SCAN_UNROLL, n_chunks)

    def scan_body(j, carry):
        for i in range(unroll):
            scan_step(j * unroll + i)
        return carry

    lax.fori_loop(0, n_chunks // unroll, scan_body, 0)

    if st_ref is not None:
        _write_state(st_ref, layer, emit == "first", lambda d, h: s_s[d][:, _hs(h)])
    o = oacc_ref[0] + oacc_ref[1]
    ms = _head_reduce(o * o, _head_block_matrix(BRANCH_W, 1.0 / HEAD_DIM))
    o_ref[...] = (o * lax.rsqrt(ms + EPS) * g_ref[...] * _silu(z_ref[...].astype(F32))).astype(BF16)


def _state_spec(layer, n_seq=None):
    return pl.BlockSpec((n_seq, None, 2, N_HEADS, HEAD_DIM, HEAD_DIM), lambda b: (b, layer, 0, 0, 0, 0))


def _state_output(emit, prev, layer, batch, n_args, n_seq=None):
    if emit == "none":
        return [], [], [], [], {}
    shape = jax.ShapeDtypeStruct((batch, DEPTH, 2, N_HEADS, HEAD_DIM, HEAD_DIM), F32)
    if emit == "first":
        spec = pl.BlockSpec((n_seq, DEPTH, 2, N_HEADS, HEAD_DIM, HEAD_DIM), lambda b: (b, 0, 0, 0, 0, 0))
        return [], [], [spec], [shape], {}
    return [pl.BlockSpec(memory_space=pl.ANY)], [prev], [_state_spec(layer, n_seq)], [shape], {n_args: 1}


def _gdn_call(proj, gab, cw8, par, norm_g, layer, batch, seq, s0=None, emit="none", prev=None):
    has_s0 = s0 is not None
    qkv_w = 3 * BRANCH_W
    in_specs = [pl.BlockSpec((seq, qkv_w), lambda b: (b, OFF_GQKV // qkv_w)),
                pl.BlockSpec((seq, BRANCH_W), lambda b: (b, OFF_GZ // BRANCH_W)),
                pl.BlockSpec((seq, 128), lambda b: (b, 0)),
                _layer_spec((8, qkv_w), layer, 1),
                _layer_spec((2, 128), layer, 1),
                _layer_spec((1, BRANCH_W), layer, 1)]
    args = [proj, proj, *_in_hbm(gab, cw8, par, norm_g)]
    if has_s0:
        in_specs.append(_state_spec(layer))
        args += _in_hbm(s0)
    st_in_specs, st_args, st_out_specs, st_shapes, aliases = _state_output(emit, prev, layer, batch, len(args))
    return pl.pallas_call(
        functools.partial(_gdn_kernel, seq=seq, has_s0=has_s0, layer=layer, emit=emit),
        grid=(batch,), in_specs=in_specs + st_in_specs,
        out_specs=[pl.BlockSpec((seq, BRANCH_W), lambda b: (b, 0))] + st_out_specs,
        out_shape=[jax.ShapeDtypeStruct((batch * seq, BRANCH_W), BF16)] + st_shapes,
        input_output_aliases=aliases,
        scratch_shapes=[pltpu.VMEM((seq, BRANCH_W), F32),
                        pltpu.VMEM((seq, BRANCH_W), F32),
                        pltpu.VMEM((seq, BRANCH_W), F32),
                        pltpu.VMEM((2, seq, BRANCH_W), F32),
                        pltpu.VMEM((2, seq, BRANCH_W), F32),
                        pltpu.VMEM((2, seq // CHUNK * 8, BRANCH_W), F32),
                        pltpu.VMEM((2, seq, BRANCH_W), BF16),
                        pltpu.VMEM((2, seq, BRANCH_W), F32),
                        pltpu.VMEM((2, seq, BRANCH_W), BF16),
                        pltpu.VMEM((2, seq, BRANCH_W), F32),
                        pltpu.VMEM((2, seq // CHUNK * 8, BRANCH_W), F32),
                        pltpu.VMEM((2, HEAD_DIM, BRANCH_W), F32),
                        pltpu.VMEM((2, seq, BRANCH_W), F32)],
        compiler_params=_params(48, 1),
        name="gdn",
    )(*args, *st_args)


_RET_LOG_GAMMA = [[float(np.log1p(-np.exp2(-(base + h)))) for h in range(N_HEADS)] for base in RET_DECAY_BASE]


def _ret_kernel(*refs, seq, n_seq, has_s0, layer, emit):
    qkv_ref, z_ref, g_ref = refs[:3]
    s0_ref = refs[3] if has_s0 else None
    o_ref, st_ref = (refs[-1], None) if emit == "none" else (refs[-2], refs[-1])
    tile = RET_TILE
    n_tiles = seq // tile
    problems = [(s, h) for s in range(n_seq) for h in range(N_HEADS)]
    heads = range(len(problems))
    lgf = [_RET_LOG_GAMMA[0][h] for _, h in problems]
    lgb = [_RET_LOG_GAMMA[1][h] for _, h in problems]
    a = lax.broadcasted_iota(jnp.int32, (tile, 1), 0).astype(F32)
    ef = [jnp.exp(a * lgf[h]) for h in heads]
    eif = [jnp.exp(-a * lgf[h]) for h in heads]
    eb = [jnp.exp(a * lgb[h]) for h in heads]
    eib = [jnp.exp(-a * lgb[h]) for h in heads]
    gf_tile = [float(np.exp(tile * lgf[h])) for h in heads]
    gb_tile = [float(np.exp(tile * lgb[h])) for h in heads]
    ii = lax.broadcasted_iota(jnp.int32, (tile, tile), 0)
    jj = lax.broadcasted_iota(jnp.int32, (tile, tile), 1)

    def head_cols(t, part, i):
        s, h = problems[i]
        return qkv_ref[s * seq + t * tile:s * seq + (t + 1) * tile,
                       part * BRANCH_W + h * HEAD_DIM:part * BRANCH_W + (h + 1) * HEAD_DIM]

    def initial_state(d, i):
        s, h = problems[i]
        return s0_ref[s, d, h]

    kf = [[head_cols(t, 1, h) * eif[h] for h in heads] for t in range(n_tiles)]
    kb = [[head_cols(t, 1, h) * eb[h] for h in heads] for t in range(n_tiles)]
    vs = [[head_cols(t, 2, h) for h in heads] for t in range(n_tiles)]
    use_states = has_s0 or n_tiles > 1 or st_ref is not None
    if use_states:
        kvf = [[_mm_tn(kf[t][h], vs[t][h]) for h in heads] for t in range(n_tiles)]
        kvb = [[_mm_tn(kb[t][h], vs[t][h]) for h in heads] for t in range(n_tiles)]
        zero = jnp.zeros((HEAD_DIM, HEAD_DIM), F32)
        zf = [[(float(np.exp(lgf[h])) * initial_state(0, h)) if has_s0 else zero for h in heads]]
        for t in range(n_tiles):
            zf.append([gf_tile[h] * (zf[t][h] + kvf[t][h]) for h in heads])
        acc = [initial_state(1, h) if has_s0 else zero for h in heads]
        zb = [None] * n_tiles
        for t in reversed(range(n_tiles)):
            zb[t] = [gb_tile[h] * acc[h] for h in heads]
            acc = [zb[t][h] + kvb[t][h] for h in heads]
        if st_ref is not None:
            stf = [zf[n_tiles][h] * float(np.exp(-lgf[h])) for h in heads]
            for s in range(n_seq):
                _write_state(st_ref.at[s], layer, emit == "first",
                             lambda d, h, s=s: (stf, acc)[d][s * N_HEADS + h])
    tiles = [[] for _ in range(n_seq)]
    for t in range(n_tiles):
        q = [head_cols(t, 0, h) * SCALE for h in heads]
        qf = [q[h] * ef[h] for h in heads]
        qb = [q[h] * eib[h] for h in heads]
        sd = [jnp.where(ii >= jj, _mm_nt(qf[h], kf[t][h]), 0.0) + jnp.where(ii <= jj, _mm_nt(qb[h], kb[t][h]), 0.0)
              for h in heads]
        o = [_mm(sd[h], vs[t][h]) for h in heads]
        if has_s0 or n_tiles > 1:
            o = [o[h] + _mm(qf[h], zf[t][h]) + _mm(qb[h], zb[t][h]) for h in heads]
        ms = [jnp.mean(o[h] * o[h], axis=-1, keepdims=True) for h in heads]
        for s in range(n_seq):
            tiles[s].append(jnp.concatenate([o[h] * lax.rsqrt(ms[h] + EPS) * g_ref[...]
                                             for h in range(s * N_HEADS, (s + 1) * N_HEADS)], axis=-1))
    o = jnp.concatenate([tile_out for s in range(n_seq) for tile_out in tiles[s]], axis=0)
    o_ref[...] = (o * _silu(z_ref[...].astype(F32))).astype(BF16)


def _ret_call(proj, norm_g, layer, batch, seq, s0=None, emit="none", prev=None):
    has_s0 = s0 is not None
    qkv_w = 3 * BRANCH_W
    n_seq = max(1, RET_ROWS_PER_STEP // seq)
    assert batch % n_seq == 0
    tm = n_seq * seq
    in_specs = [pl.BlockSpec((tm, qkv_w), lambda b: (b, OFF_C // qkv_w)),
                pl.BlockSpec((tm, BRANCH_W), lambda b: (b, (OFF_C + qkv_w) // BRANCH_W)),
                _layer_spec((1, HEAD_DIM), layer, 1)]
    args = [proj, proj, *_in_hbm(norm_g)]
    if has_s0:
        in_specs.append(_state_spec(layer, n_seq))
        args += _in_hbm(s0)
    st_in_specs, st_args, st_out_specs, st_shapes, aliases = _state_output(emit, prev, layer, batch, len(args),
                                                                           n_seq)
    return pl.pallas_call(
        functools.partial(_ret_kernel, seq=seq, n_seq=n_seq, has_s0=has_s0, layer=layer, emit=emit),
        grid=(batch // n_seq,), in_specs=in_specs + st_in_specs,
        out_specs=[pl.BlockSpec((tm, BRANCH_W), lambda b: (b, 0))] + st_out_specs,
        out_shape=[jax.ShapeDtypeStruct((batch * seq, BRANCH_W), BF16)] + st_shapes,
        input_output_aliases=aliases,
        compiler_params=_params(48, 1),
        name="retention",
    )(*args, *st_args)


def _out_kernel(*refs, final):
    if final:
        (h_ref, mod_ref, g_ref, oa_ref, ob_ref, oc_ref, od_ref, wg_ref, wb_ref, wo_ref, fn_ref,
         o_ref, y_ref) = refs
    else:
        h_ref, mod_ref, g_ref, oa_ref, ob_ref, oc_ref, od_ref, wg_ref, wb_ref, wo_ref, o_ref = refs
    x = h_ref[...]
    mod = mod_ref[0]
    hn = _modulated_norm(x, mod, g_ref[...]).astype(BF16)
    merged = None
    for n, br_ref in enumerate((oa_ref, ob_ref, oc_ref, od_ref)):
        gate = jax.nn.sigmoid(jnp.dot(hn, wg_ref[:, n * D_MODEL:(n + 1) * D_MODEL], preferred_element_type=F32))
        up = jnp.dot(br_ref[...], wb_ref[n], preferred_element_type=F32)
        merged = gate * up if merged is None else merged + gate * up
    out = jnp.dot(merged.astype(BF16), wo_ref[...], preferred_element_type=F32)
    hnew = x + mod[:, 2 * D_MODEL:] * out
    o_ref[...] = hnew
    if final:
        ms = jnp.mean(hnew * hnew, axis=-1, keepdims=True)
        y_ref[...] = hnew * lax.rsqrt(ms + EPS) * fn_ref[...]


def _out_call(h2d, mod3, norm_g3, branches, wg, wb, wo, layer, rows_per_mod, final_norm=None):
    t = h2d.shape[0]
    tm = 512
    final = final_norm is not None
    if mod3.shape[0] == 1:
        mod_idx = lambda i: (0, 0, 0)
    else:
        mod_idx = lambda i: ((i * tm) // rows_per_mod, 0, 0)
    once = pl.Buffered(1)
    in_specs = [pl.BlockSpec((tm, D_MODEL), lambda i: (i, 0)),
                pl.BlockSpec((1, 1, 3 * D_MODEL), mod_idx),
                _layer_spec((1, D_MODEL), layer, 1)]
    in_specs += [pl.BlockSpec((tm, BRANCH_W), lambda i: (i, 0))] * N_BRANCH
    in_specs += [pl.BlockSpec((None, D_MODEL, N_BRANCH * D_MODEL), lambda i: (layer, 0, 0), pipeline_mode=once),
                 pl.BlockSpec((None, N_BRANCH, BRANCH_W, D_MODEL), lambda i: (layer, 0, 0, 0), pipeline_mode=once),
                 pl.BlockSpec((None, D_MODEL, D_MODEL), lambda i: (layer, 0, 0), pipeline_mode=once)]
    mod3, norm_g3, wg, wb, wo = _in_hbm(mod3, norm_g3, wg, wb, wo)
    args = [h2d, mod3, norm_g3, *branches, wg, wb, wo]
    out_specs = [pl.BlockSpec((tm, D_MODEL), lambda i: (i, 0))]
    out_shape = [jax.ShapeDtypeStruct((t, D_MODEL), F32)]
    if final:
        in_specs.append(pl.BlockSpec((1, D_MODEL), lambda i: (0, 0)))
        args += _in_hbm(final_norm.reshape(1, D_MODEL))
        out_specs.append(pl.BlockSpec((tm, D_MODEL), lambda i: (i, 0)))
        out_shape.append(jax.ShapeDtypeStruct((t, D_MODEL), F32))
    return pl.pallas_call(
        functools.partial(_out_kernel, final=final),
        grid=(t // tm,), in_specs=in_specs, out_specs=out_specs, out_shape=out_shape,
        compiler_params=_params(48, 1),
        name="merge_out_final" if final else "merge_out",
    )(*args)


def _prep_weights(w_in):
    offs = np.concatenate([[0], np.cumsum(IN_SPLITS)])
    seg = lambda i, j: w_in[:, :, offs[i]:offs[j]]
    pad = jnp.zeros((DEPTH, D_MODEL, PROJ_W - OFF_GAB - IN_SPLITS[5]), w_in.dtype)
    wcat = jnp.concatenate([seg(7, 11), seg(11, 15), seg(6, 7), seg(4, 5), seg(0, 3), seg(3, 4), seg(5, 6), pad],
                           axis=2).astype(BF16)
    return wcat, seg(15, 16).astype(BF16)


def _rope_tables(seq):
    t = jnp.arange(seq)
    quarter = HEAD_DIM // 4
    inv = ROPE_THETA ** (-jnp.arange(quarter, dtype=F32) / quarter)

    def half(pos):
        ang = pos.astype(F32)[:, None] * inv
        c, s, zero = jnp.cos(ang), jnp.sin(ang), jnp.zeros_like(ang)
        return jnp.concatenate([c, c], -1), jnp.concatenate([-s, zero], -1), jnp.concatenate([zero, s], -1)

    parts = [jnp.concatenate([a, b], -1) for a, b in zip(half(t // GRID_W), half(t % GRID_W))]
    tab = jnp.stack(parts)
    return jnp.tile(tab, (1, 1, N_HEADS)), jnp.tile(tab, (1, 1, KV_HEADS))


def _layer(h2d, batch, seq, mod, pw, layer, ctx, caches, final_norm):
    proj, gab = _inproj_call(h2d, mod, pw["norm_g"], pw["wcat"], layer, seq)
    kw = KV_HEADS * HEAD_DIM
    if ctx is None:
        emit = "first" if caches is None else "update"
        akv, nkv, sg_all, sr_all = caches or (None, None, None, None)
        oa, *akv = _attn_ctx_call(proj, akv, layer, batch, seq, OFF_AQKV, OFF_AQKV + BRANCH_W,
                                  OFF_AQKV + BRANCH_W + kw, OFF_AZ, KV_HEADS, pw["qn"], pw["kn"])
        od, *nkv = _attn_ctx_call(proj, nkv, layer, batch, seq, OFF_D, OFF_D + BRANCH_W,
                                  OFF_D + 2 * BRANCH_W, OFF_D + 3 * BRANCH_W, N_HEADS)
        ob, sg_all = _gdn_call(proj, gab, pw["cw8"], pw["gdn_par"], pw["gdn_norm"], layer, batch, seq,
                               emit=emit, prev=sg_all)
        oc, sr_all = _ret_call(proj, pw["ret_norm"], layer, batch, seq, emit=emit, prev=sr_all)
        caches = (akv, nkv, sg_all, sr_all)
    else:
        oa = _attn_lat_call(proj, ctx["akt"], ctx["avt"], layer, batch, seq, ctx["qtab"], ctx["ktab"],
                            pw["qn"], pw["kn"])
        od = _na_call(proj, ctx["nkt"], ctx["nvt"], ctx["tb"], layer, batch, seq)
        ob, = _gdn_call(proj, gab, pw["cw8"], pw["gdn_par"], pw["gdn_norm"], layer, batch, seq, s0=ctx["sg"])
        oc, = _ret_call(proj, pw["ret_norm"], layer, batch, seq, s0=ctx["sr"])
    outs = _out_call(h2d, mod, pw["norm_g"], (oa, ob, oc, od), pw["wg"], pw["wb"], pw["wo"], layer, seq, final_norm)
    return outs, caches


def kernel(x_prompt, x_sample, cache_attn_k, cache_attn_v, cache_na_k, cache_na_v, state_gdn, state_ret, c, c_ctx, w_ada, b_ada, norm_g, w_in, conv_w, gdn_a_log, gdn_dt_bias, gdn_norm, attn_q_norm, attn_k_norm, ret_norm, na_bias, w_branch, w_out, final_norm):
    batch, seq, _ = x_prompt.shape
    dbatch, dseq, _ = x_sample.shape
    assert dbatch == 8, "the modulation kernel handles exactly one sublane tile of conditioning rows"

    wcat, wg = _prep_weights(w_in)
    par = jnp.zeros((DEPTH, 2, 128), F32)
    par = par.at[:, 0, 8:16].set(gdn_dt_bias.reshape(DEPTH, 8)).at[:, 1, 8:16].set(gdn_a_log.reshape(DEPTH, 8))
    pw = dict(
        w_ada=w_ada, b_ada=b_ada.reshape(DEPTH, 1, 3 * D_MODEL), norm_g=norm_g.reshape(DEPTH, 1, D_MODEL),
        wcat=wcat, wg=wg, wb=w_branch.astype(BF16), wo=w_out.astype(BF16),
        cw8=jnp.concatenate([conv_w, jnp.zeros((DEPTH, 8 - SHORT_CONV, 3 * BRANCH_W), F32)], axis=1),
        gdn_par=par,
        gdn_norm=jnp.tile(gdn_norm, (1, N_HEADS)).reshape(DEPTH, 1, BRANCH_W),
        ret_norm=ret_norm.reshape(DEPTH, 1, HEAD_DIM),
        qn=jnp.tile(attn_q_norm, (1, N_HEADS)).reshape(DEPTH, 1, BRANCH_W),
        kn=jnp.tile(attn_k_norm, (1, KV_HEADS)).reshape(DEPTH, 1, KV_HEADS * HEAD_DIM))

    cond = jnp.concatenate([jnp.broadcast_to(c_ctx, (8, D_MODEL)), c], axis=0)
    mods = _mod_call(cond, w_ada, pw["b_ada"])

    h = x_prompt.reshape(batch * seq, D_MODEL)
    caches = None
    for l in range(DEPTH):
        outs, caches = _layer(h, batch, seq, mods[l, 0:1].reshape(1, 1, 3 * D_MODEL), pw, l, None, caches,
                              final_norm if l == DEPTH - 1 else None)
        h = outs[0]
    y_prompt = outs[1].reshape(batch, seq, D_MODEL)
    token_major = lambda a: a.transpose(0, 1, 4, 2, 3)
    (akt, avt), (nkt, nvt), new_state_gdn, new_state_ret = caches
    new_attn_k, new_attn_v, new_na_k, new_na_v = (token_major(a) for a in (akt, avt, nkt, nvt))

    qtab, ktab = _rope_tables(dseq)
    feature_major = lambda a: a.transpose(0, 1, 3, 4, 2)
    ctx = dict(akt=feature_major(cache_attn_k), avt=feature_major(cache_attn_v),
               nkt=feature_major(cache_na_k), nvt=feature_major(cache_na_v),
               sg=state_gdn, sr=state_ret, tb=_na_bias_call(na_bias), qtab=qtab, ktab=ktab)
    h = x_sample.reshape(dbatch * dseq, D_MODEL)
    for l in range(DEPTH):
        outs, _ = _layer(h, dbatch, dseq, mods[l, 8:16].reshape(dbatch, 1, 3 * D_MODEL), pw, l, ctx, None,
                         final_norm if l == DEPTH - 1 else None)
        h = outs[0]
    y_sample = outs[1].reshape(dbatch, dseq, D_MODEL)
    return (y_prompt, y_sample, new_attn_k, new_attn_v, new_na_k, new_na_v, new_state_gdn, new_state_ret)
```

```python
import functools

import numpy as np
import jax
import jax.numpy as jnp
from jax import lax
from jax.experimental import pallas as pl
from jax.experimental.pallas import tpu as pltpu

F32 = jnp.float32
BF16 = jnp.bfloat16

D_MODEL = 1024
HEAD_DIM = 64
N_HEADS = 4
KV_HEADS = N_HEADS // 2
BRANCH_W = N_HEADS * HEAD_DIM
N_BRANCH = 4
DEPTH = 2
GRID_W = 64
CHUNK = 64
PREP_CHUNKS = 4
SCAN_UNROLL = 4
assert CHUNK == HEAD_DIM
SHORT_CONV = 5
NA_ROWS = 8
NA_COLS = 16
RET_ROWS_PER_STEP = 1024
CTX_SEQS_PER_STEP = 4
NA_ROWS_PER_STEP = 4
N_DR = 2 * NA_ROWS - 1
N_DC = 2 * NA_COLS - 1
ROPE_THETA = 10000.0
RET_DECAY_BASE = (5.0, 5.5)
RET_TILE = 256
EPS = 1e-6
SCALE = HEAD_DIM ** -0.5
LOG2E = 1.4426950408889634
NEG_INF = float("-inf")

IN_SPLITS = (256, 128, 128, 256, 768, 16, 256, 256, 256, 256, 256, 256, 256, 256, 256, 4096)
PROJ_W = 4096
OFF_C = 0
OFF_D = 1024
OFF_GZ = 2048
OFF_GQKV = 2304
OFF_AQKV = 3072
OFF_AZ = 3584
OFF_GAB = 3840

V7X_VMEM_BYTES = 64 * 1024 * 1024
MIB = 1024 * 1024


def _params(vmem_mib, n_axes):
    assert vmem_mib * MIB < V7X_VMEM_BYTES
    return pltpu.CompilerParams(dimension_semantics=("arbitrary",) * n_axes,
                                vmem_limit_bytes=vmem_mib * MIB)


def _layer_spec(block, layer, n_grid):
    zeros = (0,) * len(block)
    if n_grid == 1:
        return pl.BlockSpec((None,) + block, lambda i: (layer,) + zeros)
    return pl.BlockSpec((None,) + block, lambda i, j: (layer,) + zeros)


def _mm(a, b):
    return jnp.dot(a.astype(BF16), b.astype(BF16), preferred_element_type=F32)


def _mm_nt(a, b):
    return lax.dot_general(a.astype(BF16), b.astype(BF16), (((1,), (1,)), ((), ())),
                           preferred_element_type=F32)


def _mm_tn(a, b):
    return lax.dot_general(a.astype(BF16), b.astype(BF16), (((0,), (0,)), ((), ())),
                           preferred_element_type=F32)


def _split3(x):
    hi = x.astype(BF16)
    r = x - hi.astype(F32)
    mid = r.astype(BF16)
    lo = (r - mid.astype(F32)).astype(BF16)
    return hi, mid, lo


def _mm_exact(sel, x):
    hi, mid, lo = _split3(x)
    return (jnp.dot(sel, hi, preferred_element_type=F32) + jnp.dot(sel, mid, preferred_element_type=F32)
            + jnp.dot(sel, lo, preferred_element_type=F32))


def _mm_exact_lhs(x, sel, terms=3):
    return sum(jnp.dot(part, sel, preferred_element_type=F32) for part in _split3(x)[:terms])


def _silu(x):
    return x * jax.nn.sigmoid(x)


def _head_block_matrix(width, value):
    ri = lax.broadcasted_iota(jnp.int32, (width, width), 0) >> 6
    ci = lax.broadcasted_iota(jnp.int32, (width, width), 1) >> 6
    return jnp.where(ri == ci, value, 0.0).astype(BF16)


def _head_reduce(x, g):
    hi = x.astype(BF16)
    lo = (x - hi.astype(F32)).astype(BF16)
    return jnp.dot(hi, g, preferred_element_type=F32) + jnp.dot(lo, g, preferred_element_type=F32)


def _head_rms(x):
    ms = _head_reduce(x * x, _head_block_matrix(x.shape[1], 1.0 / HEAD_DIM))
    return x * lax.rsqrt(ms + EPS)


def _rope(x, tab_ref):
    w = x.shape[1]
    return (x * tab_ref[0] + pltpu.roll(x, w - 16, 1) * tab_ref[1] + pltpu.roll(x, 16, 1) * tab_ref[2])


def _attend(qs, parts):
    groups = range(len(qs))
    qs = [(q.astype(F32) * (SCALE * LOG2E)).astype(BF16) for q in qs]

    def score(q, part):
        k, _, bias, feature_major = part
        s = _mm(q, k) if feature_major else _mm_nt(q, k)
        return s if bias is None else s + bias

    scores = [[score(qs[g], part) for part in parts[g]] for g in groups]
    m = [functools.reduce(jnp.maximum, [s.max(axis=-1, keepdims=True) for s in scores[g]]) for g in groups]
    p = [[jnp.exp2(s - m[g]) for s in scores[g]] for g in groups]
    den = [sum(x.sum(axis=-1, keepdims=True) for x in p[g]) for g in groups]
    out = [sum(_mm_nt(x, part[1]) if part[3] else _mm(x, part[1]) for x, part in zip(p[g], parts[g]))
           for g in groups]
    return [out[g] / den[g] for g in groups]


def _hs(h):
    return slice(h * HEAD_DIM, (h + 1) * HEAD_DIM)


def _aligned(x, m):
    return x if isinstance(x, int) else pl.multiple_of(x, m)


def _mod_kernel(c_ref, w_ref, b_ref, o_ref):
    o_ref[...] = _mm(_silu(c_ref[...]), w_ref[...]) + b_ref[...]


def _mod_call(cond, w_ada, b_ada3):
    tn = 512
    rows = cond.shape[0]
    return pl.pallas_call(
        _mod_kernel,
        grid=(DEPTH, 3 * D_MODEL // tn),
        in_specs=[pl.BlockSpec((rows, D_MODEL), lambda l, j: (0, 0)),
                  pl.BlockSpec((None, D_MODEL, tn), lambda l, j: (l, 0, j)),
                  pl.BlockSpec((None, 1, tn), lambda l, j: (l, 0, j))],
        out_specs=pl.BlockSpec((None, rows, tn), lambda l, j: (l, 0, j)),
        out_shape=jax.ShapeDtypeStruct((DEPTH, rows, 3 * D_MODEL), F32),
        compiler_params=_params(24, 2),
        name="adaln_mod",
    )(cond, w_ada, b_ada3)


def _modulated_norm(x, mod, g):
    ms = jnp.mean(x * x, axis=-1, keepdims=True)
    y = x * lax.rsqrt(ms + EPS) * g
    return y * (1.0 + mod[:, D_MODEL:2 * D_MODEL]) + mod[:, :D_MODEL]


def _inproj_kernel(x_ref, mod_ref, g_ref, w_ref, o_ref, ab_ref):
    hn = _modulated_norm(x_ref[...], mod_ref[0], g_ref[...]).astype(BF16)
    tn = 512
    for j in range(PROJ_W // tn):
        y = jnp.dot(hn, w_ref[:, j * tn:(j + 1) * tn], preferred_element_type=F32)
        o_ref[:, j * tn:(j + 1) * tn] = y.astype(BF16)
        if j == OFF_GAB // tn:
            ab_ref[...] = y[:, OFF_GAB % tn:OFF_GAB % tn + 128]


def _inproj_call(x2d, mod3, norm_g3, wcat, layer, rows_per_mod):
    t = x2d.shape[0]
    tm = 512
    if mod3.shape[0] == 1:
        mod_idx = lambda i: (0, 0, 0)
    else:
        mod_idx = lambda i: ((i * tm) // rows_per_mod, 0, 0)
    return pl.pallas_call(
        _inproj_kernel,
        grid=(t // tm,),
        in_specs=[pl.BlockSpec((tm, D_MODEL), lambda i: (i, 0)),
                  pl.BlockSpec((1, 1, 3 * D_MODEL), mod_idx),
                  _layer_spec((1, D_MODEL), layer, 1),
                  pl.BlockSpec((None, D_MODEL, PROJ_W), lambda i: (layer, 0, 0), pipeline_mode=pl.Buffered(1))],
        out_specs=[pl.BlockSpec((tm, PROJ_W), lambda i: (i, 0)), pl.BlockSpec((tm, 128), lambda i: (i, 0))],
        out_shape=[jax.ShapeDtypeStruct((t, PROJ_W), BF16), jax.ShapeDtypeStruct((t, 128), F32)],
        compiler_params=_params(40, 1),
        name="inproj",
    )(x2d, mod3, norm_g3, wcat)


def _stacked_heads(q, n_kv):
    rep = N_HEADS // n_kv
    return [jnp.concatenate([q[:, _hs(g * rep + r)] for r in range(rep)], axis=0) for g in range(n_kv)]


def _unstack_heads(outs, n_kv):
    rep = N_HEADS // n_kv
    m = outs[0].shape[0] // rep
    return jnp.concatenate([outs[g][r * m:(r + 1) * m] for g in range(n_kv) for r in range(rep)], axis=-1)


def _write_layer(ref, layer, value, stacked):
    if not stacked:
        ref[...] = value
        return
    for l in range(ref.shape[0]):
        ref[l] = value if l == layer else jnp.zeros(value.shape, value.dtype)


def _write_state(st_ref, layer, stacked, piece):
    for d in range(2):
        for h in range(N_HEADS):
            value = piece(d, h)
            if stacked:
                for l in range(st_ref.shape[0]):
                    st_ref[l, d, h] = value if l == layer else jnp.zeros(value.shape, value.dtype)
            else:
                st_ref[d, h] = value


def _attn_ctx_kernel(*refs, n_kv, norm, layer, first):
    if norm:
        q_ref, k_ref, v_ref, z_ref, qn_ref, kn_ref = refs[:6]
    else:
        q_ref, k_ref, v_ref, z_ref = refs[:4]
    o_ref, kt_ref, vt_ref = refs[-3:]
    q, k, v, z = q_ref[...], k_ref[...].astype(F32), v_ref[...].astype(F32), z_ref[...].astype(F32)
    if norm:
        q = _head_rms(q.astype(F32)) * qn_ref[...]
        k = _head_rms(k) * kn_ref[...]
    n_seq = kt_ref.shape[0]
    seq = k.shape[0] // n_seq
    qs, parts = [], []
    for s in range(n_seq):
        rows = slice(s * seq, (s + 1) * seq)
        _write_layer(kt_ref.at[s], layer, k[rows].T.reshape(n_kv, HEAD_DIM, seq), first)
        _write_layer(vt_ref.at[s], layer, v[rows].T.reshape(n_kv, HEAD_DIM, seq), first)
        qs += _stacked_heads(q[rows], n_kv)
        parts += [[(k[rows, _hs(g)], v[rows, _hs(g)], None, False)] for g in range(n_kv)]
    outs = _attend(qs, parts)
    o = jnp.concatenate([_unstack_heads(outs[s * n_kv:(s + 1) * n_kv], n_kv) for s in range(n_seq)], axis=0)
    o_ref[...] = (o * _silu(z)).astype(BF16)


def _attn_ctx_call(proj, prev, layer, batch, seq, off_q, off_k, off_v, off_z, n_kv, qn=None, kn=None):
    t = batch * seq
    kvw = n_kv * HEAD_DIM
    norm = qn is not None
    first = prev is None
    n_seq = CTX_SEQS_PER_STEP
    assert batch % n_seq == 0
    tm = n_seq * seq
    in_specs = [pl.BlockSpec((tm, BRANCH_W), lambda b: (b, off_q // BRANCH_W)),
                pl.BlockSpec((tm, kvw), lambda b: (b, off_k // kvw)),
                pl.BlockSpec((tm, kvw), lambda b: (b, off_v // kvw)),
                pl.BlockSpec((tm, BRANCH_W), lambda b: (b, off_z // BRANCH_W))]
    args = [proj, proj, proj, proj]
    if norm:
        in_specs += [_layer_spec((1, BRANCH_W), layer, 1), _layer_spec((1, kvw), layer, 1)]
        args += [qn, kn]
    aliases = {}
    if first:
        cache_spec = pl.BlockSpec((n_seq, DEPTH, n_kv, HEAD_DIM, seq), lambda b: (b, 0, 0, 0, 0))
    else:
        aliases = {len(args): 1, len(args) + 1: 2}
        in_specs += [pl.BlockSpec(memory_space=pl.ANY)] * 2
        args += list(prev)
        cache_spec = pl.BlockSpec((n_seq, None, n_kv, HEAD_DIM, seq), lambda b: (b, layer, 0, 0, 0))
    cache_shape = jax.ShapeDtypeStruct((batch, DEPTH, n_kv, HEAD_DIM, seq), F32)
    return pl.pallas_call(
        functools.partial(_attn_ctx_kernel, n_kv=n_kv, norm=norm, layer=layer, first=first),
        grid=(batch // n_seq,), in_specs=in_specs,
        out_specs=[pl.BlockSpec((tm, BRANCH_W), lambda b: (b, 0)), cache_spec, cache_spec],
        out_shape=[jax.ShapeDtypeStruct((t, BRANCH_W), BF16), cache_shape, cache_shape],
        input_output_aliases=aliases,
        compiler_params=_params(32, 1),
        name="attn_ctx_norm" if norm else "attn_ctx",
    )(*args)


def _attn_lat_kernel(q_ref, kv_ref, z_ref, ckt_ref, cvt_ref, qtab_ref, ktab_ref, qn_ref, kn_ref, o_ref,
                     k_s, v_s):
    kw = KV_HEADS * HEAD_DIM

    @pl.when(pl.program_id(1) == 0)
    def _():
        kv = kv_ref[...]
        k_s[...] = _rope(_head_rms(kv[:, :kw].astype(F32)) * kn_ref[...], ktab_ref).astype(BF16)
        v_s[...] = kv[:, kw:]

    q = _rope(_head_rms(q_ref[...].astype(F32)) * qn_ref[...], qtab_ref)
    k, v = k_s[...], v_s[...]
    outs = _attend(_stacked_heads(q, KV_HEADS),
                   [[(k[:, _hs(g)], v[:, _hs(g)], None, False), (ckt_ref[g], cvt_ref[g], None, True)]
                    for g in range(KV_HEADS)])
    o_ref[...] = (_unstack_heads(outs, KV_HEADS) * _silu(z_ref[...].astype(F32))).astype(BF16)


def _attn_lat_call(proj, cache_kt, cache_vt, layer, batch, seq, qtab, ktab, qn, kn):
    tq = 512
    nq = seq // tq
    past = cache_kt.shape[-1]
    kw = KV_HEADS * HEAD_DIM
    ctx_spec = pl.BlockSpec((None, None, KV_HEADS, HEAD_DIM, past), lambda b, i: (b, layer, 0, 0, 0))
    return pl.pallas_call(
        _attn_lat_kernel,
        grid=(batch, nq),
        in_specs=[pl.BlockSpec((tq, BRANCH_W), lambda b, i: (b * nq + i, OFF_AQKV // BRANCH_W)),
                  pl.BlockSpec((seq, 2 * kw), lambda b, i: (b, (OFF_AQKV + BRANCH_W) // (2 * kw))),
                  pl.BlockSpec((tq, BRANCH_W), lambda b, i: (b * nq + i, OFF_AZ // BRANCH_W)),
                  ctx_spec, ctx_spec,
                  pl.BlockSpec((3, tq, BRANCH_W), lambda b, i: (0, i, 0)),
                  pl.BlockSpec((3, seq, kw), lambda b, i: (0, 0, 0)),
                  _layer_spec((1, BRANCH_W), layer, 2),
                  _layer_spec((1, kw), layer, 2)],
        out_specs=pl.BlockSpec((tq, BRANCH_W), lambda b, i: (b * nq + i, 0)),
        out_shape=jax.ShapeDtypeStruct((batch * seq, BRANCH_W), BF16),
        scratch_shapes=[pltpu.VMEM((seq, kw), BF16), pltpu.VMEM((seq, kw), BF16)],
        compiler_params=_params(40, 2),
        name="attn_lat",
    )(proj, proj, proj, cache_kt, cache_vt, qtab, ktab, qn, kn)


def _na_bias_kernel(t_ref, o_ref):
    nblk = o_ref.shape[0]
    c = lax.broadcasted_iota(jnp.int32, (GRID_W, 2 * GRID_W), 0)
    kc = lax.broadcasted_iota(jnp.int32, (GRID_W, 2 * GRID_W), 1) & (GRID_W - 1)
    cs = jnp.clip(c - NA_COLS // 2, 0, GRID_W - NA_COLS)
    valid = jnp.logical_and(kc >= cs, kc < cs + NA_COLS)

    unroll = 8
    assert nblk % unroll == 0

    def body(i, carry):
        rows8 = t_ref[pl.ds(pl.multiple_of(i * unroll, unroll), unroll), :]
        for u in range(unroll):
            row = jnp.broadcast_to(rows8[u:u + 1, :], (GRID_W, 2 * GRID_W))
            skewed = pltpu.roll(row, 2 * GRID_W - (NA_COLS - 1), 1, stride=1, stride_axis=0)
            o_ref[i * unroll + u] = jnp.where(valid, skewed * LOG2E, NEG_INF)
        return carry

    lax.fori_loop(0, nblk // unroll, body, 0)


def _na_bias_call(na_bias):
    nblk = DEPTH * N_HEADS * N_DR
    rows = jnp.pad(na_bias.reshape(nblk, N_DC), ((0, 1), (0, GRID_W - N_DC)))
    pairs = jnp.concatenate([rows[:-1], rows[1:]], axis=1)
    return pl.pallas_call(
        _na_bias_kernel,
        in_specs=[pl.BlockSpec((nblk, 2 * GRID_W), lambda: (0, 0))],
        out_specs=pl.BlockSpec((nblk, GRID_W, 2 * GRID_W), lambda: (0, 0, 0)),
        out_shape=jax.ShapeDtypeStruct((nblk, GRID_W, 2 * GRID_W), F32),
        name="na_bias",
    )(pairs)


def _na_kernel(q_ref, k_ref, v_ref, z_ref, ckt_ref, cvt_ref, tb_ref, o_ref, kh_s, vh_s, *, rows):
    win = NA_ROWS * GRID_W

    @pl.when(pl.program_id(1) == 0)
    def _():
        for h in range(N_HEADS):
            kh_s[h] = k_ref[:, _hs(h)]
            vh_s[h] = v_ref[:, _hs(h)]

    qs, parts = [], []
    for i in range(NA_ROWS_PER_STEP):
        r = pl.program_id(1) * NA_ROWS_PER_STEP + i
        rs = jnp.clip(r - NA_ROWS // 2, 0, rows - NA_ROWS)
        r0 = pl.multiple_of(rs * GRID_W, GRID_W)
        q = q_ref[i * GRID_W:(i + 1) * GRID_W, :]
        dr0 = rs - r + NA_ROWS - 1
        for h in range(N_HEADS):
            bias = jnp.concatenate([tb_ref[h * N_DR + dr0 + 2 * p] for p in range(NA_ROWS // 2)], axis=1)
            qs.append(q[:, _hs(h)])
            parts.append([(kh_s[h, pl.ds(r0, win), :], vh_s[h, pl.ds(r0, win), :], bias, False),
                          (ckt_ref[h], cvt_ref[h], None, True)])
    outs = _attend(qs, parts)
    o = jnp.concatenate([jnp.concatenate(outs[i * N_HEADS:(i + 1) * N_HEADS], axis=-1)
                         for i in range(NA_ROWS_PER_STEP)], axis=0)
    o_ref[...] = (o * _silu(z_ref[...].astype(F32))).astype(BF16)


def _na_call(proj, cache_kt, cache_vt, tb, layer, batch, seq):
    rows = seq // GRID_W
    assert rows >= NA_ROWS and rows % NA_ROWS_PER_STEP == 0
    steps = rows // NA_ROWS_PER_STEP
    tq = NA_ROWS_PER_STEP * GRID_W
    past = cache_kt.shape[-1]
    nblk = N_HEADS * N_DR
    cq = OFF_D // BRANCH_W
    ctx_spec = pl.BlockSpec((None, None, N_HEADS, HEAD_DIM, past), lambda b, r: (b, layer, 0, 0, 0))
    return pl.pallas_call(
        functools.partial(_na_kernel, rows=rows),
        grid=(batch, steps),
        in_specs=[pl.BlockSpec((tq, BRANCH_W), lambda b, r: (b * steps + r, cq)),
                  pl.BlockSpec((seq, BRANCH_W), lambda b, r: (b, cq + 1)),
                  pl.BlockSpec((seq, BRANCH_W), lambda b, r: (b, cq + 2)),
                  pl.BlockSpec((tq, BRANCH_W), lambda b, r: (b * steps + r, cq + 3)),
                  ctx_spec, ctx_spec,
                  pl.BlockSpec((nblk, GRID_W, 2 * GRID_W), lambda b, r: (layer, 0, 0))],
        out_specs=pl.BlockSpec((tq, BRANCH_W), lambda b, r: (b * steps + r, 0)),
        out_shape=jax.ShapeDtypeStruct((batch * seq, BRANCH_W), BF16),
        scratch_shapes=[pltpu.VMEM((N_HEADS, seq, HEAD_DIM), BF16), pltpu.VMEM((N_HEADS, seq, HEAD_DIM), BF16)],
        compiler_params=_params(32, 2),
        name="na_lat",
    )(proj, proj, proj, proj, cache_kt, cache_vt, tb)


def _gdn_kernel(*refs, seq, has_s0, layer, emit):
    qkv_ref, z_ref, ab_ref, cw_ref, par_ref, g_ref = refs[:6]
    s0_ref = refs[6] if has_s0 else None
    (q_s, k_s, v_s, gcb_s, bcb_s, r_s, mc_s, nc_s, qp_s, op_s, egl_s, s_s, oacc_ref) = refs[-13:]
    if emit == "none":
        o_ref, st_ref = refs[-14], None
    else:
        o_ref, st_ref = refs[-15], refs[-14]
    n_chunks = seq // CHUNK
    n_levels = CHUNK.bit_length() - 1
    qkv_w = 3 * BRANCH_W
    half = SHORT_CONV // 2
    pair_w = 2 * HEAD_DIM
    pairs = [slice(p * pair_w, (p + 1) * pair_w) for p in range(BRANCH_W // pair_w)]
    tr = 256
    cpt = tr // CHUNK
    head_sum = _head_block_matrix(BRANCH_W, 1.0)

    gc_i = lax.broadcasted_iota(jnp.int32, (128, BRANCH_W), 0)
    gh_j = lax.broadcasted_iota(jnp.int32, (128, BRANCH_W), 1) >> 6
    sel_beta = [jnp.where(gc_i == gh_j + 4 * d, 1.0, 0.0).astype(BF16) for d in range(2)]
    sel_gate = [jnp.where(gc_i == gh_j + 8 + 4 * d, 1.0, 0.0).astype(BF16) for d in range(2)]
    ti = lax.broadcasted_iota(jnp.int32, (tr, tr), 0)
    tj = lax.broadcasted_iota(jnp.int32, (tr, tr), 1)
    same_chunk = (ti >> 6) == (tj >> 6)
    tri = [jnp.where(jnp.logical_and(same_chunk, ti >= tj), 1.0, 0.0).astype(BF16),
           jnp.where(jnp.logical_and(same_chunk, ti <= tj), 1.0, 0.0).astype(BF16)]
    assert tr == BRANCH_W
    lane_head = lax.broadcasted_iota(jnp.int32, (1, BRANCH_W), 1) >> 6

    halo = 16
    edge = 8
    assert half <= edge
    si = lax.broadcasted_iota(jnp.int32, (tr, tr), 0)
    sj = lax.broadcasted_iota(jnp.int32, (tr, tr), 1)
    ei = lax.broadcasted_iota(jnp.int32, (edge, halo), 0)
    ej = lax.broadcasted_iota(jnp.int32, (edge, halo), 1)
    taps = [j for j in range(SHORT_CONV) if j != half]
    shift = {j: jnp.where(sj == si + (j - half), 1.0, 0.0).astype(BF16) for j in taps}
    shift_before = {j: jnp.where(ej == ei + (halo + j - half), 1.0, 0.0).astype(BF16) for j in taps if j < half}
    shift_after = {j: jnp.where(ej == ei + (j - half - edge), 1.0, 0.0).astype(BF16) for j in taps if j > half}
    for t in range(seq // tr):
        rows = slice(t * tr, (t + 1) * tr)
        x = qkv_ref[rows, :]
        y = x.astype(F32) * cw_ref[half:half + 1, :]
        for j in taps:
            y = y + jnp.dot(shift[j], x, preferred_element_type=F32) * cw_ref[j:j + 1, :]
        if t > 0:
            before = qkv_ref[t * tr - halo:t * tr, :]
            top = sum(jnp.dot(shift_before[j], before, preferred_element_type=F32) * cw_ref[j:j + 1, :]
                      for j in shift_before)
            y = jnp.concatenate([y[:edge] + top, y[edge:]], axis=0)
        if (t + 1) * tr < seq:
            after = qkv_ref[(t + 1) * tr:(t + 1) * tr + halo, :]
            bottom = sum(jnp.dot(shift_after[j], after, preferred_element_type=F32) * cw_ref[j:j + 1, :]
                         for j in shift_after)
            y = jnp.concatenate([y[:tr - edge], y[tr - edge:] + bottom], axis=0)
        y = _silu(y)
        qq, kk = y[:, :BRANCH_W], y[:, BRANCH_W:2 * BRANCH_W]
        q_s[rows, :] = qq * lax.rsqrt(_head_reduce(qq * qq, head_sum) + EPS) * SCALE
        k_s[rows, :] = kk * lax.rsqrt(_head_reduce(kk * kk, head_sum) + EPS)
        v_s[rows, :] = y[:, 2 * BRANCH_W:]
        x = ab_ref[rows, :]
        beta = jax.nn.sigmoid(x)
        xs = x + par_ref[0:1, :]
        softplus = jnp.maximum(xs, 0.0) + jnp.log1p(jnp.exp(-jnp.abs(xs)))
        la = -jnp.exp(par_ref[1:2, :]) * softplus
        for d in range(2):
            gc = _mm_exact(tri[d], la)
            gcb_s[d, rows, :] = _mm_exact_lhs(gc, sel_gate[d])
            bcb_s[d, rows, :] = _mm_exact_lhs(beta, sel_beta[d], terms=2)
            gt = gc.T[8:16, :]
            shifted = {s: (gt if s == 0 else pltpu.roll(gt, (s * HEAD_DIM) % tr, 1))
                       for s in range(1 - cpt, N_HEADS)}
            for c in range(cpt):
                r = jnp.zeros((1, BRANCH_W), F32)
                for h in range(N_HEADS):
                    r = jnp.where(lane_head == h, shifted[h - c][4 * d + h:4 * d + h + 1, :], r)
                r_s[d, (t * cpt + c) * 8:(t * cpt + c + 1) * 8, :] = jnp.broadcast_to(r, (8, BRANCH_W))

    for d in range(2):
        if has_s0:
            s_s[d] = jnp.concatenate([s0_ref[d, h] for h in range(N_HEADS)], axis=-1)
        else:
            s_s[d] = jnp.zeros((HEAD_DIM, BRANCH_W), F32)

    li = lax.broadcasted_iota(jnp.int32, (CHUNK, BRANCH_W), 0)
    lj = lax.broadcasted_iota(jnp.int32, (CHUNK, BRANCH_W), 1) & (HEAD_DIM - 1)
    incl = (li >= lj, li <= lj)
    strict = (li > lj, li < lj)
    level = [((li ^ lj) >> l) == 1 for l in range(n_levels)]
    first_head = lax.broadcasted_iota(jnp.int32, (CHUNK, pair_w), 1) < HEAD_DIM

    def expand(y):
        yb = y.astype(BF16)
        zero = jnp.zeros((CHUNK, pair_w), BF16)
        return [jnp.concatenate([jnp.where(first_head, yb[:, p], zero), jnp.where(first_head, zero, yb[:, p])],
                                axis=0) for p in pairs]

    def bdmm(x, ybd):
        xb = x.astype(BF16)
        return jnp.concatenate([jnp.dot(xb[:, p], ybd[i], preferred_element_type=F32)
                                for i, p in enumerate(pairs)], axis=1)

    def bdmm_nt(x, ybd):
        xb = x.astype(BF16)
        return jnp.concatenate([lax.dot_general(xb[:, p], ybd[i], (((1,), (1,)), ((), ())),
                                                preferred_element_type=F32)
                                for i, p in enumerate(pairs)], axis=1)

    def bdmm2(x, y1, y2):
        xb = x.astype(BF16)
        e1, e2 = expand(y1), expand(y2)
        outs = [jnp.dot(xb[:, p], jnp.concatenate([e1[i], e2[i]], axis=1), preferred_element_type=F32)
                for i, p in enumerate(pairs)]
        return (jnp.concatenate([o[:, :pair_w] for o in outs], axis=1),
                jnp.concatenate([o[:, pair_w:] for o in outs], axis=1))

    def tn_diag2(a, b1, b2):
        ab, b1b, b2b = a.astype(BF16), b1.astype(BF16), b2.astype(BF16)
        outs1, outs2 = [], []
        for p in pairs:
            full = lax.dot_general(ab[:, p], jnp.concatenate([b1b[:, p], b2b[:, p]], axis=1),
                                   (((0,), (0,)), ((), ())), preferred_element_type=F32)
            outs1.append(jnp.where(first_head, full[:HEAD_DIM, :pair_w], full[HEAD_DIM:, :pair_w]))
            outs2.append(jnp.where(first_head, full[:HEAD_DIM, pair_w:], full[HEAD_DIM:, pair_w:]))
        return jnp.concatenate(outs1, axis=1), jnp.concatenate(outs2, axis=1)

    def prepare(chains):
        n = range(len(chains))
        dd = [d for d, _ in chains]
        rows = [pl.ds(_aligned(c * CHUNK, CHUNK), CHUNK) for _, c in chains]
        gcb = [gcb_s[dd[i], rows[i], :] for i in n]
        bcb = [bcb_s[dd[i], rows[i], :] for i in n]
        grow = [r_s[dd[i], pl.ds(_aligned(chains[i][1] * 8, 8), 8), :][0:1, :] for i in n]
        dm = [jnp.exp(jnp.where(incl[dd[i]], gcb[i] - grow[i], NEG_INF)) for i in n]
        k = [k_s[rows[i], :] for i in n]
        q = [q_s[rows[i], :] for i in n]
        v = [v_s[rows[i], :] for i in n]
        kq = [bdmm_nt(jnp.concatenate([k[i], q[i]], axis=0), expand(k[i])) for i in n]
        a = [jnp.where(strict[dd[i]], bcb[i] * kq[i][:CHUNK] * dm[i], 0.0) for i in n]
        tm = [-jnp.where(level[0], a[i], 0.0) for i in n]
        for l in range(1, n_levels):
            b = [jnp.where(level[l], a[i], 0.0) for i in n]
            y = [b[i] + bdmm(tm[i], expand(b[i])) for i in n]
            tm = [tm[i] - (y[i] + bdmm(y[i], expand(tm[i]))) for i in n]
        eg = [jnp.exp(gcb[i]) for i in n]
        bv = [bcb[i] * v[i] for i in n]
        bk = [bcb[i] * k[i] * eg[i] for i in n]
        tuw = [bdmm2(tm[i], bv[i], bk[i]) for i in n]
        u = [bv[i] + tuw[i][0] for i in n]
        w = [bk[i] + tuw[i][1] for i in n]
        gl = [gcb[i][CHUNK - 1:CHUNK, :] if dd[i] == 0 else gcb[i][0:1, :] for i in n]
        kd = [k[i] * jnp.exp(gl[i] - gcb[i]) for i in n]
        qkm = [kq[i][CHUNK:] * dm[i] for i in n]
        mnc = [tn_diag2(kd[i], w[i], u[i]) for i in n]
        mc = [mnc[i][0] for i in n]
        nc = [mnc[i][1] for i in n]
        qwu = [bdmm2(qkm[i], w[i], u[i]) for i in n]
        qp = [q[i] * eg[i] - qwu[i][0] for i in n]
        op = [qwu[i][1] for i in n]
        for i in n:
            d, c = chains[i]
            mc_s[d, rows[i], :] = mc[i].astype(BF16)
            nc_s[d, rows[i], :] = nc[i]
            qp_s[d, rows[i], :] = qp[i].astype(BF16)
            op_s[d, rows[i], :] = op[i]
            egl_s[d, pl.ds(_aligned(c * 8, 8), 8), :] = jnp.broadcast_to(jnp.exp(gl[i]), (8, BRANCH_W))

    group = min(PREP_CHUNKS, n_chunks)
    if n_chunks == group:
        prepare([(d, c) for c in range(group) for d in range(2)])
    else:
        def prep_body(j, carry):
            prepare([(d, j * group + c) for c in range(group) for d in range(2)])
            return carry
        lax.fori_loop(0, n_chunks // group, prep_body, 0)

    def scan_step(i):
        for d, c in ((0, i), (1, n_chunks - 1 - i)):
            rows = pl.ds(_aligned(c * CHUNK, CHUNK), CHUNK)
            s = s_s[d]
            sbd = expand(s)
            oacc_ref[d, rows, :] = bdmm(qp_s[d, rows, :], sbd) + op_s[d, rows, :]
            egl = egl_s[d, pl.ds(_aligned(c * 8, 8), 8), :][0:1, :]
            s_s[d] = s * egl - bdmm(mc_s[d, rows, :], sbd) + nc_s[d, rows, :]

    unroll = min(SCAN_UNROLL, n_chunks)

    def scan_body(j, carry):
        for i in range(unroll):
            scan_step(j * unroll + i)
        return carry

    lax.fori_loop(0, n_chunks // unroll, scan_body, 0)

    if st_ref is not None:
        _write_state(st_ref, layer, emit == "first", lambda d, h: s_s[d][:, _hs(h)])
    o = oacc_ref[0] + oacc_ref[1]
    ms = _head_reduce(o * o, _head_block_matrix(BRANCH_W, 1.0 / HEAD_DIM))
    o_ref[...] = (o * lax.rsqrt(ms + EPS) * g_ref[...] * _silu(z_ref[...].astype(F32))).astype(BF16)


def _state_spec(layer, n_seq=None):
    return pl.BlockSpec((n_seq, None, 2, N_HEADS, HEAD_DIM, HEAD_DIM), lambda b: (b, layer, 0, 0, 0, 0))


def _state_output(emit, prev, layer, batch, n_args, n_seq=None):
    if emit == "none":
        return [], [], [], [], {}
    shape = jax.ShapeDtypeStruct((batch, DEPTH, 2, N_HEADS, HEAD_DIM, HEAD_DIM), F32)
    if emit == "first":
        spec = pl.BlockSpec((n_seq, DEPTH, 2, N_HEADS, HEAD_DIM, HEAD_DIM), lambda b: (b, 0, 0, 0, 0, 0))
        return [], [], [spec], [shape], {}
    return [pl.BlockSpec(memory_space=pl.ANY)], [prev], [_state_spec(layer, n_seq)], [shape], {n_args: 1}


def _gdn_call(proj, gab, cw8, par, norm_g, layer, batch, seq, s0=None, emit="none", prev=None):
    has_s0 = s0 is not None
    qkv_w = 3 * BRANCH_W
    in_specs = [pl.BlockSpec((seq, qkv_w), lambda b: (b, OFF_GQKV // qkv_w)),
                pl.BlockSpec((seq, BRANCH_W), lambda b: (b, OFF_GZ // BRANCH_W)),
                pl.BlockSpec((seq, 128), lambda b: (b, 0)),
                _layer_spec((8, qkv_w), layer, 1),
                _layer_spec((2, 128), layer, 1),
                _layer_spec((1, BRANCH_W), layer, 1)]
    args = [proj, proj, gab, cw8, par, norm_g]
    if has_s0:
        in_specs.append(_state_spec(layer))
        args.append(s0)
    st_in_specs, st_args, st_out_specs, st_shapes, aliases = _state_output(emit, prev, layer, batch, len(args))
    return pl.pallas_call(
        functools.partial(_gdn_kernel, seq=seq, has_s0=has_s0, layer=layer, emit=emit),
        grid=(batch,), in_specs=in_specs + st_in_specs,
        out_specs=[pl.BlockSpec((seq, BRANCH_W), lambda b: (b, 0))] + st_out_specs,
        out_shape=[jax.ShapeDtypeStruct((batch * seq, BRANCH_W), BF16)] + st_shapes,
        input_output_aliases=aliases,
        scratch_shapes=[pltpu.VMEM((seq, BRANCH_W), F32),
                        pltpu.VMEM((seq, BRANCH_W), F32),
                        pltpu.VMEM((seq, BRANCH_W), F32),
                        pltpu.VMEM((2, seq, BRANCH_W), F32),
                        pltpu.VMEM((2, seq, BRANCH_W), F32),
                        pltpu.VMEM((2, seq // CHUNK * 8, BRANCH_W), F32),
                        pltpu.VMEM((2, seq, BRANCH_W), BF16),
                        pltpu.VMEM((2, seq, BRANCH_W), F32),
                        pltpu.VMEM((2, seq, BRANCH_W), BF16),
                        pltpu.VMEM((2, seq, BRANCH_W), F32),
                        pltpu.VMEM((2, seq // CHUNK * 8, BRANCH_W), F32),
                        pltpu.VMEM((2, HEAD_DIM, BRANCH_W), F32),
                        pltpu.VMEM((2, seq, BRANCH_W), F32)],
        compiler_params=_params(48, 1),
        name="gdn",
    )(*args, *st_args)


_RET_LOG_GAMMA = [[float(np.log1p(-np.exp2(-(base + h)))) for h in range(N_HEADS)] for base in RET_DECAY_BASE]


def _ret_kernel(*refs, seq, n_seq, has_s0, layer, emit):
    qkv_ref, z_ref, g_ref = refs[:3]
    s0_ref = refs[3] if has_s0 else None
    o_ref, st_ref = (refs[-1], None) if emit == "none" else (refs[-2], refs[-1])
    tile = RET_TILE
    n_tiles = seq // tile
    problems = [(s, h) for s in range(n_seq) for h in range(N_HEADS)]
    heads = range(len(problems))
    lgf = [_RET_LOG_GAMMA[0][h] for _, h in problems]
    lgb = [_RET_LOG_GAMMA[1][h] for _, h in problems]
    a = lax.broadcasted_iota(jnp.int32, (tile, 1), 0).astype(F32)
    ef = [jnp.exp(a * lgf[h]) for h in heads]
    eif = [jnp.exp(-a * lgf[h]) for h in heads]
    eb = [jnp.exp(a * lgb[h]) for h in heads]
    eib = [jnp.exp(-a * lgb[h]) for h in heads]
    gf_tile = [float(np.exp(tile * lgf[h])) for h in heads]
    gb_tile = [float(np.exp(tile * lgb[h])) for h in heads]
    ii = lax.broadcasted_iota(jnp.int32, (tile, tile), 0)
    jj = lax.broadcasted_iota(jnp.int32, (tile, tile), 1)

    def head_cols(t, part, i):
        s, h = problems[i]
        return qkv_ref[s * seq + t * tile:s * seq + (t + 1) * tile,
                       part * BRANCH_W + h * HEAD_DIM:part * BRANCH_W + (h + 1) * HEAD_DIM]

    def initial_state(d, i):
        s, h = problems[i]
        return s0_ref[s, d, h]

    kf = [[head_cols(t, 1, h) * eif[h] for h in heads] for t in range(n_tiles)]
    kb = [[head_cols(t, 1, h) * eb[h] for h in heads] for t in range(n_tiles)]
    vs = [[head_cols(t, 2, h) for h in heads] for t in range(n_tiles)]
    use_states = has_s0 or n_tiles > 1 or st_ref is not None
    if use_states:
        kvf = [[_mm_tn(kf[t][h], vs[t][h]) for h in heads] for t in range(n_tiles)]
        kvb = [[_mm_tn(kb[t][h], vs[t][h]) for h in heads] for t in range(n_tiles)]
        zero = jnp.zeros((HEAD_DIM, HEAD_DIM), F32)
        zf = [[(float(np.exp(lgf[h])) * initial_state(0, h)) if has_s0 else zero for h in heads]]
        for t in range(n_tiles):
            zf.append([gf_tile[h] * (zf[t][h] + kvf[t][h]) for h in heads])
        acc = [initial_state(1, h) if has_s0 else zero for h in heads]
        zb = [None] * n_tiles
        for t in reversed(range(n_tiles)):
            zb[t] = [gb_tile[h] * acc[h] for h in heads]
            acc = [zb[t][h] + kvb[t][h] for h in heads]
        if st_ref is not None:
            stf = [zf[n_tiles][h] * float(np.exp(-lgf[h])) for h in heads]
            for s in range(n_seq):
                _write_state(st_ref.at[s], layer, emit == "first",
                             lambda d, h, s=s: (stf, acc)[d][s * N_HEADS + h])
    tiles = [[] for _ in range(n_seq)]
    for t in range(n_tiles):
        q = [head_cols(t, 0, h) * SCALE for h in heads]
        qf = [q[h] * ef[h] for h in heads]
        qb = [q[h] * eib[h] for h in heads]
        sd = [jnp.where(ii >= jj, _mm_nt(qf[h], kf[t][h]), 0.0) + jnp.where(ii <= jj, _mm_nt(qb[h], kb[t][h]), 0.0)
              for h in heads]
        o = [_mm(sd[h], vs[t][h]) for h in heads]
        if has_s0 or n_tiles > 1:
            o = [o[h] + _mm(qf[h], zf[t][h]) + _mm(qb[h], zb[t][h]) for h in heads]
        ms = [jnp.mean(o[h] * o[h], axis=-1, keepdims=True) for h in heads]
        for s in range(n_seq):
            tiles[s].append(jnp.concatenate([o[h] * lax.rsqrt(ms[h] + EPS) * g_ref[...]
                                             for h in range(s * N_HEADS, (s + 1) * N_HEADS)], axis=-1))
    o = jnp.concatenate([tile_out for s in range(n_seq) for tile_out in tiles[s]], axis=0)
    o_ref[...] = (o * _silu(z_ref[...].astype(F32))).astype(BF16)


def _ret_call(proj, norm_g, layer, batch, seq, s0=None, emit="none", prev=None):
    has_s0 = s0 is not None
    qkv_w = 3 * BRANCH_W
    n_seq = max(1, RET_ROWS_PER_STEP // seq)
    assert batch % n_seq == 0
    tm = n_seq * seq
    in_specs = [pl.BlockSpec((tm, qkv_w), lambda b: (b, OFF_C // qkv_w)),
                pl.BlockSpec((tm, BRANCH_W), lambda b: (b, (OFF_C + qkv_w) // BRANCH_W)),
                _layer_spec((1, HEAD_DIM), layer, 1)]
    args = [proj, proj, norm_g]
    if has_s0:
        in_specs.append(_state_spec(layer, n_seq))
        args.append(s0)
    st_in_specs, st_args, st_out_specs, st_shapes, aliases = _state_output(emit, prev, layer, batch, len(args),
                                                                           n_seq)
    return pl.pallas_call(
        functools.partial(_ret_kernel, seq=seq, n_seq=n_seq, has_s0=has_s0, layer=layer, emit=emit),
        grid=(batch // n_seq,), in_specs=in_specs + st_in_specs,
        out_specs=[pl.BlockSpec((tm, BRANCH_W), lambda b: (b, 0))] + st_out_specs,
        out_shape=[jax.ShapeDtypeStruct((batch * seq, BRANCH_W), BF16)] + st_shapes,
        input_output_aliases=aliases,
        compiler_params=_params(48, 1),
        name="retention",
    )(*args, *st_args)


def _out_kernel(*refs, final):
    if final:
        (h_ref, mod_ref, g_ref, oa_ref, ob_ref, oc_ref, od_ref, wg_ref, wb_ref, wo_ref, fn_ref,
         o_ref, y_ref) = refs
    else:
        h_ref, mod_ref, g_ref, oa_ref, ob_ref, oc_ref, od_ref, wg_ref, wb_ref, wo_ref, o_ref = refs
    x = h_ref[...]
    mod = mod_ref[0]
    hn = _modulated_norm(x, mod, g_ref[...]).astype(BF16)
    merged = None
    for n, br_ref in enumerate((oa_ref, ob_ref, oc_ref, od_ref)):
        gate = jax.nn.sigmoid(jnp.dot(hn, wg_ref[:, n * D_MODEL:(n + 1) * D_MODEL], preferred_element_type=F32))
        up = jnp.dot(br_ref[...], wb_ref[n], preferred_element_type=F32)
        merged = gate * up if merged is None else merged + gate * up
    out = jnp.dot(merged.astype(BF16), wo_ref[...], preferred_element_type=F32)
    hnew = x + mod[:, 2 * D_MODEL:] * out
    o_ref[...] = hnew
    if final:
        ms = jnp.mean(hnew * hnew, axis=-1, keepdims=True)
        y_ref[...] = hnew * lax.rsqrt(ms + EPS) * fn_ref[...]


def _out_call(h2d, mod3, norm_g3, branches, wg, wb, wo, layer, rows_per_mod, final_norm=None):
    t = h2d.shape[0]
    tm = 512
    final = final_norm is not None
    if mod3.shape[0] == 1:
        mod_idx = lambda i: (0, 0, 0)
    else:
        mod_idx = lambda i: ((i * tm) // rows_per_mod, 0, 0)
    once = pl.Buffered(1)
    in_specs = [pl.BlockSpec((tm, D_MODEL), lambda i: (i, 0)),
                pl.BlockSpec((1, 1, 3 * D_MODEL), mod_idx),
                _layer_spec((1, D_MODEL), layer, 1)]
    in_specs += [pl.BlockSpec((tm, BRANCH_W), lambda i: (i, 0))] * N_BRANCH
    in_specs += [pl.BlockSpec((None, D_MODEL, N_BRANCH * D_MODEL), lambda i: (layer, 0, 0), pipeline_mode=once),
                 pl.BlockSpec((None, N_BRANCH, BRANCH_W, D_MODEL), lambda i: (layer, 0, 0, 0), pipeline_mode=once),
                 pl.BlockSpec((None, D_MODEL, D_MODEL), lambda i: (layer, 0, 0), pipeline_mode=once)]
    args = [h2d, mod3, norm_g3, *branches, wg, wb, wo]
    out_specs = [pl.BlockSpec((tm, D_MODEL), lambda i: (i, 0))]
    out_shape = [jax.ShapeDtypeStruct((t, D_MODEL), F32)]
    if final:
        in_specs.append(pl.BlockSpec((1, D_MODEL), lambda i: (0, 0)))
        args.append(final_norm.reshape(1, D_MODEL))
        out_specs.append(pl.BlockSpec((tm, D_MODEL), lambda i: (i, 0)))
        out_shape.append(jax.ShapeDtypeStruct((t, D_MODEL), F32))
    return pl.pallas_call(
        functools.partial(_out_kernel, final=final),
        grid=(t // tm,), in_specs=in_specs, out_specs=out_specs, out_shape=out_shape,
        compiler_params=_params(48, 1),
        name="merge_out_final" if final else "merge_out",
    )(*args)


def _prep_weights(w_in):
    offs = np.concatenate([[0], np.cumsum(IN_SPLITS)])
    seg = lambda i, j: w_in[:, :, offs[i]:offs[j]]
    pad = jnp.zeros((DEPTH, D_MODEL, PROJ_W - OFF_GAB - IN_SPLITS[5]), w_in.dtype)
    wcat = jnp.concatenate([seg(7, 11), seg(11, 15), seg(6, 7), seg(4, 5), seg(0, 3), seg(3, 4), seg(5, 6), pad],
                           axis=2).astype(BF16)
    return wcat, seg(15, 16).astype(BF16)


def _rope_tables(seq):
    t = jnp.arange(seq)
    quarter = HEAD_DIM // 4
    inv = ROPE_THETA ** (-jnp.arange(quarter, dtype=F32) / quarter)

    def half(pos):
        ang = pos.astype(F32)[:, None] * inv
        c, s, zero = jnp.cos(ang), jnp.sin(ang), jnp.zeros_like(ang)
        return jnp.concatenate([c, c], -1), jnp.concatenate([-s, zero], -1), jnp.concatenate([zero, s], -1)

    parts = [jnp.concatenate([a, b], -1) for a, b in zip(half(t // GRID_W), half(t % GRID_W))]
    tab = jnp.stack(parts)
    return jnp.tile(tab, (1, 1, N_HEADS)), jnp.tile(tab, (1, 1, KV_HEADS))


def _layer(h2d, batch, seq, mod, pw, layer, ctx, caches, final_norm):
    proj, gab = _inproj_call(h2d, mod, pw["norm_g"], pw["wcat"], layer, seq)
    kw = KV_HEADS * HEAD_DIM
    if ctx is None:
        emit = "first" if caches is None else "update"
        akv, nkv, sg_all, sr_all = caches or (None, None, None, None)
        oa, *akv = _attn_ctx_call(proj, akv, layer, batch, seq, OFF_AQKV, OFF_AQKV + BRANCH_W,
                                  OFF_AQKV + BRANCH_W + kw, OFF_AZ, KV_HEADS, pw["qn"], pw["kn"])
        od, *nkv = _attn_ctx_call(proj, nkv, layer, batch, seq, OFF_D, OFF_D + BRANCH_W,
                                  OFF_D + 2 * BRANCH_W, OFF_D + 3 * BRANCH_W, N_HEADS)
        ob, sg_all = _gdn_call(proj, gab, pw["cw8"], pw["gdn_par"], pw["gdn_norm"], layer, batch, seq,
                               emit=emit, prev=sg_all)
        oc, sr_all = _ret_call(proj, pw["ret_norm"], layer, batch, seq, emit=emit, prev=sr_all)
        caches = (akv, nkv, sg_all, sr_all)
    else:
        oa = _attn_lat_call(proj, ctx["akt"], ctx["avt"], layer, batch, seq, ctx["qtab"], ctx["ktab"],
                            pw["qn"], pw["kn"])
        od = _na_call(proj, ctx["nkt"], ctx["nvt"], ctx["tb"], layer, batch, seq)
        ob, = _gdn_call(proj, gab, pw["cw8"], pw["gdn_par"], pw["gdn_norm"], layer, batch, seq, s0=ctx["sg"])
        oc, = _ret_call(proj, pw["ret_norm"], layer, batch, seq, s0=ctx["sr"])
    outs = _out_call(h2d, mod, pw["norm_g"], (oa, ob, oc, od), pw["wg"], pw["wb"], pw["wo"], layer, seq, final_norm)
    return outs, caches


def kernel(x_prompt, x_sample, cache_attn_k, cache_attn_v, cache_na_k, cache_na_v, state_gdn, state_ret, c, c_ctx, w_ada, b_ada, norm_g, w_in, conv_w, gdn_a_log, gdn_dt_bias, gdn_norm, attn_q_norm, attn_k_norm, ret_norm, na_bias, w_branch, w_out, final_norm):
    batch, seq, _ = x_prompt.shape
    dbatch, dseq, _ = x_sample.shape
    assert dbatch == 8, "the modulation kernel handles exactly one sublane tile of conditioning rows"

    wcat, wg = _prep_weights(w_in)
    par = jnp.zeros((DEPTH, 2, 128), F32)
    par = par.at[:, 0, 8:16].set(gdn_dt_bias.reshape(DEPTH, 8)).at[:, 1, 8:16].set(gdn_a_log.reshape(DEPTH, 8))
    pw = dict(
        w_ada=w_ada, b_ada=b_ada.reshape(DEPTH, 1, 3 * D_MODEL), norm_g=norm_g.reshape(DEPTH, 1, D_MODEL),
        wcat=wcat, wg=wg, wb=w_branch.astype(BF16), wo=w_out.astype(BF16),
        cw8=jnp.concatenate([conv_w, jnp.zeros((DEPTH, 8 - SHORT_CONV, 3 * BRANCH_W), F32)], axis=1),
        gdn_par=par,
        gdn_norm=jnp.tile(gdn_norm, (1, N_HEADS)).reshape(DEPTH, 1, BRANCH_W),
        ret_norm=ret_norm.reshape(DEPTH, 1, HEAD_DIM),
        qn=jnp.tile(attn_q_norm, (1, N_HEADS)).reshape(DEPTH, 1, BRANCH_W),
        kn=jnp.tile(attn_k_norm, (1, KV_HEADS)).reshape(DEPTH, 1, KV_HEADS * HEAD_DIM))

    cond = jnp.concatenate([jnp.broadcast_to(c_ctx, (8, D_MODEL)), c], axis=0)
    mods = _mod_call(cond, w_ada, pw["b_ada"])

    h = x_prompt.reshape(batch * seq, D_MODEL)
    caches = None
    for l in range(DEPTH):
        outs, caches = _layer(h, batch, seq, mods[l, 0:1].reshape(1, 1, 3 * D_MODEL), pw, l, None, caches,
                              final_norm if l == DEPTH - 1 else None)
        h = outs[0]
    y_prompt = outs[1].reshape(batch, seq, D_MODEL)
    token_major = lambda a: a.transpose(0, 1, 4, 2, 3)
    (akt, avt), (nkt, nvt), new_state_gdn, new_state_ret = caches
    new_attn_k, new_attn_v, new_na_k, new_na_v = (token_major(a) for a in (akt, avt, nkt, nvt))

    qtab, ktab = _rope_tables(dseq)
    feature_major = lambda a: a.transpose(0, 1, 3, 4, 2)
    ctx = dict(akt=feature_major(cache_attn_k), avt=feature_major(cache_attn_v),
               nkt=feature_major(cache_na_k), nvt=feature_major(cache_na_v),
               sg=state_gdn, sr=state_ret, tb=_na_bias_call(na_bias), qtab=qtab, ktab=ktab)
    h = x_sample.reshape(dbatch * dseq, D_MODEL)
    for l in range(DEPTH):
        outs, _ = _layer(h, dbatch, dseq, mods[l, 8:16].reshape(dbatch, 1, 3 * D_MODEL), pw, l, ctx, None,
                         final_norm if l == DEPTH - 1 else None)
        h = outs[0]
    y_sample = outs[1].reshape(dbatch, dseq, D_MODEL)
    return (y_prompt, y_sample, new_attn_k, new_attn_v, new_na_k, new_na_v, new_state_gdn, new_state_ret)
```

```python
import functools

import numpy as np
import jax
import jax.numpy as jnp
from jax import lax
from jax.experimental import pallas as pl
from jax.experimental.pallas import tpu as pltpu

F32 = jnp.float32
BF16 = jnp.bfloat16

D_MODEL = 1024
HEAD_DIM = 64
N_HEADS = 4
KV_HEADS = N_HEADS // 2
BRANCH_W = N_HEADS * HEAD_DIM
N_BRANCH = 4
DEPTH = 2
GRID_W = 64
CHUNK = 64
PREP_CHUNKS = 4
SCAN_UNROLL = 4
assert CHUNK == HEAD_DIM
SHORT_CONV = 5
NA_ROWS = 8
NA_COLS = 16
RET_ROWS_PER_STEP = 1024
CTX_SEQS_PER_STEP = 4
NA_ROWS_PER_STEP = 4
N_DR = 2 * NA_ROWS - 1
N_DC = 2 * NA_COLS - 1
ROPE_THETA = 10000.0
RET_DECAY_BASE = (5.0, 5.5)
RET_TILE = 256
EPS = 1e-6
SCALE = HEAD_DIM ** -0.5
LOG2E = 1.4426950408889634
NEG_INF = float("-inf")

IN_SPLITS = (256, 128, 128, 256, 768, 16, 256, 256, 256, 256, 256, 256, 256, 256, 256, 4096)
PROJ_W = 4096
OFF_AQKV = 0
OFF_AZ = 512
OFF_GQKV = 768
OFF_GZ = 1536
OFF_C = 1792
OFF_D = 2816
OFF_GAB = 3840

V7X_VMEM_BYTES = 64 * 1024 * 1024
MIB = 1024 * 1024


def _params(vmem_mib, n_axes):
    assert vmem_mib * MIB < V7X_VMEM_BYTES
    return pltpu.CompilerParams(dimension_semantics=("arbitrary",) * n_axes,
                                vmem_limit_bytes=vmem_mib * MIB)


def _layer_spec(block, layer, n_grid):
    zeros = (0,) * len(block)
    if n_grid == 1:
        return pl.BlockSpec((None,) + block, lambda i: (layer,) + zeros)
    return pl.BlockSpec((None,) + block, lambda i, j: (layer,) + zeros)


def _mm(a, b):
    return jnp.dot(a.astype(BF16), b.astype(BF16), preferred_element_type=F32)


def _mm_nt(a, b):
    return lax.dot_general(a.astype(BF16), b.astype(BF16), (((1,), (1,)), ((), ())),
                           preferred_element_type=F32)


def _mm_tn(a, b):
    return lax.dot_general(a.astype(BF16), b.astype(BF16), (((0,), (0,)), ((), ())),
                           preferred_element_type=F32)


def _split3(x):
    hi = x.astype(BF16)
    r = x - hi.astype(F32)
    mid = r.astype(BF16)
    lo = (r - mid.astype(F32)).astype(BF16)
    return hi, mid, lo


def _mm_exact(sel, x, terms=3):
    return sum(jnp.dot(sel, part, preferred_element_type=F32) for part in _split3(x)[:terms])


def _mm_exact_lhs(x, sel, terms=3):
    return sum(jnp.dot(part, sel, preferred_element_type=F32) for part in _split3(x)[:terms])


def _silu(x):
    return x * jax.nn.sigmoid(x)


def _head_block_matrix(width, value):
    ri = lax.broadcasted_iota(jnp.int32, (width, width), 0) >> 6
    ci = lax.broadcasted_iota(jnp.int32, (width, width), 1) >> 6
    return jnp.where(ri == ci, value, 0.0).astype(BF16)


def _head_reduce(x, g):
    hi = x.astype(BF16)
    lo = (x - hi.astype(F32)).astype(BF16)
    return jnp.dot(hi, g, preferred_element_type=F32) + jnp.dot(lo, g, preferred_element_type=F32)


def _head_rms(x):
    ms = _head_reduce(x * x, _head_block_matrix(x.shape[1], 1.0 / HEAD_DIM))
    return x * lax.rsqrt(ms + EPS)


def _rope(x, tab_ref):
    w = x.shape[1]
    return (x * tab_ref[0] + pltpu.roll(x, w - 16, 1) * tab_ref[1] + pltpu.roll(x, 16, 1) * tab_ref[2])


def _attend(qs, parts):
    groups = range(len(qs))
    qs = [(q.astype(F32) * (SCALE * LOG2E)).astype(BF16) for q in qs]

    def score(q, part):
        k, _, bias, feature_major = part
        s = _mm(q, k) if feature_major else _mm_nt(q, k)
        return s if bias is None else s + bias

    scores = [[score(qs[g], part) for part in parts[g]] for g in groups]
    m = [functools.reduce(jnp.maximum, [s.max(axis=-1, keepdims=True) for s in scores[g]]) for g in groups]
    p = [[jnp.exp2(s - m[g]) for s in scores[g]] for g in groups]
    den = [sum(x.sum(axis=-1, keepdims=True) for x in p[g]) for g in groups]
    out = [sum(_mm_nt(x, part[1]) if part[3] else _mm(x, part[1]) for x, part in zip(p[g], parts[g]))
           for g in groups]
    return [out[g] / den[g] for g in groups]


def _hs(h):
    return slice(h * HEAD_DIM, (h + 1) * HEAD_DIM)


def _aligned(x, m):
    return x if isinstance(x, int) else pl.multiple_of(x, m)


def _mod_kernel(c_ref, w_ref, b_ref, o_ref):
    o_ref[...] = _mm(_silu(c_ref[...]), w_ref[...]) + b_ref[...]


def _mod_call(cond, w_ada, b_ada3):
    tn = 512
    rows = cond.shape[0]
    return pl.pallas_call(
        _mod_kernel,
        grid=(DEPTH, 3 * D_MODEL // tn),
        in_specs=[pl.BlockSpec((rows, D_MODEL), lambda l, j: (0, 0)),
                  pl.BlockSpec((None, D_MODEL, tn), lambda l, j: (l, 0, j)),
                  pl.BlockSpec((None, 1, tn), lambda l, j: (l, 0, j))],
        out_specs=pl.BlockSpec((None, rows, tn), lambda l, j: (l, 0, j)),
        out_shape=jax.ShapeDtypeStruct((DEPTH, rows, 3 * D_MODEL), F32),
        compiler_params=_params(24, 2),
        name="adaln_mod",
    )(cond, w_ada, b_ada3)


def _modulated_norm(x, mod, g):
    ms = jnp.mean(x * x, axis=-1, keepdims=True)
    y = x * lax.rsqrt(ms + EPS) * g
    return y * (1.0 + mod[:, D_MODEL:2 * D_MODEL]) + mod[:, :D_MODEL]


def _inproj_kernel(x_ref, mod_ref, g_ref, *refs):
    *w_refs, o_ref, ab_ref = refs
    hn = _modulated_norm(x_ref[...], mod_ref[0], g_ref[...]).astype(BF16)
    tn = 256
    col = 0
    for w_ref in w_refs:
        for j in range(w_ref.shape[1] // tn):
            y = jnp.dot(hn, w_ref[:, j * tn:(j + 1) * tn], preferred_element_type=F32)
            o_ref[:, col:col + tn] = y.astype(BF16)
            if col == OFF_GAB:
                ab_ref[...] = y[:, :128]
            col += tn
    assert col == PROJ_W


def _inproj_call(x2d, mod3, norm_g3, weights, layer, rows_per_mod):
    t = x2d.shape[0]
    tm = 512
    if mod3.shape[0] == 1:
        mod_idx = lambda i: (0, 0, 0)
    else:
        mod_idx = lambda i: ((i * tm) // rows_per_mod, 0, 0)
    w_specs = [pl.BlockSpec((None, D_MODEL, w.shape[2]), lambda i: (layer, 0, 0), pipeline_mode=pl.Buffered(1))
               for w in weights]
    return pl.pallas_call(
        _inproj_kernel,
        grid=(t // tm,),
        in_specs=[pl.BlockSpec((tm, D_MODEL), lambda i: (i, 0)),
                  pl.BlockSpec((1, 1, 3 * D_MODEL), mod_idx),
                  _layer_spec((1, D_MODEL), layer, 1)] + w_specs,
        out_specs=[pl.BlockSpec((tm, PROJ_W), lambda i: (i, 0)), pl.BlockSpec((tm, 128), lambda i: (i, 0))],
        out_shape=[jax.ShapeDtypeStruct((t, PROJ_W), BF16), jax.ShapeDtypeStruct((t, 128), F32)],
        compiler_params=_params(40, 1),
        name="inproj",
    )(x2d, mod3, norm_g3, *weights)


def _stacked_heads(q, n_kv):
    rep = N_HEADS // n_kv
    return [jnp.concatenate([q[:, _hs(g * rep + r)] for r in range(rep)], axis=0) for g in range(n_kv)]


def _unstack_heads(outs, n_kv):
    rep = N_HEADS // n_kv
    m = outs[0].shape[0] // rep
    return jnp.concatenate([outs[g][r * m:(r + 1) * m] for g in range(n_kv) for r in range(rep)], axis=-1)


def _write_layer(ref, layer, value, stacked):
    if not stacked:
        ref[...] = value
        return
    for l in range(ref.shape[0]):
        ref[l] = value if l == layer else jnp.zeros(value.shape, value.dtype)


def _write_state(st_ref, layer, stacked, piece):
    for d in range(2):
        for h in range(N_HEADS):
            value = piece(d, h)
            if stacked:
                for l in range(st_ref.shape[0]):
                    st_ref[l, d, h] = value if l == layer else jnp.zeros(value.shape, value.dtype)
            else:
                st_ref[d, h] = value


def _attn_ctx_kernel(*refs, n_kv, norm, layer, first):
    if norm:
        q_ref, k_ref, v_ref, z_ref, qn_ref, kn_ref = refs[:6]
    else:
        q_ref, k_ref, v_ref, z_ref = refs[:4]
    o_ref, kt_ref, vt_ref = refs[-3:]
    q, k, v, z = q_ref[...], k_ref[...].astype(F32), v_ref[...].astype(F32), z_ref[...].astype(F32)
    if norm:
        q = _head_rms(q.astype(F32)) * qn_ref[...]
        k = _head_rms(k) * kn_ref[...]
    n_seq = kt_ref.shape[0]
    seq = k.shape[0] // n_seq
    qs, parts = [], []
    for s in range(n_seq):
        rows = slice(s * seq, (s + 1) * seq)
        _write_layer(kt_ref.at[s], layer, k[rows].T.reshape(n_kv, HEAD_DIM, seq), first)
        _write_layer(vt_ref.at[s], layer, v[rows].T.reshape(n_kv, HEAD_DIM, seq), first)
        qs += _stacked_heads(q[rows], n_kv)
        parts += [[(k[rows, _hs(g)], v[rows, _hs(g)], None, False)] for g in range(n_kv)]
    outs = _attend(qs, parts)
    o = jnp.concatenate([_unstack_heads(outs[s * n_kv:(s + 1) * n_kv], n_kv) for s in range(n_seq)], axis=0)
    o_ref[...] = (o * _silu(z)).astype(BF16)


def _attn_ctx_call(proj, prev, layer, batch, seq, off_q, off_k, off_v, off_z, n_kv, qn=None, kn=None):
    t = batch * seq
    kvw = n_kv * HEAD_DIM
    norm = qn is not None
    first = prev is None
    n_seq = CTX_SEQS_PER_STEP
    assert batch % n_seq == 0
    tm = n_seq * seq
    in_specs = [pl.BlockSpec((tm, BRANCH_W), lambda b: (b, off_q // BRANCH_W)),
                pl.BlockSpec((tm, kvw), lambda b: (b, off_k // kvw)),
                pl.BlockSpec((tm, kvw), lambda b: (b, off_v // kvw)),
                pl.BlockSpec((tm, BRANCH_W), lambda b: (b, off_z // BRANCH_W))]
    args = [proj, proj, proj, proj]
    if norm:
        in_specs += [_layer_spec((1, BRANCH_W), layer, 1), _layer_spec((1, kvw), layer, 1)]
        args += [qn, kn]
    aliases = {}
    if first:
        cache_spec = pl.BlockSpec((n_seq, DEPTH, n_kv, HEAD_DIM, seq), lambda b: (b, 0, 0, 0, 0))
    else:
        aliases = {len(args): 1, len(args) + 1: 2}
        in_specs += [pl.BlockSpec(memory_space=pl.ANY)] * 2
        args += list(prev)
        cache_spec = pl.BlockSpec((n_seq, None, n_kv, HEAD_DIM, seq), lambda b: (b, layer, 0, 0, 0))
    cache_shape = jax.ShapeDtypeStruct((batch, DEPTH, n_kv, HEAD_DIM, seq), F32)
    return pl.pallas_call(
        functools.partial(_attn_ctx_kernel, n_kv=n_kv, norm=norm, layer=layer, first=first),
        grid=(batch // n_seq,), in_specs=in_specs,
        out_specs=[pl.BlockSpec((tm, BRANCH_W), lambda b: (b, 0)), cache_spec, cache_spec],
        out_shape=[jax.ShapeDtypeStruct((t, BRANCH_W), BF16), cache_shape, cache_shape],
        input_output_aliases=aliases,
        compiler_params=_params(32, 1),
        name="attn_ctx_norm" if norm else "attn_ctx",
    )(*args)


def _attn_lat_kernel(q_ref, kv_ref, z_ref, ckt_ref, cvt_ref, qtab_ref, ktab_ref, qn_ref, kn_ref, o_ref,
                     k_s, v_s):
    kw = KV_HEADS * HEAD_DIM

    @pl.when(pl.program_id(1) == 0)
    def _():
        kv = kv_ref[...]
        k_s[...] = _rope(_head_rms(kv[:, :kw].astype(F32)) * kn_ref[...], ktab_ref).astype(BF16)
        v_s[...] = kv[:, kw:]

    q = _rope(_head_rms(q_ref[...].astype(F32)) * qn_ref[...], qtab_ref)
    k, v = k_s[...], v_s[...]
    outs = _attend(_stacked_heads(q, KV_HEADS),
                   [[(k[:, _hs(g)], v[:, _hs(g)], None, False), (ckt_ref[g], cvt_ref[g], None, True)]
                    for g in range(KV_HEADS)])
    o_ref[...] = (_unstack_heads(outs, KV_HEADS) * _silu(z_ref[...].astype(F32))).astype(BF16)


def _attn_lat_call(proj, cache_kt, cache_vt, layer, batch, seq, qtab, ktab, qn, kn):
    tq = 512
    nq = seq // tq
    past = cache_kt.shape[-1]
    kw = KV_HEADS * HEAD_DIM
    ctx_spec = pl.BlockSpec((None, None, KV_HEADS, HEAD_DIM, past), lambda b, i: (b, layer, 0, 0, 0))
    return pl.pallas_call(
        _attn_lat_kernel,
        grid=(batch, nq),
        in_specs=[pl.BlockSpec((tq, BRANCH_W), lambda b, i: (b * nq + i, OFF_AQKV // BRANCH_W)),
                  pl.BlockSpec((seq, 2 * kw), lambda b, i: (b, (OFF_AQKV + BRANCH_W) // (2 * kw))),
                  pl.BlockSpec((tq, BRANCH_W), lambda b, i: (b * nq + i, OFF_AZ // BRANCH_W)),
                  ctx_spec, ctx_spec,
                  pl.BlockSpec((3, tq, BRANCH_W), lambda b, i: (0, i, 0)),
                  pl.BlockSpec((3, seq, kw), lambda b, i: (0, 0, 0)),
                  _layer_spec((1, BRANCH_W), layer, 2),
                  _layer_spec((1, kw), layer, 2)],
        out_specs=pl.BlockSpec((tq, BRANCH_W), lambda b, i: (b * nq + i, 0)),
        out_shape=jax.ShapeDtypeStruct((batch * seq, BRANCH_W), BF16),
        scratch_shapes=[pltpu.VMEM((seq, kw), BF16), pltpu.VMEM((seq, kw), BF16)],
        compiler_params=_params(40, 2),
        name="attn_lat",
    )(proj, proj, proj, cache_kt, cache_vt, qtab, ktab, qn, kn)


def _na_bias_kernel(t_ref, o_ref):
    nblk = o_ref.shape[0]
    c = lax.broadcasted_iota(jnp.int32, (GRID_W, 2 * GRID_W), 0)
    kc = lax.broadcasted_iota(jnp.int32, (GRID_W, 2 * GRID_W), 1) & (GRID_W - 1)
    cs = jnp.clip(c - NA_COLS // 2, 0, GRID_W - NA_COLS)
    valid = jnp.logical_and(kc >= cs, kc < cs + NA_COLS)

    unroll = 8
    assert nblk % unroll == 0

    def body(i, carry):
        rows8 = t_ref[pl.ds(pl.multiple_of(i * unroll, unroll), unroll), :]
        for u in range(unroll):
            row = jnp.broadcast_to(rows8[u:u + 1, :], (GRID_W, 2 * GRID_W))
            skewed = pltpu.roll(row, 2 * GRID_W - (NA_COLS - 1), 1, stride=1, stride_axis=0)
            o_ref[i * unroll + u] = jnp.where(valid, skewed * LOG2E, NEG_INF)
        return carry

    lax.fori_loop(0, nblk // unroll, body, 0)


def _na_bias_call(na_bias):
    nblk = DEPTH * N_HEADS * N_DR
    rows = jnp.pad(na_bias.reshape(nblk, N_DC), ((0, 1), (0, GRID_W - N_DC)))
    pairs = jnp.concatenate([rows[:-1], rows[1:]], axis=1)
    return pl.pallas_call(
        _na_bias_kernel,
        in_specs=[pl.BlockSpec((nblk, 2 * GRID_W), lambda: (0, 0))],
        out_specs=pl.BlockSpec((nblk, GRID_W, 2 * GRID_W), lambda: (0, 0, 0)),
        out_shape=jax.ShapeDtypeStruct((nblk, GRID_W, 2 * GRID_W), F32),
        name="na_bias",
    )(pairs)


def _na_kernel(q_ref, k_ref, v_ref, z_ref, ckt_ref, cvt_ref, tb_ref, o_ref, kh_s, vh_s, *, rows):
    win = NA_ROWS * GRID_W

    @pl.when(pl.program_id(1) == 0)
    def _():
        for h in range(N_HEADS):
            kh_s[h] = k_ref[:, _hs(h)]
            vh_s[h] = v_ref[:, _hs(h)]

    qs, parts = [], []
    for i in range(NA_ROWS_PER_STEP):
        r = pl.program_id(1) * NA_ROWS_PER_STEP + i
        rs = jnp.clip(r - NA_ROWS // 2, 0, rows - NA_ROWS)
        r0 = pl.multiple_of(rs * GRID_W, GRID_W)
        q = q_ref[i * GRID_W:(i + 1) * GRID_W, :]
        dr0 = rs - r + NA_ROWS - 1
        for h in range(N_HEADS):
            bias = jnp.concatenate([tb_ref[h * N_DR + dr0 + 2 * p] for p in range(NA_ROWS // 2)], axis=1)
            qs.append(q[:, _hs(h)])
            parts.append([(kh_s[h, pl.ds(r0, win), :], vh_s[h, pl.ds(r0, win), :], bias, False),
                          (ckt_ref[h], cvt_ref[h], None, True)])
    outs = _attend(qs, parts)
    o = jnp.concatenate([jnp.concatenate(outs[i * N_HEADS:(i + 1) * N_HEADS], axis=-1)
                         for i in range(NA_ROWS_PER_STEP)], axis=0)
    o_ref[...] = (o * _silu(z_ref[...].astype(F32))).astype(BF16)


def _na_call(proj, cache_kt, cache_vt, tb, layer, batch, seq):
    rows = seq // GRID_W
    assert rows >= NA_ROWS and rows % NA_ROWS_PER_STEP == 0
    steps = rows // NA_ROWS_PER_STEP
    tq = NA_ROWS_PER_STEP * GRID_W
    past = cache_kt.shape[-1]
    nblk = N_HEADS * N_DR
    cq = OFF_D // BRANCH_W
    ctx_spec = pl.BlockSpec((None, None, N_HEADS, HEAD_DIM, past), lambda b, r: (b, layer, 0, 0, 0))
    return pl.pallas_call(
        functools.partial(_na_kernel, rows=rows),
        grid=(batch, steps),
        in_specs=[pl.BlockSpec((tq, BRANCH_W), lambda b, r: (b * steps + r, cq)),
                  pl.BlockSpec((seq, BRANCH_W), lambda b, r: (b, cq + 1)),
                  pl.BlockSpec((seq, BRANCH_W), lambda b, r: (b, cq + 2)),
                  pl.BlockSpec((tq, BRANCH_W), lambda b, r: (b * steps + r, cq + 3)),
                  ctx_spec, ctx_spec,
                  pl.BlockSpec((nblk, GRID_W, 2 * GRID_W), lambda b, r: (layer, 0, 0))],
        out_specs=pl.BlockSpec((tq, BRANCH_W), lambda b, r: (b * steps + r, 0)),
        out_shape=jax.ShapeDtypeStruct((batch * seq, BRANCH_W), BF16),
        scratch_shapes=[pltpu.VMEM((N_HEADS, seq, HEAD_DIM), BF16), pltpu.VMEM((N_HEADS, seq, HEAD_DIM), BF16)],
        compiler_params=_params(32, 2),
        name="na_lat",
    )(proj, proj, proj, proj, cache_kt, cache_vt, tb)


def _gdn_kernel(*refs, seq, has_s0, layer, emit):
    qkv_ref, z_ref, ab_ref, cw_ref, par_ref, g_ref = refs[:6]
    s0_ref = refs[6] if has_s0 else None
    (q_s, k_s, v_s, gcb_s, bcb_s, r_s, mc_s, nc_s, qp_s, op_s, egl_s, s_s, oacc_ref) = refs[-13:]
    if emit == "none":
        o_ref, st_ref = refs[-14], None
    else:
        o_ref, st_ref = refs[-15], refs[-14]
    n_chunks = seq // CHUNK
    n_levels = CHUNK.bit_length() - 1
    qkv_w = 3 * BRANCH_W
    half = SHORT_CONV // 2
    pair_w = 2 * HEAD_DIM
    pairs = [slice(p * pair_w, (p + 1) * pair_w) for p in range(BRANCH_W // pair_w)]
    tr = 256
    cpt = tr // CHUNK
    head_sum = _head_block_matrix(BRANCH_W, 1.0)

    gc_i = lax.broadcasted_iota(jnp.int32, (128, BRANCH_W), 0)
    gh_j = lax.broadcasted_iota(jnp.int32, (128, BRANCH_W), 1) >> 6
    sel_beta = [jnp.where(gc_i == gh_j + 4 * d, 1.0, 0.0).astype(BF16) for d in range(2)]
    sel_gate = [jnp.where(gc_i == gh_j + 8 + 4 * d, 1.0, 0.0).astype(BF16) for d in range(2)]
    ti = lax.broadcasted_iota(jnp.int32, (tr, tr), 0)
    tj = lax.broadcasted_iota(jnp.int32, (tr, tr), 1)
    same_chunk = (ti >> 6) == (tj >> 6)
    tri = [jnp.where(jnp.logical_and(same_chunk, ti >= tj), 1.0, 0.0).astype(BF16),
           jnp.where(jnp.logical_and(same_chunk, ti <= tj), 1.0, 0.0).astype(BF16)]
    assert tr == BRANCH_W
    lane_head = lax.broadcasted_iota(jnp.int32, (1, BRANCH_W), 1) >> 6

    halo = 16
    edge = 8
    assert half <= edge
    si = lax.broadcasted_iota(jnp.int32, (tr, tr), 0)
    sj = lax.broadcasted_iota(jnp.int32, (tr, tr), 1)
    ei = lax.broadcasted_iota(jnp.int32, (edge, halo), 0)
    ej = lax.broadcasted_iota(jnp.int32, (edge, halo), 1)
    taps = [j for j in range(SHORT_CONV) if j != half]
    shift = {j: jnp.where(sj == si + (j - half), 1.0, 0.0).astype(BF16) for j in taps}
    shift_before = {j: jnp.where(ej == ei + (halo + j - half), 1.0, 0.0).astype(BF16) for j in taps if j < half}
    shift_after = {j: jnp.where(ej == ei + (j - half - edge), 1.0, 0.0).astype(BF16) for j in taps if j > half}
    for t in range(seq // tr):
        rows = slice(t * tr, (t + 1) * tr)
        x = qkv_ref[rows, :]
        y = x.astype(F32) * cw_ref[half:half + 1, :]
        for j in taps:
            y = y + jnp.dot(shift[j], x, preferred_element_type=F32) * cw_ref[j:j + 1, :]
        if t > 0:
            before = qkv_ref[t * tr - halo:t * tr, :]
            top = sum(jnp.dot(shift_before[j], before, preferred_element_type=F32) * cw_ref[j:j + 1, :]
                      for j in shift_before)
            y = jnp.concatenate([y[:edge] + top, y[edge:]], axis=0)
        if (t + 1) * tr < seq:
            after = qkv_ref[(t + 1) * tr:(t + 1) * tr + halo, :]
            bottom = sum(jnp.dot(shift_after[j], after, preferred_element_type=F32) * cw_ref[j:j + 1, :]
                         for j in shift_after)
            y = jnp.concatenate([y[:tr - edge], y[tr - edge:] + bottom], axis=0)
        y = _silu(y)
        qq, kk = y[:, :BRANCH_W], y[:, BRANCH_W:2 * BRANCH_W]
        q_s[rows, :] = qq * lax.rsqrt(_head_reduce(qq * qq, head_sum) + EPS) * SCALE
        k_s[rows, :] = kk * lax.rsqrt(_head_reduce(kk * kk, head_sum) + EPS)
        v_s[rows, :] = y[:, 2 * BRANCH_W:]
        x = ab_ref[rows, :]
        beta = jax.nn.sigmoid(x)
        xs = x + par_ref[0:1, :]
        softplus = jnp.maximum(xs, 0.0) + jnp.log1p(jnp.exp(-jnp.abs(xs)))
        la = -jnp.exp(par_ref[1:2, :]) * softplus
        for d in range(2):
            gc = _mm_exact(tri[d], la, terms=2)
            gcb_s[d, rows, :] = _mm_exact_lhs(gc, sel_gate[d], terms=2)
            bcb_s[d, rows, :] = _mm_exact_lhs(beta, sel_beta[d], terms=2)
            gt = gc.T[8:16, :]
            shifted = {s: (gt if s == 0 else pltpu.roll(gt, (s * HEAD_DIM) % tr, 1))
                       for s in range(1 - cpt, N_HEADS)}
            for c in range(cpt):
                r = jnp.zeros((1, BRANCH_W), F32)
                for h in range(N_HEADS):
                    r = jnp.where(lane_head == h, shifted[h - c][4 * d + h:4 * d + h + 1, :], r)
                r_s[d, (t * cpt + c) * 8:(t * cpt + c + 1) * 8, :] = jnp.broadcast_to(r, (8, BRANCH_W))

    for d in range(2):
        if has_s0:
            s_s[d] = jnp.concatenate([s0_ref[d, h] for h in range(N_HEADS)], axis=-1)
        else:
            s_s[d] = jnp.zeros((HEAD_DIM, BRANCH_W), F32)

    li = lax.broadcasted_iota(jnp.int32, (CHUNK, BRANCH_W), 0)
    lj = lax.broadcasted_iota(jnp.int32, (CHUNK, BRANCH_W), 1) & (HEAD_DIM - 1)
    incl = (li >= lj, li <= lj)
    strict = (li > lj, li < lj)
    level = [((li ^ lj) >> l) == 1 for l in range(n_levels)]
    first_head = lax.broadcasted_iota(jnp.int32, (CHUNK, pair_w), 1) < HEAD_DIM

    def expand(y):
        yb = y.astype(BF16)
        zero = jnp.zeros((CHUNK, pair_w), BF16)
        return [jnp.concatenate([jnp.where(first_head, yb[:, p], zero), jnp.where(first_head, zero, yb[:, p])],
                                axis=0) for p in pairs]

    def bdmm(x, ybd):
        xb = x.astype(BF16)
        return jnp.concatenate([jnp.dot(xb[:, p], ybd[i], preferred_element_type=F32)
                                for i, p in enumerate(pairs)], axis=1)

    def bdmm_nt(x, ybd):
        xb = x.astype(BF16)
        return jnp.concatenate([lax.dot_general(xb[:, p], ybd[i], (((1,), (1,)), ((), ())),
                                                preferred_element_type=F32)
                                for i, p in enumerate(pairs)], axis=1)

    def bdmm2(x, y1, y2):
        xb = x.astype(BF16)
        e1, e2 = expand(y1), expand(y2)
        outs = [jnp.dot(xb[:, p], jnp.concatenate([e1[i], e2[i]], axis=1), preferred_element_type=F32)
                for i, p in enumerate(pairs)]
        return (jnp.concatenate([o[:, :pair_w] for o in outs], axis=1),
                jnp.concatenate([o[:, pair_w:] for o in outs], axis=1))

    def tn_diag2(a, b1, b2):
        ab, b1b, b2b = a.astype(BF16), b1.astype(BF16), b2.astype(BF16)
        outs1, outs2 = [], []
        for p in pairs:
            full = lax.dot_general(ab[:, p], jnp.concatenate([b1b[:, p], b2b[:, p]], axis=1),
                                   (((0,), (0,)), ((), ())), preferred_element_type=F32)
            outs1.append(jnp.where(first_head, full[:HEAD_DIM, :pair_w], full[HEAD_DIM:, :pair_w]))
            outs2.append(jnp.where(first_head, full[:HEAD_DIM, pair_w:], full[HEAD_DIM:, pair_w:]))
        return jnp.concatenate(outs1, axis=1), jnp.concatenate(outs2, axis=1)

    def prepare(chains):
        n = range(len(chains))
        dd = [d for d, _ in chains]
        rows = [pl.ds(_aligned(c * CHUNK, CHUNK), CHUNK) for _, c in chains]
        gcb = [gcb_s[dd[i], rows[i], :] for i in n]
        bcb = [bcb_s[dd[i], rows[i], :] for i in n]
        grow = [r_s[dd[i], pl.ds(_aligned(chains[i][1] * 8, 8), 8), :][0:1, :] for i in n]
        dm = [jnp.exp(jnp.where(incl[dd[i]], gcb[i] - grow[i], NEG_INF)) for i in n]
        k = [k_s[rows[i], :] for i in n]
        q = [q_s[rows[i], :] for i in n]
        v = [v_s[rows[i], :] for i in n]
        kq = [bdmm_nt(jnp.concatenate([k[i], q[i]], axis=0), expand(k[i])) for i in n]
        a = [jnp.where(strict[dd[i]], bcb[i] * kq[i][:CHUNK] * dm[i], 0.0) for i in n]
        tm = [-jnp.where(level[0], a[i], 0.0) for i in n]
        for l in range(1, n_levels):
            b = [jnp.where(level[l], a[i], 0.0) for i in n]
            y = [b[i] + bdmm(tm[i], expand(b[i])) for i in n]
            tm = [tm[i] - (y[i] + bdmm(y[i], expand(tm[i]))) for i in n]
        eg = [jnp.exp(gcb[i]) for i in n]
        bv = [bcb[i] * v[i] for i in n]
        bk = [bcb[i] * k[i] * eg[i] for i in n]
        tuw = [bdmm2(tm[i], bv[i], bk[i]) for i in n]
        u = [bv[i] + tuw[i][0] for i in n]
        w = [bk[i] + tuw[i][1] for i in n]
        gl = [gcb[i][CHUNK - 1:CHUNK, :] if dd[i] == 0 else gcb[i][0:1, :] for i in n]
        kd = [k[i] * jnp.exp(gl[i] - gcb[i]) for i in n]
        qkm = [kq[i][CHUNK:] * dm[i] for i in n]
        mnc = [tn_diag2(kd[i], w[i], u[i]) for i in n]
        mc = [mnc[i][0] for i in n]
        nc = [mnc[i][1] for i in n]
        qwu = [bdmm2(qkm[i], w[i], u[i]) for i in n]
        qp = [q[i] * eg[i] - qwu[i][0] for i in n]
        op = [qwu[i][1] for i in n]
        for i in n:
            d, c = chains[i]
            mc_s[d, rows[i], :] = mc[i].astype(BF16)
            nc_s[d, rows[i], :] = nc[i]
            qp_s[d, rows[i], :] = qp[i].astype(BF16)
            op_s[d, rows[i], :] = op[i]
            egl_s[d, pl.ds(_aligned(c * 8, 8), 8), :] = jnp.broadcast_to(jnp.exp(gl[i]), (8, BRANCH_W))

    group = min(PREP_CHUNKS, n_chunks)
    if n_chunks == group:
        prepare([(d, c) for c in range(group) for d in range(2)])
    else:
        def prep_body(j, carry):
            prepare([(d, j * group + c) for c in range(group) for d in range(2)])
            return carry
        lax.fori_loop(0, n_chunks // group, prep_body, 0)

    def scan_step(i):
        for d, c in ((0, i), (1, n_chunks - 1 - i)):
            rows = pl.ds(_aligned(c * CHUNK, CHUNK), CHUNK)
            s = s_s[d]
            sbd = expand(s)
            oacc_ref[d, rows, :] = bdmm(qp_s[d, rows, :], sbd) + op_s[d, rows, :]
            egl = egl_s[d, pl.ds(_aligned(c * 8, 8), 8), :][0:1, :]
            s_s[d] = s * egl - bdmm(mc_s[d, rows, :], sbd) + nc_s[d, rows, :]

    unroll = min(SCAN_UNROLL, n_chunks)

    def scan_body(j, carry):
        for i in range(unroll):
            scan_step(j * unroll + i)
        return carry

    lax.fori_loop(0, n_chunks // unroll, scan_body, 0)

    if st_ref is not None:
        _write_state(st_ref, layer, emit == "first", lambda d, h: s_s[d][:, _hs(h)])
    o = oacc_ref[0] + oacc_ref[1]
    ms = _head_reduce(o * o, _head_block_matrix(BRANCH_W, 1.0 / HEAD_DIM))
    o_ref[...] = (o * lax.rsqrt(ms + EPS) * g_ref[...] * _silu(z_ref[...].astype(F32))).astype(BF16)


def _state_spec(layer, n_seq=None):
    return pl.BlockSpec((n_seq, None, 2, N_HEADS, HEAD_DIM, HEAD_DIM), lambda b: (b, layer, 0, 0, 0, 0))


def _state_output(emit, prev, layer, batch, n_args, n_seq=None):
    if emit == "none":
        return [], [], [], [], {}
    shape = jax.ShapeDtypeStruct((batch, DEPTH, 2, N_HEADS, HEAD_DIM, HEAD_DIM), F32)
    if emit == "first":
        spec = pl.BlockSpec((n_seq, DEPTH, 2, N_HEADS, HEAD_DIM, HEAD_DIM), lambda b: (b, 0, 0, 0, 0, 0))
        return [], [], [spec], [shape], {}
    return [pl.BlockSpec(memory_space=pl.ANY)], [prev], [_state_spec(layer, n_seq)], [shape], {n_args: 1}


def _gdn_call(proj, gab, cw8, par, norm_g, layer, batch, seq, s0=None, emit="none", prev=None):
    has_s0 = s0 is not None
    qkv_w = 3 * BRANCH_W
    in_specs = [pl.BlockSpec((seq, qkv_w), lambda b: (b, OFF_GQKV // qkv_w)),
                pl.BlockSpec((seq, BRANCH_W), lambda b: (b, OFF_GZ // BRANCH_W)),
                pl.BlockSpec((seq, 128), lambda b: (b, 0)),
                _layer_spec((8, qkv_w), layer, 1),
                _layer_spec((2, 128), layer, 1),
                _layer_spec((1, BRANCH_W), layer, 1)]
    args = [proj, proj, gab, cw8, par, norm_g]
    if has_s0:
        in_specs.append(_state_spec(layer))
        args.append(s0)
    st_in_specs, st_args, st_out_specs, st_shapes, aliases = _state_output(emit, prev, layer, batch, len(args))
    return pl.pallas_call(
        functools.partial(_gdn_kernel, seq=seq, has_s0=has_s0, layer=layer, emit=emit),
        grid=(batch,), in_specs=in_specs + st_in_specs,
        out_specs=[pl.BlockSpec((seq, BRANCH_W), lambda b: (b, 0))] + st_out_specs,
        out_shape=[jax.ShapeDtypeStruct((batch * seq, BRANCH_W), BF16)] + st_shapes,
        input_output_aliases=aliases,
        scratch_shapes=[pltpu.VMEM((seq, BRANCH_W), F32),
                        pltpu.VMEM((seq, BRANCH_W), F32),
                        pltpu.VMEM((seq, BRANCH_W), F32),
                        pltpu.VMEM((2, seq, BRANCH_W), F32),
                        pltpu.VMEM((2, seq, BRANCH_W), F32),
                        pltpu.VMEM((2, seq // CHUNK * 8, BRANCH_W), F32),
                        pltpu.VMEM((2, seq, BRANCH_W), BF16),
                        pltpu.VMEM((2, seq, BRANCH_W), F32),
                        pltpu.VMEM((2, seq, BRANCH_W), BF16),
                        pltpu.VMEM((2, seq, BRANCH_W), F32),
                        pltpu.VMEM((2, seq // CHUNK * 8, BRANCH_W), F32),
                        pltpu.VMEM((2, HEAD_DIM, BRANCH_W), F32),
                        pltpu.VMEM((2, seq, BRANCH_W), F32)],
        compiler_params=_params(48, 1),
        name="gdn",
    )(*args, *st_args)


_RET_LOG_GAMMA = [[float(np.log1p(-np.exp2(-(base + h)))) for h in range(N_HEADS)] for base in RET_DECAY_BASE]


def _ret_kernel(*refs, seq, n_seq, has_s0, layer, emit):
    qkv_refs, z_ref, g_ref = refs[:3], refs[3], refs[4]
    s0_ref = refs[5] if has_s0 else None
    o_ref, st_ref = (refs[-1], None) if emit == "none" else (refs[-2], refs[-1])
    tile = RET_TILE
    n_tiles = seq // tile
    problems = [(s, h) for s in range(n_seq) for h in range(N_HEADS)]
    heads = range(len(problems))
    lgf = [_RET_LOG_GAMMA[0][h] for _, h in problems]
    lgb = [_RET_LOG_GAMMA[1][h] for _, h in problems]
    a = lax.broadcasted_iota(jnp.int32, (tile, 1), 0).astype(F32)
    ef = [jnp.exp(a * lgf[h]) for h in heads]
    eif = [jnp.exp(-a * lgf[h]) for h in heads]
    eb = [jnp.exp(a * lgb[h]) for h in heads]
    eib = [jnp.exp(-a * lgb[h]) for h in heads]
    gf_tile = [float(np.exp(tile * lgf[h])) for h in heads]
    gb_tile = [float(np.exp(tile * lgb[h])) for h in heads]
    ii = lax.broadcasted_iota(jnp.int32, (tile, tile), 0)
    jj = lax.broadcasted_iota(jnp.int32, (tile, tile), 1)

    def head_cols(t, part, i):
        s, h = problems[i]
        return qkv_refs[part][s * seq + t * tile:s * seq + (t + 1) * tile, _hs(h)]

    def initial_state(d, i):
        s, h = problems[i]
        return s0_ref[s, d, h]

    kf = [[head_cols(t, 1, h) * eif[h] for h in heads] for t in range(n_tiles)]
    kb = [[head_cols(t, 1, h) * eb[h] for h in heads] for t in range(n_tiles)]
    vs = [[head_cols(t, 2, h) for h in heads] for t in range(n_tiles)]
    use_states = has_s0 or n_tiles > 1 or st_ref is not None
    if use_states:
        kvf = [[_mm_tn(kf[t][h], vs[t][h]) for h in heads] for t in range(n_tiles)]
        kvb = [[_mm_tn(kb[t][h], vs[t][h]) for h in heads] for t in range(n_tiles)]
        zero = jnp.zeros((HEAD_DIM, HEAD_DIM), F32)
        zf = [[(float(np.exp(lgf[h])) * initial_state(0, h)) if has_s0 else zero for h in heads]]
        for t in range(n_tiles):
            zf.append([gf_tile[h] * (zf[t][h] + kvf[t][h]) for h in heads])
        acc = [initial_state(1, h) if has_s0 else zero for h in heads]
        zb = [None] * n_tiles
        for t in reversed(range(n_tiles)):
            zb[t] = [gb_tile[h] * acc[h] for h in heads]
            acc = [zb[t][h] + kvb[t][h] for h in heads]
        if st_ref is not None:
            stf = [zf[n_tiles][h] * float(np.exp(-lgf[h])) for h in heads]
            for s in range(n_seq):
                _write_state(st_ref.at[s], layer, emit == "first",
                             lambda d, h, s=s: (stf, acc)[d][s * N_HEADS + h])
    tiles = [[] for _ in range(n_seq)]
    for t in range(n_tiles):
        q = [head_cols(t, 0, h) * SCALE for h in heads]
        qf = [q[h] * ef[h] for h in heads]
        qb = [q[h] * eib[h] for h in heads]
        sd = [jnp.where(ii >= jj, _mm_nt(qf[h], kf[t][h]), 0.0) + jnp.where(ii <= jj, _mm_nt(qb[h], kb[t][h]), 0.0)
              for h in heads]
        o = [_mm(sd[h], vs[t][h]) for h in heads]
        if has_s0 or n_tiles > 1:
            o = [o[h] + _mm(qf[h], zf[t][h]) + _mm(qb[h], zb[t][h]) for h in heads]
        ms = [jnp.mean(o[h] * o[h], axis=-1, keepdims=True) for h in heads]
        for s in range(n_seq):
            tiles[s].append(jnp.concatenate([o[h] * lax.rsqrt(ms[h] + EPS) * g_ref[...]
                                             for h in range(s * N_HEADS, (s + 1) * N_HEADS)], axis=-1))
    o = jnp.concatenate([tile_out for s in range(n_seq) for tile_out in tiles[s]], axis=0)
    o_ref[...] = (o * _silu(z_ref[...].astype(F32))).astype(BF16)


def _ret_call(proj, norm_g, layer, batch, seq, s0=None, emit="none", prev=None):
    has_s0 = s0 is not None
    n_seq = max(1, RET_ROWS_PER_STEP // seq)
    assert batch % n_seq == 0
    tm = n_seq * seq
    cq = OFF_C // BRANCH_W
    in_specs = [pl.BlockSpec((tm, BRANCH_W), lambda b, part=part: (b, cq + part)) for part in range(4)]
    in_specs.append(_layer_spec((1, HEAD_DIM), layer, 1))
    args = [proj, proj, proj, proj, norm_g]
    if has_s0:
        in_specs.append(_state_spec(layer, n_seq))
        args.append(s0)
    st_in_specs, st_args, st_out_specs, st_shapes, aliases = _state_output(emit, prev, layer, batch, len(args),
                                                                           n_seq)
    return pl.pallas_call(
        functools.partial(_ret_kernel, seq=seq, n_seq=n_seq, has_s0=has_s0, layer=layer, emit=emit),
        grid=(batch // n_seq,), in_specs=in_specs + st_in_specs,
        out_specs=[pl.BlockSpec((tm, BRANCH_W), lambda b: (b, 0))] + st_out_specs,
        out_shape=[jax.ShapeDtypeStruct((batch * seq, BRANCH_W), BF16)] + st_shapes,
        input_output_aliases=aliases,
        compiler_params=_params(48, 1),
        name="retention",
    )(*args, *st_args)


def _out_kernel(*refs, final):
    if final:
        (h_ref, mod_ref, g_ref, oa_ref, ob_ref, oc_ref, od_ref, wg_ref, wb_ref, wo_ref, fn_ref,
         o_ref, y_ref) = refs
    else:
        h_ref, mod_ref, g_ref, oa_ref, ob_ref, oc_ref, od_ref, wg_ref, wb_ref, wo_ref, o_ref = refs
    x = h_ref[...]
    mod = mod_ref[0]
    hn = _modulated_norm(x, mod, g_ref[...]).astype(BF16)
    merged = None
    for n, br_ref in enumerate((oa_ref, ob_ref, oc_ref, od_ref)):
        gate = jax.nn.sigmoid(jnp.dot(hn, wg_ref[:, n * D_MODEL:(n + 1) * D_MODEL], preferred_element_type=F32))
        up = jnp.dot(br_ref[...], wb_ref[n], preferred_element_type=F32)
        merged = gate * up if merged is None else merged + gate * up
    out = jnp.dot(merged.astype(BF16), wo_ref[...], preferred_element_type=F32)
    hnew = x + mod[:, 2 * D_MODEL:] * out
    o_ref[...] = hnew
    if final:
        ms = jnp.mean(hnew * hnew, axis=-1, keepdims=True)
        y_ref[...] = hnew * lax.rsqrt(ms + EPS) * fn_ref[...]


def _out_call(h2d, mod3, norm_g3, branches, wg, wb, wo, layer, rows_per_mod, final_norm=None):
    t = h2d.shape[0]
    tm = 512
    final = final_norm is not None
    if mod3.shape[0] == 1:
        mod_idx = lambda i: (0, 0, 0)
    else:
        mod_idx = lambda i: ((i * tm) // rows_per_mod, 0, 0)
    once = pl.Buffered(1)
    in_specs = [pl.BlockSpec((tm, D_MODEL), lambda i: (i, 0)),
                pl.BlockSpec((1, 1, 3 * D_MODEL), mod_idx),
                _layer_spec((1, D_MODEL), layer, 1)]
    in_specs += [pl.BlockSpec((tm, BRANCH_W), lambda i: (i, 0))] * N_BRANCH
    in_specs += [pl.BlockSpec((None, D_MODEL, N_BRANCH * D_MODEL), lambda i: (layer, 0, 0), pipeline_mode=once),
                 pl.BlockSpec((None, N_BRANCH, BRANCH_W, D_MODEL), lambda i: (layer, 0, 0, 0), pipeline_mode=once),
                 pl.BlockSpec((None, D_MODEL, D_MODEL), lambda i: (layer, 0, 0), pipeline_mode=once)]
    args = [h2d, mod3, norm_g3, *branches, wg, wb, wo]
    out_specs = [pl.BlockSpec((tm, D_MODEL), lambda i: (i, 0))]
    out_shape = [jax.ShapeDtypeStruct((t, D_MODEL), F32)]
    if final:
        in_specs.append(pl.BlockSpec((1, D_MODEL), lambda i: (0, 0)))
        args.append(final_norm.reshape(1, D_MODEL))
        out_specs.append(pl.BlockSpec((tm, D_MODEL), lambda i: (i, 0)))
        out_shape.append(jax.ShapeDtypeStruct((t, D_MODEL), F32))
    return pl.pallas_call(
        functools.partial(_out_kernel, final=final),
        grid=(t // tm,), in_specs=in_specs, out_specs=out_specs, out_shape=out_shape,
        compiler_params=_params(48, 1),
        name="merge_out_final" if final else "merge_out",
    )(*args)


def _prep_weights(w_in):
    offs = np.concatenate([[0], np.cumsum(IN_SPLITS)])
    seg = lambda i, j: w_in[:, :, offs[i]:offs[j]]
    assert offs[5] == OFF_GZ and offs[15] - offs[6] == OFF_GAB - OFF_GZ
    pad = jnp.zeros((DEPTH, D_MODEL, PROJ_W - OFF_GAB - IN_SPLITS[5]), w_in.dtype)
    w_proj = (seg(0, 5).astype(BF16),
              seg(6, 15).astype(BF16),
              jnp.concatenate([seg(5, 6), pad], axis=2).astype(BF16))
    return w_proj, seg(15, 16).astype(BF16)


def _rope_tables(seq):
    t = jnp.arange(seq)
    quarter = HEAD_DIM // 4
    inv = ROPE_THETA ** (-jnp.arange(quarter, dtype=F32) / quarter)

    def half(pos):
        ang = pos.astype(F32)[:, None] * inv
        c, s, zero = jnp.cos(ang), jnp.sin(ang), jnp.zeros_like(ang)
        return jnp.concatenate([c, c], -1), jnp.concatenate([-s, zero], -1), jnp.concatenate([zero, s], -1)

    parts = [jnp.concatenate([a, b], -1) for a, b in zip(half(t // GRID_W), half(t % GRID_W))]
    tab = jnp.stack(parts)
    return jnp.tile(tab, (1, 1, N_HEADS)), jnp.tile(tab, (1, 1, KV_HEADS))


def _layer(h2d, batch, seq, mod, pw, layer, ctx, caches, final_norm):
    proj, gab = _inproj_call(h2d, mod, pw["norm_g"], pw["w_proj"], layer, seq)
    kw = KV_HEADS * HEAD_DIM
    if ctx is None:
        emit = "first" if caches is None else "update"
        akv, nkv, sg_all, sr_all = caches or (None, None, None, None)
        oa, *akv = _attn_ctx_call(proj, akv, layer, batch, seq, OFF_AQKV, OFF_AQKV + BRANCH_W,
                                  OFF_AQKV + BRANCH_W + kw, OFF_AZ, KV_HEADS, pw["qn"], pw["kn"])
        od, *nkv = _attn_ctx_call(proj, nkv, layer, batch, seq, OFF_D, OFF_D + BRANCH_W,
                                  OFF_D + 2 * BRANCH_W, OFF_D + 3 * BRANCH_W, N_HEADS)
        ob, sg_all = _gdn_call(proj, gab, pw["cw8"], pw["gdn_par"], pw["gdn_norm"], layer, batch, seq,
                               emit=emit, prev=sg_all)
        oc, sr_all = _ret_call(proj, pw["ret_norm"], layer, batch, seq, emit=emit, prev=sr_all)
        caches = (akv, nkv, sg_all, sr_all)
    else:
        oa = _attn_lat_call(proj, ctx["akt"], ctx["avt"], layer, batch, seq, ctx["qtab"], ctx["ktab"],
                            pw["qn"], pw["kn"])
        od = _na_call(proj, ctx["nkt"], ctx["nvt"], ctx["tb"], layer, batch, seq)
        ob, = _gdn_call(proj, gab, pw["cw8"], pw["gdn_par"], pw["gdn_norm"], layer, batch, seq, s0=ctx["sg"])
        oc, = _ret_call(proj, pw["ret_norm"], layer, batch, seq, s0=ctx["sr"])
    outs = _out_call(h2d, mod, pw["norm_g"], (oa, ob, oc, od), pw["wg"], pw["wb"], pw["wo"], layer, seq, final_norm)
    return outs, caches


def kernel(x_prompt, x_sample, cache_attn_k, cache_attn_v, cache_na_k, cache_na_v, state_gdn, state_ret, c, c_ctx, w_ada, b_ada, norm_g, w_in, conv_w, gdn_a_log, gdn_dt_bias, gdn_norm, attn_q_norm, attn_k_norm, ret_norm, na_bias, w_branch, w_out, final_norm):
    batch, seq, _ = x_prompt.shape
    dbatch, dseq, _ = x_sample.shape
    assert dbatch == 8, "the modulation kernel handles exactly one sublane tile of conditioning rows"

    w_proj, wg = _prep_weights(w_in)
    par = jnp.zeros((DEPTH, 2, 128), F32)
    par = par.at[:, 0, 8:16].set(gdn_dt_bias.reshape(DEPTH, 8)).at[:, 1, 8:16].set(gdn_a_log.reshape(DEPTH, 8))
    pw = dict(
        w_ada=w_ada, b_ada=b_ada.reshape(DEPTH, 1, 3 * D_MODEL), norm_g=norm_g.reshape(DEPTH, 1, D_MODEL),
        w_proj=w_proj, wg=wg, wb=w_branch.astype(BF16), wo=w_out.astype(BF16),
        cw8=jnp.concatenate([conv_w, jnp.zeros((DEPTH, 8 - SHORT_CONV, 3 * BRANCH_W), F32)], axis=1),
        gdn_par=par,
        gdn_norm=jnp.tile(gdn_norm, (1, N_HEADS)).reshape(DEPTH, 1, BRANCH_W),
        ret_norm=ret_norm.reshape(DEPTH, 1, HEAD_DIM),
        qn=jnp.tile(attn_q_norm, (1, N_HEADS)).reshape(DEPTH, 1, BRANCH_W),
        kn=jnp.tile(attn_k_norm, (1, KV_HEADS)).reshape(DEPTH, 1, KV_HEADS * HEAD_DIM))

    cond = jnp.concatenate([jnp.broadcast_to(c_ctx, (8, D_MODEL)), c], axis=0)
    mods = _mod_call(cond, w_ada, pw["b_ada"])

    h = x_prompt.reshape(batch * seq, D_MODEL)
    caches = None
    for l in range(DEPTH):
        outs, caches = _layer(h, batch, seq, mods[l, 0:1].reshape(1, 1, 3 * D_MODEL), pw, l, None, caches,
                              final_norm if l == DEPTH - 1 else None)
        h = outs[0]
    y_prompt = outs[1].reshape(batch, seq, D_MODEL)
    token_major = lambda a: a.transpose(0, 1, 4, 2, 3)
    (akt, avt), (nkt, nvt), new_state_gdn, new_state_ret = caches
    new_attn_k, new_attn_v, new_na_k, new_na_v = (token_major(a) for a in (akt, avt, nkt, nvt))

    qtab, ktab = _rope_tables(dseq)
    feature_major = lambda a: a.transpose(0, 1, 3, 4, 2)
    ctx = dict(akt=feature_major(cache_attn_k), avt=feature_major(cache_attn_v),
               nkt=feature_major(cache_na_k), nvt=feature_major(cache_na_v),
               sg=state_gdn, sr=state_ret, tb=_na_bias_call(na_bias), qtab=qtab, ktab=ktab)
    h = x_sample.reshape(dbatch * dseq, D_MODEL)
    for l in range(DEPTH):
        outs, _ = _layer(h, dbatch, dseq, mods[l, 8:16].reshape(dbatch, 1, 3 * D_MODEL), pw, l, ctx, None,
                         final_norm if l == DEPTH - 1 else None)
        h = outs[0]
    y_sample = outs[1].reshape(dbatch, dseq, D_MODEL)
    return (y_prompt, y_sample, new_attn_k, new_attn_v, new_na_k, new_na_v, new_state_gdn, new_state_ret)
```

```python
import functools

import numpy as np
import jax
import jax.numpy as jnp
from jax import lax
from jax.experimental import pallas as pl
from jax.experimental.pallas import tpu as pltpu

F32 = jnp.float32
BF16 = jnp.bfloat16

D_MODEL = 1024
HEAD_DIM = 64
N_HEADS = 4
KV_HEADS = N_HEADS // 2
BRANCH_W = N_HEADS * HEAD_DIM
N_BRANCH = 4
DEPTH = 2
GRID_W = 64
CHUNK = 64
PREP_CHUNKS = 4
SCAN_UNROLL = 4
assert CHUNK == HEAD_DIM
SHORT_CONV = 5
NA_ROWS = 8
NA_COLS = 16
RET_ROWS_PER_STEP = 1024
CTX_SEQS_PER_STEP = 4
NA_ROWS_PER_STEP = 4
N_DR = 2 * NA_ROWS - 1
N_DC = 2 * NA_COLS - 1
ROPE_THETA = 10000.0
RET_DECAY_BASE = (5.0, 5.5)
RET_TILE = 256
EPS = 1e-6
SCALE = HEAD_DIM ** -0.5
LOG2E = 1.4426950408889634
NEG_INF = float("-inf")

IN_SPLITS = (256, 128, 128, 256, 768, 16, 256, 256, 256, 256, 256, 256, 256, 256, 256, 4096)
PROJ_W = 4096
OFF_AQKV = 0
OFF_AZ = 512
OFF_GQKV = 768
OFF_GZ = 1536
OFF_C = 1792
OFF_D = 2816
OFF_GAB = 3840

V7X_VMEM_BYTES = 64 * 1024 * 1024
MIB = 1024 * 1024


def _params(vmem_mib, n_axes):
    assert vmem_mib * MIB < V7X_VMEM_BYTES
    return pltpu.CompilerParams(dimension_semantics=("arbitrary",) * n_axes,
                                vmem_limit_bytes=vmem_mib * MIB)


def _layer_spec(block, layer, n_grid):
    zeros = (0,) * len(block)
    if n_grid == 1:
        return pl.BlockSpec((None,) + block, lambda i: (layer,) + zeros)
    return pl.BlockSpec((None,) + block, lambda i, j: (layer,) + zeros)


def _mm(a, b):
    return jnp.dot(a.astype(BF16), b.astype(BF16), preferred_element_type=F32)


def _mm_nt(a, b):
    return lax.dot_general(a.astype(BF16), b.astype(BF16), (((1,), (1,)), ((), ())),
                           preferred_element_type=F32)


def _mm_tn(a, b):
    return lax.dot_general(a.astype(BF16), b.astype(BF16), (((0,), (0,)), ((), ())),
                           preferred_element_type=F32)


def _split3(x):
    hi = x.astype(BF16)
    r = x - hi.astype(F32)
    mid = r.astype(BF16)
    lo = (r - mid.astype(F32)).astype(BF16)
    return hi, mid, lo


def _mm_exact(sel, x, terms=3):
    return sum(jnp.dot(sel, part, preferred_element_type=F32) for part in _split3(x)[:terms])


def _mm_exact_lhs(x, sel, terms=3):
    return sum(jnp.dot(part, sel, preferred_element_type=F32) for part in _split3(x)[:terms])


def _silu(x):
    return x * jax.nn.sigmoid(x)


def _head_block_matrix(width, value):
    ri = lax.broadcasted_iota(jnp.int32, (width, width), 0) >> 6
    ci = lax.broadcasted_iota(jnp.int32, (width, width), 1) >> 6
    return jnp.where(ri == ci, value, 0.0).astype(BF16)


def _head_reduce(x, g):
    hi = x.astype(BF16)
    lo = (x - hi.astype(F32)).astype(BF16)
    return jnp.dot(hi, g, preferred_element_type=F32) + jnp.dot(lo, g, preferred_element_type=F32)


def _head_rms(x):
    ms = _head_reduce(x * x, _head_block_matrix(x.shape[1], 1.0 / HEAD_DIM))
    return x * lax.rsqrt(ms + EPS)


def _rope(x, tab_ref):
    w = x.shape[1]
    return (x * tab_ref[0] + pltpu.roll(x, w - 16, 1) * tab_ref[1] + pltpu.roll(x, 16, 1) * tab_ref[2])


def _attend(qs, parts):
    groups = range(len(qs))
    qs = [(q.astype(F32) * (SCALE * LOG2E)).astype(BF16) for q in qs]

    def score(q, part):
        k, _, bias, feature_major = part
        s = _mm(q, k) if feature_major else _mm_nt(q, k)
        return s if bias is None else s + bias

    scores = [[score(qs[g], part) for part in parts[g]] for g in groups]
    m = [functools.reduce(jnp.maximum, [s.max(axis=-1, keepdims=True) for s in scores[g]]) for g in groups]
    p = [[jnp.exp2(s - m[g]) for s in scores[g]] for g in groups]
    den = [sum(x.sum(axis=-1, keepdims=True) for x in p[g]) for g in groups]
    out = [sum(_mm_nt(x, part[1]) if part[3] else _mm(x, part[1]) for x, part in zip(p[g], parts[g]))
           for g in groups]
    return [out[g] / den[g] for g in groups]


def _hs(h):
    return slice(h * HEAD_DIM, (h + 1) * HEAD_DIM)


def _aligned(x, m):
    return x if isinstance(x, int) else pl.multiple_of(x, m)


def _mod_kernel(c_ref, w_ref, b_ref, o_ref):
    o_ref[...] = _mm(_silu(c_ref[...]), w_ref[...]) + b_ref[...]


def _mod_call(cond, w_ada, b_ada3):
    tn = 512
    rows = cond.shape[0]
    return pl.pallas_call(
        _mod_kernel,
        grid=(DEPTH, 3 * D_MODEL // tn),
        in_specs=[pl.BlockSpec((rows, D_MODEL), lambda l, j: (0, 0)),
                  pl.BlockSpec((None, D_MODEL, tn), lambda l, j: (l, 0, j)),
                  pl.BlockSpec((None, 1, tn), lambda l, j: (l, 0, j))],
        out_specs=pl.BlockSpec((None, rows, tn), lambda l, j: (l, 0, j)),
        out_shape=jax.ShapeDtypeStruct((DEPTH, rows, 3 * D_MODEL), F32),
        compiler_params=_params(24, 2),
        name="adaln_mod",
    )(cond, w_ada, b_ada3)


def _modulated_norm(x, mod, g):
    ms = jnp.mean(x * x, axis=-1, keepdims=True)
    y = x * lax.rsqrt(ms + EPS) * g
    return y * (1.0 + mod[:, D_MODEL:2 * D_MODEL]) + mod[:, :D_MODEL]


def _inproj_kernel(x_ref, mod_ref, g_ref, *refs):
    *w_refs, o_ref, ab_ref = refs
    hn = _modulated_norm(x_ref[...], mod_ref[0], g_ref[...]).astype(BF16)
    tn = 256
    col = 0
    for w_ref in w_refs:
        for j in range(w_ref.shape[0] // tn):
            y = _mm_nt(hn, w_ref[j * tn:(j + 1) * tn, :])
            o_ref[:, col:col + tn] = y.astype(BF16)
            if col == OFF_GAB:
                ab_ref[...] = y[:, :128]
            col += tn
    assert col == PROJ_W


def _inproj_call(x2d, mod3, norm_g3, weights, layer, rows_per_mod):
    t = x2d.shape[0]
    tm = 512
    if mod3.shape[0] == 1:
        mod_idx = lambda i: (0, 0, 0)
    else:
        mod_idx = lambda i: ((i * tm) // rows_per_mod, 0, 0)
    w_specs = [pl.BlockSpec((None, w.shape[1], D_MODEL), lambda i: (layer, 0, 0), pipeline_mode=pl.Buffered(1))
               for w in weights]
    return pl.pallas_call(
        _inproj_kernel,
        grid=(t // tm,),
        in_specs=[pl.BlockSpec((tm, D_MODEL), lambda i: (i, 0)),
                  pl.BlockSpec((1, 1, 3 * D_MODEL), mod_idx),
                  _layer_spec((1, D_MODEL), layer, 1)] + w_specs,
        out_specs=[pl.BlockSpec((tm, PROJ_W), lambda i: (i, 0)), pl.BlockSpec((tm, 128), lambda i: (i, 0))],
        out_shape=[jax.ShapeDtypeStruct((t, PROJ_W), BF16), jax.ShapeDtypeStruct((t, 128), F32)],
        compiler_params=_params(40, 1),
        name="inproj",
    )(x2d, mod3, norm_g3, *weights)


def _stacked_heads(q, n_kv):
    rep = N_HEADS // n_kv
    return [jnp.concatenate([q[:, _hs(g * rep + r)] for r in range(rep)], axis=0) for g in range(n_kv)]


def _unstack_heads(outs, n_kv):
    rep = N_HEADS // n_kv
    m = outs[0].shape[0] // rep
    return jnp.concatenate([outs[g][r * m:(r + 1) * m] for g in range(n_kv) for r in range(rep)], axis=-1)


def _write_layer(ref, layer, value, stacked):
    if not stacked:
        ref[...] = value
        return
    for l in range(ref.shape[0]):
        ref[l] = value if l == layer else jnp.zeros(value.shape, value.dtype)


def _write_state(st_ref, layer, stacked, piece):
    for d in range(2):
        for h in range(N_HEADS):
            value = piece(d, h)
            if stacked:
                for l in range(st_ref.shape[0]):
                    st_ref[l, d, h] = value if l == layer else jnp.zeros(value.shape, value.dtype)
            else:
                st_ref[d, h] = value


def _attn_ctx_kernel(*refs, n_kv, norm, layer, first):
    if norm:
        q_ref, k_ref, v_ref, z_ref, qn_ref, kn_ref = refs[:6]
    else:
        q_ref, k_ref, v_ref, z_ref = refs[:4]
    o_ref, kt_ref, vt_ref = refs[-3:]
    q, k, v, z = q_ref[...], k_ref[...].astype(F32), v_ref[...].astype(F32), z_ref[...].astype(F32)
    if norm:
        q = _head_rms(q.astype(F32)) * qn_ref[...]
        k = _head_rms(k) * kn_ref[...]
    n_seq = kt_ref.shape[0]
    seq = k.shape[0] // n_seq
    qs, parts = [], []
    for s in range(n_seq):
        rows = slice(s * seq, (s + 1) * seq)
        _write_layer(kt_ref.at[s], layer, k[rows].T.reshape(n_kv, HEAD_DIM, seq), first)
        _write_layer(vt_ref.at[s], layer, v[rows].T.reshape(n_kv, HEAD_DIM, seq), first)
        qs += _stacked_heads(q[rows], n_kv)
        parts += [[(k[rows, _hs(g)], v[rows, _hs(g)], None, False)] for g in range(n_kv)]
    outs = _attend(qs, parts)
    o = jnp.concatenate([_unstack_heads(outs[s * n_kv:(s + 1) * n_kv], n_kv) for s in range(n_seq)], axis=0)
    o_ref[...] = (o * _silu(z)).astype(BF16)


def _attn_ctx_call(proj, prev, layer, batch, seq, off_q, off_k, off_v, off_z, n_kv, qn=None, kn=None):
    t = batch * seq
    kvw = n_kv * HEAD_DIM
    norm = qn is not None
    first = prev is None
    n_seq = CTX_SEQS_PER_STEP
    assert batch % n_seq == 0
    tm = n_seq * seq
    in_specs = [pl.BlockSpec((tm, BRANCH_W), lambda b: (b, off_q // BRANCH_W)),
                pl.BlockSpec((tm, kvw), lambda b: (b, off_k // kvw)),
                pl.BlockSpec((tm, kvw), lambda b: (b, off_v // kvw)),
                pl.BlockSpec((tm, BRANCH_W), lambda b: (b, off_z // BRANCH_W))]
    args = [proj, proj, proj, proj]
    if norm:
        in_specs += [_layer_spec((1, BRANCH_W), layer, 1), _layer_spec((1, kvw), layer, 1)]
        args += [qn, kn]
    aliases = {}
    if first:
        cache_spec = pl.BlockSpec((n_seq, DEPTH, n_kv, HEAD_DIM, seq), lambda b: (b, 0, 0, 0, 0))
    else:
        aliases = {len(args): 1, len(args) + 1: 2}
        in_specs += [pl.BlockSpec(memory_space=pl.ANY)] * 2
        args += list(prev)
        cache_spec = pl.BlockSpec((n_seq, None, n_kv, HEAD_DIM, seq), lambda b: (b, layer, 0, 0, 0))
    cache_shape = jax.ShapeDtypeStruct((batch, DEPTH, n_kv, HEAD_DIM, seq), F32)
    return pl.pallas_call(
        functools.partial(_attn_ctx_kernel, n_kv=n_kv, norm=norm, layer=layer, first=first),
        grid=(batch // n_seq,), in_specs=in_specs,
        out_specs=[pl.BlockSpec((tm, BRANCH_W), lambda b: (b, 0)), cache_spec, cache_spec],
        out_shape=[jax.ShapeDtypeStruct((t, BRANCH_W), BF16), cache_shape, cache_shape],
        input_output_aliases=aliases,
        compiler_params=_params(32, 1),
        name="attn_ctx_norm" if norm else "attn_ctx",
    )(*args)


def _attn_lat_kernel(q_ref, kv_ref, z_ref, ckt_ref, cvt_ref, qtab_ref, ktab_ref, qn_ref, kn_ref, o_ref,
                     k_s, v_s):
    kw = KV_HEADS * HEAD_DIM

    @pl.when(pl.program_id(1) == 0)
    def _():
        kv = kv_ref[...]
        k_s[...] = _rope(_head_rms(kv[:, :kw].astype(F32)) * kn_ref[...], ktab_ref).astype(BF16)
        v_s[...] = kv[:, kw:]

    q = _rope(_head_rms(q_ref[...].astype(F32)) * qn_ref[...], qtab_ref)
    k, v = k_s[...], v_s[...]
    outs = _attend(_stacked_heads(q, KV_HEADS),
                   [[(k[:, _hs(g)], v[:, _hs(g)], None, False), (ckt_ref[g], cvt_ref[g], None, True)]
                    for g in range(KV_HEADS)])
    o_ref[...] = (_unstack_heads(outs, KV_HEADS) * _silu(z_ref[...].astype(F32))).astype(BF16)


def _attn_lat_call(proj, cache_kt, cache_vt, layer, batch, seq, qtab, ktab, qn, kn):
    tq = 512
    nq = seq // tq
    past = cache_kt.shape[-1]
    kw = KV_HEADS * HEAD_DIM
    ctx_spec = pl.BlockSpec((None, None, KV_HEADS, HEAD_DIM, past), lambda b, i: (b, layer, 0, 0, 0))
    return pl.pallas_call(
        _attn_lat_kernel,
        grid=(batch, nq),
        in_specs=[pl.BlockSpec((tq, BRANCH_W), lambda b, i: (b * nq + i, OFF_AQKV // BRANCH_W)),
                  pl.BlockSpec((seq, 2 * kw), lambda b, i: (b, (OFF_AQKV + BRANCH_W) // (2 * kw))),
                  pl.BlockSpec((tq, BRANCH_W), lambda b, i: (b * nq + i, OFF_AZ // BRANCH_W)),
                  ctx_spec, ctx_spec,
                  pl.BlockSpec((3, tq, BRANCH_W), lambda b, i: (0, i, 0)),
                  pl.BlockSpec((3, seq, kw), lambda b, i: (0, 0, 0)),
                  _layer_spec((1, BRANCH_W), layer, 2),
                  _layer_spec((1, kw), layer, 2)],
        out_specs=pl.BlockSpec((tq, BRANCH_W), lambda b, i: (b * nq + i, 0)),
        out_shape=jax.ShapeDtypeStruct((batch * seq, BRANCH_W), BF16),
        scratch_shapes=[pltpu.VMEM((seq, kw), BF16), pltpu.VMEM((seq, kw), BF16)],
        compiler_params=_params(40, 2),
        name="attn_lat",
    )(proj, proj, proj, cache_kt, cache_vt, qtab, ktab, qn, kn)


def _na_bias_kernel(t_ref, o_ref):
    nblk = o_ref.shape[0]
    c = lax.broadcasted_iota(jnp.int32, (GRID_W, 2 * GRID_W), 0)
    kc = lax.broadcasted_iota(jnp.int32, (GRID_W, 2 * GRID_W), 1) & (GRID_W - 1)
    cs = jnp.clip(c - NA_COLS // 2, 0, GRID_W - NA_COLS)
    valid = jnp.logical_and(kc >= cs, kc < cs + NA_COLS)

    unroll = 8
    assert nblk % unroll == 0

    def body(i, carry):
        rows8 = t_ref[pl.ds(pl.multiple_of(i * unroll, unroll), unroll), :]
        for u in range(unroll):
            row = jnp.broadcast_to(rows8[u:u + 1, :], (GRID_W, 2 * GRID_W))
            skewed = pltpu.roll(row, 2 * GRID_W - (NA_COLS - 1), 1, stride=1, stride_axis=0)
            o_ref[i * unroll + u] = jnp.where(valid, skewed * LOG2E, NEG_INF)
        return carry

    lax.fori_loop(0, nblk // unroll, body, 0)


def _na_bias_call(na_bias):
    nblk = DEPTH * N_HEADS * N_DR
    rows = jnp.pad(na_bias.reshape(nblk, N_DC), ((0, 1), (0, GRID_W - N_DC)))
    pairs = jnp.concatenate([rows[:-1], rows[1:]], axis=1)
    return pl.pallas_call(
        _na_bias_kernel,
        in_specs=[pl.BlockSpec((nblk, 2 * GRID_W), lambda: (0, 0))],
        out_specs=pl.BlockSpec((nblk, GRID_W, 2 * GRID_W), lambda: (0, 0, 0)),
        out_shape=jax.ShapeDtypeStruct((nblk, GRID_W, 2 * GRID_W), F32),
        name="na_bias",
    )(pairs)


def _na_kernel(q_ref, k_ref, v_ref, z_ref, ckt_ref, cvt_ref, tb_ref, o_ref, kh_s, vh_s, *, rows):
    win = NA_ROWS * GRID_W

    @pl.when(pl.program_id(1) == 0)
    def _():
        for h in range(N_HEADS):
            kh_s[h] = k_ref[:, _hs(h)]
            vh_s[h] = v_ref[:, _hs(h)]

    qs, parts = [], []
    for i in range(NA_ROWS_PER_STEP):
        r = pl.program_id(1) * NA_ROWS_PER_STEP + i
        rs = jnp.clip(r - NA_ROWS // 2, 0, rows - NA_ROWS)
        r0 = pl.multiple_of(rs * GRID_W, GRID_W)
        q = q_ref[i * GRID_W:(i + 1) * GRID_W, :]
        dr0 = rs - r + NA_ROWS - 1
        for h in range(N_HEADS):
            bias = jnp.concatenate([tb_ref[h * N_DR + dr0 + 2 * p] for p in range(NA_ROWS // 2)], axis=1)
            qs.append(q[:, _hs(h)])
            parts.append([(kh_s[h, pl.ds(r0, win), :], vh_s[h, pl.ds(r0, win), :], bias, False),
                          (ckt_ref[h], cvt_ref[h], None, True)])
    outs = _attend(qs, parts)
    o = jnp.concatenate([jnp.concatenate(outs[i * N_HEADS:(i + 1) * N_HEADS], axis=-1)
                         for i in range(NA_ROWS_PER_STEP)], axis=0)
    o_ref[...] = (o * _silu(z_ref[...].astype(F32))).astype(BF16)


def _na_call(proj, cache_kt, cache_vt, tb, layer, batch, seq):
    rows = seq // GRID_W
    assert rows >= NA_ROWS and rows % NA_ROWS_PER_STEP == 0
    steps = rows // NA_ROWS_PER_STEP
    tq = NA_ROWS_PER_STEP * GRID_W
    past = cache_kt.shape[-1]
    nblk = N_HEADS * N_DR
    cq = OFF_D // BRANCH_W
    ctx_spec = pl.BlockSpec((None, None, N_HEADS, HEAD_DIM, past), lambda b, r: (b, layer, 0, 0, 0))
    return pl.pallas_call(
        functools.partial(_na_kernel, rows=rows),
        grid=(batch, steps),
        in_specs=[pl.BlockSpec((tq, BRANCH_W), lambda b, r: (b * steps + r, cq)),
                  pl.BlockSpec((seq, BRANCH_W), lambda b, r: (b, cq + 1)),
                  pl.BlockSpec((seq, BRANCH_W), lambda b, r: (b, cq + 2)),
                  pl.BlockSpec((tq, BRANCH_W), lambda b, r: (b * steps + r, cq + 3)),
                  ctx_spec, ctx_spec,
                  pl.BlockSpec((nblk, GRID_W, 2 * GRID_W), lambda b, r: (layer, 0, 0))],
        out_specs=pl.BlockSpec((tq, BRANCH_W), lambda b, r: (b * steps + r, 0)),
        out_shape=jax.ShapeDtypeStruct((batch * seq, BRANCH_W), BF16),
        scratch_shapes=[pltpu.VMEM((N_HEADS, seq, HEAD_DIM), BF16), pltpu.VMEM((N_HEADS, seq, HEAD_DIM), BF16)],
        compiler_params=_params(32, 2),
        name="na_lat",
    )(proj, proj, proj, proj, cache_kt, cache_vt, tb)


def _gdn_kernel(*refs, seq, has_s0, layer, emit):
    qkv_ref, z_ref, ab_ref, cw_ref, par_ref, g_ref = refs[:6]
    s0_ref = refs[6] if has_s0 else None
    (q_s, k_s, v_s, gcb_s, bcb_s, r_s, mc_s, nc_s, qp_s, op_s, egl_s, s_s, oacc_ref) = refs[-13:]
    if emit == "none":
        o_ref, st_ref = refs[-14], None
    else:
        o_ref, st_ref = refs[-15], refs[-14]
    n_chunks = seq // CHUNK
    n_levels = CHUNK.bit_length() - 1
    qkv_w = 3 * BRANCH_W
    half = SHORT_CONV // 2
    pair_w = 2 * HEAD_DIM
    pairs = [slice(p * pair_w, (p + 1) * pair_w) for p in range(BRANCH_W // pair_w)]
    tr = 256
    cpt = tr // CHUNK
    head_sum = _head_block_matrix(BRANCH_W, 1.0)

    gc_i = lax.broadcasted_iota(jnp.int32, (128, BRANCH_W), 0)
    gh_j = lax.broadcasted_iota(jnp.int32, (128, BRANCH_W), 1) >> 6
    sel_beta = [jnp.where(gc_i == gh_j + 4 * d, 1.0, 0.0).astype(BF16) for d in range(2)]
    sel_gate = [jnp.where(gc_i == gh_j + 8 + 4 * d, 1.0, 0.0).astype(BF16) for d in range(2)]
    ti = lax.broadcasted_iota(jnp.int32, (tr, tr), 0)
    tj = lax.broadcasted_iota(jnp.int32, (tr, tr), 1)
    same_chunk = (ti >> 6) == (tj >> 6)
    tri = [jnp.where(jnp.logical_and(same_chunk, ti >= tj), 1.0, 0.0).astype(BF16),
           jnp.where(jnp.logical_and(same_chunk, ti <= tj), 1.0, 0.0).astype(BF16)]
    assert tr == BRANCH_W
    lane_head = lax.broadcasted_iota(jnp.int32, (1, BRANCH_W), 1) >> 6

    halo = 16
    edge = 8
    assert half <= edge
    si = lax.broadcasted_iota(jnp.int32, (tr, tr), 0)
    sj = lax.broadcasted_iota(jnp.int32, (tr, tr), 1)
    ei = lax.broadcasted_iota(jnp.int32, (edge, halo), 0)
    ej = lax.broadcasted_iota(jnp.int32, (edge, halo), 1)
    taps = [j for j in range(SHORT_CONV) if j != half]
    shift = {j: jnp.where(sj == si + (j - half), 1.0, 0.0).astype(BF16) for j in taps}
    shift_before = {j: jnp.where(ej == ei + (halo + j - half), 1.0, 0.0).astype(BF16) for j in taps if j < half}
    shift_after = {j: jnp.where(ej == ei + (j - half - edge), 1.0, 0.0).astype(BF16) for j in taps if j > half}
    for t in range(seq // tr):
        rows = slice(t * tr, (t + 1) * tr)
        x = qkv_ref[rows, :]
        y = x.astype(F32) * cw_ref[half:half + 1, :]
        for j in taps:
            y = y + jnp.dot(shift[j], x, preferred_element_type=F32) * cw_ref[j:j + 1, :]
        if t > 0:
            before = qkv_ref[t * tr - halo:t * tr, :]
            top = sum(jnp.dot(shift_before[j], before, preferred_element_type=F32) * cw_ref[j:j + 1, :]
                      for j in shift_before)
            y = jnp.concatenate([y[:edge] + top, y[edge:]], axis=0)
        if (t + 1) * tr < seq:
            after = qkv_ref[(t + 1) * tr:(t + 1) * tr + halo, :]
            bottom = sum(jnp.dot(shift_after[j], after, preferred_element_type=F32) * cw_ref[j:j + 1, :]
                         for j in shift_after)
            y = jnp.concatenate([y[:tr - edge], y[tr - edge:] + bottom], axis=0)
        y = _silu(y)
        qq, kk = y[:, :BRANCH_W], y[:, BRANCH_W:2 * BRANCH_W]
        q_s[rows, :] = qq * lax.rsqrt(_head_reduce(qq * qq, head_sum) + EPS) * SCALE
        k_s[rows, :] = kk * lax.rsqrt(_head_reduce(kk * kk, head_sum) + EPS)
        v_s[rows, :] = y[:, 2 * BRANCH_W:]
        x = ab_ref[rows, :]
        beta = jax.nn.sigmoid(x)
        xs = x + par_ref[0:1, :]
        softplus = jnp.maximum(xs, 0.0) + jnp.log1p(jnp.exp(-jnp.abs(xs)))
        la = -jnp.exp(par_ref[1:2, :]) * softplus
        for d in range(2):
            gc = _mm_exact(tri[d], la, terms=2)
            gcb_s[d, rows, :] = _mm_exact_lhs(gc, sel_gate[d], terms=2)
            bcb_s[d, rows, :] = _mm_exact_lhs(beta, sel_beta[d], terms=2)
            gt = gc.T[8:16, :]
            shifted = {s: (gt if s == 0 else pltpu.roll(gt, (s * HEAD_DIM) % tr, 1))
                       for s in range(1 - cpt, N_HEADS)}
            for c in range(cpt):
                r = jnp.zeros((1, BRANCH_W), F32)
                for h in range(N_HEADS):
                    r = jnp.where(lane_head == h, shifted[h - c][4 * d + h:4 * d + h + 1, :], r)
                r_s[d, (t * cpt + c) * 8:(t * cpt + c + 1) * 8, :] = jnp.broadcast_to(r, (8, BRANCH_W))

    for d in range(2):
        if has_s0:
            s_s[d] = jnp.concatenate([s0_ref[d, h] for h in range(N_HEADS)], axis=-1)
        else:
            s_s[d] = jnp.zeros((HEAD_DIM, BRANCH_W), F32)

    li = lax.broadcasted_iota(jnp.int32, (CHUNK, BRANCH_W), 0)
    lj = lax.broadcasted_iota(jnp.int32, (CHUNK, BRANCH_W), 1) & (HEAD_DIM - 1)
    incl = (li >= lj, li <= lj)
    strict = (li > lj, li < lj)
    level = [((li ^ lj) >> l) == 1 for l in range(n_levels)]
    first_head = lax.broadcasted_iota(jnp.int32, (CHUNK, pair_w), 1) < HEAD_DIM

    def expand(y):
        yb = y.astype(BF16)
        zero = jnp.zeros((CHUNK, pair_w), BF16)
        return [jnp.concatenate([jnp.where(first_head, yb[:, p], zero), jnp.where(first_head, zero, yb[:, p])],
                                axis=0) for p in pairs]

    def bdmm(x, ybd):
        xb = x.astype(BF16)
        return jnp.concatenate([jnp.dot(xb[:, p], ybd[i], preferred_element_type=F32)
                                for i, p in enumerate(pairs)], axis=1)

    def bdmm_nt(x, ybd):
        xb = x.astype(BF16)
        return jnp.concatenate([lax.dot_general(xb[:, p], ybd[i], (((1,), (1,)), ((), ())),
                                                preferred_element_type=F32)
                                for i, p in enumerate(pairs)], axis=1)

    def bdmm2(x, y1, y2):
        xb = x.astype(BF16)
        e1, e2 = expand(y1), expand(y2)
        outs = [jnp.dot(xb[:, p], jnp.concatenate([e1[i], e2[i]], axis=1), preferred_element_type=F32)
                for i, p in enumerate(pairs)]
        return (jnp.concatenate([o[:, :pair_w] for o in outs], axis=1),
                jnp.concatenate([o[:, pair_w:] for o in outs], axis=1))

    def tn_diag2(a, b1, b2):
        ab, b1b, b2b = a.astype(BF16), b1.astype(BF16), b2.astype(BF16)
        outs1, outs2 = [], []
        for p in pairs:
            full = lax.dot_general(ab[:, p], jnp.concatenate([b1b[:, p], b2b[:, p]], axis=1),
                                   (((0,), (0,)), ((), ())), preferred_element_type=F32)
            outs1.append(jnp.where(first_head, full[:HEAD_DIM, :pair_w], full[HEAD_DIM:, :pair_w]))
            outs2.append(jnp.where(first_head, full[:HEAD_DIM, pair_w:], full[HEAD_DIM:, pair_w:]))
        return jnp.concatenate(outs1, axis=1), jnp.concatenate(outs2, axis=1)

    def prepare(chains):
        n = range(len(chains))
        dd = [d for d, _ in chains]
        rows = [pl.ds(_aligned(c * CHUNK, CHUNK), CHUNK) for _, c in chains]
        gcb = [gcb_s[dd[i], rows[i], :] for i in n]
        bcb = [bcb_s[dd[i], rows[i], :] for i in n]
        grow = [r_s[dd[i], pl.ds(_aligned(chains[i][1] * 8, 8), 8), :][0:1, :] for i in n]
        dm = [jnp.exp(jnp.where(incl[dd[i]], gcb[i] - grow[i], NEG_INF)) for i in n]
        k = [k_s[rows[i], :] for i in n]
        q = [q_s[rows[i], :] for i in n]
        v = [v_s[rows[i], :] for i in n]
        kq = [bdmm_nt(jnp.concatenate([k[i], q[i]], axis=0), expand(k[i])) for i in n]
        a = [jnp.where(strict[dd[i]], bcb[i] * kq[i][:CHUNK] * dm[i], 0.0) for i in n]
        tm = [-jnp.where(level[0], a[i], 0.0) for i in n]
        for l in range(1, n_levels):
            b = [jnp.where(level[l], a[i], 0.0) for i in n]
            y = [b[i] + bdmm(tm[i], expand(b[i])) for i in n]
            tm = [tm[i] - (y[i] + bdmm(y[i], expand(tm[i]))) for i in n]
        eg = [jnp.exp(gcb[i]) for i in n]
        bv = [bcb[i] * v[i] for i in n]
        bk = [bcb[i] * k[i] * eg[i] for i in n]
        tuw = [bdmm2(tm[i], bv[i], bk[i]) for i in n]
        u = [bv[i] + tuw[i][0] for i in n]
        w = [bk[i] + tuw[i][1] for i in n]
        gl = [gcb[i][CHUNK - 1:CHUNK, :] if dd[i] == 0 else gcb[i][0:1, :] for i in n]
        kd = [k[i] * jnp.exp(gl[i] - gcb[i]) for i in n]
        qkm = [kq[i][CHUNK:] * dm[i] for i in n]
        mnc = [tn_diag2(kd[i], w[i], u[i]) for i in n]
        mc = [mnc[i][0] for i in n]
        nc = [mnc[i][1] for i in n]
        qwu = [bdmm2(qkm[i], w[i], u[i]) for i in n]
        qp = [q[i] * eg[i] - qwu[i][0] for i in n]
        op = [qwu[i][1] for i in n]
        for i in n:
            d, c = chains[i]
            mc_s[d, rows[i], :] = mc[i].astype(BF16)
            nc_s[d, rows[i], :] = nc[i]
            qp_s[d, rows[i], :] = qp[i].astype(BF16)
            op_s[d, rows[i], :] = op[i]
            egl_s[d, pl.ds(_aligned(c * 8, 8), 8), :] = jnp.broadcast_to(jnp.exp(gl[i]), (8, BRANCH_W))

    group = min(PREP_CHUNKS, n_chunks)
    if n_chunks == group:
        prepare([(d, c) for c in range(group) for d in range(2)])
    else:
        def prep_body(j, carry):
            prepare([(d, j * group + c) for c in range(group) for d in range(2)])
            return carry
        lax.fori_loop(0, n_chunks // group, prep_body, 0)

    def scan_step(i):
        for d, c in ((0, i), (1, n_chunks - 1 - i)):
            rows = pl.ds(_aligned(c * CHUNK, CHUNK), CHUNK)
            s = s_s[d]
            sbd = expand(s)
            oacc_ref[d, rows, :] = bdmm(qp_s[d, rows, :], sbd) + op_s[d, rows, :]
            egl = egl_s[d, pl.ds(_aligned(c * 8, 8), 8), :][0:1, :]
            s_s[d] = s * egl - bdmm(mc_s[d, rows, :], sbd) + nc_s[d, rows, :]

    unroll = min(SCAN_UNROLL, n_chunks)

    def scan_body(j, carry):
        for i in range(unroll):
            scan_step(j * unroll + i)
        return carry

    lax.fori_loop(0, n_chunks // unroll, scan_body, 0)

    if st_ref is not None:
        _write_state(st_ref, layer, emit == "first", lambda d, h: s_s[d][:, _hs(h)])
    o = oacc_ref[0] + oacc_ref[1]
    ms = _head_reduce(o * o, _head_block_matrix(BRANCH_W, 1.0 / HEAD_DIM))
    o_ref[...] = (o * lax.rsqrt(ms + EPS) * g_ref[...] * _silu(z_ref[...].astype(F32))).astype(BF16)


def _state_spec(layer, n_seq=None):
    return pl.BlockSpec((n_seq, None, 2, N_HEADS, HEAD_DIM, HEAD_DIM), lambda b: (b, layer, 0, 0, 0, 0))


def _state_output(emit, prev, layer, batch, n_args, n_seq=None):
    if emit == "none":
        return [], [], [], [], {}
    shape = jax.ShapeDtypeStruct((batch, DEPTH, 2, N_HEADS, HEAD_DIM, HEAD_DIM), F32)
    if emit == "first":
        spec = pl.BlockSpec((n_seq, DEPTH, 2, N_HEADS, HEAD_DIM, HEAD_DIM), lambda b: (b, 0, 0, 0, 0, 0))
        return [], [], [spec], [shape], {}
    return [pl.BlockSpec(memory_space=pl.ANY)], [prev], [_state_spec(layer, n_seq)], [shape], {n_args: 1}


def _gdn_call(proj, gab, cw8, par, norm_g, layer, batch, seq, s0=None, emit="none", prev=None):
    has_s0 = s0 is not None
    qkv_w = 3 * BRANCH_W
    in_specs = [pl.BlockSpec((seq, qkv_w), lambda b: (b, OFF_GQKV // qkv_w)),
                pl.BlockSpec((seq, BRANCH_W), lambda b: (b, OFF_GZ // BRANCH_W)),
                pl.BlockSpec((seq, 128), lambda b: (b, 0)),
                _layer_spec((8, qkv_w), layer, 1),
                _layer_spec((2, 128), layer, 1),
                _layer_spec((1, BRANCH_W), layer, 1)]
    args = [proj, proj, gab, cw8, par, norm_g]
    if has_s0:
        in_specs.append(_state_spec(layer))
        args.append(s0)
    st_in_specs, st_args, st_out_specs, st_shapes, aliases = _state_output(emit, prev, layer, batch, len(args))
    return pl.pallas_call(
        functools.partial(_gdn_kernel, seq=seq, has_s0=has_s0, layer=layer, emit=emit),
        grid=(batch,), in_specs=in_specs + st_in_specs,
        out_specs=[pl.BlockSpec((seq, BRANCH_W), lambda b: (b, 0))] + st_out_specs,
        out_shape=[jax.ShapeDtypeStruct((batch * seq, BRANCH_W), BF16)] + st_shapes,
        input_output_aliases=aliases,
        scratch_shapes=[pltpu.VMEM((seq, BRANCH_W), F32),
                        pltpu.VMEM((seq, BRANCH_W), F32),
                        pltpu.VMEM((seq, BRANCH_W), F32),
                        pltpu.VMEM((2, seq, BRANCH_W), F32),
                        pltpu.VMEM((2, seq, BRANCH_W), F32),
                        pltpu.VMEM((2, seq // CHUNK * 8, BRANCH_W), F32),
                        pltpu.VMEM((2, seq, BRANCH_W), BF16),
                        pltpu.VMEM((2, seq, BRANCH_W), F32),
                        pltpu.VMEM((2, seq, BRANCH_W), BF16),
                        pltpu.VMEM((2, seq, BRANCH_W), F32),
                        pltpu.VMEM((2, seq // CHUNK * 8, BRANCH_W), F32),
                        pltpu.VMEM((2, HEAD_DIM, BRANCH_W), F32),
                        pltpu.VMEM((2, seq, BRANCH_W), F32)],
        compiler_params=_params(48, 1),
        name="gdn",
    )(*args, *st_args)


_RET_LOG_GAMMA = [[float(np.log1p(-np.exp2(-(base + h)))) for h in range(N_HEADS)] for base in RET_DECAY_BASE]


def _ret_kernel(*refs, seq, n_seq, has_s0, layer, emit):
    qkv_refs, z_ref, g_ref = refs[:3], refs[3], refs[4]
    s0_ref = refs[5] if has_s0 else None
    o_ref, st_ref = (refs[-1], None) if emit == "none" else (refs[-2], refs[-1])
    tile = RET_TILE
    n_tiles = seq // tile
    problems = [(s, h) for s in range(n_seq) for h in range(N_HEADS)]
    heads = range(len(problems))
    lgf = [_RET_LOG_GAMMA[0][h] for _, h in problems]
    lgb = [_RET_LOG_GAMMA[1][h] for _, h in problems]
    a = lax.broadcasted_iota(jnp.int32, (tile, 1), 0).astype(F32)
    ef = [jnp.exp(a * lgf[h]) for h in heads]
    eif = [jnp.exp(-a * lgf[h]) for h in heads]
    eb = [jnp.exp(a * lgb[h]) for h in heads]
    eib = [jnp.exp(-a * lgb[h]) for h in heads]
    gf_tile = [float(np.exp(tile * lgf[h])) for h in heads]
    gb_tile = [float(np.exp(tile * lgb[h])) for h in heads]
    ii = lax.broadcasted_iota(jnp.int32, (tile, tile), 0)
    jj = lax.broadcasted_iota(jnp.int32, (tile, tile), 1)

    def head_cols(t, part, i):
        s, h = problems[i]
        return qkv_refs[part][s * seq + t * tile:s * seq + (t + 1) * tile, _hs(h)]

    def initial_state(d, i):
        s, h = problems[i]
        return s0_ref[s, d, h]

    kf = [[head_cols(t, 1, h) * eif[h] for h in heads] for t in range(n_tiles)]
    kb = [[head_cols(t, 1, h) * eb[h] for h in heads] for t in range(n_tiles)]
    vs = [[head_cols(t, 2, h) for h in heads] for t in range(n_tiles)]
    use_states = has_s0 or n_tiles > 1 or st_ref is not None
    if use_states:
        kvf = [[_mm_tn(kf[t][h], vs[t][h]) for h in heads] for t in range(n_tiles)]
        kvb = [[_mm_tn(kb[t][h], vs[t][h]) for h in heads] for t in range(n_tiles)]
        zero = jnp.zeros((HEAD_DIM, HEAD_DIM), F32)
        zf = [[(float(np.exp(lgf[h])) * initial_state(0, h)) if has_s0 else zero for h in heads]]
        for t in range(n_tiles):
            zf.append([gf_tile[h] * (zf[t][h] + kvf[t][h]) for h in heads])
        acc = [initial_state(1, h) if has_s0 else zero for h in heads]
        zb = [None] * n_tiles
        for t in reversed(range(n_tiles)):
            zb[t] = [gb_tile[h] * acc[h] for h in heads]
            acc = [zb[t][h] + kvb[t][h] for h in heads]
        if st_ref is not None:
            stf = [zf[n_tiles][h] * float(np.exp(-lgf[h])) for h in heads]
            for s in range(n_seq):
                _write_state(st_ref.at[s], layer, emit == "first",
                             lambda d, h, s=s: (stf, acc)[d][s * N_HEADS + h])
    tiles = [[] for _ in range(n_seq)]
    for t in range(n_tiles):
        q = [head_cols(t, 0, h) * SCALE for h in heads]
        qf = [q[h] * ef[h] for h in heads]
        qb = [q[h] * eib[h] for h in heads]
        sd = [jnp.where(ii >= jj, _mm_nt(qf[h], kf[t][h]), 0.0) + jnp.where(ii <= jj, _mm_nt(qb[h], kb[t][h]), 0.0)
              for h in heads]
        o = [_mm(sd[h], vs[t][h]) for h in heads]
        if has_s0 or n_tiles > 1:
            o = [o[h] + _mm(qf[h], zf[t][h]) + _mm(qb[h], zb[t][h]) for h in heads]
        ms = [jnp.mean(o[h] * o[h], axis=-1, keepdims=True) for h in heads]
        for s in range(n_seq):
            tiles[s].append(jnp.concatenate([o[h] * lax.rsqrt(ms[h] + EPS) * g_ref[...]
                                             for h in range(s * N_HEADS, (s + 1) * N_HEADS)], axis=-1))
    o = jnp.concatenate([tile_out for s in range(n_seq) for tile_out in tiles[s]], axis=0)
    o_ref[...] = (o * _silu(z_ref[...].astype(F32))).astype(BF16)


def _ret_call(proj, norm_g, layer, batch, seq, s0=None, emit="none", prev=None):
    has_s0 = s0 is not None
    n_seq = max(1, RET_ROWS_PER_STEP // seq)
    assert batch % n_seq == 0
    tm = n_seq * seq
    cq = OFF_C // BRANCH_W
    in_specs = [pl.BlockSpec((tm, BRANCH_W), lambda b, part=part: (b, cq + part)) for part in range(4)]
    in_specs.append(_layer_spec((1, HEAD_DIM), layer, 1))
    args = [proj, proj, proj, proj, norm_g]
    if has_s0:
        in_specs.append(_state_spec(layer, n_seq))
        args.append(s0)
    st_in_specs, st_args, st_out_specs, st_shapes, aliases = _state_output(emit, prev, layer, batch, len(args),
                                                                           n_seq)
    return pl.pallas_call(
        functools.partial(_ret_kernel, seq=seq, n_seq=n_seq, has_s0=has_s0, layer=layer, emit=emit),
        grid=(batch // n_seq,), in_specs=in_specs + st_in_specs,
        out_specs=[pl.BlockSpec((tm, BRANCH_W), lambda b: (b, 0))] + st_out_specs,
        out_shape=[jax.ShapeDtypeStruct((batch * seq, BRANCH_W), BF16)] + st_shapes,
        input_output_aliases=aliases,
        compiler_params=_params(48, 1),
        name="retention",
    )(*args, *st_args)


def _out_kernel(*refs, final):
    if final:
        (h_ref, mod_ref, g_ref, oa_ref, ob_ref, oc_ref, od_ref, wg_ref, wb_ref, wo_ref, fn_ref,
         o_ref, y_ref) = refs
    else:
        h_ref, mod_ref, g_ref, oa_ref, ob_ref, oc_ref, od_ref, wg_ref, wb_ref, wo_ref, o_ref = refs
    x = h_ref[...]
    mod = mod_ref[0]
    hn = _modulated_norm(x, mod, g_ref[...]).astype(BF16)
    merged = None
    for n, br_ref in enumerate((oa_ref, ob_ref, oc_ref, od_ref)):
        gate = jax.nn.sigmoid(_mm_nt(hn, wg_ref[n * D_MODEL:(n + 1) * D_MODEL, :]))
        up = jnp.dot(br_ref[...], wb_ref[n], preferred_element_type=F32)
        merged = gate * up if merged is None else merged + gate * up
    out = jnp.dot(merged.astype(BF16), wo_ref[...], preferred_element_type=F32)
    hnew = x + mod[:, 2 * D_MODEL:] * out
    o_ref[...] = hnew
    if final:
        ms = jnp.mean(hnew * hnew, axis=-1, keepdims=True)
        y_ref[...] = hnew * lax.rsqrt(ms + EPS) * fn_ref[...]


def _out_call(h2d, mod3, norm_g3, branches, wg, wb, wo, layer, rows_per_mod, final_norm=None):
    t = h2d.shape[0]
    tm = 512
    final = final_norm is not None
    if mod3.shape[0] == 1:
        mod_idx = lambda i: (0, 0, 0)
    else:
        mod_idx = lambda i: ((i * tm) // rows_per_mod, 0, 0)
    once = pl.Buffered(1)
    in_specs = [pl.BlockSpec((tm, D_MODEL), lambda i: (i, 0)),
                pl.BlockSpec((1, 1, 3 * D_MODEL), mod_idx),
                _layer_spec((1, D_MODEL), layer, 1)]
    in_specs += [pl.BlockSpec((tm, BRANCH_W), lambda i: (i, 0))] * N_BRANCH
    in_specs += [pl.BlockSpec((None, N_BRANCH * D_MODEL, D_MODEL), lambda i: (layer, 0, 0), pipeline_mode=once),
                 pl.BlockSpec((None, N_BRANCH, BRANCH_W, D_MODEL), lambda i: (layer, 0, 0, 0), pipeline_mode=once),
                 pl.BlockSpec((None, D_MODEL, D_MODEL), lambda i: (layer, 0, 0), pipeline_mode=once)]
    args = [h2d, mod3, norm_g3, *branches, wg, wb, wo]
    out_specs = [pl.BlockSpec((tm, D_MODEL), lambda i: (i, 0))]
    out_shape = [jax.ShapeDtypeStruct((t, D_MODEL), F32)]
    if final:
        in_specs.append(pl.BlockSpec((1, D_MODEL), lambda i: (0, 0)))
        args.append(final_norm.reshape(1, D_MODEL))
        out_specs.append(pl.BlockSpec((tm, D_MODEL), lambda i: (i, 0)))
        out_shape.append(jax.ShapeDtypeStruct((t, D_MODEL), F32))
    return pl.pallas_call(
        functools.partial(_out_kernel, final=final),
        grid=(t // tm,), in_specs=in_specs, out_specs=out_specs, out_shape=out_shape,
        compiler_params=_params(48, 1),
        name="merge_out_final" if final else "merge_out",
    )(*args)


def _prep_weights(w_in):
    offs = np.concatenate([[0], np.cumsum(IN_SPLITS)])
    seg = lambda i, j: w_in[:, :, offs[i]:offs[j]].astype(BF16).transpose(0, 2, 1)
    assert offs[5] == OFF_GZ and offs[15] - offs[6] == OFF_GAB - OFF_GZ
    pad = jnp.zeros((DEPTH, PROJ_W - OFF_GAB - IN_SPLITS[5], D_MODEL), BF16)
    w_proj = (seg(0, 5),
              seg(6, 15),
              jnp.concatenate([seg(5, 6), pad], axis=1))
    return w_proj, seg(15, 16)


def _rope_tables(seq):
    t = jnp.arange(seq)
    quarter = HEAD_DIM // 4
    inv = ROPE_THETA ** (-jnp.arange(quarter, dtype=F32) / quarter)

    def half(pos):
        ang = pos.astype(F32)[:, None] * inv
        c, s, zero = jnp.cos(ang), jnp.sin(ang), jnp.zeros_like(ang)
        return jnp.concatenate([c, c], -1), jnp.concatenate([-s, zero], -1), jnp.concatenate([zero, s], -1)

    parts = [jnp.concatenate([a, b], -1) for a, b in zip(half(t // GRID_W), half(t % GRID_W))]
    tab = jnp.stack(parts)
    return jnp.tile(tab, (1, 1, N_HEADS)), jnp.tile(tab, (1, 1, KV_HEADS))


def _layer(h2d, batch, seq, mod, pw, layer, ctx, caches, final_norm):
    proj, gab = _inproj_call(h2d, mod, pw["norm_g"], pw["w_proj"], layer, seq)
    kw = KV_HEADS * HEAD_DIM
    if ctx is None:
        emit = "first" if caches is None else "update"
        akv, nkv, sg_all, sr_all = caches or (None, None, None, None)
        oa, *akv = _attn_ctx_call(proj, akv, layer, batch, seq, OFF_AQKV, OFF_AQKV + BRANCH_W,
                                  OFF_AQKV + BRANCH_W + kw, OFF_AZ, KV_HEADS, pw["qn"], pw["kn"])
        od, *nkv = _attn_ctx_call(proj, nkv, layer, batch, seq, OFF_D, OFF_D + BRANCH_W,
                                  OFF_D + 2 * BRANCH_W, OFF_D + 3 * BRANCH_W, N_HEADS)
        ob, sg_all = _gdn_call(proj, gab, pw["cw8"], pw["gdn_par"], pw["gdn_norm"], layer, batch, seq,
                               emit=emit, prev=sg_all)
        oc, sr_all = _ret_call(proj, pw["ret_norm"], layer, batch, seq, emit=emit, prev=sr_all)
        caches = (akv, nkv, sg_all, sr_all)
    else:
        oa = _attn_lat_call(proj, ctx["akt"], ctx["avt"], layer, batch, seq, ctx["qtab"], ctx["ktab"],
                            pw["qn"], pw["kn"])
        od = _na_call(proj, ctx["nkt"], ctx["nvt"], ctx["tb"], layer, batch, seq)
        ob, = _gdn_call(proj, gab, pw["cw8"], pw["gdn_par"], pw["gdn_norm"], layer, batch, seq, s0=ctx["sg"])
        oc, = _ret_call(proj, pw["ret_norm"], layer, batch, seq, s0=ctx["sr"])
    outs = _out_call(h2d, mod, pw["norm_g"], (oa, ob, oc, od), pw["wg"], pw["wb"], pw["wo"], layer, seq, final_norm)
    return outs, caches


def kernel(x_prompt, x_sample, cache_attn_k, cache_attn_v, cache_na_k, cache_na_v, state_gdn, state_ret, c, c_ctx, w_ada, b_ada, norm_g, w_in, conv_w, gdn_a_log, gdn_dt_bias, gdn_norm, attn_q_norm, attn_k_norm, ret_norm, na_bias, w_branch, w_out, final_norm):
    batch, seq, _ = x_prompt.shape
    dbatch, dseq, _ = x_sample.shape
    assert dbatch == 8, "the modulation kernel handles exactly one sublane tile of conditioning rows"

    w_proj, wg = _prep_weights(w_in)
    par = jnp.zeros((DEPTH, 2, 128), F32)
    par = par.at[:, 0, 8:16].set(gdn_dt_bias.reshape(DEPTH, 8)).at[:, 1, 8:16].set(gdn_a_log.reshape(DEPTH, 8))
    pw = dict(
        w_ada=w_ada, b_ada=b_ada.reshape(DEPTH, 1, 3 * D_MODEL), norm_g=norm_g.reshape(DEPTH, 1, D_MODEL),
        w_proj=w_proj, wg=wg, wb=w_branch.astype(BF16), wo=w_out.astype(BF16),
        cw8=jnp.concatenate([conv_w, jnp.zeros((DEPTH, 8 - SHORT_CONV, 3 * BRANCH_W), F32)], axis=1),
        gdn_par=par,
        gdn_norm=jnp.tile(gdn_norm, (1, N_HEADS)).reshape(DEPTH, 1, BRANCH_W),
        ret_norm=ret_norm.reshape(DEPTH, 1, HEAD_DIM),
        qn=jnp.tile(attn_q_norm, (1, N_HEADS)).reshape(DEPTH, 1, BRANCH_W),
        kn=jnp.tile(attn_k_norm, (1, KV_HEADS)).reshape(DEPTH, 1, KV_HEADS * HEAD_DIM))

    cond = jnp.concatenate([jnp.broadcast_to(c_ctx, (8, D_MODEL)), c], axis=0)
    mods = _mod_call(cond, w_ada, pw["b_ada"])

    h = x_prompt.reshape(batch * seq, D_MODEL)
    caches = None
    for l in range(DEPTH):
        outs, caches = _layer(h, batch, seq, mods[l, 0:1].reshape(1, 1, 3 * D_MODEL), pw, l, None, caches,
                              final_norm if l == DEPTH - 1 else None)
        h = outs[0]
    y_prompt = outs[1].reshape(batch, seq, D_MODEL)
    token_major = lambda a: a.transpose(0, 1, 4, 2, 3)
    (akt, avt), (nkt, nvt), new_state_gdn, new_state_ret = caches
    new_attn_k, new_attn_v, new_na_k, new_na_v = (token_major(a) for a in (akt, avt, nkt, nvt))

    qtab, ktab = _rope_tables(dseq)
    feature_major = lambda a: a.transpose(0, 1, 3, 4, 2)
    ctx = dict(akt=feature_major(cache_attn_k), avt=feature_major(cache_attn_v),
               nkt=feature_major(cache_na_k), nvt=feature_major(cache_na_v),
               sg=state_gdn, sr=state_ret, tb=_na_bias_call(na_bias), qtab=qtab, ktab=ktab)
    h = x_sample.reshape(dbatch * dseq, D_MODEL)
    for l in range(DEPTH):
        outs, _ = _layer(h, dbatch, dseq, mods[l, 8:16].reshape(dbatch, 1, 3 * D_MODEL), pw, l, ctx, None,
                         final_norm if l == DEPTH - 1 else None)
        h = outs[0]
    y_sample = outs[1].reshape(dbatch, dseq, D_MODEL)
    return (y_prompt, y_sample, new_attn_k, new_attn_v, new_na_k, new_na_v, new_state_gdn, new_state_ret)
```

```python
import functools

import numpy as np
import jax
import jax.numpy as jnp
from jax import lax
from jax.experimental import pallas as pl
from jax.experimental.pallas import tpu as pltpu

F32 = jnp.float32
BF16 = jnp.bfloat16

D_MODEL = 1024
HEAD_DIM = 64
N_HEADS = 4
KV_HEADS = N_HEADS // 2
BRANCH_W = N_HEADS * HEAD_DIM
N_BRANCH = 4
DEPTH = 2
GRID_W = 64
CHUNK = 64
PREP_CHUNKS = 4
SCAN_UNROLL = 4
assert CHUNK == HEAD_DIM
SHORT_CONV = 5
NA_ROWS = 8
NA_COLS = 16
RET_ROWS_PER_STEP = 1024
CTX_SEQS_PER_STEP = 4
NA_ROWS_PER_STEP = 8
N_DR = 2 * NA_ROWS - 1
N_DC = 2 * NA_COLS - 1
ROPE_THETA = 10000.0
RET_DECAY_BASE = (5.0, 5.5)
RET_TILE = 256
EPS = 1e-6
SCALE = HEAD_DIM ** -0.5
LOG2E = 1.4426950408889634
NEG_INF = float("-inf")

IN_SPLITS = (256, 128, 128, 256, 768, 16, 256, 256, 256, 256, 256, 256, 256, 256, 256, 4096)
PROJ_W = 4096
OFF_AQKV = 0
OFF_AZ = 512
OFF_GQKV = 768
OFF_GZ = 1536
OFF_C = 1792
OFF_D = 2816
OFF_GAB = 3840

V7X_VMEM_BYTES = 64 * 1024 * 1024
MIB = 1024 * 1024


def _params(vmem_mib, n_axes):
    assert vmem_mib * MIB < V7X_VMEM_BYTES
    return pltpu.CompilerParams(dimension_semantics=("arbitrary",) * n_axes,
                                vmem_limit_bytes=vmem_mib * MIB)


def _layer_spec(block, layer, n_grid):
    zeros = (0,) * len(block)
    if n_grid == 1:
        return pl.BlockSpec((None,) + block, lambda i: (layer,) + zeros)
    return pl.BlockSpec((None,) + block, lambda i, j: (layer,) + zeros)


def _mm(a, b):
    return jnp.dot(a.astype(BF16), b.astype(BF16), preferred_element_type=F32)


def _mm_nt(a, b):
    return lax.dot_general(a.astype(BF16), b.astype(BF16), (((1,), (1,)), ((), ())),
                           preferred_element_type=F32)


def _mm_tn(a, b):
    return lax.dot_general(a.astype(BF16), b.astype(BF16), (((0,), (0,)), ((), ())),
                           preferred_element_type=F32)


def _split3(x):
    hi = x.astype(BF16)
    r = x - hi.astype(F32)
    mid = r.astype(BF16)
    lo = (r - mid.astype(F32)).astype(BF16)
    return hi, mid, lo


def _mm_exact(sel, x, terms=3):
    return sum(jnp.dot(sel, part, preferred_element_type=F32) for part in _split3(x)[:terms])


def _mm_exact_lhs(x, sel, terms=3):
    return sum(jnp.dot(part, sel, preferred_element_type=F32) for part in _split3(x)[:terms])


def _silu(x):
    return x * jax.nn.sigmoid(x)


def _head_block_matrix(width, value):
    ri = lax.broadcasted_iota(jnp.int32, (width, width), 0) >> 6
    ci = lax.broadcasted_iota(jnp.int32, (width, width), 1) >> 6
    return jnp.where(ri == ci, value, 0.0).astype(BF16)


def _head_reduce(x, g):
    hi = x.astype(BF16)
    lo = (x - hi.astype(F32)).astype(BF16)
    return jnp.dot(hi, g, preferred_element_type=F32) + jnp.dot(lo, g, preferred_element_type=F32)


def _head_rms(x):
    ms = _head_reduce(x * x, _head_block_matrix(x.shape[1], 1.0 / HEAD_DIM))
    return x * lax.rsqrt(ms + EPS)


def _rope(x, tab_ref):
    w = x.shape[1]
    return (x * tab_ref[0] + pltpu.roll(x, w - 16, 1) * tab_ref[1] + pltpu.roll(x, 16, 1) * tab_ref[2])


def _attend(qs, parts):
    groups = range(len(qs))
    qs = [(q.astype(F32) * (SCALE * LOG2E)).astype(BF16) for q in qs]

    def score(q, part):
        k, _, bias, feature_major = part
        s = _mm(q, k) if feature_major else _mm_nt(q, k)
        return s if bias is None else s + bias

    scores = [[score(qs[g], part) for part in parts[g]] for g in groups]
    m = [functools.reduce(jnp.maximum, [s.max(axis=-1, keepdims=True) for s in scores[g]]) for g in groups]
    p = [[jnp.exp2(s - m[g]) for s in scores[g]] for g in groups]
    den = [sum(x.sum(axis=-1, keepdims=True) for x in p[g]) for g in groups]
    out = [sum(_mm_nt(x, part[1]) if part[3] else _mm(x, part[1]) for x, part in zip(p[g], parts[g]))
           for g in groups]
    return [out[g] / den[g] for g in groups]


def _hs(h):
    return slice(h * HEAD_DIM, (h + 1) * HEAD_DIM)


def _aligned(x, m):
    return x if isinstance(x, int) else pl.multiple_of(x, m)


def _mod_kernel(c_ref, w_ref, b_ref, o_ref):
    o_ref[...] = _mm(_silu(c_ref[...]), w_ref[...]) + b_ref[...]


def _mod_call(cond, w_ada, b_ada3):
    tn = 512
    rows = cond.shape[0]
    return pl.pallas_call(
        _mod_kernel,
        grid=(DEPTH, 3 * D_MODEL // tn),
        in_specs=[pl.BlockSpec((rows, D_MODEL), lambda l, j: (0, 0)),
                  pl.BlockSpec((None, D_MODEL, tn), lambda l, j: (l, 0, j)),
                  pl.BlockSpec((None, 1, tn), lambda l, j: (l, 0, j))],
        out_specs=pl.BlockSpec((None, rows, tn), lambda l, j: (l, 0, j)),
        out_shape=jax.ShapeDtypeStruct((DEPTH, rows, 3 * D_MODEL), F32),
        compiler_params=_params(24, 2),
        name="adaln_mod",
    )(cond, w_ada, b_ada3)


def _modulated_norm(x, mod, g):
    ms = jnp.mean(x * x, axis=-1, keepdims=True)
    y = x * lax.rsqrt(ms + EPS) * g
    return y * (1.0 + mod[:, D_MODEL:2 * D_MODEL]) + mod[:, :D_MODEL]


def _inproj_kernel(x_ref, mod_ref, g_ref, *refs):
    *w_refs, o_ref, ab_ref = refs
    hn = _modulated_norm(x_ref[...], mod_ref[0], g_ref[...]).astype(BF16)
    tn = 256
    col = 0
    for w_ref in w_refs:
        for j in range(w_ref.shape[0] // tn):
            y = _mm_nt(hn, w_ref[j * tn:(j + 1) * tn, :])
            o_ref[:, col:col + tn] = y.astype(BF16)
            if col == OFF_GAB:
                ab_ref[...] = y[:, :128]
            col += tn
    assert col == PROJ_W


def _inproj_call(x2d, mod3, norm_g3, weights, layer, rows_per_mod):
    t = x2d.shape[0]
    tm = 1024
    if mod3.shape[0] == 1:
        mod_idx = lambda i: (0, 0, 0)
    else:
        mod_idx = lambda i: ((i * tm) // rows_per_mod, 0, 0)
    w_specs = [pl.BlockSpec((None, w.shape[1], D_MODEL), lambda i: (layer, 0, 0), pipeline_mode=pl.Buffered(1))
               for w in weights]
    return pl.pallas_call(
        _inproj_kernel,
        grid=(t // tm,),
        in_specs=[pl.BlockSpec((tm, D_MODEL), lambda i: (i, 0)),
                  pl.BlockSpec((1, 1, 3 * D_MODEL), mod_idx),
                  _layer_spec((1, D_MODEL), layer, 1)] + w_specs,
        out_specs=[pl.BlockSpec((tm, PROJ_W), lambda i: (i, 0)), pl.BlockSpec((tm, 128), lambda i: (i, 0))],
        out_shape=[jax.ShapeDtypeStruct((t, PROJ_W), BF16), jax.ShapeDtypeStruct((t, 128), F32)],
        compiler_params=_params(40, 1),
        name="inproj",
    )(x2d, mod3, norm_g3, *weights)


def _stacked_heads(q, n_kv):
    rep = N_HEADS // n_kv
    return [jnp.concatenate([q[:, _hs(g * rep + r)] for r in range(rep)], axis=0) for g in range(n_kv)]


def _unstack_heads(outs, n_kv):
    rep = N_HEADS // n_kv
    m = outs[0].shape[0] // rep
    return jnp.concatenate([outs[g][r * m:(r + 1) * m] for g in range(n_kv) for r in range(rep)], axis=-1)


def _write_layer(ref, layer, value, stacked):
    if not stacked:
        ref[...] = value
        return
    for l in range(ref.shape[0]):
        ref[l] = value if l == layer else jnp.zeros(value.shape, value.dtype)


def _write_state(st_ref, layer, stacked, piece):
    for d in range(2):
        for h in range(N_HEADS):
            value = piece(d, h)
            if stacked:
                for l in range(st_ref.shape[0]):
                    st_ref[l, d, h] = value if l == layer else jnp.zeros(value.shape, value.dtype)
            else:
                st_ref[d, h] = value


def _attn_ctx_kernel(*refs, n_kv, norm, layer, first):
    if norm:
        q_ref, k_ref, v_ref, z_ref, qn_ref, kn_ref = refs[:6]
    else:
        q_ref, k_ref, v_ref, z_ref = refs[:4]
    o_ref, kt_ref, vt_ref = refs[-3:]
    q, k, v, z = q_ref[...], k_ref[...].astype(F32), v_ref[...].astype(F32), z_ref[...].astype(F32)
    if norm:
        q = _head_rms(q.astype(F32)) * qn_ref[...]
        k = _head_rms(k) * kn_ref[...]
    n_seq = kt_ref.shape[0]
    seq = k.shape[0] // n_seq
    qs, parts = [], []
    for s in range(n_seq):
        rows = slice(s * seq, (s + 1) * seq)
        _write_layer(kt_ref.at[s], layer, k[rows].T.reshape(n_kv, HEAD_DIM, seq), first)
        _write_layer(vt_ref.at[s], layer, v[rows].T.reshape(n_kv, HEAD_DIM, seq), first)
        qs += _stacked_heads(q[rows], n_kv)
        parts += [[(k[rows, _hs(g)], v[rows, _hs(g)], None, False)] for g in range(n_kv)]
    outs = _attend(qs, parts)
    o = jnp.concatenate([_unstack_heads(outs[s * n_kv:(s + 1) * n_kv], n_kv) for s in range(n_seq)], axis=0)
    o_ref[...] = (o * _silu(z)).astype(BF16)


def _attn_ctx_call(proj, prev, layer, batch, seq, off_q, off_k, off_v, off_z, n_kv, qn=None, kn=None):
    t = batch * seq
    kvw = n_kv * HEAD_DIM
    norm = qn is not None
    first = prev is None
    n_seq = CTX_SEQS_PER_STEP
    assert batch % n_seq == 0
    tm = n_seq * seq
    in_specs = [pl.BlockSpec((tm, BRANCH_W), lambda b: (b, off_q // BRANCH_W)),
                pl.BlockSpec((tm, kvw), lambda b: (b, off_k // kvw)),
                pl.BlockSpec((tm, kvw), lambda b: (b, off_v // kvw)),
                pl.BlockSpec((tm, BRANCH_W), lambda b: (b, off_z // BRANCH_W))]
    args = [proj, proj, proj, proj]
    if norm:
        in_specs += [_layer_spec((1, BRANCH_W), layer, 1), _layer_spec((1, kvw), layer, 1)]
        args += [qn, kn]
    aliases = {}
    if first:
        cache_spec = pl.BlockSpec((n_seq, DEPTH, n_kv, HEAD_DIM, seq), lambda b: (b, 0, 0, 0, 0))
    else:
        aliases = {len(args): 1, len(args) + 1: 2}
        in_specs += [pl.BlockSpec(memory_space=pl.ANY)] * 2
        args += list(prev)
        cache_spec = pl.BlockSpec((n_seq, None, n_kv, HEAD_DIM, seq), lambda b: (b, layer, 0, 0, 0))
    cache_shape = jax.ShapeDtypeStruct((batch, DEPTH, n_kv, HEAD_DIM, seq), F32)
    return pl.pallas_call(
        functools.partial(_attn_ctx_kernel, n_kv=n_kv, norm=norm, layer=layer, first=first),
        grid=(batch // n_seq,), in_specs=in_specs,
        out_specs=[pl.BlockSpec((tm, BRANCH_W), lambda b: (b, 0)), cache_spec, cache_spec],
        out_shape=[jax.ShapeDtypeStruct((t, BRANCH_W), BF16), cache_shape, cache_shape],
        input_output_aliases=aliases,
        compiler_params=_params(32, 1),
        name="attn_ctx_norm" if norm else "attn_ctx",
    )(*args)


def _attn_lat_kernel(q_ref, kv_ref, z_ref, ckt_ref, cvt_ref, qtab_ref, ktab_ref, qn_ref, kn_ref, o_ref,
                     k_s, v_s):
    kw = KV_HEADS * HEAD_DIM

    @pl.when(pl.program_id(1) == 0)
    def _():
        kv = kv_ref[...]
        k_s[...] = _rope(_head_rms(kv[:, :kw].astype(F32)) * kn_ref[...], ktab_ref).astype(BF16)
        v_s[...] = kv[:, kw:]

    q = _rope(_head_rms(q_ref[...].astype(F32)) * qn_ref[...], qtab_ref)
    k, v = k_s[...], v_s[...]
    outs = _attend(_stacked_heads(q, KV_HEADS),
                   [[(k[:, _hs(g)], v[:, _hs(g)], None, False), (ckt_ref[g], cvt_ref[g], None, True)]
                    for g in range(KV_HEADS)])
    o_ref[...] = (_unstack_heads(outs, KV_HEADS) * _silu(z_ref[...].astype(F32))).astype(BF16)


def _attn_lat_call(proj, cache_kt, cache_vt, layer, batch, seq, qtab, ktab, qn, kn):
    tq = 512
    nq = seq // tq
    past = cache_kt.shape[-1]
    kw = KV_HEADS * HEAD_DIM
    ctx_spec = pl.BlockSpec((None, None, KV_HEADS, HEAD_DIM, past), lambda b, i: (b, layer, 0, 0, 0))
    return pl.pallas_call(
        _attn_lat_kernel,
        grid=(batch, nq),
        in_specs=[pl.BlockSpec((tq, BRANCH_W), lambda b, i: (b * nq + i, OFF_AQKV // BRANCH_W)),
                  pl.BlockSpec((seq, 2 * kw), lambda b, i: (b, (OFF_AQKV + BRANCH_W) // (2 * kw))),
                  pl.BlockSpec((tq, BRANCH_W), lambda b, i: (b * nq + i, OFF_AZ // BRANCH_W)),
                  ctx_spec, ctx_spec,
                  pl.BlockSpec((3, tq, BRANCH_W), lambda b, i: (0, i, 0)),
                  pl.BlockSpec((3, seq, kw), lambda b, i: (0, 0, 0)),
                  _layer_spec((1, BRANCH_W), layer, 2),
                  _layer_spec((1, kw), layer, 2)],
        out_specs=pl.BlockSpec((tq, BRANCH_W), lambda b, i: (b * nq + i, 0)),
        out_shape=jax.ShapeDtypeStruct((batch * seq, BRANCH_W), BF16),
        scratch_shapes=[pltpu.VMEM((seq, kw), BF16), pltpu.VMEM((seq, kw), BF16)],
        compiler_params=_params(40, 2),
        name="attn_lat",
    )(proj, proj, proj, cache_kt, cache_vt, qtab, ktab, qn, kn)


def _na_bias_kernel(t_ref, o_ref):
    nblk = o_ref.shape[0]
    c = lax.broadcasted_iota(jnp.int32, (GRID_W, 2 * GRID_W), 0)
    kc = lax.broadcasted_iota(jnp.int32, (GRID_W, 2 * GRID_W), 1) & (GRID_W - 1)
    cs = jnp.clip(c - NA_COLS // 2, 0, GRID_W - NA_COLS)
    valid = jnp.logical_and(kc >= cs, kc < cs + NA_COLS)

    unroll = 8
    assert nblk % unroll == 0

    def body(i, carry):
        rows8 = t_ref[pl.ds(pl.multiple_of(i * unroll, unroll), unroll), :]
        for u in range(unroll):
            row = jnp.broadcast_to(rows8[u:u + 1, :], (GRID_W, 2 * GRID_W))
            skewed = pltpu.roll(row, 2 * GRID_W - (NA_COLS - 1), 1, stride=1, stride_axis=0)
            o_ref[i * unroll + u] = jnp.where(valid, skewed * LOG2E, NEG_INF)
        return carry

    lax.fori_loop(0, nblk // unroll, body, 0)


def _na_bias_call(na_bias):
    nblk = DEPTH * N_HEADS * N_DR
    rows = jnp.pad(na_bias.reshape(nblk, N_DC), ((0, 1), (0, GRID_W - N_DC)))
    pairs = jnp.concatenate([rows[:-1], rows[1:]], axis=1)
    return pl.pallas_call(
        _na_bias_kernel,
        in_specs=[pl.BlockSpec((nblk, 2 * GRID_W), lambda: (0, 0))],
        out_specs=pl.BlockSpec((nblk, GRID_W, 2 * GRID_W), lambda: (0, 0, 0)),
        out_shape=jax.ShapeDtypeStruct((nblk, GRID_W, 2 * GRID_W), F32),
        name="na_bias",
    )(pairs)


def _na_kernel(q_ref, k_ref, v_ref, z_ref, ckt_ref, cvt_ref, tb_ref, o_ref, kh_s, vh_s, *, rows):
    win = NA_ROWS * GRID_W

    @pl.when(pl.program_id(1) == 0)
    def _():
        for h in range(N_HEADS):
            kh_s[h] = k_ref[:, _hs(h)]
            vh_s[h] = v_ref[:, _hs(h)]

    qs, parts = [], []
    for i in range(NA_ROWS_PER_STEP):
        r = pl.program_id(1) * NA_ROWS_PER_STEP + i
        rs = jnp.clip(r - NA_ROWS // 2, 0, rows - NA_ROWS)
        r0 = pl.multiple_of(rs * GRID_W, GRID_W)
        q = q_ref[i * GRID_W:(i + 1) * GRID_W, :]
        dr0 = rs - r + NA_ROWS - 1
        for h in range(N_HEADS):
            bias = jnp.concatenate([tb_ref[h * N_DR + dr0 + 2 * p] for p in range(NA_ROWS // 2)], axis=1)
            qs.append(q[:, _hs(h)])
            parts.append([(kh_s[h, pl.ds(r0, win), :], vh_s[h, pl.ds(r0, win), :], bias, False),
                          (ckt_ref[h], cvt_ref[h], None, True)])
    outs = _attend(qs, parts)
    o = jnp.concatenate([jnp.concatenate(outs[i * N_HEADS:(i + 1) * N_HEADS], axis=-1)
                         for i in range(NA_ROWS_PER_STEP)], axis=0)
    o_ref[...] = (o * _silu(z_ref[...].astype(F32))).astype(BF16)


def _na_call(proj, cache_kt, cache_vt, tb, layer, batch, seq):
    rows = seq // GRID_W
    assert rows >= NA_ROWS and rows % NA_ROWS_PER_STEP == 0
    steps = rows // NA_ROWS_PER_STEP
    tq = NA_ROWS_PER_STEP * GRID_W
    past = cache_kt.shape[-1]
    nblk = N_HEADS * N_DR
    cq = OFF_D // BRANCH_W
    ctx_spec = pl.BlockSpec((None, None, N_HEADS, HEAD_DIM, past), lambda b, r: (b, layer, 0, 0, 0))
    return pl.pallas_call(
        functools.partial(_na_kernel, rows=rows),
        grid=(batch, steps),
        in_specs=[pl.BlockSpec((tq, BRANCH_W), lambda b, r: (b * steps + r, cq)),
                  pl.BlockSpec((seq, BRANCH_W), lambda b, r: (b, cq + 1)),
                  pl.BlockSpec((seq, BRANCH_W), lambda b, r: (b, cq + 2)),
                  pl.BlockSpec((tq, BRANCH_W), lambda b, r: (b * steps + r, cq + 3)),
                  ctx_spec, ctx_spec,
                  pl.BlockSpec((nblk, GRID_W, 2 * GRID_W), lambda b, r: (layer, 0, 0))],
        out_specs=pl.BlockSpec((tq, BRANCH_W), lambda b, r: (b * steps + r, 0)),
        out_shape=jax.ShapeDtypeStruct((batch * seq, BRANCH_W), BF16),
        scratch_shapes=[pltpu.VMEM((N_HEADS, seq, HEAD_DIM), BF16), pltpu.VMEM((N_HEADS, seq, HEAD_DIM), BF16)],
        compiler_params=_params(32, 2),
        name="na_lat",
    )(proj, proj, proj, proj, cache_kt, cache_vt, tb)


def _gdn_kernel(*refs, seq, has_s0, layer, emit):
    qkv_ref, z_ref, ab_ref, cw_ref, par_ref, g_ref = refs[:6]
    s0_ref = refs[6] if has_s0 else None
    (q_s, k_s, v_s, gcb_s, bcb_s, r_s, mc_s, nc_s, qp_s, op_s, egl_s, s_s, oacc_ref) = refs[-13:]
    if emit == "none":
        o_ref, st_ref = refs[-14], None
    else:
        o_ref, st_ref = refs[-15], refs[-14]
    n_chunks = seq // CHUNK
    n_levels = CHUNK.bit_length() - 1
    qkv_w = 3 * BRANCH_W
    half = SHORT_CONV // 2
    pair_w = 2 * HEAD_DIM
    pairs = [slice(p * pair_w, (p + 1) * pair_w) for p in range(BRANCH_W // pair_w)]
    tr = 256
    cpt = tr // CHUNK
    head_sum = _head_block_matrix(BRANCH_W, 1.0)

    gc_i = lax.broadcasted_iota(jnp.int32, (128, BRANCH_W), 0)
    gh_j = lax.broadcasted_iota(jnp.int32, (128, BRANCH_W), 1) >> 6
    sel_beta = [jnp.where(gc_i == gh_j + 4 * d, 1.0, 0.0).astype(BF16) for d in range(2)]
    sel_gate = [jnp.where(gc_i == gh_j + 8 + 4 * d, 1.0, 0.0).astype(BF16) for d in range(2)]
    ti = lax.broadcasted_iota(jnp.int32, (tr, tr), 0)
    tj = lax.broadcasted_iota(jnp.int32, (tr, tr), 1)
    same_chunk = (ti >> 6) == (tj >> 6)
    tri = [jnp.where(jnp.logical_and(same_chunk, ti >= tj), 1.0, 0.0).astype(BF16),
           jnp.where(jnp.logical_and(same_chunk, ti <= tj), 1.0, 0.0).astype(BF16)]
    assert tr == BRANCH_W
    lane_head = lax.broadcasted_iota(jnp.int32, (1, BRANCH_W), 1) >> 6

    halo = 16
    edge = 8
    assert half <= edge
    si = lax.broadcasted_iota(jnp.int32, (tr, tr), 0)
    sj = lax.broadcasted_iota(jnp.int32, (tr, tr), 1)
    ei = lax.broadcasted_iota(jnp.int32, (edge, halo), 0)
    ej = lax.broadcasted_iota(jnp.int32, (edge, halo), 1)
    taps = [j for j in range(SHORT_CONV) if j != half]
    shift = {j: jnp.where(sj == si + (j - half), 1.0, 0.0).astype(BF16) for j in taps}
    shift_before = {j: jnp.where(ej == ei + (halo + j - half), 1.0, 0.0).astype(BF16) for j in taps if j < half}
    shift_after = {j: jnp.where(ej == ei + (j - half - edge), 1.0, 0.0).astype(BF16) for j in taps if j > half}
    for t in range(seq // tr):
        rows = slice(t * tr, (t + 1) * tr)
        x = qkv_ref[rows, :]
        y = x.astype(F32) * cw_ref[half:half + 1, :]
        for j in taps:
            y = y + jnp.dot(shift[j], x, preferred_element_type=F32) * cw_ref[j:j + 1, :]
        if t > 0:
            before = qkv_ref[t * tr - halo:t * tr, :]
            top = sum(jnp.dot(shift_before[j], before, preferred_element_type=F32) * cw_ref[j:j + 1, :]
                      for j in shift_before)
            y = jnp.concatenate([y[:edge] + top, y[edge:]], axis=0)
        if (t + 1) * tr < seq:
            after = qkv_ref[(t + 1) * tr:(t + 1) * tr + halo, :]
            bottom = sum(jnp.dot(shift_after[j], after, preferred_element_type=F32) * cw_ref[j:j + 1, :]
                         for j in shift_after)
            y = jnp.concatenate([y[:tr - edge], y[tr - edge:] + bottom], axis=0)
        y = _silu(y)
        qq, kk = y[:, :BRANCH_W], y[:, BRANCH_W:2 * BRANCH_W]
        q_s[rows, :] = qq * lax.rsqrt(_head_reduce(qq * qq, head_sum) + EPS) * SCALE
        k_s[rows, :] = kk * lax.rsqrt(_head_reduce(kk * kk, head_sum) + EPS)
        v_s[rows, :] = y[:, 2 * BRANCH_W:]
        x = ab_ref[rows, :]
        beta = jax.nn.sigmoid(x)
        xs = x + par_ref[0:1, :]
        softplus = jnp.maximum(xs, 0.0) + jnp.log1p(jnp.exp(-jnp.abs(xs)))
        la = -jnp.exp(par_ref[1:2, :]) * softplus
        for d in range(2):
            gc = _mm_exact(tri[d], la, terms=2)
            gcb_s[d, rows, :] = _mm_exact_lhs(gc, sel_gate[d], terms=2)
            bcb_s[d, rows, :] = _mm_exact_lhs(beta, sel_beta[d], terms=2)
            gt = gc.T[8:16, :]
            shifted = {s: (gt if s == 0 else pltpu.roll(gt, (s * HEAD_DIM) % tr, 1))
                       for s in range(1 - cpt, N_HEADS)}
            for c in range(cpt):
                r = jnp.zeros((1, BRANCH_W), F32)
                for h in range(N_HEADS):
                    r = jnp.where(lane_head == h, shifted[h - c][4 * d + h:4 * d + h + 1, :], r)
                r_s[d, (t * cpt + c) * 8:(t * cpt + c + 1) * 8, :] = jnp.broadcast_to(r, (8, BRANCH_W))

    for d in range(2):
        if has_s0:
            s_s[d] = jnp.concatenate([s0_ref[d, h] for h in range(N_HEADS)], axis=-1)
        else:
            s_s[d] = jnp.zeros((HEAD_DIM, BRANCH_W), F32)

    li = lax.broadcasted_iota(jnp.int32, (CHUNK, BRANCH_W), 0)
    lj = lax.broadcasted_iota(jnp.int32, (CHUNK, BRANCH_W), 1) & (HEAD_DIM - 1)
    incl = (li >= lj, li <= lj)
    strict = (li > lj, li < lj)
    level = [((li ^ lj) >> l) == 1 for l in range(n_levels)]
    first_head = lax.broadcasted_iota(jnp.int32, (CHUNK, pair_w), 1) < HEAD_DIM

    def expand(y):
        yb = y.astype(BF16)
        zero = jnp.zeros((CHUNK, pair_w), BF16)
        return [jnp.concatenate([jnp.where(first_head, yb[:, p], zero), jnp.where(first_head, zero, yb[:, p])],
                                axis=0) for p in pairs]

    def bdmm(x, ybd):
        xb = x.astype(BF16)
        return jnp.concatenate([jnp.dot(xb[:, p], ybd[i], preferred_element_type=F32)
                                for i, p in enumerate(pairs)], axis=1)

    def bdmm_nt(x, ybd):
        xb = x.astype(BF16)
        return jnp.concatenate([lax.dot_general(xb[:, p], ybd[i], (((1,), (1,)), ((), ())),
                                                preferred_element_type=F32)
                                for i, p in enumerate(pairs)], axis=1)

    def bdmm2(x, y1, y2):
        xb = x.astype(BF16)
        e1, e2 = expand(y1), expand(y2)
        outs = [jnp.dot(xb[:, p], jnp.concatenate([e1[i], e2[i]], axis=1), preferred_element_type=F32)
                for i, p in enumerate(pairs)]
        return (jnp.concatenate([o[:, :pair_w] for o in outs], axis=1),
                jnp.concatenate([o[:, pair_w:] for o in outs], axis=1))

    def tn_diag2(a, b1, b2):
        ab, b1b, b2b = a.astype(BF16), b1.astype(BF16), b2.astype(BF16)
        outs1, outs2 = [], []
        for p in pairs:
            full = lax.dot_general(ab[:, p], jnp.concatenate([b1b[:, p], b2b[:, p]], axis=1),
                                   (((0,), (0,)), ((), ())), preferred_element_type=F32)
            outs1.append(jnp.where(first_head, full[:HEAD_DIM, :pair_w], full[HEAD_DIM:, :pair_w]))
            outs2.append(jnp.where(first_head, full[:HEAD_DIM, pair_w:], full[HEAD_DIM:, pair_w:]))
        return jnp.concatenate(outs1, axis=1), jnp.concatenate(outs2, axis=1)

    def prepare(chains):
        n = range(len(chains))
        dd = [d for d, _ in chains]
        rows = [pl.ds(_aligned(c * CHUNK, CHUNK), CHUNK) for _, c in chains]
        gcb = [gcb_s[dd[i], rows[i], :] for i in n]
        bcb = [bcb_s[dd[i], rows[i], :] for i in n]
        grow = [r_s[dd[i], pl.ds(_aligned(chains[i][1] * 8, 8), 8), :][0:1, :] for i in n]
        dm = [jnp.exp(jnp.where(incl[dd[i]], gcb[i] - grow[i], NEG_INF)) for i in n]
        k = [k_s[rows[i], :] for i in n]
        q = [q_s[rows[i], :] for i in n]
        v = [v_s[rows[i], :] for i in n]
        kq = [bdmm_nt(jnp.concatenate([k[i], q[i]], axis=0), expand(k[i])) for i in n]
        a = [jnp.where(strict[dd[i]], bcb[i] * kq[i][:CHUNK] * dm[i], 0.0) for i in n]
        tm = [-jnp.where(level[0], a[i], 0.0) for i in n]
        for l in range(1, n_levels):
            b = [jnp.where(level[l], a[i], 0.0) for i in n]
            y = [b[i] + bdmm(tm[i], expand(b[i])) for i in n]
            tm = [tm[i] - (y[i] + bdmm(y[i], expand(tm[i]))) for i in n]
        eg = [jnp.exp(gcb[i]) for i in n]
        bv = [bcb[i] * v[i] for i in n]
        bk = [bcb[i] * k[i] * eg[i] for i in n]
        tuw = [bdmm2(tm[i], bv[i], bk[i]) for i in n]
        u = [bv[i] + tuw[i][0] for i in n]
        w = [bk[i] + tuw[i][1] for i in n]
        gl = [gcb[i][CHUNK - 1:CHUNK, :] if dd[i] == 0 else gcb[i][0:1, :] for i in n]
        kd = [k[i] * jnp.exp(gl[i] - gcb[i]) for i in n]
        qkm = [kq[i][CHUNK:] * dm[i] for i in n]
        mnc = [tn_diag2(kd[i], w[i], u[i]) for i in n]
        mc = [mnc[i][0] for i in n]
        nc = [mnc[i][1] for i in n]
        qwu = [bdmm2(qkm[i], w[i], u[i]) for i in n]
        qp = [q[i] * eg[i] - qwu[i][0] for i in n]
        op = [qwu[i][1] for i in n]
        for i in n:
            d, c = chains[i]
            mc_s[d, rows[i], :] = mc[i].astype(BF16)
            nc_s[d, rows[i], :] = nc[i]
            qp_s[d, rows[i], :] = qp[i].astype(BF16)
            op_s[d, rows[i], :] = op[i]
            egl_s[d, pl.ds(_aligned(c * 8, 8), 8), :] = jnp.broadcast_to(jnp.exp(gl[i]), (8, BRANCH_W))

    group = min(PREP_CHUNKS, n_chunks)
    if n_chunks == group:
        prepare([(d, c) for c in range(group) for d in range(2)])
    else:
        def prep_body(j, carry):
            prepare([(d, j * group + c) for c in range(group) for d in range(2)])
            return carry
        lax.fori_loop(0, n_chunks // group, prep_body, 0)

    def scan_step(i):
        for d, c in ((0, i), (1, n_chunks - 1 - i)):
            rows = pl.ds(_aligned(c * CHUNK, CHUNK), CHUNK)
            s = s_s[d]
            sbd = expand(s)
            oacc_ref[d, rows, :] = bdmm(qp_s[d, rows, :], sbd) + op_s[d, rows, :]
            egl = egl_s[d, pl.ds(_aligned(c * 8, 8), 8), :][0:1, :]
            s_s[d] = s * egl - bdmm(mc_s[d, rows, :], sbd) + nc_s[d, rows, :]

    unroll = min(SCAN_UNROLL, n_chunks)

    def scan_body(j, carry):
        for i in range(unroll):
            scan_step(j * unroll + i)
        return carry

    lax.fori_loop(0, n_chunks // unroll, scan_body, 0)

    if st_ref is not None:
        _write_state(st_ref, layer, emit == "first", lambda d, h: s_s[d][:, _hs(h)])
    o = oacc_ref[0] + oacc_ref[1]
    ms = _head_reduce(o * o, _head_block_matrix(BRANCH_W, 1.0 / HEAD_DIM))
    o_ref[...] = (o * lax.rsqrt(ms + EPS) * g_ref[...] * _silu(z_ref[...].astype(F32))).astype(BF16)


def _state_spec(layer, n_seq=None):
    return pl.BlockSpec((n_seq, None, 2, N_HEADS, HEAD_DIM, HEAD_DIM), lambda b: (b, layer, 0, 0, 0, 0))


def _state_output(emit, prev, layer, batch, n_args, n_seq=None):
    if emit == "none":
        return [], [], [], [], {}
    shape = jax.ShapeDtypeStruct((batch, DEPTH, 2, N_HEADS, HEAD_DIM, HEAD_DIM), F32)
    if emit == "first":
        spec = pl.BlockSpec((n_seq, DEPTH, 2, N_HEADS, HEAD_DIM, HEAD_DIM), lambda b: (b, 0, 0, 0, 0, 0))
        return [], [], [spec], [shape], {}
    return [pl.BlockSpec(memory_space=pl.ANY)], [prev], [_state_spec(layer, n_seq)], [shape], {n_args: 1}


def _gdn_call(proj, gab, cw8, par, norm_g, layer, batch, seq, s0=None, emit="none", prev=None):
    has_s0 = s0 is not None
    qkv_w = 3 * BRANCH_W
    in_specs = [pl.BlockSpec((seq, qkv_w), lambda b: (b, OFF_GQKV // qkv_w)),
                pl.BlockSpec((seq, BRANCH_W), lambda b: (b, OFF_GZ // BRANCH_W)),
                pl.BlockSpec((seq, 128), lambda b: (b, 0)),
                _layer_spec((8, qkv_w), layer, 1),
                _layer_spec((2, 128), layer, 1),
                _layer_spec((1, BRANCH_W), layer, 1)]
    args = [proj, proj, gab, cw8, par, norm_g]
    if has_s0:
        in_specs.append(_state_spec(layer))
        args.append(s0)
    st_in_specs, st_args, st_out_specs, st_shapes, aliases = _state_output(emit, prev, layer, batch, len(args))
    return pl.pallas_call(
        functools.partial(_gdn_kernel, seq=seq, has_s0=has_s0, layer=layer, emit=emit),
        grid=(batch,), in_specs=in_specs + st_in_specs,
        out_specs=[pl.BlockSpec((seq, BRANCH_W), lambda b: (b, 0))] + st_out_specs,
        out_shape=[jax.ShapeDtypeStruct((batch * seq, BRANCH_W), BF16)] + st_shapes,
        input_output_aliases=aliases,
        scratch_shapes=[pltpu.VMEM((seq, BRANCH_W), F32),
                        pltpu.VMEM((seq, BRANCH_W), F32),
                        pltpu.VMEM((seq, BRANCH_W), F32),
                        pltpu.VMEM((2, seq, BRANCH_W), F32),
                        pltpu.VMEM((2, seq, BRANCH_W), F32),
                        pltpu.VMEM((2, seq // CHUNK * 8, BRANCH_W), F32),
                        pltpu.VMEM((2, seq, BRANCH_W), BF16),
                        pltpu.VMEM((2, seq, BRANCH_W), F32),
                        pltpu.VMEM((2, seq, BRANCH_W), BF16),
                        pltpu.VMEM((2, seq, BRANCH_W), F32),
                        pltpu.VMEM((2, seq // CHUNK * 8, BRANCH_W), F32),
                        pltpu.VMEM((2, HEAD_DIM, BRANCH_W), F32),
                        pltpu.VMEM((2, seq, BRANCH_W), F32)],
        compiler_params=_params(48, 1),
        name="gdn",
    )(*args, *st_args)


_RET_LOG_GAMMA = [[float(np.log1p(-np.exp2(-(base + h)))) for h in range(N_HEADS)] for base in RET_DECAY_BASE]


def _ret_kernel(*refs, seq, n_seq, has_s0, layer, emit):
    qkv_refs, z_ref, g_ref = refs[:3], refs[3], refs[4]
    s0_ref = refs[5] if has_s0 else None
    o_ref, st_ref = (refs[-1], None) if emit == "none" else (refs[-2], refs[-1])
    tile = RET_TILE
    n_tiles = seq // tile
    problems = [(s, h) for s in range(n_seq) for h in range(N_HEADS)]
    heads = range(len(problems))
    lgf = [_RET_LOG_GAMMA[0][h] for _, h in problems]
    lgb = [_RET_LOG_GAMMA[1][h] for _, h in problems]
    a = lax.broadcasted_iota(jnp.int32, (tile, 1), 0).astype(F32)
    ef = [jnp.exp(a * lgf[h]) for h in heads]
    eif = [jnp.exp(-a * lgf[h]) for h in heads]
    eb = [jnp.exp(a * lgb[h]) for h in heads]
    eib = [jnp.exp(-a * lgb[h]) for h in heads]
    gf_tile = [float(np.exp(tile * lgf[h])) for h in heads]
    gb_tile = [float(np.exp(tile * lgb[h])) for h in heads]
    ii = lax.broadcasted_iota(jnp.int32, (tile, tile), 0)
    jj = lax.broadcasted_iota(jnp.int32, (tile, tile), 1)

    def head_cols(t, part, i):
        s, h = problems[i]
        return qkv_refs[part][s * seq + t * tile:s * seq + (t + 1) * tile, _hs(h)]

    def initial_state(d, i):
        s, h = problems[i]
        return s0_ref[s, d, h]

    kf = [[head_cols(t, 1, h) * eif[h] for h in heads] for t in range(n_tiles)]
    kb = [[head_cols(t, 1, h) * eb[h] for h in heads] for t in range(n_tiles)]
    vs = [[head_cols(t, 2, h) for h in heads] for t in range(n_tiles)]
    use_states = has_s0 or n_tiles > 1 or st_ref is not None
    if use_states:
        kvf = [[_mm_tn(kf[t][h], vs[t][h]) for h in heads] for t in range(n_tiles)]
        kvb = [[_mm_tn(kb[t][h], vs[t][h]) for h in heads] for t in range(n_tiles)]
        zero = jnp.zeros((HEAD_DIM, HEAD_DIM), F32)
        zf = [[(float(np.exp(lgf[h])) * initial_state(0, h)) if has_s0 else zero for h in heads]]
        for t in range(n_tiles):
            zf.append([gf_tile[h] * (zf[t][h] + kvf[t][h]) for h in heads])
        acc = [initial_state(1, h) if has_s0 else zero for h in heads]
        zb = [None] * n_tiles
        for t in reversed(range(n_tiles)):
            zb[t] = [gb_tile[h] * acc[h] for h in heads]
            acc = [zb[t][h] + kvb[t][h] for h in heads]
        if st_ref is not None:
            stf = [zf[n_tiles][h] * float(np.exp(-lgf[h])) for h in heads]
            for s in range(n_seq):
                _write_state(st_ref.at[s], layer, emit == "first",
                             lambda d, h, s=s: (stf, acc)[d][s * N_HEADS + h])
    tiles = [[] for _ in range(n_seq)]
    for t in range(n_tiles):
        q = [head_cols(t, 0, h) * SCALE for h in heads]
        qf = [q[h] * ef[h] for h in heads]
        qb = [q[h] * eib[h] for h in heads]
        sd = [jnp.where(ii >= jj, _mm_nt(qf[h], kf[t][h]), 0.0) + jnp.where(ii <= jj, _mm_nt(qb[h], kb[t][h]), 0.0)
              for h in heads]
        o = [_mm(sd[h], vs[t][h]) for h in heads]
        if has_s0 or n_tiles > 1:
            o = [o[h] + _mm(qf[h], zf[t][h]) + _mm(qb[h], zb[t][h]) for h in heads]
        ms = [jnp.mean(o[h] * o[h], axis=-1, keepdims=True) for h in heads]
        for s in range(n_seq):
            tiles[s].append(jnp.concatenate([o[h] * lax.rsqrt(ms[h] + EPS) * g_ref[...]
                                             for h in range(s * N_HEADS, (s + 1) * N_HEADS)], axis=-1))
    o = jnp.concatenate([tile_out for s in range(n_seq) for tile_out in tiles[s]], axis=0)
    o_ref[...] = (o * _silu(z_ref[...].astype(F32))).astype(BF16)


def _ret_call(proj, norm_g, layer, batch, seq, s0=None, emit="none", prev=None):
    has_s0 = s0 is not None
    n_seq = max(1, RET_ROWS_PER_STEP // seq)
    assert batch % n_seq == 0
    tm = n_seq * seq
    cq = OFF_C // BRANCH_W
    in_specs = [pl.BlockSpec((tm, BRANCH_W), lambda b, part=part: (b, cq + part)) for part in range(4)]
    in_specs.append(_layer_spec((1, HEAD_DIM), layer, 1))
    args = [proj, proj, proj, proj, norm_g]
    if has_s0:
        in_specs.append(_state_spec(layer, n_seq))
        args.append(s0)
    st_in_specs, st_args, st_out_specs, st_shapes, aliases = _state_output(emit, prev, layer, batch, len(args),
                                                                           n_seq)
    return pl.pallas_call(
        functools.partial(_ret_kernel, seq=seq, n_seq=n_seq, has_s0=has_s0, layer=layer, emit=emit),
        grid=(batch // n_seq,), in_specs=in_specs + st_in_specs,
        out_specs=[pl.BlockSpec((tm, BRANCH_W), lambda b: (b, 0))] + st_out_specs,
        out_shape=[jax.ShapeDtypeStruct((batch * seq, BRANCH_W), BF16)] + st_shapes,
        input_output_aliases=aliases,
        compiler_params=_params(48, 1),
        name="retention",
    )(*args, *st_args)


def _out_kernel(*refs, final):
    if final:
        (h_ref, mod_ref, g_ref, oa_ref, ob_ref, oc_ref, od_ref, wg_ref, wb_ref, wo_ref, fn_ref,
         o_ref, y_ref) = refs
    else:
        h_ref, mod_ref, g_ref, oa_ref, ob_ref, oc_ref, od_ref, wg_ref, wb_ref, wo_ref, o_ref = refs
    x = h_ref[...]
    mod = mod_ref[0]
    hn = _modulated_norm(x, mod, g_ref[...]).astype(BF16)
    merged = None
    for n, br_ref in enumerate((oa_ref, ob_ref, oc_ref, od_ref)):
        gate = jax.nn.sigmoid(_mm_nt(hn, wg_ref[n * D_MODEL:(n + 1) * D_MODEL, :]))
        up = jnp.dot(br_ref[...], wb_ref[n], preferred_element_type=F32)
        merged = gate * up if merged is None else merged + gate * up
    out = jnp.dot(merged.astype(BF16), wo_ref[...], preferred_element_type=F32)
    hnew = x + mod[:, 2 * D_MODEL:] * out
    o_ref[...] = hnew
    if final:
        ms = jnp.mean(hnew * hnew, axis=-1, keepdims=True)
        y_ref[...] = hnew * lax.rsqrt(ms + EPS) * fn_ref[...]


def _out_call(h2d, mod3, norm_g3, branches, wg, wb, wo, layer, rows_per_mod, final_norm=None):
    t = h2d.shape[0]
    tm = 512
    final = final_norm is not None
    if mod3.shape[0] == 1:
        mod_idx = lambda i: (0, 0, 0)
    else:
        mod_idx = lambda i: ((i * tm) // rows_per_mod, 0, 0)
    once = pl.Buffered(1)
    in_specs = [pl.BlockSpec((tm, D_MODEL), lambda i: (i, 0)),
                pl.BlockSpec((1, 1, 3 * D_MODEL), mod_idx),
                _layer_spec((1, D_MODEL), layer, 1)]
    in_specs += [pl.BlockSpec((tm, BRANCH_W), lambda i: (i, 0))] * N_BRANCH
    in_specs += [pl.BlockSpec((None, N_BRANCH * D_MODEL, D_MODEL), lambda i: (layer, 0, 0), pipeline_mode=once),
                 pl.BlockSpec((None, N_BRANCH, BRANCH_W, D_MODEL), lambda i: (layer, 0, 0, 0), pipeline_mode=once),
                 pl.BlockSpec((None, D_MODEL, D_MODEL), lambda i: (layer, 0, 0), pipeline_mode=once)]
    args = [h2d, mod3, norm_g3, *branches, wg, wb, wo]
    out_specs = [pl.BlockSpec((tm, D_MODEL), lambda i: (i, 0))]
    out_shape = [jax.ShapeDtypeStruct((t, D_MODEL), F32)]
    if final:
        in_specs.append(pl.BlockSpec((1, D_MODEL), lambda i: (0, 0)))
        args.append(final_norm.reshape(1, D_MODEL))
        out_specs.append(pl.BlockSpec((tm, D_MODEL), lambda i: (i, 0)))
        out_shape.append(jax.ShapeDtypeStruct((t, D_MODEL), F32))
    return pl.pallas_call(
        functools.partial(_out_kernel, final=final),
        grid=(t // tm,), in_specs=in_specs, out_specs=out_specs, out_shape=out_shape,
        compiler_params=_params(48, 1),
        name="merge_out_final" if final else "merge_out",
    )(*args)


def _prep_weights(w_in):
    offs = np.concatenate([[0], np.cumsum(IN_SPLITS)])
    seg = lambda i, j: w_in[:, :, offs[i]:offs[j]].astype(BF16).transpose(0, 2, 1)
    assert offs[5] == OFF_GZ and offs[15] - offs[6] == OFF_GAB - OFF_GZ
    pad = jnp.zeros((DEPTH, PROJ_W - OFF_GAB - IN_SPLITS[5], D_MODEL), BF16)
    w_proj = (seg(0, 5),
              seg(6, 15),
              jnp.concatenate([seg(5, 6), pad], axis=1))
    return w_proj, seg(15, 16)


def _rope_tables(seq):
    t = jnp.arange(seq)
    quarter = HEAD_DIM // 4
    inv = ROPE_THETA ** (-jnp.arange(quarter, dtype=F32) / quarter)

    def half(pos):
        ang = pos.astype(F32)[:, None] * inv
        c, s, zero = jnp.cos(ang), jnp.sin(ang), jnp.zeros_like(ang)
        return jnp.concatenate([c, c], -1), jnp.concatenate([-s, zero], -1), jnp.concatenate([zero, s], -1)

    parts = [jnp.concatenate([a, b], -1) for a, b in zip(half(t // GRID_W), half(t % GRID_W))]
    tab = jnp.stack(parts)
    return jnp.tile(tab, (1, 1, N_HEADS)), jnp.tile(tab, (1, 1, KV_HEADS))


def _layer(h2d, batch, seq, mod, pw, layer, ctx, caches, final_norm):
    proj, gab = _inproj_call(h2d, mod, pw["norm_g"], pw["w_proj"], layer, seq)
    kw = KV_HEADS * HEAD_DIM
    if ctx is None:
        emit = "first" if caches is None else "update"
        akv, nkv, sg_all, sr_all = caches or (None, None, None, None)
        oa, *akv = _attn_ctx_call(proj, akv, layer, batch, seq, OFF_AQKV, OFF_AQKV + BRANCH_W,
                                  OFF_AQKV + BRANCH_W + kw, OFF_AZ, KV_HEADS, pw["qn"], pw["kn"])
        od, *nkv = _attn_ctx_call(proj, nkv, layer, batch, seq, OFF_D, OFF_D + BRANCH_W,
                                  OFF_D + 2 * BRANCH_W, OFF_D + 3 * BRANCH_W, N_HEADS)
        ob, sg_all = _gdn_call(proj, gab, pw["cw8"], pw["gdn_par"], pw["gdn_norm"], layer, batch, seq,
                               emit=emit, prev=sg_all)
        oc, sr_all = _ret_call(proj, pw["ret_norm"], layer, batch, seq, emit=emit, prev=sr_all)
        caches = (akv, nkv, sg_all, sr_all)
    else:
        oa = _attn_lat_call(proj, ctx["akt"], ctx["avt"], layer, batch, seq, ctx["qtab"], ctx["ktab"],
                            pw["qn"], pw["kn"])
        od = _na_call(proj, ctx["nkt"], ctx["nvt"], ctx["tb"], layer, batch, seq)
        ob, = _gdn_call(proj, gab, pw["cw8"], pw["gdn_par"], pw["gdn_norm"], layer, batch, seq, s0=ctx["sg"])
        oc, = _ret_call(proj, pw["ret_norm"], layer, batch, seq, s0=ctx["sr"])
    outs = _out_call(h2d, mod, pw["norm_g"], (oa, ob, oc, od), pw["wg"], pw["wb"], pw["wo"], layer, seq, final_norm)
    return outs, caches


def kernel(x_prompt, x_sample, cache_attn_k, cache_attn_v, cache_na_k, cache_na_v, state_gdn, state_ret, c, c_ctx, w_ada, b_ada, norm_g, w_in, conv_w, gdn_a_log, gdn_dt_bias, gdn_norm, attn_q_norm, attn_k_norm, ret_norm, na_bias, w_branch, w_out, final_norm):
    batch, seq, _ = x_prompt.shape
    dbatch, dseq, _ = x_sample.shape
    assert dbatch == 8, "the modulation kernel handles exactly one sublane tile of conditioning rows"

    w_proj, wg = _prep_weights(w_in)
    par = jnp.zeros((DEPTH, 2, 128), F32)
    par = par.at[:, 0, 8:16].set(gdn_dt_bias.reshape(DEPTH, 8)).at[:, 1, 8:16].set(gdn_a_log.reshape(DEPTH, 8))
    pw = dict(
        w_ada=w_ada, b_ada=b_ada.reshape(DEPTH, 1, 3 * D_MODEL), norm_g=norm_g.reshape(DEPTH, 1, D_MODEL),
        w_proj=w_proj, wg=wg, wb=w_branch.astype(BF16), wo=w_out.astype(BF16),
        cw8=jnp.concatenate([conv_w, jnp.zeros((DEPTH, 8 - SHORT_CONV, 3 * BRANCH_W), F32)], axis=1),
        gdn_par=par,
        gdn_norm=jnp.tile(gdn_norm, (1, N_HEADS)).reshape(DEPTH, 1, BRANCH_W),
        ret_norm=ret_norm.reshape(DEPTH, 1, HEAD_DIM),
        qn=jnp.tile(attn_q_norm, (1, N_HEADS)).reshape(DEPTH, 1, BRANCH_W),
        kn=jnp.tile(attn_k_norm, (1, KV_HEADS)).reshape(DEPTH, 1, KV_HEADS * HEAD_DIM))

    cond = jnp.concatenate([jnp.broadcast_to(c_ctx, (8, D_MODEL)), c], axis=0)
    mods = _mod_call(cond, w_ada, pw["b_ada"])

    h = x_prompt.reshape(batch * seq, D_MODEL)
    caches = None
    for l in range(DEPTH):
        outs, caches = _layer(h, batch, seq, mods[l, 0:1].reshape(1, 1, 3 * D_MODEL), pw, l, None, caches,
                              final_norm if l == DEPTH - 1 else None)
        h = outs[0]
    y_prompt = outs[1].reshape(batch, seq, D_MODEL)
    token_major = lambda a: a.transpose(0, 1, 4, 2, 3)
    (akt, avt), (nkt, nvt), new_state_gdn, new_state_ret = caches
    new_attn_k, new_attn_v, new_na_k, new_na_v = (token_major(a) for a in (akt, avt, nkt, nvt))

    qtab, ktab = _rope_tables(dseq)
    feature_major = lambda a: a.transpose(0, 1, 3, 4, 2)
    ctx = dict(akt=feature_major(cache_attn_k), avt=feature_major(cache_attn_v),
               nkt=feature_major(cache_na_k), nvt=feature_major(cache_na_v),
               sg=state_gdn, sr=state_ret, tb=_na_bias_call(na_bias), qtab=qtab, ktab=ktab)
    h = x_sample.reshape(dbatch * dseq, D_MODEL)
    for l in range(DEPTH):
        outs, _ = _layer(h, dbatch, dseq, mods[l, 8:16].reshape(dbatch, 1, 3 * D_MODEL), pw, l, ctx, None,
                         final_norm if l == DEPTH - 1 else None)
        h = outs[0]
    y_sample = outs[1].reshape(dbatch, dseq, D_MODEL)
    return (y_prompt, y_sample, new_attn_k, new_attn_v, new_na_k, new_na_v, new_state_gdn, new_state_ret)
```

```python
import functools

import numpy as np
import jax
import jax.numpy as jnp
from jax import lax
from jax.experimental import pallas as pl
from jax.experimental.pallas import tpu as pltpu

F32 = jnp.float32
BF16 = jnp.bfloat16

D_MODEL = 1024
HEAD_DIM = 64
N_HEADS = 4
KV_HEADS = N_HEADS // 2
BRANCH_W = N_HEADS * HEAD_DIM
N_BRANCH = 4
DEPTH = 2
GRID_W = 64
CHUNK = 64
PREP_CHUNKS = 4
GDN_TILE_HEADS = 2
assert CHUNK == HEAD_DIM
SHORT_CONV = 5
NA_ROWS = 8
NA_COLS = 16
RET_ROWS_PER_STEP = 1024
CTX_SEQS_PER_STEP = 4
NA_ROWS_PER_STEP = 8
N_DR = 2 * NA_ROWS - 1
N_DC = 2 * NA_COLS - 1
ROPE_THETA = 10000.0
RET_DECAY_BASE = (5.0, 5.5)
RET_TILE = 256
EPS = 1e-6
SCALE = HEAD_DIM ** -0.5
LOG2E = 1.4426950408889634
NEG_INF = float("-inf")

IN_SPLITS = (256, 128, 128, 256, 768, 16, 256, 256, 256, 256, 256, 256, 256, 256, 256, 4096)
PROJ_W = 4096
OFF_AQKV = 0
OFF_AZ = 512
OFF_GQKV = 768
OFF_GZ = 1536
OFF_C = 1792
OFF_D = 2816
OFF_GAB = 3840

V7X_VMEM_BYTES = 64 * 1024 * 1024
MIB = 1024 * 1024


def _params(vmem_mib, n_axes):
    assert vmem_mib * MIB < V7X_VMEM_BYTES
    return pltpu.CompilerParams(dimension_semantics=("arbitrary",) * n_axes,
                                vmem_limit_bytes=vmem_mib * MIB)


def _layer_spec(block, layer, n_grid):
    zeros = (0,) * len(block)
    if n_grid == 1:
        return pl.BlockSpec((None,) + block, lambda i: (layer,) + zeros)
    return pl.BlockSpec((None,) + block, lambda i, j: (layer,) + zeros)


def _mm(a, b):
    return jnp.dot(a.astype(BF16), b.astype(BF16), preferred_element_type=F32)


def _mm_nt(a, b):
    return lax.dot_general(a.astype(BF16), b.astype(BF16), (((1,), (1,)), ((), ())),
                           preferred_element_type=F32)


def _mm_tn(a, b):
    return lax.dot_general(a.astype(BF16), b.astype(BF16), (((0,), (0,)), ((), ())),
                           preferred_element_type=F32)


def _split3(x):
    hi = x.astype(BF16)
    r = x - hi.astype(F32)
    mid = r.astype(BF16)
    lo = (r - mid.astype(F32)).astype(BF16)
    return hi, mid, lo


def _mm_exact(sel, x, terms=3):
    return sum(jnp.dot(sel, part, preferred_element_type=F32) for part in _split3(x)[:terms])


def _mm_exact_lhs(x, sel, terms=3):
    return sum(jnp.dot(part, sel, preferred_element_type=F32) for part in _split3(x)[:terms])


def _silu(x):
    return x * jax.nn.sigmoid(x)


def _head_block_matrix(width, value):
    ri = lax.broadcasted_iota(jnp.int32, (width, width), 0) >> 6
    ci = lax.broadcasted_iota(jnp.int32, (width, width), 1) >> 6
    return jnp.where(ri == ci, value, 0.0).astype(BF16)


def _head_reduce(x, g):
    hi = x.astype(BF16)
    lo = (x - hi.astype(F32)).astype(BF16)
    return jnp.dot(hi, g, preferred_element_type=F32) + jnp.dot(lo, g, preferred_element_type=F32)


def _head_rms(x):
    ms = _head_reduce(x * x, _head_block_matrix(x.shape[1], 1.0 / HEAD_DIM))
    return x * lax.rsqrt(ms + EPS)


def _rope(x, tab_ref):
    w = x.shape[1]
    return (x * tab_ref[0] + pltpu.roll(x, w - 16, 1) * tab_ref[1] + pltpu.roll(x, 16, 1) * tab_ref[2])


def _attend(qs, parts):
    groups = range(len(qs))
    qs = [(q.astype(F32) * (SCALE * LOG2E)).astype(BF16) for q in qs]

    def score(q, part):
        k, _, bias, feature_major = part
        s = _mm(q, k) if feature_major else _mm_nt(q, k)
        return s if bias is None else s + bias

    scores = [[score(qs[g], part) for part in parts[g]] for g in groups]
    m = [functools.reduce(jnp.maximum, [s.max(axis=-1, keepdims=True) for s in scores[g]]) for g in groups]
    p = [[jnp.exp2(s - m[g]) for s in scores[g]] for g in groups]
    den = [sum(x.sum(axis=-1, keepdims=True) for x in p[g]) for g in groups]
    out = [sum(_mm_nt(x, part[1]) if part[3] else _mm(x, part[1]) for x, part in zip(p[g], parts[g]))
           for g in groups]
    return [out[g] / den[g] for g in groups]


def _hs(h):
    return slice(h * HEAD_DIM, (h + 1) * HEAD_DIM)


def _aligned(x, m):
    return x if isinstance(x, int) else pl.multiple_of(x, m)


def _mod_kernel(c_ref, w_ref, b_ref, o_ref):
    o_ref[...] = _mm(_silu(c_ref[...]), w_ref[...]) + b_ref[...]


def _mod_call(cond, w_ada, b_ada3):
    tn = 512
    rows = cond.shape[0]
    return pl.pallas_call(
        _mod_kernel,
        grid=(DEPTH, 3 * D_MODEL // tn),
        in_specs=[pl.BlockSpec((rows, D_MODEL), lambda l, j: (0, 0)),
                  pl.BlockSpec((None, D_MODEL, tn), lambda l, j: (l, 0, j)),
                  pl.BlockSpec((None, 1, tn), lambda l, j: (l, 0, j))],
        out_specs=pl.BlockSpec((None, rows, tn), lambda l, j: (l, 0, j)),
        out_shape=jax.ShapeDtypeStruct((DEPTH, rows, 3 * D_MODEL), F32),
        compiler_params=_params(24, 2),
        name="adaln_mod",
    )(cond, w_ada, b_ada3)


def _modulated_norm(x, mod, g):
    ms = jnp.mean(x * x, axis=-1, keepdims=True)
    y = x * lax.rsqrt(ms + EPS) * g
    return y * (1.0 + mod[:, D_MODEL:2 * D_MODEL]) + mod[:, :D_MODEL]


def _inproj_kernel(x_ref, mod_ref, g_ref, *refs):
    *w_refs, o_ref, ab_ref = refs
    hn = _modulated_norm(x_ref[...], mod_ref[0], g_ref[...]).astype(BF16)
    tn = 256
    col = 0
    for w_ref in w_refs:
        for j in range(w_ref.shape[0] // tn):
            y = _mm_nt(hn, w_ref[j * tn:(j + 1) * tn, :])
            o_ref[:, col:col + tn] = y.astype(BF16)
            if col == OFF_GAB:
                ab_ref[...] = y[:, :128]
            col += tn
    assert col == PROJ_W


def _inproj_call(x2d, mod3, norm_g3, weights, layer, rows_per_mod):
    t = x2d.shape[0]
    tm = 1024
    if mod3.shape[0] == 1:
        mod_idx = lambda i: (0, 0, 0)
    else:
        mod_idx = lambda i: ((i * tm) // rows_per_mod, 0, 0)
    w_specs = [pl.BlockSpec((None, w.shape[1], D_MODEL), lambda i: (layer, 0, 0), pipeline_mode=pl.Buffered(1))
               for w in weights]
    return pl.pallas_call(
        _inproj_kernel,
        grid=(t // tm,),
        in_specs=[pl.BlockSpec((tm, D_MODEL), lambda i: (i, 0)),
                  pl.BlockSpec((1, 1, 3 * D_MODEL), mod_idx),
                  _layer_spec((1, D_MODEL), layer, 1)] + w_specs,
        out_specs=[pl.BlockSpec((tm, PROJ_W), lambda i: (i, 0)), pl.BlockSpec((tm, 128), lambda i: (i, 0))],
        out_shape=[jax.ShapeDtypeStruct((t, PROJ_W), BF16), jax.ShapeDtypeStruct((t, 128), F32)],
        compiler_params=_params(40, 1),
        name="inproj",
    )(x2d, mod3, norm_g3, *weights)


def _stacked_heads(q, n_kv):
    rep = N_HEADS // n_kv
    return [jnp.concatenate([q[:, _hs(g * rep + r)] for r in range(rep)], axis=0) for g in range(n_kv)]


def _unstack_heads(outs, n_kv):
    rep = N_HEADS // n_kv
    m = outs[0].shape[0] // rep
    return jnp.concatenate([outs[g][r * m:(r + 1) * m] for g in range(n_kv) for r in range(rep)], axis=-1)


def _write_layer(ref, layer, value, stacked):
    if not stacked:
        ref[...] = value
        return
    for l in range(ref.shape[0]):
        ref[l] = value if l == layer else jnp.zeros(value.shape, value.dtype)


def _write_state(st_ref, layer, stacked, piece):
    for d in range(2):
        for h in range(N_HEADS):
            value = piece(d, h)
            if stacked:
                for l in range(st_ref.shape[0]):
                    st_ref[l, d, h] = value if l == layer else jnp.zeros(value.shape, value.dtype)
            else:
                st_ref[d, h] = value


def _attn_ctx_kernel(*refs, n_kv, norm, layer, first):
    if norm:
        q_ref, k_ref, v_ref, z_ref, qn_ref, kn_ref = refs[:6]
    else:
        q_ref, k_ref, v_ref, z_ref = refs[:4]
    o_ref, kt_ref, vt_ref = refs[-3:]
    q, k, v, z = q_ref[...], k_ref[...].astype(F32), v_ref[...].astype(F32), z_ref[...].astype(F32)
    if norm:
        q = _head_rms(q.astype(F32)) * qn_ref[...]
        k = _head_rms(k) * kn_ref[...]
    n_seq = kt_ref.shape[0]
    seq = k.shape[0] // n_seq
    qs, parts = [], []
    for s in range(n_seq):
        rows = slice(s * seq, (s + 1) * seq)
        _write_layer(kt_ref.at[s], layer, k[rows].T.reshape(n_kv, HEAD_DIM, seq), first)
        _write_layer(vt_ref.at[s], layer, v[rows].T.reshape(n_kv, HEAD_DIM, seq), first)
        qs += _stacked_heads(q[rows], n_kv)
        parts += [[(k[rows, _hs(g)], v[rows, _hs(g)], None, False)] for g in range(n_kv)]
    outs = _attend(qs, parts)
    o = jnp.concatenate([_unstack_heads(outs[s * n_kv:(s + 1) * n_kv], n_kv) for s in range(n_seq)], axis=0)
    o_ref[...] = (o * _silu(z)).astype(BF16)


def _attn_ctx_call(proj, prev, layer, batch, seq, off_q, off_k, off_v, off_z, n_kv, qn=None, kn=None):
    t = batch * seq
    kvw = n_kv * HEAD_DIM
    norm = qn is not None
    first = prev is None
    n_seq = CTX_SEQS_PER_STEP
    assert batch % n_seq == 0
    tm = n_seq * seq
    in_specs = [pl.BlockSpec((tm, BRANCH_W), lambda b: (b, off_q // BRANCH_W)),
                pl.BlockSpec((tm, kvw), lambda b: (b, off_k // kvw)),
                pl.BlockSpec((tm, kvw), lambda b: (b, off_v // kvw)),
                pl.BlockSpec((tm, BRANCH_W), lambda b: (b, off_z // BRANCH_W))]
    args = [proj, proj, proj, proj]
    if norm:
        in_specs += [_layer_spec((1, BRANCH_W), layer, 1), _layer_spec((1, kvw), layer, 1)]
        args += [qn, kn]
    aliases = {}
    if first:
        cache_spec = pl.BlockSpec((n_seq, DEPTH, n_kv, HEAD_DIM, seq), lambda b: (b, 0, 0, 0, 0))
    else:
        aliases = {len(args): 1, len(args) + 1: 2}
        in_specs += [pl.BlockSpec(memory_space=pl.ANY)] * 2
        args += list(prev)
        cache_spec = pl.BlockSpec((n_seq, None, n_kv, HEAD_DIM, seq), lambda b: (b, layer, 0, 0, 0))
    cache_shape = jax.ShapeDtypeStruct((batch, DEPTH, n_kv, HEAD_DIM, seq), F32)
    return pl.pallas_call(
        functools.partial(_attn_ctx_kernel, n_kv=n_kv, norm=norm, layer=layer, first=first),
        grid=(batch // n_seq,), in_specs=in_specs,
        out_specs=[pl.BlockSpec((tm, BRANCH_W), lambda b: (b, 0)), cache_spec, cache_spec],
        out_shape=[jax.ShapeDtypeStruct((t, BRANCH_W), BF16), cache_shape, cache_shape],
        input_output_aliases=aliases,
        compiler_params=_params(32, 1),
        name="attn_ctx_norm" if norm else "attn_ctx",
    )(*args)


def _attn_lat_kernel(q_ref, kv_ref, z_ref, ckt_ref, cvt_ref, qtab_ref, ktab_ref, qn_ref, kn_ref, o_ref,
                     k_s, v_s):
    kw = KV_HEADS * HEAD_DIM

    @pl.when(pl.program_id(1) == 0)
    def _():
        kv = kv_ref[...]
        k_s[...] = _rope(_head_rms(kv[:, :kw].astype(F32)) * kn_ref[...], ktab_ref).astype(BF16)
        v_s[...] = kv[:, kw:]

    q = _rope(_head_rms(q_ref[...].astype(F32)) * qn_ref[...], qtab_ref)
    k, v = k_s[...], v_s[...]
    outs = _attend(_stacked_heads(q, KV_HEADS),
                   [[(k[:, _hs(g)], v[:, _hs(g)], None, False), (ckt_ref[g], cvt_ref[g], None, True)]
                    for g in range(KV_HEADS)])
    o_ref[...] = (_unstack_heads(outs, KV_HEADS) * _silu(z_ref[...].astype(F32))).astype(BF16)


def _attn_lat_call(proj, cache_kt, cache_vt, layer, batch, seq, qtab, ktab, qn, kn):
    tq = 512
    nq = seq // tq
    past = cache_kt.shape[-1]
    kw = KV_HEADS * HEAD_DIM
    ctx_spec = pl.BlockSpec((None, None, KV_HEADS, HEAD_DIM, past), lambda b, i: (b, layer, 0, 0, 0))
    return pl.pallas_call(
        _attn_lat_kernel,
        grid=(batch, nq),
        in_specs=[pl.BlockSpec((tq, BRANCH_W), lambda b, i: (b * nq + i, OFF_AQKV // BRANCH_W)),
                  pl.BlockSpec((seq, 2 * kw), lambda b, i: (b, (OFF_AQKV + BRANCH_W) // (2 * kw))),
                  pl.BlockSpec((tq, BRANCH_W), lambda b, i: (b * nq + i, OFF_AZ // BRANCH_W)),
                  ctx_spec, ctx_spec,
                  pl.BlockSpec((3, tq, BRANCH_W), lambda b, i: (0, i, 0)),
                  pl.BlockSpec((3, seq, kw), lambda b, i: (0, 0, 0)),
                  _layer_spec((1, BRANCH_W), layer, 2),
                  _layer_spec((1, kw), layer, 2)],
        out_specs=pl.BlockSpec((tq, BRANCH_W), lambda b, i: (b * nq + i, 0)),
        out_shape=jax.ShapeDtypeStruct((batch * seq, BRANCH_W), BF16),
        scratch_shapes=[pltpu.VMEM((seq, kw), BF16), pltpu.VMEM((seq, kw), BF16)],
        compiler_params=_params(40, 2),
        name="attn_lat",
    )(proj, proj, proj, cache_kt, cache_vt, qtab, ktab, qn, kn)


def _na_bias_kernel(t_ref, o_ref):
    nblk = o_ref.shape[0]
    c = lax.broadcasted_iota(jnp.int32, (GRID_W, 2 * GRID_W), 0)
    kc = lax.broadcasted_iota(jnp.int32, (GRID_W, 2 * GRID_W), 1) & (GRID_W - 1)
    cs = jnp.clip(c - NA_COLS // 2, 0, GRID_W - NA_COLS)
    valid = jnp.logical_and(kc >= cs, kc < cs + NA_COLS)

    unroll = 8
    assert nblk % unroll == 0

    def body(i, carry):
        rows8 = t_ref[pl.ds(pl.multiple_of(i * unroll, unroll), unroll), :]
        for u in range(unroll):
            row = jnp.broadcast_to(rows8[u:u + 1, :], (GRID_W, 2 * GRID_W))
            skewed = pltpu.roll(row, 2 * GRID_W - (NA_COLS - 1), 1, stride=1, stride_axis=0)
            o_ref[i * unroll + u] = jnp.where(valid, skewed * LOG2E, NEG_INF)
        return carry

    lax.fori_loop(0, nblk // unroll, body, 0)


def _na_bias_call(na_bias):
    nblk = DEPTH * N_HEADS * N_DR
    rows = jnp.pad(na_bias.reshape(nblk, N_DC), ((0, 1), (0, GRID_W - N_DC)))
    pairs = jnp.concatenate([rows[:-1], rows[1:]], axis=1)
    return pl.pallas_call(
        _na_bias_kernel,
        in_specs=[pl.BlockSpec((nblk, 2 * GRID_W), lambda: (0, 0))],
        out_specs=pl.BlockSpec((nblk, GRID_W, 2 * GRID_W), lambda: (0, 0, 0)),
        out_shape=jax.ShapeDtypeStruct((nblk, GRID_W, 2 * GRID_W), F32),
        name="na_bias",
    )(pairs)


def _na_kernel(q_ref, k_ref, v_ref, z_ref, ckt_ref, cvt_ref, tb_ref, o_ref, kh_s, vh_s, *, rows):
    win = NA_ROWS * GRID_W

    @pl.when(pl.program_id(1) == 0)
    def _():
        for h in range(N_HEADS):
            kh_s[h] = k_ref[:, _hs(h)]
            vh_s[h] = v_ref[:, _hs(h)]

    qs, parts = [], []
    for i in range(NA_ROWS_PER_STEP):
        r = pl.program_id(1) * NA_ROWS_PER_STEP + i
        rs = jnp.clip(r - NA_ROWS // 2, 0, rows - NA_ROWS)
        r0 = pl.multiple_of(rs * GRID_W, GRID_W)
        q = q_ref[i * GRID_W:(i + 1) * GRID_W, :]
        dr0 = rs - r + NA_ROWS - 1
        for h in range(N_HEADS):
            bias = jnp.concatenate([tb_ref[h * N_DR + dr0 + 2 * p] for p in range(NA_ROWS // 2)], axis=1)
            qs.append(q[:, _hs(h)])
            parts.append([(kh_s[h, pl.ds(r0, win), :], vh_s[h, pl.ds(r0, win), :], bias, False),
                          (ckt_ref[h], cvt_ref[h], None, True)])
    outs = _attend(qs, parts)
    o = jnp.concatenate([jnp.concatenate(outs[i * N_HEADS:(i + 1) * N_HEADS], axis=-1)
                         for i in range(NA_ROWS_PER_STEP)], axis=0)
    o_ref[...] = (o * _silu(z_ref[...].astype(F32))).astype(BF16)


def _na_call(proj, cache_kt, cache_vt, tb, layer, batch, seq):
    rows = seq // GRID_W
    assert rows >= NA_ROWS and rows % NA_ROWS_PER_STEP == 0
    steps = rows // NA_ROWS_PER_STEP
    tq = NA_ROWS_PER_STEP * GRID_W
    past = cache_kt.shape[-1]
    nblk = N_HEADS * N_DR
    cq = OFF_D // BRANCH_W
    ctx_spec = pl.BlockSpec((None, None, N_HEADS, HEAD_DIM, past), lambda b, r: (b, layer, 0, 0, 0))
    return pl.pallas_call(
        functools.partial(_na_kernel, rows=rows),
        grid=(batch, steps),
        in_specs=[pl.BlockSpec((tq, BRANCH_W), lambda b, r: (b * steps + r, cq)),
                  pl.BlockSpec((seq, BRANCH_W), lambda b, r: (b, cq + 1)),
                  pl.BlockSpec((seq, BRANCH_W), lambda b, r: (b, cq + 2)),
                  pl.BlockSpec((tq, BRANCH_W), lambda b, r: (b * steps + r, cq + 3)),
                  ctx_spec, ctx_spec,
                  pl.BlockSpec((nblk, GRID_W, 2 * GRID_W), lambda b, r: (layer, 0, 0))],
        out_specs=pl.BlockSpec((tq, BRANCH_W), lambda b, r: (b * steps + r, 0)),
        out_shape=jax.ShapeDtypeStruct((batch * seq, BRANCH_W), BF16),
        scratch_shapes=[pltpu.VMEM((N_HEADS, seq, HEAD_DIM), BF16), pltpu.VMEM((N_HEADS, seq, HEAD_DIM), BF16)],
        compiler_params=_params(32, 2),
        name="na_lat",
    )(proj, proj, proj, proj, cache_kt, cache_vt, tb)


def _gdn_kernel(*refs, seq, has_s0, layer, emit):
    qkv_ref, z_ref, ab_ref, cw_ref, par_ref, g_ref = refs[:6]
    s0_ref = refs[6] if has_s0 else None
    (q_s, k_s, v_s, gcb_s, bcb_s, r_s, mc_s, nc_s, qp_s, op_s, egl_s, s_s, oacc_ref) = refs[-13:]
    if emit == "none":
        o_ref, st_ref = refs[-14], None
    else:
        o_ref, st_ref = refs[-15], refs[-14]
    n_chunks = seq // CHUNK
    n_levels = CHUNK.bit_length() - 1
    qkv_w = 3 * BRANCH_W
    half = SHORT_CONV // 2
    pair_w = GDN_TILE_HEADS * HEAD_DIM
    pairs = [slice(p * pair_w, (p + 1) * pair_w) for p in range(BRANCH_W // pair_w)]
    tr = 256
    cpt = tr // CHUNK
    head_sum = _head_block_matrix(BRANCH_W, 1.0)

    gc_i = lax.broadcasted_iota(jnp.int32, (128, BRANCH_W), 0)
    gh_j = lax.broadcasted_iota(jnp.int32, (128, BRANCH_W), 1) >> 6
    sel_beta = [jnp.where(gc_i == gh_j + 4 * d, 1.0, 0.0).astype(BF16) for d in range(2)]
    sel_gate = [jnp.where(gc_i == gh_j + 8 + 4 * d, 1.0, 0.0).astype(BF16) for d in range(2)]
    ti = lax.broadcasted_iota(jnp.int32, (tr, tr), 0)
    tj = lax.broadcasted_iota(jnp.int32, (tr, tr), 1)
    same_chunk = (ti >> 6) == (tj >> 6)
    tri = [jnp.where(jnp.logical_and(same_chunk, ti >= tj), 1.0, 0.0).astype(BF16),
           jnp.where(jnp.logical_and(same_chunk, ti <= tj), 1.0, 0.0).astype(BF16)]
    assert tr == BRANCH_W
    lane_head = lax.broadcasted_iota(jnp.int32, (1, BRANCH_W), 1) >> 6

    halo = 16
    edge = 8
    assert half <= edge
    si = lax.broadcasted_iota(jnp.int32, (tr, tr), 0)
    sj = lax.broadcasted_iota(jnp.int32, (tr, tr), 1)
    ei = lax.broadcasted_iota(jnp.int32, (edge, halo), 0)
    ej = lax.broadcasted_iota(jnp.int32, (edge, halo), 1)
    taps = [j for j in range(SHORT_CONV) if j != half]
    shift = {j: jnp.where(sj == si + (j - half), 1.0, 0.0).astype(BF16) for j in taps}
    shift_before = {j: jnp.where(ej == ei + (halo + j - half), 1.0, 0.0).astype(BF16) for j in taps if j < half}
    shift_after = {j: jnp.where(ej == ei + (j - half - edge), 1.0, 0.0).astype(BF16) for j in taps if j > half}
    for t in range(seq // tr):
        rows = slice(t * tr, (t + 1) * tr)
        x = qkv_ref[rows, :]
        y = x.astype(F32) * cw_ref[half:half + 1, :]
        for j in taps:
            y = y + jnp.dot(shift[j], x, preferred_element_type=F32) * cw_ref[j:j + 1, :]
        if t > 0:
            before = qkv_ref[t * tr - halo:t * tr, :]
            top = sum(jnp.dot(shift_before[j], before, preferred_element_type=F32) * cw_ref[j:j + 1, :]
                      for j in shift_before)
            y = jnp.concatenate([y[:edge] + top, y[edge:]], axis=0)
        if (t + 1) * tr < seq:
            after = qkv_ref[(t + 1) * tr:(t + 1) * tr + halo, :]
            bottom = sum(jnp.dot(shift_after[j], after, preferred_element_type=F32) * cw_ref[j:j + 1, :]
                         for j in shift_after)
            y = jnp.concatenate([y[:tr - edge], y[tr - edge:] + bottom], axis=0)
        y = _silu(y)
        qq, kk = y[:, :BRANCH_W], y[:, BRANCH_W:2 * BRANCH_W]
        q_s[rows, :] = qq * lax.rsqrt(_head_reduce(qq * qq, head_sum) + EPS) * SCALE
        k_s[rows, :] = kk * lax.rsqrt(_head_reduce(kk * kk, head_sum) + EPS)
        v_s[rows, :] = y[:, 2 * BRANCH_W:]
        x = ab_ref[rows, :]
        beta = jax.nn.sigmoid(x)
        xs = x + par_ref[0:1, :]
        softplus = jnp.maximum(xs, 0.0) + jnp.log1p(jnp.exp(-jnp.abs(xs)))
        la = -jnp.exp(par_ref[1:2, :]) * softplus
        for d in range(2):
            gc = _mm_exact(tri[d], la, terms=2)
            gcb_s[d, rows, :] = _mm_exact_lhs(gc, sel_gate[d], terms=2)
            bcb_s[d, rows, :] = _mm_exact_lhs(beta, sel_beta[d], terms=2)
            gt = gc.T[8:16, :]
            shifted = {s: (gt if s == 0 else pltpu.roll(gt, (s * HEAD_DIM) % tr, 1))
                       for s in range(1 - cpt, N_HEADS)}
            for c in range(cpt):
                r = jnp.zeros((1, BRANCH_W), F32)
                for h in range(N_HEADS):
                    r = jnp.where(lane_head == h, shifted[h - c][4 * d + h:4 * d + h + 1, :], r)
                r_s[d, (t * cpt + c) * 8:(t * cpt + c + 1) * 8, :] = jnp.broadcast_to(r, (8, BRANCH_W))

    for d in range(2):
        if has_s0:
            s_s[d] = jnp.concatenate([s0_ref[d, h] for h in range(N_HEADS)], axis=-1)
        else:
            s_s[d] = jnp.zeros((HEAD_DIM, BRANCH_W), F32)

    li = lax.broadcasted_iota(jnp.int32, (CHUNK, BRANCH_W), 0)
    lj = lax.broadcasted_iota(jnp.int32, (CHUNK, BRANCH_W), 1) & (HEAD_DIM - 1)
    incl = (li >= lj, li <= lj)
    strict = (li > lj, li < lj)
    level = [((li ^ lj) >> l) == 1 for l in range(n_levels)]
    tile_heads = pair_w // HEAD_DIM
    head_in_tile = lax.broadcasted_iota(jnp.int32, (CHUNK, pair_w), 1) >> 6

    def diag_blocks(full):
        out = full[:HEAD_DIM]
        for a in range(1, tile_heads):
            out = jnp.where(head_in_tile == a, full[a * HEAD_DIM:(a + 1) * HEAD_DIM], out)
        return out

    def expand(y):
        yb = y.astype(BF16)
        zero = jnp.zeros((CHUNK, pair_w), BF16)
        return [jnp.concatenate([jnp.where(head_in_tile == a, yb[:, p], zero) for a in range(tile_heads)], axis=0)
                for p in pairs]

    def bdmm(x, ybd):
        xb = x.astype(BF16)
        return jnp.concatenate([jnp.dot(xb[:, p], ybd[i], preferred_element_type=F32)
                                for i, p in enumerate(pairs)], axis=1)

    def bdmm_nt(x, ybd):
        xb = x.astype(BF16)
        return jnp.concatenate([lax.dot_general(xb[:, p], ybd[i], (((1,), (1,)), ((), ())),
                                                preferred_element_type=F32)
                                for i, p in enumerate(pairs)], axis=1)

    def bdmm2(x, y1, y2):
        xb = x.astype(BF16)
        e1, e2 = expand(y1), expand(y2)
        outs = [jnp.dot(xb[:, p], jnp.concatenate([e1[i], e2[i]], axis=1), preferred_element_type=F32)
                for i, p in enumerate(pairs)]
        return (jnp.concatenate([o[:, :pair_w] for o in outs], axis=1),
                jnp.concatenate([o[:, pair_w:] for o in outs], axis=1))

    def tn_diag2(a, b1, b2):
        ab, b1b, b2b = a.astype(BF16), b1.astype(BF16), b2.astype(BF16)
        outs1, outs2 = [], []
        for p in pairs:
            full = lax.dot_general(ab[:, p], jnp.concatenate([b1b[:, p], b2b[:, p]], axis=1),
                                   (((0,), (0,)), ((), ())), preferred_element_type=F32)
            outs1.append(diag_blocks(full[:, :pair_w]))
            outs2.append(diag_blocks(full[:, pair_w:]))
        return jnp.concatenate(outs1, axis=1), jnp.concatenate(outs2, axis=1)

    def prepare(chains):
        n = range(len(chains))
        dd = [d for d, _ in chains]
        rows = [pl.ds(_aligned(c * CHUNK, CHUNK), CHUNK) for _, c in chains]
        gcb = [gcb_s[dd[i], rows[i], :] for i in n]
        bcb = [bcb_s[dd[i], rows[i], :] for i in n]
        grow = [r_s[dd[i], pl.ds(_aligned(chains[i][1] * 8, 8), 8), :][0:1, :] for i in n]
        dm = [jnp.exp(jnp.where(incl[dd[i]], gcb[i] - grow[i], NEG_INF)) for i in n]
        k = [k_s[rows[i], :] for i in n]
        q = [q_s[rows[i], :] for i in n]
        v = [v_s[rows[i], :] for i in n]
        kq = [bdmm_nt(jnp.concatenate([k[i], q[i]], axis=0), expand(k[i])) for i in n]
        a = [jnp.where(strict[dd[i]], bcb[i] * kq[i][:CHUNK] * dm[i], 0.0) for i in n]
        tm = [-jnp.where(level[0], a[i], 0.0) for i in n]
        for l in range(1, n_levels):
            b = [jnp.where(level[l], a[i], 0.0) for i in n]
            y = [b[i] + bdmm(tm[i], expand(b[i])) for i in n]
            tm = [tm[i] - (y[i] + bdmm(y[i], expand(tm[i]))) for i in n]
        eg = [jnp.exp(gcb[i]) for i in n]
        bv = [bcb[i] * v[i] for i in n]
        bk = [bcb[i] * k[i] * eg[i] for i in n]
        tuw = [bdmm2(tm[i], bv[i], bk[i]) for i in n]
        u = [bv[i] + tuw[i][0] for i in n]
        w = [bk[i] + tuw[i][1] for i in n]
        gl = [gcb[i][CHUNK - 1:CHUNK, :] if dd[i] == 0 else gcb[i][0:1, :] for i in n]
        kd = [k[i] * jnp.exp(gl[i] - gcb[i]) for i in n]
        qkm = [kq[i][CHUNK:] * dm[i] for i in n]
        mnc = [tn_diag2(kd[i], w[i], u[i]) for i in n]
        mc = [mnc[i][0] for i in n]
        nc = [mnc[i][1] for i in n]
        qwu = [bdmm2(qkm[i], w[i], u[i]) for i in n]
        qp = [q[i] * eg[i] - qwu[i][0] for i in n]
        op = [qwu[i][1] for i in n]
        for i in n:
            d, c = chains[i]
            mc_s[d, rows[i], :] = mc[i].astype(BF16)
            nc_s[d, rows[i], :] = nc[i]
            qp_s[d, rows[i], :] = qp[i].astype(BF16)
            op_s[d, rows[i], :] = op[i]
            egl_s[d, pl.ds(_aligned(c * 8, 8), 8), :] = jnp.broadcast_to(jnp.exp(gl[i]), (8, BRANCH_W))

    group = min(PREP_CHUNKS, n_chunks)
    n_groups = n_chunks // group

    def prepare_group(j):
        prepare([(d, (j if d == 0 else n_groups - 1 - j) * group + c) for c in range(group) for d in range(2)])

    def scan_step(i):
        for d, c in ((0, i), (1, n_chunks - 1 - i)):
            rows = pl.ds(_aligned(c * CHUNK, CHUNK), CHUNK)
            s = s_s[d]
            sbd = expand(s)
            oacc_ref[d, rows, :] = bdmm(qp_s[d, rows, :], sbd) + op_s[d, rows, :]
            egl = egl_s[d, pl.ds(_aligned(c * 8, 8), 8), :][0:1, :]
            s_s[d] = s * egl - bdmm(mc_s[d, rows, :], sbd) + nc_s[d, rows, :]

    def scan_group(j):
        for i in range(group):
            scan_step(j * group + i)

    prepare_group(0)
    if n_groups > 1:
        def body(j, carry):
            scan_group(j - 1)
            prepare_group(j)
            return carry
        lax.fori_loop(1, n_groups, body, 0)
    scan_group(n_groups - 1)

    if st_ref is not None:
        _write_state(st_ref, layer, emit == "first", lambda d, h: s_s[d][:, _hs(h)])
    o = oacc_ref[0] + oacc_ref[1]
    ms = _head_reduce(o * o, _head_block_matrix(BRANCH_W, 1.0 / HEAD_DIM))
    o_ref[...] = (o * lax.rsqrt(ms + EPS) * g_ref[...] * _silu(z_ref[...].astype(F32))).astype(BF16)


def _state_spec(layer, n_seq=None):
    return pl.BlockSpec((n_seq, None, 2, N_HEADS, HEAD_DIM, HEAD_DIM), lambda b: (b, layer, 0, 0, 0, 0))


def _state_output(emit, prev, layer, batch, n_args, n_seq=None):
    if emit == "none":
        return [], [], [], [], {}
    shape = jax.ShapeDtypeStruct((batch, DEPTH, 2, N_HEADS, HEAD_DIM, HEAD_DIM), F32)
    if emit == "first":
        spec = pl.BlockSpec((n_seq, DEPTH, 2, N_HEADS, HEAD_DIM, HEAD_DIM), lambda b: (b, 0, 0, 0, 0, 0))
        return [], [], [spec], [shape], {}
    return [pl.BlockSpec(memory_space=pl.ANY)], [prev], [_state_spec(layer, n_seq)], [shape], {n_args: 1}


def _gdn_call(proj, gab, cw8, par, norm_g, layer, batch, seq, s0=None, emit="none", prev=None):
    has_s0 = s0 is not None
    qkv_w = 3 * BRANCH_W
    in_specs = [pl.BlockSpec((seq, qkv_w), lambda b: (b, OFF_GQKV // qkv_w)),
                pl.BlockSpec((seq, BRANCH_W), lambda b: (b, OFF_GZ // BRANCH_W)),
                pl.BlockSpec((seq, 128), lambda b: (b, 0)),
                _layer_spec((8, qkv_w), layer, 1),
                _layer_spec((2, 128), layer, 1),
                _layer_spec((1, BRANCH_W), layer, 1)]
    args = [proj, proj, gab, cw8, par, norm_g]
    if has_s0:
        in_specs.append(_state_spec(layer))
        args.append(s0)
    st_in_specs, st_args, st_out_specs, st_shapes, aliases = _state_output(emit, prev, layer, batch, len(args))
    return pl.pallas_call(
        functools.partial(_gdn_kernel, seq=seq, has_s0=has_s0, layer=layer, emit=emit),
        grid=(batch,), in_specs=in_specs + st_in_specs,
        out_specs=[pl.BlockSpec((seq, BRANCH_W), lambda b: (b, 0))] + st_out_specs,
        out_shape=[jax.ShapeDtypeStruct((batch * seq, BRANCH_W), BF16)] + st_shapes,
        input_output_aliases=aliases,
        scratch_shapes=[pltpu.VMEM((seq, BRANCH_W), F32),
                        pltpu.VMEM((seq, BRANCH_W), F32),
                        pltpu.VMEM((seq, BRANCH_W), F32),
                        pltpu.VMEM((2, seq, BRANCH_W), F32),
                        pltpu.VMEM((2, seq, BRANCH_W), F32),
                        pltpu.VMEM((2, seq // CHUNK * 8, BRANCH_W), F32),
                        pltpu.VMEM((2, seq, BRANCH_W), BF16),
                        pltpu.VMEM((2, seq, BRANCH_W), F32),
                        pltpu.VMEM((2, seq, BRANCH_W), BF16),
                        pltpu.VMEM((2, seq, BRANCH_W), F32),
                        pltpu.VMEM((2, seq // CHUNK * 8, BRANCH_W), F32),
                        pltpu.VMEM((2, HEAD_DIM, BRANCH_W), F32),
                        pltpu.VMEM((2, seq, BRANCH_W), F32)],
        compiler_params=_params(48, 1),
        name="gdn",
    )(*args, *st_args)


_RET_LOG_GAMMA = [[float(np.log1p(-np.exp2(-(base + h)))) for h in range(N_HEADS)] for base in RET_DECAY_BASE]


def _ret_kernel(*refs, seq, n_seq, has_s0, layer, emit):
    qkv_refs, z_ref, g_ref = refs[:3], refs[3], refs[4]
    s0_ref = refs[5] if has_s0 else None
    o_ref, st_ref = (refs[-1], None) if emit == "none" else (refs[-2], refs[-1])
    tile = RET_TILE
    n_tiles = seq // tile
    problems = [(s, h) for s in range(n_seq) for h in range(N_HEADS)]
    heads = range(len(problems))
    lgf = [_RET_LOG_GAMMA[0][h] for _, h in problems]
    lgb = [_RET_LOG_GAMMA[1][h] for _, h in problems]
    a = lax.broadcasted_iota(jnp.int32, (tile, 1), 0).astype(F32)
    ef = [jnp.exp(a * lgf[h]) for h in heads]
    eif = [jnp.exp(-a * lgf[h]) for h in heads]
    eb = [jnp.exp(a * lgb[h]) for h in heads]
    eib = [jnp.exp(-a * lgb[h]) for h in heads]
    gf_tile = [float(np.exp(tile * lgf[h])) for h in heads]
    gb_tile = [float(np.exp(tile * lgb[h])) for h in heads]
    ii = lax.broadcasted_iota(jnp.int32, (tile, tile), 0)
    jj = lax.broadcasted_iota(jnp.int32, (tile, tile), 1)

    def head_cols(t, part, i):
        s, h = problems[i]
        return qkv_refs[part][s * seq + t * tile:s * seq + (t + 1) * tile, _hs(h)]

    def initial_state(d, i):
        s, h = problems[i]
        return s0_ref[s, d, h]

    kf = [[head_cols(t, 1, h) * eif[h] for h in heads] for t in range(n_tiles)]
    kb = [[head_cols(t, 1, h) * eb[h] for h in heads] for t in range(n_tiles)]
    vs = [[head_cols(t, 2, h) for h in heads] for t in range(n_tiles)]
    use_states = has_s0 or n_tiles > 1 or st_ref is not None
    if use_states:
        kvf = [[_mm_tn(kf[t][h], vs[t][h]) for h in heads] for t in range(n_tiles)]
        kvb = [[_mm_tn(kb[t][h], vs[t][h]) for h in heads] for t in range(n_tiles)]
        zero = jnp.zeros((HEAD_DIM, HEAD_DIM), F32)
        zf = [[(float(np.exp(lgf[h])) * initial_state(0, h)) if has_s0 else zero for h in heads]]
        for t in range(n_tiles):
            zf.append([gf_tile[h] * (zf[t][h] + kvf[t][h]) for h in heads])
        acc = [initial_state(1, h) if has_s0 else zero for h in heads]
        zb = [None] * n_tiles
        for t in reversed(range(n_tiles)):
            zb[t] = [gb_tile[h] * acc[h] for h in heads]
            acc = [zb[t][h] + kvb[t][h] for h in heads]
        if st_ref is not None:
            stf = [zf[n_tiles][h] * float(np.exp(-lgf[h])) for h in heads]
            for s in range(n_seq):
                _write_state(st_ref.at[s], layer, emit == "first",
                             lambda d, h, s=s: (stf, acc)[d][s * N_HEADS + h])
    tiles = [[] for _ in range(n_seq)]
    for t in range(n_tiles):
        q = [head_cols(t, 0, h) * SCALE for h in heads]
        qf = [q[h] * ef[h] for h in heads]
        qb = [q[h] * eib[h] for h in heads]
        sd = [jnp.where(ii >= jj, _mm_nt(qf[h], kf[t][h]), 0.0) + jnp.where(ii <= jj, _mm_nt(qb[h], kb[t][h]), 0.0)
              for h in heads]
        o = [_mm(sd[h], vs[t][h]) for h in heads]
        if has_s0 or n_tiles > 1:
            o = [o[h] + _mm(qf[h], zf[t][h]) + _mm(qb[h], zb[t][h]) for h in heads]
        ms = [jnp.mean(o[h] * o[h], axis=-1, keepdims=True) for h in heads]
        for s in range(n_seq):
            tiles[s].append(jnp.concatenate([o[h] * lax.rsqrt(ms[h] + EPS) * g_ref[...]
                                             for h in range(s * N_HEADS, (s + 1) * N_HEADS)], axis=-1))
    o = jnp.concatenate([tile_out for s in range(n_seq) for tile_out in tiles[s]], axis=0)
    o_ref[...] = (o * _silu(z_ref[...].astype(F32))).astype(BF16)


def _ret_call(proj, norm_g, layer, batch, seq, s0=None, emit="none", prev=None):
    has_s0 = s0 is not None
    n_seq = max(1, RET_ROWS_PER_STEP // seq)
    assert batch % n_seq == 0
    tm = n_seq * seq
    cq = OFF_C // BRANCH_W
    in_specs = [pl.BlockSpec((tm, BRANCH_W), lambda b, part=part: (b, cq + part)) for part in range(4)]
    in_specs.append(_layer_spec((1, HEAD_DIM), layer, 1))
    args = [proj, proj, proj, proj, norm_g]
    if has_s0:
        in_specs.append(_state_spec(layer, n_seq))
        args.append(s0)
    st_in_specs, st_args, st_out_specs, st_shapes, aliases = _state_output(emit, prev, layer, batch, len(args),
                                                                           n_seq)
    return pl.pallas_call(
        functools.partial(_ret_kernel, seq=seq, n_seq=n_seq, has_s0=has_s0, layer=layer, emit=emit),
        grid=(batch // n_seq,), in_specs=in_specs + st_in_specs,
        out_specs=[pl.BlockSpec((tm, BRANCH_W), lambda b: (b, 0))] + st_out_specs,
        out_shape=[jax.ShapeDtypeStruct((batch * seq, BRANCH_W), BF16)] + st_shapes,
        input_output_aliases=aliases,
        compiler_params=_params(48, 1),
        name="retention",
    )(*args, *st_args)


def _out_kernel(*refs, final):
    if final:
        (h_ref, mod_ref, g_ref, oa_ref, ob_ref, oc_ref, od_ref, wg_ref, wb_ref, wo_ref, fn_ref,
         o_ref, y_ref) = refs
    else:
        h_ref, mod_ref, g_ref, oa_ref, ob_ref, oc_ref, od_ref, wg_ref, wb_ref, wo_ref, o_ref = refs
    x = h_ref[...]
    mod = mod_ref[0]
    hn = _modulated_norm(x, mod, g_ref[...]).astype(BF16)
    merged = None
    for n, br_ref in enumerate((oa_ref, ob_ref, oc_ref, od_ref)):
        gate = jax.nn.sigmoid(_mm_nt(hn, wg_ref[n * D_MODEL:(n + 1) * D_MODEL, :]))
        up = jnp.dot(br_ref[...], wb_ref[n], preferred_element_type=F32)
        merged = gate * up if merged is None else merged + gate * up
    out = jnp.dot(merged.astype(BF16), wo_ref[...], preferred_element_type=F32)
    hnew = x + mod[:, 2 * D_MODEL:] * out
    o_ref[...] = hnew
    if final:
        ms = jnp.mean(hnew * hnew, axis=-1, keepdims=True)
        y_ref[...] = hnew * lax.rsqrt(ms + EPS) * fn_ref[...]


def _out_call(h2d, mod3, norm_g3, branches, wg, wb, wo, layer, rows_per_mod, final_norm=None):
    t = h2d.shape[0]
    tm = 512
    final = final_norm is not None
    if mod3.shape[0] == 1:
        mod_idx = lambda i: (0, 0, 0)
    else:
        mod_idx = lambda i: ((i * tm) // rows_per_mod, 0, 0)
    once = pl.Buffered(1)
    in_specs = [pl.BlockSpec((tm, D_MODEL), lambda i: (i, 0)),
                pl.BlockSpec((1, 1, 3 * D_MODEL), mod_idx),
                _layer_spec((1, D_MODEL), layer, 1)]
    in_specs += [pl.BlockSpec((tm, BRANCH_W), lambda i: (i, 0))] * N_BRANCH
    in_specs += [pl.BlockSpec((None, N_BRANCH * D_MODEL, D_MODEL), lambda i: (layer, 0, 0), pipeline_mode=once),
                 pl.BlockSpec((None, N_BRANCH, BRANCH_W, D_MODEL), lambda i: (layer, 0, 0, 0), pipeline_mode=once),
                 pl.BlockSpec((None, D_MODEL, D_MODEL), lambda i: (layer, 0, 0), pipeline_mode=once)]
    args = [h2d, mod3, norm_g3, *branches, wg, wb, wo]
    out_specs = [pl.BlockSpec((tm, D_MODEL), lambda i: (i, 0))]
    out_shape = [jax.ShapeDtypeStruct((t, D_MODEL), F32)]
    if final:
        in_specs.append(pl.BlockSpec((1, D_MODEL), lambda i: (0, 0)))
        args.append(final_norm.reshape(1, D_MODEL))
        out_specs.append(pl.BlockSpec((tm, D_MODEL), lambda i: (i, 0)))
        out_shape.append(jax.ShapeDtypeStruct((t, D_MODEL), F32))
    return pl.pallas_call(
        functools.partial(_out_kernel, final=final),
        grid=(t // tm,), in_specs=in_specs, out_specs=out_specs, out_shape=out_shape,
        compiler_params=_params(48, 1),
        name="merge_out_final" if final else "merge_out",
    )(*args)


def _prep_weights(w_in):
    offs = np.concatenate([[0], np.cumsum(IN_SPLITS)])
    seg = lambda i, j: w_in[:, :, offs[i]:offs[j]].astype(BF16).transpose(0, 2, 1)
    assert offs[5] == OFF_GZ and offs[15] - offs[6] == OFF_GAB - OFF_GZ
    pad = jnp.zeros((DEPTH, PROJ_W - OFF_GAB - IN_SPLITS[5], D_MODEL), BF16)
    w_proj = (seg(0, 5),
              seg(6, 15),
              jnp.concatenate([seg(5, 6), pad], axis=1))
    return w_proj, seg(15, 16)


def _rope_tables(seq):
    t = jnp.arange(seq)
    quarter = HEAD_DIM // 4
    inv = ROPE_THETA ** (-jnp.arange(quarter, dtype=F32) / quarter)

    def half(pos):
        ang = pos.astype(F32)[:, None] * inv
        c, s, zero = jnp.cos(ang), jnp.sin(ang), jnp.zeros_like(ang)
        return jnp.concatenate([c, c], -1), jnp.concatenate([-s, zero], -1), jnp.concatenate([zero, s], -1)

    parts = [jnp.concatenate([a, b], -1) for a, b in zip(half(t // GRID_W), half(t % GRID_W))]
    tab = jnp.stack(parts)
    return jnp.tile(tab, (1, 1, N_HEADS)), jnp.tile(tab, (1, 1, KV_HEADS))


def _layer(h2d, batch, seq, mod, pw, layer, ctx, caches, final_norm):
    proj, gab = _inproj_call(h2d, mod, pw["norm_g"], pw["w_proj"], layer, seq)
    kw = KV_HEADS * HEAD_DIM
    if ctx is None:
        emit = "first" if caches is None else "update"
        akv, nkv, sg_all, sr_all = caches or (None, None, None, None)
        oa, *akv = _attn_ctx_call(proj, akv, layer, batch, seq, OFF_AQKV, OFF_AQKV + BRANCH_W,
                                  OFF_AQKV + BRANCH_W + kw, OFF_AZ, KV_HEADS, pw["qn"], pw["kn"])
        od, *nkv = _attn_ctx_call(proj, nkv, layer, batch, seq, OFF_D, OFF_D + BRANCH_W,
                                  OFF_D + 2 * BRANCH_W, OFF_D + 3 * BRANCH_W, N_HEADS)
        ob, sg_all = _gdn_call(proj, gab, pw["cw8"], pw["gdn_par"], pw["gdn_norm"], layer, batch, seq,
                               emit=emit, prev=sg_all)
        oc, sr_all = _ret_call(proj, pw["ret_norm"], layer, batch, seq, emit=emit, prev=sr_all)
        caches = (akv, nkv, sg_all, sr_all)
    else:
        oa = _attn_lat_call(proj, ctx["akt"], ctx["avt"], layer, batch, seq, ctx["qtab"], ctx["ktab"],
                            pw["qn"], pw["kn"])
        od = _na_call(proj, ctx["nkt"], ctx["nvt"], ctx["tb"], layer, batch, seq)
        ob, = _gdn_call(proj, gab, pw["cw8"], pw["gdn_par"], pw["gdn_norm"], layer, batch, seq, s0=ctx["sg"])
        oc, = _ret_call(proj, pw["ret_norm"], layer, batch, seq, s0=ctx["sr"])
    outs = _out_call(h2d, mod, pw["norm_g"], (oa, ob, oc, od), pw["wg"], pw["wb"], pw["wo"], layer, seq, final_norm)
    return outs, caches


def kernel(x_prompt, x_sample, cache_attn_k, cache_attn_v, cache_na_k, cache_na_v, state_gdn, state_ret, c, c_ctx, w_ada, b_ada, norm_g, w_in, conv_w, gdn_a_log, gdn_dt_bias, gdn_norm, attn_q_norm, attn_k_norm, ret_norm, na_bias, w_branch, w_out, final_norm):
    batch, seq, _ = x_prompt.shape
    dbatch, dseq, _ = x_sample.shape
    assert dbatch == 8, "the modulation kernel handles exactly one sublane tile of conditioning rows"

    w_proj, wg = _prep_weights(w_in)
    par = jnp.zeros((DEPTH, 2, 128), F32)
    par = par.at[:, 0, 8:16].set(gdn_dt_bias.reshape(DEPTH, 8)).at[:, 1, 8:16].set(gdn_a_log.reshape(DEPTH, 8))
    pw = dict(
        w_ada=w_ada, b_ada=b_ada.reshape(DEPTH, 1, 3 * D_MODEL), norm_g=norm_g.reshape(DEPTH, 1, D_MODEL),
        w_proj=w_proj, wg=wg, wb=w_branch.astype(BF16), wo=w_out.astype(BF16),
        cw8=jnp.concatenate([conv_w, jnp.zeros((DEPTH, 8 - SHORT_CONV, 3 * BRANCH_W), F32)], axis=1),
        gdn_par=par,
        gdn_norm=jnp.tile(gdn_norm, (1, N_HEADS)).reshape(DEPTH, 1, BRANCH_W),
        ret_norm=ret_norm.reshape(DEPTH, 1, HEAD_DIM),
        qn=jnp.tile(attn_q_norm, (1, N_HEADS)).reshape(DEPTH, 1, BRANCH_W),
        kn=jnp.tile(attn_k_norm, (1, KV_HEADS)).reshape(DEPTH, 1, KV_HEADS * HEAD_DIM))

    cond = jnp.concatenate([jnp.broadcast_to(c_ctx, (8, D_MODEL)), c], axis=0)
    mods = _mod_call(cond, w_ada, pw["b_ada"])

    h = x_prompt.reshape(batch * seq, D_MODEL)
    caches = None
    for l in range(DEPTH):
        outs, caches = _layer(h, batch, seq, mods[l, 0:1].reshape(1, 1, 3 * D_MODEL), pw, l, None, caches,
                              final_norm if l == DEPTH - 1 else None)
        h = outs[0]
    y_prompt = outs[1].reshape(batch, seq, D_MODEL)
    token_major = lambda a: a.transpose(0, 1, 4, 2, 3)
    (akt, avt), (nkt, nvt), new_state_gdn, new_state_ret = caches
    new_attn_k, new_attn_v, new_na_k, new_na_v = (token_major(a) for a in (akt, avt, nkt, nvt))

    qtab, ktab = _rope_tables(dseq)
    feature_major = lambda a: a.transpose(0, 1, 3, 4, 2)
    ctx = dict(akt=feature_major(cache_attn_k), avt=feature_major(cache_attn_v),
               nkt=feature_major(cache_na_k), nvt=feature_major(cache_na_v),
               sg=state_gdn, sr=state_ret, tb=_na_bias_call(na_bias), qtab=qtab, ktab=ktab)
    h = x_sample.reshape(dbatch * dseq, D_MODEL)
    for l in range(DEPTH):
        outs, _ = _layer(h, dbatch, dseq, mods[l, 8:16].reshape(dbatch, 1, 3 * D_MODEL), pw, l, ctx, None,
                         final_norm if l == DEPTH - 1 else None)
        h = outs[0]
    y_sample = outs[1].reshape(dbatch, dseq, D_MODEL)
    return (y_prompt, y_sample, new_attn_k, new_attn_v, new_na_k, new_na_v, new_state_gdn, new_state_ret)
```

```python
import functools

import numpy as np
import jax
import jax.numpy as jnp
from jax import lax
from jax.experimental import pallas as pl
from jax.experimental.pallas import tpu as pltpu

F32 = jnp.float32
BF16 = jnp.bfloat16

D_MODEL = 1024
HEAD_DIM = 64
HEAD_SHIFT = HEAD_DIM.bit_length() - 1
N_HEADS = 4
KV_HEADS = N_HEADS // 2
BRANCH_W = N_HEADS * HEAD_DIM
N_BRANCH = 4
DEPTH = 2
GRID_W = 64
CHUNK = 64
PREP_CHUNKS = 4
GDN_TILE_HEADS = 2
assert CHUNK == HEAD_DIM
SHORT_CONV = 5
NA_ROWS = 8
NA_COLS = 16
RET_ROWS_PER_STEP = 1024
CTX_SEQS_PER_STEP = 4
NA_ROWS_PER_STEP = 8
N_DR = 2 * NA_ROWS - 1
N_DC = 2 * NA_COLS - 1
ROPE_THETA = 10000.0
RET_DECAY_BASE = (5.0, 5.5)
RET_TILE = 256
EPS = 1e-6
SCALE = HEAD_DIM ** -0.5
LOG2E = 1.4426950408889634
NEG_INF = float("-inf")

IN_SPLITS = (256, 128, 128, 256, 768, 16, 256, 256, 256, 256, 256, 256, 256, 256, 256, 4096)
PROJ_W = 4096
OFF_AQKV = 0
OFF_AZ = 512
OFF_GQKV = 768
OFF_GZ = 1536
OFF_C = 1792
OFF_D = 2816
OFF_GAB = 3840

V7X_VMEM_BYTES = 64 * 1024 * 1024
MIB = 1024 * 1024


def _params(vmem_mib, n_axes):
    assert vmem_mib * MIB < V7X_VMEM_BYTES
    return pltpu.CompilerParams(dimension_semantics=("arbitrary",) * n_axes,
                                vmem_limit_bytes=vmem_mib * MIB)


def _layer_spec(block, layer, n_grid):
    zeros = (0,) * len(block)
    if n_grid == 1:
        return pl.BlockSpec((None,) + block, lambda i: (layer,) + zeros)
    return pl.BlockSpec((None,) + block, lambda i, j: (layer,) + zeros)


def _mm(a, b):
    return jnp.dot(a.astype(BF16), b.astype(BF16), preferred_element_type=F32)


def _mm_nt(a, b):
    return lax.dot_general(a.astype(BF16), b.astype(BF16), (((1,), (1,)), ((), ())),
                           preferred_element_type=F32)


def _mm_tn(a, b):
    return lax.dot_general(a.astype(BF16), b.astype(BF16), (((0,), (0,)), ((), ())),
                           preferred_element_type=F32)


def _split3(x):
    hi = x.astype(BF16)
    r = x - hi.astype(F32)
    mid = r.astype(BF16)
    lo = (r - mid.astype(F32)).astype(BF16)
    return hi, mid, lo


def _mm_exact(sel, x, terms=3):
    return sum(jnp.dot(sel, part, preferred_element_type=F32) for part in _split3(x)[:terms])


def _mm_exact_lhs(x, sel, terms=3):
    return sum(jnp.dot(part, sel, preferred_element_type=F32) for part in _split3(x)[:terms])


def _silu(x):
    return x * jax.nn.sigmoid(x)


def _head_block_matrix(width, value):
    ri = lax.broadcasted_iota(jnp.int32, (width, width), 0) >> HEAD_SHIFT
    ci = lax.broadcasted_iota(jnp.int32, (width, width), 1) >> HEAD_SHIFT
    return jnp.where(ri == ci, value, 0.0).astype(BF16)


def _head_reduce(x, g, terms=2):
    return sum(jnp.dot(part, g, preferred_element_type=F32) for part in _split3(x)[:terms])


def _head_rms(x):
    ms = _head_reduce(x * x, _head_block_matrix(x.shape[1], 1.0 / HEAD_DIM))
    return x * lax.rsqrt(ms + EPS)


def _rope(x, tab_ref):
    w = x.shape[1]
    return (x * tab_ref[0] + pltpu.roll(x, w - 16, 1) * tab_ref[1] + pltpu.roll(x, 16, 1) * tab_ref[2])


def _attend(qs, parts):
    groups = range(len(qs))
    qs = [(q.astype(F32) * (SCALE * LOG2E)).astype(BF16) for q in qs]

    def score(q, part):
        k, _, bias, feature_major = part
        s = _mm(q, k) if feature_major else _mm_nt(q, k)
        return s if bias is None else s + bias

    scores = [[score(qs[g], part) for part in parts[g]] for g in groups]
    m = [functools.reduce(jnp.maximum, [s.max(axis=-1, keepdims=True) for s in scores[g]]) for g in groups]
    p = [[jnp.exp2(s - m[g]) for s in scores[g]] for g in groups]
    den = [sum(x.sum(axis=-1, keepdims=True) for x in p[g]) for g in groups]
    out = [sum(_mm_nt(x, part[1]) if part[3] else _mm(x, part[1]) for x, part in zip(p[g], parts[g]))
           for g in groups]
    return [out[g] / den[g] for g in groups]


def _hs(h):
    return slice(h * HEAD_DIM, (h + 1) * HEAD_DIM)


def _aligned(x, m):
    return x if isinstance(x, int) else pl.multiple_of(x, m)


def _mod_kernel(c_ref, w_ref, b_ref, o_ref):
    o_ref[...] = _mm(_silu(c_ref[...]), w_ref[...]) + b_ref[...]


def _mod_call(cond, w_ada, b_ada3):
    tn = 512
    rows = cond.shape[0]
    return pl.pallas_call(
        _mod_kernel,
        grid=(DEPTH, 3 * D_MODEL // tn),
        in_specs=[pl.BlockSpec((rows, D_MODEL), lambda l, j: (0, 0)),
                  pl.BlockSpec((None, D_MODEL, tn), lambda l, j: (l, 0, j)),
                  pl.BlockSpec((None, 1, tn), lambda l, j: (l, 0, j))],
        out_specs=pl.BlockSpec((None, rows, tn), lambda l, j: (l, 0, j)),
        out_shape=jax.ShapeDtypeStruct((DEPTH, rows, 3 * D_MODEL), F32),
        compiler_params=_params(24, 2),
        name="adaln_mod",
    )(cond, w_ada, b_ada3)


def _modulated_norm(x, mod, g):
    ms = jnp.mean(x * x, axis=-1, keepdims=True)
    y = x * lax.rsqrt(ms + EPS) * g
    return y * (1.0 + mod[:, D_MODEL:2 * D_MODEL]) + mod[:, :D_MODEL]


def _inproj_kernel(x_ref, mod_ref, g_ref, *refs):
    *w_refs, o_ref, ab_ref = refs
    hn = _modulated_norm(x_ref[...], mod_ref[0], g_ref[...]).astype(BF16)
    tn = 256
    col = 0
    for w_ref in w_refs:
        for j in range(w_ref.shape[0] // tn):
            y = _mm_nt(hn, w_ref[j * tn:(j + 1) * tn, :])
            o_ref[:, col:col + tn] = y.astype(BF16)
            if col == OFF_GAB:
                ab_ref[...] = y[:, :128]
            col += tn
    assert col == PROJ_W


def _inproj_call(x2d, mod3, norm_g3, weights, layer, rows_per_mod):
    t = x2d.shape[0]
    tm = 1024
    if mod3.shape[0] == 1:
        mod_idx = lambda i: (0, 0, 0)
    else:
        mod_idx = lambda i: ((i * tm) // rows_per_mod, 0, 0)
    w_specs = [pl.BlockSpec((None, w.shape[1], D_MODEL), lambda i: (layer, 0, 0), pipeline_mode=pl.Buffered(1))
               for w in weights]
    return pl.pallas_call(
        _inproj_kernel,
        grid=(t // tm,),
        in_specs=[pl.BlockSpec((tm, D_MODEL), lambda i: (i, 0)),
                  pl.BlockSpec((1, 1, 3 * D_MODEL), mod_idx),
                  _layer_spec((1, D_MODEL), layer, 1)] + w_specs,
        out_specs=[pl.BlockSpec((tm, PROJ_W), lambda i: (i, 0)), pl.BlockSpec((tm, 128), lambda i: (i, 0))],
        out_shape=[jax.ShapeDtypeStruct((t, PROJ_W), BF16), jax.ShapeDtypeStruct((t, 128), F32)],
        compiler_params=_params(40, 1),
        name="inproj",
    )(x2d, mod3, norm_g3, *weights)


def _stacked_heads(q, n_kv):
    rep = N_HEADS // n_kv
    return [jnp.concatenate([q[:, _hs(g * rep + r)] for r in range(rep)], axis=0) for g in range(n_kv)]


def _unstack_heads(outs, n_kv):
    rep = N_HEADS // n_kv
    m = outs[0].shape[0] // rep
    return jnp.concatenate([outs[g][r * m:(r + 1) * m] for g in range(n_kv) for r in range(rep)], axis=-1)


def _write_layer(ref, layer, value, stacked):
    if not stacked:
        ref[...] = value
        return
    for l in range(ref.shape[0]):
        ref[l] = value if l == layer else jnp.zeros(value.shape, value.dtype)


def _write_state(st_ref, layer, stacked, piece):
    for d in range(2):
        for h in range(N_HEADS):
            value = piece(d, h)
            if stacked:
                for l in range(st_ref.shape[0]):
                    st_ref[l, d, h] = value if l == layer else jnp.zeros(value.shape, value.dtype)
            else:
                st_ref[d, h] = value


def _attn_ctx_kernel(*refs, n_kv, norm, layer, first):
    if norm:
        q_ref, k_ref, v_ref, z_ref, qn_ref, kn_ref = refs[:6]
    else:
        q_ref, k_ref, v_ref, z_ref = refs[:4]
    o_ref, kt_ref, vt_ref = refs[-3:]
    q, k, v, z = q_ref[...], k_ref[...].astype(F32), v_ref[...].astype(F32), z_ref[...].astype(F32)
    if norm:
        q = _head_rms(q.astype(F32)) * qn_ref[...]
        k = _head_rms(k) * kn_ref[...]
    n_seq = kt_ref.shape[0]
    seq = k.shape[0] // n_seq
    qs, parts = [], []
    for s in range(n_seq):
        rows = slice(s * seq, (s + 1) * seq)
        _write_layer(kt_ref.at[s], layer, k[rows].T.reshape(n_kv, HEAD_DIM, seq), first)
        _write_layer(vt_ref.at[s], layer, v[rows].T.reshape(n_kv, HEAD_DIM, seq), first)
        qs += _stacked_heads(q[rows], n_kv)
        parts += [[(k[rows, _hs(g)], v[rows, _hs(g)], None, False)] for g in range(n_kv)]
    outs = _attend(qs, parts)
    o = jnp.concatenate([_unstack_heads(outs[s * n_kv:(s + 1) * n_kv], n_kv) for s in range(n_seq)], axis=0)
    o_ref[...] = (o * _silu(z)).astype(BF16)


def _attn_ctx_call(proj, prev, layer, batch, seq, off_q, off_k, off_v, off_z, n_kv, qn=None, kn=None):
    t = batch * seq
    kvw = n_kv * HEAD_DIM
    norm = qn is not None
    first = prev is None
    n_seq = CTX_SEQS_PER_STEP
    assert batch % n_seq == 0
    tm = n_seq * seq
    in_specs = [pl.BlockSpec((tm, BRANCH_W), lambda b: (b, off_q // BRANCH_W)),
                pl.BlockSpec((tm, kvw), lambda b: (b, off_k // kvw)),
                pl.BlockSpec((tm, kvw), lambda b: (b, off_v // kvw)),
                pl.BlockSpec((tm, BRANCH_W), lambda b: (b, off_z // BRANCH_W))]
    args = [proj, proj, proj, proj]
    if norm:
        in_specs += [_layer_spec((1, BRANCH_W), layer, 1), _layer_spec((1, kvw), layer, 1)]
        args += [qn, kn]
    aliases = {}
    if first:
        cache_spec = pl.BlockSpec((n_seq, DEPTH, n_kv, HEAD_DIM, seq), lambda b: (b, 0, 0, 0, 0))
    else:
        aliases = {len(args): 1, len(args) + 1: 2}
        in_specs += [pl.BlockSpec(memory_space=pl.ANY)] * 2
        args += list(prev)
        cache_spec = pl.BlockSpec((n_seq, None, n_kv, HEAD_DIM, seq), lambda b: (b, layer, 0, 0, 0))
    cache_shape = jax.ShapeDtypeStruct((batch, DEPTH, n_kv, HEAD_DIM, seq), F32)
    return pl.pallas_call(
        functools.partial(_attn_ctx_kernel, n_kv=n_kv, norm=norm, layer=layer, first=first),
        grid=(batch // n_seq,), in_specs=in_specs,
        out_specs=[pl.BlockSpec((tm, BRANCH_W), lambda b: (b, 0)), cache_spec, cache_spec],
        out_shape=[jax.ShapeDtypeStruct((t, BRANCH_W), BF16), cache_shape, cache_shape],
        input_output_aliases=aliases,
        compiler_params=_params(32, 1),
        name="attn_ctx_norm" if norm else "attn_ctx",
    )(*args)


def _attn_lat_kernel(q_ref, kv_ref, z_ref, ckt_ref, cvt_ref, qtab_ref, ktab_ref, qn_ref, kn_ref, o_ref,
                     k_s, v_s):
    kw = KV_HEADS * HEAD_DIM

    @pl.when(pl.program_id(1) == 0)
    def _():
        kv = kv_ref[...]
        k_s[...] = _rope(_head_rms(kv[:, :kw].astype(F32)) * kn_ref[...], ktab_ref).astype(BF16)
        v_s[...] = kv[:, kw:]

    q = _rope(_head_rms(q_ref[...].astype(F32)) * qn_ref[...], qtab_ref)
    k, v = k_s[...], v_s[...]
    outs = _attend(_stacked_heads(q, KV_HEADS),
                   [[(k[:, _hs(g)], v[:, _hs(g)], None, False), (ckt_ref[g], cvt_ref[g], None, True)]
                    for g in range(KV_HEADS)])
    o_ref[...] = (_unstack_heads(outs, KV_HEADS) * _silu(z_ref[...].astype(F32))).astype(BF16)


def _attn_lat_call(proj, cache_kt, cache_vt, layer, batch, seq, qtab, ktab, qn, kn):
    tq = 512
    nq = seq // tq
    past = cache_kt.shape[-1]
    kw = KV_HEADS * HEAD_DIM
    ctx_spec = pl.BlockSpec((None, None, KV_HEADS, HEAD_DIM, past), lambda b, i: (b, layer, 0, 0, 0))
    return pl.pallas_call(
        _attn_lat_kernel,
        grid=(batch, nq),
        in_specs=[pl.BlockSpec((tq, BRANCH_W), lambda b, i: (b * nq + i, OFF_AQKV // BRANCH_W)),
                  pl.BlockSpec((seq, 2 * kw), lambda b, i: (b, (OFF_AQKV + BRANCH_W) // (2 * kw))),
                  pl.BlockSpec((tq, BRANCH_W), lambda b, i: (b * nq + i, OFF_AZ // BRANCH_W)),
                  ctx_spec, ctx_spec,
                  pl.BlockSpec((3, tq, BRANCH_W), lambda b, i: (0, i, 0)),
                  pl.BlockSpec((3, seq, kw), lambda b, i: (0, 0, 0)),
                  _layer_spec((1, BRANCH_W), layer, 2),
                  _layer_spec((1, kw), layer, 2)],
        out_specs=pl.BlockSpec((tq, BRANCH_W), lambda b, i: (b * nq + i, 0)),
        out_shape=jax.ShapeDtypeStruct((batch * seq, BRANCH_W), BF16),
        scratch_shapes=[pltpu.VMEM((seq, kw), BF16), pltpu.VMEM((seq, kw), BF16)],
        compiler_params=_params(40, 2),
        name="attn_lat",
    )(proj, proj, proj, cache_kt, cache_vt, qtab, ktab, qn, kn)


def _na_bias_kernel(t_ref, o_ref):
    nblk = o_ref.shape[0]
    c = lax.broadcasted_iota(jnp.int32, (GRID_W, 2 * GRID_W), 0)
    kc = lax.broadcasted_iota(jnp.int32, (GRID_W, 2 * GRID_W), 1) & (GRID_W - 1)
    cs = jnp.clip(c - NA_COLS // 2, 0, GRID_W - NA_COLS)
    valid = jnp.logical_and(kc >= cs, kc < cs + NA_COLS)

    unroll = 8
    assert nblk % unroll == 0

    def body(i, carry):
        rows8 = t_ref[pl.ds(pl.multiple_of(i * unroll, unroll), unroll), :]
        for u in range(unroll):
            row = jnp.broadcast_to(rows8[u:u + 1, :], (GRID_W, 2 * GRID_W))
            skewed = pltpu.roll(row, 2 * GRID_W - (NA_COLS - 1), 1, stride=1, stride_axis=0)
            o_ref[i * unroll + u] = jnp.where(valid, skewed * LOG2E, NEG_INF)
        return carry

    lax.fori_loop(0, nblk // unroll, body, 0)


def _na_bias_call(na_bias):
    nblk = DEPTH * N_HEADS * N_DR
    rows = jnp.pad(na_bias.reshape(nblk, N_DC), ((0, 1), (0, GRID_W - N_DC)))
    pairs = jnp.concatenate([rows[:-1], rows[1:]], axis=1)
    return pl.pallas_call(
        _na_bias_kernel,
        in_specs=[pl.BlockSpec((nblk, 2 * GRID_W), lambda: (0, 0))],
        out_specs=pl.BlockSpec((nblk, GRID_W, 2 * GRID_W), lambda: (0, 0, 0)),
        out_shape=jax.ShapeDtypeStruct((nblk, GRID_W, 2 * GRID_W), F32),
        name="na_bias",
    )(pairs)


def _na_kernel(q_ref, k_ref, v_ref, z_ref, ckt_ref, cvt_ref, tb_ref, o_ref, kh_s, vh_s, *, rows):
    win = NA_ROWS * GRID_W

    @pl.when(pl.program_id(1) == 0)
    def _():
        for h in range(N_HEADS):
            kh_s[h] = k_ref[:, _hs(h)]
            vh_s[h] = v_ref[:, _hs(h)]

    qs, parts = [], []
    for i in range(NA_ROWS_PER_STEP):
        r = pl.program_id(1) * NA_ROWS_PER_STEP + i
        rs = jnp.clip(r - NA_ROWS // 2, 0, rows - NA_ROWS)
        r0 = pl.multiple_of(rs * GRID_W, GRID_W)
        q = q_ref[i * GRID_W:(i + 1) * GRID_W, :]
        dr0 = rs - r + NA_ROWS - 1
        for h in range(N_HEADS):
            bias = jnp.concatenate([tb_ref[h * N_DR + dr0 + 2 * p] for p in range(NA_ROWS // 2)], axis=1)
            qs.append(q[:, _hs(h)])
            parts.append([(kh_s[h, pl.ds(r0, win), :], vh_s[h, pl.ds(r0, win), :], bias, False),
                          (ckt_ref[h], cvt_ref[h], None, True)])
    outs = _attend(qs, parts)
    o = jnp.concatenate([jnp.concatenate(outs[i * N_HEADS:(i + 1) * N_HEADS], axis=-1)
                         for i in range(NA_ROWS_PER_STEP)], axis=0)
    o_ref[...] = (o * _silu(z_ref[...].astype(F32))).astype(BF16)


def _na_call(proj, cache_kt, cache_vt, tb, layer, batch, seq):
    rows = seq // GRID_W
    assert rows >= NA_ROWS and rows % NA_ROWS_PER_STEP == 0
    steps = rows // NA_ROWS_PER_STEP
    tq = NA_ROWS_PER_STEP * GRID_W
    past = cache_kt.shape[-1]
    nblk = N_HEADS * N_DR
    cq = OFF_D // BRANCH_W
    ctx_spec = pl.BlockSpec((None, None, N_HEADS, HEAD_DIM, past), lambda b, r: (b, layer, 0, 0, 0))
    return pl.pallas_call(
        functools.partial(_na_kernel, rows=rows),
        grid=(batch, steps),
        in_specs=[pl.BlockSpec((tq, BRANCH_W), lambda b, r: (b * steps + r, cq)),
                  pl.BlockSpec((seq, BRANCH_W), lambda b, r: (b, cq + 1)),
                  pl.BlockSpec((seq, BRANCH_W), lambda b, r: (b, cq + 2)),
                  pl.BlockSpec((tq, BRANCH_W), lambda b, r: (b * steps + r, cq + 3)),
                  ctx_spec, ctx_spec,
                  pl.BlockSpec((nblk, GRID_W, 2 * GRID_W), lambda b, r: (layer, 0, 0))],
        out_specs=pl.BlockSpec((tq, BRANCH_W), lambda b, r: (b * steps + r, 0)),
        out_shape=jax.ShapeDtypeStruct((batch * seq, BRANCH_W), BF16),
        scratch_shapes=[pltpu.VMEM((N_HEADS, seq, HEAD_DIM), BF16), pltpu.VMEM((N_HEADS, seq, HEAD_DIM), BF16)],
        compiler_params=_params(32, 2),
        name="na_lat",
    )(proj, proj, proj, proj, cache_kt, cache_vt, tb)


def _gdn_kernel(*refs, seq, has_s0, layer, emit):
    qkv_ref, z_ref, ab_ref, cw_ref, par_ref, g_ref = refs[:6]
    s0_ref = refs[6] if has_s0 else None
    (q_s, k_s, v_s, gcb_s, bcb_s, r_s, mc_s, nc_s, qp_s, op_s, egl_s, s_s, oacc_ref) = refs[-13:]
    if emit == "none":
        o_ref, st_ref = refs[-14], None
    else:
        o_ref, st_ref = refs[-15], refs[-14]
    n_chunks = seq // CHUNK
    n_levels = CHUNK.bit_length() - 1
    qkv_w = 3 * BRANCH_W
    half = SHORT_CONV // 2
    pair_w = GDN_TILE_HEADS * HEAD_DIM
    pairs = [slice(p * pair_w, (p + 1) * pair_w) for p in range(BRANCH_W // pair_w)]
    tr = 256
    cpt = tr // CHUNK
    head_sum = _head_block_matrix(BRANCH_W, 1.0)

    gc_i = lax.broadcasted_iota(jnp.int32, (128, BRANCH_W), 0)
    gh_j = lax.broadcasted_iota(jnp.int32, (128, BRANCH_W), 1) >> HEAD_SHIFT
    sel_beta = [jnp.where(gc_i == gh_j + 4 * d, 1.0, 0.0).astype(BF16) for d in range(2)]
    sel_gate = [jnp.where(gc_i == gh_j + 8 + 4 * d, 1.0, 0.0).astype(BF16) for d in range(2)]
    ti = lax.broadcasted_iota(jnp.int32, (tr, tr), 0)
    tj = lax.broadcasted_iota(jnp.int32, (tr, tr), 1)
    same_chunk = (ti >> HEAD_SHIFT) == (tj >> HEAD_SHIFT)
    tri = [jnp.where(jnp.logical_and(same_chunk, ti >= tj), 1.0, 0.0).astype(BF16),
           jnp.where(jnp.logical_and(same_chunk, ti <= tj), 1.0, 0.0).astype(BF16)]
    assert tr == BRANCH_W
    lane_head = lax.broadcasted_iota(jnp.int32, (1, BRANCH_W), 1) >> HEAD_SHIFT

    halo = 16
    edge = 8
    assert half <= edge
    si = lax.broadcasted_iota(jnp.int32, (tr, tr), 0)
    sj = lax.broadcasted_iota(jnp.int32, (tr, tr), 1)
    ei = lax.broadcasted_iota(jnp.int32, (edge, halo), 0)
    ej = lax.broadcasted_iota(jnp.int32, (edge, halo), 1)
    taps = [j for j in range(SHORT_CONV) if j != half]
    shift = {j: jnp.where(sj == si + (j - half), 1.0, 0.0).astype(BF16) for j in taps}
    shift_before = {j: jnp.where(ej == ei + (halo + j - half), 1.0, 0.0).astype(BF16) for j in taps if j < half}
    shift_after = {j: jnp.where(ej == ei + (j - half - edge), 1.0, 0.0).astype(BF16) for j in taps if j > half}
    for t in range(seq // tr):
        rows = slice(t * tr, (t + 1) * tr)
        x = qkv_ref[rows, :]
        y = x.astype(F32) * cw_ref[half:half + 1, :]
        for j in taps:
            y = y + jnp.dot(shift[j], x, preferred_element_type=F32) * cw_ref[j:j + 1, :]
        if t > 0:
            before = qkv_ref[t * tr - halo:t * tr, :]
            top = sum(jnp.dot(shift_before[j], before, preferred_element_type=F32) * cw_ref[j:j + 1, :]
                      for j in shift_before)
            y = jnp.concatenate([y[:edge] + top, y[edge:]], axis=0)
        if (t + 1) * tr < seq:
            after = qkv_ref[(t + 1) * tr:(t + 1) * tr + halo, :]
            bottom = sum(jnp.dot(shift_after[j], after, preferred_element_type=F32) * cw_ref[j:j + 1, :]
                         for j in shift_after)
            y = jnp.concatenate([y[:tr - edge], y[tr - edge:] + bottom], axis=0)
        y = _silu(y)
        qq, kk = y[:, :BRANCH_W], y[:, BRANCH_W:2 * BRANCH_W]
        q_s[rows, :] = qq * lax.rsqrt(_head_reduce(qq * qq, head_sum, terms=1) + EPS) * SCALE
        k_s[rows, :] = kk * lax.rsqrt(_head_reduce(kk * kk, head_sum, terms=1) + EPS)
        v_s[rows, :] = y[:, 2 * BRANCH_W:]
        x = ab_ref[rows, :]
        beta = jax.nn.sigmoid(x)
        xs = x + par_ref[0:1, :]
        softplus = jnp.maximum(xs, 0.0) + jnp.log1p(jnp.exp(-jnp.abs(xs)))
        la = -jnp.exp(par_ref[1:2, :]) * softplus
        for d in range(2):
            gc = _mm_exact(tri[d], la, terms=2)
            gcb_s[d, rows, :] = _mm_exact_lhs(gc, sel_gate[d], terms=2)
            bcb_s[d, rows, :] = _mm_exact_lhs(beta, sel_beta[d], terms=2)
            gt = gc.T[8:16, :]
            shifted = {s: (gt if s == 0 else pltpu.roll(gt, (s * HEAD_DIM) % tr, 1))
                       for s in range(1 - cpt, N_HEADS)}
            for c in range(cpt):
                r = jnp.zeros((1, BRANCH_W), F32)
                for h in range(N_HEADS):
                    r = jnp.where(lane_head == h, shifted[h - c][4 * d + h:4 * d + h + 1, :], r)
                r_s[d, (t * cpt + c) * 8:(t * cpt + c + 1) * 8, :] = jnp.broadcast_to(r, (8, BRANCH_W))

    for d in range(2):
        if has_s0:
            s_s[d] = jnp.concatenate([s0_ref[d, h] for h in range(N_HEADS)], axis=-1)
        else:
            s_s[d] = jnp.zeros((HEAD_DIM, BRANCH_W), F32)

    li = lax.broadcasted_iota(jnp.int32, (CHUNK, BRANCH_W), 0)
    lj = lax.broadcasted_iota(jnp.int32, (CHUNK, BRANCH_W), 1) & (HEAD_DIM - 1)
    incl = (li >= lj, li <= lj)
    strict = (li > lj, li < lj)
    level = [((li ^ lj) >> l) == 1 for l in range(n_levels)]
    tile_heads = pair_w // HEAD_DIM
    head_in_tile = lax.broadcasted_iota(jnp.int32, (CHUNK, pair_w), 1) >> HEAD_SHIFT

    def diag_blocks(full):
        out = full[:HEAD_DIM]
        for a in range(1, tile_heads):
            out = jnp.where(head_in_tile == a, full[a * HEAD_DIM:(a + 1) * HEAD_DIM], out)
        return out

    def expand(y):
        yb = y.astype(BF16)
        zero = jnp.zeros((CHUNK, pair_w), BF16)
        return [jnp.concatenate([jnp.where(head_in_tile == a, yb[:, p], zero) for a in range(tile_heads)], axis=0)
                for p in pairs]

    def bdmm(x, ybd):
        xb = x.astype(BF16)
        return jnp.concatenate([jnp.dot(xb[:, p], ybd[i], preferred_element_type=F32)
                                for i, p in enumerate(pairs)], axis=1)

    def bdmm_nt(x, ybd):
        xb = x.astype(BF16)
        return jnp.concatenate([lax.dot_general(xb[:, p], ybd[i], (((1,), (1,)), ((), ())),
                                                preferred_element_type=F32)
                                for i, p in enumerate(pairs)], axis=1)

    def bdmm2(x, y1, y2):
        xb = x.astype(BF16)
        e1, e2 = expand(y1), expand(y2)
        outs = [jnp.dot(xb[:, p], jnp.concatenate([e1[i], e2[i]], axis=1), preferred_element_type=F32)
                for i, p in enumerate(pairs)]
        return (jnp.concatenate([o[:, :pair_w] for o in outs], axis=1),
                jnp.concatenate([o[:, pair_w:] for o in outs], axis=1))

    def tn_diag2(a, b1, b2):
        ab, b1b, b2b = a.astype(BF16), b1.astype(BF16), b2.astype(BF16)
        outs1, outs2 = [], []
        for p in pairs:
            full = lax.dot_general(ab[:, p], jnp.concatenate([b1b[:, p], b2b[:, p]], axis=1),
                                   (((0,), (0,)), ((), ())), preferred_element_type=F32)
            outs1.append(diag_blocks(full[:, :pair_w]))
            outs2.append(diag_blocks(full[:, pair_w:]))
        return jnp.concatenate(outs1, axis=1), jnp.concatenate(outs2, axis=1)

    def prepare(chains):
        n = range(len(chains))
        dd = [d for d, _ in chains]
        rows = [pl.ds(_aligned(c * CHUNK, CHUNK), CHUNK) for _, c in chains]
        gcb = [gcb_s[dd[i], rows[i], :] for i in n]
        bcb = [bcb_s[dd[i], rows[i], :] for i in n]
        grow = [r_s[dd[i], pl.ds(_aligned(chains[i][1] * 8, 8), 8), :][0:1, :] for i in n]
        dm = [jnp.exp(jnp.where(incl[dd[i]], gcb[i] - grow[i], NEG_INF)) for i in n]
        k = [k_s[rows[i], :] for i in n]
        q = [q_s[rows[i], :] for i in n]
        v = [v_s[rows[i], :] for i in n]
        kq = [bdmm_nt(jnp.concatenate([k[i], q[i]], axis=0), expand(k[i])) for i in n]
        a = [jnp.where(strict[dd[i]], bcb[i] * kq[i][:CHUNK] * dm[i], 0.0) for i in n]
        tm = [-jnp.where(level[0], a[i], 0.0) for i in n]
        for l in range(1, n_levels):
            b = [jnp.where(level[l], a[i], 0.0) for i in n]
            y = [b[i] + bdmm(tm[i], expand(b[i])) for i in n]
            tm = [tm[i] - (y[i] + bdmm(y[i], expand(tm[i]))) for i in n]
        eg = [jnp.exp(gcb[i]) for i in n]
        bv = [bcb[i] * v[i] for i in n]
        bk = [bcb[i] * k[i] * eg[i] for i in n]
        tuw = [bdmm2(tm[i], bv[i], bk[i]) for i in n]
        u = [bv[i] + tuw[i][0] for i in n]
        w = [bk[i] + tuw[i][1] for i in n]
        gl = [gcb[i][CHUNK - 1:CHUNK, :] if dd[i] == 0 else gcb[i][0:1, :] for i in n]
        kd = [k[i] * jnp.exp(gl[i] - gcb[i]) for i in n]
        qkm = [kq[i][CHUNK:] * dm[i] for i in n]
        mnc = [tn_diag2(kd[i], w[i], u[i]) for i in n]
        mc = [mnc[i][0] for i in n]
        nc = [mnc[i][1] for i in n]
        qwu = [bdmm2(qkm[i], w[i], u[i]) for i in n]
        qp = [q[i] * eg[i] - qwu[i][0] for i in n]
        op = [qwu[i][1] for i in n]
        for i in n:
            d, c = chains[i]
            mc_s[d, rows[i], :] = mc[i].astype(BF16)
            nc_s[d, rows[i], :] = nc[i]
            qp_s[d, rows[i], :] = qp[i].astype(BF16)
            op_s[d, rows[i], :] = op[i]
            egl_s[d, pl.ds(_aligned(c * 8, 8), 8), :] = jnp.broadcast_to(jnp.exp(gl[i]), (8, BRANCH_W))

    group = min(PREP_CHUNKS, n_chunks)
    n_groups = n_chunks // group

    def prepare_group(j):
        prepare([(d, (j if d == 0 else n_groups - 1 - j) * group + c) for c in range(group) for d in range(2)])

    def scan_step(i):
        for d, c in ((0, i), (1, n_chunks - 1 - i)):
            rows = pl.ds(_aligned(c * CHUNK, CHUNK), CHUNK)
            s = s_s[d]
            sbd = expand(s)
            oacc_ref[d, rows, :] = bdmm(qp_s[d, rows, :], sbd) + op_s[d, rows, :]
            egl = egl_s[d, pl.ds(_aligned(c * 8, 8), 8), :][0:1, :]
            s_s[d] = s * egl - bdmm(mc_s[d, rows, :], sbd) + nc_s[d, rows, :]

    def scan_group(j):
        for i in range(group):
            scan_step(j * group + i)

    prepare_group(0)
    if n_groups > 1:
        def body(j, carry):
            scan_group(j - 1)
            prepare_group(j)
            return carry
        lax.fori_loop(1, n_groups, body, 0)
    scan_group(n_groups - 1)

    if st_ref is not None:
        _write_state(st_ref, layer, emit == "first", lambda d, h: s_s[d][:, _hs(h)])
    o = oacc_ref[0] + oacc_ref[1]
    ms = _head_reduce(o * o, _head_block_matrix(BRANCH_W, 1.0 / HEAD_DIM))
    o_ref[...] = (o * lax.rsqrt(ms + EPS) * g_ref[...] * _silu(z_ref[...].astype(F32))).astype(BF16)


def _state_spec(layer, n_seq=None):
    return pl.BlockSpec((n_seq, None, 2, N_HEADS, HEAD_DIM, HEAD_DIM), lambda b: (b, layer, 0, 0, 0, 0))


def _state_output(emit, prev, layer, batch, n_args, n_seq=None):
    if emit == "none":
        return [], [], [], [], {}
    shape = jax.ShapeDtypeStruct((batch, DEPTH, 2, N_HEADS, HEAD_DIM, HEAD_DIM), F32)
    if emit == "first":
        spec = pl.BlockSpec((n_seq, DEPTH, 2, N_HEADS, HEAD_DIM, HEAD_DIM), lambda b: (b, 0, 0, 0, 0, 0))
        return [], [], [spec], [shape], {}
    return [pl.BlockSpec(memory_space=pl.ANY)], [prev], [_state_spec(layer, n_seq)], [shape], {n_args: 1}


def _gdn_call(proj, gab, cw8, par, norm_g, layer, batch, seq, s0=None, emit="none", prev=None):
    has_s0 = s0 is not None
    qkv_w = 3 * BRANCH_W
    in_specs = [pl.BlockSpec((seq, qkv_w), lambda b: (b, OFF_GQKV // qkv_w)),
                pl.BlockSpec((seq, BRANCH_W), lambda b: (b, OFF_GZ // BRANCH_W)),
                pl.BlockSpec((seq, 128), lambda b: (b, 0)),
                _layer_spec((8, qkv_w), layer, 1),
                _layer_spec((2, 128), layer, 1),
                _layer_spec((1, BRANCH_W), layer, 1)]
    args = [proj, proj, gab, cw8, par, norm_g]
    if has_s0:
        in_specs.append(_state_spec(layer))
        args.append(s0)
    st_in_specs, st_args, st_out_specs, st_shapes, aliases = _state_output(emit, prev, layer, batch, len(args))
    return pl.pallas_call(
        functools.partial(_gdn_kernel, seq=seq, has_s0=has_s0, layer=layer, emit=emit),
        grid=(batch,), in_specs=in_specs + st_in_specs,
        out_specs=[pl.BlockSpec((seq, BRANCH_W), lambda b: (b, 0))] + st_out_specs,
        out_shape=[jax.ShapeDtypeStruct((batch * seq, BRANCH_W), BF16)] + st_shapes,
        input_output_aliases=aliases,
        scratch_shapes=[pltpu.VMEM((seq, BRANCH_W), F32),
                        pltpu.VMEM((seq, BRANCH_W), F32),
                        pltpu.VMEM((seq, BRANCH_W), F32),
                        pltpu.VMEM((2, seq, BRANCH_W), F32),
                        pltpu.VMEM((2, seq, BRANCH_W), F32),
                        pltpu.VMEM((2, seq // CHUNK * 8, BRANCH_W), F32),
                        pltpu.VMEM((2, seq, BRANCH_W), BF16),
                        pltpu.VMEM((2, seq, BRANCH_W), F32),
                        pltpu.VMEM((2, seq, BRANCH_W), BF16),
                        pltpu.VMEM((2, seq, BRANCH_W), F32),
                        pltpu.VMEM((2, seq // CHUNK * 8, BRANCH_W), F32),
                        pltpu.VMEM((2, HEAD_DIM, BRANCH_W), F32),
                        pltpu.VMEM((2, seq, BRANCH_W), F32)],
        compiler_params=_params(48, 1),
        name="gdn",
    )(*args, *st_args)


_RET_LOG_GAMMA = [[float(np.log1p(-np.exp2(-(base + h)))) for h in range(N_HEADS)] for base in RET_DECAY_BASE]


def _ret_kernel(*refs, seq, n_seq, has_s0, layer, emit):
    qkv_refs, z_ref, g_ref = refs[:3], refs[3], refs[4]
    s0_ref = refs[5] if has_s0 else None
    o_ref, st_ref = (refs[-1], None) if emit == "none" else (refs[-2], refs[-1])
    tile = RET_TILE
    n_tiles = seq // tile
    problems = [(s, h) for s in range(n_seq) for h in range(N_HEADS)]
    heads = range(len(problems))
    lgf = [_RET_LOG_GAMMA[0][h] for _, h in problems]
    lgb = [_RET_LOG_GAMMA[1][h] for _, h in problems]
    a = lax.broadcasted_iota(jnp.int32, (tile, 1), 0).astype(F32)
    ef = [jnp.exp(a * lgf[h]) for h in heads]
    eif = [jnp.exp(-a * lgf[h]) for h in heads]
    eb = [jnp.exp(a * lgb[h]) for h in heads]
    eib = [jnp.exp(-a * lgb[h]) for h in heads]
    gf_tile = [float(np.exp(tile * lgf[h])) for h in heads]
    gb_tile = [float(np.exp(tile * lgb[h])) for h in heads]
    ii = lax.broadcasted_iota(jnp.int32, (tile, tile), 0)
    jj = lax.broadcasted_iota(jnp.int32, (tile, tile), 1)

    def head_cols(t, part, i):
        s, h = problems[i]
        return qkv_refs[part][s * seq + t * tile:s * seq + (t + 1) * tile, _hs(h)]

    def initial_state(d, i):
        s, h = problems[i]
        return s0_ref[s, d, h]

    kf = [[head_cols(t, 1, h) * eif[h] for h in heads] for t in range(n_tiles)]
    kb = [[head_cols(t, 1, h) * eb[h] for h in heads] for t in range(n_tiles)]
    vs = [[head_cols(t, 2, h) for h in heads] for t in range(n_tiles)]
    use_states = has_s0 or n_tiles > 1 or st_ref is not None
    if use_states:
        kvf = [[_mm_tn(kf[t][h], vs[t][h]) for h in heads] for t in range(n_tiles)]
        kvb = [[_mm_tn(kb[t][h], vs[t][h]) for h in heads] for t in range(n_tiles)]
        zero = jnp.zeros((HEAD_DIM, HEAD_DIM), F32)
        zf = [[(float(np.exp(lgf[h])) * initial_state(0, h)) if has_s0 else zero for h in heads]]
        for t in range(n_tiles):
            zf.append([gf_tile[h] * (zf[t][h] + kvf[t][h]) for h in heads])
        acc = [initial_state(1, h) if has_s0 else zero for h in heads]
        zb = [None] * n_tiles
        for t in reversed(range(n_tiles)):
            zb[t] = [gb_tile[h] * acc[h] for h in heads]
            acc = [zb[t][h] + kvb[t][h] for h in heads]
        if st_ref is not None:
            stf = [zf[n_tiles][h] * float(np.exp(-lgf[h])) for h in heads]
            for s in range(n_seq):
                _write_state(st_ref.at[s], layer, emit == "first",
                             lambda d, h, s=s: (stf, acc)[d][s * N_HEADS + h])
    tiles = [[] for _ in range(n_seq)]
    for t in range(n_tiles):
        q = [head_cols(t, 0, h) * SCALE for h in heads]
        qf = [q[h] * ef[h] for h in heads]
        qb = [q[h] * eib[h] for h in heads]
        sd = [jnp.where(ii >= jj, _mm_nt(qf[h], kf[t][h]), 0.0) + jnp.where(ii <= jj, _mm_nt(qb[h], kb[t][h]), 0.0)
              for h in heads]
        o = [_mm(sd[h], vs[t][h]) for h in heads]
        if has_s0 or n_tiles > 1:
            o = [o[h] + _mm(qf[h], zf[t][h]) + _mm(qb[h], zb[t][h]) for h in heads]
        ms = [jnp.mean(o[h] * o[h], axis=-1, keepdims=True) for h in heads]
        for s in range(n_seq):
            tiles[s].append(jnp.concatenate([o[h] * lax.rsqrt(ms[h] + EPS) * g_ref[...]
                                             for h in range(s * N_HEADS, (s + 1) * N_HEADS)], axis=-1))
    o = jnp.concatenate([tile_out for s in range(n_seq) for tile_out in tiles[s]], axis=0)
    o_ref[...] = (o * _silu(z_ref[...].astype(F32))).astype(BF16)


def _ret_call(proj, norm_g, layer, batch, seq, s0=None, emit="none", prev=None):
    has_s0 = s0 is not None
    n_seq = max(1, RET_ROWS_PER_STEP // seq)
    assert batch % n_seq == 0
    tm = n_seq * seq
    cq = OFF_C // BRANCH_W
    in_specs = [pl.BlockSpec((tm, BRANCH_W), lambda b, part=part: (b, cq + part)) for part in range(4)]
    in_specs.append(_layer_spec((1, HEAD_DIM), layer, 1))
    args = [proj, proj, proj, proj, norm_g]
    if has_s0:
        in_specs.append(_state_spec(layer, n_seq))
        args.append(s0)
    st_in_specs, st_args, st_out_specs, st_shapes, aliases = _state_output(emit, prev, layer, batch, len(args),
                                                                           n_seq)
    return pl.pallas_call(
        functools.partial(_ret_kernel, seq=seq, n_seq=n_seq, has_s0=has_s0, layer=layer, emit=emit),
        grid=(batch // n_seq,), in_specs=in_specs + st_in_specs,
        out_specs=[pl.BlockSpec((tm, BRANCH_W), lambda b: (b, 0))] + st_out_specs,
        out_shape=[jax.ShapeDtypeStruct((batch * seq, BRANCH_W), BF16)] + st_shapes,
        input_output_aliases=aliases,
        compiler_params=_params(48, 1),
        name="retention",
    )(*args, *st_args)


def _out_kernel(*refs, final):
    if final:
        (h_ref, mod_ref, g_ref, oa_ref, ob_ref, oc_ref, od_ref, wg_ref, wb_ref, wo_ref, fn_ref,
         o_ref, y_ref) = refs
    else:
        h_ref, mod_ref, g_ref, oa_ref, ob_ref, oc_ref, od_ref, wg_ref, wb_ref, wo_ref, o_ref = refs
    x = h_ref[...]
    mod = mod_ref[0]
    hn = _modulated_norm(x, mod, g_ref[...]).astype(BF16)
    merged = None
    for n, br_ref in enumerate((oa_ref, ob_ref, oc_ref, od_ref)):
        gate = jax.nn.sigmoid(_mm_nt(hn, wg_ref[n * D_MODEL:(n + 1) * D_MODEL, :]))
        up = jnp.dot(br_ref[...], wb_ref[n], preferred_element_type=F32)
        merged = gate * up if merged is None else merged + gate * up
    out = jnp.dot(merged.astype(BF16), wo_ref[...], preferred_element_type=F32)
    hnew = x + mod[:, 2 * D_MODEL:] * out
    o_ref[...] = hnew
    if final:
        ms = jnp.mean(hnew * hnew, axis=-1, keepdims=True)
        y_ref[...] = hnew * lax.rsqrt(ms + EPS) * fn_ref[...]


def _out_call(h2d, mod3, norm_g3, branches, wg, wb, wo, layer, rows_per_mod, final_norm=None):
    t = h2d.shape[0]
    tm = 512
    final = final_norm is not None
    if mod3.shape[0] == 1:
        mod_idx = lambda i: (0, 0, 0)
    else:
        mod_idx = lambda i: ((i * tm) // rows_per_mod, 0, 0)
    once = pl.Buffered(1)
    in_specs = [pl.BlockSpec((tm, D_MODEL), lambda i: (i, 0)),
                pl.BlockSpec((1, 1, 3 * D_MODEL), mod_idx),
                _layer_spec((1, D_MODEL), layer, 1)]
    in_specs += [pl.BlockSpec((tm, BRANCH_W), lambda i: (i, 0))] * N_BRANCH
    in_specs += [pl.BlockSpec((None, N_BRANCH * D_MODEL, D_MODEL), lambda i: (layer, 0, 0), pipeline_mode=once),
                 pl.BlockSpec((None, N_BRANCH, BRANCH_W, D_MODEL), lambda i: (layer, 0, 0, 0), pipeline_mode=once),
                 pl.BlockSpec((None, D_MODEL, D_MODEL), lambda i: (layer, 0, 0), pipeline_mode=once)]
    args = [h2d, mod3, norm_g3, *branches, wg, wb, wo]
    out_specs = [pl.BlockSpec((tm, D_MODEL), lambda i: (i, 0))]
    out_shape = [jax.ShapeDtypeStruct((t, D_MODEL), F32)]
    if final:
        in_specs.append(pl.BlockSpec((1, D_MODEL), lambda i: (0, 0)))
        args.append(final_norm.reshape(1, D_MODEL))
        out_specs.append(pl.BlockSpec((tm, D_MODEL), lambda i: (i, 0)))
        out_shape.append(jax.ShapeDtypeStruct((t, D_MODEL), F32))
    return pl.pallas_call(
        functools.partial(_out_kernel, final=final),
        grid=(t // tm,), in_specs=in_specs, out_specs=out_specs, out_shape=out_shape,
        compiler_params=_params(48, 1),
        name="merge_out_final" if final else "merge_out",
    )(*args)


def _prep_weights(w_in):
    offs = np.concatenate([[0], np.cumsum(IN_SPLITS)])
    seg = lambda i, j: w_in[:, :, offs[i]:offs[j]].astype(BF16).transpose(0, 2, 1)
    assert offs[5] == OFF_GZ and offs[15] - offs[6] == OFF_GAB - OFF_GZ
    pad = jnp.zeros((DEPTH, PROJ_W - OFF_GAB - IN_SPLITS[5], D_MODEL), BF16)
    w_proj = (seg(0, 5),
              seg(6, 15),
              jnp.concatenate([seg(5, 6), pad], axis=1))
    return w_proj, seg(15, 16)


def _rope_tables(seq):
    t = jnp.arange(seq)
    quarter = HEAD_DIM // 4
    inv = ROPE_THETA ** (-jnp.arange(quarter, dtype=F32) / quarter)

    def half(pos):
        ang = pos.astype(F32)[:, None] * inv
        c, s, zero = jnp.cos(ang), jnp.sin(ang), jnp.zeros_like(ang)
        return jnp.concatenate([c, c], -1), jnp.concatenate([-s, zero], -1), jnp.concatenate([zero, s], -1)

    parts = [jnp.concatenate([a, b], -1) for a, b in zip(half(t // GRID_W), half(t % GRID_W))]
    tab = jnp.stack(parts)
    return jnp.tile(tab, (1, 1, N_HEADS)), jnp.tile(tab, (1, 1, KV_HEADS))


def _layer(h2d, batch, seq, mod, pw, layer, ctx, caches, final_norm):
    proj, gab = _inproj_call(h2d, mod, pw["norm_g"], pw["w_proj"], layer, seq)
    kw = KV_HEADS * HEAD_DIM
    if ctx is None:
        emit = "first" if caches is None else "update"
        akv, nkv, sg_all, sr_all = caches or (None, None, None, None)
        oa, *akv = _attn_ctx_call(proj, akv, layer, batch, seq, OFF_AQKV, OFF_AQKV + BRANCH_W,
                                  OFF_AQKV + BRANCH_W + kw, OFF_AZ, KV_HEADS, pw["qn"], pw["kn"])
        od, *nkv = _attn_ctx_call(proj, nkv, layer, batch, seq, OFF_D, OFF_D + BRANCH_W,
                                  OFF_D + 2 * BRANCH_W, OFF_D + 3 * BRANCH_W, N_HEADS)
        ob, sg_all = _gdn_call(proj, gab, pw["cw8"], pw["gdn_par"], pw["gdn_norm"], layer, batch, seq,
                               emit=emit, prev=sg_all)
        oc, sr_all = _ret_call(proj, pw["ret_norm"], layer, batch, seq, emit=emit, prev=sr_all)
        caches = (akv, nkv, sg_all, sr_all)
    else:
        oa = _attn_lat_call(proj, ctx["akt"], ctx["avt"], layer, batch, seq, ctx["qtab"], ctx["ktab"],
                            pw["qn"], pw["kn"])
        od = _na_call(proj, ctx["nkt"], ctx["nvt"], ctx["tb"], layer, batch, seq)
        ob, = _gdn_call(proj, gab, pw["cw8"], pw["gdn_par"], pw["gdn_norm"], layer, batch, seq, s0=ctx["sg"])
        oc, = _ret_call(proj, pw["ret_norm"], layer, batch, seq, s0=ctx["sr"])
    outs = _out_call(h2d, mod, pw["norm_g"], (oa, ob, oc, od), pw["wg"], pw["wb"], pw["wo"], layer, seq, final_norm)
    return outs, caches


def kernel(x_prompt, x_sample, cache_attn_k, cache_attn_v, cache_na_k, cache_na_v, state_gdn, state_ret, c, c_ctx, w_ada, b_ada, norm_g, w_in, conv_w, gdn_a_log, gdn_dt_bias, gdn_norm, attn_q_norm, attn_k_norm, ret_norm, na_bias, w_branch, w_out, final_norm):
    batch, seq, _ = x_prompt.shape
    dbatch, dseq, _ = x_sample.shape
    assert dbatch == 8, "the modulation kernel handles exactly one sublane tile of conditioning rows"

    w_proj, wg = _prep_weights(w_in)
    par = jnp.zeros((DEPTH, 2, 128), F32)
    par = par.at[:, 0, 8:16].set(gdn_dt_bias.reshape(DEPTH, 8)).at[:, 1, 8:16].set(gdn_a_log.reshape(DEPTH, 8))
    pw = dict(
        w_ada=w_ada, b_ada=b_ada.reshape(DEPTH, 1, 3 * D_MODEL), norm_g=norm_g.reshape(DEPTH, 1, D_MODEL),
        w_proj=w_proj, wg=wg, wb=w_branch.astype(BF16), wo=w_out.astype(BF16),
        cw8=jnp.concatenate([conv_w, jnp.zeros((DEPTH, 8 - SHORT_CONV, 3 * BRANCH_W), F32)], axis=1),
        gdn_par=par,
        gdn_norm=jnp.tile(gdn_norm, (1, N_HEADS)).reshape(DEPTH, 1, BRANCH_W),
        ret_norm=ret_norm.reshape(DEPTH, 1, HEAD_DIM),
        qn=jnp.tile(attn_q_norm, (1, N_HEADS)).reshape(DEPTH, 1, BRANCH_W),
        kn=jnp.tile(attn_k_norm, (1, KV_HEADS)).reshape(DEPTH, 1, KV_HEADS * HEAD_DIM))

    cond = jnp.concatenate([jnp.broadcast_to(c_ctx, (8, D_MODEL)), c], axis=0)
    mods = _mod_call(cond, w_ada, pw["b_ada"])

    h = x_prompt.reshape(batch * seq, D_MODEL)
    caches = None
    for l in range(DEPTH):
        outs, caches = _layer(h, batch, seq, mods[l, 0:1].reshape(1, 1, 3 * D_MODEL), pw, l, None, caches,
                              final_norm if l == DEPTH - 1 else None)
        h = outs[0]
    y_prompt = outs[1].reshape(batch, seq, D_MODEL)
    token_major = lambda a: a.transpose(0, 1, 4, 2, 3)
    (akt, avt), (nkt, nvt), new_state_gdn, new_state_ret = caches
    new_attn_k, new_attn_v, new_na_k, new_na_v = (token_major(a) for a in (akt, avt, nkt, nvt))

    qtab, ktab = _rope_tables(dseq)
    feature_major = lambda a: a.transpose(0, 1, 3, 4, 2)
    ctx = dict(akt=feature_major(cache_attn_k), avt=feature_major(cache_attn_v),
               nkt=feature_major(cache_na_k), nvt=feature_major(cache_na_v),
               sg=state_gdn, sr=state_ret, tb=_na_bias_call(na_bias), qtab=qtab, ktab=ktab)
    h = x_sample.reshape(dbatch * dseq, D_MODEL)
    for l in range(DEPTH):
        outs, _ = _layer(h, dbatch, dseq, mods[l, 8:16].reshape(dbatch, 1, 3 * D_MODEL), pw, l, ctx, None,
                         final_norm if l == DEPTH - 1 else None)
        h = outs[0]
    y_sample = outs[1].reshape(dbatch, dseq, D_MODEL)
    return (y_prompt, y_sample, new_attn_k, new_attn_v, new_na_k, new_na_v, new_state_gdn, new_state_ret)
```

```python
import functools

import numpy as np
import jax
import jax.numpy as jnp
from jax import lax
from jax.experimental import pallas as pl
from jax.experimental.pallas import tpu as pltpu

F32 = jnp.float32
BF16 = jnp.bfloat16

D_MODEL = 1024
HEAD_DIM = 64
HEAD_SHIFT = HEAD_DIM.bit_length() - 1
N_HEADS = 4
KV_HEADS = N_HEADS // 2
BRANCH_W = N_HEADS * HEAD_DIM
N_BRANCH = 4
DEPTH = 2
GRID_W = 64
CHUNK = 64
PREP_CHUNKS = 4
GDN_TILE_HEADS = 2
assert CHUNK == HEAD_DIM
SHORT_CONV = 5
NA_ROWS = 8
NA_COLS = 16
RET_ROWS_PER_STEP = 1024
CTX_SEQS_PER_STEP = 4
NA_ROWS_PER_STEP = 8
N_DR = 2 * NA_ROWS - 1
N_DC = 2 * NA_COLS - 1
ROPE_THETA = 10000.0
RET_DECAY_BASE = (5.0, 5.5)
RET_TILE = 128
EPS = 1e-6
SCALE = HEAD_DIM ** -0.5
LOG2E = 1.4426950408889634
NEG_INF = float("-inf")

IN_SPLITS = (256, 128, 128, 256, 768, 16, 256, 256, 256, 256, 256, 256, 256, 256, 256, 4096)
PROJ_W = 4096
OFF_AQKV = 0
OFF_AZ = 512
OFF_GQKV = 768
OFF_GZ = 1536
OFF_C = 1792
OFF_D = 2816
OFF_GAB = 3840

V7X_VMEM_BYTES = 64 * 1024 * 1024
MIB = 1024 * 1024


def _params(vmem_mib, n_axes):
    assert vmem_mib * MIB < V7X_VMEM_BYTES
    return pltpu.CompilerParams(dimension_semantics=("arbitrary",) * n_axes,
                                vmem_limit_bytes=vmem_mib * MIB)


def _layer_spec(block, layer, n_grid):
    zeros = (0,) * len(block)
    if n_grid == 1:
        return pl.BlockSpec((None,) + block, lambda i: (layer,) + zeros)
    return pl.BlockSpec((None,) + block, lambda i, j: (layer,) + zeros)


def _mm(a, b):
    return jnp.dot(a.astype(BF16), b.astype(BF16), preferred_element_type=F32)


def _mm_nt(a, b):
    return lax.dot_general(a.astype(BF16), b.astype(BF16), (((1,), (1,)), ((), ())),
                           preferred_element_type=F32)


def _mm_tn(a, b):
    return lax.dot_general(a.astype(BF16), b.astype(BF16), (((0,), (0,)), ((), ())),
                           preferred_element_type=F32)


def _split3(x):
    hi = x.astype(BF16)
    r = x - hi.astype(F32)
    mid = r.astype(BF16)
    lo = (r - mid.astype(F32)).astype(BF16)
    return hi, mid, lo


def _mm_exact(sel, x, terms=3):
    return sum(jnp.dot(sel, part, preferred_element_type=F32) for part in _split3(x)[:terms])


def _mm_exact_lhs(x, sel, terms=3):
    return sum(jnp.dot(part, sel, preferred_element_type=F32) for part in _split3(x)[:terms])


def _silu(x):
    return x * jax.nn.sigmoid(x)


def _head_block_matrix(width, value):
    ri = lax.broadcasted_iota(jnp.int32, (width, width), 0) >> HEAD_SHIFT
    ci = lax.broadcasted_iota(jnp.int32, (width, width), 1) >> HEAD_SHIFT
    return jnp.where(ri == ci, value, 0.0).astype(BF16)


def _head_reduce(x, g, terms=2):
    return sum(jnp.dot(part, g, preferred_element_type=F32) for part in _split3(x)[:terms])


def _head_rms(x):
    ms = _head_reduce(x * x, _head_block_matrix(x.shape[1], 1.0 / HEAD_DIM))
    return x * lax.rsqrt(ms + EPS)


def _rope(x, tab_ref):
    w = x.shape[1]
    return (x * tab_ref[0] + pltpu.roll(x, w - 16, 1) * tab_ref[1] + pltpu.roll(x, 16, 1) * tab_ref[2])


def _attend(qs, parts):
    groups = range(len(qs))
    qs = [(q.astype(F32) * (SCALE * LOG2E)).astype(BF16) for q in qs]

    def score(q, part):
        k, _, bias, feature_major = part
        s = _mm(q, k) if feature_major else _mm_nt(q, k)
        return s if bias is None else s + bias

    scores = [[score(qs[g], part) for part in parts[g]] for g in groups]
    m = [functools.reduce(jnp.maximum, [s.max(axis=-1, keepdims=True) for s in scores[g]]) for g in groups]
    p = [[jnp.exp2(s - m[g]) for s in scores[g]] for g in groups]
    den = [sum(x.sum(axis=-1, keepdims=True) for x in p[g]) for g in groups]
    out = [sum(_mm_nt(x, part[1]) if part[3] else _mm(x, part[1]) for x, part in zip(p[g], parts[g]))
           for g in groups]
    return [out[g] / den[g] for g in groups]


def _hs(h):
    return slice(h * HEAD_DIM, (h + 1) * HEAD_DIM)


def _aligned(x, m):
    return x if isinstance(x, int) else pl.multiple_of(x, m)


def _mod_kernel(c_ref, w_ref, b_ref, o_ref):
    o_ref[...] = _mm(_silu(c_ref[...]), w_ref[...]) + b_ref[...]


def _mod_call(cond, w_ada, b_ada3):
    tn = 512
    rows = cond.shape[0]
    return pl.pallas_call(
        _mod_kernel,
        grid=(DEPTH, 3 * D_MODEL // tn),
        in_specs=[pl.BlockSpec((rows, D_MODEL), lambda l, j: (0, 0)),
                  pl.BlockSpec((None, D_MODEL, tn), lambda l, j: (l, 0, j)),
                  pl.BlockSpec((None, 1, tn), lambda l, j: (l, 0, j))],
        out_specs=pl.BlockSpec((None, rows, tn), lambda l, j: (l, 0, j)),
        out_shape=jax.ShapeDtypeStruct((DEPTH, rows, 3 * D_MODEL), F32),
        compiler_params=_params(24, 2),
        name="adaln_mod",
    )(cond, w_ada, b_ada3)


def _modulated_norm(x, mod, g):
    ms = jnp.mean(x * x, axis=-1, keepdims=True)
    y = x * lax.rsqrt(ms + EPS) * g
    return y * (1.0 + mod[:, D_MODEL:2 * D_MODEL]) + mod[:, :D_MODEL]


def _inproj_kernel(x_ref, mod_ref, g_ref, *refs):
    *w_refs, o_ref, ab_ref = refs
    hn = _modulated_norm(x_ref[...], mod_ref[0], g_ref[...]).astype(BF16)
    tn = 256
    col = 0
    for w_ref in w_refs:
        for j in range(w_ref.shape[0] // tn):
            y = _mm_nt(hn, w_ref[j * tn:(j + 1) * tn, :])
            o_ref[:, col:col + tn] = y.astype(BF16)
            if col == OFF_GAB:
                ab_ref[...] = y[:, :128]
            col += tn
    assert col == PROJ_W


def _inproj_call(x2d, mod3, norm_g3, weights, layer, rows_per_mod):
    t = x2d.shape[0]
    tm = 1024
    if mod3.shape[0] == 1:
        mod_idx = lambda i: (0, 0, 0)
    else:
        mod_idx = lambda i: ((i * tm) // rows_per_mod, 0, 0)
    w_specs = [pl.BlockSpec((None, w.shape[1], D_MODEL), lambda i: (layer, 0, 0), pipeline_mode=pl.Buffered(1))
               for w in weights]
    return pl.pallas_call(
        _inproj_kernel,
        grid=(t // tm,),
        in_specs=[pl.BlockSpec((tm, D_MODEL), lambda i: (i, 0)),
                  pl.BlockSpec((1, 1, 3 * D_MODEL), mod_idx),
                  _layer_spec((1, D_MODEL), layer, 1)] + w_specs,
        out_specs=[pl.BlockSpec((tm, PROJ_W), lambda i: (i, 0)), pl.BlockSpec((tm, 128), lambda i: (i, 0))],
        out_shape=[jax.ShapeDtypeStruct((t, PROJ_W), BF16), jax.ShapeDtypeStruct((t, 128), F32)],
        compiler_params=_params(40, 1),
        name="inproj",
    )(x2d, mod3, norm_g3, *weights)


def _stacked_heads(q, n_kv):
    rep = N_HEADS // n_kv
    return [jnp.concatenate([q[:, _hs(g * rep + r)] for r in range(rep)], axis=0) for g in range(n_kv)]


def _unstack_heads(outs, n_kv):
    rep = N_HEADS // n_kv
    m = outs[0].shape[0] // rep
    return jnp.concatenate([outs[g][r * m:(r + 1) * m] for g in range(n_kv) for r in range(rep)], axis=-1)


def _write_layer(ref, layer, value, stacked):
    if not stacked:
        ref[...] = value
        return
    for l in range(ref.shape[0]):
        ref[l] = value if l == layer else jnp.zeros(value.shape, value.dtype)


def _write_state(st_ref, layer, stacked, piece):
    for d in range(2):
        for h in range(N_HEADS):
            value = piece(d, h)
            if stacked:
                for l in range(st_ref.shape[0]):
                    st_ref[l, d, h] = value if l == layer else jnp.zeros(value.shape, value.dtype)
            else:
                st_ref[d, h] = value


def _attn_ctx_kernel(*refs, n_kv, norm, layer, first):
    if norm:
        q_ref, k_ref, v_ref, z_ref, qn_ref, kn_ref = refs[:6]
    else:
        q_ref, k_ref, v_ref, z_ref = refs[:4]
    o_ref, kt_ref, vt_ref = refs[-3:]
    q, k, v, z = q_ref[...], k_ref[...].astype(F32), v_ref[...].astype(F32), z_ref[...].astype(F32)
    if norm:
        q = _head_rms(q.astype(F32)) * qn_ref[...]
        k = _head_rms(k) * kn_ref[...]
    n_seq = kt_ref.shape[0]
    seq = k.shape[0] // n_seq
    qs, parts = [], []
    for s in range(n_seq):
        rows = slice(s * seq, (s + 1) * seq)
        _write_layer(kt_ref.at[s], layer, k[rows].T.reshape(n_kv, HEAD_DIM, seq), first)
        _write_layer(vt_ref.at[s], layer, v[rows].T.reshape(n_kv, HEAD_DIM, seq), first)
        qs += _stacked_heads(q[rows], n_kv)
        parts += [[(k[rows, _hs(g)], v[rows, _hs(g)], None, False)] for g in range(n_kv)]
    outs = _attend(qs, parts)
    o = jnp.concatenate([_unstack_heads(outs[s * n_kv:(s + 1) * n_kv], n_kv) for s in range(n_seq)], axis=0)
    o_ref[...] = (o * _silu(z)).astype(BF16)


def _attn_ctx_call(proj, prev, layer, batch, seq, off_q, off_k, off_v, off_z, n_kv, qn=None, kn=None):
    t = batch * seq
    kvw = n_kv * HEAD_DIM
    norm = qn is not None
    first = prev is None
    n_seq = CTX_SEQS_PER_STEP
    assert batch % n_seq == 0
    tm = n_seq * seq
    in_specs = [pl.BlockSpec((tm, BRANCH_W), lambda b: (b, off_q // BRANCH_W)),
                pl.BlockSpec((tm, kvw), lambda b: (b, off_k // kvw)),
                pl.BlockSpec((tm, kvw), lambda b: (b, off_v // kvw)),
                pl.BlockSpec((tm, BRANCH_W), lambda b: (b, off_z // BRANCH_W))]
    args = [proj, proj, proj, proj]
    if norm:
        in_specs += [_layer_spec((1, BRANCH_W), layer, 1), _layer_spec((1, kvw), layer, 1)]
        args += [qn, kn]
    aliases = {}
    if first:
        cache_spec = pl.BlockSpec((n_seq, DEPTH, n_kv, HEAD_DIM, seq), lambda b: (b, 0, 0, 0, 0))
    else:
        aliases = {len(args): 1, len(args) + 1: 2}
        in_specs += [pl.BlockSpec(memory_space=pl.ANY)] * 2
        args += list(prev)
        cache_spec = pl.BlockSpec((n_seq, None, n_kv, HEAD_DIM, seq), lambda b: (b, layer, 0, 0, 0))
    cache_shape = jax.ShapeDtypeStruct((batch, DEPTH, n_kv, HEAD_DIM, seq), F32)
    return pl.pallas_call(
        functools.partial(_attn_ctx_kernel, n_kv=n_kv, norm=norm, layer=layer, first=first),
        grid=(batch // n_seq,), in_specs=in_specs,
        out_specs=[pl.BlockSpec((tm, BRANCH_W), lambda b: (b, 0)), cache_spec, cache_spec],
        out_shape=[jax.ShapeDtypeStruct((t, BRANCH_W), BF16), cache_shape, cache_shape],
        input_output_aliases=aliases,
        compiler_params=_params(32, 1),
        name="attn_ctx_norm" if norm else "attn_ctx",
    )(*args)


def _attn_lat_kernel(q_ref, kv_ref, z_ref, ckt_ref, cvt_ref, qtab_ref, ktab_ref, qn_ref, kn_ref, o_ref,
                     k_s, v_s):
    kw = KV_HEADS * HEAD_DIM

    @pl.when(pl.program_id(1) == 0)
    def _():
        kv = kv_ref[...]
        k_s[...] = _rope(_head_rms(kv[:, :kw].astype(F32)) * kn_ref[...], ktab_ref).astype(BF16)
        v_s[...] = kv[:, kw:]

    q = _rope(_head_rms(q_ref[...].astype(F32)) * qn_ref[...], qtab_ref)
    k, v = k_s[...], v_s[...]
    outs = _attend(_stacked_heads(q, KV_HEADS),
                   [[(k[:, _hs(g)], v[:, _hs(g)], None, False), (ckt_ref[g], cvt_ref[g], None, True)]
                    for g in range(KV_HEADS)])
    o_ref[...] = (_unstack_heads(outs, KV_HEADS) * _silu(z_ref[...].astype(F32))).astype(BF16)


def _attn_lat_call(proj, cache_kt, cache_vt, layer, batch, seq, qtab, ktab, qn, kn):
    tq = 512
    nq = seq // tq
    past = cache_kt.shape[-1]
    kw = KV_HEADS * HEAD_DIM
    ctx_spec = pl.BlockSpec((None, None, KV_HEADS, HEAD_DIM, past), lambda b, i: (b, layer, 0, 0, 0))
    return pl.pallas_call(
        _attn_lat_kernel,
        grid=(batch, nq),
        in_specs=[pl.BlockSpec((tq, BRANCH_W), lambda b, i: (b * nq + i, OFF_AQKV // BRANCH_W)),
                  pl.BlockSpec((seq, 2 * kw), lambda b, i: (b, (OFF_AQKV + BRANCH_W) // (2 * kw))),
                  pl.BlockSpec((tq, BRANCH_W), lambda b, i: (b * nq + i, OFF_AZ // BRANCH_W)),
                  ctx_spec, ctx_spec,
                  pl.BlockSpec((3, tq, BRANCH_W), lambda b, i: (0, i, 0)),
                  pl.BlockSpec((3, seq, kw), lambda b, i: (0, 0, 0)),
                  _layer_spec((1, BRANCH_W), layer, 2),
                  _layer_spec((1, kw), layer, 2)],
        out_specs=pl.BlockSpec((tq, BRANCH_W), lambda b, i: (b * nq + i, 0)),
        out_shape=jax.ShapeDtypeStruct((batch * seq, BRANCH_W), BF16),
        scratch_shapes=[pltpu.VMEM((seq, kw), BF16), pltpu.VMEM((seq, kw), BF16)],
        compiler_params=_params(40, 2),
        name="attn_lat",
    )(proj, proj, proj, cache_kt, cache_vt, qtab, ktab, qn, kn)


def _na_bias_kernel(t_ref, o_ref):
    nblk = o_ref.shape[0]
    c = lax.broadcasted_iota(jnp.int32, (GRID_W, 2 * GRID_W), 0)
    kc = lax.broadcasted_iota(jnp.int32, (GRID_W, 2 * GRID_W), 1) & (GRID_W - 1)
    cs = jnp.clip(c - NA_COLS // 2, 0, GRID_W - NA_COLS)
    valid = jnp.logical_and(kc >= cs, kc < cs + NA_COLS)

    unroll = 8
    assert nblk % unroll == 0

    def body(i, carry):
        rows8 = t_ref[pl.ds(pl.multiple_of(i * unroll, unroll), unroll), :]
        for u in range(unroll):
            row = jnp.broadcast_to(rows8[u:u + 1, :], (GRID_W, 2 * GRID_W))
            skewed = pltpu.roll(row, 2 * GRID_W - (NA_COLS - 1), 1, stride=1, stride_axis=0)
            o_ref[i * unroll + u] = jnp.where(valid, skewed * LOG2E, NEG_INF)
        return carry

    lax.fori_loop(0, nblk // unroll, body, 0)


def _na_bias_call(na_bias):
    nblk = DEPTH * N_HEADS * N_DR
    rows = jnp.pad(na_bias.reshape(nblk, N_DC), ((0, 1), (0, GRID_W - N_DC)))
    pairs = jnp.concatenate([rows[:-1], rows[1:]], axis=1)
    return pl.pallas_call(
        _na_bias_kernel,
        in_specs=[pl.BlockSpec((nblk, 2 * GRID_W), lambda: (0, 0))],
        out_specs=pl.BlockSpec((nblk, GRID_W, 2 * GRID_W), lambda: (0, 0, 0)),
        out_shape=jax.ShapeDtypeStruct((nblk, GRID_W, 2 * GRID_W), F32),
        name="na_bias",
    )(pairs)


def _na_kernel(q_ref, k_ref, v_ref, z_ref, ckt_ref, cvt_ref, tb_ref, o_ref, kh_s, vh_s, *, rows):
    win = NA_ROWS * GRID_W

    @pl.when(pl.program_id(1) == 0)
    def _():
        for h in range(N_HEADS):
            kh_s[h] = k_ref[:, _hs(h)]
            vh_s[h] = v_ref[:, _hs(h)]

    qs, parts = [], []
    for i in range(NA_ROWS_PER_STEP):
        r = pl.program_id(1) * NA_ROWS_PER_STEP + i
        rs = jnp.clip(r - NA_ROWS // 2, 0, rows - NA_ROWS)
        r0 = pl.multiple_of(rs * GRID_W, GRID_W)
        q = q_ref[i * GRID_W:(i + 1) * GRID_W, :]
        dr0 = rs - r + NA_ROWS - 1
        for h in range(N_HEADS):
            bias = jnp.concatenate([tb_ref[h * N_DR + dr0 + 2 * p] for p in range(NA_ROWS // 2)], axis=1)
            qs.append(q[:, _hs(h)])
            parts.append([(kh_s[h, pl.ds(r0, win), :], vh_s[h, pl.ds(r0, win), :], bias, False),
                          (ckt_ref[h], cvt_ref[h], None, True)])
    outs = _attend(qs, parts)
    o = jnp.concatenate([jnp.concatenate(outs[i * N_HEADS:(i + 1) * N_HEADS], axis=-1)
                         for i in range(NA_ROWS_PER_STEP)], axis=0)
    o_ref[...] = (o * _silu(z_ref[...].astype(F32))).astype(BF16)


def _na_call(proj, cache_kt, cache_vt, tb, layer, batch, seq):
    rows = seq // GRID_W
    assert rows >= NA_ROWS and rows % NA_ROWS_PER_STEP == 0
    steps = rows // NA_ROWS_PER_STEP
    tq = NA_ROWS_PER_STEP * GRID_W
    past = cache_kt.shape[-1]
    nblk = N_HEADS * N_DR
    cq = OFF_D // BRANCH_W
    ctx_spec = pl.BlockSpec((None, None, N_HEADS, HEAD_DIM, past), lambda b, r: (b, layer, 0, 0, 0))
    return pl.pallas_call(
        functools.partial(_na_kernel, rows=rows),
        grid=(batch, steps),
        in_specs=[pl.BlockSpec((tq, BRANCH_W), lambda b, r: (b * steps + r, cq)),
                  pl.BlockSpec((seq, BRANCH_W), lambda b, r: (b, cq + 1)),
                  pl.BlockSpec((seq, BRANCH_W), lambda b, r: (b, cq + 2)),
                  pl.BlockSpec((tq, BRANCH_W), lambda b, r: (b * steps + r, cq + 3)),
                  ctx_spec, ctx_spec,
                  pl.BlockSpec((nblk, GRID_W, 2 * GRID_W), lambda b, r: (layer, 0, 0))],
        out_specs=pl.BlockSpec((tq, BRANCH_W), lambda b, r: (b * steps + r, 0)),
        out_shape=jax.ShapeDtypeStruct((batch * seq, BRANCH_W), BF16),
        scratch_shapes=[pltpu.VMEM((N_HEADS, seq, HEAD_DIM), BF16), pltpu.VMEM((N_HEADS, seq, HEAD_DIM), BF16)],
        compiler_params=_params(32, 2),
        name="na_lat",
    )(proj, proj, proj, proj, cache_kt, cache_vt, tb)


def _gdn_kernel(*refs, seq, has_s0, layer, emit):
    qkv_ref, z_ref, ab_ref, cw_ref, par_ref, g_ref = refs[:6]
    s0_ref = refs[6] if has_s0 else None
    (q_s, k_s, v_s, gcb_s, bcb_s, r_s, mc_s, nc_s, qp_s, op_s, egl_s, s_s, oacc_ref) = refs[-13:]
    if emit == "none":
        o_ref, st_ref = refs[-14], None
    else:
        o_ref, st_ref = refs[-15], refs[-14]
    n_chunks = seq // CHUNK
    n_levels = CHUNK.bit_length() - 1
    qkv_w = 3 * BRANCH_W
    half = SHORT_CONV // 2
    pair_w = GDN_TILE_HEADS * HEAD_DIM
    pairs = [slice(p * pair_w, (p + 1) * pair_w) for p in range(BRANCH_W // pair_w)]
    tr = 256
    cpt = tr // CHUNK
    head_sum = _head_block_matrix(BRANCH_W, 1.0)

    gc_i = lax.broadcasted_iota(jnp.int32, (128, BRANCH_W), 0)
    gh_j = lax.broadcasted_iota(jnp.int32, (128, BRANCH_W), 1) >> HEAD_SHIFT
    sel_beta = [jnp.where(gc_i == gh_j + 4 * d, 1.0, 0.0).astype(BF16) for d in range(2)]
    sel_gate = [jnp.where(gc_i == gh_j + 8 + 4 * d, 1.0, 0.0).astype(BF16) for d in range(2)]
    ti = lax.broadcasted_iota(jnp.int32, (tr, tr), 0)
    tj = lax.broadcasted_iota(jnp.int32, (tr, tr), 1)
    same_chunk = (ti >> HEAD_SHIFT) == (tj >> HEAD_SHIFT)
    tri = [jnp.where(jnp.logical_and(same_chunk, ti >= tj), 1.0, 0.0).astype(BF16),
           jnp.where(jnp.logical_and(same_chunk, ti <= tj), 1.0, 0.0).astype(BF16)]
    assert tr == BRANCH_W
    lane_head = lax.broadcasted_iota(jnp.int32, (1, BRANCH_W), 1) >> HEAD_SHIFT

    halo = 16
    edge = 8
    assert half <= edge
    si = lax.broadcasted_iota(jnp.int32, (tr, tr), 0)
    sj = lax.broadcasted_iota(jnp.int32, (tr, tr), 1)
    ei = lax.broadcasted_iota(jnp.int32, (edge, halo), 0)
    ej = lax.broadcasted_iota(jnp.int32, (edge, halo), 1)
    taps = [j for j in range(SHORT_CONV) if j != half]
    shift = {j: jnp.where(sj == si + (j - half), 1.0, 0.0).astype(BF16) for j in taps}
    shift_before = {j: jnp.where(ej == ei + (halo + j - half), 1.0, 0.0).astype(BF16) for j in taps if j < half}
    shift_after = {j: jnp.where(ej == ei + (j - half - edge), 1.0, 0.0).astype(BF16) for j in taps if j > half}
    for t in range(seq // tr):
        rows = slice(t * tr, (t + 1) * tr)
        x = qkv_ref[rows, :]
        y = x.astype(F32) * cw_ref[half:half + 1, :]
        for j in taps:
            y = y + jnp.dot(shift[j], x, preferred_element_type=F32) * cw_ref[j:j + 1, :]
        if t > 0:
            before = qkv_ref[t * tr - halo:t * tr, :]
            top = sum(jnp.dot(shift_before[j], before, preferred_element_type=F32) * cw_ref[j:j + 1, :]
                      for j in shift_before)
            y = jnp.concatenate([y[:edge] + top, y[edge:]], axis=0)
        if (t + 1) * tr < seq:
            after = qkv_ref[(t + 1) * tr:(t + 1) * tr + halo, :]
            bottom = sum(jnp.dot(shift_after[j], after, preferred_element_type=F32) * cw_ref[j:j + 1, :]
                         for j in shift_after)
            y = jnp.concatenate([y[:tr - edge], y[tr - edge:] + bottom], axis=0)
        y = _silu(y)
        qq, kk = y[:, :BRANCH_W], y[:, BRANCH_W:2 * BRANCH_W]
        q_s[rows, :] = qq * lax.rsqrt(_head_reduce(qq * qq, head_sum, terms=1) + EPS) * SCALE
        k_s[rows, :] = kk * lax.rsqrt(_head_reduce(kk * kk, head_sum, terms=1) + EPS)
        v_s[rows, :] = y[:, 2 * BRANCH_W:]
        x = ab_ref[rows, :]
        beta = jax.nn.sigmoid(x)
        xs = x + par_ref[0:1, :]
        softplus = jnp.maximum(xs, 0.0) + jnp.log1p(jnp.exp(-jnp.abs(xs)))
        la = -jnp.exp(par_ref[1:2, :]) * softplus
        for d in range(2):
            gc = _mm_exact(tri[d], la, terms=2)
            gcb_s[d, rows, :] = _mm_exact_lhs(gc, sel_gate[d], terms=2)
            bcb_s[d, rows, :] = _mm_exact_lhs(beta, sel_beta[d], terms=2)
            gt = gc.T[8:16, :]
            shifted = {s: (gt if s == 0 else pltpu.roll(gt, (s * HEAD_DIM) % tr, 1))
                       for s in range(1 - cpt, N_HEADS)}
            for c in range(cpt):
                r = jnp.zeros((1, BRANCH_W), F32)
                for h in range(N_HEADS):
                    r = jnp.where(lane_head == h, shifted[h - c][4 * d + h:4 * d + h + 1, :], r)
                r_s[d, (t * cpt + c) * 8:(t * cpt + c + 1) * 8, :] = jnp.broadcast_to(r, (8, BRANCH_W))

    for d in range(2):
        if has_s0:
            s_s[d] = jnp.concatenate([s0_ref[d, h] for h in range(N_HEADS)], axis=-1)
        else:
            s_s[d] = jnp.zeros((HEAD_DIM, BRANCH_W), F32)

    li = lax.broadcasted_iota(jnp.int32, (CHUNK, BRANCH_W), 0)
    lj = lax.broadcasted_iota(jnp.int32, (CHUNK, BRANCH_W), 1) & (HEAD_DIM - 1)
    incl = (li >= lj, li <= lj)
    strict = (li > lj, li < lj)
    level = [((li ^ lj) >> l) == 1 for l in range(n_levels)]
    tile_heads = pair_w // HEAD_DIM
    head_in_tile = lax.broadcasted_iota(jnp.int32, (CHUNK, pair_w), 1) >> HEAD_SHIFT

    def diag_blocks(full):
        out = full[:HEAD_DIM]
        for a in range(1, tile_heads):
            out = jnp.where(head_in_tile == a, full[a * HEAD_DIM:(a + 1) * HEAD_DIM], out)
        return out

    def expand(y):
        yb = y.astype(BF16)
        zero = jnp.zeros((CHUNK, pair_w), BF16)
        return [jnp.concatenate([jnp.where(head_in_tile == a, yb[:, p], zero) for a in range(tile_heads)], axis=0)
                for p in pairs]

    def bdmm(x, ybd):
        xb = x.astype(BF16)
        return jnp.concatenate([jnp.dot(xb[:, p], ybd[i], preferred_element_type=F32)
                                for i, p in enumerate(pairs)], axis=1)

    def bdmm_nt(x, ybd):
        xb = x.astype(BF16)
        return jnp.concatenate([lax.dot_general(xb[:, p], ybd[i], (((1,), (1,)), ((), ())),
                                                preferred_element_type=F32)
                                for i, p in enumerate(pairs)], axis=1)

    def bdmm2(x, y1, y2):
        xb = x.astype(BF16)
        e1, e2 = expand(y1), expand(y2)
        outs = [jnp.dot(xb[:, p], jnp.concatenate([e1[i], e2[i]], axis=1), preferred_element_type=F32)
                for i, p in enumerate(pairs)]
        return (jnp.concatenate([o[:, :pair_w] for o in outs], axis=1),
                jnp.concatenate([o[:, pair_w:] for o in outs], axis=1))

    def tn_diag2(a, b1, b2):
        ab, b1b, b2b = a.astype(BF16), b1.astype(BF16), b2.astype(BF16)
        outs1, outs2 = [], []
        for p in pairs:
            full = lax.dot_general(ab[:, p], jnp.concatenate([b1b[:, p], b2b[:, p]], axis=1),
                                   (((0,), (0,)), ((), ())), preferred_element_type=F32)
            outs1.append(diag_blocks(full[:, :pair_w]))
            outs2.append(diag_blocks(full[:, pair_w:]))
        return jnp.concatenate(outs1, axis=1), jnp.concatenate(outs2, axis=1)

    def prepare(chains):
        n = range(len(chains))
        dd = [d for d, _ in chains]
        rows = [pl.ds(_aligned(c * CHUNK, CHUNK), CHUNK) for _, c in chains]
        gcb = [gcb_s[dd[i], rows[i], :] for i in n]
        bcb = [bcb_s[dd[i], rows[i], :] for i in n]
        grow = [r_s[dd[i], pl.ds(_aligned(chains[i][1] * 8, 8), 8), :][0:1, :] for i in n]
        dm = [jnp.exp(jnp.where(incl[dd[i]], gcb[i] - grow[i], NEG_INF)) for i in n]
        k = [k_s[rows[i], :] for i in n]
        q = [q_s[rows[i], :] for i in n]
        v = [v_s[rows[i], :] for i in n]
        kq = [bdmm_nt(jnp.concatenate([k[i], q[i]], axis=0), expand(k[i])) for i in n]
        a = [jnp.where(strict[dd[i]], bcb[i] * kq[i][:CHUNK] * dm[i], 0.0) for i in n]
        tm = [-jnp.where(level[0], a[i], 0.0) for i in n]
        for l in range(1, n_levels):
            b = [jnp.where(level[l], a[i], 0.0) for i in n]
            y = [b[i] + bdmm(tm[i], expand(b[i])) for i in n]
            tm = [tm[i] - (y[i] + bdmm(y[i], expand(tm[i]))) for i in n]
        eg = [jnp.exp(gcb[i]) for i in n]
        bv = [bcb[i] * v[i] for i in n]
        bk = [bcb[i] * k[i] * eg[i] for i in n]
        tuw = [bdmm2(tm[i], bv[i], bk[i]) for i in n]
        u = [bv[i] + tuw[i][0] for i in n]
        w = [bk[i] + tuw[i][1] for i in n]
        gl = [gcb[i][CHUNK - 1:CHUNK, :] if dd[i] == 0 else gcb[i][0:1, :] for i in n]
        kd = [k[i] * jnp.exp(gl[i] - gcb[i]) for i in n]
        qkm = [kq[i][CHUNK:] * dm[i] for i in n]
        mnc = [tn_diag2(kd[i], w[i], u[i]) for i in n]
        mc = [mnc[i][0] for i in n]
        nc = [mnc[i][1] for i in n]
        qwu = [bdmm2(qkm[i], w[i], u[i]) for i in n]
        qp = [q[i] * eg[i] - qwu[i][0] for i in n]
        op = [qwu[i][1] for i in n]
        for i in n:
            d, c = chains[i]
            mc_s[d, rows[i], :] = mc[i].astype(BF16)
            nc_s[d, rows[i], :] = nc[i]
            qp_s[d, rows[i], :] = qp[i].astype(BF16)
            op_s[d, rows[i], :] = op[i]
            egl_s[d, pl.ds(_aligned(c * 8, 8), 8), :] = jnp.broadcast_to(jnp.exp(gl[i]), (8, BRANCH_W))

    group = min(PREP_CHUNKS, n_chunks)
    n_groups = n_chunks // group

    def prepare_group(j):
        prepare([(d, (j if d == 0 else n_groups - 1 - j) * group + c) for c in range(group) for d in range(2)])

    def scan_step(i):
        for d, c in ((0, i), (1, n_chunks - 1 - i)):
            rows = pl.ds(_aligned(c * CHUNK, CHUNK), CHUNK)
            s = s_s[d]
            sbd = expand(s)
            oacc_ref[d, rows, :] = bdmm(qp_s[d, rows, :], sbd) + op_s[d, rows, :]
            egl = egl_s[d, pl.ds(_aligned(c * 8, 8), 8), :][0:1, :]
            s_s[d] = s * egl - bdmm(mc_s[d, rows, :], sbd) + nc_s[d, rows, :]

    def scan_group(j):
        for i in range(group):
            scan_step(j * group + i)

    prepare_group(0)
    if n_groups > 1:
        def body(j, carry):
            scan_group(j - 1)
            prepare_group(j)
            return carry
        lax.fori_loop(1, n_groups, body, 0)
    scan_group(n_groups - 1)

    if st_ref is not None:
        _write_state(st_ref, layer, emit == "first", lambda d, h: s_s[d][:, _hs(h)])
    o = oacc_ref[0] + oacc_ref[1]
    ms = _head_reduce(o * o, _head_block_matrix(BRANCH_W, 1.0 / HEAD_DIM))
    o_ref[...] = (o * lax.rsqrt(ms + EPS) * g_ref[...] * _silu(z_ref[...].astype(F32))).astype(BF16)


def _state_spec(layer, n_seq=None):
    return pl.BlockSpec((n_seq, None, 2, N_HEADS, HEAD_DIM, HEAD_DIM), lambda b: (b, layer, 0, 0, 0, 0))


def _state_output(emit, prev, layer, batch, n_args, n_seq=None):
    if emit == "none":
        return [], [], [], [], {}
    shape = jax.ShapeDtypeStruct((batch, DEPTH, 2, N_HEADS, HEAD_DIM, HEAD_DIM), F32)
    if emit == "first":
        spec = pl.BlockSpec((n_seq, DEPTH, 2, N_HEADS, HEAD_DIM, HEAD_DIM), lambda b: (b, 0, 0, 0, 0, 0))
        return [], [], [spec], [shape], {}
    return [pl.BlockSpec(memory_space=pl.ANY)], [prev], [_state_spec(layer, n_seq)], [shape], {n_args: 1}


def _gdn_call(proj, gab, cw8, par, norm_g, layer, batch, seq, s0=None, emit="none", prev=None):
    has_s0 = s0 is not None
    qkv_w = 3 * BRANCH_W
    in_specs = [pl.BlockSpec((seq, qkv_w), lambda b: (b, OFF_GQKV // qkv_w)),
                pl.BlockSpec((seq, BRANCH_W), lambda b: (b, OFF_GZ // BRANCH_W)),
                pl.BlockSpec((seq, 128), lambda b: (b, 0)),
                _layer_spec((8, qkv_w), layer, 1),
                _layer_spec((2, 128), layer, 1),
                _layer_spec((1, BRANCH_W), layer, 1)]
    args = [proj, proj, gab, cw8, par, norm_g]
    if has_s0:
        in_specs.append(_state_spec(layer))
        args.append(s0)
    st_in_specs, st_args, st_out_specs, st_shapes, aliases = _state_output(emit, prev, layer, batch, len(args))
    return pl.pallas_call(
        functools.partial(_gdn_kernel, seq=seq, has_s0=has_s0, layer=layer, emit=emit),
        grid=(batch,), in_specs=in_specs + st_in_specs,
        out_specs=[pl.BlockSpec((seq, BRANCH_W), lambda b: (b, 0))] + st_out_specs,
        out_shape=[jax.ShapeDtypeStruct((batch * seq, BRANCH_W), BF16)] + st_shapes,
        input_output_aliases=aliases,
        scratch_shapes=[pltpu.VMEM((seq, BRANCH_W), F32),
                        pltpu.VMEM((seq, BRANCH_W), F32),
                        pltpu.VMEM((seq, BRANCH_W), F32),
                        pltpu.VMEM((2, seq, BRANCH_W), F32),
                        pltpu.VMEM((2, seq, BRANCH_W), F32),
                        pltpu.VMEM((2, seq // CHUNK * 8, BRANCH_W), F32),
                        pltpu.VMEM((2, seq, BRANCH_W), BF16),
                        pltpu.VMEM((2, seq, BRANCH_W), F32),
                        pltpu.VMEM((2, seq, BRANCH_W), BF16),
                        pltpu.VMEM((2, seq, BRANCH_W), F32),
                        pltpu.VMEM((2, seq // CHUNK * 8, BRANCH_W), F32),
                        pltpu.VMEM((2, HEAD_DIM, BRANCH_W), F32),
                        pltpu.VMEM((2, seq, BRANCH_W), F32)],
        compiler_params=_params(48, 1),
        name="gdn",
    )(*args, *st_args)


_RET_LOG_GAMMA = [[float(np.log1p(-np.exp2(-(base + h)))) for h in range(N_HEADS)] for base in RET_DECAY_BASE]


def _ret_kernel(*refs, seq, n_seq, has_s0, layer, emit):
    qkv_refs, z_ref, g_ref = refs[:3], refs[3], refs[4]
    s0_ref = refs[5] if has_s0 else None
    o_ref, st_ref = (refs[-1], None) if emit == "none" else (refs[-2], refs[-1])
    tile = RET_TILE
    n_tiles = seq // tile
    problems = [(s, h) for s in range(n_seq) for h in range(N_HEADS)]
    heads = range(len(problems))
    lgf = [_RET_LOG_GAMMA[0][h] for _, h in problems]
    lgb = [_RET_LOG_GAMMA[1][h] for _, h in problems]
    a = lax.broadcasted_iota(jnp.int32, (tile, 1), 0).astype(F32)
    ef = [jnp.exp(a * lgf[h]) for h in heads]
    eif = [jnp.exp(-a * lgf[h]) for h in heads]
    eb = [jnp.exp(a * lgb[h]) for h in heads]
    eib = [jnp.exp(-a * lgb[h]) for h in heads]
    gf_tile = [float(np.exp(tile * lgf[h])) for h in heads]
    gb_tile = [float(np.exp(tile * lgb[h])) for h in heads]
    ii = lax.broadcasted_iota(jnp.int32, (tile, tile), 0)
    jj = lax.broadcasted_iota(jnp.int32, (tile, tile), 1)

    def head_cols(t, part, i):
        s, h = problems[i]
        return qkv_refs[part][s * seq + t * tile:s * seq + (t + 1) * tile, _hs(h)]

    def initial_state(d, i):
        s, h = problems[i]
        return s0_ref[s, d, h]

    kf = [[head_cols(t, 1, h) * eif[h] for h in heads] for t in range(n_tiles)]
    kb = [[head_cols(t, 1, h) * eb[h] for h in heads] for t in range(n_tiles)]
    vs = [[head_cols(t, 2, h) for h in heads] for t in range(n_tiles)]
    use_states = has_s0 or n_tiles > 1 or st_ref is not None
    if use_states:
        kvf = [[_mm_tn(kf[t][h], vs[t][h]) for h in heads] for t in range(n_tiles)]
        kvb = [[_mm_tn(kb[t][h], vs[t][h]) for h in heads] for t in range(n_tiles)]
        zero = jnp.zeros((HEAD_DIM, HEAD_DIM), F32)
        zf = [[(float(np.exp(lgf[h])) * initial_state(0, h)) if has_s0 else zero for h in heads]]
        for t in range(n_tiles):
            zf.append([gf_tile[h] * (zf[t][h] + kvf[t][h]) for h in heads])
        acc = [initial_state(1, h) if has_s0 else zero for h in heads]
        zb = [None] * n_tiles
        for t in reversed(range(n_tiles)):
            zb[t] = [gb_tile[h] * acc[h] for h in heads]
            acc = [zb[t][h] + kvb[t][h] for h in heads]
        if st_ref is not None:
            stf = [zf[n_tiles][h] * float(np.exp(-lgf[h])) for h in heads]
            for s in range(n_seq):
                _write_state(st_ref.at[s], layer, emit == "first",
                             lambda d, h, s=s: (stf, acc)[d][s * N_HEADS + h])
    tiles = [[] for _ in range(n_seq)]
    for t in range(n_tiles):
        q = [head_cols(t, 0, h) * SCALE for h in heads]
        qf = [q[h] * ef[h] for h in heads]
        qb = [q[h] * eib[h] for h in heads]
        sd = [jnp.where(ii >= jj, _mm_nt(qf[h], kf[t][h]), 0.0) + jnp.where(ii <= jj, _mm_nt(qb[h], kb[t][h]), 0.0)
              for h in heads]
        o = [_mm(sd[h], vs[t][h]) for h in heads]
        if has_s0 or n_tiles > 1:
            o = [o[h] + _mm(qf[h], zf[t][h]) + _mm(qb[h], zb[t][h]) for h in heads]
        ms = [jnp.mean(o[h] * o[h], axis=-1, keepdims=True) for h in heads]
        for s in range(n_seq):
            tiles[s].append(jnp.concatenate([o[h] * lax.rsqrt(ms[h] + EPS) * g_ref[...]
                                             for h in range(s * N_HEADS, (s + 1) * N_HEADS)], axis=-1))
    o = jnp.concatenate([tile_out for s in range(n_seq) for tile_out in tiles[s]], axis=0)
    o_ref[...] = (o * _silu(z_ref[...].astype(F32))).astype(BF16)


def _ret_call(proj, norm_g, layer, batch, seq, s0=None, emit="none", prev=None):
    has_s0 = s0 is not None
    n_seq = max(1, RET_ROWS_PER_STEP // seq)
    assert batch % n_seq == 0
    tm = n_seq * seq
    cq = OFF_C // BRANCH_W
    in_specs = [pl.BlockSpec((tm, BRANCH_W), lambda b, part=part: (b, cq + part)) for part in range(4)]
    in_specs.append(_layer_spec((1, HEAD_DIM), layer, 1))
    args = [proj, proj, proj, proj, norm_g]
    if has_s0:
        in_specs.append(_state_spec(layer, n_seq))
        args.append(s0)
    st_in_specs, st_args, st_out_specs, st_shapes, aliases = _state_output(emit, prev, layer, batch, len(args),
                                                                           n_seq)
    return pl.pallas_call(
        functools.partial(_ret_kernel, seq=seq, n_seq=n_seq, has_s0=has_s0, layer=layer, emit=emit),
        grid=(batch // n_seq,), in_specs=in_specs + st_in_specs,
        out_specs=[pl.BlockSpec((tm, BRANCH_W), lambda b: (b, 0))] + st_out_specs,
        out_shape=[jax.ShapeDtypeStruct((batch * seq, BRANCH_W), BF16)] + st_shapes,
        input_output_aliases=aliases,
        compiler_params=_params(48, 1),
        name="retention",
    )(*args, *st_args)


def _out_kernel(*refs, final):
    if final:
        (h_ref, mod_ref, g_ref, oa_ref, ob_ref, oc_ref, od_ref, wg_ref, wb_ref, wo_ref, fn_ref,
         o_ref, y_ref) = refs
    else:
        h_ref, mod_ref, g_ref, oa_ref, ob_ref, oc_ref, od_ref, wg_ref, wb_ref, wo_ref, o_ref = refs
    x = h_ref[...]
    mod = mod_ref[0]
    hn = _modulated_norm(x, mod, g_ref[...]).astype(BF16)
    merged = None
    for n, br_ref in enumerate((oa_ref, ob_ref, oc_ref, od_ref)):
        gate = jax.nn.sigmoid(_mm_nt(hn, wg_ref[n * D_MODEL:(n + 1) * D_MODEL, :]))
        up = jnp.dot(br_ref[...], wb_ref[n], preferred_element_type=F32)
        merged = gate * up if merged is None else merged + gate * up
    out = jnp.dot(merged.astype(BF16), wo_ref[...], preferred_element_type=F32)
    hnew = x + mod[:, 2 * D_MODEL:] * out
    o_ref[...] = hnew
    if final:
        ms = jnp.mean(hnew * hnew, axis=-1, keepdims=True)
        y_ref[...] = hnew * lax.rsqrt(ms + EPS) * fn_ref[...]


def _out_call(h2d, mod3, norm_g3, branches, wg, wb, wo, layer, rows_per_mod, final_norm=None):
    t = h2d.shape[0]
    tm = 512
    final = final_norm is not None
    if mod3.shape[0] == 1:
        mod_idx = lambda i: (0, 0, 0)
    else:
        mod_idx = lambda i: ((i * tm) // rows_per_mod, 0, 0)
    once = pl.Buffered(1)
    in_specs = [pl.BlockSpec((tm, D_MODEL), lambda i: (i, 0)),
                pl.BlockSpec((1, 1, 3 * D_MODEL), mod_idx),
                _layer_spec((1, D_MODEL), layer, 1)]
    in_specs += [pl.BlockSpec((tm, BRANCH_W), lambda i: (i, 0))] * N_BRANCH
    in_specs += [pl.BlockSpec((None, N_BRANCH * D_MODEL, D_MODEL), lambda i: (layer, 0, 0), pipeline_mode=once),
                 pl.BlockSpec((None, N_BRANCH, BRANCH_W, D_MODEL), lambda i: (layer, 0, 0, 0), pipeline_mode=once),
                 pl.BlockSpec((None, D_MODEL, D_MODEL), lambda i: (layer, 0, 0), pipeline_mode=once)]
    args = [h2d, mod3, norm_g3, *branches, wg, wb, wo]
    out_specs = [pl.BlockSpec((tm, D_MODEL), lambda i: (i, 0))]
    out_shape = [jax.ShapeDtypeStruct((t, D_MODEL), F32)]
    if final:
        in_specs.append(pl.BlockSpec((1, D_MODEL), lambda i: (0, 0)))
        args.append(final_norm.reshape(1, D_MODEL))
        out_specs.append(pl.BlockSpec((tm, D_MODEL), lambda i: (i, 0)))
        out_shape.append(jax.ShapeDtypeStruct((t, D_MODEL), F32))
    return pl.pallas_call(
        functools.partial(_out_kernel, final=final),
        grid=(t // tm,), in_specs=in_specs, out_specs=out_specs, out_shape=out_shape,
        compiler_params=_params(48, 1),
        name="merge_out_final" if final else "merge_out",
    )(*args)


def _prep_weights(w_in):
    offs = np.concatenate([[0], np.cumsum(IN_SPLITS)])
    seg = lambda i, j: w_in[:, :, offs[i]:offs[j]].astype(BF16).transpose(0, 2, 1)
    assert offs[5] == OFF_GZ and offs[15] - offs[6] == OFF_GAB - OFF_GZ
    pad = jnp.zeros((DEPTH, PROJ_W - OFF_GAB - IN_SPLITS[5], D_MODEL), BF16)
    w_proj = (seg(0, 5),
              seg(6, 15),
              jnp.concatenate([seg(5, 6), pad], axis=1))
    return w_proj, seg(15, 16)


def _rope_tables(seq):
    t = jnp.arange(seq)
    quarter = HEAD_DIM // 4
    inv = ROPE_THETA ** (-jnp.arange(quarter, dtype=F32) / quarter)

    def half(pos):
        ang = pos.astype(F32)[:, None] * inv
        c, s, zero = jnp.cos(ang), jnp.sin(ang), jnp.zeros_like(ang)
        return jnp.concatenate([c, c], -1), jnp.concatenate([-s, zero], -1), jnp.concatenate([zero, s], -1)

    parts = [jnp.concatenate([a, b], -1) for a, b in zip(half(t // GRID_W), half(t % GRID_W))]
    tab = jnp.stack(parts)
    return jnp.tile(tab, (1, 1, N_HEADS)), jnp.tile(tab, (1, 1, KV_HEADS))


def _layer(h2d, batch, seq, mod, pw, layer, ctx, caches, final_norm):
    proj, gab = _inproj_call(h2d, mod, pw["norm_g"], pw["w_proj"], layer, seq)
    kw = KV_HEADS * HEAD_DIM
    if ctx is None:
        emit = "first" if caches is None else "update"
        akv, nkv, sg_all, sr_all = caches or (None, None, None, None)
        oa, *akv = _attn_ctx_call(proj, akv, layer, batch, seq, OFF_AQKV, OFF_AQKV + BRANCH_W,
                                  OFF_AQKV + BRANCH_W + kw, OFF_AZ, KV_HEADS, pw["qn"], pw["kn"])
        od, *nkv = _attn_ctx_call(proj, nkv, layer, batch, seq, OFF_D, OFF_D + BRANCH_W,
                                  OFF_D + 2 * BRANCH_W, OFF_D + 3 * BRANCH_W, N_HEADS)
        ob, sg_all = _gdn_call(proj, gab, pw["cw8"], pw["gdn_par"], pw["gdn_norm"], layer, batch, seq,
                               emit=emit, prev=sg_all)
        oc, sr_all = _ret_call(proj, pw["ret_norm"], layer, batch, seq, emit=emit, prev=sr_all)
        caches = (akv, nkv, sg_all, sr_all)
    else:
        oa = _attn_lat_call(proj, ctx["akt"], ctx["avt"], layer, batch, seq, ctx["qtab"], ctx["ktab"],
                            pw["qn"], pw["kn"])
        od = _na_call(proj, ctx["nkt"], ctx["nvt"], ctx["tb"], layer, batch, seq)
        ob, = _gdn_call(proj, gab, pw["cw8"], pw["gdn_par"], pw["gdn_norm"], layer, batch, seq, s0=ctx["sg"])
        oc, = _ret_call(proj, pw["ret_norm"], layer, batch, seq, s0=ctx["sr"])
    outs = _out_call(h2d, mod, pw["norm_g"], (oa, ob, oc, od), pw["wg"], pw["wb"], pw["wo"], layer, seq, final_norm)
    return outs, caches


def kernel(x_prompt, x_sample, cache_attn_k, cache_attn_v, cache_na_k, cache_na_v, state_gdn, state_ret, c, c_ctx, w_ada, b_ada, norm_g, w_in, conv_w, gdn_a_log, gdn_dt_bias, gdn_norm, attn_q_norm, attn_k_norm, ret_norm, na_bias, w_branch, w_out, final_norm):
    batch, seq, _ = x_prompt.shape
    dbatch, dseq, _ = x_sample.shape
    assert dbatch == 8, "the modulation kernel handles exactly one sublane tile of conditioning rows"

    w_proj, wg = _prep_weights(w_in)
    par = jnp.zeros((DEPTH, 2, 128), F32)
    par = par.at[:, 0, 8:16].set(gdn_dt_bias.reshape(DEPTH, 8)).at[:, 1, 8:16].set(gdn_a_log.reshape(DEPTH, 8))
    pw = dict(
        w_ada=w_ada, b_ada=b_ada.reshape(DEPTH, 1, 3 * D_MODEL), norm_g=norm_g.reshape(DEPTH, 1, D_MODEL),
        w_proj=w_proj, wg=wg, wb=w_branch.astype(BF16), wo=w_out.astype(BF16),
        cw8=jnp.concatenate([conv_w, jnp.zeros((DEPTH, 8 - SHORT_CONV, 3 * BRANCH_W), F32)], axis=1),
        gdn_par=par,
        gdn_norm=jnp.tile(gdn_norm, (1, N_HEADS)).reshape(DEPTH, 1, BRANCH_W),
        ret_norm=ret_norm.reshape(DEPTH, 1, HEAD_DIM),
        qn=jnp.tile(attn_q_norm, (1, N_HEADS)).reshape(DEPTH, 1, BRANCH_W),
        kn=jnp.tile(attn_k_norm, (1, KV_HEADS)).reshape(DEPTH, 1, KV_HEADS * HEAD_DIM))

    cond = jnp.concatenate([jnp.broadcast_to(c_ctx, (8, D_MODEL)), c], axis=0)
    mods = _mod_call(cond, w_ada, pw["b_ada"])

    h = x_prompt.reshape(batch * seq, D_MODEL)
    caches = None
    for l in range(DEPTH):
        outs, caches = _layer(h, batch, seq, mods[l, 0:1].reshape(1, 1, 3 * D_MODEL), pw, l, None, caches,
                              final_norm if l == DEPTH - 1 else None)
        h = outs[0]
    y_prompt = outs[1].reshape(batch, seq, D_MODEL)
    token_major = lambda a: a.transpose(0, 1, 4, 2, 3)
    (akt, avt), (nkt, nvt), new_state_gdn, new_state_ret = caches
    new_attn_k, new_attn_v, new_na_k, new_na_v = (token_major(a) for a in (akt, avt, nkt, nvt))

    qtab, ktab = _rope_tables(dseq)
    feature_major = lambda a: a.transpose(0, 1, 3, 4, 2)
    ctx = dict(akt=feature_major(cache_attn_k), avt=feature_major(cache_attn_v),
               nkt=feature_major(cache_na_k), nvt=feature_major(cache_na_v),
               sg=state_gdn, sr=state_ret, tb=_na_bias_call(na_bias), qtab=qtab, ktab=ktab)
    h = x_sample.reshape(dbatch * dseq, D_MODEL)
    for l in range(DEPTH):
        outs, _ = _layer(h, dbatch, dseq, mods[l, 8:16].reshape(dbatch, 1, 3 * D_MODEL), pw, l, ctx, None,
                         final_norm if l == DEPTH - 1 else None)
        h = outs[0]
    y_sample = outs[1].reshape(dbatch, dseq, D_MODEL)
    return (y_prompt, y_sample, new_attn_k, new_attn_v, new_na_k, new_na_v, new_state_gdn, new_state_ret)
```

```python
import functools

import numpy as np
import jax
import jax.numpy as jnp
from jax import lax
from jax.experimental import pallas as pl
from jax.experimental.pallas import tpu as pltpu

F32 = jnp.float32
BF16 = jnp.bfloat16

D_MODEL = 1024
HEAD_DIM = 64
HEAD_SHIFT = HEAD_DIM.bit_length() - 1
N_HEADS = 4
KV_HEADS = N_HEADS // 2
BRANCH_W = N_HEADS * HEAD_DIM
N_BRANCH = 4
DEPTH = 2
GRID_W = 64
CHUNK = 64
PREP_CHUNKS = 4
GDN_TILE_HEADS = 2
assert CHUNK == HEAD_DIM
SHORT_CONV = 5
NA_ROWS = 8
NA_COLS = 16
RET_ROWS_PER_STEP = 1024
CTX_SEQS_PER_STEP = 4
NA_ROWS_PER_STEP = 8
N_DR = 2 * NA_ROWS - 1
N_DC = 2 * NA_COLS - 1
ROPE_THETA = 10000.0
RET_DECAY_BASE = (5.0, 5.5)
RET_TILE = 128
EPS = 1e-6
SCALE = HEAD_DIM ** -0.5
LOG2E = 1.4426950408889634
NEG_INF = float("-inf")

IN_SPLITS = (256, 128, 128, 256, 768, 16, 256, 256, 256, 256, 256, 256, 256, 256, 256, 4096)
PROJ_W = 4096
OFF_AQKV = 0
OFF_AZ = 512
OFF_GQKV = 768
OFF_GZ = 1536
OFF_C = 1792
OFF_D = 2816
OFF_GAB = 3840

V7X_VMEM_BYTES = 64 * 1024 * 1024
MIB = 1024 * 1024


def _params(vmem_mib, n_axes):
    assert vmem_mib * MIB < V7X_VMEM_BYTES
    return pltpu.CompilerParams(dimension_semantics=("arbitrary",) * n_axes,
                                vmem_limit_bytes=vmem_mib * MIB)


def _layer_spec(block, layer, n_grid):
    zeros = (0,) * len(block)
    if n_grid == 1:
        return pl.BlockSpec((None,) + block, lambda i: (layer,) + zeros)
    return pl.BlockSpec((None,) + block, lambda i, j: (layer,) + zeros)


def _mm(a, b):
    return jnp.dot(a.astype(BF16), b.astype(BF16), preferred_element_type=F32)


def _mm_nt(a, b):
    return lax.dot_general(a.astype(BF16), b.astype(BF16), (((1,), (1,)), ((), ())),
                           preferred_element_type=F32)


def _mm_tn(a, b):
    return lax.dot_general(a.astype(BF16), b.astype(BF16), (((0,), (0,)), ((), ())),
                           preferred_element_type=F32)


def _split3(x):
    hi = x.astype(BF16)
    r = x - hi.astype(F32)
    mid = r.astype(BF16)
    lo = (r - mid.astype(F32)).astype(BF16)
    return hi, mid, lo


def _mm_exact(sel, x, terms=3):
    return sum(jnp.dot(sel, part, preferred_element_type=F32) for part in _split3(x)[:terms])


def _mm_exact_lhs(x, sel, terms=3):
    return sum(jnp.dot(part, sel, preferred_element_type=F32) for part in _split3(x)[:terms])


def _silu(x):
    return x * jax.nn.sigmoid(x)


def _head_block_matrix(width, value):
    ri = lax.broadcasted_iota(jnp.int32, (width, width), 0) >> HEAD_SHIFT
    ci = lax.broadcasted_iota(jnp.int32, (width, width), 1) >> HEAD_SHIFT
    return jnp.where(ri == ci, value, 0.0).astype(BF16)


def _head_reduce(x, g, terms=2):
    return sum(jnp.dot(part, g, preferred_element_type=F32) for part in _split3(x)[:terms])


def _head_rms(x):
    ms = _head_reduce(x * x, _head_block_matrix(x.shape[1], 1.0 / HEAD_DIM))
    return x * lax.rsqrt(ms + EPS)


def _rope(x, tab_ref):
    w = x.shape[1]
    return (x * tab_ref[0] + pltpu.roll(x, w - 16, 1) * tab_ref[1] + pltpu.roll(x, 16, 1) * tab_ref[2])


def _attend(qs, parts):
    groups = range(len(qs))
    qs = [(q.astype(F32) * (SCALE * LOG2E)).astype(BF16) for q in qs]

    def score(q, part):
        k, _, bias, feature_major = part
        s = _mm(q, k) if feature_major else _mm_nt(q, k)
        return s if bias is None else s + bias

    scores = [[score(qs[g], part) for part in parts[g]] for g in groups]
    m = [functools.reduce(jnp.maximum, [s.max(axis=-1, keepdims=True) for s in scores[g]]) for g in groups]
    p = [[jnp.exp2(s - m[g]) for s in scores[g]] for g in groups]
    den = [sum(x.sum(axis=-1, keepdims=True) for x in p[g]) for g in groups]
    out = [sum(_mm_nt(x, part[1]) if part[3] else _mm(x, part[1]) for x, part in zip(p[g], parts[g]))
           for g in groups]
    return [out[g] / den[g] for g in groups]


def _attend_head_pairs(qs, ks, vs):
    s_len, width = qs[0].shape
    tile_w = 2 * HEAD_DIM
    tiles = [slice(t * tile_w, (t + 1) * tile_w) for t in range(width // tile_w)]
    second = (lax.broadcasted_iota(jnp.int32, (s_len, tile_w), 1) >> HEAD_SHIFT) == 1
    zero = jnp.zeros((s_len, tile_w), BF16)

    def expand(x):
        xb = x.astype(BF16)
        return jnp.concatenate([jnp.where(second, zero, xb), jnp.where(second, xb, zero)], axis=0)

    groups = [(i, t) for i in range(len(qs)) for t in tiles]
    q = [(qs[i][:, t].astype(F32) * (SCALE * LOG2E)).astype(BF16) for i, t in groups]
    s = [_mm_nt(q[g], expand(ks[i][:, t])) for g, (i, t) in enumerate(groups)]
    m = [[x[:, h * s_len:(h + 1) * s_len].max(axis=-1, keepdims=True) for h in range(2)] for x in s]
    p = [[jnp.exp2(s[g][:, h * s_len:(h + 1) * s_len] - m[g][h]) for h in range(2)] for g in range(len(groups))]
    den = [[x.sum(axis=-1, keepdims=True) for x in pg] for pg in p]
    o = [_mm(jnp.concatenate(p[g], axis=1), expand(vs[i][:, t])) for g, (i, t) in enumerate(groups)]
    o = [o[g] / jnp.where(second, den[g][1], den[g][0]) for g in range(len(groups))]
    n_t = len(tiles)
    return [jnp.concatenate(o[i * n_t:(i + 1) * n_t], axis=1) for i in range(len(qs))]


def _hs(h):
    return slice(h * HEAD_DIM, (h + 1) * HEAD_DIM)


def _aligned(x, m):
    return x if isinstance(x, int) else pl.multiple_of(x, m)


def _mod_kernel(c_ref, w_ref, b_ref, o_ref):
    o_ref[...] = _mm(_silu(c_ref[...]), w_ref[...]) + b_ref[...]


def _mod_call(cond, w_ada, b_ada3):
    tn = 512
    rows = cond.shape[0]
    return pl.pallas_call(
        _mod_kernel,
        grid=(DEPTH, 3 * D_MODEL // tn),
        in_specs=[pl.BlockSpec((rows, D_MODEL), lambda l, j: (0, 0)),
                  pl.BlockSpec((None, D_MODEL, tn), lambda l, j: (l, 0, j)),
                  pl.BlockSpec((None, 1, tn), lambda l, j: (l, 0, j))],
        out_specs=pl.BlockSpec((None, rows, tn), lambda l, j: (l, 0, j)),
        out_shape=jax.ShapeDtypeStruct((DEPTH, rows, 3 * D_MODEL), F32),
        compiler_params=_params(24, 2),
        name="adaln_mod",
    )(cond, w_ada, b_ada3)


def _modulated_norm(x, mod, g):
    ms = jnp.mean(x * x, axis=-1, keepdims=True)
    y = x * lax.rsqrt(ms + EPS) * g
    return y * (1.0 + mod[:, D_MODEL:2 * D_MODEL]) + mod[:, :D_MODEL]


def _inproj_kernel(x_ref, mod_ref, g_ref, *refs):
    *w_refs, o_ref, ab_ref = refs
    hn = _modulated_norm(x_ref[...], mod_ref[0], g_ref[...]).astype(BF16)
    tn = 256
    col = 0
    for w_ref in w_refs:
        for j in range(w_ref.shape[0] // tn):
            y = _mm_nt(hn, w_ref[j * tn:(j + 1) * tn, :])
            o_ref[:, col:col + tn] = y.astype(BF16)
            if col == OFF_GAB:
                ab_ref[...] = y[:, :128]
            col += tn
    assert col == PROJ_W


def _inproj_call(x2d, mod3, norm_g3, weights, layer, rows_per_mod):
    t = x2d.shape[0]
    tm = 1024
    if mod3.shape[0] == 1:
        mod_idx = lambda i: (0, 0, 0)
    else:
        mod_idx = lambda i: ((i * tm) // rows_per_mod, 0, 0)
    w_specs = [pl.BlockSpec((None, w.shape[1], D_MODEL), lambda i: (layer, 0, 0), pipeline_mode=pl.Buffered(1))
               for w in weights]
    return pl.pallas_call(
        _inproj_kernel,
        grid=(t // tm,),
        in_specs=[pl.BlockSpec((tm, D_MODEL), lambda i: (i, 0)),
                  pl.BlockSpec((1, 1, 3 * D_MODEL), mod_idx),
                  _layer_spec((1, D_MODEL), layer, 1)] + w_specs,
        out_specs=[pl.BlockSpec((tm, PROJ_W), lambda i: (i, 0)), pl.BlockSpec((tm, 128), lambda i: (i, 0))],
        out_shape=[jax.ShapeDtypeStruct((t, PROJ_W), BF16), jax.ShapeDtypeStruct((t, 128), F32)],
        compiler_params=_params(40, 1),
        name="inproj",
    )(x2d, mod3, norm_g3, *weights)


def _stacked_heads(q, n_kv):
    rep = N_HEADS // n_kv
    return [jnp.concatenate([q[:, _hs(g * rep + r)] for r in range(rep)], axis=0) for g in range(n_kv)]


def _unstack_heads(outs, n_kv):
    rep = N_HEADS // n_kv
    m = outs[0].shape[0] // rep
    return jnp.concatenate([outs[g][r * m:(r + 1) * m] for g in range(n_kv) for r in range(rep)], axis=-1)


def _write_layer(ref, layer, value, stacked):
    if not stacked:
        ref[...] = value
        return
    for l in range(ref.shape[0]):
        ref[l] = value if l == layer else jnp.zeros(value.shape, value.dtype)


def _write_state(st_ref, layer, stacked, piece):
    for d in range(2):
        for h in range(N_HEADS):
            value = piece(d, h)
            if stacked:
                for l in range(st_ref.shape[0]):
                    st_ref[l, d, h] = value if l == layer else jnp.zeros(value.shape, value.dtype)
            else:
                st_ref[d, h] = value


def _attn_ctx_kernel(*refs, n_kv, norm, layer, first):
    if norm:
        q_ref, k_ref, v_ref, z_ref, qn_ref, kn_ref = refs[:6]
    else:
        q_ref, k_ref, v_ref, z_ref = refs[:4]
    o_ref, kt_ref, vt_ref = refs[-3:]
    q, k, v, z = q_ref[...], k_ref[...].astype(F32), v_ref[...].astype(F32), z_ref[...].astype(F32)
    if norm:
        q = _head_rms(q.astype(F32)) * qn_ref[...]
        k = _head_rms(k) * kn_ref[...]
    n_seq = kt_ref.shape[0]
    seq = k.shape[0] // n_seq
    qs, parts = [], []
    for s in range(n_seq):
        rows = slice(s * seq, (s + 1) * seq)
        _write_layer(kt_ref.at[s], layer, k[rows].T.reshape(n_kv, HEAD_DIM, seq), first)
        _write_layer(vt_ref.at[s], layer, v[rows].T.reshape(n_kv, HEAD_DIM, seq), first)
        qs += _stacked_heads(q[rows], n_kv)
        parts += [[(k[rows, _hs(g)], v[rows, _hs(g)], None, False)] for g in range(n_kv)]
    if n_kv == N_HEADS:
        seqs = [slice(s * seq, (s + 1) * seq) for s in range(n_seq)]
        o = jnp.concatenate(_attend_head_pairs([q[r] for r in seqs], [k[r] for r in seqs], [v[r] for r in seqs]),
                            axis=0)
    else:
        outs = _attend(qs, parts)
        o = jnp.concatenate([_unstack_heads(outs[s * n_kv:(s + 1) * n_kv], n_kv) for s in range(n_seq)], axis=0)
    o_ref[...] = (o * _silu(z)).astype(BF16)


def _attn_ctx_call(proj, prev, layer, batch, seq, off_q, off_k, off_v, off_z, n_kv, qn=None, kn=None):
    t = batch * seq
    kvw = n_kv * HEAD_DIM
    norm = qn is not None
    first = prev is None
    n_seq = CTX_SEQS_PER_STEP
    assert batch % n_seq == 0
    tm = n_seq * seq
    in_specs = [pl.BlockSpec((tm, BRANCH_W), lambda b: (b, off_q // BRANCH_W)),
                pl.BlockSpec((tm, kvw), lambda b: (b, off_k // kvw)),
                pl.BlockSpec((tm, kvw), lambda b: (b, off_v // kvw)),
                pl.BlockSpec((tm, BRANCH_W), lambda b: (b, off_z // BRANCH_W))]
    args = [proj, proj, proj, proj]
    if norm:
        in_specs += [_layer_spec((1, BRANCH_W), layer, 1), _layer_spec((1, kvw), layer, 1)]
        args += [qn, kn]
    aliases = {}
    if first:
        cache_spec = pl.BlockSpec((n_seq, DEPTH, n_kv, HEAD_DIM, seq), lambda b: (b, 0, 0, 0, 0))
    else:
        aliases = {len(args): 1, len(args) + 1: 2}
        in_specs += [pl.BlockSpec(memory_space=pl.ANY)] * 2
        args += list(prev)
        cache_spec = pl.BlockSpec((n_seq, None, n_kv, HEAD_DIM, seq), lambda b: (b, layer, 0, 0, 0))
    cache_shape = jax.ShapeDtypeStruct((batch, DEPTH, n_kv, HEAD_DIM, seq), F32)
    return pl.pallas_call(
        functools.partial(_attn_ctx_kernel, n_kv=n_kv, norm=norm, layer=layer, first=first),
        grid=(batch // n_seq,), in_specs=in_specs,
        out_specs=[pl.BlockSpec((tm, BRANCH_W), lambda b: (b, 0)), cache_spec, cache_spec],
        out_shape=[jax.ShapeDtypeStruct((t, BRANCH_W), BF16), cache_shape, cache_shape],
        input_output_aliases=aliases,
        compiler_params=_params(32, 1),
        name="attn_ctx_norm" if norm else "attn_ctx",
    )(*args)


def _attn_lat_kernel(q_ref, kv_ref, z_ref, ckt_ref, cvt_ref, qtab_ref, ktab_ref, qn_ref, kn_ref, o_ref,
                     k_s, v_s):
    kw = KV_HEADS * HEAD_DIM

    @pl.when(pl.program_id(1) == 0)
    def _():
        kv = kv_ref[...]
        k_s[...] = _rope(_head_rms(kv[:, :kw].astype(F32)) * kn_ref[...], ktab_ref).astype(BF16)
        v_s[...] = kv[:, kw:]

    q = _rope(_head_rms(q_ref[...].astype(F32)) * qn_ref[...], qtab_ref)
    k, v = k_s[...], v_s[...]
    outs = _attend(_stacked_heads(q, KV_HEADS),
                   [[(k[:, _hs(g)], v[:, _hs(g)], None, False), (ckt_ref[g], cvt_ref[g], None, True)]
                    for g in range(KV_HEADS)])
    o_ref[...] = (_unstack_heads(outs, KV_HEADS) * _silu(z_ref[...].astype(F32))).astype(BF16)


def _attn_lat_call(proj, cache_kt, cache_vt, layer, batch, seq, qtab, ktab, qn, kn):
    tq = 512
    nq = seq // tq
    past = cache_kt.shape[-1]
    kw = KV_HEADS * HEAD_DIM
    ctx_spec = pl.BlockSpec((None, None, KV_HEADS, HEAD_DIM, past), lambda b, i: (b, layer, 0, 0, 0))
    return pl.pallas_call(
        _attn_lat_kernel,
        grid=(batch, nq),
        in_specs=[pl.BlockSpec((tq, BRANCH_W), lambda b, i: (b * nq + i, OFF_AQKV // BRANCH_W)),
                  pl.BlockSpec((seq, 2 * kw), lambda b, i: (b, (OFF_AQKV + BRANCH_W) // (2 * kw))),
                  pl.BlockSpec((tq, BRANCH_W), lambda b, i: (b * nq + i, OFF_AZ // BRANCH_W)),
                  ctx_spec, ctx_spec,
                  pl.BlockSpec((3, tq, BRANCH_W), lambda b, i: (0, i, 0)),
                  pl.BlockSpec((3, seq, kw), lambda b, i: (0, 0, 0)),
                  _layer_spec((1, BRANCH_W), layer, 2),
                  _layer_spec((1, kw), layer, 2)],
        out_specs=pl.BlockSpec((tq, BRANCH_W), lambda b, i: (b * nq + i, 0)),
        out_shape=jax.ShapeDtypeStruct((batch * seq, BRANCH_W), BF16),
        scratch_shapes=[pltpu.VMEM((seq, kw), BF16), pltpu.VMEM((seq, kw), BF16)],
        compiler_params=_params(40, 2),
        name="attn_lat",
    )(proj, proj, proj, cache_kt, cache_vt, qtab, ktab, qn, kn)


def _na_bias_kernel(t_ref, o_ref):
    nblk = o_ref.shape[0]
    c = lax.broadcasted_iota(jnp.int32, (GRID_W, 2 * GRID_W), 0)
    kc = lax.broadcasted_iota(jnp.int32, (GRID_W, 2 * GRID_W), 1) & (GRID_W - 1)
    cs = jnp.clip(c - NA_COLS // 2, 0, GRID_W - NA_COLS)
    valid = jnp.logical_and(kc >= cs, kc < cs + NA_COLS)

    unroll = 8
    assert nblk % unroll == 0

    def body(i, carry):
        rows8 = t_ref[pl.ds(pl.multiple_of(i * unroll, unroll), unroll), :]
        for u in range(unroll):
            row = jnp.broadcast_to(rows8[u:u + 1, :], (GRID_W, 2 * GRID_W))
            skewed = pltpu.roll(row, 2 * GRID_W - (NA_COLS - 1), 1, stride=1, stride_axis=0)
            o_ref[i * unroll + u] = jnp.where(valid, skewed * LOG2E, NEG_INF)
        return carry

    lax.fori_loop(0, nblk // unroll, body, 0)


def _na_bias_call(na_bias):
    nblk = DEPTH * N_HEADS * N_DR
    rows = jnp.pad(na_bias.reshape(nblk, N_DC), ((0, 1), (0, GRID_W - N_DC)))
    pairs = jnp.concatenate([rows[:-1], rows[1:]], axis=1)
    return pl.pallas_call(
        _na_bias_kernel,
        in_specs=[pl.BlockSpec((nblk, 2 * GRID_W), lambda: (0, 0))],
        out_specs=pl.BlockSpec((nblk, GRID_W, 2 * GRID_W), lambda: (0, 0, 0)),
        out_shape=jax.ShapeDtypeStruct((nblk, GRID_W, 2 * GRID_W), F32),
        name="na_bias",
    )(pairs)


def _na_kernel(q_ref, k_ref, v_ref, z_ref, ckt_ref, cvt_ref, tb_ref, o_ref, kh_s, vh_s, *, rows):
    win = NA_ROWS * GRID_W

    @pl.when(pl.program_id(1) == 0)
    def _():
        for h in range(N_HEADS):
            kh_s[h] = k_ref[:, _hs(h)]
            vh_s[h] = v_ref[:, _hs(h)]

    qs, parts = [], []
    for i in range(NA_ROWS_PER_STEP):
        r = pl.program_id(1) * NA_ROWS_PER_STEP + i
        rs = jnp.clip(r - NA_ROWS // 2, 0, rows - NA_ROWS)
        r0 = pl.multiple_of(rs * GRID_W, GRID_W)
        q = q_ref[i * GRID_W:(i + 1) * GRID_W, :]
        dr0 = rs - r + NA_ROWS - 1
        for h in range(N_HEADS):
            bias = jnp.concatenate([tb_ref[h * N_DR + dr0 + 2 * p] for p in range(NA_ROWS // 2)], axis=1)
            qs.append(q[:, _hs(h)])
            parts.append([(kh_s[h, pl.ds(r0, win), :], vh_s[h, pl.ds(r0, win), :], bias, False),
                          (ckt_ref[h], cvt_ref[h], None, True)])
    outs = _attend(qs, parts)
    o = jnp.concatenate([jnp.concatenate(outs[i * N_HEADS:(i + 1) * N_HEADS], axis=-1)
                         for i in range(NA_ROWS_PER_STEP)], axis=0)
    o_ref[...] = (o * _silu(z_ref[...].astype(F32))).astype(BF16)


def _na_call(proj, cache_kt, cache_vt, tb, layer, batch, seq):
    rows = seq // GRID_W
    assert rows >= NA_ROWS and rows % NA_ROWS_PER_STEP == 0
    steps = rows // NA_ROWS_PER_STEP
    tq = NA_ROWS_PER_STEP * GRID_W
    past = cache_kt.shape[-1]
    nblk = N_HEADS * N_DR
    cq = OFF_D // BRANCH_W
    ctx_spec = pl.BlockSpec((None, None, N_HEADS, HEAD_DIM, past), lambda b, r: (b, layer, 0, 0, 0))
    return pl.pallas_call(
        functools.partial(_na_kernel, rows=rows),
        grid=(batch, steps),
        in_specs=[pl.BlockSpec((tq, BRANCH_W), lambda b, r: (b * steps + r, cq)),
                  pl.BlockSpec((seq, BRANCH_W), lambda b, r: (b, cq + 1)),
                  pl.BlockSpec((seq, BRANCH_W), lambda b, r: (b, cq + 2)),
                  pl.BlockSpec((tq, BRANCH_W), lambda b, r: (b * steps + r, cq + 3)),
                  ctx_spec, ctx_spec,
                  pl.BlockSpec((nblk, GRID_W, 2 * GRID_W), lambda b, r: (layer, 0, 0))],
        out_specs=pl.BlockSpec((tq, BRANCH_W), lambda b, r: (b * steps + r, 0)),
        out_shape=jax.ShapeDtypeStruct((batch * seq, BRANCH_W), BF16),
        scratch_shapes=[pltpu.VMEM((N_HEADS, seq, HEAD_DIM), BF16), pltpu.VMEM((N_HEADS, seq, HEAD_DIM), BF16)],
        compiler_params=_params(32, 2),
        name="na_lat",
    )(proj, proj, proj, proj, cache_kt, cache_vt, tb)


def _gdn_kernel(*refs, seq, has_s0, layer, emit):
    qkv_ref, z_ref, ab_ref, cw_ref, par_ref, g_ref = refs[:6]
    s0_ref = refs[6] if has_s0 else None
    (q_s, k_s, v_s, gcb_s, bcb_s, r_s, mc_s, nc_s, qp_s, op_s, egl_s, s_s, oacc_ref) = refs[-13:]
    if emit == "none":
        o_ref, st_ref = refs[-14], None
    else:
        o_ref, st_ref = refs[-15], refs[-14]
    n_chunks = seq // CHUNK
    n_levels = CHUNK.bit_length() - 1
    qkv_w = 3 * BRANCH_W
    half = SHORT_CONV // 2
    pair_w = GDN_TILE_HEADS * HEAD_DIM
    pairs = [slice(p * pair_w, (p + 1) * pair_w) for p in range(BRANCH_W // pair_w)]
    tr = 256
    cpt = tr // CHUNK
    head_sum = _head_block_matrix(BRANCH_W, 1.0)

    gc_i = lax.broadcasted_iota(jnp.int32, (128, BRANCH_W), 0)
    gh_j = lax.broadcasted_iota(jnp.int32, (128, BRANCH_W), 1) >> HEAD_SHIFT
    sel_beta = [jnp.where(gc_i == gh_j + 4 * d, 1.0, 0.0).astype(BF16) for d in range(2)]
    sel_gate = [jnp.where(gc_i == gh_j + 8 + 4 * d, 1.0, 0.0).astype(BF16) for d in range(2)]
    ti = lax.broadcasted_iota(jnp.int32, (tr, tr), 0)
    tj = lax.broadcasted_iota(jnp.int32, (tr, tr), 1)
    same_chunk = (ti >> HEAD_SHIFT) == (tj >> HEAD_SHIFT)
    tri = [jnp.where(jnp.logical_and(same_chunk, ti >= tj), 1.0, 0.0).astype(BF16),
           jnp.where(jnp.logical_and(same_chunk, ti <= tj), 1.0, 0.0).astype(BF16)]
    assert tr == BRANCH_W
    lane_head = lax.broadcasted_iota(jnp.int32, (1, BRANCH_W), 1) >> HEAD_SHIFT

    halo = 16
    edge = 8
    assert half <= edge
    si = lax.broadcasted_iota(jnp.int32, (tr, tr), 0)
    sj = lax.broadcasted_iota(jnp.int32, (tr, tr), 1)
    ei = lax.broadcasted_iota(jnp.int32, (edge, halo), 0)
    ej = lax.broadcasted_iota(jnp.int32, (edge, halo), 1)
    taps = [j for j in range(SHORT_CONV) if j != half]
    shift = {j: jnp.where(sj == si + (j - half), 1.0, 0.0).astype(BF16) for j in taps}
    shift_before = {j: jnp.where(ej == ei + (halo + j - half), 1.0, 0.0).astype(BF16) for j in taps if j < half}
    shift_after = {j: jnp.where(ej == ei + (j - half - edge), 1.0, 0.0).astype(BF16) for j in taps if j > half}
    for t in range(seq // tr):
        rows = slice(t * tr, (t + 1) * tr)
        x = qkv_ref[rows, :]
        y = x.astype(F32) * cw_ref[half:half + 1, :]
        for j in taps:
            y = y + jnp.dot(shift[j], x, preferred_element_type=F32) * cw_ref[j:j + 1, :]
        if t > 0:
            before = qkv_ref[t * tr - halo:t * tr, :]
            top = sum(jnp.dot(shift_before[j], before, preferred_element_type=F32) * cw_ref[j:j + 1, :]
                      for j in shift_before)
            y = jnp.concatenate([y[:edge] + top, y[edge:]], axis=0)
        if (t + 1) * tr < seq:
            after = qkv_ref[(t + 1) * tr:(t + 1) * tr + halo, :]
            bottom = sum(jnp.dot(shift_after[j], after, preferred_element_type=F32) * cw_ref[j:j + 1, :]
                         for j in shift_after)
            y = jnp.concatenate([y[:tr - edge], y[tr - edge:] + bottom], axis=0)
        y = _silu(y)
        qq, kk = y[:, :BRANCH_W], y[:, BRANCH_W:2 * BRANCH_W]
        q_s[rows, :] = qq * lax.rsqrt(_head_reduce(qq * qq, head_sum, terms=1) + EPS) * SCALE
        k_s[rows, :] = kk * lax.rsqrt(_head_reduce(kk * kk, head_sum, terms=1) + EPS)
        v_s[rows, :] = y[:, 2 * BRANCH_W:]
        x = ab_ref[rows, :]
        beta = jax.nn.sigmoid(x)
        xs = x + par_ref[0:1, :]
        softplus = jnp.maximum(xs, 0.0) + jnp.log1p(jnp.exp(-jnp.abs(xs)))
        la = -jnp.exp(par_ref[1:2, :]) * softplus
        for d in range(2):
            gc = _mm_exact(tri[d], la, terms=2)
            gcb_s[d, rows, :] = _mm_exact_lhs(gc, sel_gate[d], terms=2)
            bcb_s[d, rows, :] = _mm_exact_lhs(beta, sel_beta[d], terms=2)
            gt = gc.T[8:16, :]
            shifted = {s: (gt if s == 0 else pltpu.roll(gt, (s * HEAD_DIM) % tr, 1))
                       for s in range(1 - cpt, N_HEADS)}
            for c in range(cpt):
                r = jnp.zeros((1, BRANCH_W), F32)
                for h in range(N_HEADS):
                    r = jnp.where(lane_head == h, shifted[h - c][4 * d + h:4 * d + h + 1, :], r)
                r_s[d, (t * cpt + c) * 8:(t * cpt + c + 1) * 8, :] = jnp.broadcast_to(r, (8, BRANCH_W))

    for d in range(2):
        if has_s0:
            s_s[d] = jnp.concatenate([s0_ref[d, h] for h in range(N_HEADS)], axis=-1)
        else:
            s_s[d] = jnp.zeros((HEAD_DIM, BRANCH_W), F32)

    li = lax.broadcasted_iota(jnp.int32, (CHUNK, BRANCH_W), 0)
    lj = lax.broadcasted_iota(jnp.int32, (CHUNK, BRANCH_W), 1) & (HEAD_DIM - 1)
    incl = (li >= lj, li <= lj)
    strict = (li > lj, li < lj)
    level = [((li ^ lj) >> l) == 1 for l in range(n_levels)]
    tile_heads = pair_w // HEAD_DIM
    head_in_tile = lax.broadcasted_iota(jnp.int32, (CHUNK, pair_w), 1) >> HEAD_SHIFT

    def diag_blocks(full):
        out = full[:HEAD_DIM]
        for a in range(1, tile_heads):
            out = jnp.where(head_in_tile == a, full[a * HEAD_DIM:(a + 1) * HEAD_DIM], out)
        return out

    def expand(y):
        yb = y.astype(BF16)
        zero = jnp.zeros((CHUNK, pair_w), BF16)
        return [jnp.concatenate([jnp.where(head_in_tile == a, yb[:, p], zero) for a in range(tile_heads)], axis=0)
                for p in pairs]

    def bdmm(x, ybd):
        xb = x.astype(BF16)
        return jnp.concatenate([jnp.dot(xb[:, p], ybd[i], preferred_element_type=F32)
                                for i, p in enumerate(pairs)], axis=1)

    def bdmm_nt(x, ybd):
        xb = x.astype(BF16)
        return jnp.concatenate([lax.dot_general(xb[:, p], ybd[i], (((1,), (1,)), ((), ())),
                                                preferred_element_type=F32)
                                for i, p in enumerate(pairs)], axis=1)

    def bdmm2(x, y1, y2):
        xb = x.astype(BF16)
        e1, e2 = expand(y1), expand(y2)
        outs = [jnp.dot(xb[:, p], jnp.concatenate([e1[i], e2[i]], axis=1), preferred_element_type=F32)
                for i, p in enumerate(pairs)]
        return (jnp.concatenate([o[:, :pair_w] for o in outs], axis=1),
                jnp.concatenate([o[:, pair_w:] for o in outs], axis=1))

    def tn_diag2(a, b1, b2):
        ab, b1b, b2b = a.astype(BF16), b1.astype(BF16), b2.astype(BF16)
        outs1, outs2 = [], []
        for p in pairs:
            full = lax.dot_general(ab[:, p], jnp.concatenate([b1b[:, p], b2b[:, p]], axis=1),
                                   (((0,), (0,)), ((), ())), preferred_element_type=F32)
            outs1.append(diag_blocks(full[:, :pair_w]))
            outs2.append(diag_blocks(full[:, pair_w:]))
        return jnp.concatenate(outs1, axis=1), jnp.concatenate(outs2, axis=1)

    def prepare(chains):
        n = range(len(chains))
        dd = [d for d, _ in chains]
        rows = [pl.ds(_aligned(c * CHUNK, CHUNK), CHUNK) for _, c in chains]
        gcb = [gcb_s[dd[i], rows[i], :] for i in n]
        bcb = [bcb_s[dd[i], rows[i], :] for i in n]
        grow = [r_s[dd[i], pl.ds(_aligned(chains[i][1] * 8, 8), 8), :][0:1, :] for i in n]
        dm = [jnp.exp(jnp.where(incl[dd[i]], gcb[i] - grow[i], NEG_INF)) for i in n]
        k = [k_s[rows[i], :] for i in n]
        q = [q_s[rows[i], :] for i in n]
        v = [v_s[rows[i], :] for i in n]
        kq = [bdmm_nt(jnp.concatenate([k[i], q[i]], axis=0), expand(k[i])) for i in n]
        a = [jnp.where(strict[dd[i]], bcb[i] * kq[i][:CHUNK] * dm[i], 0.0) for i in n]
        tm = [-jnp.where(level[0], a[i], 0.0) for i in n]
        for l in range(1, n_levels):
            b = [jnp.where(level[l], a[i], 0.0) for i in n]
            y = [b[i] + bdmm(tm[i], expand(b[i])) for i in n]
            tm = [tm[i] - (y[i] + bdmm(y[i], expand(tm[i]))) for i in n]
        eg = [jnp.exp(gcb[i]) for i in n]
        bv = [bcb[i] * v[i] for i in n]
        bk = [bcb[i] * k[i] * eg[i] for i in n]
        tuw = [bdmm2(tm[i], bv[i], bk[i]) for i in n]
        u = [bv[i] + tuw[i][0] for i in n]
        w = [bk[i] + tuw[i][1] for i in n]
        gl = [gcb[i][CHUNK - 1:CHUNK, :] if dd[i] == 0 else gcb[i][0:1, :] for i in n]
        kd = [k[i] * jnp.exp(gl[i] - gcb[i]) for i in n]
        qkm = [kq[i][CHUNK:] * dm[i] for i in n]
        mnc = [tn_diag2(kd[i], w[i], u[i]) for i in n]
        mc = [mnc[i][0] for i in n]
        nc = [mnc[i][1] for i in n]
        qwu = [bdmm2(qkm[i], w[i], u[i]) for i in n]
        qp = [q[i] * eg[i] - qwu[i][0] for i in n]
        op = [qwu[i][1] for i in n]
        for i in n:
            d, c = chains[i]
            mc_s[d, rows[i], :] = mc[i].astype(BF16)
            nc_s[d, rows[i], :] = nc[i]
            qp_s[d, rows[i], :] = qp[i].astype(BF16)
            op_s[d, rows[i], :] = op[i]
            egl_s[d, pl.ds(_aligned(c * 8, 8), 8), :] = jnp.broadcast_to(jnp.exp(gl[i]), (8, BRANCH_W))

    group = min(PREP_CHUNKS, n_chunks)
    n_groups = n_chunks // group

    def prepare_group(j):
        prepare([(d, (j if d == 0 else n_groups - 1 - j) * group + c) for c in range(group) for d in range(2)])

    def scan_step(i):
        for d, c in ((0, i), (1, n_chunks - 1 - i)):
            rows = pl.ds(_aligned(c * CHUNK, CHUNK), CHUNK)
            s = s_s[d]
            sbd = expand(s)
            oacc_ref[d, rows, :] = bdmm(qp_s[d, rows, :], sbd) + op_s[d, rows, :]
            egl = egl_s[d, pl.ds(_aligned(c * 8, 8), 8), :][0:1, :]
            s_s[d] = s * egl - bdmm(mc_s[d, rows, :], sbd) + nc_s[d, rows, :]

    def scan_group(j):
        for i in range(group):
            scan_step(j * group + i)

    prepare_group(0)
    if n_groups > 1:
        def body(j, carry):
            scan_group(j - 1)
            prepare_group(j)
            return carry
        lax.fori_loop(1, n_groups, body, 0)
    scan_group(n_groups - 1)

    if st_ref is not None:
        _write_state(st_ref, layer, emit == "first", lambda d, h: s_s[d][:, _hs(h)])
    o = oacc_ref[0] + oacc_ref[1]
    ms = _head_reduce(o * o, _head_block_matrix(BRANCH_W, 1.0 / HEAD_DIM))
    o_ref[...] = (o * lax.rsqrt(ms + EPS) * g_ref[...] * _silu(z_ref[...].astype(F32))).astype(BF16)


def _state_spec(layer, n_seq=None):
    return pl.BlockSpec((n_seq, None, 2, N_HEADS, HEAD_DIM, HEAD_DIM), lambda b: (b, layer, 0, 0, 0, 0))


def _state_output(emit, prev, layer, batch, n_args, n_seq=None):
    if emit == "none":
        return [], [], [], [], {}
    shape = jax.ShapeDtypeStruct((batch, DEPTH, 2, N_HEADS, HEAD_DIM, HEAD_DIM), F32)
    if emit == "first":
        spec = pl.BlockSpec((n_seq, DEPTH, 2, N_HEADS, HEAD_DIM, HEAD_DIM), lambda b: (b, 0, 0, 0, 0, 0))
        return [], [], [spec], [shape], {}
    return [pl.BlockSpec(memory_space=pl.ANY)], [prev], [_state_spec(layer, n_seq)], [shape], {n_args: 1}


def _gdn_call(proj, gab, cw8, par, norm_g, layer, batch, seq, s0=None, emit="none", prev=None):
    has_s0 = s0 is not None
    qkv_w = 3 * BRANCH_W
    in_specs = [pl.BlockSpec((seq, qkv_w), lambda b: (b, OFF_GQKV // qkv_w)),
                pl.BlockSpec((seq, BRANCH_W), lambda b: (b, OFF_GZ // BRANCH_W)),
                pl.BlockSpec((seq, 128), lambda b: (b, 0)),
                _layer_spec((8, qkv_w), layer, 1),
                _layer_spec((2, 128), layer, 1),
                _layer_spec((1, BRANCH_W), layer, 1)]
    args = [proj, proj, gab, cw8, par, norm_g]
    if has_s0:
        in_specs.append(_state_spec(layer))
        args.append(s0)
    st_in_specs, st_args, st_out_specs, st_shapes, aliases = _state_output(emit, prev, layer, batch, len(args))
    return pl.pallas_call(
        functools.partial(_gdn_kernel, seq=seq, has_s0=has_s0, layer=layer, emit=emit),
        grid=(batch,), in_specs=in_specs + st_in_specs,
        out_specs=[pl.BlockSpec((seq, BRANCH_W), lambda b: (b, 0))] + st_out_specs,
        out_shape=[jax.ShapeDtypeStruct((batch * seq, BRANCH_W), BF16)] + st_shapes,
        input_output_aliases=aliases,
        scratch_shapes=[pltpu.VMEM((seq, BRANCH_W), F32),
                        pltpu.VMEM((seq, BRANCH_W), F32),
                        pltpu.VMEM((seq, BRANCH_W), F32),
                        pltpu.VMEM((2, seq, BRANCH_W), F32),
                        pltpu.VMEM((2, seq, BRANCH_W), F32),
                        pltpu.VMEM((2, seq // CHUNK * 8, BRANCH_W), F32),
                        pltpu.VMEM((2, seq, BRANCH_W), BF16),
                        pltpu.VMEM((2, seq, BRANCH_W), F32),
                        pltpu.VMEM((2, seq, BRANCH_W), BF16),
                        pltpu.VMEM((2, seq, BRANCH_W), F32),
                        pltpu.VMEM((2, seq // CHUNK * 8, BRANCH_W), F32),
                        pltpu.VMEM((2, HEAD_DIM, BRANCH_W), F32),
                        pltpu.VMEM((2, seq, BRANCH_W), F32)],
        compiler_params=_params(48, 1),
        name="gdn",
    )(*args, *st_args)


_RET_LOG_GAMMA = [[float(np.log1p(-np.exp2(-(base + h)))) for h in range(N_HEADS)] for base in RET_DECAY_BASE]


def _ret_kernel(*refs, seq, n_seq, has_s0, layer, emit):
    qkv_refs, z_ref, g_ref = refs[:3], refs[3], refs[4]
    s0_ref = refs[5] if has_s0 else None
    o_ref, st_ref = (refs[-1], None) if emit == "none" else (refs[-2], refs[-1])
    tile = RET_TILE
    n_tiles = seq // tile
    problems = [(s, h) for s in range(n_seq) for h in range(N_HEADS)]
    heads = range(len(problems))
    lgf = [_RET_LOG_GAMMA[0][h] for _, h in problems]
    lgb = [_RET_LOG_GAMMA[1][h] for _, h in problems]
    a = lax.broadcasted_iota(jnp.int32, (tile, 1), 0).astype(F32)
    ef = [jnp.exp(a * lgf[h]) for h in heads]
    eif = [jnp.exp(-a * lgf[h]) for h in heads]
    eb = [jnp.exp(a * lgb[h]) for h in heads]
    eib = [jnp.exp(-a * lgb[h]) for h in heads]
    gf_tile = [float(np.exp(tile * lgf[h])) for h in heads]
    gb_tile = [float(np.exp(tile * lgb[h])) for h in heads]
    ii = lax.broadcasted_iota(jnp.int32, (tile, tile), 0)
    jj = lax.broadcasted_iota(jnp.int32, (tile, tile), 1)

    def head_cols(t, part, i):
        s, h = problems[i]
        return qkv_refs[part][s * seq + t * tile:s * seq + (t + 1) * tile, _hs(h)]

    def initial_state(d, i):
        s, h = problems[i]
        return s0_ref[s, d, h]

    kf = [[head_cols(t, 1, h) * eif[h] for h in heads] for t in range(n_tiles)]
    kb = [[head_cols(t, 1, h) * eb[h] for h in heads] for t in range(n_tiles)]
    vs = [[head_cols(t, 2, h) for h in heads] for t in range(n_tiles)]
    use_states = has_s0 or n_tiles > 1 or st_ref is not None
    if use_states:
        kvf = [[_mm_tn(kf[t][h], vs[t][h]) for h in heads] for t in range(n_tiles)]
        kvb = [[_mm_tn(kb[t][h], vs[t][h]) for h in heads] for t in range(n_tiles)]
        zero = jnp.zeros((HEAD_DIM, HEAD_DIM), F32)
        zf = [[(float(np.exp(lgf[h])) * initial_state(0, h)) if has_s0 else zero for h in heads]]
        for t in range(n_tiles):
            zf.append([gf_tile[h] * (zf[t][h] + kvf[t][h]) for h in heads])
        acc = [initial_state(1, h) if has_s0 else zero for h in heads]
        zb = [None] * n_tiles
        for t in reversed(range(n_tiles)):
            zb[t] = [gb_tile[h] * acc[h] for h in heads]
            acc = [zb[t][h] + kvb[t][h] for h in heads]
        if st_ref is not None:
            stf = [zf[n_tiles][h] * float(np.exp(-lgf[h])) for h in heads]
            for s in range(n_seq):
                _write_state(st_ref.at[s], layer, emit == "first",
                             lambda d, h, s=s: (stf, acc)[d][s * N_HEADS + h])
    tiles = [[] for _ in range(n_seq)]
    for t in range(n_tiles):
        q = [head_cols(t, 0, h) * SCALE for h in heads]
        qf = [q[h] * ef[h] for h in heads]
        qb = [q[h] * eib[h] for h in heads]
        sd = [jnp.where(ii >= jj, _mm_nt(qf[h], kf[t][h]), 0.0) + jnp.where(ii <= jj, _mm_nt(qb[h], kb[t][h]), 0.0)
              for h in heads]
        o = [_mm(sd[h], vs[t][h]) for h in heads]
        if has_s0 or n_tiles > 1:
            o = [o[h] + _mm(qf[h], zf[t][h]) + _mm(qb[h], zb[t][h]) for h in heads]
        ms = [jnp.mean(o[h] * o[h], axis=-1, keepdims=True) for h in heads]
        for s in range(n_seq):
            tiles[s].append(jnp.concatenate([o[h] * lax.rsqrt(ms[h] + EPS) * g_ref[...]
                                             for h in range(s * N_HEADS, (s + 1) * N_HEADS)], axis=-1))
    o = jnp.concatenate([tile_out for s in range(n_seq) for tile_out in tiles[s]], axis=0)
    o_ref[...] = (o * _silu(z_ref[...].astype(F32))).astype(BF16)


def _ret_call(proj, norm_g, layer, batch, seq, s0=None, emit="none", prev=None):
    has_s0 = s0 is not None
    n_seq = max(1, RET_ROWS_PER_STEP // seq)
    assert batch % n_seq == 0
    tm = n_seq * seq
    cq = OFF_C // BRANCH_W
    in_specs = [pl.BlockSpec((tm, BRANCH_W), lambda b, part=part: (b, cq + part)) for part in range(4)]
    in_specs.append(_layer_spec((1, HEAD_DIM), layer, 1))
    args = [proj, proj, proj, proj, norm_g]
    if has_s0:
        in_specs.append(_state_spec(layer, n_seq))
        args.append(s0)
    st_in_specs, st_args, st_out_specs, st_shapes, aliases = _state_output(emit, prev, layer, batch, len(args),
                                                                           n_seq)
    return pl.pallas_call(
        functools.partial(_ret_kernel, seq=seq, n_seq=n_seq, has_s0=has_s0, layer=layer, emit=emit),
        grid=(batch // n_seq,), in_specs=in_specs + st_in_specs,
        out_specs=[pl.BlockSpec((tm, BRANCH_W), lambda b: (b, 0))] + st_out_specs,
        out_shape=[jax.ShapeDtypeStruct((batch * seq, BRANCH_W), BF16)] + st_shapes,
        input_output_aliases=aliases,
        compiler_params=_params(48, 1),
        name="retention",
    )(*args, *st_args)


def _out_kernel(*refs, final):
    if final:
        (h_ref, mod_ref, g_ref, oa_ref, ob_ref, oc_ref, od_ref, wg_ref, wb_ref, wo_ref, fn_ref,
         o_ref, y_ref) = refs
    else:
        h_ref, mod_ref, g_ref, oa_ref, ob_ref, oc_ref, od_ref, wg_ref, wb_ref, wo_ref, o_ref = refs
    x = h_ref[...]
    mod = mod_ref[0]
    hn = _modulated_norm(x, mod, g_ref[...]).astype(BF16)
    merged = None
    for n, br_ref in enumerate((oa_ref, ob_ref, oc_ref, od_ref)):
        gate = jax.nn.sigmoid(_mm_nt(hn, wg_ref[n * D_MODEL:(n + 1) * D_MODEL, :]))
        up = jnp.dot(br_ref[...], wb_ref[n], preferred_element_type=F32)
        merged = gate * up if merged is None else merged + gate * up
    out = jnp.dot(merged.astype(BF16), wo_ref[...], preferred_element_type=F32)
    hnew = x + mod[:, 2 * D_MODEL:] * out
    o_ref[...] = hnew
    if final:
        ms = jnp.mean(hnew * hnew, axis=-1, keepdims=True)
        y_ref[...] = hnew * lax.rsqrt(ms + EPS) * fn_ref[...]


def _out_call(h2d, mod3, norm_g3, branches, wg, wb, wo, layer, rows_per_mod, final_norm=None):
    t = h2d.shape[0]
    tm = 512
    final = final_norm is not None
    if mod3.shape[0] == 1:
        mod_idx = lambda i: (0, 0, 0)
    else:
        mod_idx = lambda i: ((i * tm) // rows_per_mod, 0, 0)
    once = pl.Buffered(1)
    in_specs = [pl.BlockSpec((tm, D_MODEL), lambda i: (i, 0)),
                pl.BlockSpec((1, 1, 3 * D_MODEL), mod_idx),
                _layer_spec((1, D_MODEL), layer, 1)]
    in_specs += [pl.BlockSpec((tm, BRANCH_W), lambda i: (i, 0))] * N_BRANCH
    in_specs += [pl.BlockSpec((None, N_BRANCH * D_MODEL, D_MODEL), lambda i: (layer, 0, 0), pipeline_mode=once),
                 pl.BlockSpec((None, N_BRANCH, BRANCH_W, D_MODEL), lambda i: (layer, 0, 0, 0), pipeline_mode=once),
                 pl.BlockSpec((None, D_MODEL, D_MODEL), lambda i: (layer, 0, 0), pipeline_mode=once)]
    args = [h2d, mod3, norm_g3, *branches, wg, wb, wo]
    out_specs = [pl.BlockSpec((tm, D_MODEL), lambda i: (i, 0))]
    out_shape = [jax.ShapeDtypeStruct((t, D_MODEL), F32)]
    if final:
        in_specs.append(pl.BlockSpec((1, D_MODEL), lambda i: (0, 0)))
        args.append(final_norm.reshape(1, D_MODEL))
        out_specs.append(pl.BlockSpec((tm, D_MODEL), lambda i: (i, 0)))
        out_shape.append(jax.ShapeDtypeStruct((t, D_MODEL), F32))
    return pl.pallas_call(
        functools.partial(_out_kernel, final=final),
        grid=(t // tm,), in_specs=in_specs, out_specs=out_specs, out_shape=out_shape,
        compiler_params=_params(48, 1),
        name="merge_out_final" if final else "merge_out",
    )(*args)


def _prep_weights(w_in):
    offs = np.concatenate([[0], np.cumsum(IN_SPLITS)])
    seg = lambda i, j: w_in[:, :, offs[i]:offs[j]].astype(BF16).transpose(0, 2, 1)
    assert offs[5] == OFF_GZ and offs[15] - offs[6] == OFF_GAB - OFF_GZ
    pad = jnp.zeros((DEPTH, PROJ_W - OFF_GAB - IN_SPLITS[5], D_MODEL), BF16)
    w_proj = (seg(0, 5),
              seg(6, 15),
              jnp.concatenate([seg(5, 6), pad], axis=1))
    return w_proj, seg(15, 16)


def _rope_tables(seq):
    t = jnp.arange(seq)
    quarter = HEAD_DIM // 4
    inv = ROPE_THETA ** (-jnp.arange(quarter, dtype=F32) / quarter)

    def half(pos):
        ang = pos.astype(F32)[:, None] * inv
        c, s, zero = jnp.cos(ang), jnp.sin(ang), jnp.zeros_like(ang)
        return jnp.concatenate([c, c], -1), jnp.concatenate([-s, zero], -1), jnp.concatenate([zero, s], -1)

    parts = [jnp.concatenate([a, b], -1) for a, b in zip(half(t // GRID_W), half(t % GRID_W))]
    tab = jnp.stack(parts)
    return jnp.tile(tab, (1, 1, N_HEADS)), jnp.tile(tab, (1, 1, KV_HEADS))


def _layer(h2d, batch, seq, mod, pw, layer, ctx, caches, final_norm):
    proj, gab = _inproj_call(h2d, mod, pw["norm_g"], pw["w_proj"], layer, seq)
    kw = KV_HEADS * HEAD_DIM
    if ctx is None:
        emit = "first" if caches is None else "update"
        akv, nkv, sg_all, sr_all = caches or (None, None, None, None)
        oa, *akv = _attn_ctx_call(proj, akv, layer, batch, seq, OFF_AQKV, OFF_AQKV + BRANCH_W,
                                  OFF_AQKV + BRANCH_W + kw, OFF_AZ, KV_HEADS, pw["qn"], pw["kn"])
        od, *nkv = _attn_ctx_call(proj, nkv, layer, batch, seq, OFF_D, OFF_D + BRANCH_W,
                                  OFF_D + 2 * BRANCH_W, OFF_D + 3 * BRANCH_W, N_HEADS)
        ob, sg_all = _gdn_call(proj, gab, pw["cw8"], pw["gdn_par"], pw["gdn_norm"], layer, batch, seq,
                               emit=emit, prev=sg_all)
        oc, sr_all = _ret_call(proj, pw["ret_norm"], layer, batch, seq, emit=emit, prev=sr_all)
        caches = (akv, nkv, sg_all, sr_all)
    else:
        oa = _attn_lat_call(proj, ctx["akt"], ctx["avt"], layer, batch, seq, ctx["qtab"], ctx["ktab"],
                            pw["qn"], pw["kn"])
        od = _na_call(proj, ctx["nkt"], ctx["nvt"], ctx["tb"], layer, batch, seq)
        ob, = _gdn_call(proj, gab, pw["cw8"], pw["gdn_par"], pw["gdn_norm"], layer, batch, seq, s0=ctx["sg"])
        oc, = _ret_call(proj, pw["ret_norm"], layer, batch, seq, s0=ctx["sr"])
    outs = _out_call(h2d, mod, pw["norm_g"], (oa, ob, oc, od), pw["wg"], pw["wb"], pw["wo"], layer, seq, final_norm)
    return outs, caches


def kernel(x_prompt, x_sample, cache_attn_k, cache_attn_v, cache_na_k, cache_na_v, state_gdn, state_ret, c, c_ctx, w_ada, b_ada, norm_g, w_in, conv_w, gdn_a_log, gdn_dt_bias, gdn_norm, attn_q_norm, attn_k_norm, ret_norm, na_bias, w_branch, w_out, final_norm):
    batch, seq, _ = x_prompt.shape
    dbatch, dseq, _ = x_sample.shape
    assert dbatch == 8, "the modulation kernel handles exactly one sublane tile of conditioning rows"

    w_proj, wg = _prep_weights(w_in)
    par = jnp.zeros((DEPTH, 2, 128), F32)
    par = par.at[:, 0, 8:16].set(gdn_dt_bias.reshape(DEPTH, 8)).at[:, 1, 8:16].set(gdn_a_log.reshape(DEPTH, 8))
    pw = dict(
        w_ada=w_ada, b_ada=b_ada.reshape(DEPTH, 1, 3 * D_MODEL), norm_g=norm_g.reshape(DEPTH, 1, D_MODEL),
        w_proj=w_proj, wg=wg, wb=w_branch.astype(BF16), wo=w_out.astype(BF16),
        cw8=jnp.concatenate([conv_w, jnp.zeros((DEPTH, 8 - SHORT_CONV, 3 * BRANCH_W), F32)], axis=1),
        gdn_par=par,
        gdn_norm=jnp.tile(gdn_norm, (1, N_HEADS)).reshape(DEPTH, 1, BRANCH_W),
        ret_norm=ret_norm.reshape(DEPTH, 1, HEAD_DIM),
        qn=jnp.tile(attn_q_norm, (1, N_HEADS)).reshape(DEPTH, 1, BRANCH_W),
        kn=jnp.tile(attn_k_norm, (1, KV_HEADS)).reshape(DEPTH, 1, KV_HEADS * HEAD_DIM))

    cond = jnp.concatenate([jnp.broadcast_to(c_ctx, (8, D_MODEL)), c], axis=0)
    mods = _mod_call(cond, w_ada, pw["b_ada"])

    h = x_prompt.reshape(batch * seq, D_MODEL)
    caches = None
    for l in range(DEPTH):
        outs, caches = _layer(h, batch, seq, mods[l, 0:1].reshape(1, 1, 3 * D_MODEL), pw, l, None, caches,
                              final_norm if l == DEPTH - 1 else None)
        h = outs[0]
    y_prompt = outs[1].reshape(batch, seq, D_MODEL)
    token_major = lambda a: a.transpose(0, 1, 4, 2, 3)
    (akt, avt), (nkt, nvt), new_state_gdn, new_state_ret = caches
    new_attn_k, new_attn_v, new_na_k, new_na_v = (token_major(a) for a in (akt, avt, nkt, nvt))

    qtab, ktab = _rope_tables(dseq)
    feature_major = lambda a: a.transpose(0, 1, 3, 4, 2)
    ctx = dict(akt=feature_major(cache_attn_k), avt=feature_major(cache_attn_v),
               nkt=feature_major(cache_na_k), nvt=feature_major(cache_na_v),
               sg=state_gdn, sr=state_ret, tb=_na_bias_call(na_bias), qtab=qtab, ktab=ktab)
    h = x_sample.reshape(dbatch * dseq, D_MODEL)
    for l in range(DEPTH):
        outs, _ = _layer(h, dbatch, dseq, mods[l, 8:16].reshape(dbatch, 1, 3 * D_MODEL), pw, l, ctx, None,
                         final_norm if l == DEPTH - 1 else None)
        h = outs[0]
    y_sample = outs[1].reshape(dbatch, dseq, D_MODEL)
    return (y_prompt, y_sample, new_attn_k, new_attn_v, new_na_k, new_na_v, new_state_gdn, new_state_ret)
```
